```python
import math
import jax, jax.numpy as jnp
from jax import lax
import numpy as np

D_MODEL = 1024
BATCH = 8
SEQ = 8192
DEPTH = 2

N_A_LAYERS = DEPTH // 2
N_B_LAYERS = DEPTH - N_A_LAYERS
D_FF = 2816
FFN_HALF = 0.5
SSM_EXPAND = 2
SSM_D_INNER = SSM_EXPAND * D_MODEL
SSM_HEAD_DIM = 64
SSM_HEADS = SSM_D_INNER // SSM_HEAD_DIM
SSM_GROUPS = 4
SSM_STATE = 128
SSM_CONV = 4
SSM_CHUNK = 256
SSM_CONV_DIM = SSM_D_INNER + 2 * SSM_GROUPS * SSM_STATE
SSM_IN_DIM = SSM_D_INNER + SSM_CONV_DIM + SSM_HEADS
ATT_HEAD_DIM = 64
ATT_HEADS = D_MODEL // ATT_HEAD_DIM
ATT_KV_HEADS = ATT_HEADS // 8
ATT_GROUP = ATT_HEADS // ATT_KV_HEADS
ATT_WINDOW = 128
ATT_BLOCK = ATT_WINDOW
REL_BUCKETS = 32
REL_MAX_DIST = ATT_WINDOW
EPS = 1e-6

kernel_name = 'yoco_mamba2_swa_sink_macaron'


def rmsnorm(x, g):
    xf = x.astype(jnp.float32)
    y = xf * lax.rsqrt(jnp.mean(xf * xf, axis=-1, keepdims=True) + EPS)
    return (y * g.astype(jnp.float32)).astype(x.dtype)


def swiglu_half_step(h, g, w1, w3, w2):
    u = rmsnorm(h, g)
    return h + FFN_HALF * ((jax.nn.silu(u @ w1) * (u @ w3)) @ w2)


def causal_depthwise_conv(x, w, b):
    c = x.shape[-1]
    y = lax.conv_general_dilated(
        x, w.astype(x.dtype)[:, None, :], window_strides=(1,),
        padding=[(SSM_CONV - 1, 0)], dimension_numbers=('NWC', 'WIO', 'NWC'),
        feature_group_count=c)
    return y + b.astype(x.dtype)


def ssd_chunked_scan(x, dt, a, b_in, c_in):
    bsz, t, h, p = x.shape
    g, n = b_in.shape[2], b_in.shape[3]
    r = h // g
    L = SSM_CHUNK
    nc = -(-t // L)
    pad = nc * L - t
    f32 = jnp.float32

    def chunks(z):
        z = z.astype(f32)
        z = jnp.pad(z, [(0, 0), (0, pad)] + [(0, 0)] * (z.ndim - 2))
        z = z.reshape((bsz, nc, L) + z.shape[2:])
        return jnp.moveaxis(z, 1, 0)

    xc = chunks(x.reshape(bsz, t, g, r, p))
    dtc = chunks(dt.reshape(bsz, t, g, r))
    bc = chunks(b_in)
    cc = chunks(c_in)
    a_gr = a.astype(f32).reshape(g, r)
    causal = jnp.tril(jnp.ones((L, L), bool))[:, :, None, None]

    def step(state, inp):
        xk, dtk, bk, ck = inp
        acum = jnp.cumsum(dtk * a_gr, axis=1)
        seg = acum[:, :, None] - acum[:, None, :]
        decay = jnp.exp(jnp.where(causal, seg, -jnp.inf))
        cb = jnp.einsum('blgn,bsgn->blsg', ck, bk)
        scores = cb[..., None] * decay
        y_diag = jnp.einsum('blsgr,bsgr,bsgrp->blgrp', scores, dtk, xk)
        y_off = jnp.einsum('blgn,bgrpn,blgr->blgrp', ck, state, jnp.exp(acum))
        w_end = jnp.exp(acum[:, -1:] - acum) * dtk
        state = (state * jnp.exp(acum[:, -1])[..., None, None]
                 + jnp.einsum('bsgn,bsgr,bsgrp->bgrpn', bk, w_end, xk))
        return state, y_diag + y_off

    state0 = jnp.zeros((bsz, g, r, p, n), f32)
    _, y = lax.scan(step, state0, (xc, dtc, bc, cc))
    y = jnp.moveaxis(y, 0, 1).reshape(bsz, nc * L, h, p)[:, :t]
    return y.astype(x.dtype)


def mamba2_mixer(u, w_in, conv_w, conv_b, dt_bias, a_log, d_skip, gate_norm, w_out):
    bsz, t, _ = u.shape
    zxbcdt = u @ w_in
    z, xbc, dt = jnp.split(zxbcdt, [SSM_D_INNER, SSM_D_INNER + SSM_CONV_DIM], axis=-1)
    xbc = jax.nn.silu(causal_depthwise_conv(xbc, conv_w, conv_b))
    xs, b_in, c_in = jnp.split(xbc, [SSM_D_INNER, SSM_D_INNER + SSM_GROUPS * SSM_STATE], axis=-1)
    xs = xs.reshape(bsz, t, SSM_HEADS, SSM_HEAD_DIM)
    b_in = b_in.reshape(bsz, t, SSM_GROUPS, SSM_STATE)
    c_in = c_in.reshape(bsz, t, SSM_GROUPS, SSM_STATE)
    dt = jax.nn.softplus((dt + dt_bias).astype(jnp.float32))
    a = -jnp.exp(a_log.astype(jnp.float32))
    y = ssd_chunked_scan(xs, dt, a, b_in, c_in) + d_skip[:, None].astype(xs.dtype) * xs
    y = y.reshape(bsz, t, SSM_D_INNER) * jax.nn.silu(z)
    y = rmsnorm(y.reshape(bsz, t, SSM_GROUPS, SSM_D_INNER // SSM_GROUPS),
                gate_norm.reshape(SSM_GROUPS, SSM_D_INNER // SSM_GROUPS))
    return y.reshape(bsz, t, SSM_D_INNER) @ w_out


def shared_kv(h, kv_norm, w_kv, k_norm):
    bsz, t, _ = h.shape
    kv = rmsnorm(h, kv_norm) @ w_kv
    k, v = jnp.split(kv, 2, axis=-1)
    k = rmsnorm(k.reshape(bsz, t, ATT_KV_HEADS, ATT_HEAD_DIM), k_norm)
    v = v.reshape(bsz, t, ATT_KV_HEADS, ATT_HEAD_DIM)
    return k, v


def t5_bucket(dist):
    n = jnp.maximum(dist, 0)
    max_exact = REL_BUCKETS // 2
    nf = jnp.maximum(n, 1).astype(jnp.float32)
    large = max_exact + (jnp.log(nf / max_exact) / math.log(REL_MAX_DIST / max_exact)
                         * (REL_BUCKETS - max_exact)).astype(jnp.int32)
    large = jnp.minimum(large, REL_BUCKETS - 1)
    return jnp.where(n < max_exact, n, large)


def sliding_window_attention(u, k, v, w_q, q_norm, sinks, rel_bias, w_o):
    bsz, t, _ = u.shape
    blk = ATT_BLOCK
    nb = t // blk
    q = rmsnorm((u @ w_q).reshape(bsz, t, ATT_KV_HEADS, ATT_GROUP, ATT_HEAD_DIM), q_norm)
    qb = jnp.moveaxis(q.reshape(bsz, nb, blk, ATT_KV_HEADS, ATT_GROUP, ATT_HEAD_DIM), 1, 0)

    def band(z):
        prev = jnp.pad(z, [(0, 0), (blk, 0), (0, 0), (0, 0)])[:, :t]
        zz = jnp.concatenate([prev.reshape(bsz, nb, blk, ATT_KV_HEADS, ATT_HEAD_DIM),
                              z.reshape(bsz, nb, blk, ATT_KV_HEADS, ATT_HEAD_DIM)], axis=2)
        return jnp.moveaxis(zz, 1, 0)

    kb, vb = band(k), band(v)
    qi = jnp.arange(blk)[:, None] + blk
    kj = jnp.arange(2 * blk)[None, :]
    dist = qi - kj
    in_window = (dist >= 0) & (dist < ATT_WINDOW)
    bias = rel_bias[t5_bucket(dist)]
    bias = jnp.transpose(bias.reshape(blk, 2 * blk, ATT_KV_HEADS, ATT_GROUP),
                         (2, 3, 0, 1)).astype(jnp.float32)
    sink = sinks.reshape(ATT_KV_HEADS, ATT_GROUP)[None, :, :, None, None].astype(jnp.float32)
    scale = ATT_HEAD_DIM ** -0.5

    def block(args):
        qk, kk, vk, bi = args
        s = jnp.einsum('bqkrd,bskd->bkrqs', qk, kk).astype(jnp.float32) * scale + bias
        valid = in_window & ((bi > 0) | (kj >= blk))
        s = jnp.where(valid, s, -jnp.inf)
        m = jnp.maximum(jnp.max(s, axis=-1, keepdims=True), sink)
        p = jnp.exp(s - m)
        denom = jnp.sum(p, axis=-1, keepdims=True) + jnp.exp(sink - m)
        return jnp.einsum('bkrqs,bskd->bqkrd', (p / denom).astype(vk.dtype), vk)

    o = lax.map(block, (qb, kb, vb, jnp.arange(nb)))
    o = jnp.moveaxis(o, 0, 1).reshape(bsz, t, ATT_HEADS * ATT_HEAD_DIM)
    return o @ w_o


def _fwd_setup_inputs(seed: int = 0) -> dict:
    key = jax.random.key(seed)
    ks = jax.random.split(key, 24)
    nrm = jax.random.normal
    f32 = jnp.float32
    x = nrm(ks[0], (BATCH, SEQ, D_MODEL), f32)
    ffn_norm = 1.0 + 0.05 * nrm(ks[1], (DEPTH, 2, D_MODEL), f32)
    ffn_w1 = nrm(ks[2], (DEPTH, 2, D_MODEL, D_FF), f32) * D_MODEL ** -0.5
    ffn_w3 = nrm(ks[3], (DEPTH, 2, D_MODEL, D_FF), f32) * D_MODEL ** -0.5
    ffn_w2 = nrm(ks[4], (DEPTH, 2, D_FF, D_MODEL), f32) * D_FF ** -0.5
    ssm_norm = 1.0 + 0.05 * nrm(ks[5], (N_A_LAYERS, D_MODEL), f32)
    ssm_w_in = nrm(ks[6], (N_A_LAYERS, D_MODEL, SSM_IN_DIM), f32) * D_MODEL ** -0.5
    ssm_conv_w = nrm(ks[7], (N_A_LAYERS, SSM_CONV, SSM_CONV_DIM), f32) * SSM_CONV ** -0.5
    ssm_conv_b = 0.01 * nrm(ks[8], (N_A_LAYERS, SSM_CONV_DIM), f32)
    dt0 = jnp.exp(jax.random.uniform(ks[9], (N_A_LAYERS, SSM_HEADS), f32,
                                     math.log(1e-3), math.log(1e-1)))
    ssm_dt_bias = dt0 + jnp.log(-jnp.expm1(-dt0))
    ssm_a_log = jnp.log(jax.random.uniform(ks[10], (N_A_LAYERS, SSM_HEADS), f32, 1.0, 16.0))
    ssm_d = 1.0 + 0.1 * nrm(ks[11], (N_A_LAYERS, SSM_HEADS), f32)
    ssm_gate_norm = 1.0 + 0.05 * nrm(ks[12], (N_A_LAYERS, SSM_D_INNER), f32)
    ssm_w_out = nrm(ks[13], (N_A_LAYERS, SSM_D_INNER, D_MODEL), f32) * SSM_D_INNER ** -0.5
    kv_norm = 1.0 + 0.05 * nrm(ks[14], (D_MODEL,), f32)
    w_kv = nrm(ks[15], (D_MODEL, 2 * ATT_KV_HEADS * ATT_HEAD_DIM), f32) * D_MODEL ** -0.5
    k_norm = 1.0 + 0.05 * nrm(ks[16], (ATT_HEAD_DIM,), f32)
    attn_norm = 1.0 + 0.05 * nrm(ks[17], (N_B_LAYERS, D_MODEL), f32)
    w_q = nrm(ks[18], (N_B_LAYERS, D_MODEL, ATT_HEADS * ATT_HEAD_DIM), f32) * D_MODEL ** -0.5
    q_norm = 1.0 + 0.05 * nrm(ks[19], (N_B_LAYERS, ATT_HEAD_DIM), f32)
    sinks = 0.5 * nrm(ks[20], (N_B_LAYERS, ATT_HEADS), f32)
    w_o = nrm(ks[21], (N_B_LAYERS, ATT_HEADS * ATT_HEAD_DIM, D_MODEL), f32) * (ATT_HEADS * ATT_HEAD_DIM) ** -0.5
    rel_bias = 0.5 * nrm(ks[22], (REL_BUCKETS, ATT_HEADS), f32)
    return {'x': x, 'ffn_norm': ffn_norm, 'ffn_w1': ffn_w1, 'ffn_w3': ffn_w3, 'ffn_w2': ffn_w2,
            'ssm_norm': ssm_norm, 'ssm_w_in': ssm_w_in, 'ssm_conv_w': ssm_conv_w,
            'ssm_conv_b': ssm_conv_b, 'ssm_dt_bias': ssm_dt_bias, 'ssm_a_log': ssm_a_log,
            'ssm_d': ssm_d, 'ssm_gate_norm': ssm_gate_norm, 'ssm_w_out': ssm_w_out,
            'kv_norm': kv_norm, 'w_kv': w_kv, 'k_norm': k_norm,
            'attn_norm': attn_norm, 'w_q': w_q, 'q_norm': q_norm, 'sinks': sinks, 'w_o': w_o,
            'rel_bias': rel_bias}


def _fwd_reference(x, ffn_norm, ffn_w1, ffn_w3, ffn_w2,
              ssm_norm, ssm_w_in, ssm_conv_w, ssm_conv_b, ssm_dt_bias, ssm_a_log,
              ssm_d, ssm_gate_norm, ssm_w_out,
              kv_norm, w_kv, k_norm,
              attn_norm, w_q, q_norm, sinks, w_o,
              rel_bias):
    h = x
    k_shared, v_shared = None, None
    for layer in range(DEPTH):
        h = swiglu_half_step(h, ffn_norm[layer, 0], ffn_w1[layer, 0], ffn_w3[layer, 0], ffn_w2[layer, 0])
        if layer < N_A_LAYERS:
            i = layer
            h = h + mamba2_mixer(rmsnorm(h, ssm_norm[i]), ssm_w_in[i], ssm_conv_w[i], ssm_conv_b[i],
                                 ssm_dt_bias[i], ssm_a_log[i], ssm_d[i], ssm_gate_norm[i], ssm_w_out[i])
        else:
            j = layer - N_A_LAYERS
            h = h + sliding_window_attention(rmsnorm(h, attn_norm[j]), k_shared, v_shared,
                                             w_q[j], q_norm[j], sinks[j], rel_bias, w_o[j])
        h = swiglu_half_step(h, ffn_norm[layer, 1], ffn_w1[layer, 1], ffn_w3[layer, 1], ffn_w2[layer, 1])
        if layer == N_A_LAYERS - 1:
            k_shared, v_shared = shared_kv(h, kv_norm, w_kv, k_norm)
    return h


import jax as _jax
import jax.numpy as _jnp

TWIN_FORMAT = 'train_step'
FWD_PARAMS = ['x', 'ffn_norm', 'ffn_w1', 'ffn_w3', 'ffn_w2', 'ssm_norm', 'ssm_w_in', 'ssm_conv_w', 'ssm_conv_b', 'ssm_dt_bias', 'ssm_a_log', 'ssm_d', 'ssm_gate_norm', 'ssm_w_out', 'kv_norm', 'w_kv', 'k_norm', 'attn_norm', 'w_q', 'q_norm', 'sinks', 'w_o', 'rel_bias']
TWIN_WEIGHTS = ['ffn_norm', 'ffn_w1', 'ffn_w3', 'ffn_w2', 'ssm_norm', 'ssm_w_in', 'ssm_conv_w', 'ssm_conv_b', 'ssm_dt_bias', 'ssm_a_log', 'ssm_d', 'ssm_gate_norm', 'ssm_w_out', 'kv_norm', 'w_kv', 'k_norm', 'attn_norm', 'w_q', 'q_norm', 'sinks', 'w_o', 'rel_bias']
TWIN_DIFF_INPUT = 'x'
TWIN_INPUTS = ['x', 'ffn_norm', 'ffn_w1', 'ffn_w3', 'ffn_w2', 'ssm_norm', 'ssm_w_in', 'ssm_conv_w', 'ssm_conv_b', 'ssm_dt_bias', 'ssm_a_log', 'ssm_d', 'ssm_gate_norm', 'ssm_w_out', 'kv_norm', 'w_kv', 'k_norm', 'attn_norm', 'w_q', 'q_norm', 'sinks', 'w_o', 'rel_bias', 'loss_target', 'm_ffn_norm', 'm_ffn_w1', 'm_ffn_w3', 'm_ffn_w2', 'm_ssm_norm', 'm_ssm_w_in', 'm_ssm_conv_w', 'm_ssm_conv_b', 'm_ssm_dt_bias', 'm_ssm_a_log', 'm_ssm_d', 'm_ssm_gate_norm', 'm_ssm_w_out', 'm_kv_norm', 'm_w_kv', 'm_k_norm', 'm_attn_norm', 'm_w_q', 'm_q_norm', 'm_sinks', 'm_w_o', 'm_rel_bias', 'v_ffn_norm', 'v_ffn_w1', 'v_ffn_w3', 'v_ffn_w2', 'v_ssm_norm', 'v_ssm_w_in', 'v_ssm_conv_w', 'v_ssm_conv_b', 'v_ssm_dt_bias', 'v_ssm_a_log', 'v_ssm_d', 'v_ssm_gate_norm', 'v_ssm_w_out', 'v_kv_norm', 'v_w_kv', 'v_k_norm', 'v_attn_norm', 'v_w_q', 'v_q_norm', 'v_sinks', 'v_w_o', 'v_rel_bias']
TWIN_OUTPUTS = ['loss', 'grad_x', 'grad_ffn_norm', 'grad_ffn_w1', 'grad_ffn_w3', 'grad_ffn_w2', 'grad_ssm_norm', 'grad_ssm_w_in', 'grad_ssm_conv_w', 'grad_ssm_conv_b', 'grad_ssm_dt_bias', 'grad_ssm_a_log', 'grad_ssm_d', 'grad_ssm_gate_norm', 'grad_ssm_w_out', 'grad_kv_norm', 'grad_w_kv', 'grad_k_norm', 'grad_attn_norm', 'grad_w_q', 'grad_q_norm', 'grad_sinks', 'grad_w_o', 'grad_rel_bias', 'delta_ffn_norm', 'delta_ffn_w1', 'delta_ffn_w3', 'delta_ffn_w2', 'delta_ssm_norm', 'delta_ssm_w_in', 'delta_ssm_conv_w', 'delta_ssm_conv_b', 'delta_ssm_dt_bias', 'delta_ssm_a_log', 'delta_ssm_d', 'delta_ssm_gate_norm', 'delta_ssm_w_out', 'delta_kv_norm', 'delta_w_kv', 'delta_k_norm', 'delta_attn_norm', 'delta_w_q', 'delta_q_norm', 'delta_sinks', 'delta_w_o', 'delta_rel_bias', 'new_m_ffn_norm', 'new_m_ffn_w1', 'new_m_ffn_w3', 'new_m_ffn_w2', 'new_m_ssm_norm', 'new_m_ssm_w_in', 'new_m_ssm_conv_w', 'new_m_ssm_conv_b', 'new_m_ssm_dt_bias', 'new_m_ssm_a_log', 'new_m_ssm_d', 'new_m_ssm_gate_norm', 'new_m_ssm_w_out', 'new_m_kv_norm', 'new_m_w_kv', 'new_m_k_norm', 'new_m_attn_norm', 'new_m_w_q', 'new_m_q_norm', 'new_m_sinks', 'new_m_w_o', 'new_m_rel_bias', 'new_v_ffn_norm', 'new_v_ffn_w1', 'new_v_ffn_w3', 'new_v_ffn_w2', 'new_v_ssm_norm', 'new_v_ssm_w_in', 'new_v_ssm_conv_w', 'new_v_ssm_conv_b', 'new_v_ssm_dt_bias', 'new_v_ssm_a_log', 'new_v_ssm_d', 'new_v_ssm_gate_norm', 'new_v_ssm_w_out', 'new_v_kv_norm', 'new_v_w_kv', 'new_v_k_norm', 'new_v_attn_norm', 'new_v_w_q', 'new_v_q_norm', 'new_v_sinks', 'new_v_w_o', 'new_v_rel_bias']
TWIN_LEAF_KINDS = {'loss': 'loss', 'grad_x': 'grad_x', 'grad_ffn_norm': 'grad_w', 'grad_ffn_w1': 'grad_w', 'grad_ffn_w3': 'grad_w', 'grad_ffn_w2': 'grad_w', 'grad_ssm_norm': 'grad_w', 'grad_ssm_w_in': 'grad_w', 'grad_ssm_conv_w': 'grad_w', 'grad_ssm_conv_b': 'grad_w', 'grad_ssm_dt_bias': 'grad_w', 'grad_ssm_a_log': 'grad_w', 'grad_ssm_d': 'grad_w', 'grad_ssm_gate_norm': 'grad_w', 'grad_ssm_w_out': 'grad_w', 'grad_kv_norm': 'grad_w', 'grad_w_kv': 'grad_w', 'grad_k_norm': 'grad_w', 'grad_attn_norm': 'grad_w', 'grad_w_q': 'grad_w', 'grad_q_norm': 'grad_w', 'grad_sinks': 'grad_w', 'grad_w_o': 'grad_w', 'grad_rel_bias': 'grad_w', 'delta_ffn_norm': 'delta_w', 'delta_ffn_w1': 'delta_w', 'delta_ffn_w3': 'delta_w', 'delta_ffn_w2': 'delta_w', 'delta_ssm_norm': 'delta_w', 'delta_ssm_w_in': 'delta_w', 'delta_ssm_conv_w': 'delta_w', 'delta_ssm_conv_b': 'delta_w', 'delta_ssm_dt_bias': 'delta_w', 'delta_ssm_a_log': 'delta_w', 'delta_ssm_d': 'delta_w', 'delta_ssm_gate_norm': 'delta_w', 'delta_ssm_w_out': 'delta_w', 'delta_kv_norm': 'delta_w', 'delta_w_kv': 'delta_w', 'delta_k_norm': 'delta_w', 'delta_attn_norm': 'delta_w', 'delta_w_q': 'delta_w', 'delta_q_norm': 'delta_w', 'delta_sinks': 'delta_w', 'delta_w_o': 'delta_w', 'delta_rel_bias': 'delta_w', 'new_m_ffn_norm': 'new_m', 'new_m_ffn_w1': 'new_m', 'new_m_ffn_w3': 'new_m', 'new_m_ffn_w2': 'new_m', 'new_m_ssm_norm': 'new_m', 'new_m_ssm_w_in': 'new_m', 'new_m_ssm_conv_w': 'new_m', 'new_m_ssm_conv_b': 'new_m', 'new_m_ssm_dt_bias': 'new_m', 'new_m_ssm_a_log': 'new_m', 'new_m_ssm_d': 'new_m', 'new_m_ssm_gate_norm': 'new_m', 'new_m_ssm_w_out': 'new_m', 'new_m_kv_norm': 'new_m', 'new_m_w_kv': 'new_m', 'new_m_k_norm': 'new_m', 'new_m_attn_norm': 'new_m', 'new_m_w_q': 'new_m', 'new_m_q_norm': 'new_m', 'new_m_sinks': 'new_m', 'new_m_w_o': 'new_m', 'new_m_rel_bias': 'new_m', 'new_v_ffn_norm': 'new_v', 'new_v_ffn_w1': 'new_v', 'new_v_ffn_w3': 'new_v', 'new_v_ffn_w2': 'new_v', 'new_v_ssm_norm': 'new_v', 'new_v_ssm_w_in': 'new_v', 'new_v_ssm_conv_w': 'new_v', 'new_v_ssm_conv_b': 'new_v', 'new_v_ssm_dt_bias': 'new_v', 'new_v_ssm_a_log': 'new_v', 'new_v_ssm_d': 'new_v', 'new_v_ssm_gate_norm': 'new_v', 'new_v_ssm_w_out': 'new_v', 'new_v_kv_norm': 'new_v', 'new_v_w_kv': 'new_v', 'new_v_k_norm': 'new_v', 'new_v_attn_norm': 'new_v', 'new_v_w_q': 'new_v', 'new_v_q_norm': 'new_v', 'new_v_sinks': 'new_v', 'new_v_w_o': 'new_v', 'new_v_rel_bias': 'new_v'}


def _forward(args):
    return _fwd_reference(*[args[k] for k in FWD_PARAMS])


def _output_shape():
    def fwd():
        inp = _fwd_setup_inputs(0)
        return _fwd_reference(*[inp[k] for k in FWD_PARAMS])
    out = _jax.eval_shape(fwd)
    return out.shape, out.dtype

N_MICROBATCH = 1
ADAM_LR = 0.001
ADAM_B1 = 0.9
ADAM_B2 = 0.999
ADAM_EPS = 1e-08
ADAM_WD = 0.01
ADAM_STEP = 10
PER_EXAMPLE_BATCH_AXIS = {'x': 0, 'loss_target': 0}
SHARED_INPUTS = []
_WEIGHT_DTYPES = {'ffn_norm': _jnp.float32, 'ffn_w1': _jnp.float32, 'ffn_w3': _jnp.float32, 'ffn_w2': _jnp.float32, 'ssm_norm': _jnp.float32, 'ssm_w_in': _jnp.float32, 'ssm_conv_w': _jnp.float32, 'ssm_conv_b': _jnp.float32, 'ssm_dt_bias': _jnp.float32, 'ssm_a_log': _jnp.float32, 'ssm_d': _jnp.float32, 'ssm_gate_norm': _jnp.float32, 'ssm_w_out': _jnp.float32, 'kv_norm': _jnp.float32, 'w_kv': _jnp.float32, 'k_norm': _jnp.float32, 'attn_norm': _jnp.float32, 'w_q': _jnp.float32, 'q_norm': _jnp.float32, 'sinks': _jnp.float32, 'w_o': _jnp.float32, 'rel_bias': _jnp.float32}
MOMENT_SCALE = {'ffn_norm': 1.238448e+01, 'ffn_w1': 1.903698e-01, 'ffn_w3': 2.095339e-01, 'ffn_w2': 3.430001e-01, 'ssm_norm': 1.410017e+00, 'ssm_w_in': 4.703475e-01, 'ssm_conv_w': 2.285081e+00, 'ssm_conv_b': 6.773200e+00, 'ssm_dt_bias': 3.462561e+00, 'ssm_a_log': 1.171645e+01, 'ssm_d': 2.367439e+01, 'ssm_gate_norm': 4.384322e+01, 'ssm_w_out': 5.071312e+00, 'kv_norm': 3.583218e+00, 'w_kv': 5.991719e+00, 'k_norm': 1.460293e+01, 'attn_norm': 1.086145e-01, 'w_q': 1.077970e-01, 'q_norm': 1.459515e+01, 'sinks': 1.636113e+00, 'w_o': 1.668711e+00, 'rel_bias': 1.994142e+00}


def _to_microbatches(a, axis):
    t = _jnp.moveaxis(a, axis, 0)
    t = t.reshape((N_MICROBATCH, t.shape[0] // N_MICROBATCH) + t.shape[1:])
    return _jnp.moveaxis(t, 1, axis + 1)


def setup_inputs(seed: int = 0) -> dict:
    inp = _fwd_setup_inputs(seed)
    key = _jax.random.fold_in(_jax.random.key(seed), 7919)
    shape, _ = _output_shape()
    out = dict(inp)
    out["loss_target"] = _jax.random.normal(_jax.random.fold_in(key, 0), shape, _jnp.float32)
    for i, name in enumerate(TWIN_WEIGHTS):
        w = inp[name].astype(_jnp.float32)
        if MOMENT_SCALE is None:
            s = _jnp.sqrt(_jnp.mean(_jnp.square(w)) + 1e-30)
        else:
            s = MOMENT_SCALE[name]
        km, kv = _jax.random.split(_jax.random.fold_in(key, i + 1))
        out[name] = w
        out["m_" + name] = s * _jax.random.normal(km, w.shape, _jnp.float32)
        out["v_" + name] = (s * s) * _jax.random.uniform(kv, w.shape, _jnp.float32, 0.5, 1.5)
    if N_MICROBATCH > 1:
        for name, axis in PER_EXAMPLE_BATCH_AXIS.items():
            out[name] = _to_microbatches(out[name], axis)
    return {'x': out['x'], 'ffn_norm': out['ffn_norm'], 'ffn_w1': out['ffn_w1'], 'ffn_w3': out['ffn_w3'], 'ffn_w2': out['ffn_w2'], 'ssm_norm': out['ssm_norm'], 'ssm_w_in': out['ssm_w_in'], 'ssm_conv_w': out['ssm_conv_w'], 'ssm_conv_b': out['ssm_conv_b'], 'ssm_dt_bias': out['ssm_dt_bias'], 'ssm_a_log': out['ssm_a_log'], 'ssm_d': out['ssm_d'], 'ssm_gate_norm': out['ssm_gate_norm'], 'ssm_w_out': out['ssm_w_out'], 'kv_norm': out['kv_norm'], 'w_kv': out['w_kv'], 'k_norm': out['k_norm'], 'attn_norm': out['attn_norm'], 'w_q': out['w_q'], 'q_norm': out['q_norm'], 'sinks': out['sinks'], 'w_o': out['w_o'], 'rel_bias': out['rel_bias'], 'loss_target': out['loss_target'], 'm_ffn_norm': out['m_ffn_norm'], 'm_ffn_w1': out['m_ffn_w1'], 'm_ffn_w3': out['m_ffn_w3'], 'm_ffn_w2': out['m_ffn_w2'], 'm_ssm_norm': out['m_ssm_norm'], 'm_ssm_w_in': out['m_ssm_w_in'], 'm_ssm_conv_w': out['m_ssm_conv_w'], 'm_ssm_conv_b': out['m_ssm_conv_b'], 'm_ssm_dt_bias': out['m_ssm_dt_bias'], 'm_ssm_a_log': out['m_ssm_a_log'], 'm_ssm_d': out['m_ssm_d'], 'm_ssm_gate_norm': out['m_ssm_gate_norm'], 'm_ssm_w_out': out['m_ssm_w_out'], 'm_kv_norm': out['m_kv_norm'], 'm_w_kv': out['m_w_kv'], 'm_k_norm': out['m_k_norm'], 'm_attn_norm': out['m_attn_norm'], 'm_w_q': out['m_w_q'], 'm_q_norm': out['m_q_norm'], 'm_sinks': out['m_sinks'], 'm_w_o': out['m_w_o'], 'm_rel_bias': out['m_rel_bias'], 'v_ffn_norm': out['v_ffn_norm'], 'v_ffn_w1': out['v_ffn_w1'], 'v_ffn_w3': out['v_ffn_w3'], 'v_ffn_w2': out['v_ffn_w2'], 'v_ssm_norm': out['v_ssm_norm'], 'v_ssm_w_in': out['v_ssm_w_in'], 'v_ssm_conv_w': out['v_ssm_conv_w'], 'v_ssm_conv_b': out['v_ssm_conv_b'], 'v_ssm_dt_bias': out['v_ssm_dt_bias'], 'v_ssm_a_log': out['v_ssm_a_log'], 'v_ssm_d': out['v_ssm_d'], 'v_ssm_gate_norm': out['v_ssm_gate_norm'], 'v_ssm_w_out': out['v_ssm_w_out'], 'v_kv_norm': out['v_kv_norm'], 'v_w_kv': out['v_w_kv'], 'v_k_norm': out['v_k_norm'], 'v_attn_norm': out['v_attn_norm'], 'v_w_q': out['v_w_q'], 'v_q_norm': out['v_q_norm'], 'v_sinks': out['v_sinks'], 'v_w_o': out['v_w_o'], 'v_rel_bias': out['v_rel_bias']}


def _loss(weights, diff, rest, loss_target):
    with _jax.named_scope("forward"):
        args = {**rest, TWIN_DIFF_INPUT: diff, **{k: w.astype(_WEIGHT_DTYPES[k]) for k, w in weights.items()}}
        y = _forward(args)
    with _jax.named_scope("loss_head"):
        err = _jnp.square(y.astype(_jnp.float32) - loss_target)
        return 0.5 * _jnp.sum(_jnp.mean(err, axis=-1)) if err.ndim else 0.5 * err


def _adamw(w, g, m, v):
    m = ADAM_B1 * m + (1.0 - ADAM_B1) * g
    v = ADAM_B2 * v + (1.0 - ADAM_B2) * _jnp.square(g)
    m_hat = m / (1.0 - ADAM_B1 ** ADAM_STEP)
    v_hat = v / (1.0 - ADAM_B2 ** ADAM_STEP)
    delta = -ADAM_LR * (m_hat / (_jnp.sqrt(v_hat) + ADAM_EPS) + ADAM_WD * w)
    return delta, m, v


def reference(x, ffn_norm, ffn_w1, ffn_w3, ffn_w2, ssm_norm, ssm_w_in, ssm_conv_w, ssm_conv_b, ssm_dt_bias, ssm_a_log, ssm_d, ssm_gate_norm, ssm_w_out, kv_norm, w_kv, k_norm, attn_norm, w_q, q_norm, sinks, w_o, rel_bias, loss_target, m_ffn_norm, m_ffn_w1, m_ffn_w3, m_ffn_w2, m_ssm_norm, m_ssm_w_in, m_ssm_conv_w, m_ssm_conv_b, m_ssm_dt_bias, m_ssm_a_log, m_ssm_d, m_ssm_gate_norm, m_ssm_w_out, m_kv_norm, m_w_kv, m_k_norm, m_attn_norm, m_w_q, m_q_norm, m_sinks, m_w_o, m_rel_bias, v_ffn_norm, v_ffn_w1, v_ffn_w3, v_ffn_w2, v_ssm_norm, v_ssm_w_in, v_ssm_conv_w, v_ssm_conv_b, v_ssm_dt_bias, v_ssm_a_log, v_ssm_d, v_ssm_gate_norm, v_ssm_w_out, v_kv_norm, v_w_kv, v_k_norm, v_attn_norm, v_w_q, v_q_norm, v_sinks, v_w_o, v_rel_bias):
    given = dict(x=x, ffn_norm=ffn_norm, ffn_w1=ffn_w1, ffn_w3=ffn_w3, ffn_w2=ffn_w2, ssm_norm=ssm_norm, ssm_w_in=ssm_w_in, ssm_conv_w=ssm_conv_w, ssm_conv_b=ssm_conv_b, ssm_dt_bias=ssm_dt_bias, ssm_a_log=ssm_a_log, ssm_d=ssm_d, ssm_gate_norm=ssm_gate_norm, ssm_w_out=ssm_w_out, kv_norm=kv_norm, w_kv=w_kv, k_norm=k_norm, attn_norm=attn_norm, w_q=w_q, q_norm=q_norm, sinks=sinks, w_o=w_o, rel_bias=rel_bias, loss_target=loss_target, m_ffn_norm=m_ffn_norm, m_ffn_w1=m_ffn_w1, m_ffn_w3=m_ffn_w3, m_ffn_w2=m_ffn_w2, m_ssm_norm=m_ssm_norm, m_ssm_w_in=m_ssm_w_in, m_ssm_conv_w=m_ssm_conv_w, m_ssm_conv_b=m_ssm_conv_b, m_ssm_dt_bias=m_ssm_dt_bias, m_ssm_a_log=m_ssm_a_log, m_ssm_d=m_ssm_d, m_ssm_gate_norm=m_ssm_gate_norm, m_ssm_w_out=m_ssm_w_out, m_kv_norm=m_kv_norm, m_w_kv=m_w_kv, m_k_norm=m_k_norm, m_attn_norm=m_attn_norm, m_w_q=m_w_q, m_q_norm=m_q_norm, m_sinks=m_sinks, m_w_o=m_w_o, m_rel_bias=m_rel_bias, v_ffn_norm=v_ffn_norm, v_ffn_w1=v_ffn_w1, v_ffn_w3=v_ffn_w3, v_ffn_w2=v_ffn_w2, v_ssm_norm=v_ssm_norm, v_ssm_w_in=v_ssm_w_in, v_ssm_conv_w=v_ssm_conv_w, v_ssm_conv_b=v_ssm_conv_b, v_ssm_dt_bias=v_ssm_dt_bias, v_ssm_a_log=v_ssm_a_log, v_ssm_d=v_ssm_d, v_ssm_gate_norm=v_ssm_gate_norm, v_ssm_w_out=v_ssm_w_out, v_kv_norm=v_kv_norm, v_w_kv=v_w_kv, v_k_norm=v_k_norm, v_attn_norm=v_attn_norm, v_w_q=v_w_q, v_q_norm=v_q_norm, v_sinks=v_sinks, v_w_o=v_w_o, v_rel_bias=v_rel_bias)
    weights = {n: given[n] for n in TWIN_WEIGHTS}
    shared = {n: given[n] for n in SHARED_INPUTS}
    per_example = {n: given[n] for n in ['x']}
    grad_fn = _jax.value_and_grad(_loss, argnums=(0, 1))

    def one_microbatch(ex, loss_target):
        ex = dict(ex)
        diff = ex.pop(TWIN_DIFF_INPUT)
        return grad_fn(weights, diff, {**shared, **ex}, loss_target)

    if N_MICROBATCH == 1:
        loss, (grad_w, grad_x) = one_microbatch(per_example, given["loss_target"])
    else:
        def body(carry, xs):
            loss_sum, grad_sum = carry
            l_k, (gw_k, gx_k) = one_microbatch(xs[0], xs[1])
            with _jax.named_scope("update"):
                return (loss_sum + l_k, _jax.tree.map(_jnp.add, grad_sum, gw_k)), gx_k

        init = (_jnp.zeros((), _jnp.float32), _jax.tree.map(_jnp.zeros_like, weights))
        (loss, grad_w), grad_x = _jax.lax.scan(body, init, (per_example, given["loss_target"]))
    with _jax.named_scope("update"):
        delta_w, new_m, new_v = {}, {}, {}
        for n in TWIN_WEIGHTS:
            delta_w[n], new_m[n], new_v[n] = _adamw(weights[n], grad_w[n], given["m_" + n], given["v_" + n])
    return (loss, grad_x, *[grad_w[n] for n in TWIN_WEIGHTS], *[delta_w[n] for n in TWIN_WEIGHTS],
            *[new_m[n] for n in TWIN_WEIGHTS], *[new_v[n] for n in TWIN_WEIGHTS])
```

```python
import functools
import math

import numpy as np
import jax
import jax.numpy as jnp
from jax import lax
from jax.experimental import pallas as pl
from jax.experimental.pallas import tpu as pltpu

F32 = jnp.float32
BF16 = jnp.bfloat16

D_MODEL = 1024
D_FF = 2816
N_DEV = 8
SSM_D_INNER = 2048
SSM_HEAD_DIM = 64
SSM_HEADS = 32
SSM_GROUPS = 4
SSM_STATE = 128
SSM_CONV = 4
SSM_CHUNK = 256
SSM_CONV_DIM = SSM_D_INNER + 2 * SSM_GROUPS * SSM_STATE
ATT_HEAD_DIM = 64
ATT_HEADS = 16
ATT_KV_HEADS = 2
ATT_GROUP = 8
ATT_WINDOW = 128
REL_BUCKETS = 32
EPS = 1e-6
NEG = -1e30

ADAM_LR = 0.001
ADAM_B1 = 0.9
ADAM_B2 = 0.999
ADAM_EPS = 1e-08
ADAM_WD = 0.01
ADAM_STEP = 10

VMEM_LIMIT_BYTES = 52 * 1024 * 1024
LANES = 128
MESH_ID = pl.DeviceIdType.MESH


def _cparams(*sem):
    return pltpu.CompilerParams(dimension_semantics=sem, vmem_limit_bytes=VMEM_LIMIT_BYTES)


def _pick(dim, cands):
    for c in cands:
        if dim % c == 0:
            return c
    return dim


def mm(a, b, *, ta=False, tb=False, out_dtype=F32, res=None, alpha=1.0, name):
    if ta:
        k_dim, m_dim = a.shape
    else:
        m_dim, k_dim = a.shape
    if tb:
        n_dim, k2 = b.shape
    else:
        k2, n_dim = b.shape
    assert k_dim == k2, (a.shape, b.shape, ta, tb)
    tn = _pick(n_dim, (1024, 1408, 512, 256, 128))
    tm = _pick(m_dim, (1024, 512, 256, 128)) if tn <= 1024 else _pick(m_dim, (512, 256, 128))
    tk = _pick(k_dim, (512, 1408, 256, 128))
    nk = k_dim // tk
    has_res = res is not None
    dn = (((0 if ta else 1,), (1 if tb else 0,)), ((), ()))

    def body(*refs):
        if has_res:
            a_ref, b_ref, r_ref, o_ref, acc_ref = refs
        else:
            a_ref, b_ref, o_ref, acc_ref = refs
        k = pl.program_id(2)

        @pl.when(k == 0)
        def _():
            acc_ref[...] = jnp.zeros_like(acc_ref)

        acc_ref[...] += lax.dot_general(a_ref[...].astype(BF16), b_ref[...].astype(BF16), dn,
                                        preferred_element_type=F32)

        @pl.when(k == nk - 1)
        def _():
            r = acc_ref[...]
            if alpha != 1.0:
                r = r * alpha
            if has_res:
                r = r_ref[...] + r
            o_ref[...] = r.astype(o_ref.dtype)

    a_spec = pl.BlockSpec((tk, tm), lambda i, j, k: (k, i)) if ta else pl.BlockSpec((tm, tk), lambda i, j, k: (i, k))
    b_spec = pl.BlockSpec((tn, tk), lambda i, j, k: (j, k)) if tb else pl.BlockSpec((tk, tn), lambda i, j, k: (k, j))
    o_spec = pl.BlockSpec((tm, tn), lambda i, j, k: (i, j))
    in_specs = [a_spec, b_spec] + ([o_spec] if has_res else [])
    args = (a, b) + ((res,) if has_res else ())
    return pl.pallas_call(
        body, name=name, grid=(m_dim // tm, n_dim // tn, nk), in_specs=in_specs, out_specs=o_spec,
        out_shape=jax.ShapeDtypeStruct((m_dim, n_dim), out_dtype),
        scratch_shapes=[pltpu.VMEM((tm, tn), F32)],
        compiler_params=_cparams("parallel", "parallel", "arbitrary"),
    )(*args)


def rowmap(fn, rows, consts=(), out_rows=(), out_accs=(), *, tm, name):
    first = rows[0][0] if isinstance(rows[0], tuple) else rows[0]
    t_dim = first.shape[0]
    assert t_dim % tm == 0, (t_dim, tm)
    n_r, n_c, n_o = len(rows), len(consts), len(out_rows)

    def body(*refs):
        ins = [r[...] for r in refs[:n_r + n_c]]
        o_refs = refs[n_r + n_c:]
        outs = tuple(fn(*ins))
        for o_ref, val in zip(o_refs[:n_o], outs[:n_o]):
            o_ref[...] = val.astype(o_ref.dtype)
        if out_accs:
            @pl.when(pl.program_id(0) == 0)
            def _():
                for o_ref in o_refs[n_o:]:
                    o_ref[...] = jnp.zeros_like(o_ref)

            for o_ref, val in zip(o_refs[n_o:], outs[n_o:]):
                o_ref[...] += val

    in_specs, args = [], []
    for r in rows:
        if isinstance(r, tuple):
            args.append(r[0])
            in_specs.append(r[1])
        else:
            args.append(r)
            in_specs.append(pl.BlockSpec((tm, r.shape[1]), lambda i: (i, 0)))
    for c in consts:
        args.append(c)
        in_specs.append(pl.BlockSpec(c.shape, lambda i, nd=c.ndim: (0,) * nd))
    out_specs = [pl.BlockSpec((tm, w), lambda i: (i, 0)) for (w, _) in out_rows]
    out_specs += [pl.BlockSpec(s, lambda i, nd=len(s): (0,) * nd) for s in out_accs]
    out_shape = [jax.ShapeDtypeStruct((t_dim, w), dt) for (w, dt) in out_rows]
    out_shape += [jax.ShapeDtypeStruct(s, F32) for s in out_accs]
    return pl.pallas_call(
        body, name=name, grid=(t_dim // tm,), in_specs=in_specs, out_specs=out_specs, out_shape=out_shape,
        compiler_params=_cparams("arbitrary"),
    )(*args)


def _rms_fwd(x, g):
    r = lax.rsqrt(jnp.mean(x * x, axis=-1, keepdims=True) + EPS)
    return x * r * g


def _rms_bwd(x, g, dy):
    r = lax.rsqrt(jnp.mean(x * x, axis=-1, keepdims=True) + EPS)
    xh = x * r
    dg = jnp.sum(dy * xh, axis=0, keepdims=True)
    dxh = dy * g
    dx = r * (dxh - xh * jnp.mean(dxh * xh, axis=-1, keepdims=True))
    return dx, dg


def _sigmoid(x):
    return 1.0 / (1.0 + jnp.exp(-x))


def _silu(x):
    return x * _sigmoid(x)


def _silu_grad(x):
    s = _sigmoid(x)
    return s * (1.0 + x * (1.0 - s))


def _split3(x):
    hi = x.astype(BF16)
    r1 = x - hi.astype(F32)
    mid = r1.astype(BF16)
    lo = (r1 - mid.astype(F32)).astype(BF16)
    return hi, mid, lo


def _dot(a, b, dn=(((1,), (0,)), ((), ()))):
    return lax.dot_general(a.astype(BF16), b.astype(BF16), dn, preferred_element_type=F32)


NT = (((1,), (1,)), ((), ()))
TN = (((0,), (0,)), ((), ()))


def _col_of(mat, h):
    lane = lax.broadcasted_iota(jnp.int32, mat.shape, 1)
    return jnp.sum(jnp.where(lane == h, mat, 0.0), axis=1, keepdims=True)


def ffn_fwd(h, g, w13, w2, nm):
    u, = rowmap(lambda x, gg: (_rms_fwd(x, gg),), [h], [g], [(D_MODEL, BF16)], tm=256, name=nm + "_norm")
    ab = mm(u, w13, out_dtype=BF16, name=nm + "_up")

    def gate(t):
        t = t.astype(F32)
        return (_silu(t[:, :D_FF]) * t[:, D_FF:],)

    hm, = rowmap(gate, [ab], [], [(D_FF, BF16)], tm=256, name=nm + "_gate")
    out = mm(hm, w2, res=h, alpha=0.5, name=nm + "_down")
    return out, (u, ab, hm)


def norm_bwd(h, g, du, dout, nm):
    def fn(x, d_u, d_o, gg):
        dx, dg = _rms_bwd(x, gg, d_u)
        dh = d_o + dx
        return dh, dh, dg

    return rowmap(fn, [h, du, dout], [g], [(D_MODEL, F32), (D_MODEL, BF16)], [(1, D_MODEL)], tm=256, name=nm)


def ffn_bwd(h, g, w13, w2, saved, dout, dout_bf, nm):
    u, ab, hm = saved
    dw2 = mm(hm, dout_bf, ta=True, alpha=0.5, name=nm + "_dw2")
    dhm = mm(dout_bf, w2, tb=True, alpha=0.5, out_dtype=BF16, name=nm + "_dhm")

    def gate_bwd(d, t):
        d = d.astype(F32)
        t = t.astype(F32)
        a, b = t[:, :D_FF], t[:, D_FF:]
        return (jnp.concatenate([d * b * _silu_grad(a), d * _silu(a)], axis=1),)

    dab, = rowmap(gate_bwd, [dhm, ab], [], [(2 * D_FF, BF16)], tm=256, name=nm + "_dgate")
    dw13 = mm(u, dab, ta=True, name=nm + "_dw13")
    du = mm(dab, w13, tb=True, name=nm + "_du")
    dh, dh_bf, dg = norm_bwd(h, g, du, dout, nm + "_dnorm")
    return dh, dh_bf, dg, dw13, dw2


def _conv_pre(x, halo, w, b, tm):
    halo = jnp.where(pl.program_id(0) > 0, halo, 0.0)
    xx = jnp.concatenate([halo, x], axis=0)
    shifted = [xx[5 + k:5 + k + tm] for k in range(SSM_CONV)]
    acc = b + shifted[0] * w[0:1]
    for k in range(1, SSM_CONV):
        acc = acc + shifted[k] * w[k:k + 1]
    return acc, shifted


def _prev_halo_spec(tm, width):
    return pl.BlockSpec((8, width), lambda i: (jnp.maximum(i * (tm // 8) - 1, 0), 0))


def conv_fwd(xbc_raw, w, b, nm):
    tm = 128

    def fn(x, halo, ww, bb):
        acc, _ = _conv_pre(x, halo, ww, bb, tm)
        return (_silu(acc),)

    out, = rowmap(fn, [xbc_raw, (xbc_raw, _prev_halo_spec(tm, SSM_CONV_DIM))], [w, b],
                  [(SSM_CONV_DIM, F32)], tm=tm, name=nm)
    return out


def conv_bwd(xbc_raw, w, b, dxs, db_in, dc_in, nm):
    tm = 128
    t_dim = xbc_raw.shape[0]

    def fn1(x, halo, d1, d2, d3, ww, bb):
        acc, shifted = _conv_pre(x, halo, ww, bb, tm)
        dacc = jnp.concatenate([d1, d2, d3], axis=1) * _silu_grad(acc)
        dw = jnp.concatenate([jnp.sum(dacc * s, axis=0, keepdims=True) for s in shifted], axis=0)
        return dacc, dw, jnp.sum(dacc, axis=0, keepdims=True)

    dacc, dw, dbias = rowmap(fn1, [xbc_raw, (xbc_raw, _prev_halo_spec(tm, SSM_CONV_DIM)), dxs, db_in, dc_in],
                             [w, b], [(SSM_CONV_DIM, F32)], [(SSM_CONV, SSM_CONV_DIM), (1, SSM_CONV_DIM)],
                             tm=tm, name=nm + "_a")
    n_tiles = t_dim // tm

    def fn2(d, nxt, ww):
        nxt = jnp.where(pl.program_id(0) < n_tiles - 1, nxt, 0.0)
        dd = jnp.concatenate([d, nxt], axis=0)
        out = dd[3:3 + tm] * ww[0:1]
        for k in range(1, SSM_CONV):
            out = out + dd[3 - k:3 - k + tm] * ww[k:k + 1]
        return (out,)

    nxt_spec = pl.BlockSpec((8, SSM_CONV_DIM), lambda i: (jnp.minimum((i + 1) * (tm // 8), t_dim // 8 - 1), 0))
    dx, = rowmap(fn2, [dacc, (dacc, nxt_spec)], [w], [(SSM_CONV_DIM, BF16)], tm=tm, name=nm + "_b")
    return dx, dw, dbias


def _ssd_cumsums(dt_ref, dtT_ref, arow_ref, acol_ref, acol_s, arowT_s):
    L = SSM_CHUNK
    r = lax.broadcasted_iota(jnp.int32, (L, L), 0)
    c = lax.broadcasted_iota(jnp.int32, (L, L), 1)
    tril = (r >= c).astype(BF16)
    triu = (r <= c).astype(BF16)
    dta = dt_ref[...] * arow_ref[...]
    acc = None
    for p in _split3(dta):
        t = jnp.dot(tril, p, preferred_element_type=F32)
        acc = t if acc is None else acc + t
    acol_s[...] = acc
    dtaT = dtT_ref[...] * acol_ref[...]
    acc = None
    for p in _split3(dtaT):
        t = jnp.dot(p, triu, preferred_element_type=F32)
        acc = t if acc is None else acc + t
    arowT_s[...] = acc


def _ssd_head_terms(h, cb, acol_s, arowT_s, dt_ref, dtT_ref):
    L = SSM_CHUNK
    r = lax.broadcasted_iota(jnp.int32, (L, L), 0)
    c = lax.broadcasted_iota(jnp.int32, (L, L), 1)
    a_col = _col_of(acol_s[...], h)
    a_row = arowT_s[pl.ds(h, 1), :]
    dt_col = _col_of(dt_ref[...], h)
    dt_row = dtT_ref[pl.ds(h, 1), :]
    a_last = a_col[L - 1:L, :]
    lm = jnp.exp(jnp.where(r >= c, a_col - a_row, NEG))
    m = cb * lm * dt_row
    e_a = jnp.exp(a_col)
    e_w = jnp.exp(a_last - a_col)
    return a_col, a_last, dt_col, dt_row, lm, m, e_a, e_w


def ssd_fwd(xbc, dt, dtT, a_row, a_col, dvec, nm):
    t_dim = xbc.shape[0]
    L, P, N, H = SSM_CHUNK, SSM_HEAD_DIM, SSM_STATE, SSM_HEADS
    nc = t_dim // L
    n_pairs = H // 2
    xcols = SSM_D_INNER // LANES

    def body(x_ref, b_ref, c_ref, dt_ref, dtT_ref, arow_ref, acol_ref, dvec_ref, y_ref, st_ref,
             s_s, cb_s, acol_s, arowT_s):
        ci = pl.program_id(0)
        hp = pl.program_id(1)

        @pl.when((ci == 0) & (hp == 0))
        def _():
            s_s[...] = jnp.zeros_like(s_s)

        @pl.when(hp == 0)
        def _():
            _ssd_cumsums(dt_ref, dtT_ref, arow_ref, acol_ref, acol_s, arowT_s)

        @pl.when(hp % 4 == 0)
        def _():
            cb_s[...] = _dot(c_ref[...], b_ref[...], NT)

        bmat = b_ref[...]
        cmat = c_ref[...]
        cb = cb_s[...]
        dv = dvec_ref[pl.ds(hp, 1), :]
        for e in range(2):
            h = hp * 2 + e
            x = x_ref[:, e * P:(e + 1) * P]
            a_c, a_last, dt_col, dt_row, lm, m, e_a, e_w = _ssd_head_terms(h, cb, acol_s, arowT_s, dt_ref, dtT_ref)
            s = s_s[h]
            st_ref[0, e] = s
            y = _dot(m, x) + e_a * _dot(cmat, s, NT) + dv[:, e * P:(e + 1) * P] * x
            y_ref[:, e * P:(e + 1) * P] = y
            u = x * (e_w * dt_col)
            s_s[h] = jnp.exp(a_last) * s + _dot(u, bmat, TN)

    in_specs = [
        pl.BlockSpec((L, LANES), lambda c, p: (c, p)),
        pl.BlockSpec((L, LANES), lambda c, p: (c, xcols + p // 4)),
        pl.BlockSpec((L, LANES), lambda c, p: (c, xcols + SSM_GROUPS + p // 4)),
        pl.BlockSpec((L, H), lambda c, p: (c, 0)),
        pl.BlockSpec((H, L), lambda c, p: (0, c)),
        pl.BlockSpec((1, H), lambda c, p: (0, 0)),
        pl.BlockSpec((H, 1), lambda c, p: (0, 0)),
        pl.BlockSpec((n_pairs, LANES), lambda c, p: (0, 0)),
    ]
    out_specs = [
        pl.BlockSpec((L, LANES), lambda c, p: (c, p)),
        pl.BlockSpec((1, 2, P, N), lambda c, p: (c, p, 0, 0)),
    ]
    return pl.pallas_call(
        body, name=nm, grid=(nc, n_pairs), in_specs=in_specs, out_specs=out_specs,
        out_shape=[jax.ShapeDtypeStruct((t_dim, SSM_D_INNER), F32), jax.ShapeDtypeStruct((nc, H, P, N), F32)],
        scratch_shapes=[pltpu.VMEM((H, P, N), F32), pltpu.VMEM((L, L), F32), pltpu.VMEM((L, H), F32),
                        pltpu.VMEM((H, L), F32)],
        compiler_params=_cparams("arbitrary", "arbitrary"),
    )(xbc, xbc, xbc, dt, dtT, a_row, a_col, dvec)


def ssd_bwd(dy, xbc, dt, dtT, a_row, a_col, dvec, states, nm):
    t_dim = xbc.shape[0]
    L, P, N, H = SSM_CHUNK, SSM_HEAD_DIM, SSM_STATE, SSM_HEADS
    nc = t_dim // L
    n_pairs = H // 2
    xcols = SSM_D_INNER // LANES

    def body(dy_ref, x_ref, b_ref, c_ref, dt_ref, dtT_ref, arow_ref, acol_ref, dvec_ref, st_ref,
             dx_ref, db_ref, dc_ref, dacol_ref, darowT_ref, ddtcol_ref, ddtrowT_ref, dd_ref,
             ds_s, cb_s, dcb_s, acol_s, arowT_s):
        ci = pl.program_id(0)
        hp = pl.program_id(1)

        @pl.when((ci == 0) & (hp == 0))
        def _():
            ds_s[...] = jnp.zeros_like(ds_s)
            dd_ref[...] = jnp.zeros_like(dd_ref)

        @pl.when(hp == 0)
        def _():
            _ssd_cumsums(dt_ref, dtT_ref, arow_ref, acol_ref, acol_s, arowT_s)
            dacol_ref[...] = jnp.zeros_like(dacol_ref)
            ddtcol_ref[...] = jnp.zeros_like(ddtcol_ref)

        @pl.when(hp % 4 == 0)
        def _():
            cb_s[...] = _dot(c_ref[...], b_ref[...], NT)
            dcb_s[...] = jnp.zeros_like(dcb_s)
            db_ref[...] = jnp.zeros_like(db_ref)
            dc_ref[...] = jnp.zeros_like(dc_ref)

        bmat = b_ref[...]
        cmat = c_ref[...]
        cb = cb_s[...]
        dv = dvec_ref[pl.ds(hp, 1), :]
        lane_h = lax.broadcasted_iota(jnp.int32, (L, H), 1)
        row_l = lax.broadcasted_iota(jnp.int32, (L, 1), 0)
        dd_parts = []
        for e in range(2):
            h = hp * 2 + e
            x = x_ref[:, e * P:(e + 1) * P]
            dyh = dy_ref[:, e * P:(e + 1) * P]
            a_c, a_last, dt_col, dt_row, lm, m, e_a, e_w = _ssd_head_terms(h, cb, acol_s, arowT_s, dt_ref, dtT_ref)
            s = st_ref[0, e]
            dsp = ds_s[h]
            d_skip = dv[:, e * P:(e + 1) * P]
            dd_parts.append(jnp.sum(dyh * x, axis=0, keepdims=True))
            dx = d_skip * dyh + _dot(m, dyh, TN)
            dm = _dot(dyh, x, NT)
            q = dm * cb * lm
            ddt_row = jnp.sum(q, axis=0, keepdims=True)
            gmat = q * dt_row
            dcb_s[...] += dm * lm * dt_row
            da_col = jnp.sum(gmat, axis=1, keepdims=True)
            da_row = -jnp.sum(gmat, axis=0, keepdims=True)
            z = _dot(cmat, s, NT)
            dz = e_a * dyh
            da_col = da_col + e_a * jnp.sum(dyh * z, axis=1, keepdims=True)
            dc_ref[...] += _dot(dz, s)
            ds_y = _dot(dz, cmat, TN)
            e_al = jnp.exp(a_last)
            w_col = e_w * dt_col
            du = _dot(bmat, dsp, NT)
            db_ref[...] += _dot(x * w_col, dsp)
            dx = dx + du * w_col
            dw = jnp.sum(du * x, axis=1, keepdims=True)
            ddt_col = dw * e_w
            dwa = dw * w_col
            d_last = jnp.sum(dwa, keepdims=True) + e_al * jnp.sum(dsp * s, keepdims=True)
            da_col = da_col - dwa + jnp.where(row_l == L - 1, d_last, 0.0)
            ds_s[h] = e_al * dsp + ds_y
            dx_ref[:, e * P:(e + 1) * P] = dx
            dacol_ref[...] += jnp.where(lane_h == h, da_col, 0.0)
            ddtcol_ref[...] += jnp.where(lane_h == h, ddt_col, 0.0)
            darowT_ref[pl.ds(h, 1), :] = da_row
            ddtrowT_ref[pl.ds(h, 1), :] = ddt_row
        dd_ref[pl.ds(hp, 1), :] += jnp.concatenate(dd_parts, axis=1)

        @pl.when(hp % 4 == 3)
        def _():
            dcb = dcb_s[...]
            dc_ref[...] += _dot(dcb, bmat)
            db_ref[...] += _dot(dcb, cmat, TN)

    rc = lambda c: nc - 1 - c
    in_specs = [
        pl.BlockSpec((L, LANES), lambda c, p: (rc(c), p)),
        pl.BlockSpec((L, LANES), lambda c, p: (rc(c), p)),
        pl.BlockSpec((L, LANES), lambda c, p: (rc(c), xcols + p // 4)),
        pl.BlockSpec((L, LANES), lambda c, p: (rc(c), xcols + SSM_GROUPS + p // 4)),
        pl.BlockSpec((L, H), lambda c, p: (rc(c), 0)),
        pl.BlockSpec((H, L), lambda c, p: (0, rc(c))),
        pl.BlockSpec((1, H), lambda c, p: (0, 0)),
        pl.BlockSpec((H, 1), lambda c, p: (0, 0)),
        pl.BlockSpec((n_pairs, LANES), lambda c, p: (0, 0)),
        pl.BlockSpec((1, 2, P, N), lambda c, p: (rc(c), p, 0, 0)),
    ]
    out_specs = [
        pl.BlockSpec((L, LANES), lambda c, p: (rc(c), p)),
        pl.BlockSpec((L, LANES), lambda c, p: (rc(c), p // 4)),
        pl.BlockSpec((L, LANES), lambda c, p: (rc(c), p // 4)),
        pl.BlockSpec((L, H), lambda c, p: (rc(c), 0)),
        pl.BlockSpec((H, L), lambda c, p: (0, rc(c))),
        pl.BlockSpec((L, H), lambda c, p: (rc(c), 0)),
        pl.BlockSpec((H, L), lambda c, p: (0, rc(c))),
        pl.BlockSpec((n_pairs, LANES), lambda c, p: (0, 0)),
    ]
    gn = SSM_GROUPS * N
    out_shape = [
        jax.ShapeDtypeStruct((t_dim, SSM_D_INNER), F32), jax.ShapeDtypeStruct((t_dim, gn), F32),
        jax.ShapeDtypeStruct((t_dim, gn), F32), jax.ShapeDtypeStruct((t_dim, H), F32),
        jax.ShapeDtypeStruct((H, t_dim), F32), jax.ShapeDtypeStruct((t_dim, H), F32),
        jax.ShapeDtypeStruct((H, t_dim), F32), jax.ShapeDtypeStruct((n_pairs, LANES), F32),
    ]
    return pl.pallas_call(
        body, name=nm, grid=(nc, n_pairs), in_specs=in_specs, out_specs=out_specs, out_shape=out_shape,
        scratch_shapes=[pltpu.VMEM((H, P, N), F32), pltpu.VMEM((L, L), F32), pltpu.VMEM((L, L), F32),
                        pltpu.VMEM((L, H), F32), pltpu.VMEM((H, L), F32)],
        compiler_params=_cparams("arbitrary", "arbitrary"),
    )(dy, xbc, xbc, xbc, dt, dtT, a_row, a_col, dvec, states)


def _softplus(x):
    return jnp.maximum(x, 0.0) + jnp.log(1.0 + jnp.exp(-jnp.abs(x)))


def ssd_dt_bwd(da, ddt, dt, dt_raw, a_row, dt_bias, nm):
    L = SSM_CHUNK

    def fn(d_a, d_dt, dtv, raw, ar, bias):
        r = lax.broadcasted_iota(jnp.int32, (L, L), 0)
        c = lax.broadcasted_iota(jnp.int32, (L, L), 1)
        triu = (r <= c).astype(BF16)
        acc = None
        for p in _split3(d_a):
            t = jnp.dot(triu, p, preferred_element_type=F32)
            acc = t if acc is None else acc + t
        d_dt = d_dt + acc * ar
        d_a_h = jnp.sum(acc * dtv, axis=0, keepdims=True)
        d_raw = d_dt * _sigmoid(raw + bias)
        return d_raw, d_a_h, jnp.sum(d_raw, axis=0, keepdims=True)

    return rowmap(fn, [da, ddt, dt, dt_raw], [a_row, dt_bias], [(SSM_HEADS, BF16)],
                  [(1, SSM_HEADS), (1, SSM_HEADS)], tm=L, name=nm)


GN_W = SSM_D_INNER // SSM_GROUPS


def mamba_fwd(h, p, nm):
    u, = rowmap(lambda x, gg: (_rms_fwd(x, gg),), [h], [p["ssm_norm"]], [(D_MODEL, BF16)], tm=256, name=nm + "_norm")
    z = mm(u, p["w_z"], name=nm + "_z")
    xbc_raw = mm(u, p["w_xbc"], name=nm + "_xbc")
    dt_raw = mm(u, p["w_dt"], name=nm + "_dt")
    xbc = conv_fwd(xbc_raw, p["conv_w"], p["conv_b"], nm + "_conv")
    dt, = rowmap(lambda r, b: (_softplus(r + b),), [dt_raw], [p["dt_bias"]], [(SSM_HEADS, F32)], tm=256,
                 name=nm + "_softplus")
    dtT = dt.T
    y, states = ssd_fwd(xbc, dt, dtT, p["a_row"], p["a_col"], p["dvec"], nm + "_ssd")

    def gate_norm(yv, zv, gg):
        t = yv * _silu(zv)
        return (jnp.concatenate([_rms_fwd(t[:, k * GN_W:(k + 1) * GN_W], gg[:, k * GN_W:(k + 1) * GN_W])
                                 for k in range(SSM_GROUPS)], axis=1),)

    yn, = rowmap(gate_norm, [y, z], [p["gate_norm"]], [(SSM_D_INNER, BF16)], tm=256, name=nm + "_gatenorm")
    out = mm(yn, p["w_out"], res=h, name=nm + "_out")
    return out, (u, z, xbc_raw, dt_raw, xbc, dt, dtT, y, states, yn)


def mamba_bwd(h, p, saved, dout, dout_bf, nm):
    u, z, xbc_raw, dt_raw, xbc, dt, dtT, y, states, yn = saved
    g = {}
    g["w_out"] = mm(yn, dout_bf, ta=True, name=nm + "_dwout")
    dyn = mm(dout_bf, p["w_out"], tb=True, name=nm + "_dyn")

    def gate_norm_bwd(d, yv, zv, gg):
        sz = _silu(zv)
        t = yv * sz
        dts, dgs = [], []
        for k in range(SSM_GROUPS):
            sl = slice(k * GN_W, (k + 1) * GN_W)
            dt_k, dg_k = _rms_bwd(t[:, sl], gg[:, sl], d[:, sl])
            dts.append(dt_k)
            dgs.append(dg_k)
        d_t = jnp.concatenate(dts, axis=1)
        return d_t * sz, d_t * yv * _silu_grad(zv), jnp.concatenate(dgs, axis=1)

    dy, dz, g["gate_norm"] = rowmap(gate_norm_bwd, [dyn, y, z], [p["gate_norm"]],
                                    [(SSM_D_INNER, F32), (SSM_D_INNER, BF16)], [(1, SSM_D_INNER)], tm=256,
                                    name=nm + "_dgatenorm")
    dxs, db_in, dc_in, dacol, darowT, ddtcol, ddtrowT, dd = ssd_bwd(
        dy, xbc, dt, dtT, p["a_row"], p["a_col"], p["dvec"], states, nm + "_dssd")
    g["dvec"] = dd
    ddt_raw, g["a"], g["dt_bias"] = ssd_dt_bwd(dacol + darowT.T, ddtcol + ddtrowT.T, dt, dt_raw, p["a_row"],
                                               p["dt_bias"], nm + "_ddt")
    dxbc_raw, g["conv_w"], g["conv_b"] = conv_bwd(xbc_raw, p["conv_w"], p["conv_b"], dxs, db_in, dc_in, nm + "_dconv")
    g["w_z"] = mm(u, dz, ta=True, name=nm + "_dwz")
    g["w_xbc"] = mm(u, dxbc_raw, ta=True, name=nm + "_dwxbc")
    g["w_dt"] = mm(u, ddt_raw, ta=True, name=nm + "_dwdt")
    du = mm(dz, p["w_z"], tb=True, name=nm + "_du1")
    du = mm(dxbc_raw, p["w_xbc"], tb=True, res=du, name=nm + "_du2")
    du = mm(ddt_raw, p["w_dt"], tb=True, res=du, name=nm + "_du3")
    dh, dh_bf, g["ssm_norm"] = norm_bwd(h, p["ssm_norm"], du, dout, nm + "_dnorm")
    return dh, dh_bf, g


KV_W = ATT_KV_HEADS * ATT_HEAD_DIM


def kv_fwd(h, p, nm):
    u, = rowmap(lambda x, gg: (_rms_fwd(x, gg),), [h], [p["kv_norm"]], [(D_MODEL, BF16)], tm=256, name=nm + "_norm")
    kv_raw = mm(u, p["w_kv"], name=nm + "_proj")

    def knorm(t, gg):
        ks = [_rms_fwd(t[:, j * ATT_HEAD_DIM:(j + 1) * ATT_HEAD_DIM], gg) for j in range(ATT_KV_HEADS)]
        return jnp.concatenate(ks, axis=1), t[:, KV_W:]

    k, v = rowmap(knorm, [kv_raw], [p["k_norm"]], [(KV_W, F32), (KV_W, F32)], tm=256, name=nm + "_knorm")
    return k, v, (u, kv_raw)


def kv_bwd(h, p, saved, dk_cur, dk_prev, dv_cur, dv_prev, dout, nm):
    u, kv_raw = saved
    t_dim = h.shape[0]
    tm = ATT_WINDOW
    nb = t_dim // tm
    nxt = pl.BlockSpec((tm, KV_W), lambda i: (jnp.minimum(i + 1, nb - 1), 0))

    def fn(dkc, dkp, dvc, dvp, t, gg):
        live = pl.program_id(0) < nb - 1
        dk = dkc + jnp.where(live, dkp, 0.0)
        dv = dvc + jnp.where(live, dvp, 0.0)
        outs, dgs = [], None
        for j in range(ATT_KV_HEADS):
            sl = slice(j * ATT_HEAD_DIM, (j + 1) * ATT_HEAD_DIM)
            dx, dg = _rms_bwd(t[:, sl], gg, dk[:, sl])
            outs.append(dx)
            dgs = dg if dgs is None else dgs + dg
        return jnp.concatenate(outs + [dv], axis=1), dgs

    dkv_raw, dknorm = rowmap(fn, [dk_cur, (dk_prev, nxt), dv_cur, (dv_prev, nxt), kv_raw], [p["k_norm"]],
                             [(2 * KV_W, BF16)], [(1, ATT_HEAD_DIM)], tm=tm, name=nm + "_dknorm")
    g = {"k_norm": dknorm}
    g["w_kv"] = mm(u, dkv_raw, ta=True, name=nm + "_dwkv")
    du = mm(dkv_raw, p["w_kv"], tb=True, name=nm + "_du")
    dh, dh_bf, g["kv_norm"] = norm_bwd(h, p["kv_norm"], du, dout, nm + "_dnorm")
    return dh, dh_bf, g


def _attn_scores(q_ref, kp_ref, kc_ref, vp_ref, vc_ref, qn_ref, bias_ref, sink_ref, kv):
    hd = ATT_HEAD_DIM
    blk = ATT_WINDOW
    sl = slice(kv * hd, (kv + 1) * hd)
    kk = jnp.concatenate([kp_ref[:, sl], kc_ref[:, sl]], axis=0)
    vv = jnp.concatenate([vp_ref[:, sl], vc_ref[:, sl]], axis=0)
    gq = qn_ref[...]
    raws, rinvs = [], []
    for r in range(ATT_GROUP):
        hh = kv * ATT_GROUP + r
        x = q_ref[:, hh * hd:(hh + 1) * hd]
        raws.append(x)
        rinvs.append(lax.rsqrt(jnp.mean(x * x, axis=-1, keepdims=True) + EPS))
    xh = jnp.concatenate([x * ri for x, ri in zip(raws, rinvs)], axis=0)
    rinv = jnp.concatenate(rinvs, axis=0)
    q8 = xh * gq
    s = _dot(q8, kk, NT) * (hd ** -0.5) + bias_ref[kv]
    colk = lax.broadcasted_iota(jnp.int32, (1, 2 * blk), 1)
    s = jnp.where((pl.program_id(0) > 0) | (colk >= blk), s, NEG)
    sink = sink_ref[kv]
    m = jnp.maximum(jnp.max(s, axis=-1, keepdims=True), sink)
    pexp = jnp.exp(s - m)
    e_sink = jnp.exp(sink - m)
    den = jnp.sum(pexp, axis=-1, keepdims=True) + e_sink
    prob = pexp / den
    return kk, vv, xh, rinv, q8, prob, e_sink / den


def _attn_specs(nb):
    blk = ATT_WINDOW
    cur = lambda i: (i, 0)
    prev = lambda i: (jnp.maximum(i - 1, 0), 0)
    return [
        pl.BlockSpec((blk, D_MODEL), cur),
        pl.BlockSpec((blk, KV_W), prev), pl.BlockSpec((blk, KV_W), cur),
        pl.BlockSpec((blk, KV_W), prev), pl.BlockSpec((blk, KV_W), cur),
        pl.BlockSpec((1, ATT_HEAD_DIM), lambda i: (0, 0)),
        pl.BlockSpec((ATT_KV_HEADS, ATT_GROUP * blk, 2 * blk), lambda i: (0, 0, 0)),
        pl.BlockSpec((ATT_KV_HEADS, ATT_GROUP * blk, 1), lambda i: (0, 0, 0)),
    ]


def attn_fwd(q_raw, k, v, q_norm, bias, sink_col, nm):
    t_dim = q_raw.shape[0]
    blk, hd = ATT_WINDOW, ATT_HEAD_DIM
    nb = t_dim // blk

    def body(q_ref, kp_ref, kc_ref, vp_ref, vc_ref, qn_ref, bias_ref, sink_ref, o_ref):
        for kv in range(ATT_KV_HEADS):
            kk, vv, xh, rinv, q8, prob, p_sink = _attn_scores(q_ref, kp_ref, kc_ref, vp_ref, vc_ref, qn_ref,
                                                              bias_ref, sink_ref, kv)
            o8 = _dot(prob, vv)
            for r in range(ATT_GROUP):
                hh = kv * ATT_GROUP + r
                o_ref[:, hh * hd:(hh + 1) * hd] = o8[r * blk:(r + 1) * blk].astype(o_ref.dtype)

    return pl.pallas_call(
        body, name=nm, grid=(nb,), in_specs=_attn_specs(nb),
        out_specs=pl.BlockSpec((blk, D_MODEL), lambda i: (i, 0)),
        out_shape=jax.ShapeDtypeStruct((t_dim, D_MODEL), BF16),
        compiler_params=_cparams("arbitrary"),
    )(q_raw, k, k, v, v, q_norm, bias, sink_col)


def attn_bwd(do, q_raw, k, v, q_norm, bias, sink_col, nm):
    t_dim = q_raw.shape[0]
    blk, hd = ATT_WINDOW, ATT_HEAD_DIM
    nb = t_dim // blk
    scale = hd ** -0.5

    def body(do_ref, q_ref, kp_ref, kc_ref, vp_ref, vc_ref, qn_ref, bias_ref, sink_ref,
             dq_ref, dkc_ref, dkp_ref, dvc_ref, dvp_ref, dbias_ref, dsink_ref, dqn_ref):
        @pl.when(pl.program_id(0) == 0)
        def _():
            dbias_ref[...] = jnp.zeros_like(dbias_ref)
            dsink_ref[...] = jnp.zeros_like(dsink_ref)
            dqn_ref[...] = jnp.zeros_like(dqn_ref)

        gq = qn_ref[...]
        for kv in range(ATT_KV_HEADS):
            kk, vv, xh, rinv, q8, prob, p_sink = _attn_scores(q_ref, kp_ref, kc_ref, vp_ref, vc_ref, qn_ref,
                                                              bias_ref, sink_ref, kv)
            do8 = jnp.concatenate([do_ref[:, (kv * ATT_GROUP + r) * hd:(kv * ATT_GROUP + r + 1) * hd]
                                   for r in range(ATT_GROUP)], axis=0)
            dp = _dot(do8, vv, NT)
            delta = jnp.sum(prob * dp, axis=-1, keepdims=True)
            ds = prob * (dp - delta)
            dsink_ref[kv] += -p_sink * delta
            dbias_ref[kv] += ds
            ds_s = ds * scale
            dq8 = _dot(ds_s, kk)
            dkk = _dot(ds_s, q8, TN)
            dvv = _dot(prob, do8, TN)
            dqn_ref[...] += jnp.sum(dq8 * xh, axis=0, keepdims=True)
            dxh = dq8 * gq
            dq_raw8 = rinv * (dxh - xh * jnp.mean(dxh * xh, axis=-1, keepdims=True))
            for r in range(ATT_GROUP):
                hh = kv * ATT_GROUP + r
                dq_ref[:, hh * hd:(hh + 1) * hd] = dq_raw8[r * blk:(r + 1) * blk].astype(dq_ref.dtype)
            sl = slice(kv * hd, (kv + 1) * hd)
            dkp_ref[:, sl] = dkk[:blk]
            dkc_ref[:, sl] = dkk[blk:]
            dvp_ref[:, sl] = dvv[:blk]
            dvc_ref[:, sl] = dvv[blk:]

    cur = lambda i: (i, 0)
    row_spec = pl.BlockSpec((blk, KV_W), cur)
    out_specs = [
        pl.BlockSpec((blk, D_MODEL), cur), row_spec, row_spec, row_spec, row_spec,
        pl.BlockSpec((ATT_KV_HEADS, ATT_GROUP * blk, 2 * blk), lambda i: (0, 0, 0)),
        pl.BlockSpec((ATT_KV_HEADS, ATT_GROUP * blk, 1), lambda i: (0, 0, 0)),
        pl.BlockSpec((1, hd), lambda i: (0, 0)),
    ]
    kvs = jax.ShapeDtypeStruct((t_dim, KV_W), F32)
    out_shape = [
        jax.ShapeDtypeStruct((t_dim, D_MODEL), BF16), kvs, kvs, kvs, kvs,
        jax.ShapeDtypeStruct((ATT_KV_HEADS, ATT_GROUP * blk, 2 * blk), F32),
        jax.ShapeDtypeStruct((ATT_KV_HEADS, ATT_GROUP * blk, 1), F32),
        jax.ShapeDtypeStruct((1, hd), F32),
    ]
    return pl.pallas_call(
        body, name=nm, grid=(nb,), in_specs=[pl.BlockSpec((blk, D_MODEL), cur)] + _attn_specs(nb),
        out_specs=out_specs, out_shape=out_shape, compiler_params=_cparams("arbitrary"),
    )(do, q_raw, k, k, v, v, q_norm, bias, sink_col)


def _t5_bucket_np():
    blk = ATT_WINDOW
    qi = np.arange(blk)[:, None] + blk
    kj = np.arange(2 * blk)[None, :]
    dist = qi - kj
    n = np.maximum(dist, 0)
    max_exact = REL_BUCKETS // 2
    nf = np.maximum(n, 1).astype(np.float32)
    large = max_exact + (np.log(nf / max_exact) / math.log(ATT_WINDOW / max_exact)
                         * (REL_BUCKETS - max_exact)).astype(np.int32)
    large = np.minimum(large, REL_BUCKETS - 1)
    bucket = np.where(n < max_exact, n, large)
    in_window = (dist >= 0) & (dist < ATT_WINDOW)
    return bucket, in_window


def attn_block_fwd(h, k, v, p, nm):
    u, = rowmap(lambda x, gg: (_rms_fwd(x, gg),), [h], [p["attn_norm"]], [(D_MODEL, BF16)], tm=256, name=nm + "_norm")
    q_raw = mm(u, p["w_q"], name=nm + "_q")
    o = attn_fwd(q_raw, k, v, p["q_norm"], p["bias"], p["sink_col"], nm + "_core")
    out = mm(o, p["w_o"], res=h, name=nm + "_o")
    return out, (u, q_raw, o)


def attn_block_bwd(h, k, v, p, saved, dout, dout_bf, nm):
    u, q_raw, o = saved
    g = {}
    g["w_o"] = mm(o, dout_bf, ta=True, name=nm + "_dwo")
    do = mm(dout_bf, p["w_o"], tb=True, name=nm + "_do")
    dq_raw, dkc, dkp, dvc, dvp, g["bias"], g["sink_col"], g["q_norm"] = attn_bwd(
        do, q_raw, k, v, p["q_norm"], p["bias"], p["sink_col"], nm + "_dcore")
    g["w_q"] = mm(u, dq_raw, ta=True, name=nm + "_dwq")
    du = mm(dq_raw, p["w_q"], tb=True, name=nm + "_du")
    dh, dh_bf, g["attn_norm"] = norm_bwd(h, p["attn_norm"], du, dout, nm + "_dnorm")
    return dh, dh_bf, g, (dkc, dkp, dvc, dvp)


def local_step(x, target, w):
    t_dim = x.shape[0]
    bucket, in_window = _t5_bucket_np()
    blk = ATT_WINDOW

    ffn_p = []
    for layer in range(2):
        for half in range(2):
            ffn_p.append(dict(g=w["ffn_norm"][layer, half][None, :],
                              w13=jnp.concatenate([w["ffn_w1"][layer, half], w["ffn_w3"][layer, half]], axis=1),
                              w2=w["ffn_w2"][layer, half]))

    a_neg = -jnp.exp(w["ssm_a_log"][0])
    w_in = w["ssm_w_in"][0]
    w_dt = w_in[:, SSM_D_INNER + SSM_CONV_DIM:]
    mp = dict(ssm_norm=w["ssm_norm"], w_z=w_in[:, :SSM_D_INNER],
              w_xbc=w_in[:, SSM_D_INNER:SSM_D_INNER + SSM_CONV_DIM], w_dt=w_dt,
              conv_w=w["ssm_conv_w"][0], conv_b=w["ssm_conv_b"], dt_bias=w["ssm_dt_bias"],
              a_row=a_neg[None, :], a_col=a_neg[:, None],
              dvec=jnp.repeat(w["ssm_d"][0], SSM_HEAD_DIM).reshape(SSM_HEADS // 2, LANES),
              gate_norm=w["ssm_gate_norm"], w_out=w["ssm_w_out"][0])

    rb = w["rel_bias"]
    onehot3 = (np.arange(REL_BUCKETS)[:, None, None] == bucket[None]).astype(np.float32)
    bias = jnp.einsum("bh,bqk->hqk", rb, onehot3, precision=lax.Precision.HIGHEST)
    bias = jnp.where(in_window[None], bias, NEG)
    bias = bias.reshape(ATT_KV_HEADS, ATT_GROUP * blk, 2 * blk)
    sink_col = jnp.repeat(w["sinks"][0], blk).reshape(ATT_KV_HEADS, ATT_GROUP * blk, 1)
    ap = dict(attn_norm=w["attn_norm"], w_q=w["w_q"][0], q_norm=w["q_norm"], bias=bias, sink_col=sink_col,
              w_o=w["w_o"][0])
    kp = dict(kv_norm=w["kv_norm"][None, :], w_kv=w["w_kv"], k_norm=w["k_norm"][None, :])

    h0 = x
    h0a, s_f00 = ffn_fwd(h0, ffn_p[0]["g"], ffn_p[0]["w13"], ffn_p[0]["w2"], "f00")
    h0b, s_m = mamba_fwd(h0a, mp, "ssm")
    h1, s_f01 = ffn_fwd(h0b, ffn_p[1]["g"], ffn_p[1]["w13"], ffn_p[1]["w2"], "f01")
    k, v, s_kv = kv_fwd(h1, kp, "kv")
    h1a, s_f10 = ffn_fwd(h1, ffn_p[2]["g"], ffn_p[2]["w13"], ffn_p[2]["w2"], "f10")
    h1b, s_a = attn_block_fwd(h1a, k, v, ap, "att")
    h2, s_f11 = ffn_fwd(h1b, ffn_p[3]["g"], ffn_p[3]["w13"], ffn_p[3]["w2"], "f11")

    def loss_fn(y, t):
        e = y - t
        d = e * (1.0 / D_MODEL)
        return d, d, jnp.sum(e * e, axis=0, keepdims=True)

    dh, dh_bf, sq = rowmap(loss_fn, [h2, target], [], [(D_MODEL, F32), (D_MODEL, BF16)], [(1, D_MODEL)], tm=256,
                           name="loss")
    loss_part = jnp.sum(sq) * (0.5 / D_MODEL)

    grads = {}
    fg = [None] * 4
    dh, dh_bf, *fg[3] = ffn_bwd(h1b, ffn_p[3]["g"], ffn_p[3]["w13"], ffn_p[3]["w2"], s_f11, dh, dh_bf, "f11")
    dh, dh_bf, ga, dkv = attn_block_bwd(h1a, k, v, ap, s_a, dh, dh_bf, "att")
    dh, dh_bf, *fg[2] = ffn_bwd(h1, ffn_p[2]["g"], ffn_p[2]["w13"], ffn_p[2]["w2"], s_f10, dh, dh_bf, "f10")
    dh, dh_bf, gk = kv_bwd(h1, kp, s_kv, *dkv, dh, "kv")
    dh, dh_bf, *fg[1] = ffn_bwd(h0b, ffn_p[1]["g"], ffn_p[1]["w13"], ffn_p[1]["w2"], s_f01, dh, dh_bf, "f01")
    dh, dh_bf, gm = mamba_bwd(h0a, mp, s_m, dh, dh_bf, "ssm")
    dh, dh_bf, *fg[0] = ffn_bwd(h0, ffn_p[0]["g"], ffn_p[0]["w13"], ffn_p[0]["w2"], s_f00, dh, dh_bf, "f00")
    grad_x = dh

    grads["ffn_norm"] = jnp.stack([f[0][0] for f in fg]).reshape(2, 2, D_MODEL)
    dw13 = jnp.stack([f[1] for f in fg]).reshape(2, 2, D_MODEL, 2 * D_FF)
    grads["ffn_w1"] = dw13[..., :D_FF]
    grads["ffn_w3"] = dw13[..., D_FF:]
    grads["ffn_w2"] = jnp.stack([f[2] for f in fg]).reshape(2, 2, D_FF, D_MODEL)
    grads["ssm_norm"] = gm["ssm_norm"]
    grads["ssm_w_in"] = jnp.concatenate([gm["w_z"], gm["w_xbc"], gm["w_dt"]], axis=1)[None]
    grads["ssm_conv_w"] = gm["conv_w"][None]
    grads["ssm_conv_b"] = gm["conv_b"]
    grads["ssm_dt_bias"] = gm["dt_bias"]
    grads["ssm_a_log"] = gm["a"] * a_neg[None, :]
    grads["ssm_d"] = jnp.sum(gm["dvec"].reshape(SSM_HEADS, SSM_HEAD_DIM), axis=1)[None, :]
    grads["ssm_gate_norm"] = gm["gate_norm"]
    grads["ssm_w_out"] = gm["w_out"][None]
    grads["kv_norm"] = gk["kv_norm"][0]
    grads["w_kv"] = gk["w_kv"]
    grads["k_norm"] = gk["k_norm"][0]
    grads["attn_norm"] = ga["attn_norm"]
    grads["w_q"] = ga["w_q"][None]
    grads["q_norm"] = ga["q_norm"]
    grads["sinks"] = jnp.sum(ga["sink_col"].reshape(ATT_HEADS, blk), axis=1)[None, :]
    grads["w_o"] = ga["w_o"][None]
    onehot = (np.arange(REL_BUCKETS)[:, None] == bucket.reshape(1, -1)).astype(np.float32)
    dbias2d = ga["bias"].reshape(ATT_HEADS, blk * 2 * blk)
    grads["rel_bias"] = mm(jnp.asarray(onehot, BF16), dbias2d, tb=True, name="drelbias")
    return loss_part, grad_x, grads


PACK_W = 1024
BIG = [
    ("ffn_w1", (2, 2, 1024, 352), 3), ("ffn_w3", (2, 2, 1024, 352), 3), ("ffn_w2", (2, 2, 352, 1024), 2),
    ("ssm_w_out", (1, 256, 1024), 1), ("w_kv", (128, 256), 0), ("w_q", (1, 128, 1024), 1),
    ("w_o", (1, 128, 1024), 1), ("ssm_w_in", (1, 1024, 644), 2),
]
BIG_ROWS = [int(np.prod(s)) // PACK_W for _, s, _ in BIG]
PACK_TM = 128
PACK_R = -(-sum(BIG_ROWS) // PACK_TM) * PACK_TM

SMALL = [
    ("ffn_norm", (2, 2, 1024), 2), ("ssm_norm", (1, 1024), 1), ("ssm_conv_w", (1, 4, 3072), 2),
    ("ssm_conv_b", (1, 3072), 1), ("ssm_gate_norm", (1, 2048), 1),
    ("ssm_dt_bias", (1, 32), None), ("ssm_a_log", (1, 32), None), ("ssm_d", (1, 32), None),
    ("kv_norm", (1024,), None), ("k_norm", (64,), None), ("attn_norm", (1, 1024), None),
    ("q_norm", (1, 64), None), ("sinks", (1, 16), None), ("rel_bias", (32, 16), None),
]
SMALL_FULL_ROWS = 32
SMALL_LOCAL_ROWS = 48


def _pack_rows(parts, rows, dtype):
    flat = jnp.concatenate([p.astype(dtype).reshape(-1) for p in parts])
    pad = rows * PACK_W - flat.shape[0]
    return jnp.pad(flat, (0, pad)).reshape(rows, PACK_W)


def _shard_view(full, shard_shape, axis):
    shp = full.shape
    new = shp[:axis] + (N_DEV, shp[axis] // N_DEV) + shp[axis + 1:]
    return jnp.moveaxis(full.reshape(new), axis, 0).reshape((N_DEV,) + tuple(shard_shape))


def _unshard_view(stack, shard_shape, axis):
    moved = jnp.moveaxis(stack, 0, axis)
    shp = shard_shape[:axis] + (N_DEV * shard_shape[axis],) + shard_shape[axis + 1:]
    return moved.reshape(shp)


def _peer(k):
    x, y, c = lax.axis_index("x"), lax.axis_index("y"), lax.axis_index("c")
    bx, by, bc = (k >> 2) & 1, (k >> 1) & 1, k & 1
    px = 1 - x if bx else x
    py = 1 - y if by else y
    pc = 1 - c if bc else c
    return (px, py, pc), 4 * px + 2 * py + pc


def _my_index():
    return 4 * lax.axis_index("x") + 2 * lax.axis_index("y") + lax.axis_index("c")


def all_gather_rows(buf, name):
    rows, width = buf.shape

    def body(x_ref, out_ref, send_sems, recv_sems, local_sem):
        me = _my_index()
        mine = pltpu.make_async_copy(x_ref, out_ref.at[me], local_sem)
        mine.start()
        sends = []
        for k in range(1, N_DEV):
            peer, _ = _peer(k)
            cp = pltpu.make_async_remote_copy(src_ref=x_ref, dst_ref=out_ref.at[me], send_sem=send_sems.at[k - 1],
                                              recv_sem=recv_sems.at[k - 1], device_id=peer, device_id_type=MESH_ID)
            cp.start()
            sends.append(cp)
        for k in range(1, N_DEV):
            peer, pidx = _peer(k)
            pltpu.make_async_remote_copy(src_ref=x_ref, dst_ref=out_ref.at[pidx], send_sem=send_sems.at[k - 1],
                                         recv_sem=recv_sems.at[k - 1], device_id=peer,
                                         device_id_type=MESH_ID).wait_recv()
        for cp in sends:
            cp.wait_send()
        mine.wait()

    return pl.pallas_call(
        body, name=name, out_shape=jax.ShapeDtypeStruct((N_DEV, rows, width), buf.dtype),
        in_specs=[pl.BlockSpec(memory_space=pl.ANY)], out_specs=pl.BlockSpec(memory_space=pl.ANY),
        scratch_shapes=[pltpu.SemaphoreType.DMA((N_DEV - 1,)), pltpu.SemaphoreType.DMA((N_DEV - 1,)),
                        pltpu.SemaphoreType.DMA],
    )(buf)


def all_to_all_rows(buf, name):
    _, rows, width = buf.shape

    def body(x_ref, out_ref, send_sems, recv_sems, local_sem):
        me = _my_index()
        mine = pltpu.make_async_copy(x_ref.at[me], out_ref.at[me], local_sem)
        mine.start()
        sends = []
        for k in range(1, N_DEV):
            peer, pidx = _peer(k)
            cp = pltpu.make_async_remote_copy(src_ref=x_ref.at[pidx], dst_ref=out_ref.at[me],
                                              send_sem=send_sems.at[k - 1], recv_sem=recv_sems.at[k - 1],
                                              device_id=peer, device_id_type=MESH_ID)
            cp.start()
            sends.append(cp)
        for k in range(1, N_DEV):
            peer, pidx = _peer(k)
            pltpu.make_async_remote_copy(src_ref=x_ref.at[me], dst_ref=out_ref.at[pidx], send_sem=send_sems.at[k - 1],
                                         recv_sem=recv_sems.at[k - 1], device_id=peer,
                                         device_id_type=MESH_ID).wait_recv()
        for cp in sends:
            cp.wait_send()
        mine.wait()

    return pl.pallas_call(
        body, name=name, out_shape=jax.ShapeDtypeStruct(buf.shape, buf.dtype),
        in_specs=[pl.BlockSpec(memory_space=pl.ANY)], out_specs=pl.BlockSpec(memory_space=pl.ANY),
        scratch_shapes=[pltpu.SemaphoreType.DMA((N_DEV - 1,)), pltpu.SemaphoreType.DMA((N_DEV - 1,)),
                        pltpu.SemaphoreType.DMA],
    )(buf)


def all_reduce_small(buf, name):
    rows, width = buf.shape

    def body(x_ref, out_ref, gath, send_sems, recv_sems):
        me = _my_index()
        sends = []
        for k in range(1, N_DEV):
            peer, _ = _peer(k)
            cp = pltpu.make_async_remote_copy(src_ref=x_ref, dst_ref=gath.at[me], send_sem=send_sems.at[k - 1],
                                              recv_sem=recv_sems.at[k - 1], device_id=peer, device_id_type=MESH_ID)
            cp.start()
            sends.append(cp)
        gath[me] = x_ref[...]
        for k in range(1, N_DEV):
            peer, pidx = _peer(k)
            pltpu.make_async_remote_copy(src_ref=x_ref, dst_ref=gath.at[pidx], send_sem=send_sems.at[k - 1],
                                         recv_sem=recv_sems.at[k - 1], device_id=peer,
                                         device_id_type=MESH_ID).wait_recv()
        for cp in sends:
            cp.wait_send()
        acc = gath[0]
        for d in range(1, N_DEV):
            acc = acc + gath[d]
        out_ref[...] = acc

    return pl.pallas_call(
        body, name=name, out_shape=jax.ShapeDtypeStruct((rows, width), F32),
        in_specs=[pl.BlockSpec(memory_space=pltpu.VMEM)], out_specs=pl.BlockSpec(memory_space=pltpu.VMEM),
        scratch_shapes=[pltpu.VMEM((N_DEV, rows, width), F32), pltpu.SemaphoreType.DMA((N_DEV - 1,)),
                        pltpu.SemaphoreType.DMA((N_DEV - 1,))],
    )(buf)


def _adamw(g, w, m, v):
    m = ADAM_B1 * m + (1.0 - ADAM_B1) * g
    v = ADAM_B2 * v + (1.0 - ADAM_B2) * (g * g)
    m_hat = m / (1.0 - ADAM_B1 ** ADAM_STEP)
    v_hat = v / (1.0 - ADAM_B2 ** ADAM_STEP)
    delta = -ADAM_LR * (m_hat / (jnp.sqrt(v_hat) + ADAM_EPS) + ADAM_WD * w)
    return delta, m, v


def adamw_big(recv, w, m, v):
    def fn(ww, r, mm_, vv):
        g = r[0].astype(F32)
        for d in range(1, N_DEV):
            g = g + r[d].astype(F32)
        delta, m2, v2 = _adamw(g, ww, mm_, vv)
        return g, delta, m2, v2

    spec = pl.BlockSpec((N_DEV, PACK_TM, PACK_W), lambda i: (0, i, 0))
    return rowmap(fn, [w, (recv, spec), m, v], [], [(PACK_W, F32)] * 4, tm=PACK_TM, name="adamw_big")


def adamw_small(g, w, m, v):
    def fn(gg, ww, mm_, vv):
        return _adamw(gg, ww, mm_, vv)

    return rowmap(fn, [g, w, m, v], [], [(LANES, F32)] * 3, tm=SMALL_LOCAL_ROWS, name="adamw_small")


WEIGHT_NAMES = ["ffn_norm", "ffn_w1", "ffn_w3", "ffn_w2", "ssm_norm", "ssm_w_in", "ssm_conv_w", "ssm_conv_b",
                "ssm_dt_bias", "ssm_a_log", "ssm_d", "ssm_gate_norm", "ssm_w_out", "kv_norm", "w_kv", "k_norm",
                "attn_norm", "w_q", "q_norm", "sinks", "w_o", "rel_bias"]


def _small_local(arrs):
    flat = jnp.concatenate([arrs[n].reshape(-1) for n, _, _ in SMALL])
    return jnp.pad(flat, (0, SMALL_LOCAL_ROWS * LANES - flat.shape[0])).reshape(SMALL_LOCAL_ROWS, LANES)


def step(x, target, wts, ms, vs):
    me = _my_index()

    send_w = _pack_rows([wts[n] for n, _, _ in BIG], PACK_R, BF16)
    gathered = all_gather_rows(send_w, "gather_weights")
    full = {}
    off = 0
    for (n, shp, axis), r in zip(BIG, BIG_ROWS):
        full[n] = _unshard_view(gathered[:, off:off + r].reshape((N_DEV,) + shp), shp, axis)
        off += r
    small_sharded = [(n, s, a) for n, s, a in SMALL if a is not None]
    loc = jnp.concatenate([wts[n].reshape(-1) for n, _, _ in small_sharded])
    loc_rows = -(-loc.shape[0] // (8 * LANES)) * 8
    loc = jnp.pad(loc, (0, loc_rows * LANES - loc.shape[0])).reshape(loc_rows, LANES)
    gath_small = all_gather_rows(loc, "gather_small").reshape(N_DEV, -1)
    off = 0
    for n, s, a in small_sharded:
        shard = s[:a] + (s[a] // N_DEV,) + s[a + 1:]
        cnt = int(np.prod(shard))
        full[n] = _unshard_view(gath_small[:, off:off + cnt].reshape((N_DEV,) + shard), shard, a)
        off += cnt
    for n, s, a in SMALL:
        if a is None:
            full[n] = wts[n]

    loss_part, grad_x, grads = local_step(x[0], target[0], full)
    loss = lax.psum(loss_part, ("x", "y", "c"))

    parts = [_shard_view(grads[n], shp, axis).astype(BF16).reshape(N_DEV, -1, PACK_W) for n, shp, axis in BIG]
    parts.append(jnp.zeros((N_DEV, PACK_R - sum(BIG_ROWS), PACK_W), BF16))
    send_g = jnp.concatenate(parts, axis=1)
    recv_g = all_to_all_rows(send_g, "exchange_grads")
    small_flat = jnp.concatenate([grads[n].reshape(-1) for n, _, _ in SMALL])
    small_buf = jnp.pad(small_flat, (0, SMALL_FULL_ROWS * PACK_W - small_flat.shape[0])).reshape(SMALL_FULL_ROWS, PACK_W)
    small_sum = all_reduce_small(small_buf, "reduce_small").reshape(-1)
    g_small = {}
    off = 0
    for n, s, a in SMALL:
        cnt = int(np.prod(s))
        gfull = small_sum[off:off + cnt].reshape(s)
        off += cnt
        if a is None:
            g_small[n] = gfull
        else:
            width = s[a] // N_DEV
            g_small[n] = lax.dynamic_slice_in_dim(gfull, me * width, width, axis=a)

    out = {}
    w_pack = _pack_rows([wts[n] for n, _, _ in BIG], PACK_R, F32)
    m_pack = _pack_rows([ms[n] for n, _, _ in BIG], PACK_R, F32)
    v_pack = _pack_rows([vs[n] for n, _, _ in BIG], PACK_R, F32)
    res = adamw_big(recv_g, w_pack, m_pack, v_pack)
    off = 0
    for (n, shp, _), r in zip(BIG, BIG_ROWS):
        for kind, arr in zip(("grad", "delta", "new_m", "new_v"), res):
            out[kind + "_" + n] = arr[off:off + r].reshape(shp)
        off += r
    res_s = adamw_small(_small_local(g_small), _small_local(wts), _small_local(ms), _small_local(vs))
    flat_s = [g_small] + [r.reshape(-1) for r in res_s]
    off = 0
    for n, s, a in SMALL:
        shard = s if a is None else s[:a] + (s[a] // N_DEV,) + s[a + 1:]
        cnt = int(np.prod(shard))
        out["grad_" + n] = g_small[n]
        for kind, arr in zip(("delta", "new_m", "new_v"), flat_s[1:]):
            out[kind + "_" + n] = arr[off:off + cnt].reshape(shard)
        off += cnt
    out["loss"] = loss
    out["grad_x"] = grad_x[None]
    return out


def kernel(x, ffn_norm, ffn_w1, ffn_w3, ffn_w2, ssm_norm, ssm_w_in, ssm_conv_w, ssm_conv_b, ssm_dt_bias, ssm_a_log, ssm_d, ssm_gate_norm, ssm_w_out, kv_norm, w_kv, k_norm, attn_norm, w_q, q_norm, sinks, w_o, rel_bias, loss_target, m_ffn_norm, m_ffn_w1, m_ffn_w3, m_ffn_w2, m_ssm_norm, m_ssm_w_in, m_ssm_conv_w, m_ssm_conv_b, m_ssm_dt_bias, m_ssm_a_log, m_ssm_d, m_ssm_gate_norm, m_ssm_w_out, m_kv_norm, m_w_kv, m_k_norm, m_attn_norm, m_w_q, m_q_norm, m_sinks, m_w_o, m_rel_bias, v_ffn_norm, v_ffn_w1, v_ffn_w3, v_ffn_w2, v_ssm_norm, v_ssm_w_in, v_ssm_conv_w, v_ssm_conv_b, v_ssm_dt_bias, v_ssm_a_log, v_ssm_d, v_ssm_gate_norm, v_ssm_w_out, v_kv_norm, v_w_kv, v_k_norm, v_attn_norm, v_w_q, v_q_norm, v_sinks, v_w_o, v_rel_bias):
    args = locals()
    wts = {n: args[n] for n in WEIGHT_NAMES}
    ms = {n: args["m_" + n] for n in WEIGHT_NAMES}
    vs = {n: args["v_" + n] for n in WEIGHT_NAMES}
    out = step(x, loss_target, wts, ms, vs)
    result = [out["loss"], out["grad_x"]]
    for kind in ("grad", "delta", "new_m", "new_v"):
        result += [out[kind + "_" + n] for n in WEIGHT_NAMES]
    return tuple(result)
```

```python
import functools
import math
import operator

import numpy as np
import jax
import jax.numpy as jnp
from jax import lax
from jax.experimental import pallas as pl
from jax.experimental.pallas import tpu as pltpu

F32 = jnp.float32
BF16 = jnp.bfloat16

D_MODEL = 1024
D_FF = 2816
N_DEV = 8
SSM_D_INNER = 2048
SSM_HEAD_DIM = 64
SSM_HEADS = 32
SSM_GROUPS = 4
SSM_STATE = 128
SSM_CONV = 4
SSM_CHUNK = 256
SSM_CONV_DIM = SSM_D_INNER + 2 * SSM_GROUPS * SSM_STATE
SSM_IN_DIM = SSM_D_INNER + SSM_CONV_DIM + SSM_HEADS
ATT_HEAD_DIM = 64
ATT_HEADS = 16
ATT_KV_HEADS = 2
ATT_GROUP = 8
ATT_WINDOW = 128
REL_BUCKETS = 32
EPS = 1e-6
NEG = -1e30

ADAM_LR = 0.001
ADAM_B1 = 0.9
ADAM_B2 = 0.999
ADAM_EPS = 1e-08
ADAM_WD = 0.01
ADAM_STEP = 10

VMEM_LIMIT_BYTES = 52 * 1024 * 1024
LANES = 128
MESH_ID = pl.DeviceIdType.MESH
ANY_SPEC = pl.BlockSpec(memory_space=pl.ANY)

NT = (((1,), (1,)), ((), ()))
TN = (((0,), (0,)), ((), ()))
NN = (((1,), (0,)), ((), ()))


def _pick(dim, cands):
    for c in cands:
        if dim % c == 0:
            return c
    return dim


def _my_index():
    return 4 * lax.axis_index("x") + 2 * lax.axis_index("y") + lax.axis_index("c")


def _peer(k):
    x, y, c = lax.axis_index("x"), lax.axis_index("y"), lax.axis_index("c")
    px = 1 - x if (k >> 2) & 1 else x
    py = 1 - y if (k >> 1) & 1 else y
    pc = 1 - c if k & 1 else c
    return (px, py, pc), 4 * px + 2 * py + pc


def _piece(ref, axis, d, n):
    if axis is None:
        return ref.at[d]
    return ref.at[(slice(None),) * axis + (pl.ds(pl.multiple_of(d * n, 8), n),)]


class Comm:
    def __init__(self, items):
        self.items = list(items)

    def dst_shapes(self):
        out = []
        for kind, src, axis in self.items:
            s = tuple(src.shape)
            if kind == "g":
                shp = (N_DEV,) + s if axis is None else s[:axis] + (N_DEV * s[axis],) + s[axis + 1:]
            else:
                shp = s if axis is None else (N_DEV,) + s[:axis] + (s[axis] // N_DEV,) + s[axis + 1:]
            out.append(jax.ShapeDtypeStruct(shp, src.dtype))
        return out

    def scratch(self):
        n = len(self.items)
        return [pltpu.SemaphoreType.DMA((n * (N_DEV - 1),)), pltpu.SemaphoreType.DMA((n * (N_DEV - 1),)),
                pltpu.SemaphoreType.DMA((n,))]

    def _copies(self, srcs, dsts, sems, with_recvs=True):
        send_sems, recv_sems, local_sems = sems
        me = _my_index()
        local, sends, recvs = [], [], []
        for i, (kind, src, axis) in enumerate(self.items):
            s_ref, d_ref = srcs[i], dsts[i]
            if kind == "g":
                n = None if axis is None else src.shape[axis]
                local.append(pltpu.make_async_copy(s_ref, _piece(d_ref, axis, me, n), local_sems.at[i]))
            else:
                n = None if axis is None else src.shape[axis] // N_DEV
                local.append(pltpu.make_async_copy(_piece(s_ref, axis, me, n), d_ref.at[me], local_sems.at[i]))
            for k in range(1, N_DEV):
                peer, pidx = _peer(k)
                j = i * (N_DEV - 1) + k - 1
                if kind == "g":
                    out_src, out_dst, in_dst = s_ref, _piece(d_ref, axis, me, n), _piece(d_ref, axis, pidx, n)
                else:
                    out_src, out_dst, in_dst = _piece(s_ref, axis, pidx, n), d_ref.at[me], d_ref.at[pidx]
                sends.append(pltpu.make_async_remote_copy(
                    src_ref=out_src, dst_ref=out_dst, send_sem=send_sems.at[j], recv_sem=recv_sems.at[j],
                    device_id=peer, device_id_type=MESH_ID))
                if with_recvs:
                    recvs.append(pltpu.make_async_remote_copy(
                        src_ref=out_src, dst_ref=in_dst, send_sem=send_sems.at[j], recv_sem=recv_sems.at[j],
                        device_id=peer, device_id_type=MESH_ID))
        return local, sends, recvs

    def start(self, srcs, dsts, sems):
        local, sends, _ = self._copies(srcs, dsts, sems, with_recvs=False)
        for cp in local + sends:
            cp.start()

    def wait(self, srcs, dsts, sems):
        local, sends, recvs = self._copies(srcs, dsts, sems)
        for cp in recvs:
            cp.wait_recv()
        for cp in sends:
            cp.wait_send()
        for cp in local:
            cp.wait()


def pcall(body, *, name, grid, in_specs, out_specs, out_shape, args, scratch=(), hook=None):
    cparams = pltpu.CompilerParams(dimension_semantics=("arbitrary",) * len(grid), vmem_limit_bytes=VMEM_LIMIT_BYTES)
    if hook is None:
        outs = pl.pallas_call(body, name=name, grid=grid, in_specs=list(in_specs), out_specs=list(out_specs),
                              out_shape=list(out_shape), scratch_shapes=list(scratch), compiler_params=cparams)(*args)
        return list(outs)
    comm, sink = hook
    n_in, n_out, n_scr, n_it = len(args), len(out_shape), len(scratch), len(comm.items)
    dims = tuple(grid)

    def wrapped(*refs):
        p = 0
        ins = refs[p:p + n_in]
        p += n_in
        csrc = refs[p:p + n_it]
        p += n_it
        outs = refs[p:p + n_out]
        p += n_out
        cdst = refs[p:p + n_it]
        p += n_it
        scr = refs[p:p + n_scr]
        p += n_scr
        sems = refs[p:p + 3]
        if dims:
            ids = [pl.program_id(a) for a in range(len(dims))]
            first = functools.reduce(operator.and_, [i == 0 for i in ids])
            last = functools.reduce(operator.and_, [i == d - 1 for i, d in zip(ids, dims)])

            @pl.when(first)
            def _():
                comm.start(csrc, cdst, sems)

            body(*ins, *outs, *scr)

            @pl.when(last)
            def _():
                comm.wait(csrc, cdst, sems)
        else:
            comm.start(csrc, cdst, sems)
            body(*ins, *outs, *scr)
            comm.wait(csrc, cdst, sems)

    res = pl.pallas_call(
        wrapped, name=name, grid=grid,
        in_specs=list(in_specs) + [ANY_SPEC] * n_it, out_specs=list(out_specs) + [ANY_SPEC] * n_it,
        out_shape=list(out_shape) + comm.dst_shapes(), scratch_shapes=list(scratch) + comm.scratch(),
        compiler_params=cparams,
    )(*args, *[src for _, src, _ in comm.items])
    res = list(res)
    sink(res[n_out:])
    return res[:n_out]


def comm_only(comm, name):
    got = []
    pcall(lambda *refs: None, name=name, grid=(), in_specs=[], out_specs=[], out_shape=[], args=[],
          hook=(comm, got.extend))
    return got


def mm(a, b, *, ta=False, tb=False, out_dtype=F32, res=None, alpha=1.0, name, hook=None):
    if ta:
        k_dim, m_dim = a.shape
    else:
        m_dim, k_dim = a.shape
    if tb:
        n_dim, k2 = b.shape
    else:
        k2, n_dim = b.shape
    assert k_dim == k2, (a.shape, b.shape, ta, tb)
    tn = _pick(n_dim, (1024, 1408, 512, 256, 128))
    tm = _pick(m_dim, (1024, 1408, 512, 256, 128)) if tn <= 1024 else _pick(m_dim, (512, 256, 128))
    tk = _pick(k_dim, (512, 1408, 256, 128))
    nk = k_dim // tk
    has_res = res is not None
    dn = (((0 if ta else 1,), (1 if tb else 0,)), ((), ()))

    def body(*refs):
        if has_res:
            a_ref, b_ref, r_ref, o_ref, acc_ref = refs
        else:
            a_ref, b_ref, o_ref, acc_ref = refs
        k = pl.program_id(2)

        @pl.when(k == 0)
        def _():
            acc_ref[...] = jnp.zeros_like(acc_ref)

        acc_ref[...] += lax.dot_general(a_ref[...].astype(BF16), b_ref[...].astype(BF16), dn,
                                        preferred_element_type=F32)

        @pl.when(k == nk - 1)
        def _():
            r = acc_ref[...]
            if alpha != 1.0:
                r = r * alpha
            if has_res:
                r = r_ref[...] + r
            o_ref[...] = r.astype(o_ref.dtype)

    a_spec = pl.BlockSpec((tk, tm), lambda i, j, k: (k, i)) if ta else pl.BlockSpec((tm, tk), lambda i, j, k: (i, k))
    b_spec = pl.BlockSpec((tn, tk), lambda i, j, k: (j, k)) if tb else pl.BlockSpec((tk, tn), lambda i, j, k: (k, j))
    o_spec = pl.BlockSpec((tm, tn), lambda i, j, k: (i, j))
    in_specs = [a_spec, b_spec] + ([o_spec] if has_res else [])
    args = [a, b] + ([res] if has_res else [])
    out, = pcall(body, name=name, grid=(m_dim // tm, n_dim // tn, nk), in_specs=in_specs, out_specs=[o_spec],
                 out_shape=[jax.ShapeDtypeStruct((m_dim, n_dim), out_dtype)], args=args,
                 scratch=[pltpu.VMEM((tm, tn), F32)], hook=hook)
    return out


def rowmap(fn, rows, consts=(), out_rows=(), out_accs=(), *, tm, name, hook=None):
    first = rows[0][0] if isinstance(rows[0], tuple) else rows[0]
    t_dim = first.shape[0]
    assert t_dim % tm == 0, (t_dim, tm)
    n_r, n_c, n_o = len(rows), len(consts), len(out_rows)

    def body(*refs):
        ins = [r[...] for r in refs[:n_r + n_c]]
        o_refs = refs[n_r + n_c:]
        outs = tuple(fn(*ins))
        for o_ref, val in zip(o_refs[:n_o], outs[:n_o]):
            o_ref[...] = val.astype(o_ref.dtype)
        if out_accs:
            @pl.when(pl.program_id(0) == 0)
            def _():
                for o_ref in o_refs[n_o:]:
                    o_ref[...] = jnp.zeros_like(o_ref)

            for o_ref, val in zip(o_refs[n_o:], outs[n_o:]):
                o_ref[...] += val

    in_specs, args = [], []
    for r in rows:
        if isinstance(r, tuple):
            args.append(r[0])
            in_specs.append(r[1])
        else:
            args.append(r)
            in_specs.append(pl.BlockSpec((tm, r.shape[1]), lambda i: (i, 0)))
    for c in consts:
        args.append(c)
        in_specs.append(pl.BlockSpec(c.shape, lambda i, nd=c.ndim: (0,) * nd))
    out_specs = [pl.BlockSpec((tm, w), lambda i: (i, 0)) for (w, _) in out_rows]
    out_specs += [pl.BlockSpec(s, lambda i, nd=len(s): (0,) * nd) for s in out_accs]
    out_shape = [jax.ShapeDtypeStruct((t_dim, w), dt) for (w, dt) in out_rows]
    out_shape += [jax.ShapeDtypeStruct(s, F32) for s in out_accs]
    return pcall(body, name=name, grid=(t_dim // tm,), in_specs=in_specs, out_specs=out_specs, out_shape=out_shape,
                 args=args, hook=hook)


def _rms_fwd(x, g):
    r = lax.rsqrt(jnp.mean(x * x, axis=-1, keepdims=True) + EPS)
    return x * r * g


def _rms_bwd(x, g, dy):
    r = lax.rsqrt(jnp.mean(x * x, axis=-1, keepdims=True) + EPS)
    xh = x * r
    dg = jnp.sum(dy * xh, axis=0, keepdims=True)
    dxh = dy * g
    dx = r * (dxh - xh * jnp.mean(dxh * xh, axis=-1, keepdims=True))
    return dx, dg


def _sigmoid(x):
    return 1.0 / (1.0 + jnp.exp(-x))


def _silu(x):
    return x * _sigmoid(x)


def _silu_grad(x):
    s = _sigmoid(x)
    return s * (1.0 + x * (1.0 - s))


def _split3(x):
    hi = x.astype(BF16)
    r1 = x - hi.astype(F32)
    mid = r1.astype(BF16)
    lo = (r1 - mid.astype(F32)).astype(BF16)
    return hi, mid, lo


def _dot(a, b, dn=NN):
    return lax.dot_general(a.astype(BF16), b.astype(BF16), dn, preferred_element_type=F32)


def _col_of(mat, h):
    lane = lax.broadcasted_iota(jnp.int32, mat.shape, 1)
    return jnp.sum(jnp.where(lane == h, mat, 0.0), axis=1, keepdims=True)


FFN_TN = 1408


def ffn_upgate(u, w1t, w3t, nm, hook=None):
    t_dim = u.shape[0]
    tm = _pick(t_dim, (512, 256, 128))
    tn = FFN_TN

    def body(u_ref, w1_ref, w3_ref, a_ref, b_ref, hm_ref):
        uu = u_ref[...]
        a = lax.dot_general(uu, w1_ref[...], NT, preferred_element_type=F32)
        b = lax.dot_general(uu, w3_ref[...], NT, preferred_element_type=F32)
        a_ref[...] = a.astype(a_ref.dtype)
        b_ref[...] = b.astype(b_ref.dtype)
        hm_ref[...] = (_silu(a) * b).astype(hm_ref.dtype)

    w_spec = pl.BlockSpec((tn, D_MODEL), lambda j, i: (j, 0))
    o_spec = pl.BlockSpec((tm, tn), lambda j, i: (i, j))
    o_shape = jax.ShapeDtypeStruct((t_dim, D_FF), BF16)
    return pcall(body, name=nm, grid=(D_FF // tn, t_dim // tm),
                 in_specs=[pl.BlockSpec((tm, D_MODEL), lambda j, i: (i, 0)), w_spec, w_spec],
                 out_specs=[o_spec] * 3, out_shape=[o_shape] * 3, args=[u, w1t, w3t], hook=hook)


def ffn_dgate(dout_bf, w2, a, b, nm, hook=None):
    t_dim = dout_bf.shape[0]
    tm = _pick(t_dim, (512, 256, 128))
    tn = FFN_TN

    def body(d_ref, w2_ref, a_ref, b_ref, da_ref, db_ref):
        dhm = 0.5 * lax.dot_general(d_ref[...], w2_ref[...], NT, preferred_element_type=F32)
        av = a_ref[...].astype(F32)
        bv = b_ref[...].astype(F32)
        da_ref[...] = (dhm * bv * _silu_grad(av)).astype(da_ref.dtype)
        db_ref[...] = (dhm * _silu(av)).astype(db_ref.dtype)

    t_spec = pl.BlockSpec((tm, tn), lambda j, i: (i, j))
    o_shape = jax.ShapeDtypeStruct((t_dim, D_FF), BF16)
    return pcall(body, name=nm, grid=(D_FF // tn, t_dim // tm),
                 in_specs=[pl.BlockSpec((tm, D_MODEL), lambda j, i: (i, 0)),
                           pl.BlockSpec((tn, D_MODEL), lambda j, i: (j, 0)), t_spec, t_spec],
                 out_specs=[t_spec] * 2, out_shape=[o_shape] * 2, args=[dout_bf, w2, a, b], hook=hook)


def ffn_fwd(h, g, w1t, w3t, w2, nm, hook=None):
    u, = rowmap(lambda x, gg: (_rms_fwd(x, gg),), [h], [g], [(D_MODEL, BF16)], tm=256, name=nm + "_norm")
    a, b, hm = ffn_upgate(u, w1t, w3t, nm + "_upgate", hook=hook)
    out = mm(hm, w2, res=h, alpha=0.5, name=nm + "_down")
    return out, (u, a, b, hm)


def norm_bwd(h, g, du, dout, nm):
    def fn(x, d_u, d_o, gg):
        dx, dg = _rms_bwd(x, gg, d_u)
        dh = d_o + dx
        return dh, dh, dg

    return rowmap(fn, [h, du, dout], [g], [(D_MODEL, F32), (D_MODEL, BF16)], [(1, D_MODEL)], tm=256, name=nm)


def ffn_bwd(h, g, w1t, w3t, w2, saved, dout, dout_bf, nm, hook=None):
    u, a, b, hm = saved
    dw2 = mm(hm, dout_bf, ta=True, alpha=0.5, out_dtype=BF16, name=nm + "_dw2")
    da, db = ffn_dgate(dout_bf, w2, a, b, nm + "_dgate", hook=hook)
    dw1t = mm(da, u, ta=True, out_dtype=BF16, name=nm + "_dw1")
    dw3t = mm(db, u, ta=True, out_dtype=BF16, name=nm + "_dw3")
    du = mm(da, w1t, name=nm + "_du1")
    du = mm(db, w3t, res=du, name=nm + "_du2")
    dh, dh_bf, dg = norm_bwd(h, g, du, dout, nm + "_dnorm")
    return dh, dh_bf, dg, dw1t, dw3t, dw2


def _conv_pre(x, halo, w, b, tm):
    halo = jnp.where(pl.program_id(0) > 0, halo, 0.0)
    xx = jnp.concatenate([halo, x], axis=0)
    shifted = [xx[5 + k:5 + k + tm] for k in range(SSM_CONV)]
    acc = b + shifted[0] * w[0:1]
    for k in range(1, SSM_CONV):
        acc = acc + shifted[k] * w[k:k + 1]
    return acc, shifted


def _prev_halo_spec(tm, width):
    return pl.BlockSpec((8, width), lambda i: (jnp.maximum(i * (tm // 8) - 1, 0), 0))


def conv_fwd(xbc_raw, w, b, nm):
    tm = 128

    def fn(x, halo, ww, bb):
        acc, _ = _conv_pre(x, halo, ww, bb, tm)
        return (_silu(acc),)

    out, = rowmap(fn, [xbc_raw, (xbc_raw, _prev_halo_spec(tm, SSM_CONV_DIM))], [w, b],
                  [(SSM_CONV_DIM, F32)], tm=tm, name=nm)
    return out


def conv_bwd(xbc_raw, w, b, dxs, db_in, dc_in, nm):
    tm = 128
    t_dim = xbc_raw.shape[0]

    def fn1(x, halo, d1, d2, d3, ww, bb):
        acc, shifted = _conv_pre(x, halo, ww, bb, tm)
        dacc = jnp.concatenate([d1, d2, d3], axis=1) * _silu_grad(acc)
        dw = jnp.concatenate([jnp.sum(dacc * s, axis=0, keepdims=True) for s in shifted], axis=0)
        return dacc, dw, jnp.sum(dacc, axis=0, keepdims=True)

    dacc, dw, dbias = rowmap(fn1, [xbc_raw, (xbc_raw, _prev_halo_spec(tm, SSM_CONV_DIM)), dxs, db_in, dc_in],
                             [w, b], [(SSM_CONV_DIM, F32)], [(SSM_CONV, SSM_CONV_DIM), (1, SSM_CONV_DIM)],
                             tm=tm, name=nm + "_a")
    n_tiles = t_dim // tm

    def fn2(d, nxt, ww):
        nxt = jnp.where(pl.program_id(0) < n_tiles - 1, nxt, 0.0)
        dd = jnp.concatenate([d, nxt], axis=0)
        out = dd[3:3 + tm] * ww[0:1]
        for k in range(1, SSM_CONV):
            out = out + dd[3 - k:3 - k + tm] * ww[k:k + 1]
        return (out,)

    nxt_spec = pl.BlockSpec((8, SSM_CONV_DIM), lambda i: (jnp.minimum((i + 1) * (tm // 8), t_dim // 8 - 1), 0))
    dx, = rowmap(fn2, [dacc, (dacc, nxt_spec)], [w], [(SSM_CONV_DIM, BF16)], tm=tm, name=nm + "_b")
    return dx, dw, dbias


def _ssd_cumsums(dt_ref, dtT_ref, arow_ref, acol_ref, acol_s, arowT_s):
    L = SSM_CHUNK
    r = lax.broadcasted_iota(jnp.int32, (L, L), 0)
    c = lax.broadcasted_iota(jnp.int32, (L, L), 1)
    tril = (r >= c).astype(BF16)
    triu = (r <= c).astype(BF16)
    dta = dt_ref[...] * arow_ref[...]
    acc = None
    for p in _split3(dta):
        t = jnp.dot(tril, p, preferred_element_type=F32)
        acc = t if acc is None else acc + t
    acol_s[...] = acc
    dtaT = dtT_ref[...] * acol_ref[...]
    acc = None
    for p in _split3(dtaT):
        t = jnp.dot(p, triu, preferred_element_type=F32)
        acc = t if acc is None else acc + t
    arowT_s[...] = acc


def _ssd_head_terms(h, cb, acol_s, arowT_s, dt_ref, dtT_ref):
    L = SSM_CHUNK
    r = lax.broadcasted_iota(jnp.int32, (L, L), 0)
    c = lax.broadcasted_iota(jnp.int32, (L, L), 1)
    a_col = _col_of(acol_s[...], h)
    a_row = arowT_s[pl.ds(h, 1), :]
    dt_col = _col_of(dt_ref[...], h)
    dt_row = dtT_ref[pl.ds(h, 1), :]
    a_last = a_col[L - 1:L, :]
    lm = jnp.exp(jnp.where(r >= c, a_col - a_row, NEG))
    m = cb * lm * dt_row
    e_a = jnp.exp(a_col)
    e_w = jnp.exp(a_last - a_col)
    return a_col, a_last, dt_col, dt_row, lm, m, e_a, e_w


def ssd_fwd(xbc, dt, dtT, a_row, a_col, dvec, nm, hook=None):
    t_dim = xbc.shape[0]
    L, P, N, H = SSM_CHUNK, SSM_HEAD_DIM, SSM_STATE, SSM_HEADS
    nc = t_dim // L
    n_pairs = H // 2
    xcols = SSM_D_INNER // LANES

    def body(x_ref, b_ref, c_ref, dt_ref, dtT_ref, arow_ref, acol_ref, dvec_ref, y_ref, st_ref,
             s_s, cb_s, acol_s, arowT_s):
        ci = pl.program_id(0)
        hp = pl.program_id(1)

        @pl.when((ci == 0) & (hp == 0))
        def _():
            s_s[...] = jnp.zeros_like(s_s)

        @pl.when(hp == 0)
        def _():
            _ssd_cumsums(dt_ref, dtT_ref, arow_ref, acol_ref, acol_s, arowT_s)

        @pl.when(hp % 4 == 0)
        def _():
            cb_s[...] = _dot(c_ref[...], b_ref[...], NT)

        bmat = b_ref[...]
        cmat = c_ref[...]
        cb = cb_s[...]
        dv = dvec_ref[pl.ds(hp, 1), :]
        for e in range(2):
            h = hp * 2 + e
            x = x_ref[:, e * P:(e + 1) * P]
            a_c, a_last, dt_col, dt_row, lm, m, e_a, e_w = _ssd_head_terms(h, cb, acol_s, arowT_s, dt_ref, dtT_ref)
            s = s_s[h]
            st_ref[0, e] = s
            y = _dot(m, x) + e_a * _dot(cmat, s, NT) + dv[:, e * P:(e + 1) * P] * x
            y_ref[:, e * P:(e + 1) * P] = y
            u = x * (e_w * dt_col)
            s_s[h] = jnp.exp(a_last) * s + _dot(u, bmat, TN)

    in_specs = [
        pl.BlockSpec((L, LANES), lambda c, p: (c, p)),
        pl.BlockSpec((L, LANES), lambda c, p: (c, xcols + p // 4)),
        pl.BlockSpec((L, LANES), lambda c, p: (c, xcols + SSM_GROUPS + p // 4)),
        pl.BlockSpec((L, H), lambda c, p: (c, 0)),
        pl.BlockSpec((H, L), lambda c, p: (0, c)),
        pl.BlockSpec((1, H), lambda c, p: (0, 0)),
        pl.BlockSpec((H, 1), lambda c, p: (0, 0)),
        pl.BlockSpec((n_pairs, LANES), lambda c, p: (0, 0)),
    ]
    out_specs = [
        pl.BlockSpec((L, LANES), lambda c, p: (c, p)),
        pl.BlockSpec((1, 2, P, N), lambda c, p: (c, p, 0, 0)),
    ]
    return pcall(
        body, name=nm, grid=(nc, n_pairs), in_specs=in_specs, out_specs=out_specs,
        out_shape=[jax.ShapeDtypeStruct((t_dim, SSM_D_INNER), F32), jax.ShapeDtypeStruct((nc, H, P, N), F32)],
        scratch=[pltpu.VMEM((H, P, N), F32), pltpu.VMEM((L, L), F32), pltpu.VMEM((L, H), F32),
                 pltpu.VMEM((H, L), F32)],
        args=[xbc, xbc, xbc, dt, dtT, a_row, a_col, dvec], hook=hook)


def ssd_bwd(dy, xbc, dt, dtT, a_row, a_col, dvec, states, nm, hook=None):
    t_dim = xbc.shape[0]
    L, P, N, H = SSM_CHUNK, SSM_HEAD_DIM, SSM_STATE, SSM_HEADS
    nc = t_dim // L
    n_pairs = H // 2
    xcols = SSM_D_INNER // LANES

    def body(dy_ref, x_ref, b_ref, c_ref, dt_ref, dtT_ref, arow_ref, acol_ref, dvec_ref, st_ref,
             dx_ref, db_ref, dc_ref, dacol_ref, darowT_ref, ddtcol_ref, ddtrowT_ref, dd_ref,
             ds_s, cb_s, dcb_s, acol_s, arowT_s):
        ci = pl.program_id(0)
        hp = pl.program_id(1)

        @pl.when((ci == 0) & (hp == 0))
        def _():
            ds_s[...] = jnp.zeros_like(ds_s)
            dd_ref[...] = jnp.zeros_like(dd_ref)

        @pl.when(hp == 0)
        def _():
            _ssd_cumsums(dt_ref, dtT_ref, arow_ref, acol_ref, acol_s, arowT_s)
            dacol_ref[...] = jnp.zeros_like(dacol_ref)
            ddtcol_ref[...] = jnp.zeros_like(ddtcol_ref)

        @pl.when(hp % 4 == 0)
        def _():
            cb_s[...] = _dot(c_ref[...], b_ref[...], NT)
            dcb_s[...] = jnp.zeros_like(dcb_s)
            db_ref[...] = jnp.zeros_like(db_ref)
            dc_ref[...] = jnp.zeros_like(dc_ref)

        bmat = b_ref[...]
        cmat = c_ref[...]
        cb = cb_s[...]
        dv = dvec_ref[pl.ds(hp, 1), :]
        lane_h = lax.broadcasted_iota(jnp.int32, (L, H), 1)
        row_l = lax.broadcasted_iota(jnp.int32, (L, 1), 0)
        dd_parts = []
        for e in range(2):
            h = hp * 2 + e
            x = x_ref[:, e * P:(e + 1) * P]
            dyh = dy_ref[:, e * P:(e + 1) * P]
            a_c, a_last, dt_col, dt_row, lm, m, e_a, e_w = _ssd_head_terms(h, cb, acol_s, arowT_s, dt_ref, dtT_ref)
            s = st_ref[0, e]
            dsp = ds_s[h]
            d_skip = dv[:, e * P:(e + 1) * P]
            dd_parts.append(jnp.sum(dyh * x, axis=0, keepdims=True))
            dx = d_skip * dyh + _dot(m, dyh, TN)
            dm = _dot(dyh, x, NT)
            q = dm * cb * lm
            ddt_row = jnp.sum(q, axis=0, keepdims=True)
            gmat = q * dt_row
            dcb_s[...] += dm * lm * dt_row
            da_col = jnp.sum(gmat, axis=1, keepdims=True)
            da_row = -jnp.sum(gmat, axis=0, keepdims=True)
            z = _dot(cmat, s, NT)
            dz = e_a * dyh
            da_col = da_col + e_a * jnp.sum(dyh * z, axis=1, keepdims=True)
            dc_ref[...] += _dot(dz, s)
            ds_y = _dot(dz, cmat, TN)
            e_al = jnp.exp(a_last)
            w_col = e_w * dt_col
            du = _dot(bmat, dsp, NT)
            db_ref[...] += _dot(x * w_col, dsp)
            dx = dx + du * w_col
            dw = jnp.sum(du * x, axis=1, keepdims=True)
            ddt_col = dw * e_w
            dwa = dw * w_col
            d_last = jnp.sum(dwa, keepdims=True) + e_al * jnp.sum(dsp * s, keepdims=True)
            da_col = da_col - dwa + jnp.where(row_l == L - 1, d_last, 0.0)
            ds_s[h] = e_al * dsp + ds_y
            dx_ref[:, e * P:(e + 1) * P] = dx
            dacol_ref[...] += jnp.where(lane_h == h, da_col, 0.0)
            ddtcol_ref[...] += jnp.where(lane_h == h, ddt_col, 0.0)
            darowT_ref[pl.ds(h, 1), :] = da_row
            ddtrowT_ref[pl.ds(h, 1), :] = ddt_row
        dd_ref[pl.ds(hp, 1), :] += jnp.concatenate(dd_parts, axis=1)

        @pl.when(hp % 4 == 3)
        def _():
            dcb = dcb_s[...]
            dc_ref[...] += _dot(dcb, bmat)
            db_ref[...] += _dot(dcb, cmat, TN)

    rc = lambda c: nc - 1 - c
    in_specs = [
        pl.BlockSpec((L, LANES), lambda c, p: (rc(c), p)),
        pl.BlockSpec((L, LANES), lambda c, p: (rc(c), p)),
        pl.BlockSpec((L, LANES), lambda c, p: (rc(c), xcols + p // 4)),
        pl.BlockSpec((L, LANES), lambda c, p: (rc(c), xcols + SSM_GROUPS + p // 4)),
        pl.BlockSpec((L, H), lambda c, p: (rc(c), 0)),
        pl.BlockSpec((H, L), lambda c, p: (0, rc(c))),
        pl.BlockSpec((1, H), lambda c, p: (0, 0)),
        pl.BlockSpec((H, 1), lambda c, p: (0, 0)),
        pl.BlockSpec((n_pairs, LANES), lambda c, p: (0, 0)),
        pl.BlockSpec((1, 2, P, N), lambda c, p: (rc(c), p, 0, 0)),
    ]
    out_specs = [
        pl.BlockSpec((L, LANES), lambda c, p: (rc(c), p)),
        pl.BlockSpec((L, LANES), lambda c, p: (rc(c), p // 4)),
        pl.BlockSpec((L, LANES), lambda c, p: (rc(c), p // 4)),
        pl.BlockSpec((L, H), lambda c, p: (rc(c), 0)),
        pl.BlockSpec((H, L), lambda c, p: (0, rc(c))),
        pl.BlockSpec((L, H), lambda c, p: (rc(c), 0)),
        pl.BlockSpec((H, L), lambda c, p: (0, rc(c))),
        pl.BlockSpec((n_pairs, LANES), lambda c, p: (0, 0)),
    ]
    gn = SSM_GROUPS * N
    out_shape = [
        jax.ShapeDtypeStruct((t_dim, SSM_D_INNER), F32), jax.ShapeDtypeStruct((t_dim, gn), F32),
        jax.ShapeDtypeStruct((t_dim, gn), F32), jax.ShapeDtypeStruct((t_dim, H), F32),
        jax.ShapeDtypeStruct((H, t_dim), F32), jax.ShapeDtypeStruct((t_dim, H), F32),
        jax.ShapeDtypeStruct((H, t_dim), F32), jax.ShapeDtypeStruct((n_pairs, LANES), F32),
    ]
    return pcall(
        body, name=nm, grid=(nc, n_pairs), in_specs=in_specs, out_specs=out_specs, out_shape=out_shape,
        scratch=[pltpu.VMEM((H, P, N), F32), pltpu.VMEM((L, L), F32), pltpu.VMEM((L, L), F32),
                 pltpu.VMEM((L, H), F32), pltpu.VMEM((H, L), F32)],
        args=[dy, xbc, xbc, xbc, dt, dtT, a_row, a_col, dvec, states], hook=hook)


def _softplus(x):
    return jnp.maximum(x, 0.0) + jnp.log(1.0 + jnp.exp(-jnp.abs(x)))


def ssd_dt_bwd(da, ddt, dt, dt_raw, a_row, dt_bias, nm):
    L = SSM_CHUNK

    def fn(d_a, d_dt, dtv, raw, ar, bias):
        r = lax.broadcasted_iota(jnp.int32, (L, L), 0)
        c = lax.broadcasted_iota(jnp.int32, (L, L), 1)
        triu = (r <= c).astype(BF16)
        acc = None
        for p in _split3(d_a):
            t = jnp.dot(triu, p, preferred_element_type=F32)
            acc = t if acc is None else acc + t
        d_dt = d_dt + acc * ar
        d_a_h = jnp.sum(acc * dtv, axis=0, keepdims=True)
        d_raw = d_dt * _sigmoid(raw + bias)
        return d_raw, d_a_h, jnp.sum(d_raw, axis=0, keepdims=True)

    return rowmap(fn, [da, ddt, dt, dt_raw], [a_row, dt_bias], [(SSM_HEADS, BF16)],
                  [(1, SSM_HEADS), (1, SSM_HEADS)], tm=L, name=nm)


GN_W = SSM_D_INNER // SSM_GROUPS


def mamba_fwd(h, p, nm, hook=None):
    u, = rowmap(lambda x, gg: (_rms_fwd(x, gg),), [h], [p["ssm_norm"]], [(D_MODEL, BF16)], tm=256, name=nm + "_norm")
    z = mm(u, p["w_zt"], tb=True, name=nm + "_z")
    xbc_raw = mm(u, p["w_xbct"], tb=True, name=nm + "_xbc")
    dt_raw = mm(u, p["w_dtt"], tb=True, name=nm + "_dt")
    xbc = conv_fwd(xbc_raw, p["conv_w"], p["conv_b"], nm + "_conv")
    dt, = rowmap(lambda r, b: (_softplus(r + b),), [dt_raw], [p["dt_bias"]], [(SSM_HEADS, F32)], tm=256,
                 name=nm + "_softplus")
    dtT = dt.T
    y, states = ssd_fwd(xbc, dt, dtT, p["a_row"], p["a_col"], p["dvec"], nm + "_ssd", hook=hook)

    def gate_norm(yv, zv, gg):
        t = yv * _silu(zv)
        return (jnp.concatenate([_rms_fwd(t[:, k * GN_W:(k + 1) * GN_W], gg[:, k * GN_W:(k + 1) * GN_W])
                                 for k in range(SSM_GROUPS)], axis=1),)

    yn, = rowmap(gate_norm, [y, z], [p["gate_norm"]], [(SSM_D_INNER, BF16)], tm=256, name=nm + "_gatenorm")
    out = mm(yn, p["w_out"], res=h, name=nm + "_out")
    return out, (u, z, xbc_raw, dt_raw, xbc, dt, dtT, y, states, yn)


def mamba_bwd(h, p, saved, dout, dout_bf, nm, hook=None):
    u, z, xbc_raw, dt_raw, xbc, dt, dtT, y, states, yn = saved
    g = {}
    g["w_out"] = mm(yn, dout_bf, ta=True, out_dtype=BF16, name=nm + "_dwout")
    dyn = mm(dout_bf, p["w_out"], tb=True, name=nm + "_dyn")

    def gate_norm_bwd(d, yv, zv, gg):
        sz = _silu(zv)
        t = yv * sz
        dts, dgs = [], []
        for k in range(SSM_GROUPS):
            sl = slice(k * GN_W, (k + 1) * GN_W)
            dt_k, dg_k = _rms_bwd(t[:, sl], gg[:, sl], d[:, sl])
            dts.append(dt_k)
            dgs.append(dg_k)
        d_t = jnp.concatenate(dts, axis=1)
        return d_t * sz, d_t * yv * _silu_grad(zv), jnp.concatenate(dgs, axis=1)

    dy, dz, g["gate_norm"] = rowmap(gate_norm_bwd, [dyn, y, z], [p["gate_norm"]],
                                    [(SSM_D_INNER, F32), (SSM_D_INNER, BF16)], [(1, SSM_D_INNER)], tm=256,
                                    name=nm + "_dgatenorm")
    dxs, db_in, dc_in, dacol, darowT, ddtcol, ddtrowT, dd = ssd_bwd(
        dy, xbc, dt, dtT, p["a_row"], p["a_col"], p["dvec"], states, nm + "_dssd", hook=hook)
    g["dvec"] = dd
    ddt_raw, g["a"], g["dt_bias"] = ssd_dt_bwd(dacol + darowT.T, ddtcol + ddtrowT.T, dt, dt_raw, p["a_row"],
                                               p["dt_bias"], nm + "_ddt")
    dxbc_raw, g["conv_w"], g["conv_b"] = conv_bwd(xbc_raw, p["conv_w"], p["conv_b"], dxs, db_in, dc_in, nm + "_dconv")
    g["w_zt"] = mm(dz, u, ta=True, out_dtype=BF16, name=nm + "_dwz")
    g["w_xbct"] = mm(dxbc_raw, u, ta=True, out_dtype=BF16, name=nm + "_dwxbc")
    g["w_dtt"] = mm(ddt_raw, u, ta=True, out_dtype=BF16, name=nm + "_dwdt")
    du = mm(dz, p["w_zt"], name=nm + "_du1")
    du = mm(dxbc_raw, p["w_xbct"], res=du, name=nm + "_du2")
    du = mm(ddt_raw, p["w_dtt"], res=du, name=nm + "_du3")
    dh, dh_bf, g["ssm_norm"] = norm_bwd(h, p["ssm_norm"], du, dout, nm + "_dnorm")
    return dh, dh_bf, g


KV_W = ATT_KV_HEADS * ATT_HEAD_DIM


def kv_fwd(h, p, nm):
    u, = rowmap(lambda x, gg: (_rms_fwd(x, gg),), [h], [p["kv_norm"]], [(D_MODEL, BF16)], tm=256, name=nm + "_norm")
    kv_raw = mm(u, p["w_kv"], name=nm + "_proj")

    def knorm(t, gg):
        ks = [_rms_fwd(t[:, j * ATT_HEAD_DIM:(j + 1) * ATT_HEAD_DIM], gg) for j in range(ATT_KV_HEADS)]
        return jnp.concatenate(ks, axis=1), t[:, KV_W:]

    k, v = rowmap(knorm, [kv_raw], [p["k_norm"]], [(KV_W, F32), (KV_W, F32)], tm=256, name=nm + "_knorm")
    return k, v, (u, kv_raw)


def kv_bwd(h, p, saved, dk_cur, dk_prev, dv_cur, dv_prev, dout, nm):
    u, kv_raw = saved
    t_dim = h.shape[0]
    tm = ATT_WINDOW
    nb = t_dim // tm
    nxt = pl.BlockSpec((tm, KV_W), lambda i: (jnp.minimum(i + 1, nb - 1), 0))

    def fn(dkc, dkp, dvc, dvp, t, gg):
        live = pl.program_id(0) < nb - 1
        dk = dkc + jnp.where(live, dkp, 0.0)
        dv = dvc + jnp.where(live, dvp, 0.0)
        outs, dgs = [], None
        for j in range(ATT_KV_HEADS):
            sl = slice(j * ATT_HEAD_DIM, (j + 1) * ATT_HEAD_DIM)
            dx, dg = _rms_bwd(t[:, sl], gg, dk[:, sl])
            outs.append(dx)
            dgs = dg if dgs is None else dgs + dg
        return jnp.concatenate(outs + [dv], axis=1), dgs

    dkv_raw, dknorm = rowmap(fn, [dk_cur, (dk_prev, nxt), dv_cur, (dv_prev, nxt), kv_raw], [p["k_norm"]],
                             [(2 * KV_W, BF16)], [(1, ATT_HEAD_DIM)], tm=tm, name=nm + "_dknorm")
    g = {"k_norm": dknorm}
    g["w_kv"] = mm(u, dkv_raw, ta=True, out_dtype=BF16, name=nm + "_dwkv")
    du = mm(dkv_raw, p["w_kv"], tb=True, name=nm + "_du")
    dh, dh_bf, g["kv_norm"] = norm_bwd(h, p["kv_norm"], du, dout, nm + "_dnorm")
    return dh, dh_bf, g


def _attn_scores(q_ref, kp_ref, kc_ref, vp_ref, vc_ref, qn_ref, bias_ref, sink_ref, kv):
    hd = ATT_HEAD_DIM
    blk = ATT_WINDOW
    sl = slice(kv * hd, (kv + 1) * hd)
    kk = jnp.concatenate([kp_ref[:, sl], kc_ref[:, sl]], axis=0)
    vv = jnp.concatenate([vp_ref[:, sl], vc_ref[:, sl]], axis=0)
    gq = qn_ref[...]
    raws, rinvs = [], []
    for r in range(ATT_GROUP):
        hh = kv * ATT_GROUP + r
        x = q_ref[:, hh * hd:(hh + 1) * hd]
        raws.append(x)
        rinvs.append(lax.rsqrt(jnp.mean(x * x, axis=-1, keepdims=True) + EPS))
    xh = jnp.concatenate([x * ri for x, ri in zip(raws, rinvs)], axis=0)
    rinv = jnp.concatenate(rinvs, axis=0)
    q8 = xh * gq
    s = _dot(q8, kk, NT) * (hd ** -0.5) + bias_ref[kv]
    colk = lax.broadcasted_iota(jnp.int32, (1, 2 * blk), 1)
    s = jnp.where((pl.program_id(0) > 0) | (colk >= blk), s, NEG)
    sink = sink_ref[kv]
    m = jnp.maximum(jnp.max(s, axis=-1, keepdims=True), sink)
    pexp = jnp.exp(s - m)
    e_sink = jnp.exp(sink - m)
    den = jnp.sum(pexp, axis=-1, keepdims=True) + e_sink
    prob = pexp / den
    return kk, vv, xh, rinv, q8, prob, e_sink / den


def _attn_specs(nb):
    blk = ATT_WINDOW
    cur = lambda i: (i, 0)
    prev = lambda i: (jnp.maximum(i - 1, 0), 0)
    return [
        pl.BlockSpec((blk, D_MODEL), cur),
        pl.BlockSpec((blk, KV_W), prev), pl.BlockSpec((blk, KV_W), cur),
        pl.BlockSpec((blk, KV_W), prev), pl.BlockSpec((blk, KV_W), cur),
        pl.BlockSpec((1, ATT_HEAD_DIM), lambda i: (0, 0)),
        pl.BlockSpec((ATT_KV_HEADS, ATT_GROUP * blk, 2 * blk), lambda i: (0, 0, 0)),
        pl.BlockSpec((ATT_KV_HEADS, ATT_GROUP * blk, 1), lambda i: (0, 0, 0)),
    ]


def attn_fwd(q_raw, k, v, q_norm, bias, sink_col, nm):
    t_dim = q_raw.shape[0]
    blk, hd = ATT_WINDOW, ATT_HEAD_DIM
    nb = t_dim // blk

    def body(q_ref, kp_ref, kc_ref, vp_ref, vc_ref, qn_ref, bias_ref, sink_ref, o_ref):
        for kv in range(ATT_KV_HEADS):
            kk, vv, xh, rinv, q8, prob, p_sink = _attn_scores(q_ref, kp_ref, kc_ref, vp_ref, vc_ref, qn_ref,
                                                              bias_ref, sink_ref, kv)
            o8 = _dot(prob, vv)
            for r in range(ATT_GROUP):
                hh = kv * ATT_GROUP + r
                o_ref[:, hh * hd:(hh + 1) * hd] = o8[r * blk:(r + 1) * blk].astype(o_ref.dtype)

    out, = pcall(body, name=nm, grid=(nb,), in_specs=_attn_specs(nb),
                 out_specs=[pl.BlockSpec((blk, D_MODEL), lambda i: (i, 0))],
                 out_shape=[jax.ShapeDtypeStruct((t_dim, D_MODEL), BF16)],
                 args=[q_raw, k, k, v, v, q_norm, bias, sink_col])
    return out


def attn_bwd(do, q_raw, k, v, q_norm, bias, sink_col, nm, hook=None):
    t_dim = q_raw.shape[0]
    blk, hd = ATT_WINDOW, ATT_HEAD_DIM
    nb = t_dim // blk
    scale = hd ** -0.5

    def body(do_ref, q_ref, kp_ref, kc_ref, vp_ref, vc_ref, qn_ref, bias_ref, sink_ref,
             dq_ref, dkc_ref, dkp_ref, dvc_ref, dvp_ref, dbias_ref, dsink_ref, dqn_ref):
        @pl.when(pl.program_id(0) == 0)
        def _():
            dbias_ref[...] = jnp.zeros_like(dbias_ref)
            dsink_ref[...] = jnp.zeros_like(dsink_ref)
            dqn_ref[...] = jnp.zeros_like(dqn_ref)

        gq = qn_ref[...]
        for kv in range(ATT_KV_HEADS):
            kk, vv, xh, rinv, q8, prob, p_sink = _attn_scores(q_ref, kp_ref, kc_ref, vp_ref, vc_ref, qn_ref,
                                                              bias_ref, sink_ref, kv)
            do8 = jnp.concatenate([do_ref[:, (kv * ATT_GROUP + r) * hd:(kv * ATT_GROUP + r + 1) * hd]
                                   for r in range(ATT_GROUP)], axis=0)
            dp = _dot(do8, vv, NT)
            delta = jnp.sum(prob * dp, axis=-1, keepdims=True)
            ds = prob * (dp - delta)
            dsink_ref[kv] += -p_sink * delta
            dbias_ref[kv] += ds
            ds_s = ds * scale
            dq8 = _dot(ds_s, kk)
            dkk = _dot(ds_s, q8, TN)
            dvv = _dot(prob, do8, TN)
            dqn_ref[...] += jnp.sum(dq8 * xh, axis=0, keepdims=True)
            dxh = dq8 * gq
            dq_raw8 = rinv * (dxh - xh * jnp.mean(dxh * xh, axis=-1, keepdims=True))
            for r in range(ATT_GROUP):
                hh = kv * ATT_GROUP + r
                dq_ref[:, hh * hd:(hh + 1) * hd] = dq_raw8[r * blk:(r + 1) * blk].astype(dq_ref.dtype)
            sl = slice(kv * hd, (kv + 1) * hd)
            dkp_ref[:, sl] = dkk[:blk]
            dkc_ref[:, sl] = dkk[blk:]
            dvp_ref[:, sl] = dvv[:blk]
            dvc_ref[:, sl] = dvv[blk:]

    cur = lambda i: (i, 0)
    row_spec = pl.BlockSpec((blk, KV_W), cur)
    out_specs = [
        pl.BlockSpec((blk, D_MODEL), cur), row_spec, row_spec, row_spec, row_spec,
        pl.BlockSpec((ATT_KV_HEADS, ATT_GROUP * blk, 2 * blk), lambda i: (0, 0, 0)),
        pl.BlockSpec((ATT_KV_HEADS, ATT_GROUP * blk, 1), lambda i: (0, 0, 0)),
        pl.BlockSpec((1, hd), lambda i: (0, 0)),
    ]
    kvs = jax.ShapeDtypeStruct((t_dim, KV_W), F32)
    out_shape = [
        jax.ShapeDtypeStruct((t_dim, D_MODEL), BF16), kvs, kvs, kvs, kvs,
        jax.ShapeDtypeStruct((ATT_KV_HEADS, ATT_GROUP * blk, 2 * blk), F32),
        jax.ShapeDtypeStruct((ATT_KV_HEADS, ATT_GROUP * blk, 1), F32),
        jax.ShapeDtypeStruct((1, hd), F32),
    ]
    return pcall(body, name=nm, grid=(nb,), in_specs=[pl.BlockSpec((blk, D_MODEL), cur)] + _attn_specs(nb),
                 out_specs=out_specs, out_shape=out_shape,
                 args=[do, q_raw, k, k, v, v, q_norm, bias, sink_col], hook=hook)


def _t5_bucket_np():
    blk = ATT_WINDOW
    qi = np.arange(blk)[:, None] + blk
    kj = np.arange(2 * blk)[None, :]
    dist = qi - kj
    n = np.maximum(dist, 0)
    max_exact = REL_BUCKETS // 2
    nf = np.maximum(n, 1).astype(np.float32)
    large = max_exact + (np.log(nf / max_exact) / math.log(ATT_WINDOW / max_exact)
                         * (REL_BUCKETS - max_exact)).astype(np.int32)
    large = np.minimum(large, REL_BUCKETS - 1)
    bucket = np.where(n < max_exact, n, large)
    in_window = (dist >= 0) & (dist < ATT_WINDOW)
    return bucket, in_window


def attn_block_fwd(h, k, v, p, nm):
    u, = rowmap(lambda x, gg: (_rms_fwd(x, gg),), [h], [p["attn_norm"]], [(D_MODEL, BF16)], tm=256, name=nm + "_norm")
    q_raw = mm(u, p["w_q"], name=nm + "_q")
    o = attn_fwd(q_raw, k, v, p["q_norm"], p["bias"], p["sink_col"], nm + "_core")
    out = mm(o, p["w_o"], res=h, name=nm + "_o")
    return out, (u, q_raw, o)


def attn_block_bwd(h, k, v, p, saved, dout, dout_bf, nm, hook=None):
    u, q_raw, o = saved
    g = {}
    g["w_o"] = mm(o, dout_bf, ta=True, out_dtype=BF16, name=nm + "_dwo")
    do = mm(dout_bf, p["w_o"], tb=True, name=nm + "_do")
    dq_raw, dkc, dkp, dvc, dvp, g["bias"], g["sink_col"], g["q_norm"] = attn_bwd(
        do, q_raw, k, v, p["q_norm"], p["bias"], p["sink_col"], nm + "_dcore", hook=hook)
    g["w_q"] = mm(u, dq_raw, ta=True, out_dtype=BF16, name=nm + "_dwq")
    du = mm(dq_raw, p["w_q"], tb=True, name=nm + "_du")
    dh, dh_bf, g["attn_norm"] = norm_bwd(h, p["attn_norm"], du, dout, nm + "_dnorm")
    return dh, dh_bf, g, (dkc, dkp, dvc, dvp)


FFN_TAGS = ["00", "01", "10", "11"]


def local_step(x, target, small, io):
    bucket, in_window = _t5_bucket_np()
    blk = ATT_WINDOW
    w = small

    def ffn_w(tag):
        return io.w("w1t_" + tag), io.w("w3t_" + tag), io.w("w2_" + tag)

    fnorm = {tag: w["ffn_norm"][int(tag[0]), int(tag[1])][None, :] for tag in FFN_TAGS}
    a_neg = -jnp.exp(w["ssm_a_log"][0])

    def mamba_p():
        w_int = io.w("w_int")
        return dict(ssm_norm=w["ssm_norm"], w_zt=w_int[:SSM_D_INNER],
                    w_xbct=w_int[SSM_D_INNER:SSM_D_INNER + SSM_CONV_DIM], w_dtt=w_int[SSM_D_INNER + SSM_CONV_DIM:],
                    conv_w=w["ssm_conv_w"][0], conv_b=w["ssm_conv_b"], dt_bias=w["ssm_dt_bias"],
                    a_row=a_neg[None, :], a_col=a_neg[:, None],
                    dvec=jnp.repeat(w["ssm_d"][0], SSM_HEAD_DIM).reshape(SSM_HEADS // 2, LANES),
                    gate_norm=w["ssm_gate_norm"], w_out=io.w("w_out"))

    rb = w["rel_bias"]
    onehot3 = (np.arange(REL_BUCKETS)[:, None, None] == bucket[None]).astype(np.float32)
    bias = jnp.einsum("bh,bqk->hqk", rb, onehot3, precision=lax.Precision.HIGHEST)
    bias = jnp.where(in_window[None], bias, NEG)
    bias = bias.reshape(ATT_KV_HEADS, ATT_GROUP * blk, 2 * blk)
    sink_col = jnp.repeat(w["sinks"][0], blk).reshape(ATT_KV_HEADS, ATT_GROUP * blk, 1)

    def attn_p():
        return dict(attn_norm=w["attn_norm"], w_q=io.w("w_q"), q_norm=w["q_norm"], bias=bias, sink_col=sink_col,
                    w_o=io.w("w_o"))

    def kv_p():
        return dict(kv_norm=w["kv_norm"][None, :], w_kv=io.w("w_kv"), k_norm=w["k_norm"][None, :])

    h0 = x
    h0a, s_f00 = ffn_fwd(h0, fnorm["00"], *ffn_w("00"), "f00", hook=io.hook("f00_upgate"))
    mp = mamba_p()
    h0b, s_m = mamba_fwd(h0a, mp, "ssm", hook=io.hook("ssm_ssd"))
    h1, s_f01 = ffn_fwd(h0b, fnorm["01"], *ffn_w("01"), "f01", hook=io.hook("f01_upgate"))
    kp = kv_p()
    k, v, s_kv = kv_fwd(h1, kp, "kv")
    h1a, s_f10 = ffn_fwd(h1, fnorm["10"], *ffn_w("10"), "f10", hook=io.hook("f10_upgate"))
    ap = attn_p()
    h1b, s_a = attn_block_fwd(h1a, k, v, ap, "att")
    h2, s_f11 = ffn_fwd(h1b, fnorm["11"], *ffn_w("11"), "f11")

    def loss_fn(y, t):
        e = y - t
        d = e * (1.0 / D_MODEL)
        return d, d, jnp.sum(e * e, axis=0, keepdims=True)

    dh, dh_bf, sq = rowmap(loss_fn, [h2, target], [], [(D_MODEL, F32), (D_MODEL, BF16)], [(1, D_MODEL)], tm=256,
                           name="loss")
    loss_part = jnp.sum(sq) * (0.5 / D_MODEL)

    fg = {}

    def ffn_back(tag, h_in, saved, dh, dh_bf, site):
        dh, dh_bf, dg, dw1t, dw3t, dw2 = ffn_bwd(h_in, fnorm[tag], *ffn_w(tag), saved, dh, dh_bf, "f" + tag,
                                                 hook=io.hook(site))
        fg[tag] = dg[0]
        io.put("w1t_" + tag, dw1t)
        io.put("w3t_" + tag, dw3t)
        io.put("w2_" + tag, dw2)
        return dh, dh_bf

    dh, dh_bf = ffn_back("11", h1b, s_f11, dh, dh_bf, "f11_dgate")
    dh, dh_bf, ga, dkv = attn_block_bwd(h1a, k, v, ap, s_a, dh, dh_bf, "att", hook=io.hook("att_dcore"))
    io.put("w_q", ga["w_q"])
    io.put("w_o", ga["w_o"])
    dh, dh_bf = ffn_back("10", h1, s_f10, dh, dh_bf, "f10_dgate")
    dh, dh_bf, gk = kv_bwd(h1, kp, s_kv, *dkv, dh, "kv")
    io.put("w_kv", gk["w_kv"])
    dh, dh_bf = ffn_back("01", h0b, s_f01, dh, dh_bf, "f01_dgate")
    dh, dh_bf, gm = mamba_bwd(h0a, mp, s_m, dh, dh_bf, "ssm", hook=io.hook("ssm_dssd"))
    io.put("w_int", jnp.concatenate([gm["w_zt"], gm["w_xbct"], gm["w_dtt"]], axis=0))
    io.put("w_out", gm["w_out"])
    dh, dh_bf = ffn_back("00", h0, s_f00, dh, dh_bf, "f00_dgate")
    grad_x = dh

    grads = {}
    grads["ffn_norm"] = jnp.stack([fg[tag] for tag in FFN_TAGS]).reshape(2, 2, D_MODEL)
    grads["ssm_norm"] = gm["ssm_norm"]
    grads["ssm_conv_w"] = gm["conv_w"][None]
    grads["ssm_conv_b"] = gm["conv_b"]
    grads["ssm_dt_bias"] = gm["dt_bias"]
    grads["ssm_a_log"] = gm["a"] * a_neg[None, :]
    grads["ssm_d"] = jnp.sum(gm["dvec"].reshape(SSM_HEADS, SSM_HEAD_DIM), axis=1)[None, :]
    grads["ssm_gate_norm"] = gm["gate_norm"]
    grads["kv_norm"] = gk["kv_norm"][0]
    grads["k_norm"] = gk["k_norm"][0]
    grads["attn_norm"] = ga["attn_norm"]
    grads["q_norm"] = ga["q_norm"]
    grads["sinks"] = jnp.sum(ga["sink_col"].reshape(ATT_HEADS, blk), axis=1)[None, :]
    onehot = (np.arange(REL_BUCKETS)[:, None] == bucket.reshape(1, -1)).astype(np.float32)
    dbias2d = ga["bias"].reshape(ATT_HEADS, blk * 2 * blk)
    grads["rel_bias"] = mm(jnp.asarray(onehot, BF16), dbias2d, tb=True, name="drelbias")
    return loss_part, grad_x, grads


def _adamw(g, w, m, v):
    m = ADAM_B1 * m + (1.0 - ADAM_B1) * g
    v = ADAM_B2 * v + (1.0 - ADAM_B2) * (g * g)
    m_hat = m / (1.0 - ADAM_B1 ** ADAM_STEP)
    v_hat = v / (1.0 - ADAM_B2 ** ADAM_STEP)
    delta = -ADAM_LR * (m_hat / (jnp.sqrt(v_hat) + ADAM_EPS) + ADAM_WD * w)
    return delta, m, v


def _slot_sum(r):
    g = r[0].astype(F32)
    for d in range(1, N_DEV):
        g = g + r[d].astype(F32)
    return g


def adamw_rows(recvs, w, m, v, name):
    n_l, rows, width = w.shape
    tr = 32
    assert rows % tr == 0, rows
    nt = rows // tr

    def body(*refs):
        r_refs = refs[:n_l]
        w_ref, m_ref, v_ref, g_o, d_o, m_o, v_o = refs[n_l:]
        li = pl.program_id(0)
        for k in range(n_l):
            @pl.when(li == k)
            def _(k=k):
                g = _slot_sum(r_refs[k])
                delta, m2, v2 = _adamw(g, w_ref[0], m_ref[0], v_ref[0])
                g_o[0] = g
                d_o[0] = delta
                m_o[0] = m2
                v_o[0] = v2

    def r_spec(k):
        return pl.BlockSpec((N_DEV, tr, width),
                            lambda li, j: (0, jnp.where(li == k, j, jnp.where(li > k, nt - 1, 0)), 0))

    w_spec = pl.BlockSpec((1, tr, width), lambda li, j: (li, j, 0))
    shp = jax.ShapeDtypeStruct(w.shape, F32)
    return pcall(body, name=name, grid=(n_l, nt), in_specs=[r_spec(k) for k in range(n_l)] + [w_spec] * 3,
                 out_specs=[w_spec] * 4, out_shape=[shp] * 4, args=list(recvs) + [w, m, v])


def adamw_cols(recvs, w, m, v, name):
    n_l, rows, n = w.shape
    tr = 256
    nt = rows // tr

    def body(*refs):
        r_refs = refs[:n_l]
        w_ref, m_ref, v_ref, g_o, d_o, m_o, v_o = refs[n_l:]
        li = pl.program_id(0)
        for k in range(n_l):
            @pl.when(li == k)
            def _(k=k):
                g = _slot_sum(r_refs[k]).T
                delta, m2, v2 = _adamw(g, w_ref[0], m_ref[0], v_ref[0])
                g_o[0] = g
                d_o[0] = delta
                m_o[0] = m2
                v_o[0] = v2

    def r_spec(k):
        return pl.BlockSpec((N_DEV, n, tr),
                            lambda li, j: (0, 0, jnp.where(li == k, j, jnp.where(li > k, nt - 1, 0))))

    w_spec = pl.BlockSpec((1, tr, n), lambda li, j: (li, j, 0))
    shp = jax.ShapeDtypeStruct(w.shape, F32)
    return pcall(body, name=name, grid=(n_l, nt), in_specs=[r_spec(k) for k in range(n_l)] + [w_spec] * 3,
                 out_specs=[w_spec] * 4, out_shape=[shp] * 4, args=list(recvs) + [w, m, v])


WEIGHT_NAMES = ["ffn_norm", "ffn_w1", "ffn_w3", "ffn_w2", "ssm_norm", "ssm_w_in", "ssm_conv_w", "ssm_conv_b",
                "ssm_dt_bias", "ssm_a_log", "ssm_d", "ssm_gate_norm", "ssm_w_out", "kv_norm", "w_kv", "k_norm",
                "attn_norm", "w_q", "q_norm", "sinks", "w_o", "rel_bias"]

SMALL = [
    ("ffn_norm", (2, 2, 1024), 2), ("ssm_norm", (1, 1024), 1), ("ssm_conv_w", (1, 4, 3072), 2),
    ("ssm_conv_b", (1, 3072), 1), ("ssm_gate_norm", (1, 2048), 1),
    ("ssm_dt_bias", (1, 32), None), ("ssm_a_log", (1, 32), None), ("ssm_d", (1, 32), None),
    ("kv_norm", (1024,), None), ("k_norm", (64,), None), ("attn_norm", (1, 1024), None),
    ("q_norm", (1, 64), None), ("sinks", (1, 16), None), ("rel_bias", (32, 16), None),
]
SMALL_W = 1024
SMALL_FULL_ROWS = 32
SMALL_LOCAL_ROWS = 48

GATHER_PLAN = {
    "f00_upgate": ["w_int", "w_out"],
    "ssm_ssd": ["w1t_01", "w3t_01", "w2_01", "w_kv", "w_q", "w_o"],
    "f01_upgate": ["w1t_10", "w3t_10", "w2_10"],
    "f10_upgate": ["w1t_11", "w3t_11", "w2_11"],
}
FIRST_GATHER = ["w1t_00", "w3t_00", "w2_00"]
SCATTER_PLAN = {
    "att_dcore": ["w1t_11", "w3t_11", "w2_11"],
    "ssm_dssd": ["w_q", "w_o", "w1t_10", "w3t_10", "w2_10", "w_kv", "w1t_01", "w3t_01", "w2_01"],
    "f00_dgate": ["w_int", "w_out"],
}
LAST_SCATTER = ["w1t_00", "w3t_00", "w2_00"]
SLOT_MAJOR = ("w_int",)


def _shard_shape(s, a):
    return s[:a] + (s[a] // N_DEV,) + s[a + 1:]


def _unshard_view(stack, shard_shape, axis):
    moved = jnp.moveaxis(stack, 0, axis)
    return moved.reshape(shard_shape[:axis] + (N_DEV * shard_shape[axis],) + shard_shape[axis + 1:])


def _small_local(arrs):
    flat = jnp.concatenate([arrs[n].reshape(-1) for n, _, _ in SMALL])
    return jnp.pad(flat, (0, SMALL_LOCAL_ROWS * LANES - flat.shape[0])).reshape(SMALL_LOCAL_ROWS, LANES)


class StepIO:
    def __init__(self, pieces):
        self.pieces = pieces
        self.full = {}
        self.grad = {}
        self.recv = {}

    def gather_comm(self, names, extra=()):
        items = [("g", self.pieces[n], None if n in SLOT_MAJOR else 0) for n in names] + list(extra)

        def sink(outs):
            for n, o in zip(names, outs):
                self.full[n] = o.reshape((-1,) + o.shape[2:]) if n in SLOT_MAJOR else o
            self.extra_out = list(outs[len(names):])

        return Comm(items), sink

    def scatter_comm(self, names, extra=()):
        items = []
        for n in names:
            g = self.grad[n]
            if n in SLOT_MAJOR:
                items.append(("s", g.reshape((N_DEV, g.shape[0] // N_DEV) + g.shape[1:]), None))
            else:
                items.append(("s", g, 0))
        items += list(extra)

        def sink(outs):
            for n, o in zip(names, outs):
                self.recv[n] = o
            self.extra_out = list(outs[len(names):])

        return Comm(items), sink

    def w(self, name):
        return self.full[name]

    def put(self, name, g):
        self.grad[name] = g

    def hook(self, site):
        if site in GATHER_PLAN:
            return self.gather_comm(GATHER_PLAN[site])
        if site in SCATTER_PLAN:
            return self.scatter_comm(SCATTER_PLAN[site])
        return None


def step(x, target, wts, ms, vs):
    me = _my_index()

    pieces = {}
    for li in range(2):
        for hi in range(2):
            tag = "%d%d" % (li, hi)
            pieces["w1t_" + tag] = wts["ffn_w1"][li, hi].T.astype(BF16)
            pieces["w3t_" + tag] = wts["ffn_w3"][li, hi].T.astype(BF16)
            pieces["w2_" + tag] = wts["ffn_w2"][li, hi].astype(BF16)
    pieces["w_int"] = wts["ssm_w_in"][0].T.astype(BF16)
    pieces["w_out"] = wts["ssm_w_out"][0].astype(BF16)
    pieces["w_kv"] = wts["w_kv"].astype(BF16)
    pieces["w_q"] = wts["w_q"][0].astype(BF16)
    pieces["w_o"] = wts["w_o"][0].astype(BF16)
    io = StepIO(pieces)

    small_sharded = [(n, s, a) for n, s, a in SMALL if a is not None]
    loc = jnp.concatenate([wts[n].reshape(-1) for n, _, _ in small_sharded])
    loc_rows = -(-loc.shape[0] // (8 * LANES)) * 8
    loc = jnp.pad(loc, (0, loc_rows * LANES - loc.shape[0])).reshape(loc_rows, LANES)
    comm, sink = io.gather_comm(FIRST_GATHER, extra=[("g", loc, None)])
    sink(comm_only(comm, "gather_first"))
    gath_small = io.extra_out[0].reshape(N_DEV, -1)
    small = {}
    off = 0
    for n, s, a in small_sharded:
        shard = _shard_shape(s, a)
        cnt = int(np.prod(shard))
        small[n] = _unshard_view(gath_small[:, off:off + cnt].reshape((N_DEV,) + shard), shard, a)
        off += cnt
    for n, s, a in SMALL:
        if a is None:
            small[n] = wts[n]

    loss_part, grad_x, g_small_local = local_step(x[0], target[0], small, io)
    loss = lax.psum(loss_part, ("x", "y", "c"))

    small_flat = jnp.concatenate([g_small_local[n].reshape(-1) for n, _, _ in SMALL])
    small_buf = jnp.pad(small_flat, (0, SMALL_FULL_ROWS * SMALL_W - small_flat.shape[0]))
    small_buf = small_buf.reshape(SMALL_FULL_ROWS, SMALL_W)
    comm, sink = io.scatter_comm(LAST_SCATTER, extra=[("g", small_buf, None)])
    sink(comm_only(comm, "exchange_last"))
    small_all = io.extra_out[0]
    def sum_body(r_ref, o_ref):
        o_ref[...] = _slot_sum(r_ref)

    vmem = pl.BlockSpec(memory_space=pltpu.VMEM)
    small_sum, = pcall(sum_body, name="sum_small", grid=(), in_specs=[vmem], out_specs=[vmem],
                       out_shape=[jax.ShapeDtypeStruct((SMALL_FULL_ROWS, SMALL_W), F32)], args=[small_all])
    small_sum = small_sum.reshape(-1)
    g_small = {}
    off = 0
    for n, s, a in SMALL:
        cnt = int(np.prod(s))
        gfull = small_sum[off:off + cnt].reshape(s)
        off += cnt
        if a is None:
            g_small[n] = gfull
        else:
            width = s[a] // N_DEV
            g_small[n] = lax.dynamic_slice_in_dim(gfull, me * width, width, axis=a)

    out = {}

    def emit(name, res, shape):
        for kind, arr in zip(("grad", "delta", "new_m", "new_v"), res):
            out[kind + "_" + name] = arr.reshape(shape)

    for name, key in (("ffn_w1", "w1t_"), ("ffn_w3", "w3t_")):
        shp = wts[name].shape
        view = lambda t: t.reshape((4,) + shp[2:])
        res = adamw_cols([io.recv[key + tag] for tag in FFN_TAGS], view(wts[name]), view(ms[name]), view(vs[name]),
                         "adamw_" + name)
        emit(name, res, shp)
    shp = wts["ffn_w2"].shape
    view = lambda t: t.reshape((4,) + shp[2:])
    res = adamw_rows([io.recv["w2_" + tag] for tag in FFN_TAGS], view(wts["ffn_w2"]), view(ms["ffn_w2"]),
                     view(vs["ffn_w2"]), "adamw_ffn_w2")
    emit("ffn_w2", res, shp)
    res = adamw_cols([io.recv["w_int"]], wts["ssm_w_in"], ms["ssm_w_in"], vs["ssm_w_in"], "adamw_ssm_w_in")
    emit("ssm_w_in", res, wts["ssm_w_in"].shape)
    for name, key in (("ssm_w_out", "w_out"), ("w_kv", "w_kv"), ("w_q", "w_q"), ("w_o", "w_o")):
        shp = wts[name].shape
        view = lambda t: t.reshape((1,) + shp[-2:])
        res = adamw_rows([io.recv[key]], view(wts[name]), view(ms[name]), view(vs[name]), "adamw_" + name)
        emit(name, res, shp)

    res_s = rowmap(lambda gg, ww, mm_, vv: _adamw(gg, ww, mm_, vv),
                   [_small_local(g_small), _small_local(wts), _small_local(ms), _small_local(vs)], [],
                   [(LANES, F32)] * 3, tm=SMALL_LOCAL_ROWS, name="adamw_small")
    flat_s = [r.reshape(-1) for r in res_s]
    off = 0
    for n, s, a in SMALL:
        shard = s if a is None else _shard_shape(s, a)
        cnt = int(np.prod(shard))
        out["grad_" + n] = g_small[n]
        for kind, arr in zip(("delta", "new_m", "new_v"), flat_s):
            out[kind + "_" + n] = arr[off:off + cnt].reshape(shard)
        off += cnt
    out["loss"] = loss
    out["grad_x"] = grad_x[None]
    return out


def kernel(x, ffn_norm, ffn_w1, ffn_w3, ffn_w2, ssm_norm, ssm_w_in, ssm_conv_w, ssm_conv_b, ssm_dt_bias, ssm_a_log, ssm_d, ssm_gate_norm, ssm_w_out, kv_norm, w_kv, k_norm, attn_norm, w_q, q_norm, sinks, w_o, rel_bias, loss_target, m_ffn_norm, m_ffn_w1, m_ffn_w3, m_ffn_w2, m_ssm_norm, m_ssm_w_in, m_ssm_conv_w, m_ssm_conv_b, m_ssm_dt_bias, m_ssm_a_log, m_ssm_d, m_ssm_gate_norm, m_ssm_w_out, m_kv_norm, m_w_kv, m_k_norm, m_attn_norm, m_w_q, m_q_norm, m_sinks, m_w_o, m_rel_bias, v_ffn_norm, v_ffn_w1, v_ffn_w3, v_ffn_w2, v_ssm_norm, v_ssm_w_in, v_ssm_conv_w, v_ssm_conv_b, v_ssm_dt_bias, v_ssm_a_log, v_ssm_d, v_ssm_gate_norm, v_ssm_w_out, v_kv_norm, v_w_kv, v_k_norm, v_attn_norm, v_w_q, v_q_norm, v_sinks, v_w_o, v_rel_bias):
    args = locals()
    wts = {n: args[n] for n in WEIGHT_NAMES}
    ms = {n: args["m_" + n] for n in WEIGHT_NAMES}
    vs = {n: args["v_" + n] for n in WEIGHT_NAMES}
    out = step(x, loss_target, wts, ms, vs)
    result = [out["loss"], out["grad_x"]]
    for kind in ("grad", "delta", "new_m", "new_v"):
        result += [out[kind + "_" + n] for n in WEIGHT_NAMES]
    return tuple(result)
```

```python
import functools
import math
import operator

import numpy as np
import jax
import jax.numpy as jnp
from jax import lax
from jax.experimental import pallas as pl
from jax.experimental.pallas import tpu as pltpu

F32 = jnp.float32
BF16 = jnp.bfloat16

D_MODEL = 1024
D_FF = 2816
N_DEV = 8
SSM_D_INNER = 2048
SSM_HEAD_DIM = 64
SSM_HEADS = 32
SSM_GROUPS = 4
SSM_STATE = 128
SSM_CONV = 4
SSM_CHUNK = 256
SSM_CONV_DIM = SSM_D_INNER + 2 * SSM_GROUPS * SSM_STATE
SSM_IN_DIM = SSM_D_INNER + SSM_CONV_DIM + SSM_HEADS
ATT_HEAD_DIM = 64
ATT_HEADS = 16
ATT_KV_HEADS = 2
ATT_GROUP = 8
ATT_WINDOW = 128
REL_BUCKETS = 32
EPS = 1e-6
NEG = -1e30

ADAM_LR = 0.001
ADAM_B1 = 0.9
ADAM_B2 = 0.999
ADAM_EPS = 1e-08
ADAM_WD = 0.01
ADAM_STEP = 10

VMEM_LIMIT_BYTES = 52 * 1024 * 1024
LANES = 128
MESH_ID = pl.DeviceIdType.MESH
ANY_SPEC = pl.BlockSpec(memory_space=pl.ANY)

NT = (((1,), (1,)), ((), ()))
TN = (((0,), (0,)), ((), ()))
NN = (((1,), (0,)), ((), ()))


def _pick(dim, cands):
    for c in cands:
        if dim % c == 0:
            return c
    return dim


def _my_index():
    return 4 * lax.axis_index("x") + 2 * lax.axis_index("y") + lax.axis_index("c")


def _peer(k):
    x, y, c = lax.axis_index("x"), lax.axis_index("y"), lax.axis_index("c")
    px = 1 - x if (k >> 2) & 1 else x
    py = 1 - y if (k >> 1) & 1 else y
    pc = 1 - c if k & 1 else c
    return (px, py, pc), 4 * px + 2 * py + pc


def _piece(ref, axis, d, n):
    if axis is None:
        return ref.at[d]
    return ref.at[(slice(None),) * axis + (pl.ds(pl.multiple_of(d * n, 8), n),)]


class Comm:
    def __init__(self, items):
        self.items = list(items)

    def dst_shapes(self):
        out = []
        for kind, src, axis in self.items:
            s = tuple(src.shape)
            if kind == "g":
                shp = (N_DEV,) + s if axis is None else s[:axis] + (N_DEV * s[axis],) + s[axis + 1:]
            else:
                shp = s if axis is None else (N_DEV,) + s[:axis] + (s[axis] // N_DEV,) + s[axis + 1:]
            out.append(jax.ShapeDtypeStruct(shp, src.dtype))
        return out

    def scratch(self):
        n = len(self.items)
        return [pltpu.SemaphoreType.DMA((n * (N_DEV - 1),)), pltpu.SemaphoreType.DMA((n * (N_DEV - 1),)),
                pltpu.SemaphoreType.DMA((n,))]

    def _copies(self, srcs, dsts, sems, with_recvs=True):
        send_sems, recv_sems, local_sems = sems
        me = _my_index()
        local, sends, recvs = [], [], []
        for i, (kind, src, axis) in enumerate(self.items):
            s_ref, d_ref = srcs[i], dsts[i]
            if kind == "g":
                n = None if axis is None else src.shape[axis]
                local.append(pltpu.make_async_copy(s_ref, _piece(d_ref, axis, me, n), local_sems.at[i]))
            else:
                n = None if axis is None else src.shape[axis] // N_DEV
                local.append(pltpu.make_async_copy(_piece(s_ref, axis, me, n), d_ref.at[me], local_sems.at[i]))
            for k in range(1, N_DEV):
                peer, pidx = _peer(k)
                j = i * (N_DEV - 1) + k - 1
                if kind == "g":
                    out_src, out_dst, in_dst = s_ref, _piece(d_ref, axis, me, n), _piece(d_ref, axis, pidx, n)
                else:
                    out_src, out_dst, in_dst = _piece(s_ref, axis, pidx, n), d_ref.at[me], d_ref.at[pidx]
                sends.append(pltpu.make_async_remote_copy(
                    src_ref=out_src, dst_ref=out_dst, send_sem=send_sems.at[j], recv_sem=recv_sems.at[j],
                    device_id=peer, device_id_type=MESH_ID))
                if with_recvs:
                    recvs.append(pltpu.make_async_remote_copy(
                        src_ref=out_src, dst_ref=in_dst, send_sem=send_sems.at[j], recv_sem=recv_sems.at[j],
                        device_id=peer, device_id_type=MESH_ID))
        return local, sends, recvs

    def start(self, srcs, dsts, sems):
        local, sends, _ = self._copies(srcs, dsts, sems, with_recvs=False)
        for cp in local + sends:
            cp.start()

    def wait(self, srcs, dsts, sems):
        local, sends, recvs = self._copies(srcs, dsts, sems)
        for cp in recvs:
            cp.wait_recv()
        for cp in sends:
            cp.wait_send()
        for cp in local:
            cp.wait()


def pcall(body, *, name, grid, in_specs, out_specs, out_shape, args, scratch=(), hook=None):
    cparams = pltpu.CompilerParams(dimension_semantics=("arbitrary",) * len(grid), vmem_limit_bytes=VMEM_LIMIT_BYTES)
    if hook is None:
        outs = pl.pallas_call(body, name=name, grid=grid, in_specs=list(in_specs), out_specs=list(out_specs),
                              out_shape=list(out_shape), scratch_shapes=list(scratch), compiler_params=cparams)(*args)
        return list(outs)
    comm, sink = hook
    n_in, n_out, n_scr, n_it = len(args), len(out_shape), len(scratch), len(comm.items)
    dims = tuple(grid)

    def wrapped(*refs):
        p = 0
        ins = refs[p:p + n_in]
        p += n_in
        csrc = refs[p:p + n_it]
        p += n_it
        outs = refs[p:p + n_out]
        p += n_out
        cdst = refs[p:p + n_it]
        p += n_it
        scr = refs[p:p + n_scr]
        p += n_scr
        sems = refs[p:p + 3]
        if dims:
            ids = [pl.program_id(a) for a in range(len(dims))]
            first = functools.reduce(operator.and_, [i == 0 for i in ids])
            last = functools.reduce(operator.and_, [i == d - 1 for i, d in zip(ids, dims)])

            @pl.when(first)
            def _():
                comm.start(csrc, cdst, sems)

            body(*ins, *outs, *scr)

            @pl.when(last)
            def _():
                comm.wait(csrc, cdst, sems)
        else:
            comm.start(csrc, cdst, sems)
            body(*ins, *outs, *scr)
            comm.wait(csrc, cdst, sems)

    res = pl.pallas_call(
        wrapped, name=name, grid=grid,
        in_specs=list(in_specs) + [ANY_SPEC] * n_it, out_specs=list(out_specs) + [ANY_SPEC] * n_it,
        out_shape=list(out_shape) + comm.dst_shapes(), scratch_shapes=list(scratch) + comm.scratch(),
        compiler_params=cparams,
    )(*args, *[src for _, src, _ in comm.items])
    res = list(res)
    sink(res[n_out:])
    return res[:n_out]


def comm_only(comm, name):
    got = []
    pcall(lambda *refs: None, name=name, grid=(), in_specs=[], out_specs=[], out_shape=[], args=[],
          hook=(comm, got.extend))
    return got


def mm(a, b, *, ta=False, tb=False, out_dtype=F32, res=None, alpha=1.0, name, hook=None):
    if ta:
        k_dim, m_dim = a.shape
    else:
        m_dim, k_dim = a.shape
    if tb:
        n_dim, k2 = b.shape
    else:
        k2, n_dim = b.shape
    assert k_dim == k2, (a.shape, b.shape, ta, tb)
    tn = _pick(n_dim, (1024, 1408, 512, 256, 128))
    tm = _pick(m_dim, (1024, 1408, 512, 256, 128)) if tn <= 1024 else _pick(m_dim, (512, 256, 128))
    tk = _pick(k_dim, (512, 1408, 256, 128))
    nk = k_dim // tk
    has_res = res is not None
    dn = (((0 if ta else 1,), (1 if tb else 0,)), ((), ()))

    def body(*refs):
        if has_res:
            a_ref, b_ref, r_ref, o_ref, acc_ref = refs
        else:
            a_ref, b_ref, o_ref, acc_ref = refs
        k = pl.program_id(2)

        @pl.when(k == 0)
        def _():
            acc_ref[...] = jnp.zeros_like(acc_ref)

        acc_ref[...] += lax.dot_general(a_ref[...].astype(BF16), b_ref[...].astype(BF16), dn,
                                        preferred_element_type=F32)

        @pl.when(k == nk - 1)
        def _():
            r = acc_ref[...]
            if alpha != 1.0:
                r = r * alpha
            if has_res:
                r = r_ref[...] + r
            o_ref[...] = r.astype(o_ref.dtype)

    a_spec = pl.BlockSpec((tk, tm), lambda i, j, k: (k, i)) if ta else pl.BlockSpec((tm, tk), lambda i, j, k: (i, k))
    b_spec = pl.BlockSpec((tn, tk), lambda i, j, k: (j, k)) if tb else pl.BlockSpec((tk, tn), lambda i, j, k: (k, j))
    o_spec = pl.BlockSpec((tm, tn), lambda i, j, k: (i, j))
    in_specs = [a_spec, b_spec] + ([o_spec] if has_res else [])
    args = [a, b] + ([res] if has_res else [])
    out, = pcall(body, name=name, grid=(m_dim // tm, n_dim // tn, nk), in_specs=in_specs, out_specs=[o_spec],
                 out_shape=[jax.ShapeDtypeStruct((m_dim, n_dim), out_dtype)], args=args,
                 scratch=[pltpu.VMEM((tm, tn), F32)], hook=hook)
    return out


def rowmap(fn, rows, consts=(), out_rows=(), out_accs=(), *, tm, name, hook=None):
    first = rows[0][0] if isinstance(rows[0], tuple) else rows[0]
    t_dim = first.shape[0]
    assert t_dim % tm == 0, (t_dim, tm)
    n_r, n_c, n_o = len(rows), len(consts), len(out_rows)

    def body(*refs):
        ins = [r[...] for r in refs[:n_r + n_c]]
        o_refs = refs[n_r + n_c:]
        outs = tuple(fn(*ins))
        for o_ref, val in zip(o_refs[:n_o], outs[:n_o]):
            o_ref[...] = val.astype(o_ref.dtype)
        if out_accs:
            @pl.when(pl.program_id(0) == 0)
            def _():
                for o_ref in o_refs[n_o:]:
                    o_ref[...] = jnp.zeros_like(o_ref)

            for o_ref, val in zip(o_refs[n_o:], outs[n_o:]):
                o_ref[...] += val

    in_specs, args = [], []
    for r in rows:
        if isinstance(r, tuple):
            args.append(r[0])
            in_specs.append(r[1])
        else:
            args.append(r)
            in_specs.append(pl.BlockSpec((tm, r.shape[1]), lambda i: (i, 0)))
    for c in consts:
        args.append(c)
        in_specs.append(pl.BlockSpec(c.shape, lambda i, nd=c.ndim: (0,) * nd))
    out_specs = [pl.BlockSpec((tm, w), lambda i: (i, 0)) for (w, _) in out_rows]
    out_specs += [pl.BlockSpec(s, lambda i, nd=len(s): (0,) * nd) for s in out_accs]
    out_shape = [jax.ShapeDtypeStruct((t_dim, w), dt) for (w, dt) in out_rows]
    out_shape += [jax.ShapeDtypeStruct(s, F32) for s in out_accs]
    return pcall(body, name=name, grid=(t_dim // tm,), in_specs=in_specs, out_specs=out_specs, out_shape=out_shape,
                 args=args, hook=hook)


def _rms_fwd(x, g):
    r = lax.rsqrt(jnp.mean(x * x, axis=-1, keepdims=True) + EPS)
    return x * r * g


def _rms_bwd(x, g, dy):
    r = lax.rsqrt(jnp.mean(x * x, axis=-1, keepdims=True) + EPS)
    xh = x * r
    dg = jnp.sum(dy * xh, axis=0, keepdims=True)
    dxh = dy * g
    dx = r * (dxh - xh * jnp.mean(dxh * xh, axis=-1, keepdims=True))
    return dx, dg


def _sigmoid(x):
    return 1.0 / (1.0 + jnp.exp(-x))


def _silu(x):
    return x * _sigmoid(x)


def _silu_grad(x):
    s = _sigmoid(x)
    return s * (1.0 + x * (1.0 - s))


def _split3(x):
    hi = x.astype(BF16)
    r1 = x - hi.astype(F32)
    mid = r1.astype(BF16)
    lo = (r1 - mid.astype(F32)).astype(BF16)
    return hi, mid, lo


def _dot(a, b, dn=NN):
    return lax.dot_general(a.astype(BF16), b.astype(BF16), dn, preferred_element_type=F32)


def _col_of(mat, h):
    lane = lax.broadcasted_iota(jnp.int32, mat.shape, 1)
    return jnp.sum(jnp.where(lane == h, mat, 0.0), axis=1, keepdims=True)


FFN_TN = 1408


def ffn_upgate(u, w1t, w3t, nm, hook=None):
    t_dim = u.shape[0]
    tm = _pick(t_dim, (512, 256, 128))
    tn = FFN_TN

    def body(u_ref, w1_ref, w3_ref, a_ref, b_ref, hm_ref):
        uu = u_ref[...]
        a = lax.dot_general(uu, w1_ref[...], NT, preferred_element_type=F32)
        b = lax.dot_general(uu, w3_ref[...], NT, preferred_element_type=F32)
        a_ref[...] = a.astype(a_ref.dtype)
        b_ref[...] = b.astype(b_ref.dtype)
        hm_ref[...] = (_silu(a) * b).astype(hm_ref.dtype)

    w_spec = pl.BlockSpec((tn, D_MODEL), lambda j, i: (j, 0))
    o_spec = pl.BlockSpec((tm, tn), lambda j, i: (i, j))
    o_shape = jax.ShapeDtypeStruct((t_dim, D_FF), BF16)
    return pcall(body, name=nm, grid=(D_FF // tn, t_dim // tm),
                 in_specs=[pl.BlockSpec((tm, D_MODEL), lambda j, i: (i, 0)), w_spec, w_spec],
                 out_specs=[o_spec] * 3, out_shape=[o_shape] * 3, args=[u, w1t, w3t], hook=hook)


def ffn_dgate(dout_bf, w2, a, b, nm, hook=None):
    t_dim = dout_bf.shape[0]
    tm = _pick(t_dim, (512, 256, 128))
    tn = FFN_TN

    def body(d_ref, w2_ref, a_ref, b_ref, da_ref, db_ref):
        dhm = 0.5 * lax.dot_general(d_ref[...], w2_ref[...], NT, preferred_element_type=F32)
        av = a_ref[...].astype(F32)
        bv = b_ref[...].astype(F32)
        da_ref[...] = (dhm * bv * _silu_grad(av)).astype(da_ref.dtype)
        db_ref[...] = (dhm * _silu(av)).astype(db_ref.dtype)

    t_spec = pl.BlockSpec((tm, tn), lambda j, i: (i, j))
    o_shape = jax.ShapeDtypeStruct((t_dim, D_FF), BF16)
    return pcall(body, name=nm, grid=(D_FF // tn, t_dim // tm),
                 in_specs=[pl.BlockSpec((tm, D_MODEL), lambda j, i: (i, 0)),
                           pl.BlockSpec((tn, D_MODEL), lambda j, i: (j, 0)), t_spec, t_spec],
                 out_specs=[t_spec] * 2, out_shape=[o_shape] * 2, args=[dout_bf, w2, a, b], hook=hook)


def ffn_fwd(h, g, w1t, w3t, w2, nm, hook=None):
    u, = rowmap(lambda x, gg: (_rms_fwd(x, gg),), [h], [g], [(D_MODEL, BF16)], tm=256, name=nm + "_norm")
    a, b, hm = ffn_upgate(u, w1t, w3t, nm + "_upgate", hook=hook)
    out = mm(hm, w2, res=h, alpha=0.5, name=nm + "_down")
    return out, (u, a, b, hm)


def norm_bwd(h, g, du, dout, nm):
    def fn(x, d_u, d_o, gg):
        dx, dg = _rms_bwd(x, gg, d_u)
        dh = d_o + dx
        return dh, dh, dg

    return rowmap(fn, [h, du, dout], [g], [(D_MODEL, F32), (D_MODEL, BF16)], [(1, D_MODEL)], tm=256, name=nm)


def ffn_bwd(h, g, w1t, w3t, w2, saved, dout, dout_bf, nm, hook=None):
    u, a, b, hm = saved
    dw2 = mm(hm, dout_bf, ta=True, alpha=0.5, out_dtype=BF16, name=nm + "_dw2")
    da, db = ffn_dgate(dout_bf, w2, a, b, nm + "_dgate", hook=hook)
    dw1t = mm(da, u, ta=True, out_dtype=BF16, name=nm + "_dw1")
    dw3t = mm(db, u, ta=True, out_dtype=BF16, name=nm + "_dw3")
    du = mm(da, w1t, name=nm + "_du1")
    du = mm(db, w3t, res=du, name=nm + "_du2")
    dh, dh_bf, dg = norm_bwd(h, g, du, dout, nm + "_dnorm")
    return dh, dh_bf, dg, dw1t, dw3t, dw2


def _conv_pre(x, halo, w, b, tm):
    halo = jnp.where(pl.program_id(0) > 0, halo, 0.0)
    xx = jnp.concatenate([halo, x], axis=0)
    shifted = [xx[5 + k:5 + k + tm] for k in range(SSM_CONV)]
    acc = b + shifted[0] * w[0:1]
    for k in range(1, SSM_CONV):
        acc = acc + shifted[k] * w[k:k + 1]
    return acc, shifted


def _prev_halo_spec(tm, width):
    return pl.BlockSpec((8, width), lambda i: (jnp.maximum(i * (tm // 8) - 1, 0), 0))


def conv_fwd(xbc_raw, w, b, nm):
    tm = 128

    def fn(x, halo, ww, bb):
        acc, _ = _conv_pre(x, halo, ww, bb, tm)
        return (_silu(acc),)

    out, = rowmap(fn, [xbc_raw, (xbc_raw, _prev_halo_spec(tm, SSM_CONV_DIM))], [w, b],
                  [(SSM_CONV_DIM, F32)], tm=tm, name=nm)
    return out


def conv_bwd(xbc_raw, w, b, dxs, db_in, dc_in, nm):
    tm = 128
    t_dim = xbc_raw.shape[0]

    def fn1(x, halo, d1, d2, d3, ww, bb):
        acc, shifted = _conv_pre(x, halo, ww, bb, tm)
        dacc = jnp.concatenate([d1, d2, d3], axis=1) * _silu_grad(acc)
        dw = jnp.concatenate([jnp.sum(dacc * s, axis=0, keepdims=True) for s in shifted], axis=0)
        return dacc, dw, jnp.sum(dacc, axis=0, keepdims=True)

    dacc, dw, dbias = rowmap(fn1, [xbc_raw, (xbc_raw, _prev_halo_spec(tm, SSM_CONV_DIM)), dxs, db_in, dc_in],
                             [w, b], [(SSM_CONV_DIM, F32)], [(SSM_CONV, SSM_CONV_DIM), (1, SSM_CONV_DIM)],
                             tm=tm, name=nm + "_a")
    n_tiles = t_dim // tm

    def fn2(d, nxt, ww):
        nxt = jnp.where(pl.program_id(0) < n_tiles - 1, nxt, 0.0)
        dd = jnp.concatenate([d, nxt], axis=0)
        out = dd[3:3 + tm] * ww[0:1]
        for k in range(1, SSM_CONV):
            out = out + dd[3 - k:3 - k + tm] * ww[k:k + 1]
        return (out,)

    nxt_spec = pl.BlockSpec((8, SSM_CONV_DIM), lambda i: (jnp.minimum((i + 1) * (tm // 8), t_dim // 8 - 1), 0))
    dx, = rowmap(fn2, [dacc, (dacc, nxt_spec)], [w], [(SSM_CONV_DIM, BF16)], tm=tm, name=nm + "_b")
    return dx, dw, dbias


GRP_W = SSM_D_INNER // SSM_GROUPS
HPG = SSM_HEADS // SSM_GROUPS
HEAD_SHIFT = 6


def _split2(x):
    hi = x.astype(BF16)
    return hi, (x - hi.astype(F32)).astype(BF16)


def _expand_mats():
    e = ((lax.broadcasted_iota(jnp.int32, (HPG, GRP_W), 1) >> HEAD_SHIFT)
         == lax.broadcasted_iota(jnp.int32, (HPG, GRP_W), 0)).astype(BF16)
    et = ((lax.broadcasted_iota(jnp.int32, (GRP_W, HPG), 0) >> HEAD_SHIFT)
          == lax.broadcasted_iota(jnp.int32, (GRP_W, HPG), 1)).astype(BF16)
    return e, et


def _expand(v, e_m):
    hi, lo = _split2(v)
    return jnp.dot(hi, e_m, preferred_element_type=F32) + jnp.dot(lo, e_m, preferred_element_type=F32)


def _reduce8(v, et_m):
    acc = None
    for p in _split3(v):
        t = jnp.dot(p, et_m, preferred_element_type=F32)
        acc = t if acc is None else acc + t
    return acc


def _ssd_group_terms(dt_ref, dtT_ref, arow_ref, acol_ref):
    L = SSM_CHUNK
    r = lax.broadcasted_iota(jnp.int32, (L, L), 0)
    c = lax.broadcasted_iota(jnp.int32, (L, L), 1)
    tril = (r >= c).astype(BF16)
    triu = (r <= c).astype(BF16)
    dtg = dt_ref[0]
    acol = None
    for p in _split3(dtg * arow_ref[0]):
        t = jnp.dot(tril, p, preferred_element_type=F32)
        acol = t if acol is None else acol + t
    arowT = None
    for p in _split3(dtT_ref[0] * acol_ref[0]):
        t = jnp.dot(p, triu, preferred_element_type=F32)
        arowT = t if arowT is None else arowT + t
    return dtg, acol, arowT, r >= c


def _state_decay(a_last_col, et_m):
    hi, lo = _split2(jnp.broadcast_to(jnp.exp(a_last_col), (HPG, SSM_STATE)))
    return jnp.dot(et_m, hi, preferred_element_type=F32) + jnp.dot(et_m, lo, preferred_element_type=F32)


def _ssd_specs(nc, rev):
    L, N = SSM_CHUNK, SSM_STATE
    xcols = SSM_D_INNER // LANES
    ch = (lambda c: nc - 1 - c) if rev else (lambda c: c)
    return [
        pl.BlockSpec((L, GRP_W), lambda c, g: (ch(c), g)),
        pl.BlockSpec((L, N), lambda c, g: (ch(c), xcols + g)),
        pl.BlockSpec((L, N), lambda c, g: (ch(c), xcols + SSM_GROUPS + g)),
        pl.BlockSpec((1, L, HPG), lambda c, g: (g, ch(c), 0)),
        pl.BlockSpec((1, HPG, L), lambda c, g: (g, 0, ch(c))),
        pl.BlockSpec((1, 1, HPG), lambda c, g: (g, 0, 0)),
        pl.BlockSpec((1, HPG, 1), lambda c, g: (g, 0, 0)),
        pl.BlockSpec((1, GRP_W), lambda c, g: (0, g)),
    ]


def ssd_fwd(xbc, dt_g, dtT_g, a_row, a_col, dvec, nm, hook=None):
    t_dim = xbc.shape[0]
    L, P, N = SSM_CHUNK, SSM_HEAD_DIM, SSM_STATE
    nc = t_dim // L

    def body(x_ref, b_ref, c_ref, dt_ref, dtT_ref, arow_ref, acol_ref, dvec_ref, y_ref, st_ref, s_s):
        ci = pl.program_id(0)
        g = pl.program_id(1)

        @pl.when((ci == 0) & (g == 0))
        def _():
            s_s[...] = jnp.zeros_like(s_s)

        e_m, et_m = _expand_mats()
        dtg, acol, arowT, causal = _ssd_group_terms(dt_ref, dtT_ref, arow_ref, acol_ref)
        a_last_row = acol[L - 1:L, :]
        x = x_ref[...]
        bm = b_ref[...]
        cm = c_ref[...]
        cb = _dot(cm, bm, NT)
        s = s_s[g]
        st_ref[0, 0] = s
        ea_x = _expand(jnp.exp(acol), e_m)
        dt_x = _expand(dtg, e_m)
        w_x = _expand(jnp.exp(a_last_row - acol) * dtg, e_m)
        yb = ea_x * _dot(cm, s, NT) + dvec_ref[...] * x
        xd = (x * dt_x).astype(BF16)
        for e in range(HPG):
            sl = slice(e * P, (e + 1) * P)
            lm = jnp.exp(jnp.where(causal, acol[:, e:e + 1] - arowT[e:e + 1, :], NEG))
            m = (cb * lm).astype(BF16)
            y_ref[:, sl] = yb[:, sl] + jnp.dot(m, xd[:, sl], preferred_element_type=F32)
        s_s[g] = _state_decay(arowT[:, L - 1:L], et_m) * s + _dot(x * w_x, bm, TN)

    out_specs = [
        pl.BlockSpec((L, GRP_W), lambda c, g: (c, g)),
        pl.BlockSpec((1, 1, GRP_W, N), lambda c, g: (c, g, 0, 0)),
    ]
    return pcall(
        body, name=nm, grid=(nc, SSM_GROUPS), in_specs=_ssd_specs(nc, False), out_specs=out_specs,
        out_shape=[jax.ShapeDtypeStruct((t_dim, SSM_D_INNER), F32),
                   jax.ShapeDtypeStruct((nc, SSM_GROUPS, GRP_W, N), F32)],
        scratch=[pltpu.VMEM((SSM_GROUPS, GRP_W, N), F32)],
        args=[xbc, xbc, xbc, dt_g, dtT_g, a_row, a_col, dvec], hook=hook)


def ssd_bwd(dy, xbc, dt_g, dtT_g, a_row, a_col, dvec, states, nm, hook=None):
    t_dim = xbc.shape[0]
    L, P, N = SSM_CHUNK, SSM_HEAD_DIM, SSM_STATE
    nc = t_dim // L

    def body(dy_ref, x_ref, b_ref, c_ref, dt_ref, dtT_ref, arow_ref, acol_ref, dvec_ref, st_ref,
             dx_ref, db_ref, dc_ref, da_ref, ddt_ref, dd_ref, ds_s, yd_s, dxd_s):
        ci = pl.program_id(0)
        g = pl.program_id(1)

        @pl.when((ci == 0) & (g == 0))
        def _():
            ds_s[...] = jnp.zeros_like(ds_s)
            dd_ref[...] = jnp.zeros_like(dd_ref)

        e_m, et_m = _expand_mats()
        dtg, acol, arowT, causal = _ssd_group_terms(dt_ref, dtT_ref, arow_ref, acol_ref)
        a_last_row = acol[L - 1:L, :]
        x = x_ref[...]
        dy = dy_ref[...]
        bm = b_ref[...]
        cm = c_ref[...]
        cb = _dot(cm, bm, NT)
        s = st_ref[0, 0]
        dsp = ds_s[g]
        ew8 = jnp.exp(a_last_row - acol)
        ea_x = _expand(jnp.exp(acol), e_m)
        dt_x = _expand(dtg, e_m)
        ew_x = _expand(ew8, e_m)
        w_x = ew_x * dt_x
        z = _dot(cm, s, NT)
        dz = ea_x * dy
        dc = _dot(dz, s)
        ds_y = _dot(dz, cm, TN)
        du = _dot(bm, dsp, NT)
        u = x * w_x
        db = _dot(u, dsp)
        xd = (x * dt_x).astype(BF16)
        dyb = dy.astype(BF16)
        dcb = jnp.zeros((L, L), F32)
        for e in range(HPG):
            sl = slice(e * P, (e + 1) * P)
            lm = jnp.exp(jnp.where(causal, acol[:, e:e + 1] - arowT[e:e + 1, :], NEG))
            m = (cb * lm).astype(BF16)
            yd_s[:, sl] = jnp.dot(m, xd[:, sl], preferred_element_type=F32)
            dxd_s[:, sl] = lax.dot_general(m, dyb[:, sl], TN, preferred_element_type=F32)
            dcb = dcb + lax.dot_general(dyb[:, sl], xd[:, sl], NT, preferred_element_type=F32) * lm
        dxd = dxd_s[...]
        dx_ref[...] = dvec_ref[...] * dy + du * w_x + dt_x * dxd
        ddt = _reduce8(x * (ew_x * du + dxd), et_m)
        da = (_reduce8(dz * z + dyb.astype(F32) * yd_s[...], et_m)
              - _reduce8(xd.astype(F32) * dxd + du * u, et_m))
        dwa_row = _reduce8(jnp.broadcast_to(jnp.sum(du * u, axis=0, keepdims=True), (8, GRP_W)), et_m)[0:1]
        t_nh = None
        for p in _split3(dsp * s):
            t = lax.dot_general(p, et_m, TN, preferred_element_type=F32)
            t_nh = t if t_nh is None else t_nh + t
        d_last = dwa_row + jnp.exp(a_last_row) * jnp.sum(t_nh, axis=0, keepdims=True)
        row_l = lax.broadcasted_iota(jnp.int32, (L, 1), 0)
        da_ref[0] = da + jnp.where(row_l == L - 1, d_last, 0.0)
        ddt_ref[0] = ddt
        dd_ref[g] += jnp.sum(dy * x, axis=0, keepdims=True)
        dc_ref[...] = dc + _dot(dcb, bm)
        db_ref[...] = db + _dot(dcb, cm, TN)
        ds_s[g] = _state_decay(arowT[:, L - 1:L], et_m) * dsp + ds_y

    rc = lambda c: nc - 1 - c
    in_specs = ([pl.BlockSpec((L, GRP_W), lambda c, g: (rc(c), g))] + _ssd_specs(nc, True)
                + [pl.BlockSpec((1, 1, GRP_W, N), lambda c, g: (rc(c), g, 0, 0))])
    out_specs = [
        pl.BlockSpec((L, GRP_W), lambda c, g: (rc(c), g)),
        pl.BlockSpec((L, N), lambda c, g: (rc(c), g)),
        pl.BlockSpec((L, N), lambda c, g: (rc(c), g)),
        pl.BlockSpec((1, L, HPG), lambda c, g: (g, rc(c), 0)),
        pl.BlockSpec((1, L, HPG), lambda c, g: (g, rc(c), 0)),
        pl.BlockSpec((SSM_GROUPS, 1, GRP_W), lambda c, g: (0, 0, 0)),
    ]
    gn = SSM_GROUPS * N
    out_shape = [
        jax.ShapeDtypeStruct((t_dim, SSM_D_INNER), F32), jax.ShapeDtypeStruct((t_dim, gn), F32),
        jax.ShapeDtypeStruct((t_dim, gn), F32), jax.ShapeDtypeStruct((SSM_GROUPS, t_dim, HPG), F32),
        jax.ShapeDtypeStruct((SSM_GROUPS, t_dim, HPG), F32), jax.ShapeDtypeStruct((SSM_GROUPS, 1, GRP_W), F32),
    ]
    return pcall(
        body, name=nm, grid=(nc, SSM_GROUPS), in_specs=in_specs, out_specs=out_specs, out_shape=out_shape,
        scratch=[pltpu.VMEM((SSM_GROUPS, GRP_W, N), F32), pltpu.VMEM((L, GRP_W), F32), pltpu.VMEM((L, GRP_W), F32)],
        args=[dy, xbc, xbc, xbc, dt_g, dtT_g, a_row, a_col, dvec, states], hook=hook)


def _softplus(x):
    return jnp.maximum(x, 0.0) + jnp.log(1.0 + jnp.exp(-jnp.abs(x)))


def ssd_dt_bwd(da, ddt, dt, dt_raw, a_row, dt_bias, nm):
    L = SSM_CHUNK

    def fn(d_a, d_dt, dtv, raw, ar, bias):
        r = lax.broadcasted_iota(jnp.int32, (L, L), 0)
        c = lax.broadcasted_iota(jnp.int32, (L, L), 1)
        triu = (r <= c).astype(BF16)
        acc = None
        for p in _split3(d_a):
            t = jnp.dot(triu, p, preferred_element_type=F32)
            acc = t if acc is None else acc + t
        d_dt = d_dt + acc * ar
        d_a_h = jnp.sum(acc * dtv, axis=0, keepdims=True)
        d_raw = d_dt * _sigmoid(raw + bias)
        return d_raw, d_a_h, jnp.sum(d_raw, axis=0, keepdims=True)

    return rowmap(fn, [da, ddt, dt, dt_raw], [a_row, dt_bias], [(SSM_HEADS, BF16)],
                  [(1, SSM_HEADS), (1, SSM_HEADS)], tm=L, name=nm)


GN_W = SSM_D_INNER // SSM_GROUPS


def mamba_fwd(h, p, nm, hook=None):
    u, = rowmap(lambda x, gg: (_rms_fwd(x, gg),), [h], [p["ssm_norm"]], [(D_MODEL, BF16)], tm=256, name=nm + "_norm")
    z = mm(u, p["w_zt"], tb=True, name=nm + "_z")
    xbc_raw = mm(u, p["w_xbct"], tb=True, name=nm + "_xbc")
    dt_raw = mm(u, p["w_dtt"], tb=True, name=nm + "_dt")
    xbc = conv_fwd(xbc_raw, p["conv_w"], p["conv_b"], nm + "_conv")
    dt, = rowmap(lambda r, b: (_softplus(r + b),), [dt_raw], [p["dt_bias"]], [(SSM_HEADS, F32)], tm=256,
                 name=nm + "_softplus")
    dt_g = dt.reshape(-1, SSM_GROUPS, HPG).transpose(1, 0, 2)
    dtT_g = dt_g.transpose(0, 2, 1)
    y, states = ssd_fwd(xbc, dt_g, dtT_g, p["a_row"], p["a_col"], p["dvec"], nm + "_ssd", hook=hook)

    def gate_norm(yv, zv, gg):
        t = yv * _silu(zv)
        return (jnp.concatenate([_rms_fwd(t[:, k * GN_W:(k + 1) * GN_W], gg[:, k * GN_W:(k + 1) * GN_W])
                                 for k in range(SSM_GROUPS)], axis=1),)

    yn, = rowmap(gate_norm, [y, z], [p["gate_norm"]], [(SSM_D_INNER, BF16)], tm=256, name=nm + "_gatenorm")
    out = mm(yn, p["w_out"], res=h, name=nm + "_out")
    return out, (u, z, xbc_raw, dt_raw, xbc, dt, dt_g, dtT_g, y, states, yn)


def mamba_bwd(h, p, saved, dout, dout_bf, nm, hook=None):
    u, z, xbc_raw, dt_raw, xbc, dt, dt_g, dtT_g, y, states, yn = saved
    g = {}
    g["w_out"] = mm(yn, dout_bf, ta=True, out_dtype=BF16, name=nm + "_dwout")
    dyn = mm(dout_bf, p["w_out"], tb=True, name=nm + "_dyn")

    def gate_norm_bwd(d, yv, zv, gg):
        sz = _silu(zv)
        t = yv * sz
        dts, dgs = [], []
        for k in range(SSM_GROUPS):
            sl = slice(k * GN_W, (k + 1) * GN_W)
            dt_k, dg_k = _rms_bwd(t[:, sl], gg[:, sl], d[:, sl])
            dts.append(dt_k)
            dgs.append(dg_k)
        d_t = jnp.concatenate(dts, axis=1)
        return d_t * sz, d_t * yv * _silu_grad(zv), jnp.concatenate(dgs, axis=1)

    dy, dz, g["gate_norm"] = rowmap(gate_norm_bwd, [dyn, y, z], [p["gate_norm"]],
                                    [(SSM_D_INNER, F32), (SSM_D_INNER, BF16)], [(1, SSM_D_INNER)], tm=256,
                                    name=nm + "_dgatenorm")
    dxs, db_in, dc_in, da_g, ddt_g, dd = ssd_bwd(
        dy, xbc, dt_g, dtT_g, p["a_row"], p["a_col"], p["dvec"], states, nm + "_dssd", hook=hook)
    g["dvec"] = dd
    per_head = lambda t: t.transpose(1, 0, 2).reshape(-1, SSM_HEADS)
    ddt_raw, g["a"], g["dt_bias"] = ssd_dt_bwd(per_head(da_g), per_head(ddt_g), dt, dt_raw, p["a_heads"],
                                               p["dt_bias"], nm + "_ddt")
    dxbc_raw, g["conv_w"], g["conv_b"] = conv_bwd(xbc_raw, p["conv_w"], p["conv_b"], dxs, db_in, dc_in, nm + "_dconv")
    g["w_zt"] = mm(dz, u, ta=True, out_dtype=BF16, name=nm + "_dwz")
    g["w_xbct"] = mm(dxbc_raw, u, ta=True, out_dtype=BF16, name=nm + "_dwxbc")
    g["w_dtt"] = mm(ddt_raw, u, ta=True, out_dtype=BF16, name=nm + "_dwdt")
    du = mm(dz, p["w_zt"], name=nm + "_du1")
    du = mm(dxbc_raw, p["w_xbct"], res=du, name=nm + "_du2")
    du = mm(ddt_raw, p["w_dtt"], res=du, name=nm + "_du3")
    dh, dh_bf, g["ssm_norm"] = norm_bwd(h, p["ssm_norm"], du, dout, nm + "_dnorm")
    return dh, dh_bf, g


KV_W = ATT_KV_HEADS * ATT_HEAD_DIM


def kv_fwd(h, p, nm):
    u, = rowmap(lambda x, gg: (_rms_fwd(x, gg),), [h], [p["kv_norm"]], [(D_MODEL, BF16)], tm=256, name=nm + "_norm")
    kv_raw = mm(u, p["w_kv"], name=nm + "_proj")

    def knorm(t, gg):
        ks = [_rms_fwd(t[:, j * ATT_HEAD_DIM:(j + 1) * ATT_HEAD_DIM], gg) for j in range(ATT_KV_HEADS)]
        return jnp.concatenate(ks, axis=1), t[:, KV_W:]

    k, v = rowmap(knorm, [kv_raw], [p["k_norm"]], [(KV_W, F32), (KV_W, F32)], tm=256, name=nm + "_knorm")
    return k, v, (u, kv_raw)


def kv_bwd(h, p, saved, dk_cur, dk_prev, dv_cur, dv_prev, dout, nm):
    u, kv_raw = saved
    t_dim = h.shape[0]
    tm = ATT_WINDOW
    nb = t_dim // tm
    nxt = pl.BlockSpec((tm, KV_W), lambda i: (jnp.minimum(i + 1, nb - 1), 0))

    def fn(dkc, dkp, dvc, dvp, t, gg):
        live = pl.program_id(0) < nb - 1
        dk = dkc + jnp.where(live, dkp, 0.0)
        dv = dvc + jnp.where(live, dvp, 0.0)
        outs, dgs = [], None
        for j in range(ATT_KV_HEADS):
            sl = slice(j * ATT_HEAD_DIM, (j + 1) * ATT_HEAD_DIM)
            dx, dg = _rms_bwd(t[:, sl], gg, dk[:, sl])
            outs.append(dx)
            dgs = dg if dgs is None else dgs + dg
        return jnp.concatenate(outs + [dv], axis=1), dgs

    dkv_raw, dknorm = rowmap(fn, [dk_cur, (dk_prev, nxt), dv_cur, (dv_prev, nxt), kv_raw], [p["k_norm"]],
                             [(2 * KV_W, BF16)], [(1, ATT_HEAD_DIM)], tm=tm, name=nm + "_dknorm")
    g = {"k_norm": dknorm}
    g["w_kv"] = mm(u, dkv_raw, ta=True, out_dtype=BF16, name=nm + "_dwkv")
    du = mm(dkv_raw, p["w_kv"], tb=True, name=nm + "_du")
    dh, dh_bf, g["kv_norm"] = norm_bwd(h, p["kv_norm"], du, dout, nm + "_dnorm")
    return dh, dh_bf, g


def _attn_scores(q_ref, kp_ref, kc_ref, vp_ref, vc_ref, qn_ref, bias_ref, sink_ref, kv):
    hd = ATT_HEAD_DIM
    blk = ATT_WINDOW
    sl = slice(kv * hd, (kv + 1) * hd)
    kk = jnp.concatenate([kp_ref[:, sl], kc_ref[:, sl]], axis=0)
    vv = jnp.concatenate([vp_ref[:, sl], vc_ref[:, sl]], axis=0)
    gq = qn_ref[...]
    raws, rinvs = [], []
    for r in range(ATT_GROUP):
        hh = kv * ATT_GROUP + r
        x = q_ref[:, hh * hd:(hh + 1) * hd]
        raws.append(x)
        rinvs.append(lax.rsqrt(jnp.mean(x * x, axis=-1, keepdims=True) + EPS))
    xh = jnp.concatenate([x * ri for x, ri in zip(raws, rinvs)], axis=0)
    rinv = jnp.concatenate(rinvs, axis=0)
    q8 = xh * gq
    s = _dot(q8, kk, NT) * (hd ** -0.5) + bias_ref[kv]
    colk = lax.broadcasted_iota(jnp.int32, (1, 2 * blk), 1)
    s = jnp.where((pl.program_id(0) > 0) | (colk >= blk), s, NEG)
    sink = sink_ref[kv]
    m = jnp.maximum(jnp.max(s, axis=-1, keepdims=True), sink)
    pexp = jnp.exp(s - m)
    e_sink = jnp.exp(sink - m)
    den = jnp.sum(pexp, axis=-1, keepdims=True) + e_sink
    prob = pexp / den
    return kk, vv, xh, rinv, q8, prob, e_sink / den


def _attn_specs(nb):
    blk = ATT_WINDOW
    cur = lambda i: (i, 0)
    prev = lambda i: (jnp.maximum(i - 1, 0), 0)
    return [
        pl.BlockSpec((blk, D_MODEL), cur),
        pl.BlockSpec((blk, KV_W), prev), pl.BlockSpec((blk, KV_W), cur),
        pl.BlockSpec((blk, KV_W), prev), pl.BlockSpec((blk, KV_W), cur),
        pl.BlockSpec((1, ATT_HEAD_DIM), lambda i: (0, 0)),
        pl.BlockSpec((ATT_KV_HEADS, ATT_GROUP * blk, 2 * blk), lambda i: (0, 0, 0)),
        pl.BlockSpec((ATT_KV_HEADS, ATT_GROUP * blk, 1), lambda i: (0, 0, 0)),
    ]


def attn_fwd(q_raw, k, v, q_norm, bias, sink_col, nm):
    t_dim = q_raw.shape[0]
    blk, hd = ATT_WINDOW, ATT_HEAD_DIM
    nb = t_dim // blk

    def body(q_ref, kp_ref, kc_ref, vp_ref, vc_ref, qn_ref, bias_ref, sink_ref, o_ref):
        for kv in range(ATT_KV_HEADS):
            kk, vv, xh, rinv, q8, prob, p_sink = _attn_scores(q_ref, kp_ref, kc_ref, vp_ref, vc_ref, qn_ref,
                                                              bias_ref, sink_ref, kv)
            o8 = _dot(prob, vv)
            for r in range(ATT_GROUP):
                hh = kv * ATT_GROUP + r
                o_ref[:, hh * hd:(hh + 1) * hd] = o8[r * blk:(r + 1) * blk].astype(o_ref.dtype)

    out, = pcall(body, name=nm, grid=(nb,), in_specs=_attn_specs(nb),
                 out_specs=[pl.BlockSpec((blk, D_MODEL), lambda i: (i, 0))],
                 out_shape=[jax.ShapeDtypeStruct((t_dim, D_MODEL), BF16)],
                 args=[q_raw, k, k, v, v, q_norm, bias, sink_col])
    return out


def attn_bwd(do, q_raw, k, v, q_norm, bias, sink_col, nm, hook=None):
    t_dim = q_raw.shape[0]
    blk, hd = ATT_WINDOW, ATT_HEAD_DIM
    nb = t_dim // blk
    scale = hd ** -0.5

    def body(do_ref, q_ref, kp_ref, kc_ref, vp_ref, vc_ref, qn_ref, bias_ref, sink_ref,
             dq_ref, dkc_ref, dkp_ref, dvc_ref, dvp_ref, dbias_ref, dsink_ref, dqn_ref):
        @pl.when(pl.program_id(0) == 0)
        def _():
            dbias_ref[...] = jnp.zeros_like(dbias_ref)
            dsink_ref[...] = jnp.zeros_like(dsink_ref)
            dqn_ref[...] = jnp.zeros_like(dqn_ref)

        gq = qn_ref[...]
        for kv in range(ATT_KV_HEADS):
            kk, vv, xh, rinv, q8, prob, p_sink = _attn_scores(q_ref, kp_ref, kc_ref, vp_ref, vc_ref, qn_ref,
                                                              bias_ref, sink_ref, kv)
            do8 = jnp.concatenate([do_ref[:, (kv * ATT_GROUP + r) * hd:(kv * ATT_GROUP + r + 1) * hd]
                                   for r in range(ATT_GROUP)], axis=0)
            dp = _dot(do8, vv, NT)
            delta = jnp.sum(prob * dp, axis=-1, keepdims=True)
            ds = prob * (dp - delta)
            dsink_ref[kv] += -p_sink * delta
            dbias_ref[kv] += ds
            ds_s = ds * scale
            dq8 = _dot(ds_s, kk)
            dkk = _dot(ds_s, q8, TN)
            dvv = _dot(prob, do8, TN)
            dqn_ref[...] += jnp.sum(dq8 * xh, axis=0, keepdims=True)
            dxh = dq8 * gq
            dq_raw8 = rinv * (dxh - xh * jnp.mean(dxh * xh, axis=-1, keepdims=True))
            for r in range(ATT_GROUP):
                hh = kv * ATT_GROUP + r
                dq_ref[:, hh * hd:(hh + 1) * hd] = dq_raw8[r * blk:(r + 1) * blk].astype(dq_ref.dtype)
            sl = slice(kv * hd, (kv + 1) * hd)
            dkp_ref[:, sl] = dkk[:blk]
            dkc_ref[:, sl] = dkk[blk:]
            dvp_ref[:, sl] = dvv[:blk]
            dvc_ref[:, sl] = dvv[blk:]

    cur = lambda i: (i, 0)
    row_spec = pl.BlockSpec((blk, KV_W), cur)
    out_specs = [
        pl.BlockSpec((blk, D_MODEL), cur), row_spec, row_spec, row_spec, row_spec,
        pl.BlockSpec((ATT_KV_HEADS, ATT_GROUP * blk, 2 * blk), lambda i: (0, 0, 0)),
        pl.BlockSpec((ATT_KV_HEADS, ATT_GROUP * blk, 1), lambda i: (0, 0, 0)),
        pl.BlockSpec((1, hd), lambda i: (0, 0)),
    ]
    kvs = jax.ShapeDtypeStruct((t_dim, KV_W), F32)
    out_shape = [
        jax.ShapeDtypeStruct((t_dim, D_MODEL), BF16), kvs, kvs, kvs, kvs,
        jax.ShapeDtypeStruct((ATT_KV_HEADS, ATT_GROUP * blk, 2 * blk), F32),
        jax.ShapeDtypeStruct((ATT_KV_HEADS, ATT_GROUP * blk, 1), F32),
        jax.ShapeDtypeStruct((1, hd), F32),
    ]
    return pcall(body, name=nm, grid=(nb,), in_specs=[pl.BlockSpec((blk, D_MODEL), cur)] + _attn_specs(nb),
                 out_specs=out_specs, out_shape=out_shape,
                 args=[do, q_raw, k, k, v, v, q_norm, bias, sink_col], hook=hook)


def _t5_bucket_np():
    blk = ATT_WINDOW
    qi = np.arange(blk)[:, None] + blk
    kj = np.arange(2 * blk)[None, :]
    dist = qi - kj
    n = np.maximum(dist, 0)
    max_exact = REL_BUCKETS // 2
    nf = np.maximum(n, 1).astype(np.float32)
    large = max_exact + (np.log(nf / max_exact) / math.log(ATT_WINDOW / max_exact)
                         * (REL_BUCKETS - max_exact)).astype(np.int32)
    large = np.minimum(large, REL_BUCKETS - 1)
    bucket = np.where(n < max_exact, n, large)
    in_window = (dist >= 0) & (dist < ATT_WINDOW)
    return bucket, in_window


def attn_block_fwd(h, k, v, p, nm):
    u, = rowmap(lambda x, gg: (_rms_fwd(x, gg),), [h], [p["attn_norm"]], [(D_MODEL, BF16)], tm=256, name=nm + "_norm")
    q_raw = mm(u, p["w_q"], name=nm + "_q")
    o = attn_fwd(q_raw, k, v, p["q_norm"], p["bias"], p["sink_col"], nm + "_core")
    out = mm(o, p["w_o"], res=h, name=nm + "_o")
    return out, (u, q_raw, o)


def attn_block_bwd(h, k, v, p, saved, dout, dout_bf, nm, hook=None):
    u, q_raw, o = saved
    g = {}
    g["w_o"] = mm(o, dout_bf, ta=True, out_dtype=BF16, name=nm + "_dwo")
    do = mm(dout_bf, p["w_o"], tb=True, name=nm + "_do")
    dq_raw, dkc, dkp, dvc, dvp, g["bias"], g["sink_col"], g["q_norm"] = attn_bwd(
        do, q_raw, k, v, p["q_norm"], p["bias"], p["sink_col"], nm + "_dcore", hook=hook)
    g["w_q"] = mm(u, dq_raw, ta=True, out_dtype=BF16, name=nm + "_dwq")
    du = mm(dq_raw, p["w_q"], tb=True, name=nm + "_du")
    dh, dh_bf, g["attn_norm"] = norm_bwd(h, p["attn_norm"], du, dout, nm + "_dnorm")
    return dh, dh_bf, g, (dkc, dkp, dvc, dvp)


FFN_TAGS = ["00", "01", "10", "11"]


def local_step(x, target, small, io):
    bucket, in_window = _t5_bucket_np()
    blk = ATT_WINDOW
    w = small

    def ffn_w(tag):
        return io.w("w1t_" + tag), io.w("w3t_" + tag), io.w("w2_" + tag)

    fnorm = {tag: w["ffn_norm"][int(tag[0]), int(tag[1])][None, :] for tag in FFN_TAGS}
    a_neg = -jnp.exp(w["ssm_a_log"][0])

    def mamba_p():
        w_int = io.w("w_int")
        return dict(ssm_norm=w["ssm_norm"], w_zt=w_int[:SSM_D_INNER],
                    w_xbct=w_int[SSM_D_INNER:SSM_D_INNER + SSM_CONV_DIM], w_dtt=w_int[SSM_D_INNER + SSM_CONV_DIM:],
                    conv_w=w["ssm_conv_w"][0], conv_b=w["ssm_conv_b"], dt_bias=w["ssm_dt_bias"],
                    a_heads=a_neg[None, :], a_row=a_neg.reshape(SSM_GROUPS, 1, HPG),
                    a_col=a_neg.reshape(SSM_GROUPS, HPG, 1),
                    dvec=jnp.repeat(w["ssm_d"][0], SSM_HEAD_DIM)[None, :],
                    gate_norm=w["ssm_gate_norm"], w_out=io.w("w_out"))

    rb = w["rel_bias"]
    onehot3 = (np.arange(REL_BUCKETS)[:, None, None] == bucket[None]).astype(np.float32)
    bias = jnp.einsum("bh,bqk->hqk", rb, onehot3, precision=lax.Precision.HIGHEST)
    bias = jnp.where(in_window[None], bias, NEG)
    bias = bias.reshape(ATT_KV_HEADS, ATT_GROUP * blk, 2 * blk)
    sink_col = jnp.repeat(w["sinks"][0], blk).reshape(ATT_KV_HEADS, ATT_GROUP * blk, 1)

    def attn_p():
        return dict(attn_norm=w["attn_norm"], w_q=io.w("w_q"), q_norm=w["q_norm"], bias=bias, sink_col=sink_col,
                    w_o=io.w("w_o"))

    def kv_p():
        return dict(kv_norm=w["kv_norm"][None, :], w_kv=io.w("w_kv"), k_norm=w["k_norm"][None, :])

    h0 = x
    h0a, s_f00 = ffn_fwd(h0, fnorm["00"], *ffn_w("00"), "f00", hook=io.hook("f00_upgate"))
    mp = mamba_p()
    h0b, s_m = mamba_fwd(h0a, mp, "ssm", hook=io.hook("ssm_ssd"))
    h1, s_f01 = ffn_fwd(h0b, fnorm["01"], *ffn_w("01"), "f01", hook=io.hook("f01_upgate"))
    kp = kv_p()
    k, v, s_kv = kv_fwd(h1, kp, "kv")
    h1a, s_f10 = ffn_fwd(h1, fnorm["10"], *ffn_w("10"), "f10", hook=io.hook("f10_upgate"))
    ap = attn_p()
    h1b, s_a = attn_block_fwd(h1a, k, v, ap, "att")
    h2, s_f11 = ffn_fwd(h1b, fnorm["11"], *ffn_w("11"), "f11")

    def loss_fn(y, t):
        e = y - t
        d = e * (1.0 / D_MODEL)
        return d, d, jnp.sum(e * e, axis=0, keepdims=True)

    dh, dh_bf, sq = rowmap(loss_fn, [h2, target], [], [(D_MODEL, F32), (D_MODEL, BF16)], [(1, D_MODEL)], tm=256,
                           name="loss")
    loss_part = jnp.sum(sq) * (0.5 / D_MODEL)

    fg = {}

    def ffn_back(tag, h_in, saved, dh, dh_bf, site):
        dh, dh_bf, dg, dw1t, dw3t, dw2 = ffn_bwd(h_in, fnorm[tag], *ffn_w(tag), saved, dh, dh_bf, "f" + tag,
                                                 hook=io.hook(site))
        fg[tag] = dg[0]
        io.put("w1t_" + tag, dw1t)
        io.put("w3t_" + tag, dw3t)
        io.put("w2_" + tag, dw2)
        return dh, dh_bf

    dh, dh_bf = ffn_back("11", h1b, s_f11, dh, dh_bf, "f11_dgate")
    dh, dh_bf, ga, dkv = attn_block_bwd(h1a, k, v, ap, s_a, dh, dh_bf, "att", hook=io.hook("att_dcore"))
    io.put("w_q", ga["w_q"])
    io.put("w_o", ga["w_o"])
    dh, dh_bf = ffn_back("10", h1, s_f10, dh, dh_bf, "f10_dgate")
    dh, dh_bf, gk = kv_bwd(h1, kp, s_kv, *dkv, dh, "kv")
    io.put("w_kv", gk["w_kv"])
    dh, dh_bf = ffn_back("01", h0b, s_f01, dh, dh_bf, "f01_dgate")
    dh, dh_bf, gm = mamba_bwd(h0a, mp, s_m, dh, dh_bf, "ssm", hook=io.hook("ssm_dssd"))
    io.put("w_int", jnp.concatenate([gm["w_zt"], gm["w_xbct"], gm["w_dtt"]], axis=0))
    io.put("w_out", gm["w_out"])
    dh, dh_bf = ffn_back("00", h0, s_f00, dh, dh_bf, "f00_dgate")
    grad_x = dh

    grads = {}
    grads["ffn_norm"] = jnp.stack([fg[tag] for tag in FFN_TAGS]).reshape(2, 2, D_MODEL)
    grads["ssm_norm"] = gm["ssm_norm"]
    grads["ssm_conv_w"] = gm["conv_w"][None]
    grads["ssm_conv_b"] = gm["conv_b"]
    grads["ssm_dt_bias"] = gm["dt_bias"]
    grads["ssm_a_log"] = gm["a"] * a_neg[None, :]
    grads["ssm_d"] = jnp.sum(gm["dvec"].reshape(SSM_HEADS, SSM_HEAD_DIM), axis=1)[None, :]
    grads["ssm_gate_norm"] = gm["gate_norm"]
    grads["kv_norm"] = gk["kv_norm"][0]
    grads["k_norm"] = gk["k_norm"][0]
    grads["attn_norm"] = ga["attn_norm"]
    grads["q_norm"] = ga["q_norm"]
    grads["sinks"] = jnp.sum(ga["sink_col"].reshape(ATT_HEADS, blk), axis=1)[None, :]
    onehot = (np.arange(REL_BUCKETS)[:, None] == bucket.reshape(1, -1)).astype(np.float32)
    dbias2d = ga["bias"].reshape(ATT_HEADS, blk * 2 * blk)
    grads["rel_bias"] = mm(jnp.asarray(onehot, BF16), dbias2d, tb=True, name="drelbias")
    return loss_part, grad_x, grads


def _adamw(g, w, m, v):
    m = ADAM_B1 * m + (1.0 - ADAM_B1) * g
    v = ADAM_B2 * v + (1.0 - ADAM_B2) * (g * g)
    m_hat = m / (1.0 - ADAM_B1 ** ADAM_STEP)
    v_hat = v / (1.0 - ADAM_B2 ** ADAM_STEP)
    delta = -ADAM_LR * (m_hat / (jnp.sqrt(v_hat) + ADAM_EPS) + ADAM_WD * w)
    return delta, m, v


def _slot_sum(r):
    g = r[0].astype(F32)
    for d in range(1, N_DEV):
        g = g + r[d].astype(F32)
    return g


def adamw_rows(recvs, w, m, v, name):
    n_l, rows, width = w.shape
    tr = 32
    assert rows % tr == 0, rows
    nt = rows // tr

    def body(*refs):
        r_refs = refs[:n_l]
        w_ref, m_ref, v_ref, g_o, d_o, m_o, v_o = refs[n_l:]
        li = pl.program_id(0)
        for k in range(n_l):
            @pl.when(li == k)
            def _(k=k):
                g = _slot_sum(r_refs[k])
                delta, m2, v2 = _adamw(g, w_ref[0], m_ref[0], v_ref[0])
                g_o[0] = g
                d_o[0] = delta
                m_o[0] = m2
                v_o[0] = v2

    def r_spec(k):
        return pl.BlockSpec((N_DEV, tr, width),
                            lambda li, j: (0, jnp.where(li == k, j, jnp.where(li > k, nt - 1, 0)), 0))

    w_spec = pl.BlockSpec((1, tr, width), lambda li, j: (li, j, 0))
    shp = jax.ShapeDtypeStruct(w.shape, F32)
    return pcall(body, name=name, grid=(n_l, nt), in_specs=[r_spec(k) for k in range(n_l)] + [w_spec] * 3,
                 out_specs=[w_spec] * 4, out_shape=[shp] * 4, args=list(recvs) + [w, m, v])


def adamw_cols(recvs, w, m, v, name):
    n_l, rows, n = w.shape
    tr = 256
    nt = rows // tr

    def body(*refs):
        r_refs = refs[:n_l]
        w_ref, m_ref, v_ref, g_o, d_o, m_o, v_o = refs[n_l:]
        li = pl.program_id(0)
        for k in range(n_l):
            @pl.when(li == k)
            def _(k=k):
                g = _slot_sum(r_refs[k]).T
                delta, m2, v2 = _adamw(g, w_ref[0], m_ref[0], v_ref[0])
                g_o[0] = g
                d_o[0] = delta
                m_o[0] = m2
                v_o[0] = v2

    def r_spec(k):
        return pl.BlockSpec((N_DEV, n, tr),
                            lambda li, j: (0, 0, jnp.where(li == k, j, jnp.where(li > k, nt - 1, 0))))

    w_spec = pl.BlockSpec((1, tr, n), lambda li, j: (li, j, 0))
    shp = jax.ShapeDtypeStruct(w.shape, F32)
    return pcall(body, name=name, grid=(n_l, nt), in_specs=[r_spec(k) for k in range(n_l)] + [w_spec] * 3,
                 out_specs=[w_spec] * 4, out_shape=[shp] * 4, args=list(recvs) + [w, m, v])


WEIGHT_NAMES = ["ffn_norm", "ffn_w1", "ffn_w3", "ffn_w2", "ssm_norm", "ssm_w_in", "ssm_conv_w", "ssm_conv_b",
                "ssm_dt_bias", "ssm_a_log", "ssm_d", "ssm_gate_norm", "ssm_w_out", "kv_norm", "w_kv", "k_norm",
                "attn_norm", "w_q", "q_norm", "sinks", "w_o", "rel_bias"]

SMALL = [
    ("ffn_norm", (2, 2, 1024), 2), ("ssm_norm", (1, 1024), 1), ("ssm_conv_w", (1, 4, 3072), 2),
    ("ssm_conv_b", (1, 3072), 1), ("ssm_gate_norm", (1, 2048), 1),
    ("ssm_dt_bias", (1, 32), None), ("ssm_a_log", (1, 32), None), ("ssm_d", (1, 32), None),
    ("kv_norm", (1024,), None), ("k_norm", (64,), None), ("attn_norm", (1, 1024), None),
    ("q_norm", (1, 64), None), ("sinks", (1, 16), None), ("rel_bias", (32, 16), None),
]
SMALL_W = 1024
SMALL_FULL_ROWS = 32
SMALL_LOCAL_ROWS = 48

GATHER_PLAN = {
    "f00_upgate": ["w_int", "w_out"],
    "ssm_ssd": ["w1t_01", "w3t_01", "w2_01", "w_kv", "w_q", "w_o"],
    "f01_upgate": ["w1t_10", "w3t_10", "w2_10"],
    "f10_upgate": ["w1t_11", "w3t_11", "w2_11"],
}
FIRST_GATHER = ["w1t_00", "w3t_00", "w2_00"]
SCATTER_PLAN = {
    "att_dcore": ["w1t_11", "w3t_11", "w2_11"],
    "ssm_dssd": ["w_q", "w_o", "w1t_10", "w3t_10", "w2_10", "w_kv", "w1t_01", "w3t_01", "w2_01"],
    "f00_dgate": ["w_int", "w_out"],
}
LAST_SCATTER = ["w1t_00", "w3t_00", "w2_00"]
SLOT_MAJOR = ("w_int",)


def _shard_shape(s, a):
    return s[:a] + (s[a] // N_DEV,) + s[a + 1:]


def _unshard_view(stack, shard_shape, axis):
    moved = jnp.moveaxis(stack, 0, axis)
    return moved.reshape(shard_shape[:axis] + (N_DEV * shard_shape[axis],) + shard_shape[axis + 1:])


def _small_local(arrs):
    flat = jnp.concatenate([arrs[n].reshape(-1) for n, _, _ in SMALL])
    return jnp.pad(flat, (0, SMALL_LOCAL_ROWS * LANES - flat.shape[0])).reshape(SMALL_LOCAL_ROWS, LANES)


class StepIO:
    def __init__(self, pieces):
        self.pieces = pieces
        self.full = {}
        self.grad = {}
        self.recv = {}

    def gather_comm(self, names, extra=()):
        items = [("g", self.pieces[n], None if n in SLOT_MAJOR else 0) for n in names] + list(extra)

        def sink(outs):
            for n, o in zip(names, outs):
                self.full[n] = o.reshape((-1,) + o.shape[2:]) if n in SLOT_MAJOR else o
            self.extra_out = list(outs[len(names):])

        return Comm(items), sink

    def scatter_comm(self, names, extra=()):
        items = []
        for n in names:
            g = self.grad[n]
            if n in SLOT_MAJOR:
                items.append(("s", g.reshape((N_DEV, g.shape[0] // N_DEV) + g.shape[1:]), None))
            else:
                items.append(("s", g, 0))
        items += list(extra)

        def sink(outs):
            for n, o in zip(names, outs):
                self.recv[n] = o
            self.extra_out = list(outs[len(names):])

        return Comm(items), sink

    def w(self, name):
        return self.full[name]

    def put(self, name, g):
        self.grad[name] = g

    def hook(self, site):
        if site in GATHER_PLAN:
            return self.gather_comm(GATHER_PLAN[site])
        if site in SCATTER_PLAN:
            return self.scatter_comm(SCATTER_PLAN[site])
        return None


def step(x, target, wts, ms, vs):
    me = _my_index()

    pieces = {}
    for li in range(2):
        for hi in range(2):
            tag = "%d%d" % (li, hi)
            pieces["w1t_" + tag] = wts["ffn_w1"][li, hi].T.astype(BF16)
            pieces["w3t_" + tag] = wts["ffn_w3"][li, hi].T.astype(BF16)
            pieces["w2_" + tag] = wts["ffn_w2"][li, hi].astype(BF16)
    pieces["w_int"] = wts["ssm_w_in"][0].T.astype(BF16)
    pieces["w_out"] = wts["ssm_w_out"][0].astype(BF16)
    pieces["w_kv"] = wts["w_kv"].astype(BF16)
    pieces["w_q"] = wts["w_q"][0].astype(BF16)
    pieces["w_o"] = wts["w_o"][0].astype(BF16)
    io = StepIO(pieces)

    small_sharded = [(n, s, a) for n, s, a in SMALL if a is not None]
    loc = jnp.concatenate([wts[n].reshape(-1) for n, _, _ in small_sharded])
    loc_rows = -(-loc.shape[0] // (8 * LANES)) * 8
    loc = jnp.pad(loc, (0, loc_rows * LANES - loc.shape[0])).reshape(loc_rows, LANES)
    comm, sink = io.gather_comm(FIRST_GATHER, extra=[("g", loc, None)])
    sink(comm_only(comm, "gather_first"))
    gath_small = io.extra_out[0].reshape(N_DEV, -1)
    small = {}
    off = 0
    for n, s, a in small_sharded:
        shard = _shard_shape(s, a)
        cnt = int(np.prod(shard))
        small[n] = _unshard_view(gath_small[:, off:off + cnt].reshape((N_DEV,) + shard), shard, a)
        off += cnt
    for n, s, a in SMALL:
        if a is None:
            small[n] = wts[n]

    loss_part, grad_x, g_small_local = local_step(x[0], target[0], small, io)
    loss = lax.psum(loss_part, ("x", "y", "c"))

    small_flat = jnp.concatenate([g_small_local[n].reshape(-1) for n, _, _ in SMALL])
    small_buf = jnp.pad(small_flat, (0, SMALL_FULL_ROWS * SMALL_W - small_flat.shape[0]))
    small_buf = small_buf.reshape(SMALL_FULL_ROWS, SMALL_W)
    comm, sink = io.scatter_comm(LAST_SCATTER, extra=[("g", small_buf, None)])
    sink(comm_only(comm, "exchange_last"))
    small_all = io.extra_out[0]
    def sum_body(r_ref, o_ref):
        o_ref[...] = _slot_sum(r_ref)

    vmem = pl.BlockSpec(memory_space=pltpu.VMEM)
    small_sum, = pcall(sum_body, name="sum_small", grid=(), in_specs=[vmem], out_specs=[vmem],
                       out_shape=[jax.ShapeDtypeStruct((SMALL_FULL_ROWS, SMALL_W), F32)], args=[small_all])
    small_sum = small_sum.reshape(-1)
    g_small = {}
    off = 0
    for n, s, a in SMALL:
        cnt = int(np.prod(s))
        gfull = small_sum[off:off + cnt].reshape(s)
        off += cnt
        if a is None:
            g_small[n] = gfull
        else:
            width = s[a] // N_DEV
            g_small[n] = lax.dynamic_slice_in_dim(gfull, me * width, width, axis=a)

    out = {}

    def emit(name, res, shape):
        for kind, arr in zip(("grad", "delta", "new_m", "new_v"), res):
            out[kind + "_" + name] = arr.reshape(shape)

    for name, key in (("ffn_w1", "w1t_"), ("ffn_w3", "w3t_")):
        shp = wts[name].shape
        view = lambda t: t.reshape((4,) + shp[2:])
        res = adamw_cols([io.recv[key + tag] for tag in FFN_TAGS], view(wts[name]), view(ms[name]), view(vs[name]),
                         "adamw_" + name)
        emit(name, res, shp)
    shp = wts["ffn_w2"].shape
    view = lambda t: t.reshape((4,) + shp[2:])
    res = adamw_rows([io.recv["w2_" + tag] for tag in FFN_TAGS], view(wts["ffn_w2"]), view(ms["ffn_w2"]),
                     view(vs["ffn_w2"]), "adamw_ffn_w2")
    emit("ffn_w2", res, shp)
    res = adamw_cols([io.recv["w_int"]], wts["ssm_w_in"], ms["ssm_w_in"], vs["ssm_w_in"], "adamw_ssm_w_in")
    emit("ssm_w_in", res, wts["ssm_w_in"].shape)
    for name, key in (("ssm_w_out", "w_out"), ("w_kv", "w_kv"), ("w_q", "w_q"), ("w_o", "w_o")):
        shp = wts[name].shape
        view = lambda t: t.reshape((1,) + shp[-2:])
        res = adamw_rows([io.recv[key]], view(wts[name]), view(ms[name]), view(vs[name]), "adamw_" + name)
        emit(name, res, shp)

    res_s = rowmap(lambda gg, ww, mm_, vv: _adamw(gg, ww, mm_, vv),
                   [_small_local(g_small), _small_local(wts), _small_local(ms), _small_local(vs)], [],
                   [(LANES, F32)] * 3, tm=SMALL_LOCAL_ROWS, name="adamw_small")
    flat_s = [r.reshape(-1) for r in res_s]
    off = 0
    for n, s, a in SMALL:
        shard = s if a is None else _shard_shape(s, a)
        cnt = int(np.prod(shard))
        out["grad_" + n] = g_small[n]
        for kind, arr in zip(("delta", "new_m", "new_v"), flat_s):
            out[kind + "_" + n] = arr[off:off + cnt].reshape(shard)
        off += cnt
    out["loss"] = loss
    out["grad_x"] = grad_x[None]
    return out


def kernel(x, ffn_norm, ffn_w1, ffn_w3, ffn_w2, ssm_norm, ssm_w_in, ssm_conv_w, ssm_conv_b, ssm_dt_bias, ssm_a_log, ssm_d, ssm_gate_norm, ssm_w_out, kv_norm, w_kv, k_norm, attn_norm, w_q, q_norm, sinks, w_o, rel_bias, loss_target, m_ffn_norm, m_ffn_w1, m_ffn_w3, m_ffn_w2, m_ssm_norm, m_ssm_w_in, m_ssm_conv_w, m_ssm_conv_b, m_ssm_dt_bias, m_ssm_a_log, m_ssm_d, m_ssm_gate_norm, m_ssm_w_out, m_kv_norm, m_w_kv, m_k_norm, m_attn_norm, m_w_q, m_q_norm, m_sinks, m_w_o, m_rel_bias, v_ffn_norm, v_ffn_w1, v_ffn_w3, v_ffn_w2, v_ssm_norm, v_ssm_w_in, v_ssm_conv_w, v_ssm_conv_b, v_ssm_dt_bias, v_ssm_a_log, v_ssm_d, v_ssm_gate_norm, v_ssm_w_out, v_kv_norm, v_w_kv, v_k_norm, v_attn_norm, v_w_q, v_q_norm, v_sinks, v_w_o, v_rel_bias):
    args = locals()
    wts = {n: args[n] for n in WEIGHT_NAMES}
    ms = {n: args["m_" + n] for n in WEIGHT_NAMES}
    vs = {n: args["v_" + n] for n in WEIGHT_NAMES}
    out = step(x, loss_target, wts, ms, vs)
    result = [out["loss"], out["grad_x"]]
    for kind in ("grad", "delta", "new_m", "new_v"):
        result += [out[kind + "_" + n] for n in WEIGHT_NAMES]
    return tuple(result)
```

```python
import functools
import math
import operator

import numpy as np
import jax
import jax.numpy as jnp
from jax import lax
from jax.experimental import pallas as pl
from jax.experimental.pallas import tpu as pltpu

F32 = jnp.float32
BF16 = jnp.bfloat16

D_MODEL = 1024
D_FF = 2816
N_DEV = 8
SSM_D_INNER = 2048
SSM_HEAD_DIM = 64
SSM_HEADS = 32
SSM_GROUPS = 4
SSM_STATE = 128
SSM_CONV = 4
SSM_CHUNK = 256
SSM_CONV_DIM = SSM_D_INNER + 2 * SSM_GROUPS * SSM_STATE
SSM_IN_DIM = SSM_D_INNER + SSM_CONV_DIM + SSM_HEADS
ATT_HEAD_DIM = 64
ATT_HEADS = 16
ATT_KV_HEADS = 2
ATT_GROUP = 8
ATT_WINDOW = 128
REL_BUCKETS = 32
EPS = 1e-6
NEG = -1e30

ADAM_LR = 0.001
ADAM_B1 = 0.9
ADAM_B2 = 0.999
ADAM_EPS = 1e-08
ADAM_WD = 0.01
ADAM_STEP = 10

VMEM_LIMIT_BYTES = 52 * 1024 * 1024
LANES = 128
MESH_ID = pl.DeviceIdType.MESH
ANY_SPEC = pl.BlockSpec(memory_space=pl.ANY)

NT = (((1,), (1,)), ((), ()))
TN = (((0,), (0,)), ((), ()))
NN = (((1,), (0,)), ((), ()))


def _pick(dim, cands):
    for c in cands:
        if dim % c == 0:
            return c
    return dim


def _my_index():
    return 4 * lax.axis_index("x") + 2 * lax.axis_index("y") + lax.axis_index("c")


def _peer(k):
    x, y, c = lax.axis_index("x"), lax.axis_index("y"), lax.axis_index("c")
    px = 1 - x if (k >> 2) & 1 else x
    py = 1 - y if (k >> 1) & 1 else y
    pc = 1 - c if k & 1 else c
    return (px, py, pc), 4 * px + 2 * py + pc


def _piece(ref, axis, d, n):
    if axis is None:
        return ref.at[d]
    return ref.at[(slice(None),) * axis + (pl.ds(pl.multiple_of(d * n, 8), n),)]


SIBLING = 1
CHIP_PEERS = (4, 2, 6)
N_CHIPS = 4
SEMS_PER_ITEM = N_DEV - 1


def _my_chip():
    return 2 * lax.axis_index("x") + lax.axis_index("y")


class Comm:
    def __init__(self, items):
        self.items = list(items)

    def dst_shapes(self):
        out = []
        for kind, src, axis in self.items:
            s = tuple(src.shape)
            if kind == "g":
                shp = (N_DEV,) + s
            elif kind == "g2":
                shp = (N_DEV,) + s if axis is None else s[:axis] + (N_DEV * s[axis],) + s[axis + 1:]
            elif kind == "sa":
                shp = (s[0], 1) + s[2:]
            else:
                shp = s
            out.append(jax.ShapeDtypeStruct(shp, src.dtype))
        return out

    def scratch(self):
        n = len(self.items)
        return [pltpu.SemaphoreType.DMA((n * SEMS_PER_ITEM,)), pltpu.SemaphoreType.DMA((n * SEMS_PER_ITEM,)),
                pltpu.SemaphoreType.DMA((n,))]

    def _run(self, srcs, dsts, sems, starting):
        send_sems, recv_sems, local_sems = sems
        me = _my_index()
        core = lax.axis_index("c")
        chip = _my_chip()
        for i, (kind, src, axis) in enumerate(self.items):
            s_ref, d_ref = srcs[i], dsts[i]
            base = i * SEMS_PER_ITEM

            def rdma(src_ref, dst_ref, j, peer):
                return pltpu.make_async_remote_copy(
                    src_ref=src_ref, dst_ref=dst_ref, send_sem=send_sems.at[base + j], recv_sem=recv_sems.at[base + j],
                    device_id=peer, device_id_type=MESH_ID)

            if kind == "g":
                local = pltpu.make_async_copy(s_ref, d_ref.at[me], local_sems.at[i])
                outs = [rdma(s_ref, d_ref.at[me], k - 1, _peer(k)[0]) for k in range(1, N_DEV)]
                if starting:
                    local.start()
                    for cp in outs:
                        cp.start()
                else:
                    for k in range(1, N_DEV):
                        rdma(s_ref, d_ref.at[_peer(k)[1]], k - 1, _peer(k)[0]).wait_recv()
                    for cp in outs:
                        cp.wait_send()
                    local.wait()
            elif kind == "g2":
                n = None if axis is None else src.shape[axis]
                mine = _piece(d_ref, axis, me, n)
                sib = _peer(SIBLING)[0]
                local = pltpu.make_async_copy(s_ref, mine, local_sems.at[i])
                outs = [rdma(s_ref, mine, 0, sib)] + [rdma(s_ref, mine, 1 + j, _peer(k)[0])
                                                      for j, k in enumerate(CHIP_PEERS)]
                if starting:
                    local.start()
                    for cp in outs:
                        cp.start()
                else:
                    passed = []
                    for j, k in enumerate(CHIP_PEERS):
                        theirs = _piece(d_ref, axis, _peer(k)[1], n)
                        rdma(s_ref, theirs, 1 + j, _peer(k)[0]).wait_recv()
                        fwd = rdma(theirs, theirs, 4 + j, sib)
                        fwd.start()
                        passed.append(fwd)
                    rdma(s_ref, _piece(d_ref, axis, _peer(SIBLING)[1], n), 0, sib).wait_recv()
                    for j, k in enumerate(CHIP_PEERS):
                        rdma(s_ref, _piece(d_ref, axis, _peer(k ^ SIBLING)[1], n), 4 + j, sib).wait_recv()
                    for cp in outs + passed:
                        cp.wait_send()
                    local.wait()
            elif kind == "sa":
                cp = rdma(s_ref.at[(slice(None), pl.ds(1 - core, 1))], d_ref, 0, _peer(SIBLING)[0])
                if starting:
                    cp.start()
                else:
                    cp.wait_recv()
                    cp.wait_send()
            else:
                local = pltpu.make_async_copy(s_ref.at[chip], d_ref.at[chip], local_sems.at[i])
                outs = [rdma(s_ref.at[_peer(k)[1] >> 1], d_ref.at[chip], 1 + j, _peer(k)[0])
                        for j, k in enumerate(CHIP_PEERS)]
                if starting:
                    local.start()
                    for cp in outs:
                        cp.start()
                else:
                    for j, k in enumerate(CHIP_PEERS):
                        rdma(s_ref.at[chip], d_ref.at[_peer(k)[1] >> 1], 1 + j, _peer(k)[0]).wait_recv()
                    for cp in outs:
                        cp.wait_send()
                    local.wait()

    def start(self, srcs, dsts, sems):
        self._run(srcs, dsts, sems, True)

    def wait(self, srcs, dsts, sems):
        self._run(srcs, dsts, sems, False)


def pcall(body, *, name, grid, in_specs, out_specs, out_shape, args, scratch=(), hook=None):
    cparams = pltpu.CompilerParams(dimension_semantics=("arbitrary",) * len(grid), vmem_limit_bytes=VMEM_LIMIT_BYTES)
    if hook is None:
        outs = pl.pallas_call(body, name=name, grid=grid, in_specs=list(in_specs), out_specs=list(out_specs),
                              out_shape=list(out_shape), scratch_shapes=list(scratch), compiler_params=cparams)(*args)
        return list(outs)
    comm, sink = hook
    n_in, n_out, n_scr, n_it = len(args), len(out_shape), len(scratch), len(comm.items)
    dims = tuple(grid)

    def wrapped(*refs):
        p = 0
        ins = refs[p:p + n_in]
        p += n_in
        csrc = refs[p:p + n_it]
        p += n_it
        outs = refs[p:p + n_out]
        p += n_out
        cdst = refs[p:p + n_it]
        p += n_it
        scr = refs[p:p + n_scr]
        p += n_scr
        sems = refs[p:p + 3]
        if dims:
            ids = [pl.program_id(a) for a in range(len(dims))]
            first = functools.reduce(operator.and_, [i == 0 for i in ids])
            last = functools.reduce(operator.and_, [i == d - 1 for i, d in zip(ids, dims)])

            @pl.when(first)
            def _():
                comm.start(csrc, cdst, sems)

            body(*ins, *outs, *scr)

            @pl.when(last)
            def _():
                comm.wait(csrc, cdst, sems)
        else:
            comm.start(csrc, cdst, sems)
            body(*ins, *outs, *scr)
            comm.wait(csrc, cdst, sems)

    res = pl.pallas_call(
        wrapped, name=name, grid=grid,
        in_specs=list(in_specs) + [ANY_SPEC] * n_it, out_specs=list(out_specs) + [ANY_SPEC] * n_it,
        out_shape=list(out_shape) + comm.dst_shapes(), scratch_shapes=list(scratch) + comm.scratch(),
        compiler_params=cparams,
    )(*args, *[src for _, src, _ in comm.items])
    res = list(res)
    sink(res[n_out:])
    return res[:n_out]


def comm_only(comm, name):
    got = []
    pcall(lambda *refs: None, name=name, grid=(), in_specs=[], out_specs=[], out_shape=[], args=[],
          hook=(comm, got.extend))
    return got


def mm(a, b, *, ta=False, tb=False, out_dtype=F32, res=None, alpha=1.0, name, hook=None):
    if ta:
        k_dim, m_dim = a.shape
    else:
        m_dim, k_dim = a.shape
    if tb:
        n_dim, k2 = b.shape
    else:
        k2, n_dim = b.shape
    assert k_dim == k2, (a.shape, b.shape, ta, tb)
    tn = _pick(n_dim, (1024, 1408, 512, 256, 128))
    tm = _pick(m_dim, (1024, 1408, 512, 256, 128)) if tn <= 1024 else _pick(m_dim, (512, 256, 128))
    tk = _pick(k_dim, (512, 1408, 256, 128))
    nk = k_dim // tk
    has_res = res is not None
    dn = (((0 if ta else 1,), (1 if tb else 0,)), ((), ()))

    def body(*refs):
        if has_res:
            a_ref, b_ref, r_ref, o_ref, acc_ref = refs
        else:
            a_ref, b_ref, o_ref, acc_ref = refs
        k = pl.program_id(2)

        @pl.when(k == 0)
        def _():
            acc_ref[...] = jnp.zeros_like(acc_ref)

        acc_ref[...] += lax.dot_general(a_ref[...].astype(BF16), b_ref[...].astype(BF16), dn,
                                        preferred_element_type=F32)

        @pl.when(k == nk - 1)
        def _():
            r = acc_ref[...]
            if alpha != 1.0:
                r = r * alpha
            if has_res:
                r = r_ref[...] + r
            o_ref[...] = r.astype(o_ref.dtype)

    a_spec = pl.BlockSpec((tk, tm), lambda i, j, k: (k, i)) if ta else pl.BlockSpec((tm, tk), lambda i, j, k: (i, k))
    b_spec = pl.BlockSpec((tn, tk), lambda i, j, k: (j, k)) if tb else pl.BlockSpec((tk, tn), lambda i, j, k: (k, j))
    o_spec = pl.BlockSpec((tm, tn), lambda i, j, k: (i, j))
    in_specs = [a_spec, b_spec] + ([o_spec] if has_res else [])
    args = [a, b] + ([res] if has_res else [])
    out, = pcall(body, name=name, grid=(m_dim // tm, n_dim // tn, nk), in_specs=in_specs, out_specs=[o_spec],
                 out_shape=[jax.ShapeDtypeStruct((m_dim, n_dim), out_dtype)], args=args,
                 scratch=[pltpu.VMEM((tm, tn), F32)], hook=hook)
    return out


def rowmap(fn, rows, consts=(), out_rows=(), out_accs=(), *, tm, name, hook=None):
    first = rows[0][0] if isinstance(rows[0], tuple) else rows[0]
    t_dim = first.shape[0]
    assert t_dim % tm == 0, (t_dim, tm)
    n_r, n_c, n_o = len(rows), len(consts), len(out_rows)

    def body(*refs):
        ins = [r[...] for r in refs[:n_r + n_c]]
        o_refs = refs[n_r + n_c:]
        outs = tuple(fn(*ins))
        for o_ref, val in zip(o_refs[:n_o], outs[:n_o]):
            o_ref[...] = val.astype(o_ref.dtype)
        if out_accs:
            @pl.when(pl.program_id(0) == 0)
            def _():
                for o_ref in o_refs[n_o:]:
                    o_ref[...] = jnp.zeros_like(o_ref)

            for o_ref, val in zip(o_refs[n_o:], outs[n_o:]):
                o_ref[...] += val

    in_specs, args = [], []
    for r in rows:
        if isinstance(r, tuple):
            args.append(r[0])
            in_specs.append(r[1])
        else:
            args.append(r)
            in_specs.append(pl.BlockSpec((tm, r.shape[1]), lambda i: (i, 0)))
    for c in consts:
        args.append(c)
        in_specs.append(pl.BlockSpec(c.shape, lambda i, nd=c.ndim: (0,) * nd))
    out_specs = [pl.BlockSpec((tm, w), lambda i: (i, 0)) for (w, _) in out_rows]
    out_specs += [pl.BlockSpec(s, lambda i, nd=len(s): (0,) * nd) for s in out_accs]
    out_shape = [jax.ShapeDtypeStruct((t_dim, w), dt) for (w, dt) in out_rows]
    out_shape += [jax.ShapeDtypeStruct(s, F32) for s in out_accs]
    return pcall(body, name=name, grid=(t_dim // tm,), in_specs=in_specs, out_specs=out_specs, out_shape=out_shape,
                 args=args, hook=hook)


def _rms_fwd(x, g):
    r = lax.rsqrt(jnp.mean(x * x, axis=-1, keepdims=True) + EPS)
    return x * r * g


def _rms_bwd(x, g, dy):
    r = lax.rsqrt(jnp.mean(x * x, axis=-1, keepdims=True) + EPS)
    xh = x * r
    dg = jnp.sum(dy * xh, axis=0, keepdims=True)
    dxh = dy * g
    dx = r * (dxh - xh * jnp.mean(dxh * xh, axis=-1, keepdims=True))
    return dx, dg


def _sigmoid(x):
    return 1.0 / (1.0 + jnp.exp(-x))


def _silu(x):
    return x * _sigmoid(x)


def _silu_grad(x):
    s = _sigmoid(x)
    return s * (1.0 + x * (1.0 - s))


def _split3(x):
    hi = x.astype(BF16)
    r1 = x - hi.astype(F32)
    mid = r1.astype(BF16)
    lo = (r1 - mid.astype(F32)).astype(BF16)
    return hi, mid, lo


def _dot(a, b, dn=NN):
    return lax.dot_general(a.astype(BF16), b.astype(BF16), dn, preferred_element_type=F32)


def _col_of(mat, h):
    lane = lax.broadcasted_iota(jnp.int32, mat.shape, 1)
    return jnp.sum(jnp.where(lane == h, mat, 0.0), axis=1, keepdims=True)


FFN_TN = 1408


def ffn_upgate(u, w1t, w3t, nm, hook=None):
    t_dim = u.shape[0]
    tm = _pick(t_dim, (512, 256, 128))
    tn = FFN_TN

    def body(u_ref, w1_ref, w3_ref, a_ref, b_ref, hm_ref):
        uu = u_ref[...]
        a = lax.dot_general(uu, w1_ref[...], NT, preferred_element_type=F32)
        b = lax.dot_general(uu, w3_ref[...], NT, preferred_element_type=F32)
        a_ref[...] = a.astype(a_ref.dtype)
        b_ref[...] = b.astype(b_ref.dtype)
        hm_ref[...] = (_silu(a) * b).astype(hm_ref.dtype)

    w_spec = pl.BlockSpec((tn, D_MODEL), lambda j, i: (j, 0))
    o_spec = pl.BlockSpec((tm, tn), lambda j, i: (i, j))
    o_shape = jax.ShapeDtypeStruct((t_dim, D_FF), BF16)
    return pcall(body, name=nm, grid=(D_FF // tn, t_dim // tm),
                 in_specs=[pl.BlockSpec((tm, D_MODEL), lambda j, i: (i, 0)), w_spec, w_spec],
                 out_specs=[o_spec] * 3, out_shape=[o_shape] * 3, args=[u, w1t, w3t], hook=hook)


def ffn_dgate(dout_bf, w2, a, b, nm, hook=None):
    t_dim = dout_bf.shape[0]
    tm = _pick(t_dim, (512, 256, 128))
    tn = FFN_TN

    def body(d_ref, w2_ref, a_ref, b_ref, da_ref, db_ref):
        dhm = 0.5 * lax.dot_general(d_ref[...], w2_ref[...], NT, preferred_element_type=F32)
        av = a_ref[...].astype(F32)
        bv = b_ref[...].astype(F32)
        da_ref[...] = (dhm * bv * _silu_grad(av)).astype(da_ref.dtype)
        db_ref[...] = (dhm * _silu(av)).astype(db_ref.dtype)

    t_spec = pl.BlockSpec((tm, tn), lambda j, i: (i, j))
    o_shape = jax.ShapeDtypeStruct((t_dim, D_FF), BF16)
    return pcall(body, name=nm, grid=(D_FF // tn, t_dim // tm),
                 in_specs=[pl.BlockSpec((tm, D_MODEL), lambda j, i: (i, 0)),
                           pl.BlockSpec((tn, D_MODEL), lambda j, i: (j, 0)), t_spec, t_spec],
                 out_specs=[t_spec] * 2, out_shape=[o_shape] * 2, args=[dout_bf, w2, a, b], hook=hook)


def ffn_fwd(h, g, tag, io):
    nm = "f" + tag
    w1t, w3t, w2 = io.w("w1t_" + tag), io.w("w3t_" + tag), io.w("w2_" + tag)
    u, = rowmap(lambda x, gg: (_rms_fwd(x, gg),), [h], [g], [(D_MODEL, BF16)], tm=256, name=nm + "_norm")
    a, b, hm = ffn_upgate(u, w1t, w3t, nm + "_upgate", hook=io.hook(nm + "_upgate"))
    out = mm(hm, w2, res=h, alpha=0.5, name=nm + "_down")
    return out, (u, a, b, hm)


def norm_bwd(h, g, du, dout, nm):
    def fn(x, d_u, d_o, gg):
        dx, dg = _rms_bwd(x, gg, d_u)
        dh = d_o + dx
        return dh, dh, dg

    return rowmap(fn, [h, du, dout], [g], [(D_MODEL, F32), (D_MODEL, BF16)], [(1, D_MODEL)], tm=256, name=nm)


def ffn_bwd(h, g, tag, saved, dout, dout_bf, io):
    nm = "f" + tag
    w1t, w3t, w2 = io.w("w1t_" + tag), io.w("w3t_" + tag), io.w("w2_" + tag)
    u, a, b, hm = saved
    io.put("w2_" + tag, mm(hm, dout_bf, ta=True, alpha=0.5, out_dtype=BF16, name=nm + "_dw2",
                           hook=io.hook(nm + "_dw2")))
    da, db = ffn_dgate(dout_bf, w2, a, b, nm + "_dgate", hook=io.hook(nm + "_dgate"))
    io.put("w1t_" + tag, mm(da, u, ta=True, out_dtype=BF16, name=nm + "_dw1"))
    io.put("w3t_" + tag, mm(db, u, ta=True, out_dtype=BF16, name=nm + "_dw3"))
    du = mm(da, w1t, name=nm + "_du1", hook=io.hook(nm + "_du1"))
    du = mm(db, w3t, res=du, name=nm + "_du2")
    return norm_bwd(h, g, du, dout, nm + "_dnorm")


def _conv_pre(x, halo, w, b, tm):
    halo = jnp.where(pl.program_id(0) > 0, halo, 0.0)
    xx = jnp.concatenate([halo, x], axis=0)
    shifted = [xx[5 + k:5 + k + tm] for k in range(SSM_CONV)]
    acc = b + shifted[0] * w[0:1]
    for k in range(1, SSM_CONV):
        acc = acc + shifted[k] * w[k:k + 1]
    return acc, shifted


def _prev_halo_spec(tm, width):
    return pl.BlockSpec((8, width), lambda i: (jnp.maximum(i * (tm // 8) - 1, 0), 0))


def conv_fwd(xbc_raw, w, b, nm):
    tm = 128

    def fn(x, halo, ww, bb):
        acc, _ = _conv_pre(x, halo, ww, bb, tm)
        return (_silu(acc),)

    out, = rowmap(fn, [xbc_raw, (xbc_raw, _prev_halo_spec(tm, SSM_CONV_DIM))], [w, b],
                  [(SSM_CONV_DIM, F32)], tm=tm, name=nm)
    return out


def conv_bwd(xbc_raw, w, b, dxs, db_in, dc_in, nm):
    tm = 128
    t_dim = xbc_raw.shape[0]

    def fn1(x, halo, d1, d2, d3, ww, bb):
        acc, shifted = _conv_pre(x, halo, ww, bb, tm)
        dacc = jnp.concatenate([d1, d2, d3], axis=1) * _silu_grad(acc)
        dw = jnp.concatenate([jnp.sum(dacc * s, axis=0, keepdims=True) for s in shifted], axis=0)
        return dacc, dw, jnp.sum(dacc, axis=0, keepdims=True)

    dacc, dw, dbias = rowmap(fn1, [xbc_raw, (xbc_raw, _prev_halo_spec(tm, SSM_CONV_DIM)), dxs, db_in, dc_in],
                             [w, b], [(SSM_CONV_DIM, F32)], [(SSM_CONV, SSM_CONV_DIM), (1, SSM_CONV_DIM)],
                             tm=tm, name=nm + "_a")
    n_tiles = t_dim // tm

    def fn2(d, nxt, ww):
        nxt = jnp.where(pl.program_id(0) < n_tiles - 1, nxt, 0.0)
        dd = jnp.concatenate([d, nxt], axis=0)
        out = dd[3:3 + tm] * ww[0:1]
        for k in range(1, SSM_CONV):
            out = out + dd[3 - k:3 - k + tm] * ww[k:k + 1]
        return (out,)

    nxt_spec = pl.BlockSpec((8, SSM_CONV_DIM), lambda i: (jnp.minimum((i + 1) * (tm // 8), t_dim // 8 - 1), 0))
    dx, = rowmap(fn2, [dacc, (dacc, nxt_spec)], [w], [(SSM_CONV_DIM, BF16)], tm=tm, name=nm + "_b")
    return dx, dw, dbias


GRP_W = SSM_D_INNER // SSM_GROUPS
HPG = SSM_HEADS // SSM_GROUPS
HEAD_SHIFT = 6


def _split2(x):
    hi = x.astype(BF16)
    return hi, (x - hi.astype(F32)).astype(BF16)


def _expand_mats():
    e = ((lax.broadcasted_iota(jnp.int32, (HPG, GRP_W), 1) >> HEAD_SHIFT)
         == lax.broadcasted_iota(jnp.int32, (HPG, GRP_W), 0)).astype(BF16)
    et = ((lax.broadcasted_iota(jnp.int32, (GRP_W, HPG), 0) >> HEAD_SHIFT)
          == lax.broadcasted_iota(jnp.int32, (GRP_W, HPG), 1)).astype(BF16)
    return e, et


def _expand(v, e_m):
    hi, lo = _split2(v)
    return jnp.dot(hi, e_m, preferred_element_type=F32) + jnp.dot(lo, e_m, preferred_element_type=F32)


def _reduce8(v, et_m):
    acc = None
    for p in _split3(v):
        t = jnp.dot(p, et_m, preferred_element_type=F32)
        acc = t if acc is None else acc + t
    return acc


def _ssd_group_terms(dt_ref, dtT_ref, arow_ref, acol_ref):
    L = SSM_CHUNK
    r = lax.broadcasted_iota(jnp.int32, (L, L), 0)
    c = lax.broadcasted_iota(jnp.int32, (L, L), 1)
    tril = (r >= c).astype(BF16)
    triu = (r <= c).astype(BF16)
    dtg = dt_ref[0]
    acol = None
    for p in _split3(dtg * arow_ref[0]):
        t = jnp.dot(tril, p, preferred_element_type=F32)
        acol = t if acol is None else acol + t
    arowT = None
    for p in _split3(dtT_ref[0] * acol_ref[0]):
        t = jnp.dot(p, triu, preferred_element_type=F32)
        arowT = t if arowT is None else arowT + t
    return dtg, acol, arowT, r >= c


def _state_decay(a_last_col, et_m):
    hi, lo = _split2(jnp.broadcast_to(jnp.exp(a_last_col), (HPG, SSM_STATE)))
    return jnp.dot(et_m, hi, preferred_element_type=F32) + jnp.dot(et_m, lo, preferred_element_type=F32)


def _ssd_specs(nc, rev):
    L, N = SSM_CHUNK, SSM_STATE
    xcols = SSM_D_INNER // LANES
    ch = (lambda c: nc - 1 - c) if rev else (lambda c: c)
    return [
        pl.BlockSpec((L, GRP_W), lambda c, g: (ch(c), g)),
        pl.BlockSpec((L, N), lambda c, g: (ch(c), xcols + g)),
        pl.BlockSpec((L, N), lambda c, g: (ch(c), xcols + SSM_GROUPS + g)),
        pl.BlockSpec((1, L, HPG), lambda c, g: (g, ch(c), 0)),
        pl.BlockSpec((1, HPG, L), lambda c, g: (g, 0, ch(c))),
        pl.BlockSpec((1, 1, HPG), lambda c, g: (g, 0, 0)),
        pl.BlockSpec((1, HPG, 1), lambda c, g: (g, 0, 0)),
        pl.BlockSpec((1, GRP_W), lambda c, g: (0, g)),
    ]


def ssd_fwd(xbc, dt_g, dtT_g, a_row, a_col, dvec, nm, hook=None):
    t_dim = xbc.shape[0]
    L, P, N = SSM_CHUNK, SSM_HEAD_DIM, SSM_STATE
    nc = t_dim // L

    def body(x_ref, b_ref, c_ref, dt_ref, dtT_ref, arow_ref, acol_ref, dvec_ref, y_ref, st_ref, s_s):
        ci = pl.program_id(0)
        g = pl.program_id(1)

        @pl.when((ci == 0) & (g == 0))
        def _():
            s_s[...] = jnp.zeros_like(s_s)

        e_m, et_m = _expand_mats()
        dtg, acol, arowT, causal = _ssd_group_terms(dt_ref, dtT_ref, arow_ref, acol_ref)
        a_last_row = acol[L - 1:L, :]
        x = x_ref[...]
        bm = b_ref[...]
        cm = c_ref[...]
        cb = _dot(cm, bm, NT)
        s = s_s[g]
        st_ref[0, 0] = s
        ea_x = _expand(jnp.exp(acol), e_m)
        dt_x = _expand(dtg, e_m)
        w_x = _expand(jnp.exp(a_last_row - acol) * dtg, e_m)
        yb = ea_x * _dot(cm, s, NT) + dvec_ref[...] * x
        xd = (x * dt_x).astype(BF16)
        for e in range(HPG):
            sl = slice(e * P, (e + 1) * P)
            lm = jnp.exp(jnp.where(causal, acol[:, e:e + 1] - arowT[e:e + 1, :], NEG))
            m = (cb * lm).astype(BF16)
            y_ref[:, sl] = yb[:, sl] + jnp.dot(m, xd[:, sl], preferred_element_type=F32)
        s_s[g] = _state_decay(arowT[:, L - 1:L], et_m) * s + _dot(x * w_x, bm, TN)

    out_specs = [
        pl.BlockSpec((L, GRP_W), lambda c, g: (c, g)),
        pl.BlockSpec((1, 1, GRP_W, N), lambda c, g: (c, g, 0, 0)),
    ]
    return pcall(
        body, name=nm, grid=(nc, SSM_GROUPS), in_specs=_ssd_specs(nc, False), out_specs=out_specs,
        out_shape=[jax.ShapeDtypeStruct((t_dim, SSM_D_INNER), F32),
                   jax.ShapeDtypeStruct((nc, SSM_GROUPS, GRP_W, N), F32)],
        scratch=[pltpu.VMEM((SSM_GROUPS, GRP_W, N), F32)],
        args=[xbc, xbc, xbc, dt_g, dtT_g, a_row, a_col, dvec], hook=hook)


def ssd_bwd(dy, xbc, dt_g, dtT_g, a_row, a_col, dvec, states, nm, hook=None):
    t_dim = xbc.shape[0]
    L, P, N = SSM_CHUNK, SSM_HEAD_DIM, SSM_STATE
    nc = t_dim // L

    def body(dy_ref, x_ref, b_ref, c_ref, dt_ref, dtT_ref, arow_ref, acol_ref, dvec_ref, st_ref,
             dx_ref, db_ref, dc_ref, da_ref, ddt_ref, dd_ref, ds_s, yd_s, dxd_s):
        ci = pl.program_id(0)
        g = pl.program_id(1)

        @pl.when((ci == 0) & (g == 0))
        def _():
            ds_s[...] = jnp.zeros_like(ds_s)
            dd_ref[...] = jnp.zeros_like(dd_ref)

        e_m, et_m = _expand_mats()
        dtg, acol, arowT, causal = _ssd_group_terms(dt_ref, dtT_ref, arow_ref, acol_ref)
        a_last_row = acol[L - 1:L, :]
        x = x_ref[...]
        dy = dy_ref[...]
        bm = b_ref[...]
        cm = c_ref[...]
        cb = _dot(cm, bm, NT)
        s = st_ref[0, 0]
        dsp = ds_s[g]
        ew8 = jnp.exp(a_last_row - acol)
        ea_x = _expand(jnp.exp(acol), e_m)
        dt_x = _expand(dtg, e_m)
        ew_x = _expand(ew8, e_m)
        w_x = ew_x * dt_x
        z = _dot(cm, s, NT)
        dz = ea_x * dy
        dc = _dot(dz, s)
        ds_y = _dot(dz, cm, TN)
        du = _dot(bm, dsp, NT)
        u = x * w_x
        db = _dot(u, dsp)
        xd = (x * dt_x).astype(BF16)
        dyb = dy.astype(BF16)
        dcb = jnp.zeros((L, L), F32)
        for e in range(HPG):
            sl = slice(e * P, (e + 1) * P)
            lm = jnp.exp(jnp.where(causal, acol[:, e:e + 1] - arowT[e:e + 1, :], NEG))
            m = (cb * lm).astype(BF16)
            yd_s[:, sl] = jnp.dot(m, xd[:, sl], preferred_element_type=F32)
            dxd_s[:, sl] = lax.dot_general(m, dyb[:, sl], TN, preferred_element_type=F32)
            dcb = dcb + lax.dot_general(dyb[:, sl], xd[:, sl], NT, preferred_element_type=F32) * lm
        dxd = dxd_s[...]
        dx_ref[...] = dvec_ref[...] * dy + du * w_x + dt_x * dxd
        ddt = _reduce8(x * (ew_x * du + dxd), et_m)
        da = (_reduce8(dz * z + dyb.astype(F32) * yd_s[...], et_m)
              - _reduce8(xd.astype(F32) * dxd + du * u, et_m))
        dwa_row = _reduce8(jnp.broadcast_to(jnp.sum(du * u, axis=0, keepdims=True), (8, GRP_W)), et_m)[0:1]
        t_nh = None
        for p in _split3(dsp * s):
            t = lax.dot_general(p, et_m, TN, preferred_element_type=F32)
            t_nh = t if t_nh is None else t_nh + t
        d_last = dwa_row + jnp.exp(a_last_row) * jnp.sum(t_nh, axis=0, keepdims=True)
        row_l = lax.broadcasted_iota(jnp.int32, (L, 1), 0)
        da_ref[0] = da + jnp.where(row_l == L - 1, d_last, 0.0)
        ddt_ref[0] = ddt
        dd_ref[g] += jnp.sum(dy * x, axis=0, keepdims=True)
        dc_ref[...] = dc + _dot(dcb, bm)
        db_ref[...] = db + _dot(dcb, cm, TN)
        ds_s[g] = _state_decay(arowT[:, L - 1:L], et_m) * dsp + ds_y

    rc = lambda c: nc - 1 - c
    in_specs = ([pl.BlockSpec((L, GRP_W), lambda c, g: (rc(c), g))] + _ssd_specs(nc, True)
                + [pl.BlockSpec((1, 1, GRP_W, N), lambda c, g: (rc(c), g, 0, 0))])
    out_specs = [
        pl.BlockSpec((L, GRP_W), lambda c, g: (rc(c), g)),
        pl.BlockSpec((L, N), lambda c, g: (rc(c), g)),
        pl.BlockSpec((L, N), lambda c, g: (rc(c), g)),
        pl.BlockSpec((1, L, HPG), lambda c, g: (g, rc(c), 0)),
        pl.BlockSpec((1, L, HPG), lambda c, g: (g, rc(c), 0)),
        pl.BlockSpec((SSM_GROUPS, 1, GRP_W), lambda c, g: (0, 0, 0)),
    ]
    gn = SSM_GROUPS * N
    out_shape = [
        jax.ShapeDtypeStruct((t_dim, SSM_D_INNER), F32), jax.ShapeDtypeStruct((t_dim, gn), F32),
        jax.ShapeDtypeStruct((t_dim, gn), F32), jax.ShapeDtypeStruct((SSM_GROUPS, t_dim, HPG), F32),
        jax.ShapeDtypeStruct((SSM_GROUPS, t_dim, HPG), F32), jax.ShapeDtypeStruct((SSM_GROUPS, 1, GRP_W), F32),
    ]
    return pcall(
        body, name=nm, grid=(nc, SSM_GROUPS), in_specs=in_specs, out_specs=out_specs, out_shape=out_shape,
        scratch=[pltpu.VMEM((SSM_GROUPS, GRP_W, N), F32), pltpu.VMEM((L, GRP_W), F32), pltpu.VMEM((L, GRP_W), F32)],
        args=[dy, xbc, xbc, xbc, dt_g, dtT_g, a_row, a_col, dvec, states], hook=hook)


def _softplus(x):
    return jnp.maximum(x, 0.0) + jnp.log(1.0 + jnp.exp(-jnp.abs(x)))


def ssd_dt_bwd(da, ddt, dt, dt_raw, a_row, dt_bias, nm):
    L = SSM_CHUNK

    def fn(d_a, d_dt, dtv, raw, ar, bias):
        r = lax.broadcasted_iota(jnp.int32, (L, L), 0)
        c = lax.broadcasted_iota(jnp.int32, (L, L), 1)
        triu = (r <= c).astype(BF16)
        acc = None
        for p in _split3(d_a):
            t = jnp.dot(triu, p, preferred_element_type=F32)
            acc = t if acc is None else acc + t
        d_dt = d_dt + acc * ar
        d_a_h = jnp.sum(acc * dtv, axis=0, keepdims=True)
        d_raw = d_dt * _sigmoid(raw + bias)
        return d_raw, d_a_h, jnp.sum(d_raw, axis=0, keepdims=True)

    return rowmap(fn, [da, ddt, dt, dt_raw], [a_row, dt_bias], [(SSM_HEADS, BF16)],
                  [(1, SSM_HEADS), (1, SSM_HEADS)], tm=L, name=nm)


GN_W = SSM_D_INNER // SSM_GROUPS


def mamba_fwd(h, p, nm, io):
    u, = rowmap(lambda x, gg: (_rms_fwd(x, gg),), [h], [p["ssm_norm"]], [(D_MODEL, BF16)], tm=256, name=nm + "_norm")
    z = mm(u, p["w_zt"], tb=True, name=nm + "_z")
    xbc_raw = mm(u, p["w_xbct"], tb=True, name=nm + "_xbc", hook=io.hook(nm + "_xbc"))
    dt_raw = mm(u, p["w_dtt"], tb=True, name=nm + "_dt")
    xbc = conv_fwd(xbc_raw, p["conv_w"], p["conv_b"], nm + "_conv")
    dt, = rowmap(lambda r, b: (_softplus(r + b),), [dt_raw], [p["dt_bias"]], [(SSM_HEADS, F32)], tm=256,
                 name=nm + "_softplus")
    dt_g = dt.reshape(-1, SSM_GROUPS, HPG).transpose(1, 0, 2)
    dtT_g = dt_g.transpose(0, 2, 1)
    y, states = ssd_fwd(xbc, dt_g, dtT_g, p["a_row"], p["a_col"], p["dvec"], nm + "_ssd", hook=io.hook(nm + "_ssd"))

    def gate_norm(yv, zv, gg):
        t = yv * _silu(zv)
        return (jnp.concatenate([_rms_fwd(t[:, k * GN_W:(k + 1) * GN_W], gg[:, k * GN_W:(k + 1) * GN_W])
                                 for k in range(SSM_GROUPS)], axis=1),)

    yn, = rowmap(gate_norm, [y, z], [p["gate_norm"]], [(SSM_D_INNER, BF16)], tm=256, name=nm + "_gatenorm")
    out = mm(yn, p["w_out"], res=h, name=nm + "_out")
    return out, (u, z, xbc_raw, dt_raw, xbc, dt, dt_g, dtT_g, y, states, yn)


def mamba_bwd(h, p, saved, dout, dout_bf, nm, io):
    u, z, xbc_raw, dt_raw, xbc, dt, dt_g, dtT_g, y, states, yn = saved
    g = {}
    io.put("w_out", mm(yn, dout_bf, ta=True, out_dtype=BF16, name=nm + "_dwout"))
    dyn = mm(dout_bf, p["w_out"], tb=True, name=nm + "_dyn")

    def gate_norm_bwd(d, yv, zv, gg):
        sz = _silu(zv)
        t = yv * sz
        dts, dgs = [], []
        for k in range(SSM_GROUPS):
            sl = slice(k * GN_W, (k + 1) * GN_W)
            dt_k, dg_k = _rms_bwd(t[:, sl], gg[:, sl], d[:, sl])
            dts.append(dt_k)
            dgs.append(dg_k)
        d_t = jnp.concatenate(dts, axis=1)
        return d_t * sz, d_t * yv * _silu_grad(zv), jnp.concatenate(dgs, axis=1)

    dy, dz, g["gate_norm"] = rowmap(gate_norm_bwd, [dyn, y, z], [p["gate_norm"]],
                                    [(SSM_D_INNER, F32), (SSM_D_INNER, BF16)], [(1, SSM_D_INNER)], tm=256,
                                    name=nm + "_dgatenorm")
    dxs, db_in, dc_in, da_g, ddt_g, dd = ssd_bwd(
        dy, xbc, dt_g, dtT_g, p["a_row"], p["a_col"], p["dvec"], states, nm + "_dssd", hook=io.hook(nm + "_dssd"))
    g["dvec"] = dd
    per_head = lambda t: t.transpose(1, 0, 2).reshape(-1, SSM_HEADS)
    ddt_raw, g["a"], g["dt_bias"] = ssd_dt_bwd(per_head(da_g), per_head(ddt_g), dt, dt_raw, p["a_heads"],
                                               p["dt_bias"], nm + "_ddt")
    dxbc_raw, g["conv_w"], g["conv_b"] = conv_bwd(xbc_raw, p["conv_w"], p["conv_b"], dxs, db_in, dc_in, nm + "_dconv")
    io.put("w_int", jnp.concatenate([mm(dz, u, ta=True, out_dtype=BF16, name=nm + "_dwz"),
                                     mm(dxbc_raw, u, ta=True, out_dtype=BF16, name=nm + "_dwxbc"),
                                     mm(ddt_raw, u, ta=True, out_dtype=BF16, name=nm + "_dwdt")], axis=0))
    du = mm(dz, p["w_zt"], name=nm + "_du1")
    du = mm(dxbc_raw, p["w_xbct"], res=du, name=nm + "_du2", hook=io.hook(nm + "_du2"))
    du = mm(ddt_raw, p["w_dtt"], res=du, name=nm + "_du3")
    dh, dh_bf, g["ssm_norm"] = norm_bwd(h, p["ssm_norm"], du, dout, nm + "_dnorm")
    return dh, dh_bf, g


KV_W = ATT_KV_HEADS * ATT_HEAD_DIM


def kv_fwd(h, p, nm):
    u, = rowmap(lambda x, gg: (_rms_fwd(x, gg),), [h], [p["kv_norm"]], [(D_MODEL, BF16)], tm=256, name=nm + "_norm")
    kv_raw = mm(u, p["w_kv"], name=nm + "_proj")

    def knorm(t, gg):
        ks = [_rms_fwd(t[:, j * ATT_HEAD_DIM:(j + 1) * ATT_HEAD_DIM], gg) for j in range(ATT_KV_HEADS)]
        return jnp.concatenate(ks, axis=1), t[:, KV_W:]

    k, v = rowmap(knorm, [kv_raw], [p["k_norm"]], [(KV_W, F32), (KV_W, F32)], tm=256, name=nm + "_knorm")
    return k, v, (u, kv_raw)


def kv_bwd(h, p, saved, dk_cur, dk_prev, dv_cur, dv_prev, dout, nm, io):
    u, kv_raw = saved
    t_dim = h.shape[0]
    tm = ATT_WINDOW
    nb = t_dim // tm
    nxt = pl.BlockSpec((tm, KV_W), lambda i: (jnp.minimum(i + 1, nb - 1), 0))

    def fn(dkc, dkp, dvc, dvp, t, gg):
        live = pl.program_id(0) < nb - 1
        dk = dkc + jnp.where(live, dkp, 0.0)
        dv = dvc + jnp.where(live, dvp, 0.0)
        outs, dgs = [], None
        for j in range(ATT_KV_HEADS):
            sl = slice(j * ATT_HEAD_DIM, (j + 1) * ATT_HEAD_DIM)
            dx, dg = _rms_bwd(t[:, sl], gg, dk[:, sl])
            outs.append(dx)
            dgs = dg if dgs is None else dgs + dg
        return jnp.concatenate(outs + [dv], axis=1), dgs

    dkv_raw, dknorm = rowmap(fn, [dk_cur, (dk_prev, nxt), dv_cur, (dv_prev, nxt), kv_raw], [p["k_norm"]],
                             [(2 * KV_W, BF16)], [(1, ATT_HEAD_DIM)], tm=tm, name=nm + "_dknorm",
                             hook=io.hook(nm + "_dknorm"))
    g = {"k_norm": dknorm}
    io.put("w_kv", mm(u, dkv_raw, ta=True, out_dtype=BF16, name=nm + "_dwkv"))
    du = mm(dkv_raw, p["w_kv"], tb=True, name=nm + "_du", hook=io.hook(nm + "_du"))
    dh, dh_bf, g["kv_norm"] = norm_bwd(h, p["kv_norm"], du, dout, nm + "_dnorm")
    return dh, dh_bf, g


def _attn_scores(q_ref, kp_ref, kc_ref, vp_ref, vc_ref, qn_ref, bias_ref, sink_ref, kv):
    hd = ATT_HEAD_DIM
    blk = ATT_WINDOW
    sl = slice(kv * hd, (kv + 1) * hd)
    kk = jnp.concatenate([kp_ref[:, sl], kc_ref[:, sl]], axis=0)
    vv = jnp.concatenate([vp_ref[:, sl], vc_ref[:, sl]], axis=0)
    gq = qn_ref[...]
    raws, rinvs = [], []
    for r in range(ATT_GROUP):
        hh = kv * ATT_GROUP + r
        x = q_ref[:, hh * hd:(hh + 1) * hd]
        raws.append(x)
        rinvs.append(lax.rsqrt(jnp.mean(x * x, axis=-1, keepdims=True) + EPS))
    xh = jnp.concatenate([x * ri for x, ri in zip(raws, rinvs)], axis=0)
    rinv = jnp.concatenate(rinvs, axis=0)
    q8 = xh * gq
    s = _dot(q8, kk, NT) * (hd ** -0.5) + bias_ref[kv]
    colk = lax.broadcasted_iota(jnp.int32, (1, 2 * blk), 1)
    s = jnp.where((pl.program_id(0) > 0) | (colk >= blk), s, NEG)
    sink = sink_ref[kv]
    m = jnp.maximum(jnp.max(s, axis=-1, keepdims=True), sink)
    pexp = jnp.exp(s - m)
    e_sink = jnp.exp(sink - m)
    den = jnp.sum(pexp, axis=-1, keepdims=True) + e_sink
    prob = pexp / den
    return kk, vv, xh, rinv, q8, prob, e_sink / den


def _attn_specs(nb):
    blk = ATT_WINDOW
    cur = lambda i: (i, 0)
    prev = lambda i: (jnp.maximum(i - 1, 0), 0)
    return [
        pl.BlockSpec((blk, D_MODEL), cur),
        pl.BlockSpec((blk, KV_W), prev), pl.BlockSpec((blk, KV_W), cur),
        pl.BlockSpec((blk, KV_W), prev), pl.BlockSpec((blk, KV_W), cur),
        pl.BlockSpec((1, ATT_HEAD_DIM), lambda i: (0, 0)),
        pl.BlockSpec((ATT_KV_HEADS, ATT_GROUP * blk, 2 * blk), lambda i: (0, 0, 0)),
        pl.BlockSpec((ATT_KV_HEADS, ATT_GROUP * blk, 1), lambda i: (0, 0, 0)),
    ]


def attn_fwd(q_raw, k, v, q_norm, bias, sink_col, nm):
    t_dim = q_raw.shape[0]
    blk, hd = ATT_WINDOW, ATT_HEAD_DIM
    nb = t_dim // blk

    def body(q_ref, kp_ref, kc_ref, vp_ref, vc_ref, qn_ref, bias_ref, sink_ref, o_ref):
        for kv in range(ATT_KV_HEADS):
            kk, vv, xh, rinv, q8, prob, p_sink = _attn_scores(q_ref, kp_ref, kc_ref, vp_ref, vc_ref, qn_ref,
                                                              bias_ref, sink_ref, kv)
            o8 = _dot(prob, vv)
            for r in range(ATT_GROUP):
                hh = kv * ATT_GROUP + r
                o_ref[:, hh * hd:(hh + 1) * hd] = o8[r * blk:(r + 1) * blk].astype(o_ref.dtype)

    out, = pcall(body, name=nm, grid=(nb,), in_specs=_attn_specs(nb),
                 out_specs=[pl.BlockSpec((blk, D_MODEL), lambda i: (i, 0))],
                 out_shape=[jax.ShapeDtypeStruct((t_dim, D_MODEL), BF16)],
                 args=[q_raw, k, k, v, v, q_norm, bias, sink_col])
    return out


def attn_bwd(do, q_raw, k, v, q_norm, bias, sink_col, nm, hook=None):
    t_dim = q_raw.shape[0]
    blk, hd = ATT_WINDOW, ATT_HEAD_DIM
    nb = t_dim // blk
    scale = hd ** -0.5

    def body(do_ref, q_ref, kp_ref, kc_ref, vp_ref, vc_ref, qn_ref, bias_ref, sink_ref,
             dq_ref, dkc_ref, dkp_ref, dvc_ref, dvp_ref, dbias_ref, dsink_ref, dqn_ref):
        @pl.when(pl.program_id(0) == 0)
        def _():
            dbias_ref[...] = jnp.zeros_like(dbias_ref)
            dsink_ref[...] = jnp.zeros_like(dsink_ref)
            dqn_ref[...] = jnp.zeros_like(dqn_ref)

        gq = qn_ref[...]
        for kv in range(ATT_KV_HEADS):
            kk, vv, xh, rinv, q8, prob, p_sink = _attn_scores(q_ref, kp_ref, kc_ref, vp_ref, vc_ref, qn_ref,
                                                              bias_ref, sink_ref, kv)
            do8 = jnp.concatenate([do_ref[:, (kv * ATT_GROUP + r) * hd:(kv * ATT_GROUP + r + 1) * hd]
                                   for r in range(ATT_GROUP)], axis=0)
            dp = _dot(do8, vv, NT)
            delta = jnp.sum(prob * dp, axis=-1, keepdims=True)
            ds = prob * (dp - delta)
            dsink_ref[kv] += -p_sink * delta
            dbias_ref[kv] += ds
            ds_s = ds * scale
            dq8 = _dot(ds_s, kk)
            dkk = _dot(ds_s, q8, TN)
            dvv = _dot(prob, do8, TN)
            dqn_ref[...] += jnp.sum(dq8 * xh, axis=0, keepdims=True)
            dxh = dq8 * gq
            dq_raw8 = rinv * (dxh - xh * jnp.mean(dxh * xh, axis=-1, keepdims=True))
            for r in range(ATT_GROUP):
                hh = kv * ATT_GROUP + r
                dq_ref[:, hh * hd:(hh + 1) * hd] = dq_raw8[r * blk:(r + 1) * blk].astype(dq_ref.dtype)
            sl = slice(kv * hd, (kv + 1) * hd)
            dkp_ref[:, sl] = dkk[:blk]
            dkc_ref[:, sl] = dkk[blk:]
            dvp_ref[:, sl] = dvv[:blk]
            dvc_ref[:, sl] = dvv[blk:]

    cur = lambda i: (i, 0)
    row_spec = pl.BlockSpec((blk, KV_W), cur)
    out_specs = [
        pl.BlockSpec((blk, D_MODEL), cur), row_spec, row_spec, row_spec, row_spec,
        pl.BlockSpec((ATT_KV_HEADS, ATT_GROUP * blk, 2 * blk), lambda i: (0, 0, 0)),
        pl.BlockSpec((ATT_KV_HEADS, ATT_GROUP * blk, 1), lambda i: (0, 0, 0)),
        pl.BlockSpec((1, hd), lambda i: (0, 0)),
    ]
    kvs = jax.ShapeDtypeStruct((t_dim, KV_W), F32)
    out_shape = [
        jax.ShapeDtypeStruct((t_dim, D_MODEL), BF16), kvs, kvs, kvs, kvs,
        jax.ShapeDtypeStruct((ATT_KV_HEADS, ATT_GROUP * blk, 2 * blk), F32),
        jax.ShapeDtypeStruct((ATT_KV_HEADS, ATT_GROUP * blk, 1), F32),
        jax.ShapeDtypeStruct((1, hd), F32),
    ]
    return pcall(body, name=nm, grid=(nb,), in_specs=[pl.BlockSpec((blk, D_MODEL), cur)] + _attn_specs(nb),
                 out_specs=out_specs, out_shape=out_shape,
                 args=[do, q_raw, k, k, v, v, q_norm, bias, sink_col], hook=hook)


def _t5_bucket_np():
    blk = ATT_WINDOW
    qi = np.arange(blk)[:, None] + blk
    kj = np.arange(2 * blk)[None, :]
    dist = qi - kj
    n = np.maximum(dist, 0)
    max_exact = REL_BUCKETS // 2
    nf = np.maximum(n, 1).astype(np.float32)
    large = max_exact + (np.log(nf / max_exact) / math.log(ATT_WINDOW / max_exact)
                         * (REL_BUCKETS - max_exact)).astype(np.int32)
    large = np.minimum(large, REL_BUCKETS - 1)
    bucket = np.where(n < max_exact, n, large)
    in_window = (dist >= 0) & (dist < ATT_WINDOW)
    return bucket, in_window


def attn_block_fwd(h, k, v, p, nm):
    u, = rowmap(lambda x, gg: (_rms_fwd(x, gg),), [h], [p["attn_norm"]], [(D_MODEL, BF16)], tm=256, name=nm + "_norm")
    q_raw = mm(u, p["w_q"], name=nm + "_q")
    o = attn_fwd(q_raw, k, v, p["q_norm"], p["bias"], p["sink_col"], nm + "_core")
    out = mm(o, p["w_o"], res=h, name=nm + "_o")
    return out, (u, q_raw, o)


def attn_block_bwd(h, k, v, p, saved, dout, dout_bf, nm, io):
    u, q_raw, o = saved
    g = {}
    io.put("w_o", mm(o, dout_bf, ta=True, out_dtype=BF16, name=nm + "_dwo", hook=io.hook(nm + "_dwo")))
    do = mm(dout_bf, p["w_o"], tb=True, name=nm + "_do")
    dq_raw, dkc, dkp, dvc, dvp, g["bias"], g["sink_col"], g["q_norm"] = attn_bwd(
        do, q_raw, k, v, p["q_norm"], p["bias"], p["sink_col"], nm + "_dcore", hook=io.hook(nm + "_dcore"))
    io.put("w_q", mm(u, dq_raw, ta=True, out_dtype=BF16, name=nm + "_dwq"))
    du = mm(dq_raw, p["w_q"], tb=True, name=nm + "_du")
    dh, dh_bf, g["attn_norm"] = norm_bwd(h, p["attn_norm"], du, dout, nm + "_dnorm")
    return dh, dh_bf, g, (dkc, dkp, dvc, dvp)


FFN_TAGS = ["00", "01", "10", "11"]


def local_step(x, target, small, io):
    bucket, in_window = _t5_bucket_np()
    blk = ATT_WINDOW
    w = small

    fnorm = {tag: w["ffn_norm"][int(tag[0]), int(tag[1])][None, :] for tag in FFN_TAGS}
    a_neg = -jnp.exp(w["ssm_a_log"][0])

    def mamba_p():
        w_int = io.w("w_int")
        return dict(ssm_norm=w["ssm_norm"], w_zt=w_int[:SSM_D_INNER],
                    w_xbct=w_int[SSM_D_INNER:SSM_D_INNER + SSM_CONV_DIM], w_dtt=w_int[SSM_D_INNER + SSM_CONV_DIM:],
                    conv_w=w["ssm_conv_w"][0], conv_b=w["ssm_conv_b"], dt_bias=w["ssm_dt_bias"],
                    a_heads=a_neg[None, :], a_row=a_neg.reshape(SSM_GROUPS, 1, HPG),
                    a_col=a_neg.reshape(SSM_GROUPS, HPG, 1),
                    dvec=jnp.repeat(w["ssm_d"][0], SSM_HEAD_DIM)[None, :],
                    gate_norm=w["ssm_gate_norm"], w_out=io.w("w_out"))

    rb = w["rel_bias"]
    onehot3 = (np.arange(REL_BUCKETS)[:, None, None] == bucket[None]).astype(np.float32)
    bias = jnp.einsum("bh,bqk->hqk", rb, onehot3, precision=lax.Precision.HIGHEST)
    bias = jnp.where(in_window[None], bias, NEG)
    bias = bias.reshape(ATT_KV_HEADS, ATT_GROUP * blk, 2 * blk)
    sink_col = jnp.repeat(w["sinks"][0], blk).reshape(ATT_KV_HEADS, ATT_GROUP * blk, 1)

    def attn_p():
        return dict(attn_norm=w["attn_norm"], w_q=io.w("w_q"), q_norm=w["q_norm"], bias=bias, sink_col=sink_col,
                    w_o=io.w("w_o"))

    def kv_p():
        return dict(kv_norm=w["kv_norm"][None, :], w_kv=io.w("w_kv"), k_norm=w["k_norm"][None, :])

    h0 = x
    h0a, s_f00 = ffn_fwd(h0, fnorm["00"], "00", io)
    mp = mamba_p()
    h0b, s_m = mamba_fwd(h0a, mp, "ssm", io)
    h1, s_f01 = ffn_fwd(h0b, fnorm["01"], "01", io)
    kp = kv_p()
    k, v, s_kv = kv_fwd(h1, kp, "kv")
    h1a, s_f10 = ffn_fwd(h1, fnorm["10"], "10", io)
    ap = attn_p()
    h1b, s_a = attn_block_fwd(h1a, k, v, ap, "att")
    h2, s_f11 = ffn_fwd(h1b, fnorm["11"], "11", io)

    def loss_fn(y, t):
        e = y - t
        d = e * (1.0 / D_MODEL)
        return d, d, jnp.sum(e * e, axis=0, keepdims=True)

    dh, dh_bf, sq = rowmap(loss_fn, [h2, target], [], [(D_MODEL, F32), (D_MODEL, BF16)], [(1, D_MODEL)], tm=256,
                           name="loss")
    loss_part = jnp.sum(sq) * (0.5 / D_MODEL)

    fg = {}

    def ffn_back(tag, h_in, saved, dh, dh_bf):
        dh, dh_bf, dg = ffn_bwd(h_in, fnorm[tag], tag, saved, dh, dh_bf, io)
        fg[tag] = dg[0]
        return dh, dh_bf

    dh, dh_bf = ffn_back("11", h1b, s_f11, dh, dh_bf)
    dh, dh_bf, ga, dkv = attn_block_bwd(h1a, k, v, ap, s_a, dh, dh_bf, "att", io)
    dh, dh_bf = ffn_back("10", h1, s_f10, dh, dh_bf)
    dh, dh_bf, gk = kv_bwd(h1, kp, s_kv, *dkv, dh, "kv", io)
    dh, dh_bf = ffn_back("01", h0b, s_f01, dh, dh_bf)
    dh, dh_bf, gm = mamba_bwd(h0a, mp, s_m, dh, dh_bf, "ssm", io)
    dh, dh_bf = ffn_back("00", h0, s_f00, dh, dh_bf)
    grad_x = dh

    grads = {}
    grads["ffn_norm"] = jnp.stack([fg[tag] for tag in FFN_TAGS]).reshape(2, 2, D_MODEL)
    grads["ssm_norm"] = gm["ssm_norm"]
    grads["ssm_conv_w"] = gm["conv_w"][None]
    grads["ssm_conv_b"] = gm["conv_b"]
    grads["ssm_dt_bias"] = gm["dt_bias"]
    grads["ssm_a_log"] = gm["a"] * a_neg[None, :]
    grads["ssm_d"] = jnp.sum(gm["dvec"].reshape(SSM_HEADS, SSM_HEAD_DIM), axis=1)[None, :]
    grads["ssm_gate_norm"] = gm["gate_norm"]
    grads["kv_norm"] = gk["kv_norm"][0]
    grads["k_norm"] = gk["k_norm"][0]
    grads["attn_norm"] = ga["attn_norm"]
    grads["q_norm"] = ga["q_norm"]
    grads["sinks"] = jnp.sum(ga["sink_col"].reshape(ATT_HEADS, blk), axis=1)[None, :]
    onehot = (np.arange(REL_BUCKETS)[:, None] == bucket.reshape(1, -1)).astype(np.float32)
    dbias2d = ga["bias"].reshape(ATT_HEADS, blk * 2 * blk)
    grads["rel_bias"] = mm(jnp.asarray(onehot, BF16), dbias2d, tb=True, name="drelbias")
    return loss_part, grad_x, grads


def _adamw(g, w, m, v):
    m = ADAM_B1 * m + (1.0 - ADAM_B1) * g
    v = ADAM_B2 * v + (1.0 - ADAM_B2) * (g * g)
    m_hat = m / (1.0 - ADAM_B1 ** ADAM_STEP)
    v_hat = v / (1.0 - ADAM_B2 ** ADAM_STEP)
    delta = -ADAM_LR * (m_hat / (jnp.sqrt(v_hat) + ADAM_EPS) + ADAM_WD * w)
    return delta, m, v


def _slot_sum(r):
    g = r[0].astype(F32)
    for d in range(1, r.shape[0]):
        g = g + r[d].astype(F32)
    return g


def adamw_rows(recvs, w, m, v, name):
    n_l, rows, width = w.shape
    n_slots = recvs[0].shape[0]
    tr = 32
    assert rows % tr == 0, rows
    nt = rows // tr

    def body(*refs):
        r_refs = refs[:n_l]
        w_ref, m_ref, v_ref, g_o, d_o, m_o, v_o = refs[n_l:]
        li = pl.program_id(0)
        for k in range(n_l):
            @pl.when(li == k)
            def _(k=k):
                g = _slot_sum(r_refs[k])
                delta, m2, v2 = _adamw(g, w_ref[0], m_ref[0], v_ref[0])
                g_o[0] = g
                d_o[0] = delta
                m_o[0] = m2
                v_o[0] = v2

    def r_spec(k):
        return pl.BlockSpec((n_slots, tr, width),
                            lambda li, j: (0, jnp.where(li == k, j, jnp.where(li > k, nt - 1, 0)), 0))

    w_spec = pl.BlockSpec((1, tr, width), lambda li, j: (li, j, 0))
    shp = jax.ShapeDtypeStruct(w.shape, F32)
    return pcall(body, name=name, grid=(n_l, nt), in_specs=[r_spec(k) for k in range(n_l)] + [w_spec] * 3,
                 out_specs=[w_spec] * 4, out_shape=[shp] * 4, args=list(recvs) + [w, m, v])


def adamw_cols(recvs, w, m, v, name):
    n_l, rows, n = w.shape
    n_slots = recvs[0].shape[0]
    tr = 256
    nt = rows // tr

    def body(*refs):
        r_refs = refs[:n_l]
        w_ref, m_ref, v_ref, g_o, d_o, m_o, v_o = refs[n_l:]
        li = pl.program_id(0)
        for k in range(n_l):
            @pl.when(li == k)
            def _(k=k):
                g = _slot_sum(r_refs[k]).T
                delta, m2, v2 = _adamw(g, w_ref[0], m_ref[0], v_ref[0])
                g_o[0] = g
                d_o[0] = delta
                m_o[0] = m2
                v_o[0] = v2

    def r_spec(k):
        return pl.BlockSpec((n_slots, n, tr),
                            lambda li, j: (0, 0, jnp.where(li == k, j, jnp.where(li > k, nt - 1, 0))))

    w_spec = pl.BlockSpec((1, tr, n), lambda li, j: (li, j, 0))
    shp = jax.ShapeDtypeStruct(w.shape, F32)
    return pcall(body, name=name, grid=(n_l, nt), in_specs=[r_spec(k) for k in range(n_l)] + [w_spec] * 3,
                 out_specs=[w_spec] * 4, out_shape=[shp] * 4, args=list(recvs) + [w, m, v])


WEIGHT_NAMES = ["ffn_norm", "ffn_w1", "ffn_w3", "ffn_w2", "ssm_norm", "ssm_w_in", "ssm_conv_w", "ssm_conv_b",
                "ssm_dt_bias", "ssm_a_log", "ssm_d", "ssm_gate_norm", "ssm_w_out", "kv_norm", "w_kv", "k_norm",
                "attn_norm", "w_q", "q_norm", "sinks", "w_o", "rel_bias"]

SMALL = [
    ("ffn_norm", (2, 2, 1024), 2), ("ssm_norm", (1, 1024), 1), ("ssm_conv_w", (1, 4, 3072), 2),
    ("ssm_conv_b", (1, 3072), 1), ("ssm_gate_norm", (1, 2048), 1),
    ("ssm_dt_bias", (1, 32), None), ("ssm_a_log", (1, 32), None), ("ssm_d", (1, 32), None),
    ("kv_norm", (1024,), None), ("k_norm", (64,), None), ("attn_norm", (1, 1024), None),
    ("q_norm", (1, 64), None), ("sinks", (1, 16), None), ("rel_bias", (32, 16), None),
]
SMALL_W = 1024
SMALL_FULL_ROWS = 32
SMALL_LOCAL_ROWS = 48

MAT_GROUPS = {
    "f00": ["w1t_00", "w3t_00", "w2_00"], "f01": ["w1t_01", "w3t_01", "w2_01"],
    "f10": ["w1t_10", "w3t_10", "w2_10"], "f11": ["w1t_11", "w3t_11", "w2_11"],
    "ssm": ["w_int", "w_out"], "att": ["w_q", "w_o", "w_kv"],
}
FIRST_GATHER = "f00"
GATHER_PLAN = {"f00_upgate": ["ssm"], "ssm_xbc": ["f01"], "ssm_ssd": ["att", "f10"], "f01_upgate": ["f11"]}
SCATTER_A_PLAN = {"att_dwo": "f11", "kv_dknorm": "f10", "kv_du": "att", "f01_du1": "f01", "ssm_du2": "ssm",
                  "f00_du1": "f00"}
SCATTER_B_PLAN = {"att_dcore": "f11", "f01_dw2": "att", "f01_dgate": "f10", "ssm_dssd": "f01", "f00_dgate": "ssm"}
LAST_SCATTER = "f00"
SLOT_MAJOR = ("w_int",)


def _shard_shape(s, a):
    return s[:a] + (s[a] // N_DEV,) + s[a + 1:]


def _unshard_view(stack, shard_shape, axis):
    moved = jnp.moveaxis(stack, 0, axis)
    return moved.reshape(shard_shape[:axis] + (N_DEV * shard_shape[axis],) + shard_shape[axis + 1:])


def _small_local(arrs):
    flat = jnp.concatenate([arrs[n].reshape(-1) for n, _, _ in SMALL])
    return jnp.pad(flat, (0, SMALL_LOCAL_ROWS * LANES - flat.shape[0])).reshape(SMALL_LOCAL_ROWS, LANES)


def chip_partial(g4, ra, name):
    _, _, n, width = g4.shape

    def body(g_ref, r_ref, o_ref):
        core = lax.axis_index("c")
        own = g_ref[0, pl.ds(core, 1)]
        o_ref[0] = (own[0].astype(F32) + r_ref[0, 0].astype(F32)).astype(o_ref.dtype)

    out, = pcall(body, name=name, grid=(N_CHIPS,),
                 in_specs=[pl.BlockSpec((1, 2, n, width), lambda q: (q, 0, 0, 0)),
                           pl.BlockSpec((1, 1, n, width), lambda q: (q, 0, 0, 0))],
                 out_specs=[pl.BlockSpec((1, n, width), lambda q: (q, 0, 0))],
                 out_shape=[jax.ShapeDtypeStruct((N_CHIPS, n, width), g4.dtype)], args=[g4, ra])
    return out


class StepIO:
    def __init__(self, pieces):
        self.pieces = pieces
        self.full = {}
        self.grad = {}
        self.from_sibling = {}
        self.recv = {}

    def w(self, name):
        return self.full[name]

    def put(self, name, g):
        self.grad[name] = g

    def _by_chip_core(self, name):
        g = self.grad[name]
        return g.reshape((N_CHIPS, 2, g.shape[0] // N_DEV) + g.shape[1:])

    def gather_items(self, groups):
        names = [n for grp in groups for n in MAT_GROUPS[grp]]
        items = [("g2", self.pieces[n], None if n in SLOT_MAJOR else 0) for n in names]

        def sink(outs):
            for n, o in zip(names, outs):
                self.full[n] = o.reshape((-1,) + o.shape[2:]) if n in SLOT_MAJOR else o

        return items, sink

    def scatter_a_items(self, group):
        names = MAT_GROUPS[group]
        items = [("sa", self._by_chip_core(n), None) for n in names]

        def sink(outs):
            for n, o in zip(names, outs):
                self.from_sibling[n] = o

        return items, sink

    def scatter_b_items(self, group):
        names = MAT_GROUPS[group]
        items = [("sb", chip_partial(self._by_chip_core(n), self.from_sibling[n], "partial_" + n), None)
                 for n in names]

        def sink(outs):
            for n, o in zip(names, outs):
                self.recv[n] = o

        return items, sink

    def hook(self, site):
        parts = []
        if site in GATHER_PLAN:
            parts.append(self.gather_items(GATHER_PLAN[site]))
        if site in SCATTER_A_PLAN:
            parts.append(self.scatter_a_items(SCATTER_A_PLAN[site]))
        if site in SCATTER_B_PLAN:
            parts.append(self.scatter_b_items(SCATTER_B_PLAN[site]))
        if not parts:
            return None
        return combine_hooks(parts)


def combine_hooks(parts):
    items = [it for its, _ in parts for it in its]

    def sink(outs):
        p = 0
        for its, snk in parts:
            snk(outs[p:p + len(its)])
            p += len(its)

    return Comm(items), sink


def step(x, target, wts, ms, vs):
    me = _my_index()

    pieces = {}
    for li in range(2):
        for hi in range(2):
            tag = "%d%d" % (li, hi)
            pieces["w1t_" + tag] = wts["ffn_w1"][li, hi].T.astype(BF16)
            pieces["w3t_" + tag] = wts["ffn_w3"][li, hi].T.astype(BF16)
            pieces["w2_" + tag] = wts["ffn_w2"][li, hi].astype(BF16)
    pieces["w_int"] = wts["ssm_w_in"][0].T.astype(BF16)
    pieces["w_out"] = wts["ssm_w_out"][0].astype(BF16)
    pieces["w_kv"] = wts["w_kv"].astype(BF16)
    pieces["w_q"] = wts["w_q"][0].astype(BF16)
    pieces["w_o"] = wts["w_o"][0].astype(BF16)
    io = StepIO(pieces)

    small_sharded = [(n, s, a) for n, s, a in SMALL if a is not None]
    loc = jnp.concatenate([wts[n].reshape(-1) for n, _, _ in small_sharded])
    loc_rows = -(-loc.shape[0] // (8 * LANES)) * 8
    loc = jnp.pad(loc, (0, loc_rows * LANES - loc.shape[0])).reshape(loc_rows, LANES)
    got_small = []
    comm, sink = combine_hooks([io.gather_items([FIRST_GATHER]), ([("g", loc, None)], got_small.extend)])
    sink(comm_only(comm, "gather_first"))
    gath_small = got_small[0].reshape(N_DEV, -1)
    small = {}
    off = 0
    for n, s, a in small_sharded:
        shard = _shard_shape(s, a)
        cnt = int(np.prod(shard))
        small[n] = _unshard_view(gath_small[:, off:off + cnt].reshape((N_DEV,) + shard), shard, a)
        off += cnt
    for n, s, a in SMALL:
        if a is None:
            small[n] = wts[n]

    loss_part, grad_x, g_small_local = local_step(x[0], target[0], small, io)
    loss = lax.psum(loss_part, ("x", "y", "c"))

    small_flat = jnp.concatenate([g_small_local[n].reshape(-1) for n, _, _ in SMALL])
    small_buf = jnp.pad(small_flat, (0, SMALL_FULL_ROWS * SMALL_W - small_flat.shape[0]))
    small_buf = small_buf.reshape(SMALL_FULL_ROWS, SMALL_W)
    got_small = []
    comm, sink = combine_hooks([io.scatter_b_items(LAST_SCATTER), ([("g", small_buf, None)], got_small.extend)])
    sink(comm_only(comm, "exchange_last"))
    small_all = got_small[0]

    def sum_body(r_ref, o_ref):
        o_ref[...] = _slot_sum(r_ref)

    vmem = pl.BlockSpec(memory_space=pltpu.VMEM)
    small_sum, = pcall(sum_body, name="sum_small", grid=(), in_specs=[vmem], out_specs=[vmem],
                       out_shape=[jax.ShapeDtypeStruct((SMALL_FULL_ROWS, SMALL_W), F32)], args=[small_all])
    small_sum = small_sum.reshape(-1)
    g_small = {}
    off = 0
    for n, s, a in SMALL:
        cnt = int(np.prod(s))
        gfull = small_sum[off:off + cnt].reshape(s)
        off += cnt
        if a is None:
            g_small[n] = gfull
        else:
            width = s[a] // N_DEV
            g_small[n] = lax.dynamic_slice_in_dim(gfull, me * width, width, axis=a)

    out = {}

    def emit(name, res, shape):
        for kind, arr in zip(("grad", "delta", "new_m", "new_v"), res):
            out[kind + "_" + name] = arr.reshape(shape)

    for name, key in (("ffn_w1", "w1t_"), ("ffn_w3", "w3t_")):
        shp = wts[name].shape
        view = lambda t: t.reshape((4,) + shp[2:])
        res = adamw_cols([io.recv[key + tag] for tag in FFN_TAGS], view(wts[name]), view(ms[name]), view(vs[name]),
                         "adamw_" + name)
        emit(name, res, shp)
    shp = wts["ffn_w2"].shape
    view = lambda t: t.reshape((4,) + shp[2:])
    res = adamw_rows([io.recv["w2_" + tag] for tag in FFN_TAGS], view(wts["ffn_w2"]), view(ms["ffn_w2"]),
                     view(vs["ffn_w2"]), "adamw_ffn_w2")
    emit("ffn_w2", res, shp)
    res = adamw_cols([io.recv["w_int"]], wts["ssm_w_in"], ms["ssm_w_in"], vs["ssm_w_in"], "adamw_ssm_w_in")
    emit("ssm_w_in", res, wts["ssm_w_in"].shape)
    for name, key in (("ssm_w_out", "w_out"), ("w_kv", "w_kv"), ("w_q", "w_q"), ("w_o", "w_o")):
        shp = wts[name].shape
        view = lambda t: t.reshape((1,) + shp[-2:])
        res = adamw_rows([io.recv[key]], view(wts[name]), view(ms[name]), view(vs[name]), "adamw_" + name)
        emit(name, res, shp)

    res_s = rowmap(lambda gg, ww, mm_, vv: _adamw(gg, ww, mm_, vv),
                   [_small_local(g_small), _small_local(wts), _small_local(ms), _small_local(vs)], [],
                   [(LANES, F32)] * 3, tm=SMALL_LOCAL_ROWS, name="adamw_small")
    flat_s = [r.reshape(-1) for r in res_s]
    off = 0
    for n, s, a in SMALL:
        shard = s if a is None else _shard_shape(s, a)
        cnt = int(np.prod(shard))
        out["grad_" + n] = g_small[n]
        for kind, arr in zip(("delta", "new_m", "new_v"), flat_s):
            out[kind + "_" + n] = arr[off:off + cnt].reshape(shard)
        off += cnt
    out["loss"] = loss
    out["grad_x"] = grad_x[None]
    return out


def kernel(x, ffn_norm, ffn_w1, ffn_w3, ffn_w2, ssm_norm, ssm_w_in, ssm_conv_w, ssm_conv_b, ssm_dt_bias, ssm_a_log, ssm_d, ssm_gate_norm, ssm_w_out, kv_norm, w_kv, k_norm, attn_norm, w_q, q_norm, sinks, w_o, rel_bias, loss_target, m_ffn_norm, m_ffn_w1, m_ffn_w3, m_ffn_w2, m_ssm_norm, m_ssm_w_in, m_ssm_conv_w, m_ssm_conv_b, m_ssm_dt_bias, m_ssm_a_log, m_ssm_d, m_ssm_gate_norm, m_ssm_w_out, m_kv_norm, m_w_kv, m_k_norm, m_attn_norm, m_w_q, m_q_norm, m_sinks, m_w_o, m_rel_bias, v_ffn_norm, v_ffn_w1, v_ffn_w3, v_ffn_w2, v_ssm_norm, v_ssm_w_in, v_ssm_conv_w, v_ssm_conv_b, v_ssm_dt_bias, v_ssm_a_log, v_ssm_d, v_ssm_gate_norm, v_ssm_w_out, v_kv_norm, v_w_kv, v_k_norm, v_attn_norm, v_w_q, v_q_norm, v_sinks, v_w_o, v_rel_bias):
    args = locals()
    wts = {n: args[n] for n in WEIGHT_NAMES}
    ms = {n: args["m_" + n] for n in WEIGHT_NAMES}
    vs = {n: args["v_" + n] for n in WEIGHT_NAMES}
    out = step(x, loss_target, wts, ms, vs)
    result = [out["loss"], out["grad_x"]]
    for kind in ("grad", "delta", "new_m", "new_v"):
        result += [out[kind + "_" + n] for n in WEIGHT_NAMES]
    return tuple(result)
```

```python
import functools
import math
import operator

import numpy as np
import jax
import jax.numpy as jnp
from jax import lax
from jax.experimental import pallas as pl
from jax.experimental.pallas import tpu as pltpu

F32 = jnp.float32
BF16 = jnp.bfloat16

D_MODEL = 1024
D_FF = 2816
N_DEV = 8
SSM_D_INNER = 2048
SSM_HEAD_DIM = 64
SSM_HEADS = 32
SSM_GROUPS = 4
SSM_STATE = 128
SSM_CONV = 4
SSM_CHUNK = 256
SSM_CONV_DIM = SSM_D_INNER + 2 * SSM_GROUPS * SSM_STATE
SSM_IN_DIM = SSM_D_INNER + SSM_CONV_DIM + SSM_HEADS
ATT_HEAD_DIM = 64
ATT_HEADS = 16
ATT_KV_HEADS = 2
ATT_GROUP = 8
ATT_WINDOW = 128
REL_BUCKETS = 32
EPS = 1e-6
NEG = -1e30

ADAM_LR = 0.001
ADAM_B1 = 0.9
ADAM_B2 = 0.999
ADAM_EPS = 1e-08
ADAM_WD = 0.01
ADAM_STEP = 10

VMEM_LIMIT_BYTES = 52 * 1024 * 1024
LANES = 128
MESH_ID = pl.DeviceIdType.MESH
ANY_SPEC = pl.BlockSpec(memory_space=pl.ANY)

NT = (((1,), (1,)), ((), ()))
TN = (((0,), (0,)), ((), ()))
NN = (((1,), (0,)), ((), ()))


def _pick(dim, cands):
    for c in cands:
        if dim % c == 0:
            return c
    return dim


def _my_index():
    return 4 * lax.axis_index("x") + 2 * lax.axis_index("y") + lax.axis_index("c")


def _peer(k):
    x, y, c = lax.axis_index("x"), lax.axis_index("y"), lax.axis_index("c")
    px = 1 - x if (k >> 2) & 1 else x
    py = 1 - y if (k >> 1) & 1 else y
    pc = 1 - c if k & 1 else c
    return (px, py, pc), 4 * px + 2 * py + pc


def _piece(ref, axis, d, n):
    if axis is None:
        return ref.at[d]
    return ref.at[(slice(None),) * axis + (pl.ds(pl.multiple_of(d * n, 8), n),)]


SIBLING = 1
CHIP_PEERS = (4, 2, 6)
N_CHIPS = 4
SEMS_PER_ITEM = N_DEV - 1


def _my_chip():
    return 2 * lax.axis_index("x") + lax.axis_index("y")


class Comm:
    def __init__(self, items):
        self.items = list(items)

    def dst_shapes(self):
        out = []
        for kind, src, axis in self.items:
            s = tuple(src.shape)
            if kind == "g":
                shp = (N_DEV,) + s
            elif kind == "g2":
                shp = (N_DEV,) + s if axis is None else s[:axis] + (N_DEV * s[axis],) + s[axis + 1:]
            elif kind == "sa":
                shp = (s[0], 1) + s[2:]
            else:
                shp = s
            out.append(jax.ShapeDtypeStruct(shp, src.dtype))
        return out

    def scratch(self):
        n = len(self.items)
        return [pltpu.SemaphoreType.DMA((n * SEMS_PER_ITEM,)), pltpu.SemaphoreType.DMA((n * SEMS_PER_ITEM,)),
                pltpu.SemaphoreType.DMA((n,))]

    def _run(self, srcs, dsts, sems, starting):
        send_sems, recv_sems, local_sems = sems
        me = _my_index()
        core = lax.axis_index("c")
        chip = _my_chip()
        for i, (kind, src, axis) in enumerate(self.items):
            s_ref, d_ref = srcs[i], dsts[i]
            base = i * SEMS_PER_ITEM

            def rdma(src_ref, dst_ref, j, peer):
                return pltpu.make_async_remote_copy(
                    src_ref=src_ref, dst_ref=dst_ref, send_sem=send_sems.at[base + j], recv_sem=recv_sems.at[base + j],
                    device_id=peer, device_id_type=MESH_ID)

            if kind == "g":
                local = pltpu.make_async_copy(s_ref, d_ref.at[me], local_sems.at[i])
                outs = [rdma(s_ref, d_ref.at[me], k - 1, _peer(k)[0]) for k in range(1, N_DEV)]
                if starting:
                    local.start()
                    for cp in outs:
                        cp.start()
                else:
                    for k in range(1, N_DEV):
                        rdma(s_ref, d_ref.at[_peer(k)[1]], k - 1, _peer(k)[0]).wait_recv()
                    for cp in outs:
                        cp.wait_send()
                    local.wait()
            elif kind == "g2":
                n = None if axis is None else src.shape[axis]
                mine = _piece(d_ref, axis, me, n)
                sib = _peer(SIBLING)[0]
                local = pltpu.make_async_copy(s_ref, mine, local_sems.at[i])
                outs = [rdma(s_ref, mine, 0, sib)] + [rdma(s_ref, mine, 1 + j, _peer(k)[0])
                                                      for j, k in enumerate(CHIP_PEERS)]
                if starting:
                    local.start()
                    for cp in outs:
                        cp.start()
                else:
                    passed = []
                    for j, k in enumerate(CHIP_PEERS):
                        theirs = _piece(d_ref, axis, _peer(k)[1], n)
                        rdma(s_ref, theirs, 1 + j, _peer(k)[0]).wait_recv()
                        fwd = rdma(theirs, theirs, 4 + j, sib)
                        fwd.start()
                        passed.append(fwd)
                    rdma(s_ref, _piece(d_ref, axis, _peer(SIBLING)[1], n), 0, sib).wait_recv()
                    for j, k in enumerate(CHIP_PEERS):
                        rdma(s_ref, _piece(d_ref, axis, _peer(k ^ SIBLING)[1], n), 4 + j, sib).wait_recv()
                    for cp in outs + passed:
                        cp.wait_send()
                    local.wait()
            elif kind == "sa":
                cp = rdma(s_ref.at[(slice(None), pl.ds(1 - core, 1))], d_ref, 0, _peer(SIBLING)[0])
                if starting:
                    cp.start()
                else:
                    cp.wait_recv()
                    cp.wait_send()
            else:
                local = pltpu.make_async_copy(s_ref.at[chip], d_ref.at[chip], local_sems.at[i])
                outs = [rdma(s_ref.at[_peer(k)[1] >> 1], d_ref.at[chip], 1 + j, _peer(k)[0])
                        for j, k in enumerate(CHIP_PEERS)]
                if starting:
                    local.start()
                    for cp in outs:
                        cp.start()
                else:
                    for j, k in enumerate(CHIP_PEERS):
                        rdma(s_ref.at[chip], d_ref.at[_peer(k)[1] >> 1], 1 + j, _peer(k)[0]).wait_recv()
                    for cp in outs:
                        cp.wait_send()
                    local.wait()

    def start(self, srcs, dsts, sems):
        self._run(srcs, dsts, sems, True)

    def wait(self, srcs, dsts, sems):
        self._run(srcs, dsts, sems, False)


def pcall(body, *, name, grid, in_specs, out_specs, out_shape, args, scratch=(), hook=None):
    cparams = pltpu.CompilerParams(dimension_semantics=("arbitrary",) * len(grid), vmem_limit_bytes=VMEM_LIMIT_BYTES)
    if hook is None:
        outs = pl.pallas_call(body, name=name, grid=grid, in_specs=list(in_specs), out_specs=list(out_specs),
                              out_shape=list(out_shape), scratch_shapes=list(scratch), compiler_params=cparams)(*args)
        return list(outs)
    comm, sink = hook
    n_in, n_out, n_scr, n_it = len(args), len(out_shape), len(scratch), len(comm.items)
    dims = tuple(grid)

    def wrapped(*refs):
        p = 0
        ins = refs[p:p + n_in]
        p += n_in
        csrc = refs[p:p + n_it]
        p += n_it
        outs = refs[p:p + n_out]
        p += n_out
        cdst = refs[p:p + n_it]
        p += n_it
        scr = refs[p:p + n_scr]
        p += n_scr
        sems = refs[p:p + 3]
        if dims:
            ids = [pl.program_id(a) for a in range(len(dims))]
            first = functools.reduce(operator.and_, [i == 0 for i in ids])
            last = functools.reduce(operator.and_, [i == d - 1 for i, d in zip(ids, dims)])

            @pl.when(first)
            def _():
                comm.start(csrc, cdst, sems)

            body(*ins, *outs, *scr)

            @pl.when(last)
            def _():
                comm.wait(csrc, cdst, sems)
        else:
            comm.start(csrc, cdst, sems)
            body(*ins, *outs, *scr)
            comm.wait(csrc, cdst, sems)

    res = pl.pallas_call(
        wrapped, name=name, grid=grid,
        in_specs=list(in_specs) + [ANY_SPEC] * n_it, out_specs=list(out_specs) + [ANY_SPEC] * n_it,
        out_shape=list(out_shape) + comm.dst_shapes(), scratch_shapes=list(scratch) + comm.scratch(),
        compiler_params=cparams,
    )(*args, *[src for _, src, _ in comm.items])
    res = list(res)
    sink(res[n_out:])
    return res[:n_out]


def comm_only(comm, name):
    got = []
    pcall(lambda *refs: None, name=name, grid=(), in_specs=[], out_specs=[], out_shape=[], args=[],
          hook=(comm, got.extend))
    return got


def mm(a, b, *, ta=False, tb=False, out_dtype=F32, res=None, alpha=1.0, name, hook=None):
    if ta:
        k_dim, m_dim = a.shape
    else:
        m_dim, k_dim = a.shape
    if tb:
        n_dim, k2 = b.shape
    else:
        k2, n_dim = b.shape
    assert k_dim == k2, (a.shape, b.shape, ta, tb)
    tn = _pick(n_dim, (1024, 1408, 512, 256, 128))
    tm = _pick(m_dim, (1024, 1408, 512, 256, 128)) if tn <= 1024 else _pick(m_dim, (512, 256, 128))
    tk = _pick(k_dim, (512, 1408, 256, 128))
    nk = k_dim // tk
    has_res = res is not None
    dn = (((0 if ta else 1,), (1 if tb else 0,)), ((), ()))

    def body(*refs):
        if has_res:
            a_ref, b_ref, r_ref, o_ref, acc_ref = refs
        else:
            a_ref, b_ref, o_ref, acc_ref = refs
        k = pl.program_id(2)

        @pl.when(k == 0)
        def _():
            acc_ref[...] = jnp.zeros_like(acc_ref)

        acc_ref[...] += lax.dot_general(a_ref[...].astype(BF16), b_ref[...].astype(BF16), dn,
                                        preferred_element_type=F32)

        @pl.when(k == nk - 1)
        def _():
            r = acc_ref[...]
            if alpha != 1.0:
                r = r * alpha
            if has_res:
                r = r_ref[...] + r
            o_ref[...] = r.astype(o_ref.dtype)

    a_spec = pl.BlockSpec((tk, tm), lambda i, j, k: (k, i)) if ta else pl.BlockSpec((tm, tk), lambda i, j, k: (i, k))
    b_spec = pl.BlockSpec((tn, tk), lambda i, j, k: (j, k)) if tb else pl.BlockSpec((tk, tn), lambda i, j, k: (k, j))
    o_spec = pl.BlockSpec((tm, tn), lambda i, j, k: (i, j))
    in_specs = [a_spec, b_spec] + ([o_spec] if has_res else [])
    args = [a, b] + ([res] if has_res else [])
    out, = pcall(body, name=name, grid=(m_dim // tm, n_dim // tn, nk), in_specs=in_specs, out_specs=[o_spec],
                 out_shape=[jax.ShapeDtypeStruct((m_dim, n_dim), out_dtype)], args=args,
                 scratch=[pltpu.VMEM((tm, tn), F32)], hook=hook)
    return out


def rowmap(fn, rows, consts=(), out_rows=(), out_accs=(), *, tm, name, hook=None):
    first = rows[0][0] if isinstance(rows[0], tuple) else rows[0]
    t_dim = first.shape[0]
    assert t_dim % tm == 0, (t_dim, tm)
    n_r, n_c, n_o = len(rows), len(consts), len(out_rows)

    def body(*refs):
        ins = [r[...] for r in refs[:n_r + n_c]]
        o_refs = refs[n_r + n_c:]
        outs = tuple(fn(*ins))
        for o_ref, val in zip(o_refs[:n_o], outs[:n_o]):
            o_ref[...] = val.astype(o_ref.dtype)
        if out_accs:
            @pl.when(pl.program_id(0) == 0)
            def _():
                for o_ref in o_refs[n_o:]:
                    o_ref[...] = jnp.zeros_like(o_ref)

            for o_ref, val in zip(o_refs[n_o:], outs[n_o:]):
                o_ref[...] += val

    in_specs, args = [], []
    for r in rows:
        if isinstance(r, tuple):
            args.append(r[0])
            in_specs.append(r[1])
        else:
            args.append(r)
            in_specs.append(pl.BlockSpec((tm, r.shape[1]), lambda i: (i, 0)))
    for c in consts:
        args.append(c)
        in_specs.append(pl.BlockSpec(c.shape, lambda i, nd=c.ndim: (0,) * nd))
    out_specs = [pl.BlockSpec((tm, w), lambda i: (i, 0)) for (w, _) in out_rows]
    out_specs += [pl.BlockSpec(s, lambda i, nd=len(s): (0,) * nd) for s in out_accs]
    out_shape = [jax.ShapeDtypeStruct((t_dim, w), dt) for (w, dt) in out_rows]
    out_shape += [jax.ShapeDtypeStruct(s, F32) for s in out_accs]
    return pcall(body, name=name, grid=(t_dim // tm,), in_specs=in_specs, out_specs=out_specs, out_shape=out_shape,
                 args=args, hook=hook)


def _rms_fwd(x, g):
    r = lax.rsqrt(jnp.mean(x * x, axis=-1, keepdims=True) + EPS)
    return x * r * g


def _rms_bwd(x, g, dy):
    r = lax.rsqrt(jnp.mean(x * x, axis=-1, keepdims=True) + EPS)
    xh = x * r
    dg = jnp.sum(dy * xh, axis=0, keepdims=True)
    dxh = dy * g
    dx = r * (dxh - xh * jnp.mean(dxh * xh, axis=-1, keepdims=True))
    return dx, dg


def _sigmoid(x):
    return 1.0 / (1.0 + jnp.exp(-x))


def _silu(x):
    return x * _sigmoid(x)


def _silu_grad(x):
    s = _sigmoid(x)
    return s * (1.0 + x * (1.0 - s))


def _split3(x):
    hi = x.astype(BF16)
    r1 = x - hi.astype(F32)
    mid = r1.astype(BF16)
    lo = (r1 - mid.astype(F32)).astype(BF16)
    return hi, mid, lo


def _dot(a, b, dn=NN):
    return lax.dot_general(a.astype(BF16), b.astype(BF16), dn, preferred_element_type=F32)


def _col_of(mat, h):
    lane = lax.broadcasted_iota(jnp.int32, mat.shape, 1)
    return jnp.sum(jnp.where(lane == h, mat, 0.0), axis=1, keepdims=True)


FFN_TN = 1408


def ffn_upgate(h, g, w1t, w3t, nm, hook=None):
    t_dim = h.shape[0]
    tm = _pick(t_dim, (512, 256, 128))
    tn = FFN_TN

    n_j = D_FF // tn
    u_w = D_MODEL // n_j

    def body(h_ref, g_ref, w1_ref, w3_ref, u_ref, a_ref, b_ref, hm_ref):
        uu = _rms_fwd(h_ref[...], g_ref[...]).astype(BF16)
        for j in range(n_j):
            @pl.when(pl.program_id(0) == j)
            def _(j=j):
                u_ref[...] = uu[:, j * u_w:(j + 1) * u_w]

        a = lax.dot_general(uu, w1_ref[...], NT, preferred_element_type=F32)
        b = lax.dot_general(uu, w3_ref[...], NT, preferred_element_type=F32)
        a_ref[...] = a.astype(a_ref.dtype)
        b_ref[...] = b.astype(b_ref.dtype)
        hm_ref[...] = (_silu(a) * b).astype(hm_ref.dtype)

    row_spec = pl.BlockSpec((tm, D_MODEL), lambda j, i: (i, 0))
    w_spec = pl.BlockSpec((tn, D_MODEL), lambda j, i: (j, 0))
    o_spec = pl.BlockSpec((tm, tn), lambda j, i: (i, j))
    o_shape = jax.ShapeDtypeStruct((t_dim, D_FF), BF16)
    return pcall(body, name=nm, grid=(D_FF // tn, t_dim // tm),
                 in_specs=[row_spec, pl.BlockSpec((1, D_MODEL), lambda j, i: (0, 0)), w_spec, w_spec],
                 out_specs=[pl.BlockSpec((tm, u_w), lambda j, i: (i, j))] + [o_spec] * 3,
                 out_shape=[jax.ShapeDtypeStruct((t_dim, D_MODEL), BF16)] + [o_shape] * 3,
                 args=[h, g, w1t, w3t], hook=hook)


def ffn_dgate(dout_bf, w2, a, b, nm, hook=None):
    t_dim = dout_bf.shape[0]
    tm = _pick(t_dim, (512, 256, 128))
    tn = FFN_TN

    def body(d_ref, w2_ref, a_ref, b_ref, da_ref, db_ref):
        dhm = 0.5 * lax.dot_general(d_ref[...], w2_ref[...], NT, preferred_element_type=F32)
        av = a_ref[...].astype(F32)
        bv = b_ref[...].astype(F32)
        sg = _sigmoid(av)
        da_ref[...] = (dhm * bv * (sg * (1.0 + av * (1.0 - sg)))).astype(da_ref.dtype)
        db_ref[...] = (dhm * (av * sg)).astype(db_ref.dtype)

    t_spec = pl.BlockSpec((tm, tn), lambda j, i: (i, j))
    o_shape = jax.ShapeDtypeStruct((t_dim, D_FF), BF16)
    return pcall(body, name=nm, grid=(D_FF // tn, t_dim // tm),
                 in_specs=[pl.BlockSpec((tm, D_MODEL), lambda j, i: (i, 0)),
                           pl.BlockSpec((tn, D_MODEL), lambda j, i: (j, 0)), t_spec, t_spec],
                 out_specs=[t_spec] * 2, out_shape=[o_shape] * 2, args=[dout_bf, w2, a, b], hook=hook)


def ffn_fwd(h, g, tag, io):
    nm = "f" + tag
    w1t, w3t, w2 = io.w("w1t_" + tag), io.w("w3t_" + tag), io.w("w2_" + tag)
    u, a, b, hm = ffn_upgate(h, g, w1t, w3t, nm + "_upgate", hook=io.hook(nm + "_upgate"))
    out = mm(hm, w2, res=h, alpha=0.5, name=nm + "_down")
    return out, (u, a, b, hm)


def du_norm_bwd(pairs, h, g, dout, nm, hook=None):
    t_dim = h.shape[0]
    tm = 256
    n_p = len(pairs)

    def body(*refs):
        h_ref, d_ref, g_ref = refs[2 * n_p:2 * n_p + 3]
        dh_ref, dhb_ref, dg_ref = refs[2 * n_p + 3:]
        du = None
        for p, (_, _, tb) in enumerate(pairs):
            t = lax.dot_general(refs[2 * p][...].astype(BF16), refs[2 * p + 1][...].astype(BF16), NT if tb else NN,
                                preferred_element_type=F32)
            du = t if du is None else du + t
        dx, dg = _rms_bwd(h_ref[...], g_ref[...], du)
        dh = d_ref[...] + dx
        dh_ref[...] = dh
        dhb_ref[...] = dh.astype(dhb_ref.dtype)

        @pl.when(pl.program_id(0) == 0)
        def _():
            dg_ref[...] = jnp.zeros_like(dg_ref)

        dg_ref[...] += dg

    in_specs, args = [], []
    for a, b, _ in pairs:
        in_specs += [pl.BlockSpec((tm, a.shape[1]), lambda i: (i, 0)), pl.BlockSpec(b.shape, lambda i: (0, 0))]
        args += [a, b]
    row_spec = pl.BlockSpec((tm, D_MODEL), lambda i: (i, 0))
    vec_spec = pl.BlockSpec((1, D_MODEL), lambda i: (0, 0))
    return pcall(body, name=nm, grid=(t_dim // tm,), in_specs=in_specs + [row_spec, row_spec, vec_spec],
                 out_specs=[row_spec, row_spec, vec_spec],
                 out_shape=[jax.ShapeDtypeStruct((t_dim, D_MODEL), F32), jax.ShapeDtypeStruct((t_dim, D_MODEL), BF16),
                            jax.ShapeDtypeStruct((1, D_MODEL), F32)],
                 args=args + [h, dout, g], hook=hook)


def ffn_bwd(h, g, tag, saved, dout, dout_bf, io):
    nm = "f" + tag
    w1t, w3t, w2 = io.w("w1t_" + tag), io.w("w3t_" + tag), io.w("w2_" + tag)
    u, a, b, hm = saved
    io.put("w2_" + tag, mm(hm, dout_bf, ta=True, alpha=0.5, out_dtype=BF16, name=nm + "_dw2",
                           hook=io.hook(nm + "_dw2")))
    da, db = ffn_dgate(dout_bf, w2, a, b, nm + "_dgate", hook=io.hook(nm + "_dgate"))
    io.put("w1t_" + tag, mm(da, u, ta=True, out_dtype=BF16, name=nm + "_dw1"))
    io.put("w3t_" + tag, mm(db, u, ta=True, out_dtype=BF16, name=nm + "_dw3"))
    return du_norm_bwd([(da, w1t, False), (db, w3t, False)], h, g, dout, nm + "_du", hook=io.hook(nm + "_du"))


def _conv_pre(x, halo, w, b, tm):
    halo = jnp.where(pl.program_id(0) > 0, halo, 0.0)
    xx = jnp.concatenate([halo, x], axis=0)
    shifted = [xx[5 + k:5 + k + tm] for k in range(SSM_CONV)]
    acc = b + shifted[0] * w[0:1]
    for k in range(1, SSM_CONV):
        acc = acc + shifted[k] * w[k:k + 1]
    return acc, shifted


def _prev_halo_spec(tm, width):
    return pl.BlockSpec((8, width), lambda i: (jnp.maximum(i * (tm // 8) - 1, 0), 0))


def conv_fwd(xbc_raw, w, b, nm):
    tm = 128

    def fn(x, halo, ww, bb):
        acc, _ = _conv_pre(x, halo, ww, bb, tm)
        return (_silu(acc),)

    out, = rowmap(fn, [xbc_raw, (xbc_raw, _prev_halo_spec(tm, SSM_CONV_DIM))], [w, b],
                  [(SSM_CONV_DIM, F32)], tm=tm, name=nm)
    return out


def conv_bwd(xbc_raw, w, b, dxs, db_in, dc_in, nm):
    tm = 128
    t_dim = xbc_raw.shape[0]

    def fn1(x, halo, d1, d2, d3, ww, bb):
        acc, shifted = _conv_pre(x, halo, ww, bb, tm)
        dacc = jnp.concatenate([d1, d2, d3], axis=1) * _silu_grad(acc)
        dw = jnp.concatenate([jnp.sum(dacc * s, axis=0, keepdims=True) for s in shifted], axis=0)
        return dacc, dw, jnp.sum(dacc, axis=0, keepdims=True)

    dacc, dw, dbias = rowmap(fn1, [xbc_raw, (xbc_raw, _prev_halo_spec(tm, SSM_CONV_DIM)), dxs, db_in, dc_in],
                             [w, b], [(SSM_CONV_DIM, F32)], [(SSM_CONV, SSM_CONV_DIM), (1, SSM_CONV_DIM)],
                             tm=tm, name=nm + "_a")
    n_tiles = t_dim // tm

    def fn2(d, nxt, ww):
        nxt = jnp.where(pl.program_id(0) < n_tiles - 1, nxt, 0.0)
        dd = jnp.concatenate([d, nxt], axis=0)
        out = dd[3:3 + tm] * ww[0:1]
        for k in range(1, SSM_CONV):
            out = out + dd[3 - k:3 - k + tm] * ww[k:k + 1]
        return (out,)

    nxt_spec = pl.BlockSpec((8, SSM_CONV_DIM), lambda i: (jnp.minimum((i + 1) * (tm // 8), t_dim // 8 - 1), 0))
    dx, = rowmap(fn2, [dacc, (dacc, nxt_spec)], [w], [(SSM_CONV_DIM, BF16)], tm=tm, name=nm + "_b")
    return dx, dw, dbias


GRP_W = SSM_D_INNER // SSM_GROUPS
HPG = SSM_HEADS // SSM_GROUPS
HEAD_SHIFT = 6


def _split2(x):
    hi = x.astype(BF16)
    return hi, (x - hi.astype(F32)).astype(BF16)


def _expand_mats():
    e = ((lax.broadcasted_iota(jnp.int32, (HPG, GRP_W), 1) >> HEAD_SHIFT)
         == lax.broadcasted_iota(jnp.int32, (HPG, GRP_W), 0)).astype(BF16)
    et = ((lax.broadcasted_iota(jnp.int32, (GRP_W, HPG), 0) >> HEAD_SHIFT)
          == lax.broadcasted_iota(jnp.int32, (GRP_W, HPG), 1)).astype(BF16)
    return e, et


def _expand(v, e_m):
    hi, lo = _split2(v)
    return jnp.dot(hi, e_m, preferred_element_type=F32) + jnp.dot(lo, e_m, preferred_element_type=F32)


def _reduce8(v, et_m):
    acc = None
    for p in _split3(v):
        t = jnp.dot(p, et_m, preferred_element_type=F32)
        acc = t if acc is None else acc + t
    return acc


def _ssd_group_terms(dt_ref, dtT_ref, arow_ref, acol_ref):
    L = SSM_CHUNK
    r = lax.broadcasted_iota(jnp.int32, (L, L), 0)
    c = lax.broadcasted_iota(jnp.int32, (L, L), 1)
    tril = (r >= c).astype(BF16)
    triu = (r <= c).astype(BF16)
    dtg = dt_ref[0]
    acol = None
    for p in _split3(dtg * arow_ref[0]):
        t = jnp.dot(tril, p, preferred_element_type=F32)
        acol = t if acol is None else acol + t
    arowT = None
    for p in _split3(dtT_ref[0] * acol_ref[0]):
        t = jnp.dot(p, triu, preferred_element_type=F32)
        arowT = t if arowT is None else arowT + t
    return dtg, acol, arowT, r >= c


def _state_decay(a_last_col, et_m):
    hi, lo = _split2(jnp.broadcast_to(jnp.exp(a_last_col), (HPG, SSM_STATE)))
    return jnp.dot(et_m, hi, preferred_element_type=F32) + jnp.dot(et_m, lo, preferred_element_type=F32)


def _ssd_specs(nc, rev):
    L, N = SSM_CHUNK, SSM_STATE
    xcols = SSM_D_INNER // LANES
    ch = (lambda c: nc - 1 - c) if rev else (lambda c: c)
    return [
        pl.BlockSpec((L, GRP_W), lambda c, g: (ch(c), g)),
        pl.BlockSpec((L, N), lambda c, g: (ch(c), xcols + g)),
        pl.BlockSpec((L, N), lambda c, g: (ch(c), xcols + SSM_GROUPS + g)),
        pl.BlockSpec((1, L, HPG), lambda c, g: (g, ch(c), 0)),
        pl.BlockSpec((1, HPG, L), lambda c, g: (g, 0, ch(c))),
        pl.BlockSpec((1, 1, HPG), lambda c, g: (g, 0, 0)),
        pl.BlockSpec((1, HPG, 1), lambda c, g: (g, 0, 0)),
        pl.BlockSpec((1, GRP_W), lambda c, g: (0, g)),
    ]


def ssd_fwd(xbc, dt_g, dtT_g, a_row, a_col, dvec, nm, hook=None):
    t_dim = xbc.shape[0]
    L, P, N = SSM_CHUNK, SSM_HEAD_DIM, SSM_STATE
    nc = t_dim // L

    def body(x_ref, b_ref, c_ref, dt_ref, dtT_ref, arow_ref, acol_ref, dvec_ref, y_ref, st_ref, s_s):
        ci = pl.program_id(0)
        g = pl.program_id(1)

        @pl.when((ci == 0) & (g == 0))
        def _():
            s_s[...] = jnp.zeros_like(s_s)

        e_m, et_m = _expand_mats()
        dtg, acol, arowT, causal = _ssd_group_terms(dt_ref, dtT_ref, arow_ref, acol_ref)
        a_last_row = acol[L - 1:L, :]
        x = x_ref[...]
        bm = b_ref[...]
        cm = c_ref[...]
        cb = _dot(cm, bm, NT)
        s = s_s[g]
        st_ref[0, 0] = s
        ea_x = _expand(jnp.exp(acol), e_m)
        dt_x = _expand(dtg, e_m)
        w_x = _expand(jnp.exp(a_last_row - acol) * dtg, e_m)
        yb = ea_x * _dot(cm, s, NT) + dvec_ref[...] * x
        xd = (x * dt_x).astype(BF16)
        for e in range(HPG):
            sl = slice(e * P, (e + 1) * P)
            lm = jnp.exp(jnp.where(causal, acol[:, e:e + 1] - arowT[e:e + 1, :], NEG))
            m = (cb * lm).astype(BF16)
            y_ref[:, sl] = yb[:, sl] + jnp.dot(m, xd[:, sl], preferred_element_type=F32)
        s_s[g] = _state_decay(arowT[:, L - 1:L], et_m) * s + _dot(x * w_x, bm, TN)

    out_specs = [
        pl.BlockSpec((L, GRP_W), lambda c, g: (c, g)),
        pl.BlockSpec((1, 1, GRP_W, N), lambda c, g: (c, g, 0, 0)),
    ]
    return pcall(
        body, name=nm, grid=(nc, SSM_GROUPS), in_specs=_ssd_specs(nc, False), out_specs=out_specs,
        out_shape=[jax.ShapeDtypeStruct((t_dim, SSM_D_INNER), F32),
                   jax.ShapeDtypeStruct((nc, SSM_GROUPS, GRP_W, N), F32)],
        scratch=[pltpu.VMEM((SSM_GROUPS, GRP_W, N), F32)],
        args=[xbc, xbc, xbc, dt_g, dtT_g, a_row, a_col, dvec], hook=hook)


def ssd_bwd(dy, xbc, dt_g, dtT_g, a_row, a_col, dvec, states, nm, hook=None):
    t_dim = xbc.shape[0]
    L, P, N = SSM_CHUNK, SSM_HEAD_DIM, SSM_STATE
    nc = t_dim // L

    def body(dy_ref, x_ref, b_ref, c_ref, dt_ref, dtT_ref, arow_ref, acol_ref, dvec_ref, st_ref,
             dx_ref, db_ref, dc_ref, da_ref, ddt_ref, dd_ref, ds_s, yd_s, dxd_s):
        ci = pl.program_id(0)
        g = pl.program_id(1)

        @pl.when((ci == 0) & (g == 0))
        def _():
            ds_s[...] = jnp.zeros_like(ds_s)
            dd_ref[...] = jnp.zeros_like(dd_ref)

        e_m, et_m = _expand_mats()
        dtg, acol, arowT, causal = _ssd_group_terms(dt_ref, dtT_ref, arow_ref, acol_ref)
        a_last_row = acol[L - 1:L, :]
        x = x_ref[...]
        dy = dy_ref[...]
        bm = b_ref[...]
        cm = c_ref[...]
        cb = _dot(cm, bm, NT)
        s = st_ref[0, 0]
        dsp = ds_s[g]
        ew8 = jnp.exp(a_last_row - acol)
        ea_x = _expand(jnp.exp(acol), e_m)
        dt_x = _expand(dtg, e_m)
        ew_x = _expand(ew8, e_m)
        w_x = ew_x * dt_x
        z = _dot(cm, s, NT)
        dz = ea_x * dy
        dc = _dot(dz, s)
        ds_y = _dot(dz, cm, TN)
        du = _dot(bm, dsp, NT)
        u = x * w_x
        db = _dot(u, dsp)
        xd = (x * dt_x).astype(BF16)
        dyb = dy.astype(BF16)
        dcb = jnp.zeros((L, L), F32)
        for e in range(HPG):
            sl = slice(e * P, (e + 1) * P)
            lm = jnp.exp(jnp.where(causal, acol[:, e:e + 1] - arowT[e:e + 1, :], NEG))
            m = (cb * lm).astype(BF16)
            yd_s[:, sl] = jnp.dot(m, xd[:, sl], preferred_element_type=F32)
            dxd_s[:, sl] = lax.dot_general(m, dyb[:, sl], TN, preferred_element_type=F32)
            dcb = dcb + lax.dot_general(dyb[:, sl], xd[:, sl], NT, preferred_element_type=F32) * lm
        dxd = dxd_s[...]
        dx_ref[...] = dvec_ref[...] * dy + du * w_x + dt_x * dxd
        ddt = _reduce8(x * (ew_x * du + dxd), et_m)
        da = (_reduce8(dz * z + dyb.astype(F32) * yd_s[...], et_m)
              - _reduce8(xd.astype(F32) * dxd + du * u, et_m))
        dwa_row = _reduce8(jnp.broadcast_to(jnp.sum(du * u, axis=0, keepdims=True), (8, GRP_W)), et_m)[0:1]
        t_nh = None
        for p in _split3(dsp * s):
            t = lax.dot_general(p, et_m, TN, preferred_element_type=F32)
            t_nh = t if t_nh is None else t_nh + t
        d_last = dwa_row + jnp.exp(a_last_row) * jnp.sum(t_nh, axis=0, keepdims=True)
        row_l = lax.broadcasted_iota(jnp.int32, (L, 1), 0)
        da_ref[0] = da + jnp.where(row_l == L - 1, d_last, 0.0)
        ddt_ref[0] = ddt
        dd_ref[g] += jnp.sum(dy * x, axis=0, keepdims=True)
        dc_ref[...] = dc + _dot(dcb, bm)
        db_ref[...] = db + _dot(dcb, cm, TN)
        ds_s[g] = _state_decay(arowT[:, L - 1:L], et_m) * dsp + ds_y

    rc = lambda c: nc - 1 - c
    in_specs = ([pl.BlockSpec((L, GRP_W), lambda c, g: (rc(c), g))] + _ssd_specs(nc, True)
                + [pl.BlockSpec((1, 1, GRP_W, N), lambda c, g: (rc(c), g, 0, 0))])
    out_specs = [
        pl.BlockSpec((L, GRP_W), lambda c, g: (rc(c), g)),
        pl.BlockSpec((L, N), lambda c, g: (rc(c), g)),
        pl.BlockSpec((L, N), lambda c, g: (rc(c), g)),
        pl.BlockSpec((1, L, HPG), lambda c, g: (g, rc(c), 0)),
        pl.BlockSpec((1, L, HPG), lambda c, g: (g, rc(c), 0)),
        pl.BlockSpec((SSM_GROUPS, 1, GRP_W), lambda c, g: (0, 0, 0)),
    ]
    gn = SSM_GROUPS * N
    out_shape = [
        jax.ShapeDtypeStruct((t_dim, SSM_D_INNER), F32), jax.ShapeDtypeStruct((t_dim, gn), F32),
        jax.ShapeDtypeStruct((t_dim, gn), F32), jax.ShapeDtypeStruct((SSM_GROUPS, t_dim, HPG), F32),
        jax.ShapeDtypeStruct((SSM_GROUPS, t_dim, HPG), F32), jax.ShapeDtypeStruct((SSM_GROUPS, 1, GRP_W), F32),
    ]
    return pcall(
        body, name=nm, grid=(nc, SSM_GROUPS), in_specs=in_specs, out_specs=out_specs, out_shape=out_shape,
        scratch=[pltpu.VMEM((SSM_GROUPS, GRP_W, N), F32), pltpu.VMEM((L, GRP_W), F32), pltpu.VMEM((L, GRP_W), F32)],
        args=[dy, xbc, xbc, xbc, dt_g, dtT_g, a_row, a_col, dvec, states], hook=hook)


def _softplus(x):
    return jnp.maximum(x, 0.0) + jnp.log(1.0 + jnp.exp(-jnp.abs(x)))


def ssd_dt_bwd(da, ddt, dt, dt_raw, a_row, dt_bias, nm):
    L = SSM_CHUNK

    def fn(d_a, d_dt, dtv, raw, ar, bias):
        r = lax.broadcasted_iota(jnp.int32, (L, L), 0)
        c = lax.broadcasted_iota(jnp.int32, (L, L), 1)
        triu = (r <= c).astype(BF16)
        acc = None
        for p in _split3(d_a):
            t = jnp.dot(triu, p, preferred_element_type=F32)
            acc = t if acc is None else acc + t
        d_dt = d_dt + acc * ar
        d_a_h = jnp.sum(acc * dtv, axis=0, keepdims=True)
        d_raw = d_dt * _sigmoid(raw + bias)
        return d_raw, d_a_h, jnp.sum(d_raw, axis=0, keepdims=True)

    return rowmap(fn, [da, ddt, dt, dt_raw], [a_row, dt_bias], [(SSM_HEADS, BF16)],
                  [(1, SSM_HEADS), (1, SSM_HEADS)], tm=L, name=nm)


GN_W = SSM_D_INNER // SSM_GROUPS


def mamba_fwd(h, p, nm, io):
    u, = rowmap(lambda x, gg: (_rms_fwd(x, gg),), [h], [p["ssm_norm"]], [(D_MODEL, BF16)], tm=256, name=nm + "_norm")
    z = mm(u, p["w_zt"], tb=True, name=nm + "_z")
    xbc_raw = mm(u, p["w_xbct"], tb=True, name=nm + "_xbc", hook=io.hook(nm + "_xbc"))
    dt_raw = mm(u, p["w_dtt"], tb=True, name=nm + "_dt")
    xbc = conv_fwd(xbc_raw, p["conv_w"], p["conv_b"], nm + "_conv")
    dt, = rowmap(lambda r, b: (_softplus(r + b),), [dt_raw], [p["dt_bias"]], [(SSM_HEADS, F32)], tm=256,
                 name=nm + "_softplus")
    dt_g = dt.reshape(-1, SSM_GROUPS, HPG).transpose(1, 0, 2)
    dtT_g = dt_g.transpose(0, 2, 1)
    y, states = ssd_fwd(xbc, dt_g, dtT_g, p["a_row"], p["a_col"], p["dvec"], nm + "_ssd", hook=io.hook(nm + "_ssd"))

    def gate_norm(yv, zv, gg):
        t = yv * _silu(zv)
        return (jnp.concatenate([_rms_fwd(t[:, k * GN_W:(k + 1) * GN_W], gg[:, k * GN_W:(k + 1) * GN_W])
                                 for k in range(SSM_GROUPS)], axis=1),)

    yn, = rowmap(gate_norm, [y, z], [p["gate_norm"]], [(SSM_D_INNER, BF16)], tm=256, name=nm + "_gatenorm")
    out = mm(yn, p["w_out"], res=h, name=nm + "_out")
    return out, (u, z, xbc_raw, dt_raw, xbc, dt, dt_g, dtT_g, y, states, yn)


def mamba_bwd(h, p, saved, dout, dout_bf, nm, io):
    u, z, xbc_raw, dt_raw, xbc, dt, dt_g, dtT_g, y, states, yn = saved
    g = {}
    io.put("w_out", mm(yn, dout_bf, ta=True, out_dtype=BF16, name=nm + "_dwout"))
    dyn = mm(dout_bf, p["w_out"], tb=True, name=nm + "_dyn")

    def gate_norm_bwd(d, yv, zv, gg):
        sz = _silu(zv)
        t = yv * sz
        dts, dgs = [], []
        for k in range(SSM_GROUPS):
            sl = slice(k * GN_W, (k + 1) * GN_W)
            dt_k, dg_k = _rms_bwd(t[:, sl], gg[:, sl], d[:, sl])
            dts.append(dt_k)
            dgs.append(dg_k)
        d_t = jnp.concatenate(dts, axis=1)
        return d_t * sz, d_t * yv * _silu_grad(zv), jnp.concatenate(dgs, axis=1)

    dy, dz, g["gate_norm"] = rowmap(gate_norm_bwd, [dyn, y, z], [p["gate_norm"]],
                                    [(SSM_D_INNER, F32), (SSM_D_INNER, BF16)], [(1, SSM_D_INNER)], tm=256,
                                    name=nm + "_dgatenorm")
    dxs, db_in, dc_in, da_g, ddt_g, dd = ssd_bwd(
        dy, xbc, dt_g, dtT_g, p["a_row"], p["a_col"], p["dvec"], states, nm + "_dssd", hook=io.hook(nm + "_dssd"))
    g["dvec"] = dd
    per_head = lambda t: t.transpose(1, 0, 2).reshape(-1, SSM_HEADS)
    ddt_raw, g["a"], g["dt_bias"] = ssd_dt_bwd(per_head(da_g), per_head(ddt_g), dt, dt_raw, p["a_heads"],
                                               p["dt_bias"], nm + "_ddt")
    dxbc_raw, g["conv_w"], g["conv_b"] = conv_bwd(xbc_raw, p["conv_w"], p["conv_b"], dxs, db_in, dc_in, nm + "_dconv")
    io.put("w_int", jnp.concatenate([mm(dz, u, ta=True, out_dtype=BF16, name=nm + "_dwz"),
                                     mm(dxbc_raw, u, ta=True, out_dtype=BF16, name=nm + "_dwxbc"),
                                     mm(ddt_raw, u, ta=True, out_dtype=BF16, name=nm + "_dwdt")], axis=0))
    dh, dh_bf, g["ssm_norm"] = du_norm_bwd(
        [(dz, p["w_zt"], False), (dxbc_raw, p["w_xbct"], False), (ddt_raw, p["w_dtt"], False)],
        h, p["ssm_norm"], dout, nm + "_du", hook=io.hook(nm + "_du"))
    return dh, dh_bf, g


KV_W = ATT_KV_HEADS * ATT_HEAD_DIM


def kv_fwd(h, p, nm):
    u, = rowmap(lambda x, gg: (_rms_fwd(x, gg),), [h], [p["kv_norm"]], [(D_MODEL, BF16)], tm=256, name=nm + "_norm")
    kv_raw = mm(u, p["w_kv"], name=nm + "_proj")

    def knorm(t, gg):
        ks = [_rms_fwd(t[:, j * ATT_HEAD_DIM:(j + 1) * ATT_HEAD_DIM], gg) for j in range(ATT_KV_HEADS)]
        return jnp.concatenate(ks, axis=1), t[:, KV_W:]

    k, v = rowmap(knorm, [kv_raw], [p["k_norm"]], [(KV_W, F32), (KV_W, F32)], tm=256, name=nm + "_knorm")
    return k, v, (u, kv_raw)


def kv_bwd(h, p, saved, dk_cur, dk_prev, dv_cur, dv_prev, dout, nm, io):
    u, kv_raw = saved
    t_dim = h.shape[0]
    tm = ATT_WINDOW
    nb = t_dim // tm
    nxt = pl.BlockSpec((tm, KV_W), lambda i: (jnp.minimum(i + 1, nb - 1), 0))

    def fn(dkc, dkp, dvc, dvp, t, gg):
        live = pl.program_id(0) < nb - 1
        dk = dkc + jnp.where(live, dkp, 0.0)
        dv = dvc + jnp.where(live, dvp, 0.0)
        outs, dgs = [], None
        for j in range(ATT_KV_HEADS):
            sl = slice(j * ATT_HEAD_DIM, (j + 1) * ATT_HEAD_DIM)
            dx, dg = _rms_bwd(t[:, sl], gg, dk[:, sl])
            outs.append(dx)
            dgs = dg if dgs is None else dgs + dg
        return jnp.concatenate(outs + [dv], axis=1), dgs

    dkv_raw, dknorm = rowmap(fn, [dk_cur, (dk_prev, nxt), dv_cur, (dv_prev, nxt), kv_raw], [p["k_norm"]],
                             [(2 * KV_W, BF16)], [(1, ATT_HEAD_DIM)], tm=tm, name=nm + "_dknorm",
                             hook=io.hook(nm + "_dknorm"))
    g = {"k_norm": dknorm}
    io.put("w_kv", mm(u, dkv_raw, ta=True, out_dtype=BF16, name=nm + "_dwkv"))
    dh, dh_bf, g["kv_norm"] = du_norm_bwd([(dkv_raw, p["w_kv"], True)], h, p["kv_norm"], dout, nm + "_du",
                                          hook=io.hook(nm + "_du"))
    return dh, dh_bf, g


def _attn_scores(q_ref, kp_ref, kc_ref, vp_ref, vc_ref, qn_ref, bias_ref, sink_ref, kv):
    hd = ATT_HEAD_DIM
    blk = ATT_WINDOW
    sl = slice(kv * hd, (kv + 1) * hd)
    kk = jnp.concatenate([kp_ref[:, sl], kc_ref[:, sl]], axis=0)
    vv = jnp.concatenate([vp_ref[:, sl], vc_ref[:, sl]], axis=0)
    gq = qn_ref[...]
    raws, rinvs = [], []
    for r in range(ATT_GROUP):
        hh = kv * ATT_GROUP + r
        x = q_ref[:, hh * hd:(hh + 1) * hd]
        raws.append(x)
        rinvs.append(lax.rsqrt(jnp.mean(x * x, axis=-1, keepdims=True) + EPS))
    xh = jnp.concatenate([x * ri for x, ri in zip(raws, rinvs)], axis=0)
    rinv = jnp.concatenate(rinvs, axis=0)
    q8 = xh * gq
    s = _dot(q8, kk, NT) * (hd ** -0.5) + bias_ref[kv]
    colk = lax.broadcasted_iota(jnp.int32, (1, 2 * blk), 1)
    s = jnp.where((pl.program_id(0) > 0) | (colk >= blk), s, NEG)
    sink = sink_ref[kv]
    m = jnp.maximum(jnp.max(s, axis=-1, keepdims=True), sink)
    pexp = jnp.exp(s - m)
    e_sink = jnp.exp(sink - m)
    den = jnp.sum(pexp, axis=-1, keepdims=True) + e_sink
    prob = pexp / den
    return kk, vv, xh, rinv, q8, prob, e_sink / den


def _attn_specs(nb):
    blk = ATT_WINDOW
    cur = lambda i: (i, 0)
    prev = lambda i: (jnp.maximum(i - 1, 0), 0)
    return [
        pl.BlockSpec((blk, D_MODEL), cur),
        pl.BlockSpec((blk, KV_W), prev), pl.BlockSpec((blk, KV_W), cur),
        pl.BlockSpec((blk, KV_W), prev), pl.BlockSpec((blk, KV_W), cur),
        pl.BlockSpec((1, ATT_HEAD_DIM), lambda i: (0, 0)),
        pl.BlockSpec((ATT_KV_HEADS, ATT_GROUP * blk, 2 * blk), lambda i: (0, 0, 0)),
        pl.BlockSpec((ATT_KV_HEADS, ATT_GROUP * blk, 1), lambda i: (0, 0, 0)),
    ]


def attn_fwd(q_raw, k, v, q_norm, bias, sink_col, nm):
    t_dim = q_raw.shape[0]
    blk, hd = ATT_WINDOW, ATT_HEAD_DIM
    nb = t_dim // blk

    def body(q_ref, kp_ref, kc_ref, vp_ref, vc_ref, qn_ref, bias_ref, sink_ref, o_ref):
        for kv in range(ATT_KV_HEADS):
            kk, vv, xh, rinv, q8, prob, p_sink = _attn_scores(q_ref, kp_ref, kc_ref, vp_ref, vc_ref, qn_ref,
                                                              bias_ref, sink_ref, kv)
            o8 = _dot(prob, vv)
            for r in range(ATT_GROUP):
                hh = kv * ATT_GROUP + r
                o_ref[:, hh * hd:(hh + 1) * hd] = o8[r * blk:(r + 1) * blk].astype(o_ref.dtype)

    out, = pcall(body, name=nm, grid=(nb,), in_specs=_attn_specs(nb),
                 out_specs=[pl.BlockSpec((blk, D_MODEL), lambda i: (i, 0))],
                 out_shape=[jax.ShapeDtypeStruct((t_dim, D_MODEL), BF16)],
                 args=[q_raw, k, k, v, v, q_norm, bias, sink_col])
    return out


def attn_bwd(do, q_raw, k, v, q_norm, bias, sink_col, nm, hook=None):
    t_dim = q_raw.shape[0]
    blk, hd = ATT_WINDOW, ATT_HEAD_DIM
    nb = t_dim // blk
    scale = hd ** -0.5

    def body(do_ref, q_ref, kp_ref, kc_ref, vp_ref, vc_ref, qn_ref, bias_ref, sink_ref,
             dq_ref, dkc_ref, dkp_ref, dvc_ref, dvp_ref, dbias_ref, dsink_ref, dqn_ref):
        @pl.when(pl.program_id(0) == 0)
        def _():
            dbias_ref[...] = jnp.zeros_like(dbias_ref)
            dsink_ref[...] = jnp.zeros_like(dsink_ref)
            dqn_ref[...] = jnp.zeros_like(dqn_ref)

        gq = qn_ref[...]
        for kv in range(ATT_KV_HEADS):
            kk, vv, xh, rinv, q8, prob, p_sink = _attn_scores(q_ref, kp_ref, kc_ref, vp_ref, vc_ref, qn_ref,
                                                              bias_ref, sink_ref, kv)
            do8 = jnp.concatenate([do_ref[:, (kv * ATT_GROUP + r) * hd:(kv * ATT_GROUP + r + 1) * hd]
                                   for r in range(ATT_GROUP)], axis=0)
            dp = _dot(do8, vv, NT)
            delta = jnp.sum(prob * dp, axis=-1, keepdims=True)
            ds = prob * (dp - delta)
            dsink_ref[kv] += -p_sink * delta
            dbias_ref[kv] += ds
            ds_s = ds * scale
            dq8 = _dot(ds_s, kk)
            dkk = _dot(ds_s, q8, TN)
            dvv = _dot(prob, do8, TN)
            dqn_ref[...] += jnp.sum(dq8 * xh, axis=0, keepdims=True)
            dxh = dq8 * gq
            dq_raw8 = rinv * (dxh - xh * jnp.mean(dxh * xh, axis=-1, keepdims=True))
            for r in range(ATT_GROUP):
                hh = kv * ATT_GROUP + r
                dq_ref[:, hh * hd:(hh + 1) * hd] = dq_raw8[r * blk:(r + 1) * blk].astype(dq_ref.dtype)
            sl = slice(kv * hd, (kv + 1) * hd)
            dkp_ref[:, sl] = dkk[:blk]
            dkc_ref[:, sl] = dkk[blk:]
            dvp_ref[:, sl] = dvv[:blk]
            dvc_ref[:, sl] = dvv[blk:]

    cur = lambda i: (i, 0)
    row_spec = pl.BlockSpec((blk, KV_W), cur)
    out_specs = [
        pl.BlockSpec((blk, D_MODEL), cur), row_spec, row_spec, row_spec, row_spec,
        pl.BlockSpec((ATT_KV_HEADS, ATT_GROUP * blk, 2 * blk), lambda i: (0, 0, 0)),
        pl.BlockSpec((ATT_KV_HEADS, ATT_GROUP * blk, 1), lambda i: (0, 0, 0)),
        pl.BlockSpec((1, hd), lambda i: (0, 0)),
    ]
    kvs = jax.ShapeDtypeStruct((t_dim, KV_W), F32)
    out_shape = [
        jax.ShapeDtypeStruct((t_dim, D_MODEL), BF16), kvs, kvs, kvs, kvs,
        jax.ShapeDtypeStruct((ATT_KV_HEADS, ATT_GROUP * blk, 2 * blk), F32),
        jax.ShapeDtypeStruct((ATT_KV_HEADS, ATT_GROUP * blk, 1), F32),
        jax.ShapeDtypeStruct((1, hd), F32),
    ]
    return pcall(body, name=nm, grid=(nb,), in_specs=[pl.BlockSpec((blk, D_MODEL), cur)] + _attn_specs(nb),
                 out_specs=out_specs, out_shape=out_shape,
                 args=[do, q_raw, k, k, v, v, q_norm, bias, sink_col], hook=hook)


def _t5_bucket_np():
    blk = ATT_WINDOW
    qi = np.arange(blk)[:, None] + blk
    kj = np.arange(2 * blk)[None, :]
    dist = qi - kj
    n = np.maximum(dist, 0)
    max_exact = REL_BUCKETS // 2
    nf = np.maximum(n, 1).astype(np.float32)
    large = max_exact + (np.log(nf / max_exact) / math.log(ATT_WINDOW / max_exact)
                         * (REL_BUCKETS - max_exact)).astype(np.int32)
    large = np.minimum(large, REL_BUCKETS - 1)
    bucket = np.where(n < max_exact, n, large)
    in_window = (dist >= 0) & (dist < ATT_WINDOW)
    return bucket, in_window


def attn_block_fwd(h, k, v, p, nm):
    u, = rowmap(lambda x, gg: (_rms_fwd(x, gg),), [h], [p["attn_norm"]], [(D_MODEL, BF16)], tm=256, name=nm + "_norm")
    q_raw = mm(u, p["w_q"], name=nm + "_q")
    o = attn_fwd(q_raw, k, v, p["q_norm"], p["bias"], p["sink_col"], nm + "_core")
    out = mm(o, p["w_o"], res=h, name=nm + "_o")
    return out, (u, q_raw, o)


def attn_block_bwd(h, k, v, p, saved, dout, dout_bf, nm, io):
    u, q_raw, o = saved
    g = {}
    io.put("w_o", mm(o, dout_bf, ta=True, out_dtype=BF16, name=nm + "_dwo", hook=io.hook(nm + "_dwo")))
    do = mm(dout_bf, p["w_o"], tb=True, name=nm + "_do")
    dq_raw, dkc, dkp, dvc, dvp, g["bias"], g["sink_col"], g["q_norm"] = attn_bwd(
        do, q_raw, k, v, p["q_norm"], p["bias"], p["sink_col"], nm + "_dcore", hook=io.hook(nm + "_dcore"))
    io.put("w_q", mm(u, dq_raw, ta=True, out_dtype=BF16, name=nm + "_dwq"))
    dh, dh_bf, g["attn_norm"] = du_norm_bwd([(dq_raw, p["w_q"], True)], h, p["attn_norm"], dout, nm + "_du")
    return dh, dh_bf, g, (dkc, dkp, dvc, dvp)


FFN_TAGS = ["00", "01", "10", "11"]


def local_step(x, target, small, io):
    bucket, in_window = _t5_bucket_np()
    blk = ATT_WINDOW
    w = small

    fnorm = {tag: w["ffn_norm"][int(tag[0]), int(tag[1])][None, :] for tag in FFN_TAGS}
    a_neg = -jnp.exp(w["ssm_a_log"][0])

    def mamba_p():
        w_int = io.w("w_int")
        return dict(ssm_norm=w["ssm_norm"], w_zt=w_int[:SSM_D_INNER],
                    w_xbct=w_int[SSM_D_INNER:SSM_D_INNER + SSM_CONV_DIM], w_dtt=w_int[SSM_D_INNER + SSM_CONV_DIM:],
                    conv_w=w["ssm_conv_w"][0], conv_b=w["ssm_conv_b"], dt_bias=w["ssm_dt_bias"],
                    a_heads=a_neg[None, :], a_row=a_neg.reshape(SSM_GROUPS, 1, HPG),
                    a_col=a_neg.reshape(SSM_GROUPS, HPG, 1),
                    dvec=jnp.repeat(w["ssm_d"][0], SSM_HEAD_DIM)[None, :],
                    gate_norm=w["ssm_gate_norm"], w_out=io.w("w_out"))

    rb = w["rel_bias"]
    onehot3 = (np.arange(REL_BUCKETS)[:, None, None] == bucket[None]).astype(np.float32)
    bias = jnp.einsum("bh,bqk->hqk", rb, onehot3, precision=lax.Precision.HIGHEST)
    bias = jnp.where(in_window[None], bias, NEG)
    bias = bias.reshape(ATT_KV_HEADS, ATT_GROUP * blk, 2 * blk)
    sink_col = jnp.repeat(w["sinks"][0], blk).reshape(ATT_KV_HEADS, ATT_GROUP * blk, 1)

    def attn_p():
        return dict(attn_norm=w["attn_norm"], w_q=io.w("w_q"), q_norm=w["q_norm"], bias=bias, sink_col=sink_col,
                    w_o=io.w("w_o"))

    def kv_p():
        return dict(kv_norm=w["kv_norm"][None, :], w_kv=io.w("w_kv"), k_norm=w["k_norm"][None, :])

    h0 = x
    h0a, s_f00 = ffn_fwd(h0, fnorm["00"], "00", io)
    mp = mamba_p()
    h0b, s_m = mamba_fwd(h0a, mp, "ssm", io)
    h1, s_f01 = ffn_fwd(h0b, fnorm["01"], "01", io)
    kp = kv_p()
    k, v, s_kv = kv_fwd(h1, kp, "kv")
    h1a, s_f10 = ffn_fwd(h1, fnorm["10"], "10", io)
    ap = attn_p()
    h1b, s_a = attn_block_fwd(h1a, k, v, ap, "att")
    h2, s_f11 = ffn_fwd(h1b, fnorm["11"], "11", io)

    def loss_fn(y, t):
        e = y - t
        d = e * (1.0 / D_MODEL)
        return d, d, jnp.sum(e * e, axis=0, keepdims=True)

    dh, dh_bf, sq = rowmap(loss_fn, [h2, target], [], [(D_MODEL, F32), (D_MODEL, BF16)], [(1, D_MODEL)], tm=256,
                           name="loss")
    loss_part = jnp.sum(sq) * (0.5 / D_MODEL)

    fg = {}

    def ffn_back(tag, h_in, saved, dh, dh_bf):
        dh, dh_bf, dg = ffn_bwd(h_in, fnorm[tag], tag, saved, dh, dh_bf, io)
        fg[tag] = dg[0]
        return dh, dh_bf

    dh, dh_bf = ffn_back("11", h1b, s_f11, dh, dh_bf)
    dh, dh_bf, ga, dkv = attn_block_bwd(h1a, k, v, ap, s_a, dh, dh_bf, "att", io)
    dh, dh_bf = ffn_back("10", h1, s_f10, dh, dh_bf)
    dh, dh_bf, gk = kv_bwd(h1, kp, s_kv, *dkv, dh, "kv", io)
    dh, dh_bf = ffn_back("01", h0b, s_f01, dh, dh_bf)
    dh, dh_bf, gm = mamba_bwd(h0a, mp, s_m, dh, dh_bf, "ssm", io)
    dh, dh_bf = ffn_back("00", h0, s_f00, dh, dh_bf)
    grad_x = dh

    grads = {}
    grads["ffn_norm"] = jnp.stack([fg[tag] for tag in FFN_TAGS]).reshape(2, 2, D_MODEL)
    grads["ssm_norm"] = gm["ssm_norm"]
    grads["ssm_conv_w"] = gm["conv_w"][None]
    grads["ssm_conv_b"] = gm["conv_b"]
    grads["ssm_dt_bias"] = gm["dt_bias"]
    grads["ssm_a_log"] = gm["a"] * a_neg[None, :]
    grads["ssm_d"] = jnp.sum(gm["dvec"].reshape(SSM_HEADS, SSM_HEAD_DIM), axis=1)[None, :]
    grads["ssm_gate_norm"] = gm["gate_norm"]
    grads["kv_norm"] = gk["kv_norm"][0]
    grads["k_norm"] = gk["k_norm"][0]
    grads["attn_norm"] = ga["attn_norm"]
    grads["q_norm"] = ga["q_norm"]
    grads["sinks"] = jnp.sum(ga["sink_col"].reshape(ATT_HEADS, blk), axis=1)[None, :]
    onehot = (np.arange(REL_BUCKETS)[:, None] == bucket.reshape(1, -1)).astype(np.float32)
    dbias2d = ga["bias"].reshape(ATT_HEADS, blk * 2 * blk)
    grads["rel_bias"] = mm(jnp.asarray(onehot, BF16), dbias2d, tb=True, name="drelbias")
    return loss_part, grad_x, grads


def _adamw(g, w, m, v):
    m = ADAM_B1 * m + (1.0 - ADAM_B1) * g
    v = ADAM_B2 * v + (1.0 - ADAM_B2) * (g * g)
    m_hat = m / (1.0 - ADAM_B1 ** ADAM_STEP)
    v_hat = v / (1.0 - ADAM_B2 ** ADAM_STEP)
    delta = -ADAM_LR * (m_hat / (jnp.sqrt(v_hat) + ADAM_EPS) + ADAM_WD * w)
    return delta, m, v


def _slot_sum(r):
    g = r[0].astype(F32)
    for d in range(1, r.shape[0]):
        g = g + r[d].astype(F32)
    return g


def adamw_rows(recvs, w, m, v, name):
    n_l, rows, width = w.shape
    n_slots = recvs[0].shape[0]
    tr = 32
    assert rows % tr == 0, rows
    nt = rows // tr

    def body(*refs):
        r_refs = refs[:n_l]
        w_ref, m_ref, v_ref, g_o, d_o, m_o, v_o = refs[n_l:]
        li = pl.program_id(0)
        for k in range(n_l):
            @pl.when(li == k)
            def _(k=k):
                g = _slot_sum(r_refs[k])
                delta, m2, v2 = _adamw(g, w_ref[0], m_ref[0], v_ref[0])
                g_o[0] = g
                d_o[0] = delta
                m_o[0] = m2
                v_o[0] = v2

    def r_spec(k):
        return pl.BlockSpec((n_slots, tr, width),
                            lambda li, j: (0, jnp.where(li == k, j, jnp.where(li > k, nt - 1, 0)), 0))

    w_spec = pl.BlockSpec((1, tr, width), lambda li, j: (li, j, 0))
    shp = jax.ShapeDtypeStruct(w.shape, F32)
    return pcall(body, name=name, grid=(n_l, nt), in_specs=[r_spec(k) for k in range(n_l)] + [w_spec] * 3,
                 out_specs=[w_spec] * 4, out_shape=[shp] * 4, args=list(recvs) + [w, m, v])


def adamw_cols(recvs, w, m, v, name):
    n_l, rows, n = w.shape
    n_slots = recvs[0].shape[0]
    tr = 256
    nt = rows // tr

    def body(*refs):
        r_refs = refs[:n_l]
        w_ref, m_ref, v_ref, g_o, d_o, m_o, v_o = refs[n_l:]
        li = pl.program_id(0)
        for k in range(n_l):
            @pl.when(li == k)
            def _(k=k):
                g = _slot_sum(r_refs[k]).T
                delta, m2, v2 = _adamw(g, w_ref[0], m_ref[0], v_ref[0])
                g_o[0] = g
                d_o[0] = delta
                m_o[0] = m2
                v_o[0] = v2

    def r_spec(k):
        return pl.BlockSpec((n_slots, n, tr),
                            lambda li, j: (0, 0, jnp.where(li == k, j, jnp.where(li > k, nt - 1, 0))))

    w_spec = pl.BlockSpec((1, tr, n), lambda li, j: (li, j, 0))
    shp = jax.ShapeDtypeStruct(w.shape, F32)
    return pcall(body, name=name, grid=(n_l, nt), in_specs=[r_spec(k) for k in range(n_l)] + [w_spec] * 3,
                 out_specs=[w_spec] * 4, out_shape=[shp] * 4, args=list(recvs) + [w, m, v])


WEIGHT_NAMES = ["ffn_norm", "ffn_w1", "ffn_w3", "ffn_w2", "ssm_norm", "ssm_w_in", "ssm_conv_w", "ssm_conv_b",
                "ssm_dt_bias", "ssm_a_log", "ssm_d", "ssm_gate_norm", "ssm_w_out", "kv_norm", "w_kv", "k_norm",
                "attn_norm", "w_q", "q_norm", "sinks", "w_o", "rel_bias"]

SMALL = [
    ("ffn_norm", (2, 2, 1024), 2), ("ssm_norm", (1, 1024), 1), ("ssm_conv_w", (1, 4, 3072), 2),
    ("ssm_conv_b", (1, 3072), 1), ("ssm_gate_norm", (1, 2048), 1),
    ("ssm_dt_bias", (1, 32), None), ("ssm_a_log", (1, 32), None), ("ssm_d", (1, 32), None),
    ("kv_norm", (1024,), None), ("k_norm", (64,), None), ("attn_norm", (1, 1024), None),
    ("q_norm", (1, 64), None), ("sinks", (1, 16), None), ("rel_bias", (32, 16), None),
]
SMALL_W = 1024
SMALL_FULL_ROWS = 32
SMALL_LOCAL_ROWS = 48

MAT_GROUPS = {
    "f00": ["w1t_00", "w3t_00", "w2_00"], "f01": ["w1t_01", "w3t_01", "w2_01"],
    "f10": ["w1t_10", "w3t_10", "w2_10"], "f11": ["w1t_11", "w3t_11", "w2_11"],
    "ssm": ["w_int", "w_out"], "att": ["w_q", "w_o", "w_kv"],
}
FIRST_GATHER = "f00"
GATHER_PLAN = {"f00_upgate": ["ssm"], "ssm_xbc": ["f01"], "ssm_ssd": ["att", "f10"], "f01_upgate": ["f11"]}
SCATTER_A_PLAN = {"att_dwo": "f11", "kv_dknorm": "f10", "kv_du": "att", "f01_du": "f01", "ssm_du": "ssm",
                  "f00_du": "f00"}
SCATTER_B_PLAN = {"att_dcore": "f11", "f01_dw2": "att", "f01_dgate": "f10", "ssm_dssd": "f01", "f00_dgate": "ssm"}
LAST_SCATTER = "f00"
SLOT_MAJOR = ("w_int",)


def _shard_shape(s, a):
    return s[:a] + (s[a] // N_DEV,) + s[a + 1:]


def _unshard_view(stack, shard_shape, axis):
    moved = jnp.moveaxis(stack, 0, axis)
    return moved.reshape(shard_shape[:axis] + (N_DEV * shard_shape[axis],) + shard_shape[axis + 1:])


def _small_local(arrs):
    flat = jnp.concatenate([arrs[n].reshape(-1) for n, _, _ in SMALL])
    return jnp.pad(flat, (0, SMALL_LOCAL_ROWS * LANES - flat.shape[0])).reshape(SMALL_LOCAL_ROWS, LANES)


def chip_partial(g4, ra, name):
    _, _, n, width = g4.shape

    def body(g_ref, r_ref, o_ref):
        core = lax.axis_index("c")
        own = g_ref[0, pl.ds(core, 1)]
        o_ref[0] = (own[0].astype(F32) + r_ref[0, 0].astype(F32)).astype(o_ref.dtype)

    out, = pcall(body, name=name, grid=(N_CHIPS,),
                 in_specs=[pl.BlockSpec((1, 2, n, width), lambda q: (q, 0, 0, 0)),
                           pl.BlockSpec((1, 1, n, width), lambda q: (q, 0, 0, 0))],
                 out_specs=[pl.BlockSpec((1, n, width), lambda q: (q, 0, 0))],
                 out_shape=[jax.ShapeDtypeStruct((N_CHIPS, n, width), g4.dtype)], args=[g4, ra])
    return out


class StepIO:
    def __init__(self, pieces):
        self.pieces = pieces
        self.full = {}
        self.grad = {}
        self.from_sibling = {}
        self.recv = {}

    def w(self, name):
        return self.full[name]

    def put(self, name, g):
        self.grad[name] = g

    def _by_chip_core(self, name):
        g = self.grad[name]
        return g.reshape((N_CHIPS, 2, g.shape[0] // N_DEV) + g.shape[1:])

    def gather_items(self, groups):
        names = [n for grp in groups for n in MAT_GROUPS[grp]]
        items = [("g2", self.pieces[n], None if n in SLOT_MAJOR else 0) for n in names]

        def sink(outs):
            for n, o in zip(names, outs):
                self.full[n] = o.reshape((-1,) + o.shape[2:]) if n in SLOT_MAJOR else o

        return items, sink

    def scatter_a_items(self, group):
        names = MAT_GROUPS[group]
        items = [("sa", self._by_chip_core(n), None) for n in names]

        def sink(outs):
            for n, o in zip(names, outs):
                self.from_sibling[n] = o

        return items, sink

    def scatter_b_items(self, group):
        names = MAT_GROUPS[group]
        items = [("sb", chip_partial(self._by_chip_core(n), self.from_sibling[n], "partial_" + n), None)
                 for n in names]

        def sink(outs):
            for n, o in zip(names, outs):
                self.recv[n] = o

        return items, sink

    def hook(self, site):
        parts = []
        if site in GATHER_PLAN:
            parts.append(self.gather_items(GATHER_PLAN[site]))
        if site in SCATTER_A_PLAN:
            parts.append(self.scatter_a_items(SCATTER_A_PLAN[site]))
        if site in SCATTER_B_PLAN:
            parts.append(self.scatter_b_items(SCATTER_B_PLAN[site]))
        if not parts:
            return None
        return combine_hooks(parts)


def combine_hooks(parts):
    items = [it for its, _ in parts for it in its]

    def sink(outs):
        p = 0
        for its, snk in parts:
            snk(outs[p:p + len(its)])
            p += len(its)

    return Comm(items), sink


def step(x, target, wts, ms, vs):
    me = _my_index()

    pieces = {}
    for li in range(2):
        for hi in range(2):
            tag = "%d%d" % (li, hi)
            pieces["w1t_" + tag] = wts["ffn_w1"][li, hi].T.astype(BF16)
            pieces["w3t_" + tag] = wts["ffn_w3"][li, hi].T.astype(BF16)
            pieces["w2_" + tag] = wts["ffn_w2"][li, hi].astype(BF16)
    pieces["w_int"] = wts["ssm_w_in"][0].T.astype(BF16)
    pieces["w_out"] = wts["ssm_w_out"][0].astype(BF16)
    pieces["w_kv"] = wts["w_kv"].astype(BF16)
    pieces["w_q"] = wts["w_q"][0].astype(BF16)
    pieces["w_o"] = wts["w_o"][0].astype(BF16)
    io = StepIO(pieces)

    small_sharded = [(n, s, a) for n, s, a in SMALL if a is not None]
    loc = jnp.concatenate([wts[n].reshape(-1) for n, _, _ in small_sharded])
    loc_rows = -(-loc.shape[0] // (8 * LANES)) * 8
    loc = jnp.pad(loc, (0, loc_rows * LANES - loc.shape[0])).reshape(loc_rows, LANES)
    got_small = []
    comm, sink = combine_hooks([io.gather_items([FIRST_GATHER]), ([("g", loc, None)], got_small.extend)])
    sink(comm_only(comm, "gather_first"))
    gath_small = got_small[0].reshape(N_DEV, -1)
    small = {}
    off = 0
    for n, s, a in small_sharded:
        shard = _shard_shape(s, a)
        cnt = int(np.prod(shard))
        small[n] = _unshard_view(gath_small[:, off:off + cnt].reshape((N_DEV,) + shard), shard, a)
        off += cnt
    for n, s, a in SMALL:
        if a is None:
            small[n] = wts[n]

    loss_part, grad_x, g_small_local = local_step(x[0], target[0], small, io)
    loss = lax.psum(loss_part, ("x", "y", "c"))

    small_flat = jnp.concatenate([g_small_local[n].reshape(-1) for n, _, _ in SMALL])
    small_buf = jnp.pad(small_flat, (0, SMALL_FULL_ROWS * SMALL_W - small_flat.shape[0]))
    small_buf = small_buf.reshape(SMALL_FULL_ROWS, SMALL_W)
    got_small = []
    comm, sink = combine_hooks([io.scatter_b_items(LAST_SCATTER), ([("g", small_buf, None)], got_small.extend)])
    sink(comm_only(comm, "exchange_last"))
    small_all = got_small[0]

    def sum_body(r_ref, o_ref):
        o_ref[...] = _slot_sum(r_ref)

    vmem = pl.BlockSpec(memory_space=pltpu.VMEM)
    small_sum, = pcall(sum_body, name="sum_small", grid=(), in_specs=[vmem], out_specs=[vmem],
                       out_shape=[jax.ShapeDtypeStruct((SMALL_FULL_ROWS, SMALL_W), F32)], args=[small_all])
    small_sum = small_sum.reshape(-1)
    g_small = {}
    off = 0
    for n, s, a in SMALL:
        cnt = int(np.prod(s))
        gfull = small_sum[off:off + cnt].reshape(s)
        off += cnt
        if a is None:
            g_small[n] = gfull
        else:
            width = s[a] // N_DEV
            g_small[n] = lax.dynamic_slice_in_dim(gfull, me * width, width, axis=a)

    out = {}

    def emit(name, res, shape):
        for kind, arr in zip(("grad", "delta", "new_m", "new_v"), res):
            out[kind + "_" + name] = arr.reshape(shape)

    for name, key in (("ffn_w1", "w1t_"), ("ffn_w3", "w3t_")):
        shp = wts[name].shape
        view = lambda t: t.reshape((4,) + shp[2:])
        res = adamw_cols([io.recv[key + tag] for tag in FFN_TAGS], view(wts[name]), view(ms[name]), view(vs[name]),
                         "adamw_" + name)
        emit(name, res, shp)
    shp = wts["ffn_w2"].shape
    view = lambda t: t.reshape((4,) + shp[2:])
    res = adamw_rows([io.recv["w2_" + tag] for tag in FFN_TAGS], view(wts["ffn_w2"]), view(ms["ffn_w2"]),
                     view(vs["ffn_w2"]), "adamw_ffn_w2")
    emit("ffn_w2", res, shp)
    res = adamw_cols([io.recv["w_int"]], wts["ssm_w_in"], ms["ssm_w_in"], vs["ssm_w_in"], "adamw_ssm_w_in")
    emit("ssm_w_in", res, wts["ssm_w_in"].shape)
    for name, key in (("ssm_w_out", "w_out"), ("w_kv", "w_kv"), ("w_q", "w_q"), ("w_o", "w_o")):
        shp = wts[name].shape
        view = lambda t: t.reshape((1,) + shp[-2:])
        res = adamw_rows([io.recv[key]], view(wts[name]), view(ms[name]), view(vs[name]), "adamw_" + name)
        emit(name, res, shp)

    res_s = rowmap(lambda gg, ww, mm_, vv: _adamw(gg, ww, mm_, vv),
                   [_small_local(g_small), _small_local(wts), _small_local(ms), _small_local(vs)], [],
                   [(LANES, F32)] * 3, tm=SMALL_LOCAL_ROWS, name="adamw_small")
    flat_s = [r.reshape(-1) for r in res_s]
    off = 0
    for n, s, a in SMALL:
        shard = s if a is None else _shard_shape(s, a)
        cnt = int(np.prod(shard))
        out["grad_" + n] = g_small[n]
        for kind, arr in zip(("delta", "new_m", "new_v"), flat_s):
            out[kind + "_" + n] = arr[off:off + cnt].reshape(shard)
        off += cnt
    out["loss"] = loss
    out["grad_x"] = grad_x[None]
    return out


def kernel(x, ffn_norm, ffn_w1, ffn_w3, ffn_w2, ssm_norm, ssm_w_in, ssm_conv_w, ssm_conv_b, ssm_dt_bias, ssm_a_log, ssm_d, ssm_gate_norm, ssm_w_out, kv_norm, w_kv, k_norm, attn_norm, w_q, q_norm, sinks, w_o, rel_bias, loss_target, m_ffn_norm, m_ffn_w1, m_ffn_w3, m_ffn_w2, m_ssm_norm, m_ssm_w_in, m_ssm_conv_w, m_ssm_conv_b, m_ssm_dt_bias, m_ssm_a_log, m_ssm_d, m_ssm_gate_norm, m_ssm_w_out, m_kv_norm, m_w_kv, m_k_norm, m_attn_norm, m_w_q, m_q_norm, m_sinks, m_w_o, m_rel_bias, v_ffn_norm, v_ffn_w1, v_ffn_w3, v_ffn_w2, v_ssm_norm, v_ssm_w_in, v_ssm_conv_w, v_ssm_conv_b, v_ssm_dt_bias, v_ssm_a_log, v_ssm_d, v_ssm_gate_norm, v_ssm_w_out, v_kv_norm, v_w_kv, v_k_norm, v_attn_norm, v_w_q, v_q_norm, v_sinks, v_w_o, v_rel_bias):
    args = locals()
    wts = {n: args[n] for n in WEIGHT_NAMES}
    ms = {n: args["m_" + n] for n in WEIGHT_NAMES}
    vs = {n: args["v_" + n] for n in WEIGHT_NAMES}
    out = step(x, loss_target, wts, ms, vs)
    result = [out["loss"], out["grad_x"]]
    for kind in ("grad", "delta", "new_m", "new_v"):
        result += [out[kind + "_" + n] for n in WEIGHT_NAMES]
    return tuple(result)
```

```python
import functools
import math
import operator

import numpy as np
import jax
import jax.numpy as jnp
from jax import lax
from jax.experimental import pallas as pl
from jax.experimental.pallas import tpu as pltpu

F32 = jnp.float32
BF16 = jnp.bfloat16

D_MODEL = 1024
D_FF = 2816
N_DEV = 8
SSM_D_INNER = 2048
SSM_HEAD_DIM = 64
SSM_HEADS = 32
SSM_GROUPS = 4
SSM_STATE = 128
SSM_CONV = 4
SSM_CHUNK = 256
SSM_CONV_DIM = SSM_D_INNER + 2 * SSM_GROUPS * SSM_STATE
SSM_IN_DIM = SSM_D_INNER + SSM_CONV_DIM + SSM_HEADS
ATT_HEAD_DIM = 64
ATT_HEADS = 16
ATT_KV_HEADS = 2
ATT_GROUP = 8
ATT_WINDOW = 128
REL_BUCKETS = 32
EPS = 1e-6
NEG = -1e30

ADAM_LR = 0.001
ADAM_B1 = 0.9
ADAM_B2 = 0.999
ADAM_EPS = 1e-08
ADAM_WD = 0.01
ADAM_STEP = 10

VMEM_LIMIT_BYTES = 52 * 1024 * 1024
LANES = 128
MESH_ID = pl.DeviceIdType.MESH
ANY_SPEC = pl.BlockSpec(memory_space=pl.ANY)

NT = (((1,), (1,)), ((), ()))
TN = (((0,), (0,)), ((), ()))
NN = (((1,), (0,)), ((), ()))


def _pick(dim, cands):
    for c in cands:
        if dim % c == 0:
            return c
    return dim


def _my_index():
    return 4 * lax.axis_index("x") + 2 * lax.axis_index("y") + lax.axis_index("c")


def _peer(k):
    x, y, c = lax.axis_index("x"), lax.axis_index("y"), lax.axis_index("c")
    px = 1 - x if (k >> 2) & 1 else x
    py = 1 - y if (k >> 1) & 1 else y
    pc = 1 - c if k & 1 else c
    return (px, py, pc), 4 * px + 2 * py + pc


def _piece(ref, axis, d, n):
    if axis is None:
        return ref.at[d]
    return ref.at[(slice(None),) * axis + (pl.ds(pl.multiple_of(d * n, 8), n),)]


SIBLING = 1
CHIP_PEERS = (4, 2, 6)
N_CHIPS = 4
SEMS_PER_ITEM = N_DEV - 1


def _my_chip():
    return 2 * lax.axis_index("x") + lax.axis_index("y")


class Comm:
    def __init__(self, items):
        self.items = list(items)

    def dst_shapes(self):
        out = []
        for kind, src, axis in self.items:
            s = tuple(src.shape)
            if kind == "g":
                shp = (N_DEV,) + s
            elif kind == "g2":
                shp = (N_DEV,) + s if axis is None else s[:axis] + (N_DEV * s[axis],) + s[axis + 1:]
            elif kind == "sa":
                shp = (s[0], 1) + s[2:]
            else:
                shp = s
            out.append(jax.ShapeDtypeStruct(shp, src.dtype))
        return out

    def scratch(self):
        n = len(self.items)
        return [pltpu.SemaphoreType.DMA((n * SEMS_PER_ITEM,)), pltpu.SemaphoreType.DMA((n * SEMS_PER_ITEM,)),
                pltpu.SemaphoreType.DMA((n,))]

    def _run(self, srcs, dsts, sems, starting):
        send_sems, recv_sems, local_sems = sems
        me = _my_index()
        core = lax.axis_index("c")
        chip = _my_chip()
        for i, (kind, src, axis) in enumerate(self.items):
            s_ref, d_ref = srcs[i], dsts[i]
            base = i * SEMS_PER_ITEM

            def rdma(src_ref, dst_ref, j, peer):
                return pltpu.make_async_remote_copy(
                    src_ref=src_ref, dst_ref=dst_ref, send_sem=send_sems.at[base + j], recv_sem=recv_sems.at[base + j],
                    device_id=peer, device_id_type=MESH_ID)

            if kind == "g":
                local = pltpu.make_async_copy(s_ref, d_ref.at[me], local_sems.at[i])
                outs = [rdma(s_ref, d_ref.at[me], k - 1, _peer(k)[0]) for k in range(1, N_DEV)]
                if starting:
                    local.start()
                    for cp in outs:
                        cp.start()
                else:
                    for k in range(1, N_DEV):
                        rdma(s_ref, d_ref.at[_peer(k)[1]], k - 1, _peer(k)[0]).wait_recv()
                    for cp in outs:
                        cp.wait_send()
                    local.wait()
            elif kind == "g2":
                n = None if axis is None else src.shape[axis]
                mine = _piece(d_ref, axis, me, n)
                sib = _peer(SIBLING)[0]
                local = pltpu.make_async_copy(s_ref, mine, local_sems.at[i])
                outs = [rdma(s_ref, mine, 0, sib)] + [rdma(s_ref, mine, 1 + j, _peer(k)[0])
                                                      for j, k in enumerate(CHIP_PEERS)]
                if starting:
                    local.start()
                    for cp in outs:
                        cp.start()
                else:
                    passed = []
                    for j, k in enumerate(CHIP_PEERS):
                        theirs = _piece(d_ref, axis, _peer(k)[1], n)
                        rdma(s_ref, theirs, 1 + j, _peer(k)[0]).wait_recv()
                        fwd = rdma(theirs, theirs, 4 + j, sib)
                        fwd.start()
                        passed.append(fwd)
                    rdma(s_ref, _piece(d_ref, axis, _peer(SIBLING)[1], n), 0, sib).wait_recv()
                    for j, k in enumerate(CHIP_PEERS):
                        rdma(s_ref, _piece(d_ref, axis, _peer(k ^ SIBLING)[1], n), 4 + j, sib).wait_recv()
                    for cp in outs + passed:
                        cp.wait_send()
                    local.wait()
            elif kind == "sa":
                cp = rdma(s_ref.at[(slice(None), pl.ds(1 - core, 1))], d_ref, 0, _peer(SIBLING)[0])
                if starting:
                    cp.start()
                else:
                    cp.wait_recv()
                    cp.wait_send()
            else:
                local = pltpu.make_async_copy(s_ref.at[chip], d_ref.at[chip], local_sems.at[i])
                outs = [rdma(s_ref.at[_peer(k)[1] >> 1], d_ref.at[chip], 1 + j, _peer(k)[0])
                        for j, k in enumerate(CHIP_PEERS)]
                if starting:
                    local.start()
                    for cp in outs:
                        cp.start()
                else:
                    for j, k in enumerate(CHIP_PEERS):
                        rdma(s_ref.at[chip], d_ref.at[_peer(k)[1] >> 1], 1 + j, _peer(k)[0]).wait_recv()
                    for cp in outs:
                        cp.wait_send()
                    local.wait()

    def start(self, srcs, dsts, sems):
        self._run(srcs, dsts, sems, True)

    def wait(self, srcs, dsts, sems):
        self._run(srcs, dsts, sems, False)


def pcall(body, *, name, grid, in_specs, out_specs, out_shape, args, scratch=(), hook=None):
    cparams = pltpu.CompilerParams(dimension_semantics=("arbitrary",) * len(grid), vmem_limit_bytes=VMEM_LIMIT_BYTES)
    if hook is None:
        outs = pl.pallas_call(body, name=name, grid=grid, in_specs=list(in_specs), out_specs=list(out_specs),
                              out_shape=list(out_shape), scratch_shapes=list(scratch), compiler_params=cparams)(*args)
        return list(outs)
    comm, sink = hook
    n_in, n_out, n_scr, n_it = len(args), len(out_shape), len(scratch), len(comm.items)
    dims = tuple(grid)

    def wrapped(*refs):
        p = 0
        ins = refs[p:p + n_in]
        p += n_in
        csrc = refs[p:p + n_it]
        p += n_it
        outs = refs[p:p + n_out]
        p += n_out
        cdst = refs[p:p + n_it]
        p += n_it
        scr = refs[p:p + n_scr]
        p += n_scr
        sems = refs[p:p + 3]
        if dims:
            ids = [pl.program_id(a) for a in range(len(dims))]
            first = functools.reduce(operator.and_, [i == 0 for i in ids])
            last = functools.reduce(operator.and_, [i == d - 1 for i, d in zip(ids, dims)])

            @pl.when(first)
            def _():
                comm.start(csrc, cdst, sems)

            body(*ins, *outs, *scr)

            @pl.when(last)
            def _():
                comm.wait(csrc, cdst, sems)
        else:
            comm.start(csrc, cdst, sems)
            body(*ins, *outs, *scr)
            comm.wait(csrc, cdst, sems)

    res = pl.pallas_call(
        wrapped, name=name, grid=grid,
        in_specs=list(in_specs) + [ANY_SPEC] * n_it, out_specs=list(out_specs) + [ANY_SPEC] * n_it,
        out_shape=list(out_shape) + comm.dst_shapes(), scratch_shapes=list(scratch) + comm.scratch(),
        compiler_params=cparams,
    )(*args, *[src for _, src, _ in comm.items])
    res = list(res)
    sink(res[n_out:])
    return res[:n_out]


def comm_only(comm, name):
    got = []
    pcall(lambda *refs: None, name=name, grid=(), in_specs=[], out_specs=[], out_shape=[], args=[],
          hook=(comm, got.extend))
    return got


def mm(a, b, *, ta=False, tb=False, out_dtype=F32, res=None, alpha=1.0, name, hook=None):
    if ta:
        k_dim, m_dim = a.shape
    else:
        m_dim, k_dim = a.shape
    if tb:
        n_dim, k2 = b.shape
    else:
        k2, n_dim = b.shape
    assert k_dim == k2, (a.shape, b.shape, ta, tb)
    tn = _pick(n_dim, (1024, 1408, 512, 256, 128))
    tm = _pick(m_dim, (1024, 1408, 512, 256, 128)) if tn <= 1024 else _pick(m_dim, (512, 256, 128))
    tk = _pick(k_dim, (512, 1408, 256, 128))
    nk = k_dim // tk
    has_res = res is not None
    dn = (((0 if ta else 1,), (1 if tb else 0,)), ((), ()))

    def body(*refs):
        if has_res:
            a_ref, b_ref, r_ref, o_ref, acc_ref = refs
        else:
            a_ref, b_ref, o_ref, acc_ref = refs
        k = pl.program_id(2)

        @pl.when(k == 0)
        def _():
            acc_ref[...] = jnp.zeros_like(acc_ref)

        acc_ref[...] += lax.dot_general(a_ref[...].astype(BF16), b_ref[...].astype(BF16), dn,
                                        preferred_element_type=F32)

        @pl.when(k == nk - 1)
        def _():
            r = acc_ref[...]
            if alpha != 1.0:
                r = r * alpha
            if has_res:
                r = r_ref[...] + r
            o_ref[...] = r.astype(o_ref.dtype)

    a_spec = pl.BlockSpec((tk, tm), lambda i, j, k: (k, i)) if ta else pl.BlockSpec((tm, tk), lambda i, j, k: (i, k))
    b_spec = pl.BlockSpec((tn, tk), lambda i, j, k: (j, k)) if tb else pl.BlockSpec((tk, tn), lambda i, j, k: (k, j))
    o_spec = pl.BlockSpec((tm, tn), lambda i, j, k: (i, j))
    in_specs = [a_spec, b_spec] + ([o_spec] if has_res else [])
    args = [a, b] + ([res] if has_res else [])
    out, = pcall(body, name=name, grid=(m_dim // tm, n_dim // tn, nk), in_specs=in_specs, out_specs=[o_spec],
                 out_shape=[jax.ShapeDtypeStruct((m_dim, n_dim), out_dtype)], args=args,
                 scratch=[pltpu.VMEM((tm, tn), F32)], hook=hook)
    return out


def rowmap(fn, rows, consts=(), out_rows=(), out_accs=(), *, tm, name, hook=None):
    first = rows[0][0] if isinstance(rows[0], tuple) else rows[0]
    t_dim = first.shape[0]
    assert t_dim % tm == 0, (t_dim, tm)
    n_r, n_c, n_o = len(rows), len(consts), len(out_rows)

    def body(*refs):
        ins = [r[...] for r in refs[:n_r + n_c]]
        o_refs = refs[n_r + n_c:]
        outs = tuple(fn(*ins))
        for o_ref, val in zip(o_refs[:n_o], outs[:n_o]):
            o_ref[...] = val.astype(o_ref.dtype)
        if out_accs:
            @pl.when(pl.program_id(0) == 0)
            def _():
                for o_ref in o_refs[n_o:]:
                    o_ref[...] = jnp.zeros_like(o_ref)

            for o_ref, val in zip(o_refs[n_o:], outs[n_o:]):
                o_ref[...] += val

    in_specs, args = [], []
    for r in rows:
        if isinstance(r, tuple):
            args.append(r[0])
            in_specs.append(r[1])
        else:
            args.append(r)
            in_specs.append(pl.BlockSpec((tm, r.shape[1]), lambda i: (i, 0)))
    for c in consts:
        args.append(c)
        in_specs.append(pl.BlockSpec(c.shape, lambda i, nd=c.ndim: (0,) * nd))
    out_specs = [pl.BlockSpec((tm, w), lambda i: (i, 0)) for (w, _) in out_rows]
    out_specs += [pl.BlockSpec(s, lambda i, nd=len(s): (0,) * nd) for s in out_accs]
    out_shape = [jax.ShapeDtypeStruct((t_dim, w), dt) for (w, dt) in out_rows]
    out_shape += [jax.ShapeDtypeStruct(s, F32) for s in out_accs]
    return pcall(body, name=name, grid=(t_dim // tm,), in_specs=in_specs, out_specs=out_specs, out_shape=out_shape,
                 args=args, hook=hook)


def _rms_fwd(x, g):
    r = lax.rsqrt(jnp.mean(x * x, axis=-1, keepdims=True) + EPS)
    return x * r * g


def _rms_bwd(x, g, dy):
    r = lax.rsqrt(jnp.mean(x * x, axis=-1, keepdims=True) + EPS)
    xh = x * r
    dg = jnp.sum(dy * xh, axis=0, keepdims=True)
    dxh = dy * g
    dx = r * (dxh - xh * jnp.mean(dxh * xh, axis=-1, keepdims=True))
    return dx, dg


def _sigmoid(x):
    return 1.0 / (1.0 + jnp.exp(-x))


def _silu(x):
    return x * _sigmoid(x)


def _silu_grad(x):
    s = _sigmoid(x)
    return s * (1.0 + x * (1.0 - s))


def _split3(x):
    hi = x.astype(BF16)
    r1 = x - hi.astype(F32)
    mid = r1.astype(BF16)
    lo = (r1 - mid.astype(F32)).astype(BF16)
    return hi, mid, lo


def _dot(a, b, dn=NN):
    return lax.dot_general(a.astype(BF16), b.astype(BF16), dn, preferred_element_type=F32)


def _col_of(mat, h):
    lane = lax.broadcasted_iota(jnp.int32, mat.shape, 1)
    return jnp.sum(jnp.where(lane == h, mat, 0.0), axis=1, keepdims=True)


FFN_TN = 1408


def ffn_upgate(h, g, w1t, w3t, nm, hook=None):
    t_dim = h.shape[0]
    tm = _pick(t_dim, (512, 256, 128))
    tn = FFN_TN

    n_j = D_FF // tn
    u_w = D_MODEL // n_j

    def body(h_ref, g_ref, w1_ref, w3_ref, u_ref, a_ref, b_ref, hm_ref):
        uu = _rms_fwd(h_ref[...], g_ref[...]).astype(BF16)
        for j in range(n_j):
            @pl.when(pl.program_id(0) == j)
            def _(j=j):
                u_ref[...] = uu[:, j * u_w:(j + 1) * u_w]

        a = lax.dot_general(uu, w1_ref[...], NT, preferred_element_type=F32)
        b = lax.dot_general(uu, w3_ref[...], NT, preferred_element_type=F32)
        a_ref[...] = a.astype(a_ref.dtype)
        b_ref[...] = b.astype(b_ref.dtype)
        hm_ref[...] = (_silu(a) * b).astype(hm_ref.dtype)

    row_spec = pl.BlockSpec((tm, D_MODEL), lambda j, i: (i, 0))
    w_spec = pl.BlockSpec((tn, D_MODEL), lambda j, i: (j, 0))
    o_spec = pl.BlockSpec((tm, tn), lambda j, i: (i, j))
    o_shape = jax.ShapeDtypeStruct((t_dim, D_FF), BF16)
    return pcall(body, name=nm, grid=(D_FF // tn, t_dim // tm),
                 in_specs=[row_spec, pl.BlockSpec((1, D_MODEL), lambda j, i: (0, 0)), w_spec, w_spec],
                 out_specs=[pl.BlockSpec((tm, u_w), lambda j, i: (i, j))] + [o_spec] * 3,
                 out_shape=[jax.ShapeDtypeStruct((t_dim, D_MODEL), BF16)] + [o_shape] * 3,
                 args=[h, g, w1t, w3t], hook=hook)


def ffn_dgate(dout_bf, w2, a, b, nm, hook=None):
    t_dim = dout_bf.shape[0]
    tm = _pick(t_dim, (512, 256, 128))
    tn = FFN_TN

    def body(d_ref, w2_ref, a_ref, b_ref, da_ref, db_ref):
        dhm = 0.5 * lax.dot_general(d_ref[...], w2_ref[...], NT, preferred_element_type=F32)
        av = a_ref[...].astype(F32)
        bv = b_ref[...].astype(F32)
        sg = _sigmoid(av)
        da_ref[...] = (dhm * bv * (sg * (1.0 + av * (1.0 - sg)))).astype(da_ref.dtype)
        db_ref[...] = (dhm * (av * sg)).astype(db_ref.dtype)

    t_spec = pl.BlockSpec((tm, tn), lambda j, i: (i, j))
    o_shape = jax.ShapeDtypeStruct((t_dim, D_FF), BF16)
    return pcall(body, name=nm, grid=(D_FF // tn, t_dim // tm),
                 in_specs=[pl.BlockSpec((tm, D_MODEL), lambda j, i: (i, 0)),
                           pl.BlockSpec((tn, D_MODEL), lambda j, i: (j, 0)), t_spec, t_spec],
                 out_specs=[t_spec] * 2, out_shape=[o_shape] * 2, args=[dout_bf, w2, a, b], hook=hook)


def ffn_fwd(h, g, tag, io):
    nm = "f" + tag
    u, a, b, hm = ffn_upgate(h, g, io.w("w1t_" + tag), io.w("w3t_" + tag), nm + "_upgate",
                             hook=io.hook(nm + "_upgate"))
    out = mm(hm, io.w("w2_" + tag), res=h, alpha=0.5, name=nm + "_down")
    return out, (u, a, b, hm)


def du_norm_bwd(pairs, h, g, dout, nm, hook=None):
    t_dim = h.shape[0]
    tm = 256
    n_p = len(pairs)

    def body(*refs):
        h_ref, d_ref, g_ref = refs[2 * n_p:2 * n_p + 3]
        dh_ref, dhb_ref, dg_ref = refs[2 * n_p + 3:]
        du = None
        for p, (_, _, tb) in enumerate(pairs):
            t = lax.dot_general(refs[2 * p][...].astype(BF16), refs[2 * p + 1][...].astype(BF16), NT if tb else NN,
                                preferred_element_type=F32)
            du = t if du is None else du + t
        dx, dg = _rms_bwd(h_ref[...], g_ref[...], du)
        dh = d_ref[...] + dx
        dh_ref[...] = dh
        dhb_ref[...] = dh.astype(dhb_ref.dtype)

        @pl.when(pl.program_id(0) == 0)
        def _():
            dg_ref[...] = jnp.zeros_like(dg_ref)

        dg_ref[...] += dg

    in_specs, args = [], []
    for a, b, _ in pairs:
        in_specs += [pl.BlockSpec((tm, a.shape[1]), lambda i: (i, 0)), pl.BlockSpec(b.shape, lambda i: (0, 0))]
        args += [a, b]
    row_spec = pl.BlockSpec((tm, D_MODEL), lambda i: (i, 0))
    vec_spec = pl.BlockSpec((1, D_MODEL), lambda i: (0, 0))
    return pcall(body, name=nm, grid=(t_dim // tm,), in_specs=in_specs + [row_spec, row_spec, vec_spec],
                 out_specs=[row_spec, row_spec, vec_spec],
                 out_shape=[jax.ShapeDtypeStruct((t_dim, D_MODEL), F32), jax.ShapeDtypeStruct((t_dim, D_MODEL), BF16),
                            jax.ShapeDtypeStruct((1, D_MODEL), F32)],
                 args=args + [h, dout, g], hook=hook)


def ffn_bwd(h, g, tag, saved, dout, dout_bf, io):
    nm = "f" + tag
    w1t, w3t, w2 = io.w("w1t_" + tag), io.w("w3t_" + tag), io.w("w2_" + tag)
    u, a, b, hm = saved
    io.put("w2_" + tag, mm(hm, dout_bf, ta=True, alpha=0.5, out_dtype=BF16, name=nm + "_dw2",
                           hook=io.hook(nm + "_dw2")))
    da, db = ffn_dgate(dout_bf, w2, a, b, nm + "_dgate", hook=io.hook(nm + "_dgate"))
    io.put("w1t_" + tag, mm(da, u, ta=True, out_dtype=BF16, name=nm + "_dw1"))
    io.put("w3t_" + tag, mm(db, u, ta=True, out_dtype=BF16, name=nm + "_dw3", hook=io.hook(nm + "_dw3")))
    return du_norm_bwd([(da, w1t, False), (db, w3t, False)], h, g, dout, nm + "_du", hook=io.hook(nm + "_du"))


def _conv_pre(x, halo, w, b, tm):
    halo = jnp.where(pl.program_id(0) > 0, halo, 0.0)
    xx = jnp.concatenate([halo, x], axis=0)
    shifted = [xx[5 + k:5 + k + tm] for k in range(SSM_CONV)]
    acc = b + shifted[0] * w[0:1]
    for k in range(1, SSM_CONV):
        acc = acc + shifted[k] * w[k:k + 1]
    return acc, shifted


def _prev_halo_spec(tm, width):
    return pl.BlockSpec((8, width), lambda i: (jnp.maximum(i * (tm // 8) - 1, 0), 0))


def conv_fwd(xbc_raw, w, b, nm):
    tm = 128

    def fn(x, halo, ww, bb):
        acc, _ = _conv_pre(x, halo, ww, bb, tm)
        return (_silu(acc),)

    out, = rowmap(fn, [xbc_raw, (xbc_raw, _prev_halo_spec(tm, SSM_CONV_DIM))], [w, b],
                  [(SSM_CONV_DIM, F32)], tm=tm, name=nm)
    return out


def conv_bwd(xbc_raw, w, b, dxs, db_in, dc_in, nm):
    tm = 128
    t_dim = xbc_raw.shape[0]

    def fn1(x, halo, d1, d2, d3, ww, bb):
        acc, shifted = _conv_pre(x, halo, ww, bb, tm)
        dacc = jnp.concatenate([d1, d2, d3], axis=1) * _silu_grad(acc)
        dw = jnp.concatenate([jnp.sum(dacc * s, axis=0, keepdims=True) for s in shifted], axis=0)
        return dacc, dw, jnp.sum(dacc, axis=0, keepdims=True)

    dacc, dw, dbias = rowmap(fn1, [xbc_raw, (xbc_raw, _prev_halo_spec(tm, SSM_CONV_DIM)), dxs, db_in, dc_in],
                             [w, b], [(SSM_CONV_DIM, F32)], [(SSM_CONV, SSM_CONV_DIM), (1, SSM_CONV_DIM)],
                             tm=tm, name=nm + "_a")
    n_tiles = t_dim // tm

    def fn2(d, nxt, ww):
        nxt = jnp.where(pl.program_id(0) < n_tiles - 1, nxt, 0.0)
        dd = jnp.concatenate([d, nxt], axis=0)
        out = dd[3:3 + tm] * ww[0:1]
        for k in range(1, SSM_CONV):
            out = out + dd[3 - k:3 - k + tm] * ww[k:k + 1]
        return (out,)

    nxt_spec = pl.BlockSpec((8, SSM_CONV_DIM), lambda i: (jnp.minimum((i + 1) * (tm // 8), t_dim // 8 - 1), 0))
    dx, = rowmap(fn2, [dacc, (dacc, nxt_spec)], [w], [(SSM_CONV_DIM, BF16)], tm=tm, name=nm + "_b")
    return dx, dw, dbias


GRP_W = SSM_D_INNER // SSM_GROUPS
HPG = SSM_HEADS // SSM_GROUPS
HEAD_SHIFT = 6


def _split2(x):
    hi = x.astype(BF16)
    return hi, (x - hi.astype(F32)).astype(BF16)


def _expand_mats():
    e = ((lax.broadcasted_iota(jnp.int32, (HPG, GRP_W), 1) >> HEAD_SHIFT)
         == lax.broadcasted_iota(jnp.int32, (HPG, GRP_W), 0)).astype(BF16)
    et = ((lax.broadcasted_iota(jnp.int32, (GRP_W, HPG), 0) >> HEAD_SHIFT)
          == lax.broadcasted_iota(jnp.int32, (GRP_W, HPG), 1)).astype(BF16)
    return e, et


def _expand(v, e_m):
    hi, lo = _split2(v)
    return jnp.dot(hi, e_m, preferred_element_type=F32) + jnp.dot(lo, e_m, preferred_element_type=F32)


def _reduce8(v, et_m):
    acc = None
    for p in _split3(v):
        t = jnp.dot(p, et_m, preferred_element_type=F32)
        acc = t if acc is None else acc + t
    return acc


def _ssd_group_terms(dt_ref, dtT_ref, arow_ref, acol_ref):
    L = SSM_CHUNK
    r = lax.broadcasted_iota(jnp.int32, (L, L), 0)
    c = lax.broadcasted_iota(jnp.int32, (L, L), 1)
    tril = (r >= c).astype(BF16)
    triu = (r <= c).astype(BF16)
    dtg = dt_ref[0]
    acol = None
    for p in _split3(dtg * arow_ref[0]):
        t = jnp.dot(tril, p, preferred_element_type=F32)
        acol = t if acol is None else acol + t
    arowT = None
    for p in _split3(dtT_ref[0] * acol_ref[0]):
        t = jnp.dot(p, triu, preferred_element_type=F32)
        arowT = t if arowT is None else arowT + t
    return dtg, acol, arowT, r >= c


def _state_decay(a_last_col, et_m):
    hi, lo = _split2(jnp.broadcast_to(jnp.exp(a_last_col), (HPG, SSM_STATE)))
    return jnp.dot(et_m, hi, preferred_element_type=F32) + jnp.dot(et_m, lo, preferred_element_type=F32)


def _ssd_specs(nc, rev):
    L, N = SSM_CHUNK, SSM_STATE
    xcols = SSM_D_INNER // LANES
    ch = (lambda c: nc - 1 - c) if rev else (lambda c: c)
    return [
        pl.BlockSpec((L, GRP_W), lambda c, g: (ch(c), g)),
        pl.BlockSpec((L, N), lambda c, g: (ch(c), xcols + g)),
        pl.BlockSpec((L, N), lambda c, g: (ch(c), xcols + SSM_GROUPS + g)),
        pl.BlockSpec((1, L, HPG), lambda c, g: (g, ch(c), 0)),
        pl.BlockSpec((1, HPG, L), lambda c, g: (g, 0, ch(c))),
        pl.BlockSpec((1, 1, HPG), lambda c, g: (g, 0, 0)),
        pl.BlockSpec((1, HPG, 1), lambda c, g: (g, 0, 0)),
        pl.BlockSpec((1, GRP_W), lambda c, g: (0, g)),
    ]


def ssd_fwd(xbc, dt_g, dtT_g, a_row, a_col, dvec, nm, hook=None):
    t_dim = xbc.shape[0]
    L, P, N = SSM_CHUNK, SSM_HEAD_DIM, SSM_STATE
    nc = t_dim // L

    def body(x_ref, b_ref, c_ref, dt_ref, dtT_ref, arow_ref, acol_ref, dvec_ref, y_ref, st_ref, s_s):
        ci = pl.program_id(0)
        g = pl.program_id(1)

        @pl.when((ci == 0) & (g == 0))
        def _():
            s_s[...] = jnp.zeros_like(s_s)

        e_m, et_m = _expand_mats()
        dtg, acol, arowT, causal = _ssd_group_terms(dt_ref, dtT_ref, arow_ref, acol_ref)
        a_last_row = acol[L - 1:L, :]
        x = x_ref[...]
        bm = b_ref[...]
        cm = c_ref[...]
        cb = _dot(cm, bm, NT)
        s = s_s[g]
        st_ref[0, 0] = s
        ea_x = _expand(jnp.exp(acol), e_m)
        dt_x = _expand(dtg, e_m)
        w_x = _expand(jnp.exp(a_last_row - acol) * dtg, e_m)
        yb = ea_x * _dot(cm, s, NT) + dvec_ref[...] * x
        xd = (x * dt_x).astype(BF16)
        for e in range(HPG):
            sl = slice(e * P, (e + 1) * P)
            lm = jnp.exp(jnp.where(causal, acol[:, e:e + 1] - arowT[e:e + 1, :], NEG))
            m = (cb * lm).astype(BF16)
            y_ref[:, sl] = yb[:, sl] + jnp.dot(m, xd[:, sl], preferred_element_type=F32)
        s_s[g] = _state_decay(arowT[:, L - 1:L], et_m) * s + _dot(x * w_x, bm, TN)

    out_specs = [
        pl.BlockSpec((L, GRP_W), lambda c, g: (c, g)),
        pl.BlockSpec((1, 1, GRP_W, N), lambda c, g: (c, g, 0, 0)),
    ]
    return pcall(
        body, name=nm, grid=(nc, SSM_GROUPS), in_specs=_ssd_specs(nc, False), out_specs=out_specs,
        out_shape=[jax.ShapeDtypeStruct((t_dim, SSM_D_INNER), F32),
                   jax.ShapeDtypeStruct((nc, SSM_GROUPS, GRP_W, N), F32)],
        scratch=[pltpu.VMEM((SSM_GROUPS, GRP_W, N), F32)],
        args=[xbc, xbc, xbc, dt_g, dtT_g, a_row, a_col, dvec], hook=hook)


def ssd_bwd(dy, xbc, dt_g, dtT_g, a_row, a_col, dvec, states, nm, hook=None):
    t_dim = xbc.shape[0]
    L, P, N = SSM_CHUNK, SSM_HEAD_DIM, SSM_STATE
    nc = t_dim // L

    def body(dy_ref, x_ref, b_ref, c_ref, dt_ref, dtT_ref, arow_ref, acol_ref, dvec_ref, st_ref,
             dx_ref, db_ref, dc_ref, da_ref, ddt_ref, dd_ref, ds_s, yd_s, dxd_s):
        ci = pl.program_id(0)
        g = pl.program_id(1)

        @pl.when((ci == 0) & (g == 0))
        def _():
            ds_s[...] = jnp.zeros_like(ds_s)
            dd_ref[...] = jnp.zeros_like(dd_ref)

        e_m, et_m = _expand_mats()
        dtg, acol, arowT, causal = _ssd_group_terms(dt_ref, dtT_ref, arow_ref, acol_ref)
        a_last_row = acol[L - 1:L, :]
        x = x_ref[...]
        dy = dy_ref[...]
        bm = b_ref[...]
        cm = c_ref[...]
        cb = _dot(cm, bm, NT)
        s = st_ref[0, 0]
        dsp = ds_s[g]
        ew8 = jnp.exp(a_last_row - acol)
        ea_x = _expand(jnp.exp(acol), e_m)
        dt_x = _expand(dtg, e_m)
        ew_x = _expand(ew8, e_m)
        w_x = ew_x * dt_x
        z = _dot(cm, s, NT)
        dz = ea_x * dy
        dc = _dot(dz, s)
        ds_y = _dot(dz, cm, TN)
        du = _dot(bm, dsp, NT)
        u = x * w_x
        db = _dot(u, dsp)
        xd = (x * dt_x).astype(BF16)
        dyb = dy.astype(BF16)
        dcb = jnp.zeros((L, L), F32)
        for e in range(HPG):
            sl = slice(e * P, (e + 1) * P)
            lm = jnp.exp(jnp.where(causal, acol[:, e:e + 1] - arowT[e:e + 1, :], NEG))
            m = (cb * lm).astype(BF16)
            yd_s[:, sl] = jnp.dot(m, xd[:, sl], preferred_element_type=F32)
            dxd_s[:, sl] = lax.dot_general(m, dyb[:, sl], TN, preferred_element_type=F32)
            dcb = dcb + lax.dot_general(dyb[:, sl], xd[:, sl], NT, preferred_element_type=F32) * lm
        dxd = dxd_s[...]
        dx_ref[...] = dvec_ref[...] * dy + du * w_x + dt_x * dxd
        ddt = _reduce8(x * (ew_x * du + dxd), et_m)
        da = (_reduce8(dz * z + dyb.astype(F32) * yd_s[...], et_m)
              - _reduce8(xd.astype(F32) * dxd + du * u, et_m))
        dwa_row = _reduce8(jnp.broadcast_to(jnp.sum(du * u, axis=0, keepdims=True), (8, GRP_W)), et_m)[0:1]
        t_nh = None
        for p in _split3(dsp * s):
            t = lax.dot_general(p, et_m, TN, preferred_element_type=F32)
            t_nh = t if t_nh is None else t_nh + t
        d_last = dwa_row + jnp.exp(a_last_row) * jnp.sum(t_nh, axis=0, keepdims=True)
        row_l = lax.broadcasted_iota(jnp.int32, (L, 1), 0)
        da_ref[0] = da + jnp.where(row_l == L - 1, d_last, 0.0)
        ddt_ref[0] = ddt
        dd_ref[g] += jnp.sum(dy * x, axis=0, keepdims=True)
        dc_ref[...] = dc + _dot(dcb, bm)
        db_ref[...] = db + _dot(dcb, cm, TN)
        ds_s[g] = _state_decay(arowT[:, L - 1:L], et_m) * dsp + ds_y

    rc = lambda c: nc - 1 - c
    in_specs = ([pl.BlockSpec((L, GRP_W), lambda c, g: (rc(c), g))] + _ssd_specs(nc, True)
                + [pl.BlockSpec((1, 1, GRP_W, N), lambda c, g: (rc(c), g, 0, 0))])
    out_specs = [
        pl.BlockSpec((L, GRP_W), lambda c, g: (rc(c), g)),
        pl.BlockSpec((L, N), lambda c, g: (rc(c), g)),
        pl.BlockSpec((L, N), lambda c, g: (rc(c), g)),
        pl.BlockSpec((1, L, HPG), lambda c, g: (g, rc(c), 0)),
        pl.BlockSpec((1, L, HPG), lambda c, g: (g, rc(c), 0)),
        pl.BlockSpec((SSM_GROUPS, 1, GRP_W), lambda c, g: (0, 0, 0)),
    ]
    gn = SSM_GROUPS * N
    out_shape = [
        jax.ShapeDtypeStruct((t_dim, SSM_D_INNER), F32), jax.ShapeDtypeStruct((t_dim, gn), F32),
        jax.ShapeDtypeStruct((t_dim, gn), F32), jax.ShapeDtypeStruct((SSM_GROUPS, t_dim, HPG), F32),
        jax.ShapeDtypeStruct((SSM_GROUPS, t_dim, HPG), F32), jax.ShapeDtypeStruct((SSM_GROUPS, 1, GRP_W), F32),
    ]
    return pcall(
        body, name=nm, grid=(nc, SSM_GROUPS), in_specs=in_specs, out_specs=out_specs, out_shape=out_shape,
        scratch=[pltpu.VMEM((SSM_GROUPS, GRP_W, N), F32), pltpu.VMEM((L, GRP_W), F32), pltpu.VMEM((L, GRP_W), F32)],
        args=[dy, xbc, xbc, xbc, dt_g, dtT_g, a_row, a_col, dvec, states], hook=hook)


def _softplus(x):
    return jnp.maximum(x, 0.0) + jnp.log(1.0 + jnp.exp(-jnp.abs(x)))


def ssd_dt_bwd(da, ddt, dt, dt_raw, a_row, dt_bias, nm):
    L = SSM_CHUNK

    def fn(d_a, d_dt, dtv, raw, ar, bias):
        r = lax.broadcasted_iota(jnp.int32, (L, L), 0)
        c = lax.broadcasted_iota(jnp.int32, (L, L), 1)
        triu = (r <= c).astype(BF16)
        acc = None
        for p in _split3(d_a):
            t = jnp.dot(triu, p, preferred_element_type=F32)
            acc = t if acc is None else acc + t
        d_dt = d_dt + acc * ar
        d_a_h = jnp.sum(acc * dtv, axis=0, keepdims=True)
        d_raw = d_dt * _sigmoid(raw + bias)
        return d_raw, d_a_h, jnp.sum(d_raw, axis=0, keepdims=True)

    return rowmap(fn, [da, ddt, dt, dt_raw], [a_row, dt_bias], [(SSM_HEADS, BF16)],
                  [(1, SSM_HEADS), (1, SSM_HEADS)], tm=L, name=nm)


GN_W = SSM_D_INNER // SSM_GROUPS


def mamba_fwd(h, p, nm, io):
    u, = rowmap(lambda x, gg: (_rms_fwd(x, gg),), [h], [p["ssm_norm"]], [(D_MODEL, BF16)], tm=256, name=nm + "_norm")
    z = mm(u, p["w_zt"], tb=True, name=nm + "_z")
    xbc_raw = mm(u, p["w_xbct"], tb=True, name=nm + "_xbc", hook=io.hook(nm + "_xbc"))
    dt_raw = mm(u, p["w_dtt"], tb=True, name=nm + "_dt")
    xbc = conv_fwd(xbc_raw, p["conv_w"], p["conv_b"], nm + "_conv")
    dt, = rowmap(lambda r, b: (_softplus(r + b),), [dt_raw], [p["dt_bias"]], [(SSM_HEADS, F32)], tm=256,
                 name=nm + "_softplus")
    dt_g = dt.reshape(-1, SSM_GROUPS, HPG).transpose(1, 0, 2)
    dtT_g = dt_g.transpose(0, 2, 1)
    y, states = ssd_fwd(xbc, dt_g, dtT_g, p["a_row"], p["a_col"], p["dvec"], nm + "_ssd", hook=io.hook(nm + "_ssd"))

    def gate_norm(yv, zv, gg):
        t = yv * _silu(zv)
        return (jnp.concatenate([_rms_fwd(t[:, k * GN_W:(k + 1) * GN_W], gg[:, k * GN_W:(k + 1) * GN_W])
                                 for k in range(SSM_GROUPS)], axis=1),)

    yn, = rowmap(gate_norm, [y, z], [p["gate_norm"]], [(SSM_D_INNER, BF16)], tm=256, name=nm + "_gatenorm")
    out = mm(yn, p["w_out"], res=h, name=nm + "_out")
    return out, (u, z, xbc_raw, dt_raw, xbc, dt, dt_g, dtT_g, y, states, yn)


def mamba_bwd(h, p, saved, dout, dout_bf, nm, io):
    u, z, xbc_raw, dt_raw, xbc, dt, dt_g, dtT_g, y, states, yn = saved
    g = {}
    io.put("w_out", mm(yn, dout_bf, ta=True, out_dtype=BF16, name=nm + "_dwout"))
    dyn = mm(dout_bf, p["w_out"], tb=True, name=nm + "_dyn")

    def gate_norm_bwd(d, yv, zv, gg):
        sz = _silu(zv)
        t = yv * sz
        dts, dgs = [], []
        for k in range(SSM_GROUPS):
            sl = slice(k * GN_W, (k + 1) * GN_W)
            dt_k, dg_k = _rms_bwd(t[:, sl], gg[:, sl], d[:, sl])
            dts.append(dt_k)
            dgs.append(dg_k)
        d_t = jnp.concatenate(dts, axis=1)
        return d_t * sz, d_t * yv * _silu_grad(zv), jnp.concatenate(dgs, axis=1)

    dy, dz, g["gate_norm"] = rowmap(gate_norm_bwd, [dyn, y, z], [p["gate_norm"]],
                                    [(SSM_D_INNER, F32), (SSM_D_INNER, BF16)], [(1, SSM_D_INNER)], tm=256,
                                    name=nm + "_dgatenorm")
    dxs, db_in, dc_in, da_g, ddt_g, dd = ssd_bwd(
        dy, xbc, dt_g, dtT_g, p["a_row"], p["a_col"], p["dvec"], states, nm + "_dssd", hook=io.hook(nm + "_dssd"))
    g["dvec"] = dd
    per_head = lambda t: t.transpose(1, 0, 2).reshape(-1, SSM_HEADS)
    ddt_raw, g["a"], g["dt_bias"] = ssd_dt_bwd(per_head(da_g), per_head(ddt_g), dt, dt_raw, p["a_heads"],
                                               p["dt_bias"], nm + "_ddt")
    dxbc_raw, g["conv_w"], g["conv_b"] = conv_bwd(xbc_raw, p["conv_w"], p["conv_b"], dxs, db_in, dc_in, nm + "_dconv")
    io.put("w_int", jnp.concatenate([mm(dz, u, ta=True, out_dtype=BF16, name=nm + "_dwz"),
                                     mm(dxbc_raw, u, ta=True, out_dtype=BF16, name=nm + "_dwxbc"),
                                     mm(ddt_raw, u, ta=True, out_dtype=BF16, name=nm + "_dwdt")], axis=0))
    dh, dh_bf, g["ssm_norm"] = du_norm_bwd(
        [(dz, p["w_zt"], False), (dxbc_raw, p["w_xbct"], False), (ddt_raw, p["w_dtt"], False)],
        h, p["ssm_norm"], dout, nm + "_du", hook=io.hook(nm + "_du"))
    return dh, dh_bf, g


KV_W = ATT_KV_HEADS * ATT_HEAD_DIM


def kv_fwd(h, p, nm):
    u, = rowmap(lambda x, gg: (_rms_fwd(x, gg),), [h], [p["kv_norm"]], [(D_MODEL, BF16)], tm=256, name=nm + "_norm")
    kv_raw = mm(u, p["w_kv"], name=nm + "_proj")

    def knorm(t, gg):
        ks = [_rms_fwd(t[:, j * ATT_HEAD_DIM:(j + 1) * ATT_HEAD_DIM], gg) for j in range(ATT_KV_HEADS)]
        return jnp.concatenate(ks, axis=1), t[:, KV_W:]

    k, v = rowmap(knorm, [kv_raw], [p["k_norm"]], [(KV_W, F32), (KV_W, F32)], tm=256, name=nm + "_knorm")
    return k, v, (u, kv_raw)


def kv_bwd(h, p, saved, dk_cur, dk_prev, dv_cur, dv_prev, dout, nm, io):
    u, kv_raw = saved
    t_dim = h.shape[0]
    tm = ATT_WINDOW
    nb = t_dim // tm
    nxt = pl.BlockSpec((tm, KV_W), lambda i: (jnp.minimum(i + 1, nb - 1), 0))

    def fn(dkc, dkp, dvc, dvp, t, gg):
        live = pl.program_id(0) < nb - 1
        dk = dkc + jnp.where(live, dkp, 0.0)
        dv = dvc + jnp.where(live, dvp, 0.0)
        outs, dgs = [], None
        for j in range(ATT_KV_HEADS):
            sl = slice(j * ATT_HEAD_DIM, (j + 1) * ATT_HEAD_DIM)
            dx, dg = _rms_bwd(t[:, sl], gg, dk[:, sl])
            outs.append(dx)
            dgs = dg if dgs is None else dgs + dg
        return jnp.concatenate(outs + [dv], axis=1), dgs

    dkv_raw, dknorm = rowmap(fn, [dk_cur, (dk_prev, nxt), dv_cur, (dv_prev, nxt), kv_raw], [p["k_norm"]],
                             [(2 * KV_W, BF16)], [(1, ATT_HEAD_DIM)], tm=tm, name=nm + "_dknorm",
                             hook=io.hook(nm + "_dknorm"))
    g = {"k_norm": dknorm}
    io.put("w_kv", mm(u, dkv_raw, ta=True, out_dtype=BF16, name=nm + "_dwkv"))
    dh, dh_bf, g["kv_norm"] = du_norm_bwd([(dkv_raw, p["w_kv"], True)], h, p["kv_norm"], dout, nm + "_du",
                                          hook=io.hook(nm + "_du"))
    return dh, dh_bf, g


def _head_mats():
    n, h = ATT_HEADS * ATT_HEAD_DIM, ATT_HEADS
    r_m = ((lax.broadcasted_iota(jnp.int32, (n, h), 0) >> HEAD_SHIFT)
           == lax.broadcasted_iota(jnp.int32, (n, h), 1)).astype(BF16)
    rt_m = ((lax.broadcasted_iota(jnp.int32, (h, n), 1) >> HEAD_SHIFT)
            == lax.broadcasted_iota(jnp.int32, (h, n), 0)).astype(BF16)
    return r_m, rt_m


def _dot2(v, m):
    hi, lo = _split2(v)
    return jnp.dot(hi, m, preferred_element_type=F32) + jnp.dot(lo, m, preferred_element_type=F32)


def _q_normalised(q_ref, r_m, rt_m):
    q = q_ref[...]
    rinv = lax.rsqrt(_dot2(q * q, r_m) * (1.0 / ATT_HEAD_DIM) + EPS)
    rinv_x = _dot2(rinv, rt_m)
    return q * rinv_x, rinv_x


def _attn_scores(xh_all, kp_ref, kc_ref, vp_ref, vc_ref, qn_ref, bias_ref, sink_ref, kv):
    hd = ATT_HEAD_DIM
    blk = ATT_WINDOW
    sl = slice(kv * hd, (kv + 1) * hd)
    kk = jnp.concatenate([kp_ref[:, sl], kc_ref[:, sl]], axis=0)
    vv = jnp.concatenate([vp_ref[:, sl], vc_ref[:, sl]], axis=0)
    xh = jnp.concatenate([xh_all[:, (kv * ATT_GROUP + r) * hd:(kv * ATT_GROUP + r + 1) * hd]
                          for r in range(ATT_GROUP)], axis=0)
    q8 = xh * qn_ref[...]
    s = _dot(q8, kk, NT) * (hd ** -0.5) + bias_ref[kv]
    colk = lax.broadcasted_iota(jnp.int32, (1, 2 * blk), 1)
    s = jnp.where((pl.program_id(0) > 0) | (colk >= blk), s, NEG)
    sink = sink_ref[kv]
    m = jnp.maximum(jnp.max(s, axis=-1, keepdims=True), sink)
    pexp = jnp.exp(s - m)
    e_sink = jnp.exp(sink - m)
    inv_den = 1.0 / (jnp.sum(pexp, axis=-1, keepdims=True) + e_sink)
    return kk, vv, q8, pexp * inv_den, e_sink * inv_den


def _attn_specs(nb):
    blk = ATT_WINDOW
    cur = lambda i: (i, 0)
    prev = lambda i: (jnp.maximum(i - 1, 0), 0)
    return [
        pl.BlockSpec((blk, D_MODEL), cur),
        pl.BlockSpec((blk, KV_W), prev), pl.BlockSpec((blk, KV_W), cur),
        pl.BlockSpec((blk, KV_W), prev), pl.BlockSpec((blk, KV_W), cur),
        pl.BlockSpec((1, ATT_HEAD_DIM), lambda i: (0, 0)),
        pl.BlockSpec((ATT_KV_HEADS, ATT_GROUP * blk, 2 * blk), lambda i: (0, 0, 0)),
        pl.BlockSpec((ATT_KV_HEADS, ATT_GROUP * blk, 1), lambda i: (0, 0, 0)),
    ]


def attn_fwd(q_raw, k, v, q_norm, bias, sink_col, nm):
    t_dim = q_raw.shape[0]
    blk, hd = ATT_WINDOW, ATT_HEAD_DIM
    nb = t_dim // blk

    def body(q_ref, kp_ref, kc_ref, vp_ref, vc_ref, qn_ref, bias_ref, sink_ref, o_ref):
        xh_all, _ = _q_normalised(q_ref, *_head_mats())
        for kv in range(ATT_KV_HEADS):
            kk, vv, q8, prob, p_sink = _attn_scores(xh_all, kp_ref, kc_ref, vp_ref, vc_ref, qn_ref,
                                                    bias_ref, sink_ref, kv)
            o8 = _dot(prob, vv)
            for r in range(ATT_GROUP):
                hh = kv * ATT_GROUP + r
                o_ref[:, hh * hd:(hh + 1) * hd] = o8[r * blk:(r + 1) * blk].astype(o_ref.dtype)

    out, = pcall(body, name=nm, grid=(nb,), in_specs=_attn_specs(nb),
                 out_specs=[pl.BlockSpec((blk, D_MODEL), lambda i: (i, 0))],
                 out_shape=[jax.ShapeDtypeStruct((t_dim, D_MODEL), BF16)],
                 args=[q_raw, k, k, v, v, q_norm, bias, sink_col])
    return out


def attn_bwd(do, q_raw, k, v, q_norm, bias, sink_col, nm, hook=None):
    t_dim = q_raw.shape[0]
    blk, hd = ATT_WINDOW, ATT_HEAD_DIM
    nb = t_dim // blk
    scale = hd ** -0.5

    def body(do_ref, q_ref, kp_ref, kc_ref, vp_ref, vc_ref, qn_ref, qnx_ref, bias_ref, sink_ref,
             dq_ref, dkc_ref, dkp_ref, dvc_ref, dvp_ref, dbias_ref, dsink_ref, dqn_ref, dqh_s):
        @pl.when(pl.program_id(0) == 0)
        def _():
            dbias_ref[...] = jnp.zeros_like(dbias_ref)
            dsink_ref[...] = jnp.zeros_like(dsink_ref)
            dqn_ref[...] = jnp.zeros_like(dqn_ref)

        r_m, rt_m = _head_mats()
        xh_all, rinv_x = _q_normalised(q_ref, r_m, rt_m)
        for kv in range(ATT_KV_HEADS):
            kk, vv, q8, prob, p_sink = _attn_scores(xh_all, kp_ref, kc_ref, vp_ref, vc_ref, qn_ref,
                                                    bias_ref, sink_ref, kv)
            do8 = jnp.concatenate([do_ref[:, (kv * ATT_GROUP + r) * hd:(kv * ATT_GROUP + r + 1) * hd]
                                   for r in range(ATT_GROUP)], axis=0)
            dp = _dot(do8, vv, NT)
            delta = jnp.sum(prob * dp, axis=-1, keepdims=True)
            ds = prob * (dp - delta)
            dsink_ref[kv] += -p_sink * delta
            dbias_ref[kv] += ds
            ds_s = ds * scale
            dq8 = _dot(ds_s, kk)
            dkk = _dot(ds_s, q8, TN)
            dvv = _dot(prob, do8, TN)
            for r in range(ATT_GROUP):
                hh = kv * ATT_GROUP + r
                dqh_s[:, hh * hd:(hh + 1) * hd] = dq8[r * blk:(r + 1) * blk]
            sl = slice(kv * hd, (kv + 1) * hd)
            dkp_ref[:, sl] = dkk[:blk]
            dkc_ref[:, sl] = dkk[blk:]
            dvp_ref[:, sl] = dvv[:blk]
            dvc_ref[:, sl] = dvv[blk:]
        dqh = dqh_s[...]
        dqn_ref[...] += jnp.sum(dqh * xh_all, axis=0, keepdims=True)
        dxh = dqh * qnx_ref[...]
        t_x = _dot2(_dot2(dxh * xh_all, r_m) * (1.0 / hd), rt_m)
        dq_ref[...] = (rinv_x * (dxh - xh_all * t_x)).astype(dq_ref.dtype)

    cur = lambda i: (i, 0)
    row_spec = pl.BlockSpec((blk, KV_W), cur)
    out_specs = [
        pl.BlockSpec((blk, D_MODEL), cur), row_spec, row_spec, row_spec, row_spec,
        pl.BlockSpec((ATT_KV_HEADS, ATT_GROUP * blk, 2 * blk), lambda i: (0, 0, 0)),
        pl.BlockSpec((ATT_KV_HEADS, ATT_GROUP * blk, 1), lambda i: (0, 0, 0)),
        pl.BlockSpec((1, D_MODEL), lambda i: (0, 0)),
    ]
    kvs = jax.ShapeDtypeStruct((t_dim, KV_W), F32)
    out_shape = [
        jax.ShapeDtypeStruct((t_dim, D_MODEL), BF16), kvs, kvs, kvs, kvs,
        jax.ShapeDtypeStruct((ATT_KV_HEADS, ATT_GROUP * blk, 2 * blk), F32),
        jax.ShapeDtypeStruct((ATT_KV_HEADS, ATT_GROUP * blk, 1), F32),
        jax.ShapeDtypeStruct((1, D_MODEL), F32),
    ]
    specs = _attn_specs(nb)
    in_specs = [pl.BlockSpec((blk, D_MODEL), cur)] + specs[:6] + [pl.BlockSpec((1, D_MODEL), lambda i: (0, 0))] + specs[6:]
    dq, dkc, dkp, dvc, dvp, dbias, dsink, dqn_x = pcall(
        body, name=nm, grid=(nb,), in_specs=in_specs, out_specs=out_specs, out_shape=out_shape,
        scratch=[pltpu.VMEM((blk, D_MODEL), F32)],
        args=[do, q_raw, k, k, v, v, q_norm, jnp.tile(q_norm, (1, ATT_HEADS)), bias, sink_col], hook=hook)
    return dq, dkc, dkp, dvc, dvp, dbias, dsink, jnp.sum(dqn_x.reshape(ATT_HEADS, hd), axis=0, keepdims=True)


def _t5_bucket_np():
    blk = ATT_WINDOW
    qi = np.arange(blk)[:, None] + blk
    kj = np.arange(2 * blk)[None, :]
    dist = qi - kj
    n = np.maximum(dist, 0)
    max_exact = REL_BUCKETS // 2
    nf = np.maximum(n, 1).astype(np.float32)
    large = max_exact + (np.log(nf / max_exact) / math.log(ATT_WINDOW / max_exact)
                         * (REL_BUCKETS - max_exact)).astype(np.int32)
    large = np.minimum(large, REL_BUCKETS - 1)
    bucket = np.where(n < max_exact, n, large)
    in_window = (dist >= 0) & (dist < ATT_WINDOW)
    return bucket, in_window


def attn_block_fwd(h, k, v, p, nm):
    u, = rowmap(lambda x, gg: (_rms_fwd(x, gg),), [h], [p["attn_norm"]], [(D_MODEL, BF16)], tm=256, name=nm + "_norm")
    q_raw = mm(u, p["w_q"], name=nm + "_q")
    o = attn_fwd(q_raw, k, v, p["q_norm"], p["bias"], p["sink_col"], nm + "_core")
    out = mm(o, p["w_o"], res=h, name=nm + "_o")
    return out, (u, q_raw, o)


def attn_block_bwd(h, k, v, p, saved, dout, dout_bf, nm, io):
    u, q_raw, o = saved
    g = {}
    io.put("w_o", mm(o, dout_bf, ta=True, out_dtype=BF16, name=nm + "_dwo", hook=io.hook(nm + "_dwo")))
    do = mm(dout_bf, p["w_o"], tb=True, name=nm + "_do")
    dq_raw, dkc, dkp, dvc, dvp, g["bias"], g["sink_col"], g["q_norm"] = attn_bwd(
        do, q_raw, k, v, p["q_norm"], p["bias"], p["sink_col"], nm + "_dcore", hook=io.hook(nm + "_dcore"))
    io.put("w_q", mm(u, dq_raw, ta=True, out_dtype=BF16, name=nm + "_dwq"))
    dh, dh_bf, g["attn_norm"] = du_norm_bwd([(dq_raw, p["w_q"], True)], h, p["attn_norm"], dout, nm + "_du")
    return dh, dh_bf, g, (dkc, dkp, dvc, dvp)


FFN_TAGS = ["00", "01", "10", "11"]


def local_step(x, target, small, io):
    bucket, in_window = _t5_bucket_np()
    blk = ATT_WINDOW
    w = small

    fnorm = {tag: w["ffn_norm"][int(tag[0]), int(tag[1])][None, :] for tag in FFN_TAGS}
    a_neg = -jnp.exp(w["ssm_a_log"][0])

    def mamba_p():
        w_int = io.w("w_int")
        return dict(ssm_norm=w["ssm_norm"], w_zt=w_int[:SSM_D_INNER],
                    w_xbct=w_int[SSM_D_INNER:SSM_D_INNER + SSM_CONV_DIM], w_dtt=w_int[SSM_D_INNER + SSM_CONV_DIM:],
                    conv_w=w["ssm_conv_w"][0], conv_b=w["ssm_conv_b"], dt_bias=w["ssm_dt_bias"],
                    a_heads=a_neg[None, :], a_row=a_neg.reshape(SSM_GROUPS, 1, HPG),
                    a_col=a_neg.reshape(SSM_GROUPS, HPG, 1),
                    dvec=jnp.repeat(w["ssm_d"][0], SSM_HEAD_DIM)[None, :],
                    gate_norm=w["ssm_gate_norm"], w_out=io.w("w_out"))

    rb = w["rel_bias"]
    onehot3 = (np.arange(REL_BUCKETS)[:, None, None] == bucket[None]).astype(np.float32)
    bias = jnp.einsum("bh,bqk->hqk", rb, onehot3, precision=lax.Precision.HIGHEST)
    bias = jnp.where(in_window[None], bias, NEG)
    bias = bias.reshape(ATT_KV_HEADS, ATT_GROUP * blk, 2 * blk)
    sink_col = jnp.repeat(w["sinks"][0], blk).reshape(ATT_KV_HEADS, ATT_GROUP * blk, 1)

    def attn_p():
        return dict(attn_norm=w["attn_norm"], w_q=io.w("w_q"), q_norm=w["q_norm"], bias=bias, sink_col=sink_col,
                    w_o=io.w("w_o"))

    def kv_p():
        return dict(kv_norm=w["kv_norm"][None, :], w_kv=io.w("w_kv"), k_norm=w["k_norm"][None, :])

    h0 = x
    h0a, s_f00 = ffn_fwd(h0, fnorm["00"], "00", io)
    mp = mamba_p()
    h0b, s_m = mamba_fwd(h0a, mp, "ssm", io)
    h1, s_f01 = ffn_fwd(h0b, fnorm["01"], "01", io)
    kp = kv_p()
    k, v, s_kv = kv_fwd(h1, kp, "kv")
    h1a, s_f10 = ffn_fwd(h1, fnorm["10"], "10", io)
    ap = attn_p()
    h1b, s_a = attn_block_fwd(h1a, k, v, ap, "att")
    h2, s_f11 = ffn_fwd(h1b, fnorm["11"], "11", io)

    def loss_fn(y, t):
        e = y - t
        d = e * (1.0 / D_MODEL)
        return d, d, jnp.sum(e * e, axis=0, keepdims=True)

    dh, dh_bf, sq = rowmap(loss_fn, [h2, target], [], [(D_MODEL, F32), (D_MODEL, BF16)], [(1, D_MODEL)], tm=256,
                           name="loss")
    loss_part = jnp.sum(sq) * (0.5 / D_MODEL)

    fg = {}

    def ffn_back(tag, h_in, saved, dh, dh_bf):
        dh, dh_bf, dg = ffn_bwd(h_in, fnorm[tag], tag, saved, dh, dh_bf, io)
        fg[tag] = dg[0]
        return dh, dh_bf

    dh, dh_bf = ffn_back("11", h1b, s_f11, dh, dh_bf)
    dh, dh_bf, ga, dkv = attn_block_bwd(h1a, k, v, ap, s_a, dh, dh_bf, "att", io)
    dh, dh_bf = ffn_back("10", h1, s_f10, dh, dh_bf)
    dh, dh_bf, gk = kv_bwd(h1, kp, s_kv, *dkv, dh, "kv", io)
    dh, dh_bf = ffn_back("01", h0b, s_f01, dh, dh_bf)
    dh, dh_bf, gm = mamba_bwd(h0a, mp, s_m, dh, dh_bf, "ssm", io)
    dh, dh_bf = ffn_back("00", h0, s_f00, dh, dh_bf)
    grad_x = dh

    grads = {}
    grads["ffn_norm"] = jnp.stack([fg[tag] for tag in FFN_TAGS]).reshape(2, 2, D_MODEL)
    grads["ssm_norm"] = gm["ssm_norm"]
    grads["ssm_conv_w"] = gm["conv_w"][None]
    grads["ssm_conv_b"] = gm["conv_b"]
    grads["ssm_dt_bias"] = gm["dt_bias"]
    grads["ssm_a_log"] = gm["a"] * a_neg[None, :]
    grads["ssm_d"] = jnp.sum(gm["dvec"].reshape(SSM_HEADS, SSM_HEAD_DIM), axis=1)[None, :]
    grads["ssm_gate_norm"] = gm["gate_norm"]
    grads["kv_norm"] = gk["kv_norm"][0]
    grads["k_norm"] = gk["k_norm"][0]
    grads["attn_norm"] = ga["attn_norm"]
    grads["q_norm"] = ga["q_norm"]
    grads["sinks"] = jnp.sum(ga["sink_col"].reshape(ATT_HEADS, blk), axis=1)[None, :]
    onehot = (np.arange(REL_BUCKETS)[:, None] == bucket.reshape(1, -1)).astype(np.float32)
    dbias2d = ga["bias"].reshape(ATT_HEADS, blk * 2 * blk)
    grads["rel_bias"] = mm(jnp.asarray(onehot, BF16), dbias2d, tb=True, name="drelbias")
    return loss_part, grad_x, grads


def _adamw(g, w, m, v):
    m = ADAM_B1 * m + (1.0 - ADAM_B1) * g
    v = ADAM_B2 * v + (1.0 - ADAM_B2) * (g * g)
    m_hat = m / (1.0 - ADAM_B1 ** ADAM_STEP)
    v_hat = v / (1.0 - ADAM_B2 ** ADAM_STEP)
    delta = -ADAM_LR * (m_hat / (jnp.sqrt(v_hat) + ADAM_EPS) + ADAM_WD * w)
    return delta, m, v


def _slot_sum(r):
    g = r[0].astype(F32)
    for d in range(1, r.shape[0]):
        g = g + r[d].astype(F32)
    return g


def adamw_rows(recvs, w, m, v, name):
    n_l, rows, width = w.shape
    n_slots = recvs[0].shape[0]
    tr = 32
    assert rows % tr == 0, rows
    nt = rows // tr

    def body(*refs):
        r_refs = refs[:n_l]
        w_ref, m_ref, v_ref, g_o, d_o, m_o, v_o = refs[n_l:]
        li = pl.program_id(0)
        for k in range(n_l):
            @pl.when(li == k)
            def _(k=k):
                g = _slot_sum(r_refs[k])
                delta, m2, v2 = _adamw(g, w_ref[0], m_ref[0], v_ref[0])
                g_o[0] = g
                d_o[0] = delta
                m_o[0] = m2
                v_o[0] = v2

    def r_spec(k):
        return pl.BlockSpec((n_slots, tr, width),
                            lambda li, j: (0, jnp.where(li == k, j, jnp.where(li > k, nt - 1, 0)), 0))

    w_spec = pl.BlockSpec((1, tr, width), lambda li, j: (li, j, 0))
    shp = jax.ShapeDtypeStruct(w.shape, F32)
    return pcall(body, name=name, grid=(n_l, nt), in_specs=[r_spec(k) for k in range(n_l)] + [w_spec] * 3,
                 out_specs=[w_spec] * 4, out_shape=[shp] * 4, args=list(recvs) + [w, m, v])


def adamw_cols(recvs, w, m, v, name):
    n_l, rows, n = w.shape
    n_slots = recvs[0].shape[0]
    tr = 256
    nt = rows // tr

    def body(*refs):
        r_refs = refs[:n_l]
        w_ref, m_ref, v_ref, g_o, d_o, m_o, v_o = refs[n_l:]
        li = pl.program_id(0)
        for k in range(n_l):
            @pl.when(li == k)
            def _(k=k):
                g = _slot_sum(r_refs[k]).T
                delta, m2, v2 = _adamw(g, w_ref[0], m_ref[0], v_ref[0])
                g_o[0] = g
                d_o[0] = delta
                m_o[0] = m2
                v_o[0] = v2

    def r_spec(k):
        return pl.BlockSpec((n_slots, n, tr),
                            lambda li, j: (0, 0, jnp.where(li == k, j, jnp.where(li > k, nt - 1, 0))))

    w_spec = pl.BlockSpec((1, tr, n), lambda li, j: (li, j, 0))
    shp = jax.ShapeDtypeStruct(w.shape, F32)
    return pcall(body, name=name, grid=(n_l, nt), in_specs=[r_spec(k) for k in range(n_l)] + [w_spec] * 3,
                 out_specs=[w_spec] * 4, out_shape=[shp] * 4, args=list(recvs) + [w, m, v])


WEIGHT_NAMES = ["ffn_norm", "ffn_w1", "ffn_w3", "ffn_w2", "ssm_norm", "ssm_w_in", "ssm_conv_w", "ssm_conv_b",
                "ssm_dt_bias", "ssm_a_log", "ssm_d", "ssm_gate_norm", "ssm_w_out", "kv_norm", "w_kv", "k_norm",
                "attn_norm", "w_q", "q_norm", "sinks", "w_o", "rel_bias"]

SMALL = [
    ("ffn_norm", (2, 2, 1024), 2), ("ssm_norm", (1, 1024), 1), ("ssm_conv_w", (1, 4, 3072), 2),
    ("ssm_conv_b", (1, 3072), 1), ("ssm_gate_norm", (1, 2048), 1),
    ("ssm_dt_bias", (1, 32), None), ("ssm_a_log", (1, 32), None), ("ssm_d", (1, 32), None),
    ("kv_norm", (1024,), None), ("k_norm", (64,), None), ("attn_norm", (1, 1024), None),
    ("q_norm", (1, 64), None), ("sinks", (1, 16), None), ("rel_bias", (32, 16), None),
]
SMALL_W = 1024
SMALL_FULL_ROWS = 32
SMALL_LOCAL_ROWS = 48

MAT_GROUPS = {
    "f00_up": ["w1t_00", "w3t_00"], "f00_down": ["w2_00"], "f01": ["w1t_01", "w3t_01", "w2_01"],
    "f10": ["w1t_10", "w3t_10", "w2_10"], "f11": ["w1t_11", "w3t_11", "w2_11"],
    "ssm": ["w_int", "w_out"], "att": ["w_q", "w_o", "w_kv"],
    "f00_early": ["w2_00", "w1t_00"], "f00_late": ["w3t_00"],
}
FIRST_GATHER = "f00_up"
GATHER_PLAN = {"f00_upgate": ["f00_down", "ssm"], "ssm_xbc": ["f01"], "ssm_ssd": ["att", "f10"],
               "f01_upgate": ["f11"]}
SCATTER_A_PLAN = {"att_dwo": "f11", "kv_dknorm": "f10", "kv_du": "att", "f01_du": "f01", "ssm_du": "ssm",
                  "f00_dw3": "f00_early", "f00_du": "f00_late"}
SCATTER_B_PLAN = {"att_dcore": "f11", "f01_dw2": "att", "f01_dgate": "f10", "ssm_dssd": "f01", "f00_dgate": "ssm",
                  "f00_du": "f00_early"}
LAST_SCATTER = "f00_late"
SLOT_MAJOR = ("w_int",)


def _shard_shape(s, a):
    return s[:a] + (s[a] // N_DEV,) + s[a + 1:]


def _unshard_view(stack, shard_shape, axis):
    moved = jnp.moveaxis(stack, 0, axis)
    return moved.reshape(shard_shape[:axis] + (N_DEV * shard_shape[axis],) + shard_shape[axis + 1:])


def _small_local(arrs):
    flat = jnp.concatenate([arrs[n].reshape(-1) for n, _, _ in SMALL])
    return jnp.pad(flat, (0, SMALL_LOCAL_ROWS * LANES - flat.shape[0])).reshape(SMALL_LOCAL_ROWS, LANES)


def chip_partial(g4, ra, name):
    _, _, n, width = g4.shape

    def body(g_ref, r_ref, o_ref):
        core = lax.axis_index("c")
        own = g_ref[0, pl.ds(core, 1)]
        o_ref[0] = (own[0].astype(F32) + r_ref[0, 0].astype(F32)).astype(o_ref.dtype)

    out, = pcall(body, name=name, grid=(N_CHIPS,),
                 in_specs=[pl.BlockSpec((1, 2, n, width), lambda q: (q, 0, 0, 0)),
                           pl.BlockSpec((1, 1, n, width), lambda q: (q, 0, 0, 0))],
                 out_specs=[pl.BlockSpec((1, n, width), lambda q: (q, 0, 0))],
                 out_shape=[jax.ShapeDtypeStruct((N_CHIPS, n, width), g4.dtype)], args=[g4, ra])
    return out


class StepIO:
    def __init__(self, pieces):
        self.pieces = pieces
        self.full = {}
        self.grad = {}
        self.from_sibling = {}
        self.recv = {}

    def w(self, name):
        return self.full[name]

    def put(self, name, g):
        self.grad[name] = g

    def _by_chip_core(self, name):
        g = self.grad[name]
        return g.reshape((N_CHIPS, 2, g.shape[0] // N_DEV) + g.shape[1:])

    def gather_items(self, groups):
        names = [n for grp in groups for n in MAT_GROUPS[grp]]
        items = [("g2", self.pieces[n], None if n in SLOT_MAJOR else 0) for n in names]

        def sink(outs):
            for n, o in zip(names, outs):
                self.full[n] = o.reshape((-1,) + o.shape[2:]) if n in SLOT_MAJOR else o

        return items, sink

    def scatter_a_items(self, group):
        names = MAT_GROUPS[group]
        items = [("sa", self._by_chip_core(n), None) for n in names]

        def sink(outs):
            for n, o in zip(names, outs):
                self.from_sibling[n] = o

        return items, sink

    def scatter_b_items(self, group):
        names = MAT_GROUPS[group]
        items = [("sb", chip_partial(self._by_chip_core(n), self.from_sibling[n], "partial_" + n), None)
                 for n in names]

        def sink(outs):
            for n, o in zip(names, outs):
                self.recv[n] = o

        return items, sink

    def hook(self, site):
        parts = []
        if site in GATHER_PLAN:
            parts.append(self.gather_items(GATHER_PLAN[site]))
        if site in SCATTER_A_PLAN:
            parts.append(self.scatter_a_items(SCATTER_A_PLAN[site]))
        if site in SCATTER_B_PLAN:
            parts.append(self.scatter_b_items(SCATTER_B_PLAN[site]))
        if not parts:
            return None
        return combine_hooks(parts)


def combine_hooks(parts):
    items = [it for its, _ in parts for it in its]

    def sink(outs):
        p = 0
        for its, snk in parts:
            snk(outs[p:p + len(its)])
            p += len(its)

    return Comm(items), sink


def step(x, target, wts, ms, vs):
    me = _my_index()

    pieces = {}
    for li in range(2):
        for hi in range(2):
            tag = "%d%d" % (li, hi)
            pieces["w1t_" + tag] = wts["ffn_w1"][li, hi].T.astype(BF16)
            pieces["w3t_" + tag] = wts["ffn_w3"][li, hi].T.astype(BF16)
            pieces["w2_" + tag] = wts["ffn_w2"][li, hi].astype(BF16)
    pieces["w_int"] = wts["ssm_w_in"][0].T.astype(BF16)
    pieces["w_out"] = wts["ssm_w_out"][0].astype(BF16)
    pieces["w_kv"] = wts["w_kv"].astype(BF16)
    pieces["w_q"] = wts["w_q"][0].astype(BF16)
    pieces["w_o"] = wts["w_o"][0].astype(BF16)
    io = StepIO(pieces)

    small_sharded = [(n, s, a) for n, s, a in SMALL if a is not None]
    loc = jnp.concatenate([wts[n].reshape(-1) for n, _, _ in small_sharded])
    loc_rows = -(-loc.shape[0] // (8 * LANES)) * 8
    loc = jnp.pad(loc, (0, loc_rows * LANES - loc.shape[0])).reshape(loc_rows, LANES)
    got_small = []
    comm, sink = combine_hooks([io.gather_items([FIRST_GATHER]), ([("g", loc, None)], got_small.extend)])
    sink(comm_only(comm, "gather_first"))
    gath_small = got_small[0].reshape(N_DEV, -1)
    small = {}
    off = 0
    for n, s, a in small_sharded:
        shard = _shard_shape(s, a)
        cnt = int(np.prod(shard))
        small[n] = _unshard_view(gath_small[:, off:off + cnt].reshape((N_DEV,) + shard), shard, a)
        off += cnt
    for n, s, a in SMALL:
        if a is None:
            small[n] = wts[n]

    loss_part, grad_x, g_small_local = local_step(x[0], target[0], small, io)
    loss = lax.psum(loss_part, ("x", "y", "c"))

    small_flat = jnp.concatenate([g_small_local[n].reshape(-1) for n, _, _ in SMALL])
    small_buf = jnp.pad(small_flat, (0, SMALL_FULL_ROWS * SMALL_W - small_flat.shape[0]))
    small_buf = small_buf.reshape(SMALL_FULL_ROWS, SMALL_W)
    got_small = []
    comm, sink = combine_hooks([io.scatter_b_items(LAST_SCATTER), ([("g", small_buf, None)], got_small.extend)])
    sink(comm_only(comm, "exchange_last"))
    small_all = got_small[0]

    def sum_body(r_ref, o_ref):
        o_ref[...] = _slot_sum(r_ref)

    vmem = pl.BlockSpec(memory_space=pltpu.VMEM)
    small_sum, = pcall(sum_body, name="sum_small", grid=(), in_specs=[vmem], out_specs=[vmem],
                       out_shape=[jax.ShapeDtypeStruct((SMALL_FULL_ROWS, SMALL_W), F32)], args=[small_all])
    small_sum = small_sum.reshape(-1)
    g_small = {}
    off = 0
    for n, s, a in SMALL:
        cnt = int(np.prod(s))
        gfull = small_sum[off:off + cnt].reshape(s)
        off += cnt
        if a is None:
            g_small[n] = gfull
        else:
            width = s[a] // N_DEV
            g_small[n] = lax.dynamic_slice_in_dim(gfull, me * width, width, axis=a)

    out = {}

    def emit(name, res, shape):
        for kind, arr in zip(("grad", "delta", "new_m", "new_v"), res):
            out[kind + "_" + name] = arr.reshape(shape)

    for name, key in (("ffn_w1", "w1t_"), ("ffn_w3", "w3t_")):
        shp = wts[name].shape
        view = lambda t: t.reshape((4,) + shp[2:])
        res = adamw_cols([io.recv[key + tag] for tag in FFN_TAGS], view(wts[name]), view(ms[name]), view(vs[name]),
                         "adamw_" + name)
        emit(name, res, shp)
    shp = wts["ffn_w2"].shape
    view = lambda t: t.reshape((4,) + shp[2:])
    res = adamw_rows([io.recv["w2_" + tag] for tag in FFN_TAGS], view(wts["ffn_w2"]), view(ms["ffn_w2"]),
                     view(vs["ffn_w2"]), "adamw_ffn_w2")
    emit("ffn_w2", res, shp)
    res = adamw_cols([io.recv["w_int"]], wts["ssm_w_in"], ms["ssm_w_in"], vs["ssm_w_in"], "adamw_ssm_w_in")
    emit("ssm_w_in", res, wts["ssm_w_in"].shape)
    for name, key in (("ssm_w_out", "w_out"), ("w_kv", "w_kv"), ("w_q", "w_q"), ("w_o", "w_o")):
        shp = wts[name].shape
        view = lambda t: t.reshape((1,) + shp[-2:])
        res = adamw_rows([io.recv[key]], view(wts[name]), view(ms[name]), view(vs[name]), "adamw_" + name)
        emit(name, res, shp)

    res_s = rowmap(lambda gg, ww, mm_, vv: _adamw(gg, ww, mm_, vv),
                   [_small_local(g_small), _small_local(wts), _small_local(ms), _small_local(vs)], [],
                   [(LANES, F32)] * 3, tm=SMALL_LOCAL_ROWS, name="adamw_small")
    flat_s = [r.reshape(-1) for r in res_s]
    off = 0
    for n, s, a in SMALL:
        shard = s if a is None else _shard_shape(s, a)
        cnt = int(np.prod(shard))
        out["grad_" + n] = g_small[n]
        for kind, arr in zip(("delta", "new_m", "new_v"), flat_s):
            out[kind + "_" + n] = arr[off:off + cnt].reshape(shard)
        off += cnt
    out["loss"] = loss
    out["grad_x"] = grad_x[None]
    return out


def kernel(x, ffn_norm, ffn_w1, ffn_w3, ffn_w2, ssm_norm, ssm_w_in, ssm_conv_w, ssm_conv_b, ssm_dt_bias, ssm_a_log, ssm_d, ssm_gate_norm, ssm_w_out, kv_norm, w_kv, k_norm, attn_norm, w_q, q_norm, sinks, w_o, rel_bias, loss_target, m_ffn_norm, m_ffn_w1, m_ffn_w3, m_ffn_w2, m_ssm_norm, m_ssm_w_in, m_ssm_conv_w, m_ssm_conv_b, m_ssm_dt_bias, m_ssm_a_log, m_ssm_d, m_ssm_gate_norm, m_ssm_w_out, m_kv_norm, m_w_kv, m_k_norm, m_attn_norm, m_w_q, m_q_norm, m_sinks, m_w_o, m_rel_bias, v_ffn_norm, v_ffn_w1, v_ffn_w3, v_ffn_w2, v_ssm_norm, v_ssm_w_in, v_ssm_conv_w, v_ssm_conv_b, v_ssm_dt_bias, v_ssm_a_log, v_ssm_d, v_ssm_gate_norm, v_ssm_w_out, v_kv_norm, v_w_kv, v_k_norm, v_attn_norm, v_w_q, v_q_norm, v_sinks, v_w_o, v_rel_bias):
    args = locals()
    wts = {n: args[n] for n in WEIGHT_NAMES}
    ms = {n: args["m_" + n] for n in WEIGHT_NAMES}
    vs = {n: args["v_" + n] for n in WEIGHT_NAMES}
    out = step(x, loss_target, wts, ms, vs)
    result = [out["loss"], out["grad_x"]]
    for kind in ("grad", "delta", "new_m", "new_v"):
        result += [out[kind + "_" + n] for n in WEIGHT_NAMES]
    return tuple(result)
```

```python
import functools
import math
import operator

import numpy as np
import jax
import jax.numpy as jnp
from jax import lax
from jax.experimental import pallas as pl
from jax.experimental.pallas import tpu as pltpu

F32 = jnp.float32
BF16 = jnp.bfloat16

D_MODEL = 1024
D_FF = 2816
N_DEV = 8
SSM_D_INNER = 2048
SSM_HEAD_DIM = 64
SSM_HEADS = 32
SSM_GROUPS = 4
SSM_STATE = 128
SSM_CONV = 4
SSM_CHUNK = 256
SSM_CONV_DIM = SSM_D_INNER + 2 * SSM_GROUPS * SSM_STATE
SSM_IN_DIM = SSM_D_INNER + SSM_CONV_DIM + SSM_HEADS
ATT_HEAD_DIM = 64
ATT_HEADS = 16
ATT_KV_HEADS = 2
ATT_GROUP = 8
ATT_WINDOW = 128
REL_BUCKETS = 32
EPS = 1e-6
NEG = -1e30

ADAM_LR = 0.001
ADAM_B1 = 0.9
ADAM_B2 = 0.999
ADAM_EPS = 1e-08
ADAM_WD = 0.01
ADAM_STEP = 10

VMEM_LIMIT_BYTES = 52 * 1024 * 1024
LANES = 128
MESH_ID = pl.DeviceIdType.MESH
ANY_SPEC = pl.BlockSpec(memory_space=pl.ANY)

NT = (((1,), (1,)), ((), ()))
TN = (((0,), (0,)), ((), ()))
NN = (((1,), (0,)), ((), ()))


def _pick(dim, cands):
    for c in cands:
        if dim % c == 0:
            return c
    return dim


def _my_index():
    return 4 * lax.axis_index("x") + 2 * lax.axis_index("y") + lax.axis_index("c")


def _peer(k):
    x, y, c = lax.axis_index("x"), lax.axis_index("y"), lax.axis_index("c")
    px = 1 - x if (k >> 2) & 1 else x
    py = 1 - y if (k >> 1) & 1 else y
    pc = 1 - c if k & 1 else c
    return (px, py, pc), 4 * px + 2 * py + pc


def _piece(ref, axis, d, n):
    if axis is None:
        return ref.at[d]
    return ref.at[(slice(None),) * axis + (pl.ds(pl.multiple_of(d * n, 8), n),)]


SIBLING = 1
CHIP_PEERS = (4, 2, 6)
N_CHIPS = 4
SEMS_PER_ITEM = N_DEV - 1


def _my_chip():
    return 2 * lax.axis_index("x") + lax.axis_index("y")


class Comm:
    def __init__(self, items):
        self.items = list(items)

    def dst_shapes(self):
        out = []
        for kind, src, axis in self.items:
            s = tuple(src.shape)
            if kind == "g":
                shp = (N_DEV,) + s
            elif kind == "g2":
                shp = (N_DEV,) + s if axis is None else s[:axis] + (N_DEV * s[axis],) + s[axis + 1:]
            elif kind == "sa":
                shp = (s[0], 1) + s[2:]
            else:
                shp = s
            out.append(jax.ShapeDtypeStruct(shp, src.dtype))
        return out

    def scratch(self):
        n = len(self.items)
        return [pltpu.SemaphoreType.DMA((n * SEMS_PER_ITEM,)), pltpu.SemaphoreType.DMA((n * SEMS_PER_ITEM,)),
                pltpu.SemaphoreType.DMA((n,))]

    def _run(self, srcs, dsts, sems, starting):
        send_sems, recv_sems, local_sems = sems
        me = _my_index()
        core = lax.axis_index("c")
        chip = _my_chip()
        for i, (kind, src, axis) in enumerate(self.items):
            s_ref, d_ref = srcs[i], dsts[i]
            base = i * SEMS_PER_ITEM

            def rdma(src_ref, dst_ref, j, peer):
                return pltpu.make_async_remote_copy(
                    src_ref=src_ref, dst_ref=dst_ref, send_sem=send_sems.at[base + j], recv_sem=recv_sems.at[base + j],
                    device_id=peer, device_id_type=MESH_ID)

            if kind == "g":
                local = pltpu.make_async_copy(s_ref, d_ref.at[me], local_sems.at[i])
                outs = [rdma(s_ref, d_ref.at[me], k - 1, _peer(k)[0]) for k in range(1, N_DEV)]
                if starting:
                    local.start()
                    for cp in outs:
                        cp.start()
                else:
                    for k in range(1, N_DEV):
                        rdma(s_ref, d_ref.at[_peer(k)[1]], k - 1, _peer(k)[0]).wait_recv()
                    for cp in outs:
                        cp.wait_send()
                    local.wait()
            elif kind == "g2":
                n = None if axis is None else src.shape[axis]
                mine = _piece(d_ref, axis, me, n)
                sib = _peer(SIBLING)[0]
                local = pltpu.make_async_copy(s_ref, mine, local_sems.at[i])
                outs = [rdma(s_ref, mine, 0, sib)] + [rdma(s_ref, mine, 1 + j, _peer(k)[0])
                                                      for j, k in enumerate(CHIP_PEERS)]
                if starting:
                    local.start()
                    for cp in outs:
                        cp.start()
                else:
                    passed = []
                    for j, k in enumerate(CHIP_PEERS):
                        theirs = _piece(d_ref, axis, _peer(k)[1], n)
                        rdma(s_ref, theirs, 1 + j, _peer(k)[0]).wait_recv()
                        fwd = rdma(theirs, theirs, 4 + j, sib)
                        fwd.start()
                        passed.append(fwd)
                    rdma(s_ref, _piece(d_ref, axis, _peer(SIBLING)[1], n), 0, sib).wait_recv()
                    for j, k in enumerate(CHIP_PEERS):
                        rdma(s_ref, _piece(d_ref, axis, _peer(k ^ SIBLING)[1], n), 4 + j, sib).wait_recv()
                    for cp in outs + passed:
                        cp.wait_send()
                    local.wait()
            elif kind == "sa":
                cp = rdma(s_ref.at[(slice(None), pl.ds(1 - core, 1))], d_ref, 0, _peer(SIBLING)[0])
                if starting:
                    cp.start()
                else:
                    cp.wait_recv()
                    cp.wait_send()
            else:
                local = pltpu.make_async_copy(s_ref.at[chip], d_ref.at[chip], local_sems.at[i])
                outs = [rdma(s_ref.at[_peer(k)[1] >> 1], d_ref.at[chip], 1 + j, _peer(k)[0])
                        for j, k in enumerate(CHIP_PEERS)]
                if starting:
                    local.start()
                    for cp in outs:
                        cp.start()
                else:
                    for j, k in enumerate(CHIP_PEERS):
                        rdma(s_ref.at[chip], d_ref.at[_peer(k)[1] >> 1], 1 + j, _peer(k)[0]).wait_recv()
                    for cp in outs:
                        cp.wait_send()
                    local.wait()

    def start(self, srcs, dsts, sems):
        self._run(srcs, dsts, sems, True)

    def wait(self, srcs, dsts, sems):
        self._run(srcs, dsts, sems, False)


def pcall(body, *, name, grid, in_specs, out_specs, out_shape, args, scratch=(), hook=None):
    cparams = pltpu.CompilerParams(dimension_semantics=("arbitrary",) * len(grid), vmem_limit_bytes=VMEM_LIMIT_BYTES)
    if hook is None:
        outs = pl.pallas_call(body, name=name, grid=grid, in_specs=list(in_specs), out_specs=list(out_specs),
                              out_shape=list(out_shape), scratch_shapes=list(scratch), compiler_params=cparams)(*args)
        return list(outs)
    comm, sink = hook
    n_in, n_out, n_scr, n_it = len(args), len(out_shape), len(scratch), len(comm.items)
    dims = tuple(grid)

    def wrapped(*refs):
        p = 0
        ins = refs[p:p + n_in]
        p += n_in
        csrc = refs[p:p + n_it]
        p += n_it
        outs = refs[p:p + n_out]
        p += n_out
        cdst = refs[p:p + n_it]
        p += n_it
        scr = refs[p:p + n_scr]
        p += n_scr
        sems = refs[p:p + 3]
        if dims:
            ids = [pl.program_id(a) for a in range(len(dims))]
            first = functools.reduce(operator.and_, [i == 0 for i in ids])
            last = functools.reduce(operator.and_, [i == d - 1 for i, d in zip(ids, dims)])

            @pl.when(first)
            def _():
                comm.start(csrc, cdst, sems)

            body(*ins, *outs, *scr)

            @pl.when(last)
            def _():
                comm.wait(csrc, cdst, sems)
        else:
            comm.start(csrc, cdst, sems)
            body(*ins, *outs, *scr)
            comm.wait(csrc, cdst, sems)

    res = pl.pallas_call(
        wrapped, name=name, grid=grid,
        in_specs=list(in_specs) + [ANY_SPEC] * n_it, out_specs=list(out_specs) + [ANY_SPEC] * n_it,
        out_shape=list(out_shape) + comm.dst_shapes(), scratch_shapes=list(scratch) + comm.scratch(),
        compiler_params=cparams,
    )(*args, *[src for _, src, _ in comm.items])
    res = list(res)
    sink(res[n_out:])
    return res[:n_out]


def comm_only(comm, name):
    got = []
    pcall(lambda *refs: None, name=name, grid=(), in_specs=[], out_specs=[], out_shape=[], args=[],
          hook=(comm, got.extend))
    return got


def mm(a, b, *, ta=False, tb=False, out_dtype=F32, res=None, alpha=1.0, name, hook=None):
    if ta:
        k_dim, m_dim = a.shape
    else:
        m_dim, k_dim = a.shape
    if tb:
        n_dim, k2 = b.shape
    else:
        k2, n_dim = b.shape
    assert k_dim == k2, (a.shape, b.shape, ta, tb)
    tn = _pick(n_dim, (1024, 1408, 512, 256, 128))
    tm = _pick(m_dim, (1024, 1408, 512, 256, 128)) if tn <= 1024 else _pick(m_dim, (512, 256, 128))
    tk = _pick(k_dim, (512, 1408, 256, 128))
    nk = k_dim // tk
    has_res = res is not None
    dn = (((0 if ta else 1,), (1 if tb else 0,)), ((), ()))

    def body(*refs):
        if has_res:
            a_ref, b_ref, r_ref, o_ref, acc_ref = refs
        else:
            a_ref, b_ref, o_ref, acc_ref = refs
        k = pl.program_id(2)

        @pl.when(k == 0)
        def _():
            acc_ref[...] = jnp.zeros_like(acc_ref)

        acc_ref[...] += lax.dot_general(a_ref[...].astype(BF16), b_ref[...].astype(BF16), dn,
                                        preferred_element_type=F32)

        @pl.when(k == nk - 1)
        def _():
            r = acc_ref[...]
            if alpha != 1.0:
                r = r * alpha
            if has_res:
                r = r_ref[...] + r
            o_ref[...] = r.astype(o_ref.dtype)

    a_spec = pl.BlockSpec((tk, tm), lambda i, j, k: (k, i)) if ta else pl.BlockSpec((tm, tk), lambda i, j, k: (i, k))
    b_spec = pl.BlockSpec((tn, tk), lambda i, j, k: (j, k)) if tb else pl.BlockSpec((tk, tn), lambda i, j, k: (k, j))
    o_spec = pl.BlockSpec((tm, tn), lambda i, j, k: (i, j))
    in_specs = [a_spec, b_spec] + ([o_spec] if has_res else [])
    args = [a, b] + ([res] if has_res else [])
    out, = pcall(body, name=name, grid=(m_dim // tm, n_dim // tn, nk), in_specs=in_specs, out_specs=[o_spec],
                 out_shape=[jax.ShapeDtypeStruct((m_dim, n_dim), out_dtype)], args=args,
                 scratch=[pltpu.VMEM((tm, tn), F32)], hook=hook)
    return out


def rowmap(fn, rows, consts=(), out_rows=(), out_accs=(), *, tm, name, hook=None):
    first = rows[0][0] if isinstance(rows[0], tuple) else rows[0]
    t_dim = first.shape[0]
    assert t_dim % tm == 0, (t_dim, tm)
    n_r, n_c, n_o = len(rows), len(consts), len(out_rows)

    def body(*refs):
        ins = [r[...] for r in refs[:n_r + n_c]]
        o_refs = refs[n_r + n_c:]
        outs = tuple(fn(*ins))
        for o_ref, val in zip(o_refs[:n_o], outs[:n_o]):
            o_ref[...] = val.astype(o_ref.dtype)
        if out_accs:
            @pl.when(pl.program_id(0) == 0)
            def _():
                for o_ref in o_refs[n_o:]:
                    o_ref[...] = jnp.zeros_like(o_ref)

            for o_ref, val in zip(o_refs[n_o:], outs[n_o:]):
                o_ref[...] += val

    in_specs, args = [], []
    for r in rows:
        if isinstance(r, tuple):
            args.append(r[0])
            in_specs.append(r[1])
        else:
            args.append(r)
            in_specs.append(pl.BlockSpec((tm, r.shape[1]), lambda i: (i, 0)))
    for c in consts:
        args.append(c)
        in_specs.append(pl.BlockSpec(c.shape, lambda i, nd=c.ndim: (0,) * nd))
    out_specs = [pl.BlockSpec((tm, w), lambda i: (i, 0)) for (w, _) in out_rows]
    out_specs += [pl.BlockSpec(s, lambda i, nd=len(s): (0,) * nd) for s in out_accs]
    out_shape = [jax.ShapeDtypeStruct((t_dim, w), dt) for (w, dt) in out_rows]
    out_shape += [jax.ShapeDtypeStruct(s, F32) for s in out_accs]
    return pcall(body, name=name, grid=(t_dim // tm,), in_specs=in_specs, out_specs=out_specs, out_shape=out_shape,
                 args=args, hook=hook)


def _rms_fwd(x, g):
    r = lax.rsqrt(jnp.mean(x * x, axis=-1, keepdims=True) + EPS)
    return x * r * g


def _rms_bwd(x, g, dy):
    r = lax.rsqrt(jnp.mean(x * x, axis=-1, keepdims=True) + EPS)
    xh = x * r
    dg = jnp.sum(dy * xh, axis=0, keepdims=True)
    dxh = dy * g
    dx = r * (dxh - xh * jnp.mean(dxh * xh, axis=-1, keepdims=True))
    return dx, dg


def _sigmoid(x):
    return 1.0 / (1.0 + jnp.exp(-x))


def _silu(x):
    return x * _sigmoid(x)


def _silu_grad(x):
    s = _sigmoid(x)
    return s * (1.0 + x * (1.0 - s))


def _split3(x):
    hi = x.astype(BF16)
    r1 = x - hi.astype(F32)
    mid = r1.astype(BF16)
    lo = (r1 - mid.astype(F32)).astype(BF16)
    return hi, mid, lo


def _dot(a, b, dn=NN):
    return lax.dot_general(a.astype(BF16), b.astype(BF16), dn, preferred_element_type=F32)


def _col_of(mat, h):
    lane = lax.broadcasted_iota(jnp.int32, mat.shape, 1)
    return jnp.sum(jnp.where(lane == h, mat, 0.0), axis=1, keepdims=True)


FFN_TN = 1408


def ffn_upgate(h, g, w1t, w3t, nm, hook=None):
    t_dim = h.shape[0]
    tm = _pick(t_dim, (512, 256, 128))
    tn = FFN_TN

    n_j = D_FF // tn
    u_w = D_MODEL // n_j

    def body(h_ref, g_ref, w1_ref, w3_ref, u_ref, a_ref, b_ref, hm_ref):
        uu = _rms_fwd(h_ref[...], g_ref[...]).astype(BF16)
        for j in range(n_j):
            @pl.when(pl.program_id(0) == j)
            def _(j=j):
                u_ref[...] = uu[:, j * u_w:(j + 1) * u_w]

        a = lax.dot_general(uu, w1_ref[...], NT, preferred_element_type=F32)
        b = lax.dot_general(uu, w3_ref[...], NT, preferred_element_type=F32)
        a_ref[...] = a.astype(a_ref.dtype)
        b_ref[...] = b.astype(b_ref.dtype)
        hm_ref[...] = (_silu(a) * b).astype(hm_ref.dtype)

    row_spec = pl.BlockSpec((tm, D_MODEL), lambda j, i: (i, 0))
    w_spec = pl.BlockSpec((tn, D_MODEL), lambda j, i: (j, 0))
    o_spec = pl.BlockSpec((tm, tn), lambda j, i: (i, j))
    o_shape = jax.ShapeDtypeStruct((t_dim, D_FF), BF16)
    return pcall(body, name=nm, grid=(D_FF // tn, t_dim // tm),
                 in_specs=[row_spec, pl.BlockSpec((1, D_MODEL), lambda j, i: (0, 0)), w_spec, w_spec],
                 out_specs=[pl.BlockSpec((tm, u_w), lambda j, i: (i, j))] + [o_spec] * 3,
                 out_shape=[jax.ShapeDtypeStruct((t_dim, D_MODEL), BF16)] + [o_shape] * 3,
                 args=[h, g, w1t, w3t], hook=hook)


def ffn_dgate(dout_bf, w2, a, b, nm, hook=None):
    t_dim = dout_bf.shape[0]
    tm = _pick(t_dim, (512, 256, 128))
    tn = FFN_TN

    def body(d_ref, w2_ref, a_ref, b_ref, da_ref, db_ref):
        dhm = 0.5 * lax.dot_general(d_ref[...], w2_ref[...], NT, preferred_element_type=F32)
        av = a_ref[...].astype(F32)
        bv = b_ref[...].astype(F32)
        sg = _sigmoid(av)
        da_ref[...] = (dhm * bv * (sg * (1.0 + av * (1.0 - sg)))).astype(da_ref.dtype)
        db_ref[...] = (dhm * (av * sg)).astype(db_ref.dtype)

    t_spec = pl.BlockSpec((tm, tn), lambda j, i: (i, j))
    o_shape = jax.ShapeDtypeStruct((t_dim, D_FF), BF16)
    return pcall(body, name=nm, grid=(D_FF // tn, t_dim // tm),
                 in_specs=[pl.BlockSpec((tm, D_MODEL), lambda j, i: (i, 0)),
                           pl.BlockSpec((tn, D_MODEL), lambda j, i: (j, 0)), t_spec, t_spec],
                 out_specs=[t_spec] * 2, out_shape=[o_shape] * 2, args=[dout_bf, w2, a, b], hook=hook)


def ffn_fwd(h, g, tag, io, target=None):
    nm = "f" + tag
    u, a, b, hm = ffn_upgate(h, g, io.w("w1t_" + tag), io.w("w3t_" + tag), nm + "_upgate",
                             hook=io.hook(nm + "_upgate"))
    if target is None:
        return mm(hm, io.w("w2_" + tag), res=h, alpha=0.5, name=nm + "_down"), (u, a, b, hm)

    def down_loss(hmv, hv, t, w2):
        e = hv + 0.5 * _dot(hmv, w2) - t
        d = e * (1.0 / D_MODEL)
        return d, d, jnp.sum(e * e, axis=0, keepdims=True)

    res = rowmap(down_loss, [hm, h, target], [io.w("w2_" + tag)], [(D_MODEL, F32), (D_MODEL, BF16)],
                 [(1, D_MODEL)], tm=256, name=nm + "_down_loss")
    return res, (u, a, b, hm)


def du_norm_bwd(pairs, h, g, dout, nm, hook=None):
    t_dim = h.shape[0]
    tm = 256
    n_p = len(pairs)

    def body(*refs):
        h_ref, d_ref, g_ref = refs[2 * n_p:2 * n_p + 3]
        dh_ref, dhb_ref, dg_ref = refs[2 * n_p + 3:]
        du = None
        for p, (_, _, tb) in enumerate(pairs):
            t = lax.dot_general(refs[2 * p][...].astype(BF16), refs[2 * p + 1][...].astype(BF16), NT if tb else NN,
                                preferred_element_type=F32)
            du = t if du is None else du + t
        dx, dg = _rms_bwd(h_ref[...], g_ref[...], du)
        dh = d_ref[...] + dx
        dh_ref[...] = dh
        dhb_ref[...] = dh.astype(dhb_ref.dtype)

        @pl.when(pl.program_id(0) == 0)
        def _():
            dg_ref[...] = jnp.zeros_like(dg_ref)

        dg_ref[...] += dg

    in_specs, args = [], []
    for a, b, _ in pairs:
        in_specs += [pl.BlockSpec((tm, a.shape[1]), lambda i: (i, 0)), pl.BlockSpec(b.shape, lambda i: (0, 0))]
        args += [a, b]
    row_spec = pl.BlockSpec((tm, D_MODEL), lambda i: (i, 0))
    vec_spec = pl.BlockSpec((1, D_MODEL), lambda i: (0, 0))
    return pcall(body, name=nm, grid=(t_dim // tm,), in_specs=in_specs + [row_spec, row_spec, vec_spec],
                 out_specs=[row_spec, row_spec, vec_spec],
                 out_shape=[jax.ShapeDtypeStruct((t_dim, D_MODEL), F32), jax.ShapeDtypeStruct((t_dim, D_MODEL), BF16),
                            jax.ShapeDtypeStruct((1, D_MODEL), F32)],
                 args=args + [h, dout, g], hook=hook)


def ffn_bwd(h, g, tag, saved, dout, dout_bf, io):
    nm = "f" + tag
    w1t, w3t, w2 = io.w("w1t_" + tag), io.w("w3t_" + tag), io.w("w2_" + tag)
    u, a, b, hm = saved
    io.put("w2_" + tag, mm(hm, dout_bf, ta=True, alpha=0.5, out_dtype=BF16, name=nm + "_dw2",
                           hook=io.hook(nm + "_dw2")))
    da, db = ffn_dgate(dout_bf, w2, a, b, nm + "_dgate", hook=io.hook(nm + "_dgate"))
    io.put("w1t_" + tag, mm(da, u, ta=True, out_dtype=BF16, name=nm + "_dw1"))
    io.put("w3t_" + tag, mm(db, u, ta=True, out_dtype=BF16, name=nm + "_dw3", hook=io.hook(nm + "_dw3")))
    return du_norm_bwd([(da, w1t, False), (db, w3t, False)], h, g, dout, nm + "_du", hook=io.hook(nm + "_du"))


def _conv_pre(x, halo, w, b, tm):
    halo = jnp.where(pl.program_id(0) > 0, halo, 0.0)
    xx = jnp.concatenate([halo, x], axis=0)
    shifted = [xx[5 + k:5 + k + tm] for k in range(SSM_CONV)]
    acc = b + shifted[0] * w[0:1]
    for k in range(1, SSM_CONV):
        acc = acc + shifted[k] * w[k:k + 1]
    return acc, shifted


def _prev_halo_spec(tm, width):
    return pl.BlockSpec((8, width), lambda i: (jnp.maximum(i * (tm // 8) - 1, 0), 0))


def conv_fwd(xbc_raw, w, b, nm):
    tm = 128

    def fn(x, halo, ww, bb):
        acc, _ = _conv_pre(x, halo, ww, bb, tm)
        return (_silu(acc),)

    out, = rowmap(fn, [xbc_raw, (xbc_raw, _prev_halo_spec(tm, SSM_CONV_DIM))], [w, b],
                  [(SSM_CONV_DIM, F32)], tm=tm, name=nm)
    return out


def conv_bwd(xbc_raw, w, b, dxs, db_in, dc_in, nm):
    tm = 128
    t_dim = xbc_raw.shape[0]

    def fn1(x, halo, d1, d2, d3, ww, bb):
        acc, shifted = _conv_pre(x, halo, ww, bb, tm)
        dacc = jnp.concatenate([d1, d2, d3], axis=1) * _silu_grad(acc)
        dw = jnp.concatenate([jnp.sum(dacc * s, axis=0, keepdims=True) for s in shifted], axis=0)
        return dacc, dw, jnp.sum(dacc, axis=0, keepdims=True)

    dacc, dw, dbias = rowmap(fn1, [xbc_raw, (xbc_raw, _prev_halo_spec(tm, SSM_CONV_DIM)), dxs, db_in, dc_in],
                             [w, b], [(SSM_CONV_DIM, F32)], [(SSM_CONV, SSM_CONV_DIM), (1, SSM_CONV_DIM)],
                             tm=tm, name=nm + "_a")
    n_tiles = t_dim // tm

    def fn2(d, nxt, ww):
        nxt = jnp.where(pl.program_id(0) < n_tiles - 1, nxt, 0.0)
        dd = jnp.concatenate([d, nxt], axis=0)
        out = dd[3:3 + tm] * ww[0:1]
        for k in range(1, SSM_CONV):
            out = out + dd[3 - k:3 - k + tm] * ww[k:k + 1]
        return (out,)

    nxt_spec = pl.BlockSpec((8, SSM_CONV_DIM), lambda i: (jnp.minimum((i + 1) * (tm // 8), t_dim // 8 - 1), 0))
    dx, = rowmap(fn2, [dacc, (dacc, nxt_spec)], [w], [(SSM_CONV_DIM, BF16)], tm=tm, name=nm + "_b")
    return dx, dw, dbias


GRP_W = SSM_D_INNER // SSM_GROUPS
HPG = SSM_HEADS // SSM_GROUPS
HEAD_SHIFT = 6


def _split2(x):
    hi = x.astype(BF16)
    return hi, (x - hi.astype(F32)).astype(BF16)


def _expand_mats():
    e = ((lax.broadcasted_iota(jnp.int32, (HPG, GRP_W), 1) >> HEAD_SHIFT)
         == lax.broadcasted_iota(jnp.int32, (HPG, GRP_W), 0)).astype(BF16)
    et = ((lax.broadcasted_iota(jnp.int32, (GRP_W, HPG), 0) >> HEAD_SHIFT)
          == lax.broadcasted_iota(jnp.int32, (GRP_W, HPG), 1)).astype(BF16)
    return e, et


def _expand(v, e_m):
    hi, lo = _split2(v)
    return jnp.dot(hi, e_m, preferred_element_type=F32) + jnp.dot(lo, e_m, preferred_element_type=F32)


def _reduce8(v, et_m):
    acc = None
    for p in _split3(v):
        t = jnp.dot(p, et_m, preferred_element_type=F32)
        acc = t if acc is None else acc + t
    return acc


def _ssd_group_terms(dt_ref, dtT_ref, arow_ref, acol_ref):
    L = SSM_CHUNK
    r = lax.broadcasted_iota(jnp.int32, (L, L), 0)
    c = lax.broadcasted_iota(jnp.int32, (L, L), 1)
    tril = (r >= c).astype(BF16)
    triu = (r <= c).astype(BF16)
    dtg = dt_ref[0]
    acol = None
    for p in _split3(dtg * arow_ref[0]):
        t = jnp.dot(tril, p, preferred_element_type=F32)
        acol = t if acol is None else acol + t
    arowT = None
    for p in _split3(dtT_ref[0] * acol_ref[0]):
        t = jnp.dot(p, triu, preferred_element_type=F32)
        arowT = t if arowT is None else arowT + t
    return dtg, acol, arowT, r >= c


def _state_decay(a_last_col, et_m):
    hi, lo = _split2(jnp.broadcast_to(jnp.exp(a_last_col), (HPG, SSM_STATE)))
    return jnp.dot(et_m, hi, preferred_element_type=F32) + jnp.dot(et_m, lo, preferred_element_type=F32)


def _ssd_specs(nc, rev):
    L, N = SSM_CHUNK, SSM_STATE
    xcols = SSM_D_INNER // LANES
    ch = (lambda c: nc - 1 - c) if rev else (lambda c: c)
    return [
        pl.BlockSpec((L, GRP_W), lambda c, g: (ch(c), g)),
        pl.BlockSpec((L, N), lambda c, g: (ch(c), xcols + g)),
        pl.BlockSpec((L, N), lambda c, g: (ch(c), xcols + SSM_GROUPS + g)),
        pl.BlockSpec((1, L, HPG), lambda c, g: (g, ch(c), 0)),
        pl.BlockSpec((1, HPG, L), lambda c, g: (g, 0, ch(c))),
        pl.BlockSpec((1, 1, HPG), lambda c, g: (g, 0, 0)),
        pl.BlockSpec((1, HPG, 1), lambda c, g: (g, 0, 0)),
        pl.BlockSpec((1, GRP_W), lambda c, g: (0, g)),
    ]


def ssd_fwd(xbc, dt_g, dtT_g, a_row, a_col, dvec, nm, hook=None):
    t_dim = xbc.shape[0]
    L, P, N = SSM_CHUNK, SSM_HEAD_DIM, SSM_STATE
    nc = t_dim // L

    def body(x_ref, b_ref, c_ref, dt_ref, dtT_ref, arow_ref, acol_ref, dvec_ref, y_ref, st_ref, s_s):
        ci = pl.program_id(0)
        g = pl.program_id(1)

        @pl.when((ci == 0) & (g == 0))
        def _():
            s_s[...] = jnp.zeros_like(s_s)

        e_m, et_m = _expand_mats()
        dtg, acol, arowT, causal = _ssd_group_terms(dt_ref, dtT_ref, arow_ref, acol_ref)
        a_last_row = acol[L - 1:L, :]
        x = x_ref[...]
        bm = b_ref[...]
        cm = c_ref[...]
        cb = _dot(cm, bm, NT)
        s = s_s[g]
        st_ref[0, 0] = s
        ea_x = _expand(jnp.exp(acol), e_m)
        dt_x = _expand(dtg, e_m)
        w_x = _expand(jnp.exp(a_last_row - acol) * dtg, e_m)
        yb = ea_x * _dot(cm, s, NT) + dvec_ref[...] * x
        xd = (x * dt_x).astype(BF16)
        for e in range(HPG):
            sl = slice(e * P, (e + 1) * P)
            lm = jnp.exp(jnp.where(causal, acol[:, e:e + 1] - arowT[e:e + 1, :], NEG))
            m = (cb * lm).astype(BF16)
            y_ref[:, sl] = yb[:, sl] + jnp.dot(m, xd[:, sl], preferred_element_type=F32)
        s_s[g] = _state_decay(arowT[:, L - 1:L], et_m) * s + _dot(x * w_x, bm, TN)

    out_specs = [
        pl.BlockSpec((L, GRP_W), lambda c, g: (c, g)),
        pl.BlockSpec((1, 1, GRP_W, N), lambda c, g: (c, g, 0, 0)),
    ]
    return pcall(
        body, name=nm, grid=(nc, SSM_GROUPS), in_specs=_ssd_specs(nc, False), out_specs=out_specs,
        out_shape=[jax.ShapeDtypeStruct((t_dim, SSM_D_INNER), F32),
                   jax.ShapeDtypeStruct((nc, SSM_GROUPS, GRP_W, N), F32)],
        scratch=[pltpu.VMEM((SSM_GROUPS, GRP_W, N), F32)],
        args=[xbc, xbc, xbc, dt_g, dtT_g, a_row, a_col, dvec], hook=hook)


def ssd_bwd(dy, xbc, dt_g, dtT_g, a_row, a_col, dvec, states, nm, hook=None):
    t_dim = xbc.shape[0]
    L, P, N = SSM_CHUNK, SSM_HEAD_DIM, SSM_STATE
    nc = t_dim // L

    def body(dy_ref, x_ref, b_ref, c_ref, dt_ref, dtT_ref, arow_ref, acol_ref, dvec_ref, st_ref,
             dx_ref, db_ref, dc_ref, da_ref, ddt_ref, dd_ref, ds_s, yd_s, dxd_s):
        ci = pl.program_id(0)
        g = pl.program_id(1)

        @pl.when((ci == 0) & (g == 0))
        def _():
            ds_s[...] = jnp.zeros_like(ds_s)
            dd_ref[...] = jnp.zeros_like(dd_ref)

        e_m, et_m = _expand_mats()
        dtg, acol, arowT, causal = _ssd_group_terms(dt_ref, dtT_ref, arow_ref, acol_ref)
        a_last_row = acol[L - 1:L, :]
        x = x_ref[...]
        dy = dy_ref[...]
        bm = b_ref[...]
        cm = c_ref[...]
        cb = _dot(cm, bm, NT)
        s = st_ref[0, 0]
        dsp = ds_s[g]
        ew8 = jnp.exp(a_last_row - acol)
        ea_x = _expand(jnp.exp(acol), e_m)
        dt_x = _expand(dtg, e_m)
        ew_x = _expand(ew8, e_m)
        w_x = ew_x * dt_x
        z = _dot(cm, s, NT)
        dz = ea_x * dy
        dc = _dot(dz, s)
        ds_y = _dot(dz, cm, TN)
        du = _dot(bm, dsp, NT)
        u = x * w_x
        db = _dot(u, dsp)
        xd = (x * dt_x).astype(BF16)
        dyb = dy.astype(BF16)
        dcb = jnp.zeros((L, L), F32)
        for e in range(HPG):
            sl = slice(e * P, (e + 1) * P)
            lm = jnp.exp(jnp.where(causal, acol[:, e:e + 1] - arowT[e:e + 1, :], NEG))
            m = (cb * lm).astype(BF16)
            yd_s[:, sl] = jnp.dot(m, xd[:, sl], preferred_element_type=F32)
            dxd_s[:, sl] = lax.dot_general(m, dyb[:, sl], TN, preferred_element_type=F32)
            dcb = dcb + lax.dot_general(dyb[:, sl], xd[:, sl], NT, preferred_element_type=F32) * lm
        dxd = dxd_s[...]
        dx_ref[...] = dvec_ref[...] * dy + du * w_x + dt_x * dxd
        ddt = _reduce8(x * (ew_x * du + dxd), et_m)
        da = (_reduce8(dz * z + dyb.astype(F32) * yd_s[...], et_m)
              - _reduce8(xd.astype(F32) * dxd + du * u, et_m))
        dwa_row = _reduce8(jnp.broadcast_to(jnp.sum(du * u, axis=0, keepdims=True), (8, GRP_W)), et_m)[0:1]
        t_nh = None
        for p in _split3(dsp * s):
            t = lax.dot_general(p, et_m, TN, preferred_element_type=F32)
            t_nh = t if t_nh is None else t_nh + t
        d_last = dwa_row + jnp.exp(a_last_row) * jnp.sum(t_nh, axis=0, keepdims=True)
        row_l = lax.broadcasted_iota(jnp.int32, (L, 1), 0)
        da_ref[0] = da + jnp.where(row_l == L - 1, d_last, 0.0)
        ddt_ref[0] = ddt
        dd_ref[g] += jnp.sum(dy * x, axis=0, keepdims=True)
        dc_ref[...] = dc + _dot(dcb, bm)
        db_ref[...] = db + _dot(dcb, cm, TN)
        ds_s[g] = _state_decay(arowT[:, L - 1:L], et_m) * dsp + ds_y

    rc = lambda c: nc - 1 - c
    in_specs = ([pl.BlockSpec((L, GRP_W), lambda c, g: (rc(c), g))] + _ssd_specs(nc, True)
                + [pl.BlockSpec((1, 1, GRP_W, N), lambda c, g: (rc(c), g, 0, 0))])
    out_specs = [
        pl.BlockSpec((L, GRP_W), lambda c, g: (rc(c), g)),
        pl.BlockSpec((L, N), lambda c, g: (rc(c), g)),
        pl.BlockSpec((L, N), lambda c, g: (rc(c), g)),
        pl.BlockSpec((1, L, HPG), lambda c, g: (g, rc(c), 0)),
        pl.BlockSpec((1, L, HPG), lambda c, g: (g, rc(c), 0)),
        pl.BlockSpec((SSM_GROUPS, 1, GRP_W), lambda c, g: (0, 0, 0)),
    ]
    gn = SSM_GROUPS * N
    out_shape = [
        jax.ShapeDtypeStruct((t_dim, SSM_D_INNER), F32), jax.ShapeDtypeStruct((t_dim, gn), F32),
        jax.ShapeDtypeStruct((t_dim, gn), F32), jax.ShapeDtypeStruct((SSM_GROUPS, t_dim, HPG), F32),
        jax.ShapeDtypeStruct((SSM_GROUPS, t_dim, HPG), F32), jax.ShapeDtypeStruct((SSM_GROUPS, 1, GRP_W), F32),
    ]
    return pcall(
        body, name=nm, grid=(nc, SSM_GROUPS), in_specs=in_specs, out_specs=out_specs, out_shape=out_shape,
        scratch=[pltpu.VMEM((SSM_GROUPS, GRP_W, N), F32), pltpu.VMEM((L, GRP_W), F32), pltpu.VMEM((L, GRP_W), F32)],
        args=[dy, xbc, xbc, xbc, dt_g, dtT_g, a_row, a_col, dvec, states], hook=hook)


def _softplus(x):
    return jnp.maximum(x, 0.0) + jnp.log(1.0 + jnp.exp(-jnp.abs(x)))


def ssd_dt_bwd(da, ddt, dt, dt_raw, a_row, dt_bias, nm):
    L = SSM_CHUNK

    def fn(d_a, d_dt, dtv, raw, ar, bias):
        r = lax.broadcasted_iota(jnp.int32, (L, L), 0)
        c = lax.broadcasted_iota(jnp.int32, (L, L), 1)
        triu = (r <= c).astype(BF16)
        acc = None
        for p in _split3(d_a):
            t = jnp.dot(triu, p, preferred_element_type=F32)
            acc = t if acc is None else acc + t
        d_dt = d_dt + acc * ar
        d_a_h = jnp.sum(acc * dtv, axis=0, keepdims=True)
        d_raw = d_dt * _sigmoid(raw + bias)
        return d_raw, d_a_h, jnp.sum(d_raw, axis=0, keepdims=True)

    return rowmap(fn, [da, ddt, dt, dt_raw], [a_row, dt_bias], [(SSM_HEADS, BF16)],
                  [(1, SSM_HEADS), (1, SSM_HEADS)], tm=L, name=nm)


GN_W = SSM_D_INNER // SSM_GROUPS


def mamba_fwd(h, p, nm, io):
    def in_proj(x, gg, w_zt, w_xbct, w_dtt):
        uu = _rms_fwd(x, gg).astype(BF16)
        return uu, _dot(uu, w_zt, NT), _dot(uu, w_xbct, NT), _dot(uu, w_dtt, NT)

    u, z, xbc_raw, dt_raw = rowmap(in_proj, [h], [p["ssm_norm"], p["w_zt"], p["w_xbct"], p["w_dtt"]],
                                   [(D_MODEL, BF16), (SSM_D_INNER, F32), (SSM_CONV_DIM, F32), (SSM_HEADS, F32)],
                                   tm=256, name=nm + "_in", hook=io.hook(nm + "_in"))
    xbc = conv_fwd(xbc_raw, p["conv_w"], p["conv_b"], nm + "_conv")
    dt, = rowmap(lambda r, b: (_softplus(r + b),), [dt_raw], [p["dt_bias"]], [(SSM_HEADS, F32)], tm=256,
                 name=nm + "_softplus")
    dt_g = dt.reshape(-1, SSM_GROUPS, HPG).transpose(1, 0, 2)
    dtT_g = dt_g.transpose(0, 2, 1)
    y, states = ssd_fwd(xbc, dt_g, dtT_g, p["a_row"], p["a_col"], p["dvec"], nm + "_ssd", hook=io.hook(nm + "_ssd"))

    def gate_norm_out(yv, zv, hv, gg, w_out):
        t = yv * _silu(zv)
        yn = jnp.concatenate([_rms_fwd(t[:, k * GN_W:(k + 1) * GN_W], gg[:, k * GN_W:(k + 1) * GN_W])
                              for k in range(SSM_GROUPS)], axis=1).astype(BF16)
        return yn, hv + _dot(yn, w_out)

    yn, out = rowmap(gate_norm_out, [y, z, h], [p["gate_norm"], p["w_out"]],
                     [(SSM_D_INNER, BF16), (D_MODEL, F32)], tm=256, name=nm + "_out")
    return out, (u, z, xbc_raw, dt_raw, xbc, dt, dt_g, dtT_g, y, states, yn)


def mamba_bwd(h, p, saved, dout, dout_bf, nm, io):
    u, z, xbc_raw, dt_raw, xbc, dt, dt_g, dtT_g, y, states, yn = saved
    g = {}
    io.put("w_out", mm(yn, dout_bf, ta=True, out_dtype=BF16, name=nm + "_dwout"))
    def gate_norm_bwd(d_o, yv, zv, gg, w_out):
        d = _dot(d_o, w_out, NT)
        sz = _silu(zv)
        t = yv * sz
        dts, dgs = [], []
        for k in range(SSM_GROUPS):
            sl = slice(k * GN_W, (k + 1) * GN_W)
            dt_k, dg_k = _rms_bwd(t[:, sl], gg[:, sl], d[:, sl])
            dts.append(dt_k)
            dgs.append(dg_k)
        d_t = jnp.concatenate(dts, axis=1)
        return d_t * sz, d_t * yv * _silu_grad(zv), jnp.concatenate(dgs, axis=1)

    dy, dz, g["gate_norm"] = rowmap(gate_norm_bwd, [dout_bf, y, z], [p["gate_norm"], p["w_out"]],
                                    [(SSM_D_INNER, F32), (SSM_D_INNER, BF16)], [(1, SSM_D_INNER)], tm=256,
                                    name=nm + "_dgatenorm")
    dxs, db_in, dc_in, da_g, ddt_g, dd = ssd_bwd(
        dy, xbc, dt_g, dtT_g, p["a_row"], p["a_col"], p["dvec"], states, nm + "_dssd", hook=io.hook(nm + "_dssd"))
    g["dvec"] = dd
    per_head = lambda t: t.transpose(1, 0, 2).reshape(-1, SSM_HEADS)
    ddt_raw, g["a"], g["dt_bias"] = ssd_dt_bwd(per_head(da_g), per_head(ddt_g), dt, dt_raw, p["a_heads"],
                                               p["dt_bias"], nm + "_ddt")
    dxbc_raw, g["conv_w"], g["conv_b"] = conv_bwd(xbc_raw, p["conv_w"], p["conv_b"], dxs, db_in, dc_in, nm + "_dconv")
    io.put("w_int", jnp.concatenate([mm(dz, u, ta=True, out_dtype=BF16, name=nm + "_dwz"),
                                     mm(dxbc_raw, u, ta=True, out_dtype=BF16, name=nm + "_dwxbc"),
                                     mm(ddt_raw, u, ta=True, out_dtype=BF16, name=nm + "_dwdt")], axis=0))
    dh, dh_bf, g["ssm_norm"] = du_norm_bwd(
        [(dz, p["w_zt"], False), (dxbc_raw, p["w_xbct"], False), (ddt_raw, p["w_dtt"], False)],
        h, p["ssm_norm"], dout, nm + "_du", hook=io.hook(nm + "_du"))
    return dh, dh_bf, g


KV_W = ATT_KV_HEADS * ATT_HEAD_DIM


def kv_fwd(h, p, nm):
    def kv_proj(x, gg, w_kv, gk):
        uu = _rms_fwd(x, gg).astype(BF16)
        t = _dot(uu, w_kv)
        ks = [_rms_fwd(t[:, j * ATT_HEAD_DIM:(j + 1) * ATT_HEAD_DIM], gk) for j in range(ATT_KV_HEADS)]
        return uu, t, jnp.concatenate(ks, axis=1), t[:, KV_W:]

    u, kv_raw, k, v = rowmap(kv_proj, [h], [p["kv_norm"], p["w_kv"], p["k_norm"]],
                             [(D_MODEL, BF16), (2 * KV_W, F32), (KV_W, F32), (KV_W, F32)], tm=256, name=nm + "_proj")
    return k, v, (u, kv_raw)


def kv_bwd(h, p, saved, dk_cur, dk_prev, dv_cur, dv_prev, dout, nm, io):
    u, kv_raw = saved
    t_dim = h.shape[0]
    tm = ATT_WINDOW
    nb = t_dim // tm
    nxt = pl.BlockSpec((tm, KV_W), lambda i: (jnp.minimum(i + 1, nb - 1), 0))

    def fn(dkc, dkp, dvc, dvp, t, gg):
        live = pl.program_id(0) < nb - 1
        dk = dkc + jnp.where(live, dkp, 0.0)
        dv = dvc + jnp.where(live, dvp, 0.0)
        outs, dgs = [], None
        for j in range(ATT_KV_HEADS):
            sl = slice(j * ATT_HEAD_DIM, (j + 1) * ATT_HEAD_DIM)
            dx, dg = _rms_bwd(t[:, sl], gg, dk[:, sl])
            outs.append(dx)
            dgs = dg if dgs is None else dgs + dg
        return jnp.concatenate(outs + [dv], axis=1), dgs

    dkv_raw, dknorm = rowmap(fn, [dk_cur, (dk_prev, nxt), dv_cur, (dv_prev, nxt), kv_raw], [p["k_norm"]],
                             [(2 * KV_W, BF16)], [(1, ATT_HEAD_DIM)], tm=tm, name=nm + "_dknorm",
                             hook=io.hook(nm + "_dknorm"))
    g = {"k_norm": dknorm}
    io.put("w_kv", mm(u, dkv_raw, ta=True, out_dtype=BF16, name=nm + "_dwkv"))
    dh, dh_bf, g["kv_norm"] = du_norm_bwd([(dkv_raw, p["w_kv"], True)], h, p["kv_norm"], dout, nm + "_du",
                                          hook=io.hook(nm + "_du"))
    return dh, dh_bf, g


def _attn_scores(q_ref, kp_ref, kc_ref, vp_ref, vc_ref, qn_ref, bias_ref, sink_ref, kv):
    hd = ATT_HEAD_DIM
    blk = ATT_WINDOW
    sl = slice(kv * hd, (kv + 1) * hd)
    kk = jnp.concatenate([kp_ref[:, sl], kc_ref[:, sl]], axis=0)
    vv = jnp.concatenate([vp_ref[:, sl], vc_ref[:, sl]], axis=0)
    gq = qn_ref[...]
    raws, rinvs = [], []
    for r in range(ATT_GROUP):
        hh = kv * ATT_GROUP + r
        x = q_ref[:, hh * hd:(hh + 1) * hd]
        raws.append(x)
        rinvs.append(lax.rsqrt(jnp.mean(x * x, axis=-1, keepdims=True) + EPS))
    xh = jnp.concatenate([x * ri for x, ri in zip(raws, rinvs)], axis=0)
    rinv = jnp.concatenate(rinvs, axis=0)
    q8 = xh * gq
    s = _dot(q8, kk, NT) * (hd ** -0.5) + bias_ref[kv]
    colk = lax.broadcasted_iota(jnp.int32, (1, 2 * blk), 1)
    s = jnp.where((pl.program_id(0) > 0) | (colk >= blk), s, NEG)
    sink = sink_ref[kv]
    m = jnp.maximum(jnp.max(s, axis=-1, keepdims=True), sink)
    pexp = jnp.exp(s - m)
    e_sink = jnp.exp(sink - m)
    inv_den = 1.0 / (jnp.sum(pexp, axis=-1, keepdims=True) + e_sink)
    return kk, vv, xh, rinv, q8, pexp * inv_den, e_sink * inv_den


def _attn_specs(nb):
    blk = ATT_WINDOW
    cur = lambda i: (i, 0)
    prev = lambda i: (jnp.maximum(i - 1, 0), 0)
    return [
        pl.BlockSpec((blk, D_MODEL), cur),
        pl.BlockSpec((blk, KV_W), prev), pl.BlockSpec((blk, KV_W), cur),
        pl.BlockSpec((blk, KV_W), prev), pl.BlockSpec((blk, KV_W), cur),
        pl.BlockSpec((1, ATT_HEAD_DIM), lambda i: (0, 0)),
        pl.BlockSpec((ATT_KV_HEADS, ATT_GROUP * blk, 2 * blk), lambda i: (0, 0, 0)),
        pl.BlockSpec((ATT_KV_HEADS, ATT_GROUP * blk, 1), lambda i: (0, 0, 0)),
    ]


def attn_fwd(q_raw, k, v, q_norm, bias, sink_col, nm):
    t_dim = q_raw.shape[0]
    blk, hd = ATT_WINDOW, ATT_HEAD_DIM
    nb = t_dim // blk

    def body(q_ref, kp_ref, kc_ref, vp_ref, vc_ref, qn_ref, bias_ref, sink_ref, o_ref):
        for kv in range(ATT_KV_HEADS):
            kk, vv, xh, rinv, q8, prob, p_sink = _attn_scores(q_ref, kp_ref, kc_ref, vp_ref, vc_ref, qn_ref,
                                                              bias_ref, sink_ref, kv)
            o8 = _dot(prob, vv)
            for r in range(ATT_GROUP):
                hh = kv * ATT_GROUP + r
                o_ref[:, hh * hd:(hh + 1) * hd] = o8[r * blk:(r + 1) * blk].astype(o_ref.dtype)

    out, = pcall(body, name=nm, grid=(nb,), in_specs=_attn_specs(nb),
                 out_specs=[pl.BlockSpec((blk, D_MODEL), lambda i: (i, 0))],
                 out_shape=[jax.ShapeDtypeStruct((t_dim, D_MODEL), BF16)],
                 args=[q_raw, k, k, v, v, q_norm, bias, sink_col])
    return out


def attn_bwd(do, q_raw, k, v, q_norm, bias, sink_col, nm, hook=None):
    t_dim = q_raw.shape[0]
    blk, hd = ATT_WINDOW, ATT_HEAD_DIM
    nb = t_dim // blk
    scale = hd ** -0.5

    def body(do_ref, q_ref, kp_ref, kc_ref, vp_ref, vc_ref, qn_ref, bias_ref, sink_ref,
             dq_ref, dkc_ref, dkp_ref, dvc_ref, dvp_ref, dbias_ref, dsink_ref, dqn_ref):
        @pl.when(pl.program_id(0) == 0)
        def _():
            dbias_ref[...] = jnp.zeros_like(dbias_ref)
            dsink_ref[...] = jnp.zeros_like(dsink_ref)
            dqn_ref[...] = jnp.zeros_like(dqn_ref)

        gq = qn_ref[...]
        for kv in range(ATT_KV_HEADS):
            kk, vv, xh, rinv, q8, prob, p_sink = _attn_scores(q_ref, kp_ref, kc_ref, vp_ref, vc_ref, qn_ref,
                                                              bias_ref, sink_ref, kv)
            do8 = jnp.concatenate([do_ref[:, (kv * ATT_GROUP + r) * hd:(kv * ATT_GROUP + r + 1) * hd]
                                   for r in range(ATT_GROUP)], axis=0)
            dp = _dot(do8, vv, NT)
            delta = jnp.sum(prob * dp, axis=-1, keepdims=True)
            ds = prob * (dp - delta)
            dsink_ref[kv] += -p_sink * delta
            dbias_ref[kv] += ds
            ds_s = ds * scale
            dq8 = _dot(ds_s, kk)
            dkk = _dot(ds_s, q8, TN)
            dvv = _dot(prob, do8, TN)
            dqn_ref[...] += jnp.sum(dq8 * xh, axis=0, keepdims=True)
            dxh = dq8 * gq
            dq_raw8 = rinv * (dxh - xh * jnp.mean(dxh * xh, axis=-1, keepdims=True))
            for r in range(ATT_GROUP):
                hh = kv * ATT_GROUP + r
                dq_ref[:, hh * hd:(hh + 1) * hd] = dq_raw8[r * blk:(r + 1) * blk].astype(dq_ref.dtype)
            sl = slice(kv * hd, (kv + 1) * hd)
            dkp_ref[:, sl] = dkk[:blk]
            dkc_ref[:, sl] = dkk[blk:]
            dvp_ref[:, sl] = dvv[:blk]
            dvc_ref[:, sl] = dvv[blk:]

    cur = lambda i: (i, 0)
    row_spec = pl.BlockSpec((blk, KV_W), cur)
    out_specs = [
        pl.BlockSpec((blk, D_MODEL), cur), row_spec, row_spec, row_spec, row_spec,
        pl.BlockSpec((ATT_KV_HEADS, ATT_GROUP * blk, 2 * blk), lambda i: (0, 0, 0)),
        pl.BlockSpec((ATT_KV_HEADS, ATT_GROUP * blk, 1), lambda i: (0, 0, 0)),
        pl.BlockSpec((1, hd), lambda i: (0, 0)),
    ]
    kvs = jax.ShapeDtypeStruct((t_dim, KV_W), F32)
    out_shape = [
        jax.ShapeDtypeStruct((t_dim, D_MODEL), BF16), kvs, kvs, kvs, kvs,
        jax.ShapeDtypeStruct((ATT_KV_HEADS, ATT_GROUP * blk, 2 * blk), F32),
        jax.ShapeDtypeStruct((ATT_KV_HEADS, ATT_GROUP * blk, 1), F32),
        jax.ShapeDtypeStruct((1, hd), F32),
    ]
    return pcall(body, name=nm, grid=(nb,), in_specs=[pl.BlockSpec((blk, D_MODEL), cur)] + _attn_specs(nb),
                 out_specs=out_specs, out_shape=out_shape,
                 args=[do, q_raw, k, k, v, v, q_norm, bias, sink_col], hook=hook)


def _t5_bucket_np():
    blk = ATT_WINDOW
    qi = np.arange(blk)[:, None] + blk
    kj = np.arange(2 * blk)[None, :]
    dist = qi - kj
    n = np.maximum(dist, 0)
    max_exact = REL_BUCKETS // 2
    nf = np.maximum(n, 1).astype(np.float32)
    large = max_exact + (np.log(nf / max_exact) / math.log(ATT_WINDOW / max_exact)
                         * (REL_BUCKETS - max_exact)).astype(np.int32)
    large = np.minimum(large, REL_BUCKETS - 1)
    bucket = np.where(n < max_exact, n, large)
    in_window = (dist >= 0) & (dist < ATT_WINDOW)
    return bucket, in_window


def attn_block_fwd(h, k, v, p, nm):
    def q_proj(x, gg, w_q):
        uu = _rms_fwd(x, gg).astype(BF16)
        return uu, _dot(uu, w_q)

    u, q_raw = rowmap(q_proj, [h], [p["attn_norm"], p["w_q"]], [(D_MODEL, BF16), (D_MODEL, F32)], tm=256,
                      name=nm + "_q")
    o = attn_fwd(q_raw, k, v, p["q_norm"], p["bias"], p["sink_col"], nm + "_core")
    out = mm(o, p["w_o"], res=h, name=nm + "_o")
    return out, (u, q_raw, o)


def attn_block_bwd(h, k, v, p, saved, dout, dout_bf, nm, io):
    u, q_raw, o = saved
    g = {}
    io.put("w_o", mm(o, dout_bf, ta=True, out_dtype=BF16, name=nm + "_dwo", hook=io.hook(nm + "_dwo")))
    do = mm(dout_bf, p["w_o"], tb=True, name=nm + "_do")
    dq_raw, dkc, dkp, dvc, dvp, g["bias"], g["sink_col"], g["q_norm"] = attn_bwd(
        do, q_raw, k, v, p["q_norm"], p["bias"], p["sink_col"], nm + "_dcore", hook=io.hook(nm + "_dcore"))
    io.put("w_q", mm(u, dq_raw, ta=True, out_dtype=BF16, name=nm + "_dwq"))
    dh, dh_bf, g["attn_norm"] = du_norm_bwd([(dq_raw, p["w_q"], True)], h, p["attn_norm"], dout, nm + "_du")
    return dh, dh_bf, g, (dkc, dkp, dvc, dvp)


FFN_TAGS = ["00", "01", "10", "11"]


def local_step(x, target, small, io):
    bucket, in_window = _t5_bucket_np()
    blk = ATT_WINDOW
    w = small

    fnorm = {tag: w["ffn_norm"][int(tag[0]), int(tag[1])][None, :] for tag in FFN_TAGS}
    a_neg = -jnp.exp(w["ssm_a_log"][0])

    def mamba_p():
        w_int = io.w("w_int")
        return dict(ssm_norm=w["ssm_norm"], w_zt=w_int[:SSM_D_INNER],
                    w_xbct=w_int[SSM_D_INNER:SSM_D_INNER + SSM_CONV_DIM], w_dtt=w_int[SSM_D_INNER + SSM_CONV_DIM:],
                    conv_w=w["ssm_conv_w"][0], conv_b=w["ssm_conv_b"], dt_bias=w["ssm_dt_bias"],
                    a_heads=a_neg[None, :], a_row=a_neg.reshape(SSM_GROUPS, 1, HPG),
                    a_col=a_neg.reshape(SSM_GROUPS, HPG, 1),
                    dvec=jnp.repeat(w["ssm_d"][0], SSM_HEAD_DIM)[None, :],
                    gate_norm=w["ssm_gate_norm"], w_out=io.w("w_out"))

    rb = w["rel_bias"]
    onehot3 = (np.arange(REL_BUCKETS)[:, None, None] == bucket[None]).astype(np.float32)
    bias = jnp.einsum("bh,bqk->hqk", rb, onehot3, precision=lax.Precision.HIGHEST)
    bias = jnp.where(in_window[None], bias, NEG)
    bias = bias.reshape(ATT_KV_HEADS, ATT_GROUP * blk, 2 * blk)
    sink_col = jnp.repeat(w["sinks"][0], blk).reshape(ATT_KV_HEADS, ATT_GROUP * blk, 1)

    def attn_p():
        return dict(attn_norm=w["attn_norm"], w_q=io.w("w_q"), q_norm=w["q_norm"], bias=bias, sink_col=sink_col,
                    w_o=io.w("w_o"))

    def kv_p():
        return dict(kv_norm=w["kv_norm"][None, :], w_kv=io.w("w_kv"), k_norm=w["k_norm"][None, :])

    h0 = x
    h0a, s_f00 = ffn_fwd(h0, fnorm["00"], "00", io)
    mp = mamba_p()
    h0b, s_m = mamba_fwd(h0a, mp, "ssm", io)
    h1, s_f01 = ffn_fwd(h0b, fnorm["01"], "01", io)
    kp = kv_p()
    k, v, s_kv = kv_fwd(h1, kp, "kv")
    h1a, s_f10 = ffn_fwd(h1, fnorm["10"], "10", io)
    ap = attn_p()
    h1b, s_a = attn_block_fwd(h1a, k, v, ap, "att")
    (dh, dh_bf, sq), s_f11 = ffn_fwd(h1b, fnorm["11"], "11", io, target=target)
    loss_part = jnp.sum(sq) * (0.5 / D_MODEL)

    fg = {}

    def ffn_back(tag, h_in, saved, dh, dh_bf):
        dh, dh_bf, dg = ffn_bwd(h_in, fnorm[tag], tag, saved, dh, dh_bf, io)
        fg[tag] = dg[0]
        return dh, dh_bf

    dh, dh_bf = ffn_back("11", h1b, s_f11, dh, dh_bf)
    dh, dh_bf, ga, dkv = attn_block_bwd(h1a, k, v, ap, s_a, dh, dh_bf, "att", io)
    dh, dh_bf = ffn_back("10", h1, s_f10, dh, dh_bf)
    dh, dh_bf, gk = kv_bwd(h1, kp, s_kv, *dkv, dh, "kv", io)
    dh, dh_bf = ffn_back("01", h0b, s_f01, dh, dh_bf)
    dh, dh_bf, gm = mamba_bwd(h0a, mp, s_m, dh, dh_bf, "ssm", io)
    dh, dh_bf = ffn_back("00", h0, s_f00, dh, dh_bf)
    grad_x = dh

    grads = {}
    grads["ffn_norm"] = jnp.stack([fg[tag] for tag in FFN_TAGS]).reshape(2, 2, D_MODEL)
    grads["ssm_norm"] = gm["ssm_norm"]
    grads["ssm_conv_w"] = gm["conv_w"][None]
    grads["ssm_conv_b"] = gm["conv_b"]
    grads["ssm_dt_bias"] = gm["dt_bias"]
    grads["ssm_a_log"] = gm["a"] * a_neg[None, :]
    grads["ssm_d"] = jnp.sum(gm["dvec"].reshape(SSM_HEADS, SSM_HEAD_DIM), axis=1)[None, :]
    grads["ssm_gate_norm"] = gm["gate_norm"]
    grads["kv_norm"] = gk["kv_norm"][0]
    grads["k_norm"] = gk["k_norm"][0]
    grads["attn_norm"] = ga["attn_norm"]
    grads["q_norm"] = ga["q_norm"]
    grads["sinks"] = jnp.sum(ga["sink_col"].reshape(ATT_HEADS, blk), axis=1)[None, :]
    onehot = (np.arange(REL_BUCKETS)[:, None] == bucket.reshape(1, -1)).astype(np.float32)
    dbias2d = ga["bias"].reshape(ATT_HEADS, blk * 2 * blk)
    grads["rel_bias"] = mm(jnp.asarray(onehot, BF16), dbias2d, tb=True, name="drelbias")
    return loss_part, grad_x, grads


def _adamw(g, w, m, v):
    m = ADAM_B1 * m + (1.0 - ADAM_B1) * g
    v = ADAM_B2 * v + (1.0 - ADAM_B2) * (g * g)
    m_hat = m / (1.0 - ADAM_B1 ** ADAM_STEP)
    v_hat = v / (1.0 - ADAM_B2 ** ADAM_STEP)
    delta = -ADAM_LR * (m_hat / (jnp.sqrt(v_hat) + ADAM_EPS) + ADAM_WD * w)
    return delta, m, v


def _slot_sum(r):
    g = r[0].astype(F32)
    for d in range(1, r.shape[0]):
        g = g + r[d].astype(F32)
    return g


def adamw_rows(recvs, w, m, v, name):
    n_l, rows, width = w.shape
    n_slots = recvs[0].shape[0]
    tr = 32
    assert rows % tr == 0, rows
    nt = rows // tr

    def body(*refs):
        r_refs = refs[:n_l]
        w_ref, m_ref, v_ref, g_o, d_o, m_o, v_o = refs[n_l:]
        li = pl.program_id(0)
        for k in range(n_l):
            @pl.when(li == k)
            def _(k=k):
                g = _slot_sum(r_refs[k])
                delta, m2, v2 = _adamw(g, w_ref[0], m_ref[0], v_ref[0])
                g_o[0] = g
                d_o[0] = delta
                m_o[0] = m2
                v_o[0] = v2

    def r_spec(k):
        return pl.BlockSpec((n_slots, tr, width),
                            lambda li, j: (0, jnp.where(li == k, j, jnp.where(li > k, nt - 1, 0)), 0))

    w_spec = pl.BlockSpec((1, tr, width), lambda li, j: (li, j, 0))
    shp = jax.ShapeDtypeStruct(w.shape, F32)
    return pcall(body, name=name, grid=(n_l, nt), in_specs=[r_spec(k) for k in range(n_l)] + [w_spec] * 3,
                 out_specs=[w_spec] * 4, out_shape=[shp] * 4, args=list(recvs) + [w, m, v])


def adamw_cols(recvs, w, m, v, name):
    n_l, rows, n = w.shape
    n_slots = recvs[0].shape[0]
    tr = 256
    nt = rows // tr

    def body(*refs):
        r_refs = refs[:n_l]
        w_ref, m_ref, v_ref, g_o, d_o, m_o, v_o = refs[n_l:]
        li = pl.program_id(0)
        for k in range(n_l):
            @pl.when(li == k)
            def _(k=k):
                g = _slot_sum(r_refs[k]).T
                delta, m2, v2 = _adamw(g, w_ref[0], m_ref[0], v_ref[0])
                g_o[0] = g
                d_o[0] = delta
                m_o[0] = m2
                v_o[0] = v2

    def r_spec(k):
        return pl.BlockSpec((n_slots, n, tr),
                            lambda li, j: (0, 0, jnp.where(li == k, j, jnp.where(li > k, nt - 1, 0))))

    w_spec = pl.BlockSpec((1, tr, n), lambda li, j: (li, j, 0))
    shp = jax.ShapeDtypeStruct(w.shape, F32)
    return pcall(body, name=name, grid=(n_l, nt), in_specs=[r_spec(k) for k in range(n_l)] + [w_spec] * 3,
                 out_specs=[w_spec] * 4, out_shape=[shp] * 4, args=list(recvs) + [w, m, v])


WEIGHT_NAMES = ["ffn_norm", "ffn_w1", "ffn_w3", "ffn_w2", "ssm_norm", "ssm_w_in", "ssm_conv_w", "ssm_conv_b",
                "ssm_dt_bias", "ssm_a_log", "ssm_d", "ssm_gate_norm", "ssm_w_out", "kv_norm", "w_kv", "k_norm",
                "attn_norm", "w_q", "q_norm", "sinks", "w_o", "rel_bias"]

SMALL = [
    ("ffn_norm", (2, 2, 1024), 2), ("ssm_norm", (1, 1024), 1), ("ssm_conv_w", (1, 4, 3072), 2),
    ("ssm_conv_b", (1, 3072), 1), ("ssm_gate_norm", (1, 2048), 1),
    ("ssm_dt_bias", (1, 32), None), ("ssm_a_log", (1, 32), None), ("ssm_d", (1, 32), None),
    ("kv_norm", (1024,), None), ("k_norm", (64,), None), ("attn_norm", (1, 1024), None),
    ("q_norm", (1, 64), None), ("sinks", (1, 16), None), ("rel_bias", (32, 16), None),
]
SMALL_W = 1024
SMALL_FULL_ROWS = 32
SMALL_LOCAL_ROWS = 48

MAT_GROUPS = {
    "f00_up": ["w1t_00", "w3t_00"], "f00_down": ["w2_00"], "f01": ["w1t_01", "w3t_01", "w2_01"],
    "f10": ["w1t_10", "w3t_10", "w2_10"], "f11": ["w1t_11", "w3t_11", "w2_11"],
    "ssm": ["w_int", "w_out"], "att": ["w_q", "w_o", "w_kv"],
    "f00_early": ["w2_00", "w1t_00"], "f00_late": ["w3t_00"],
}
FIRST_GATHER = "f00_up"
GATHER_PLAN = {"f00_upgate": ["f00_down", "ssm"], "ssm_in": ["f01"], "ssm_ssd": ["att", "f10"],
               "f01_upgate": ["f11"]}
SCATTER_A_PLAN = {"att_dwo": "f11", "kv_dknorm": "f10", "kv_du": "att", "f01_du": "f01", "ssm_du": "ssm",
                  "f00_dw3": "f00_early", "f00_du": "f00_late"}
SCATTER_B_PLAN = {"att_dcore": "f11", "f01_dw2": "att", "f01_dgate": "f10", "ssm_dssd": "f01", "f00_dgate": "ssm",
                  "f00_du": "f00_early"}
LAST_SCATTER = "f00_late"
SLOT_MAJOR = ("w_int",)


def _shard_shape(s, a):
    return s[:a] + (s[a] // N_DEV,) + s[a + 1:]


def _unshard_view(stack, shard_shape, axis):
    moved = jnp.moveaxis(stack, 0, axis)
    return moved.reshape(shard_shape[:axis] + (N_DEV * shard_shape[axis],) + shard_shape[axis + 1:])


def _small_local(arrs):
    flat = jnp.concatenate([arrs[n].reshape(-1) for n, _, _ in SMALL])
    return jnp.pad(flat, (0, SMALL_LOCAL_ROWS * LANES - flat.shape[0])).reshape(SMALL_LOCAL_ROWS, LANES)


def chip_partial(g4, ra, name):
    _, _, n, width = g4.shape

    def body(g_ref, r_ref, o_ref):
        core = lax.axis_index("c")
        own = g_ref[0, pl.ds(core, 1)]
        o_ref[0] = (own[0].astype(F32) + r_ref[0, 0].astype(F32)).astype(o_ref.dtype)

    out, = pcall(body, name=name, grid=(N_CHIPS,),
                 in_specs=[pl.BlockSpec((1, 2, n, width), lambda q: (q, 0, 0, 0)),
                           pl.BlockSpec((1, 1, n, width), lambda q: (q, 0, 0, 0))],
                 out_specs=[pl.BlockSpec((1, n, width), lambda q: (q, 0, 0))],
                 out_shape=[jax.ShapeDtypeStruct((N_CHIPS, n, width), g4.dtype)], args=[g4, ra])
    return out


class StepIO:
    def __init__(self, pieces):
        self.pieces = pieces
        self.full = {}
        self.grad = {}
        self.from_sibling = {}
        self.recv = {}

    def w(self, name):
        return self.full[name]

    def put(self, name, g):
        self.grad[name] = g

    def _by_chip_core(self, name):
        g = self.grad[name]
        return g.reshape((N_CHIPS, 2, g.shape[0] // N_DEV) + g.shape[1:])

    def gather_items(self, groups):
        names = [n for grp in groups for n in MAT_GROUPS[grp]]
        items = [("g2", self.pieces[n], None if n in SLOT_MAJOR else 0) for n in names]

        def sink(outs):
            for n, o in zip(names, outs):
                self.full[n] = o.reshape((-1,) + o.shape[2:]) if n in SLOT_MAJOR else o

        return items, sink

    def scatter_a_items(self, group):
        names = MAT_GROUPS[group]
        items = [("sa", self._by_chip_core(n), None) for n in names]

        def sink(outs):
            for n, o in zip(names, outs):
                self.from_sibling[n] = o

        return items, sink

    def scatter_b_items(self, group):
        names = MAT_GROUPS[group]
        items = [("sb", chip_partial(self._by_chip_core(n), self.from_sibling[n], "partial_" + n), None)
                 for n in names]

        def sink(outs):
            for n, o in zip(names, outs):
                self.recv[n] = o

        return items, sink

    def hook(self, site):
        parts = []
        if site in GATHER_PLAN:
            parts.append(self.gather_items(GATHER_PLAN[site]))
        if site in SCATTER_A_PLAN:
            parts.append(self.scatter_a_items(SCATTER_A_PLAN[site]))
        if site in SCATTER_B_PLAN:
            parts.append(self.scatter_b_items(SCATTER_B_PLAN[site]))
        if not parts:
            return None
        return combine_hooks(parts)


def combine_hooks(parts):
    items = [it for its, _ in parts for it in its]

    def sink(outs):
        p = 0
        for its, snk in parts:
            snk(outs[p:p + len(its)])
            p += len(its)

    return Comm(items), sink


def step(x, target, wts, ms, vs):
    me = _my_index()

    pieces = {}
    for li in range(2):
        for hi in range(2):
            tag = "%d%d" % (li, hi)
            pieces["w1t_" + tag] = wts["ffn_w1"][li, hi].T.astype(BF16)
            pieces["w3t_" + tag] = wts["ffn_w3"][li, hi].T.astype(BF16)
            pieces["w2_" + tag] = wts["ffn_w2"][li, hi].astype(BF16)
    pieces["w_int"] = wts["ssm_w_in"][0].T.astype(BF16)
    pieces["w_out"] = wts["ssm_w_out"][0].astype(BF16)
    pieces["w_kv"] = wts["w_kv"].astype(BF16)
    pieces["w_q"] = wts["w_q"][0].astype(BF16)
    pieces["w_o"] = wts["w_o"][0].astype(BF16)
    io = StepIO(pieces)

    small_sharded = [(n, s, a) for n, s, a in SMALL if a is not None]
    loc = jnp.concatenate([wts[n].reshape(-1) for n, _, _ in small_sharded])
    loc_rows = -(-loc.shape[0] // (8 * LANES)) * 8
    loc = jnp.pad(loc, (0, loc_rows * LANES - loc.shape[0])).reshape(loc_rows, LANES)
    got_small = []
    comm, sink = combine_hooks([io.gather_items([FIRST_GATHER]), ([("g", loc, None)], got_small.extend)])
    sink(comm_only(comm, "gather_first"))
    gath_small = got_small[0].reshape(N_DEV, -1)
    small = {}
    off = 0
    for n, s, a in small_sharded:
        shard = _shard_shape(s, a)
        cnt = int(np.prod(shard))
        small[n] = _unshard_view(gath_small[:, off:off + cnt].reshape((N_DEV,) + shard), shard, a)
        off += cnt
    for n, s, a in SMALL:
        if a is None:
            small[n] = wts[n]

    loss_part, grad_x, g_small_local = local_step(x[0], target[0], small, io)
    loss = lax.psum(loss_part, ("x", "y", "c"))

    small_flat = jnp.concatenate([g_small_local[n].reshape(-1) for n, _, _ in SMALL])
    small_buf = jnp.pad(small_flat, (0, SMALL_FULL_ROWS * SMALL_W - small_flat.shape[0]))
    small_buf = small_buf.reshape(SMALL_FULL_ROWS, SMALL_W)
    got_small = []
    comm, sink = combine_hooks([io.scatter_b_items(LAST_SCATTER), ([("g", small_buf, None)], got_small.extend)])
    sink(comm_only(comm, "exchange_last"))
    small_all = got_small[0]

    def sum_body(r_ref, o_ref):
        o_ref[...] = _slot_sum(r_ref)

    vmem = pl.BlockSpec(memory_space=pltpu.VMEM)
    small_sum, = pcall(sum_body, name="sum_small", grid=(), in_specs=[vmem], out_specs=[vmem],
                       out_shape=[jax.ShapeDtypeStruct((SMALL_FULL_ROWS, SMALL_W), F32)], args=[small_all])
    small_sum = small_sum.reshape(-1)
    g_small = {}
    off = 0
    for n, s, a in SMALL:
        cnt = int(np.prod(s))
        gfull = small_sum[off:off + cnt].reshape(s)
        off += cnt
        if a is None:
            g_small[n] = gfull
        else:
            width = s[a] // N_DEV
            g_small[n] = lax.dynamic_slice_in_dim(gfull, me * width, width, axis=a)

    out = {}

    def emit(name, res, shape):
        for kind, arr in zip(("grad", "delta", "new_m", "new_v"), res):
            out[kind + "_" + name] = arr.reshape(shape)

    for name, key in (("ffn_w1", "w1t_"), ("ffn_w3", "w3t_")):
        shp = wts[name].shape
        view = lambda t: t.reshape((4,) + shp[2:])
        res = adamw_cols([io.recv[key + tag] for tag in FFN_TAGS], view(wts[name]), view(ms[name]), view(vs[name]),
                         "adamw_" + name)
        emit(name, res, shp)
    shp = wts["ffn_w2"].shape
    view = lambda t: t.reshape((4,) + shp[2:])
    res = adamw_rows([io.recv["w2_" + tag] for tag in FFN_TAGS], view(wts["ffn_w2"]), view(ms["ffn_w2"]),
                     view(vs["ffn_w2"]), "adamw_ffn_w2")
    emit("ffn_w2", res, shp)
    res = adamw_cols([io.recv["w_int"]], wts["ssm_w_in"], ms["ssm_w_in"], vs["ssm_w_in"], "adamw_ssm_w_in")
    emit("ssm_w_in", res, wts["ssm_w_in"].shape)
    for name, key in (("ssm_w_out", "w_out"), ("w_kv", "w_kv"), ("w_q", "w_q"), ("w_o", "w_o")):
        shp = wts[name].shape
        view = lambda t: t.reshape((1,) + shp[-2:])
        res = adamw_rows([io.recv[key]], view(wts[name]), view(ms[name]), view(vs[name]), "adamw_" + name)
        emit(name, res, shp)

    res_s = rowmap(lambda gg, ww, mm_, vv: _adamw(gg, ww, mm_, vv),
                   [_small_local(g_small), _small_local(wts), _small_local(ms), _small_local(vs)], [],
                   [(LANES, F32)] * 3, tm=SMALL_LOCAL_ROWS, name="adamw_small")
    flat_s = [r.reshape(-1) for r in res_s]
    off = 0
    for n, s, a in SMALL:
        shard = s if a is None else _shard_shape(s, a)
        cnt = int(np.prod(shard))
        out["grad_" + n] = g_small[n]
        for kind, arr in zip(("delta", "new_m", "new_v"), flat_s):
            out[kind + "_" + n] = arr[off:off + cnt].reshape(shard)
        off += cnt
    out["loss"] = loss
    out["grad_x"] = grad_x[None]
    return out


def kernel(x, ffn_norm, ffn_w1, ffn_w3, ffn_w2, ssm_norm, ssm_w_in, ssm_conv_w, ssm_conv_b, ssm_dt_bias, ssm_a_log, ssm_d, ssm_gate_norm, ssm_w_out, kv_norm, w_kv, k_norm, attn_norm, w_q, q_norm, sinks, w_o, rel_bias, loss_target, m_ffn_norm, m_ffn_w1, m_ffn_w3, m_ffn_w2, m_ssm_norm, m_ssm_w_in, m_ssm_conv_w, m_ssm_conv_b, m_ssm_dt_bias, m_ssm_a_log, m_ssm_d, m_ssm_gate_norm, m_ssm_w_out, m_kv_norm, m_w_kv, m_k_norm, m_attn_norm, m_w_q, m_q_norm, m_sinks, m_w_o, m_rel_bias, v_ffn_norm, v_ffn_w1, v_ffn_w3, v_ffn_w2, v_ssm_norm, v_ssm_w_in, v_ssm_conv_w, v_ssm_conv_b, v_ssm_dt_bias, v_ssm_a_log, v_ssm_d, v_ssm_gate_norm, v_ssm_w_out, v_kv_norm, v_w_kv, v_k_norm, v_attn_norm, v_w_q, v_q_norm, v_sinks, v_w_o, v_rel_bias):
    args = locals()
    wts = {n: args[n] for n in WEIGHT_NAMES}
    ms = {n: args["m_" + n] for n in WEIGHT_NAMES}
    vs = {n: args["v_" + n] for n in WEIGHT_NAMES}
    out = step(x, loss_target, wts, ms, vs)
    result = [out["loss"], out["grad_x"]]
    for kind in ("grad", "delta", "new_m", "new_v"):
        result += [out[kind + "_" + n] for n in WEIGHT_NAMES]
    return tuple(result)
```

```python
import functools
import math
import operator

import numpy as np
import jax
import jax.numpy as jnp
from jax import lax
from jax.experimental import pallas as pl
from jax.experimental.pallas import tpu as pltpu

F32 = jnp.float32
BF16 = jnp.bfloat16

D_MODEL = 1024
D_FF = 2816
N_DEV = 8
SSM_D_INNER = 2048
SSM_HEAD_DIM = 64
SSM_HEADS = 32
SSM_GROUPS = 4
SSM_STATE = 128
SSM_CONV = 4
SSM_CHUNK = 256
SSM_CONV_DIM = SSM_D_INNER + 2 * SSM_GROUPS * SSM_STATE
SSM_IN_DIM = SSM_D_INNER + SSM_CONV_DIM + SSM_HEADS
ATT_HEAD_DIM = 64
ATT_HEADS = 16
ATT_KV_HEADS = 2
ATT_GROUP = 8
ATT_WINDOW = 128
REL_BUCKETS = 32
EPS = 1e-6
NEG = -1e30

ADAM_LR = 0.001
ADAM_B1 = 0.9
ADAM_B2 = 0.999
ADAM_EPS = 1e-08
ADAM_WD = 0.01
ADAM_STEP = 10

VMEM_LIMIT_BYTES = 52 * 1024 * 1024
LANES = 128
MESH_ID = pl.DeviceIdType.MESH
ANY_SPEC = pl.BlockSpec(memory_space=pl.ANY)

NT = (((1,), (1,)), ((), ()))
TN = (((0,), (0,)), ((), ()))
NN = (((1,), (0,)), ((), ()))


def _pick(dim, cands):
    for c in cands:
        if dim % c == 0:
            return c
    return dim


def _my_index():
    return 4 * lax.axis_index("x") + 2 * lax.axis_index("y") + lax.axis_index("c")


def _peer(k):
    x, y, c = lax.axis_index("x"), lax.axis_index("y"), lax.axis_index("c")
    px = 1 - x if (k >> 2) & 1 else x
    py = 1 - y if (k >> 1) & 1 else y
    pc = 1 - c if k & 1 else c
    return (px, py, pc), 4 * px + 2 * py + pc


def _piece(ref, axis, d, n):
    if axis is None:
        return ref.at[d]
    return ref.at[(slice(None),) * axis + (pl.ds(pl.multiple_of(d * n, 8), n),)]


SIBLING = 1
CHIP_PEERS = (4, 2, 6)
N_CHIPS = 4
SEMS_PER_ITEM = N_DEV - 1


def _my_chip():
    return 2 * lax.axis_index("x") + lax.axis_index("y")


class Comm:
    def __init__(self, items):
        self.items = list(items)

    def dst_shapes(self):
        out = []
        for kind, src, axis in self.items:
            s = tuple(src.shape)
            if kind == "g":
                shp = (N_DEV,) + s
            elif kind == "g2":
                shp = (N_DEV,) + s if axis is None else s[:axis] + (N_DEV * s[axis],) + s[axis + 1:]
            elif kind == "sa":
                shp = (s[0], 1) + s[2:]
            else:
                shp = s
            out.append(jax.ShapeDtypeStruct(shp, src.dtype))
        return out

    def scratch(self):
        n = len(self.items)
        return [pltpu.SemaphoreType.DMA((n * SEMS_PER_ITEM,)), pltpu.SemaphoreType.DMA((n * SEMS_PER_ITEM,)),
                pltpu.SemaphoreType.DMA((n,))]

    def _run(self, srcs, dsts, sems, starting):
        send_sems, recv_sems, local_sems = sems
        me = _my_index()
        core = lax.axis_index("c")
        chip = _my_chip()
        for i, (kind, src, axis) in enumerate(self.items):
            s_ref, d_ref = srcs[i], dsts[i]
            base = i * SEMS_PER_ITEM

            def rdma(src_ref, dst_ref, j, peer):
                return pltpu.make_async_remote_copy(
                    src_ref=src_ref, dst_ref=dst_ref, send_sem=send_sems.at[base + j], recv_sem=recv_sems.at[base + j],
                    device_id=peer, device_id_type=MESH_ID)

            if kind == "g":
                local = pltpu.make_async_copy(s_ref, d_ref.at[me], local_sems.at[i])
                outs = [rdma(s_ref, d_ref.at[me], k - 1, _peer(k)[0]) for k in range(1, N_DEV)]
                if starting:
                    local.start()
                    for cp in outs:
                        cp.start()
                else:
                    for k in range(1, N_DEV):
                        rdma(s_ref, d_ref.at[_peer(k)[1]], k - 1, _peer(k)[0]).wait_recv()
                    for cp in outs:
                        cp.wait_send()
                    local.wait()
            elif kind == "g2":
                n = None if axis is None else src.shape[axis]
                mine = _piece(d_ref, axis, me, n)
                sib = _peer(SIBLING)[0]
                local = pltpu.make_async_copy(s_ref, mine, local_sems.at[i])
                outs = [rdma(s_ref, mine, 0, sib)] + [rdma(s_ref, mine, 1 + j, _peer(k)[0])
                                                      for j, k in enumerate(CHIP_PEERS)]
                if starting:
                    local.start()
                    for cp in outs:
                        cp.start()
                else:
                    passed = []
                    for j, k in enumerate(CHIP_PEERS):
                        theirs = _piece(d_ref, axis, _peer(k)[1], n)
                        rdma(s_ref, theirs, 1 + j, _peer(k)[0]).wait_recv()
                        fwd = rdma(theirs, theirs, 4 + j, sib)
                        fwd.start()
                        passed.append(fwd)
                    rdma(s_ref, _piece(d_ref, axis, _peer(SIBLING)[1], n), 0, sib).wait_recv()
                    for j, k in enumerate(CHIP_PEERS):
                        rdma(s_ref, _piece(d_ref, axis, _peer(k ^ SIBLING)[1], n), 4 + j, sib).wait_recv()
                    for cp in outs + passed:
                        cp.wait_send()
                    local.wait()
            elif kind == "sa":
                cp = rdma(s_ref.at[(slice(None), pl.ds(1 - core, 1))], d_ref, 0, _peer(SIBLING)[0])
                if starting:
                    cp.start()
                else:
                    cp.wait_recv()
                    cp.wait_send()
            else:
                local = pltpu.make_async_copy(s_ref.at[chip], d_ref.at[chip], local_sems.at[i])
                outs = [rdma(s_ref.at[_peer(k)[1] >> 1], d_ref.at[chip], 1 + j, _peer(k)[0])
                        for j, k in enumerate(CHIP_PEERS)]
                if starting:
                    local.start()
                    for cp in outs:
                        cp.start()
                else:
                    for j, k in enumerate(CHIP_PEERS):
                        rdma(s_ref.at[chip], d_ref.at[_peer(k)[1] >> 1], 1 + j, _peer(k)[0]).wait_recv()
                    for cp in outs:
                        cp.wait_send()
                    local.wait()

    def start(self, srcs, dsts, sems):
        self._run(srcs, dsts, sems, True)

    def wait(self, srcs, dsts, sems):
        self._run(srcs, dsts, sems, False)


def pcall(body, *, name, grid, in_specs, out_specs, out_shape, args, scratch=(), hook=None):
    cparams = pltpu.CompilerParams(dimension_semantics=("arbitrary",) * len(grid), vmem_limit_bytes=VMEM_LIMIT_BYTES)
    if hook is None:
        outs = pl.pallas_call(body, name=name, grid=grid, in_specs=list(in_specs), out_specs=list(out_specs),
                              out_shape=list(out_shape), scratch_shapes=list(scratch), compiler_params=cparams)(*args)
        return list(outs)
    comm, sink = hook
    n_in, n_out, n_scr, n_it = len(args), len(out_shape), len(scratch), len(comm.items)
    dims = tuple(grid)

    def wrapped(*refs):
        p = 0
        ins = refs[p:p + n_in]
        p += n_in
        csrc = refs[p:p + n_it]
        p += n_it
        outs = refs[p:p + n_out]
        p += n_out
        cdst = refs[p:p + n_it]
        p += n_it
        scr = refs[p:p + n_scr]
        p += n_scr
        sems = refs[p:p + 3]
        if dims:
            ids = [pl.program_id(a) for a in range(len(dims))]
            first = functools.reduce(operator.and_, [i == 0 for i in ids])
            last = functools.reduce(operator.and_, [i == d - 1 for i, d in zip(ids, dims)])

            @pl.when(first)
            def _():
                comm.start(csrc, cdst, sems)

            body(*ins, *outs, *scr)

            @pl.when(last)
            def _():
                comm.wait(csrc, cdst, sems)
        else:
            comm.start(csrc, cdst, sems)
            body(*ins, *outs, *scr)
            comm.wait(csrc, cdst, sems)

    res = pl.pallas_call(
        wrapped, name=name, grid=grid,
        in_specs=list(in_specs) + [ANY_SPEC] * n_it, out_specs=list(out_specs) + [ANY_SPEC] * n_it,
        out_shape=list(out_shape) + comm.dst_shapes(), scratch_shapes=list(scratch) + comm.scratch(),
        compiler_params=cparams,
    )(*args, *[src for _, src, _ in comm.items])
    res = list(res)
    sink(res[n_out:])
    return res[:n_out]


def comm_only(comm, name):
    got = []
    pcall(lambda *refs: None, name=name, grid=(), in_specs=[], out_specs=[], out_shape=[], args=[],
          hook=(comm, got.extend))
    return got


def mm(a, b, *, ta=False, tb=False, out_dtype=F32, res=None, alpha=1.0, name, hook=None):
    if ta:
        k_dim, m_dim = a.shape
    else:
        m_dim, k_dim = a.shape
    if tb:
        n_dim, k2 = b.shape
    else:
        k2, n_dim = b.shape
    assert k_dim == k2, (a.shape, b.shape, ta, tb)
    tn = _pick(n_dim, (1024, 1408, 512, 256, 128))
    tm = _pick(m_dim, (1024, 1408, 512, 256, 128)) if tn <= 1024 else _pick(m_dim, (512, 256, 128))
    tk = _pick(k_dim, (1024, 512, 256, 128)) if ta else _pick(k_dim, (512, 1408, 256, 128))
    nk = k_dim // tk
    has_res = res is not None
    dn = (((0 if ta else 1,), (1 if tb else 0,)), ((), ()))

    def body(*refs):
        if has_res:
            a_ref, b_ref, r_ref, o_ref, acc_ref = refs
        else:
            a_ref, b_ref, o_ref, acc_ref = refs
        k = pl.program_id(2)

        @pl.when(k == 0)
        def _():
            acc_ref[...] = jnp.zeros_like(acc_ref)

        acc_ref[...] += lax.dot_general(a_ref[...].astype(BF16), b_ref[...].astype(BF16), dn,
                                        preferred_element_type=F32)

        @pl.when(k == nk - 1)
        def _():
            r = acc_ref[...]
            if alpha != 1.0:
                r = r * alpha
            if has_res:
                r = r_ref[...] + r
            o_ref[...] = r.astype(o_ref.dtype)

    a_spec = pl.BlockSpec((tk, tm), lambda i, j, k: (k, i)) if ta else pl.BlockSpec((tm, tk), lambda i, j, k: (i, k))
    b_spec = pl.BlockSpec((tn, tk), lambda i, j, k: (j, k)) if tb else pl.BlockSpec((tk, tn), lambda i, j, k: (k, j))
    o_spec = pl.BlockSpec((tm, tn), lambda i, j, k: (i, j))
    in_specs = [a_spec, b_spec] + ([o_spec] if has_res else [])
    args = [a, b] + ([res] if has_res else [])
    out, = pcall(body, name=name, grid=(m_dim // tm, n_dim // tn, nk), in_specs=in_specs, out_specs=[o_spec],
                 out_shape=[jax.ShapeDtypeStruct((m_dim, n_dim), out_dtype)], args=args,
                 scratch=[pltpu.VMEM((tm, tn), F32)], hook=hook)
    return out


def rowmap(fn, rows, consts=(), out_rows=(), out_accs=(), *, tm, name, hook=None):
    first = rows[0][0] if isinstance(rows[0], tuple) else rows[0]
    t_dim = first.shape[0]
    assert t_dim % tm == 0, (t_dim, tm)
    n_r, n_c, n_o = len(rows), len(consts), len(out_rows)

    def body(*refs):
        ins = [r[...] for r in refs[:n_r + n_c]]
        o_refs = refs[n_r + n_c:]
        outs = tuple(fn(*ins))
        for o_ref, val in zip(o_refs[:n_o], outs[:n_o]):
            o_ref[...] = val.astype(o_ref.dtype)
        if out_accs:
            @pl.when(pl.program_id(0) == 0)
            def _():
                for o_ref in o_refs[n_o:]:
                    o_ref[...] = jnp.zeros_like(o_ref)

            for o_ref, val in zip(o_refs[n_o:], outs[n_o:]):
                o_ref[...] += val

    in_specs, args = [], []
    for r in rows:
        if isinstance(r, tuple):
            args.append(r[0])
            in_specs.append(r[1])
        else:
            args.append(r)
            in_specs.append(pl.BlockSpec((tm, r.shape[1]), lambda i: (i, 0)))
    for c in consts:
        args.append(c)
        in_specs.append(pl.BlockSpec(c.shape, lambda i, nd=c.ndim: (0,) * nd))
    out_specs = [pl.BlockSpec((tm, w), lambda i: (i, 0)) for (w, _) in out_rows]
    out_specs += [pl.BlockSpec(s, lambda i, nd=len(s): (0,) * nd) for s in out_accs]
    out_shape = [jax.ShapeDtypeStruct((t_dim, w), dt) for (w, dt) in out_rows]
    out_shape += [jax.ShapeDtypeStruct(s, F32) for s in out_accs]
    return pcall(body, name=name, grid=(t_dim // tm,), in_specs=in_specs, out_specs=out_specs, out_shape=out_shape,
                 args=args, hook=hook)


def _rms_fwd(x, g):
    r = lax.rsqrt(jnp.mean(x * x, axis=-1, keepdims=True) + EPS)
    return x * r * g


def _rms_bwd(x, g, dy):
    r = lax.rsqrt(jnp.mean(x * x, axis=-1, keepdims=True) + EPS)
    xh = x * r
    dg = jnp.sum(dy * xh, axis=0, keepdims=True)
    dxh = dy * g
    dx = r * (dxh - xh * jnp.mean(dxh * xh, axis=-1, keepdims=True))
    return dx, dg


def _sigmoid(x):
    return 1.0 / (1.0 + jnp.exp(-x))


def _silu(x):
    return x * _sigmoid(x)


def _silu_grad(x):
    s = _sigmoid(x)
    return s * (1.0 + x * (1.0 - s))


def _split3(x):
    hi = x.astype(BF16)
    r1 = x - hi.astype(F32)
    mid = r1.astype(BF16)
    lo = (r1 - mid.astype(F32)).astype(BF16)
    return hi, mid, lo


def _dot(a, b, dn=NN):
    return lax.dot_general(a.astype(BF16), b.astype(BF16), dn, preferred_element_type=F32)


def _col_of(mat, h):
    lane = lax.broadcasted_iota(jnp.int32, mat.shape, 1)
    return jnp.sum(jnp.where(lane == h, mat, 0.0), axis=1, keepdims=True)


FFN_TN = 1408


def ffn_upgate(h, g, w1t, w3t, nm, hook=None):
    t_dim = h.shape[0]
    tm = _pick(t_dim, (512, 256, 128))
    tn = FFN_TN

    n_j = D_FF // tn
    u_w = D_MODEL // n_j

    def body(h_ref, g_ref, w1_ref, w3_ref, u_ref, a_ref, b_ref, hm_ref):
        uu = _rms_fwd(h_ref[...], g_ref[...]).astype(BF16)
        for j in range(n_j):
            @pl.when(pl.program_id(0) == j)
            def _(j=j):
                u_ref[...] = uu[:, j * u_w:(j + 1) * u_w]

        a = lax.dot_general(uu, w1_ref[...], NT, preferred_element_type=F32)
        b = lax.dot_general(uu, w3_ref[...], NT, preferred_element_type=F32)
        a_ref[...] = a.astype(a_ref.dtype)
        b_ref[...] = b.astype(b_ref.dtype)
        hm_ref[...] = (_silu(a) * b).astype(hm_ref.dtype)

    row_spec = pl.BlockSpec((tm, D_MODEL), lambda j, i: (i, 0))
    w_spec = pl.BlockSpec((tn, D_MODEL), lambda j, i: (j, 0))
    o_spec = pl.BlockSpec((tm, tn), lambda j, i: (i, j))
    o_shape = jax.ShapeDtypeStruct((t_dim, D_FF), BF16)
    return pcall(body, name=nm, grid=(D_FF // tn, t_dim // tm),
                 in_specs=[row_spec, pl.BlockSpec((1, D_MODEL), lambda j, i: (0, 0)), w_spec, w_spec],
                 out_specs=[pl.BlockSpec((tm, u_w), lambda j, i: (i, j))] + [o_spec] * 3,
                 out_shape=[jax.ShapeDtypeStruct((t_dim, D_MODEL), BF16)] + [o_shape] * 3,
                 args=[h, g, w1t, w3t], hook=hook)


def ffn_dgate(dout_bf, w2, a, b, nm, hook=None):
    t_dim = dout_bf.shape[0]
    tm = _pick(t_dim, (512, 256, 128))
    tn = FFN_TN

    def body(d_ref, w2_ref, a_ref, b_ref, da_ref, db_ref):
        dhm = 0.5 * lax.dot_general(d_ref[...], w2_ref[...], NT, preferred_element_type=F32)
        av = a_ref[...].astype(F32)
        bv = b_ref[...].astype(F32)
        sg = _sigmoid(av)
        da_ref[...] = (dhm * bv * (sg * (1.0 + av * (1.0 - sg)))).astype(da_ref.dtype)
        db_ref[...] = (dhm * (av * sg)).astype(db_ref.dtype)

    t_spec = pl.BlockSpec((tm, tn), lambda j, i: (i, j))
    o_shape = jax.ShapeDtypeStruct((t_dim, D_FF), BF16)
    return pcall(body, name=nm, grid=(D_FF // tn, t_dim // tm),
                 in_specs=[pl.BlockSpec((tm, D_MODEL), lambda j, i: (i, 0)),
                           pl.BlockSpec((tn, D_MODEL), lambda j, i: (j, 0)), t_spec, t_spec],
                 out_specs=[t_spec] * 2, out_shape=[o_shape] * 2, args=[dout_bf, w2, a, b], hook=hook)


def ffn_fwd(h, g, tag, io, target=None):
    nm = "f" + tag
    u, a, b, hm = ffn_upgate(h, g, io.w("w1t_" + tag), io.w("w3t_" + tag), nm + "_upgate",
                             hook=io.hook(nm + "_upgate"))
    if target is None:
        return mm(hm, io.w("w2_" + tag), res=h, alpha=0.5, name=nm + "_down"), (u, a, b, hm)

    def down_loss(hmv, hv, t, w2):
        e = hv + 0.5 * _dot(hmv, w2) - t
        d = e * (1.0 / D_MODEL)
        return d, d, jnp.sum(e * e, axis=0, keepdims=True)

    res = rowmap(down_loss, [hm, h, target], [io.w("w2_" + tag)], [(D_MODEL, F32), (D_MODEL, BF16)],
                 [(1, D_MODEL)], tm=256, name=nm + "_down_loss")
    return res, (u, a, b, hm)


def du_norm_bwd(pairs, h, g, dout, nm, hook=None):
    t_dim = h.shape[0]
    tm = 256
    n_p = len(pairs)

    def body(*refs):
        h_ref, d_ref, g_ref = refs[2 * n_p:2 * n_p + 3]
        dh_ref, dhb_ref, dg_ref = refs[2 * n_p + 3:]
        du = None
        for p, (_, _, tb) in enumerate(pairs):
            t = lax.dot_general(refs[2 * p][...].astype(BF16), refs[2 * p + 1][...].astype(BF16), NT if tb else NN,
                                preferred_element_type=F32)
            du = t if du is None else du + t
        dx, dg = _rms_bwd(h_ref[...], g_ref[...], du)
        dh = d_ref[...] + dx
        dh_ref[...] = dh
        dhb_ref[...] = dh.astype(dhb_ref.dtype)

        @pl.when(pl.program_id(0) == 0)
        def _():
            dg_ref[...] = jnp.zeros_like(dg_ref)

        dg_ref[...] += dg

    in_specs, args = [], []
    for a, b, _ in pairs:
        in_specs += [pl.BlockSpec((tm, a.shape[1]), lambda i: (i, 0)), pl.BlockSpec(b.shape, lambda i: (0, 0))]
        args += [a, b]
    row_spec = pl.BlockSpec((tm, D_MODEL), lambda i: (i, 0))
    vec_spec = pl.BlockSpec((1, D_MODEL), lambda i: (0, 0))
    return pcall(body, name=nm, grid=(t_dim // tm,), in_specs=in_specs + [row_spec, row_spec, vec_spec],
                 out_specs=[row_spec, row_spec, vec_spec],
                 out_shape=[jax.ShapeDtypeStruct((t_dim, D_MODEL), F32), jax.ShapeDtypeStruct((t_dim, D_MODEL), BF16),
                            jax.ShapeDtypeStruct((1, D_MODEL), F32)],
                 args=args + [h, dout, g], hook=hook)


def ffn_bwd(h, g, tag, saved, dout, dout_bf, io):
    nm = "f" + tag
    w1t, w3t, w2 = io.w("w1t_" + tag), io.w("w3t_" + tag), io.w("w2_" + tag)
    u, a, b, hm = saved
    io.put("w2_" + tag, mm(hm, dout_bf, ta=True, alpha=0.5, out_dtype=BF16, name=nm + "_dw2",
                           hook=io.hook(nm + "_dw2")))
    da, db = ffn_dgate(dout_bf, w2, a, b, nm + "_dgate", hook=io.hook(nm + "_dgate"))
    io.put("w1t_" + tag, mm(da, u, ta=True, out_dtype=BF16, name=nm + "_dw1"))
    io.put("w3t_" + tag, mm(db, u, ta=True, out_dtype=BF16, name=nm + "_dw3", hook=io.hook(nm + "_dw3")))
    return du_norm_bwd([(da, w1t, False), (db, w3t, False)], h, g, dout, nm + "_du", hook=io.hook(nm + "_du"))


def _conv_pre(x, halo, w, b, tm):
    halo = jnp.where(pl.program_id(0) > 0, halo, 0.0)
    xx = jnp.concatenate([halo, x], axis=0)
    shifted = [xx[5 + k:5 + k + tm] for k in range(SSM_CONV)]
    acc = b + shifted[0] * w[0:1]
    for k in range(1, SSM_CONV):
        acc = acc + shifted[k] * w[k:k + 1]
    return acc, shifted


def _prev_halo_spec(tm, width):
    return pl.BlockSpec((8, width), lambda i: (jnp.maximum(i * (tm // 8) - 1, 0), 0))


def conv_fwd(xbc_raw, w, b, nm):
    tm = 128

    def fn(x, halo, ww, bb):
        acc, _ = _conv_pre(x, halo, ww, bb, tm)
        return (_silu(acc),)

    out, = rowmap(fn, [xbc_raw, (xbc_raw, _prev_halo_spec(tm, SSM_CONV_DIM))], [w, b],
                  [(SSM_CONV_DIM, F32)], tm=tm, name=nm)
    return out


def conv_bwd(xbc_raw, w, b, dxs, db_in, dc_in, nm):
    tm = 128
    t_dim = xbc_raw.shape[0]

    def fn1(x, halo, d1, d2, d3, ww, bb):
        acc, shifted = _conv_pre(x, halo, ww, bb, tm)
        dacc = jnp.concatenate([d1, d2, d3], axis=1) * _silu_grad(acc)
        dw = jnp.concatenate([jnp.sum(dacc * s, axis=0, keepdims=True) for s in shifted], axis=0)
        return dacc, dw, jnp.sum(dacc, axis=0, keepdims=True)

    dacc, dw, dbias = rowmap(fn1, [xbc_raw, (xbc_raw, _prev_halo_spec(tm, SSM_CONV_DIM)), dxs, db_in, dc_in],
                             [w, b], [(SSM_CONV_DIM, F32)], [(SSM_CONV, SSM_CONV_DIM), (1, SSM_CONV_DIM)],
                             tm=tm, name=nm + "_a")
    n_tiles = t_dim // tm

    def fn2(d, nxt, ww):
        nxt = jnp.where(pl.program_id(0) < n_tiles - 1, nxt, 0.0)
        dd = jnp.concatenate([d, nxt], axis=0)
        out = dd[3:3 + tm] * ww[0:1]
        for k in range(1, SSM_CONV):
            out = out + dd[3 - k:3 - k + tm] * ww[k:k + 1]
        return (out,)

    nxt_spec = pl.BlockSpec((8, SSM_CONV_DIM), lambda i: (jnp.minimum((i + 1) * (tm // 8), t_dim // 8 - 1), 0))
    dx, = rowmap(fn2, [dacc, (dacc, nxt_spec)], [w], [(SSM_CONV_DIM, BF16)], tm=tm, name=nm + "_b")
    return dx, dw, dbias


GRP_W = SSM_D_INNER // SSM_GROUPS
HPG = SSM_HEADS // SSM_GROUPS
HEAD_SHIFT = 6


def _split2(x):
    hi = x.astype(BF16)
    return hi, (x - hi.astype(F32)).astype(BF16)


def _expand_mats():
    e = ((lax.broadcasted_iota(jnp.int32, (HPG, GRP_W), 1) >> HEAD_SHIFT)
         == lax.broadcasted_iota(jnp.int32, (HPG, GRP_W), 0)).astype(BF16)
    et = ((lax.broadcasted_iota(jnp.int32, (GRP_W, HPG), 0) >> HEAD_SHIFT)
          == lax.broadcasted_iota(jnp.int32, (GRP_W, HPG), 1)).astype(BF16)
    return e, et


def _expand(v, e_m):
    hi, lo = _split2(v)
    return jnp.dot(hi, e_m, preferred_element_type=F32) + jnp.dot(lo, e_m, preferred_element_type=F32)


def _reduce8(v, et_m):
    acc = None
    for p in _split3(v):
        t = jnp.dot(p, et_m, preferred_element_type=F32)
        acc = t if acc is None else acc + t
    return acc


def _ssd_group_terms(dt_ref, dtT_ref, arow_ref, acol_ref):
    L = SSM_CHUNK
    r = lax.broadcasted_iota(jnp.int32, (L, L), 0)
    c = lax.broadcasted_iota(jnp.int32, (L, L), 1)
    tril = (r >= c).astype(BF16)
    triu = (r <= c).astype(BF16)
    dtg = dt_ref[0]
    acol = None
    for p in _split3(dtg * arow_ref[0]):
        t = jnp.dot(tril, p, preferred_element_type=F32)
        acol = t if acol is None else acol + t
    arowT = None
    for p in _split3(dtT_ref[0] * acol_ref[0]):
        t = jnp.dot(p, triu, preferred_element_type=F32)
        arowT = t if arowT is None else arowT + t
    return dtg, acol, arowT, r >= c


def _state_decay(a_last_col, et_m):
    hi, lo = _split2(jnp.broadcast_to(jnp.exp(a_last_col), (HPG, SSM_STATE)))
    return jnp.dot(et_m, hi, preferred_element_type=F32) + jnp.dot(et_m, lo, preferred_element_type=F32)


def _ssd_specs(nc, rev):
    L, N = SSM_CHUNK, SSM_STATE
    xcols = SSM_D_INNER // LANES
    ch = (lambda c: nc - 1 - c) if rev else (lambda c: c)
    return [
        pl.BlockSpec((L, GRP_W), lambda c, g: (ch(c), g)),
        pl.BlockSpec((L, N), lambda c, g: (ch(c), xcols + g)),
        pl.BlockSpec((L, N), lambda c, g: (ch(c), xcols + SSM_GROUPS + g)),
        pl.BlockSpec((1, L, HPG), lambda c, g: (g, ch(c), 0)),
        pl.BlockSpec((1, HPG, L), lambda c, g: (g, 0, ch(c))),
        pl.BlockSpec((1, 1, HPG), lambda c, g: (g, 0, 0)),
        pl.BlockSpec((1, HPG, 1), lambda c, g: (g, 0, 0)),
        pl.BlockSpec((1, GRP_W), lambda c, g: (0, g)),
    ]


def ssd_fwd(xbc, dt_g, dtT_g, a_row, a_col, dvec, nm, hook=None):
    t_dim = xbc.shape[0]
    L, P, N = SSM_CHUNK, SSM_HEAD_DIM, SSM_STATE
    nc = t_dim // L

    def body(x_ref, b_ref, c_ref, dt_ref, dtT_ref, arow_ref, acol_ref, dvec_ref, y_ref, st_ref, s_s):
        ci = pl.program_id(0)
        g = pl.program_id(1)

        @pl.when((ci == 0) & (g == 0))
        def _():
            s_s[...] = jnp.zeros_like(s_s)

        e_m, et_m = _expand_mats()
        dtg, acol, arowT, causal = _ssd_group_terms(dt_ref, dtT_ref, arow_ref, acol_ref)
        a_last_row = acol[L - 1:L, :]
        x = x_ref[...]
        bm = b_ref[...]
        cm = c_ref[...]
        cb = _dot(cm, bm, NT)
        s = s_s[g]
        st_ref[0, 0] = s
        ea_x = _expand(jnp.exp(acol), e_m)
        dt_x = _expand(dtg, e_m)
        w_x = _expand(jnp.exp(a_last_row - acol) * dtg, e_m)
        yb = ea_x * _dot(cm, s, NT) + dvec_ref[...] * x
        xd = (x * dt_x).astype(BF16)
        for e in range(HPG):
            sl = slice(e * P, (e + 1) * P)
            lm = jnp.exp(jnp.where(causal, acol[:, e:e + 1] - arowT[e:e + 1, :], NEG))
            m = (cb * lm).astype(BF16)
            y_ref[:, sl] = yb[:, sl] + jnp.dot(m, xd[:, sl], preferred_element_type=F32)
        s_s[g] = _state_decay(arowT[:, L - 1:L], et_m) * s + _dot(x * w_x, bm, TN)

    out_specs = [
        pl.BlockSpec((L, GRP_W), lambda c, g: (c, g)),
        pl.BlockSpec((1, 1, GRP_W, N), lambda c, g: (c, g, 0, 0)),
    ]
    return pcall(
        body, name=nm, grid=(nc, SSM_GROUPS), in_specs=_ssd_specs(nc, False), out_specs=out_specs,
        out_shape=[jax.ShapeDtypeStruct((t_dim, SSM_D_INNER), F32),
                   jax.ShapeDtypeStruct((nc, SSM_GROUPS, GRP_W, N), F32)],
        scratch=[pltpu.VMEM((SSM_GROUPS, GRP_W, N), F32)],
        args=[xbc, xbc, xbc, dt_g, dtT_g, a_row, a_col, dvec], hook=hook)


def ssd_bwd(dy, xbc, dt_g, dtT_g, a_row, a_col, dvec, states, nm, hook=None):
    t_dim = xbc.shape[0]
    L, P, N = SSM_CHUNK, SSM_HEAD_DIM, SSM_STATE
    nc = t_dim // L

    def body(dy_ref, x_ref, b_ref, c_ref, dt_ref, dtT_ref, arow_ref, acol_ref, dvec_ref, st_ref,
             dx_ref, db_ref, dc_ref, da_ref, ddt_ref, dd_ref, ds_s, yd_s, dxd_s):
        ci = pl.program_id(0)
        g = pl.program_id(1)

        @pl.when((ci == 0) & (g == 0))
        def _():
            ds_s[...] = jnp.zeros_like(ds_s)
            dd_ref[...] = jnp.zeros_like(dd_ref)

        e_m, et_m = _expand_mats()
        dtg, acol, arowT, causal = _ssd_group_terms(dt_ref, dtT_ref, arow_ref, acol_ref)
        a_last_row = acol[L - 1:L, :]
        x = x_ref[...]
        dy = dy_ref[...]
        bm = b_ref[...]
        cm = c_ref[...]
        cb = _dot(cm, bm, NT)
        s = st_ref[0, 0]
        dsp = ds_s[g]
        ew8 = jnp.exp(a_last_row - acol)
        ea_x = _expand(jnp.exp(acol), e_m)
        dt_x = _expand(dtg, e_m)
        ew_x = _expand(ew8, e_m)
        w_x = ew_x * dt_x
        z = _dot(cm, s, NT)
        dz = ea_x * dy
        dc = _dot(dz, s)
        ds_y = _dot(dz, cm, TN)
        du = _dot(bm, dsp, NT)
        u = x * w_x
        db = _dot(u, dsp)
        xd = (x * dt_x).astype(BF16)
        dyb = dy.astype(BF16)
        dcb = jnp.zeros((L, L), F32)
        for e in range(HPG):
            sl = slice(e * P, (e + 1) * P)
            lm = jnp.exp(jnp.where(causal, acol[:, e:e + 1] - arowT[e:e + 1, :], NEG))
            m = (cb * lm).astype(BF16)
            yd_s[:, sl] = jnp.dot(m, xd[:, sl], preferred_element_type=F32)
            dxd_s[:, sl] = lax.dot_general(m, dyb[:, sl], TN, preferred_element_type=F32)
            dcb = dcb + lax.dot_general(dyb[:, sl], xd[:, sl], NT, preferred_element_type=F32) * lm
        dxd = dxd_s[...]
        dx_ref[...] = dvec_ref[...] * dy + du * w_x + dt_x * dxd
        ddt = _reduce8(x * (ew_x * du + dxd), et_m)
        da = (_reduce8(dz * z + dyb.astype(F32) * yd_s[...], et_m)
              - _reduce8(xd.astype(F32) * dxd + du * u, et_m))
        dwa_row = _reduce8(jnp.broadcast_to(jnp.sum(du * u, axis=0, keepdims=True), (8, GRP_W)), et_m)[0:1]
        t_nh = None
        for p in _split3(dsp * s):
            t = lax.dot_general(p, et_m, TN, preferred_element_type=F32)
            t_nh = t if t_nh is None else t_nh + t
        d_last = dwa_row + jnp.exp(a_last_row) * jnp.sum(t_nh, axis=0, keepdims=True)
        row_l = lax.broadcasted_iota(jnp.int32, (L, 1), 0)
        da_ref[0] = da + jnp.where(row_l == L - 1, d_last, 0.0)
        ddt_ref[0] = ddt
        dd_ref[g] += jnp.sum(dy * x, axis=0, keepdims=True)
        dc_ref[...] = dc + _dot(dcb, bm)
        db_ref[...] = db + _dot(dcb, cm, TN)
        ds_s[g] = _state_decay(arowT[:, L - 1:L], et_m) * dsp + ds_y

    rc = lambda c: nc - 1 - c
    in_specs = ([pl.BlockSpec((L, GRP_W), lambda c, g: (rc(c), g))] + _ssd_specs(nc, True)
                + [pl.BlockSpec((1, 1, GRP_W, N), lambda c, g: (rc(c), g, 0, 0))])
    out_specs = [
        pl.BlockSpec((L, GRP_W), lambda c, g: (rc(c), g)),
        pl.BlockSpec((L, N), lambda c, g: (rc(c), g)),
        pl.BlockSpec((L, N), lambda c, g: (rc(c), g)),
        pl.BlockSpec((1, L, HPG), lambda c, g: (g, rc(c), 0)),
        pl.BlockSpec((1, L, HPG), lambda c, g: (g, rc(c), 0)),
        pl.BlockSpec((SSM_GROUPS, 1, GRP_W), lambda c, g: (0, 0, 0)),
    ]
    gn = SSM_GROUPS * N
    out_shape = [
        jax.ShapeDtypeStruct((t_dim, SSM_D_INNER), F32), jax.ShapeDtypeStruct((t_dim, gn), F32),
        jax.ShapeDtypeStruct((t_dim, gn), F32), jax.ShapeDtypeStruct((SSM_GROUPS, t_dim, HPG), F32),
        jax.ShapeDtypeStruct((SSM_GROUPS, t_dim, HPG), F32), jax.ShapeDtypeStruct((SSM_GROUPS, 1, GRP_W), F32),
    ]
    return pcall(
        body, name=nm, grid=(nc, SSM_GROUPS), in_specs=in_specs, out_specs=out_specs, out_shape=out_shape,
        scratch=[pltpu.VMEM((SSM_GROUPS, GRP_W, N), F32), pltpu.VMEM((L, GRP_W), F32), pltpu.VMEM((L, GRP_W), F32)],
        args=[dy, xbc, xbc, xbc, dt_g, dtT_g, a_row, a_col, dvec, states], hook=hook)


def _softplus(x):
    return jnp.maximum(x, 0.0) + jnp.log(1.0 + jnp.exp(-jnp.abs(x)))


def ssd_dt_bwd(da, ddt, dt, dt_raw, a_row, dt_bias, nm):
    L = SSM_CHUNK

    def fn(d_a, d_dt, dtv, raw, ar, bias):
        r = lax.broadcasted_iota(jnp.int32, (L, L), 0)
        c = lax.broadcasted_iota(jnp.int32, (L, L), 1)
        triu = (r <= c).astype(BF16)
        acc = None
        for p in _split3(d_a):
            t = jnp.dot(triu, p, preferred_element_type=F32)
            acc = t if acc is None else acc + t
        d_dt = d_dt + acc * ar
        d_a_h = jnp.sum(acc * dtv, axis=0, keepdims=True)
        d_raw = d_dt * _sigmoid(raw + bias)
        return d_raw, d_a_h, jnp.sum(d_raw, axis=0, keepdims=True)

    return rowmap(fn, [da, ddt, dt, dt_raw], [a_row, dt_bias], [(SSM_HEADS, BF16)],
                  [(1, SSM_HEADS), (1, SSM_HEADS)], tm=L, name=nm)


GN_W = SSM_D_INNER // SSM_GROUPS


def mamba_fwd(h, p, nm, io):
    def in_proj(x, gg, w_zt, w_xbct, w_dtt):
        uu = _rms_fwd(x, gg).astype(BF16)
        return uu, _dot(uu, w_zt, NT), _dot(uu, w_xbct, NT), _dot(uu, w_dtt, NT)

    u, z, xbc_raw, dt_raw = rowmap(in_proj, [h], [p["ssm_norm"], p["w_zt"], p["w_xbct"], p["w_dtt"]],
                                   [(D_MODEL, BF16), (SSM_D_INNER, F32), (SSM_CONV_DIM, F32), (SSM_HEADS, F32)],
                                   tm=256, name=nm + "_in", hook=io.hook(nm + "_in"))
    xbc = conv_fwd(xbc_raw, p["conv_w"], p["conv_b"], nm + "_conv")
    dt, = rowmap(lambda r, b: (_softplus(r + b),), [dt_raw], [p["dt_bias"]], [(SSM_HEADS, F32)], tm=256,
                 name=nm + "_softplus")
    dt_g = dt.reshape(-1, SSM_GROUPS, HPG).transpose(1, 0, 2)
    dtT_g = dt_g.transpose(0, 2, 1)
    y, states = ssd_fwd(xbc, dt_g, dtT_g, p["a_row"], p["a_col"], p["dvec"], nm + "_ssd", hook=io.hook(nm + "_ssd"))

    def gate_norm_out(yv, zv, hv, gg, w_out):
        t = yv * _silu(zv)
        yn = jnp.concatenate([_rms_fwd(t[:, k * GN_W:(k + 1) * GN_W], gg[:, k * GN_W:(k + 1) * GN_W])
                              for k in range(SSM_GROUPS)], axis=1).astype(BF16)
        return yn, hv + _dot(yn, w_out)

    yn, out = rowmap(gate_norm_out, [y, z, h], [p["gate_norm"], p["w_out"]],
                     [(SSM_D_INNER, BF16), (D_MODEL, F32)], tm=256, name=nm + "_out")
    return out, (u, z, xbc_raw, dt_raw, xbc, dt, dt_g, dtT_g, y, states, yn)


def mamba_bwd(h, p, saved, dout, dout_bf, nm, io):
    u, z, xbc_raw, dt_raw, xbc, dt, dt_g, dtT_g, y, states, yn = saved
    g = {}
    io.put("w_out", mm(yn, dout_bf, ta=True, out_dtype=BF16, name=nm + "_dwout"))
    def gate_norm_bwd(d_o, yv, zv, gg, w_out):
        d = _dot(d_o, w_out, NT)
        sz = _silu(zv)
        t = yv * sz
        dts, dgs = [], []
        for k in range(SSM_GROUPS):
            sl = slice(k * GN_W, (k + 1) * GN_W)
            dt_k, dg_k = _rms_bwd(t[:, sl], gg[:, sl], d[:, sl])
            dts.append(dt_k)
            dgs.append(dg_k)
        d_t = jnp.concatenate(dts, axis=1)
        return d_t * sz, d_t * yv * _silu_grad(zv), jnp.concatenate(dgs, axis=1)

    dy, dz, g["gate_norm"] = rowmap(gate_norm_bwd, [dout_bf, y, z], [p["gate_norm"], p["w_out"]],
                                    [(SSM_D_INNER, F32), (SSM_D_INNER, BF16)], [(1, SSM_D_INNER)], tm=256,
                                    name=nm + "_dgatenorm")
    dxs, db_in, dc_in, da_g, ddt_g, dd = ssd_bwd(
        dy, xbc, dt_g, dtT_g, p["a_row"], p["a_col"], p["dvec"], states, nm + "_dssd", hook=io.hook(nm + "_dssd"))
    g["dvec"] = dd
    per_head = lambda t: t.transpose(1, 0, 2).reshape(-1, SSM_HEADS)
    ddt_raw, g["a"], g["dt_bias"] = ssd_dt_bwd(per_head(da_g), per_head(ddt_g), dt, dt_raw, p["a_heads"],
                                               p["dt_bias"], nm + "_ddt")
    dxbc_raw, g["conv_w"], g["conv_b"] = conv_bwd(xbc_raw, p["conv_w"], p["conv_b"], dxs, db_in, dc_in, nm + "_dconv")
    io.put("w_int", jnp.concatenate([mm(dz, u, ta=True, out_dtype=BF16, name=nm + "_dwz"),
                                     mm(dxbc_raw, u, ta=True, out_dtype=BF16, name=nm + "_dwxbc"),
                                     mm(ddt_raw, u, ta=True, out_dtype=BF16, name=nm + "_dwdt")], axis=0))
    dh, dh_bf, g["ssm_norm"] = du_norm_bwd(
        [(dz, p["w_zt"], False), (dxbc_raw, p["w_xbct"], False), (ddt_raw, p["w_dtt"], False)],
        h, p["ssm_norm"], dout, nm + "_du", hook=io.hook(nm + "_du"))
    return dh, dh_bf, g


KV_W = ATT_KV_HEADS * ATT_HEAD_DIM


def kv_fwd(h, p, nm):
    def kv_proj(x, gg, w_kv, gk):
        uu = _rms_fwd(x, gg).astype(BF16)
        t = _dot(uu, w_kv)
        ks = [_rms_fwd(t[:, j * ATT_HEAD_DIM:(j + 1) * ATT_HEAD_DIM], gk) for j in range(ATT_KV_HEADS)]
        return uu, t, jnp.concatenate(ks, axis=1), t[:, KV_W:]

    u, kv_raw, k, v = rowmap(kv_proj, [h], [p["kv_norm"], p["w_kv"], p["k_norm"]],
                             [(D_MODEL, BF16), (2 * KV_W, F32), (KV_W, F32), (KV_W, F32)], tm=256, name=nm + "_proj")
    return k, v, (u, kv_raw)


def kv_bwd(h, p, saved, dk_cur, dk_prev, dv_cur, dv_prev, dout, nm, io):
    u, kv_raw = saved
    t_dim = h.shape[0]
    tm = ATT_WINDOW
    nb = t_dim // tm
    nxt = pl.BlockSpec((tm, KV_W), lambda i: (jnp.minimum(i + 1, nb - 1), 0))

    def fn(dkc, dkp, dvc, dvp, t, gg):
        live = pl.program_id(0) < nb - 1
        dk = dkc + jnp.where(live, dkp, 0.0)
        dv = dvc + jnp.where(live, dvp, 0.0)
        outs, dgs = [], None
        for j in range(ATT_KV_HEADS):
            sl = slice(j * ATT_HEAD_DIM, (j + 1) * ATT_HEAD_DIM)
            dx, dg = _rms_bwd(t[:, sl], gg, dk[:, sl])
            outs.append(dx)
            dgs = dg if dgs is None else dgs + dg
        return jnp.concatenate(outs + [dv], axis=1), dgs

    dkv_raw, dknorm = rowmap(fn, [dk_cur, (dk_prev, nxt), dv_cur, (dv_prev, nxt), kv_raw], [p["k_norm"]],
                             [(2 * KV_W, BF16)], [(1, ATT_HEAD_DIM)], tm=tm, name=nm + "_dknorm",
                             hook=io.hook(nm + "_dknorm"))
    g = {"k_norm": dknorm}
    io.put("w_kv", mm(u, dkv_raw, ta=True, out_dtype=BF16, name=nm + "_dwkv"))
    dh, dh_bf, g["kv_norm"] = du_norm_bwd([(dkv_raw, p["w_kv"], True)], h, p["kv_norm"], dout, nm + "_du",
                                          hook=io.hook(nm + "_du"))
    return dh, dh_bf, g


def _attn_scores(q_ref, kp_ref, kc_ref, vp_ref, vc_ref, qn_ref, bias_ref, sink_ref, kv):
    hd = ATT_HEAD_DIM
    blk = ATT_WINDOW
    sl = slice(kv * hd, (kv + 1) * hd)
    kk = jnp.concatenate([kp_ref[:, sl], kc_ref[:, sl]], axis=0)
    vv = jnp.concatenate([vp_ref[:, sl], vc_ref[:, sl]], axis=0)
    gq = qn_ref[...]
    raws, rinvs = [], []
    for r in range(ATT_GROUP):
        hh = kv * ATT_GROUP + r
        x = q_ref[:, hh * hd:(hh + 1) * hd]
        raws.append(x)
        rinvs.append(lax.rsqrt(jnp.mean(x * x, axis=-1, keepdims=True) + EPS))
    xh = jnp.concatenate([x * ri for x, ri in zip(raws, rinvs)], axis=0)
    rinv = jnp.concatenate(rinvs, axis=0)
    q8 = xh * gq
    s = _dot(q8, kk, NT) * (hd ** -0.5) + bias_ref[kv]
    colk = lax.broadcasted_iota(jnp.int32, (1, 2 * blk), 1)
    s = jnp.where((pl.program_id(0) > 0) | (colk >= blk), s, NEG)
    sink = sink_ref[kv]
    m = jnp.maximum(jnp.max(s, axis=-1, keepdims=True), sink)
    pexp = jnp.exp(s - m)
    e_sink = jnp.exp(sink - m)
    inv_den = 1.0 / (jnp.sum(pexp, axis=-1, keepdims=True) + e_sink)
    return kk, vv, xh, rinv, q8, pexp * inv_den, e_sink * inv_den


def _attn_specs(nb):
    blk = ATT_WINDOW
    cur = lambda i: (i, 0)
    prev = lambda i: (jnp.maximum(i - 1, 0), 0)
    return [
        pl.BlockSpec((blk, D_MODEL), cur),
        pl.BlockSpec((blk, KV_W), prev), pl.BlockSpec((blk, KV_W), cur),
        pl.BlockSpec((blk, KV_W), prev), pl.BlockSpec((blk, KV_W), cur),
        pl.BlockSpec((1, ATT_HEAD_DIM), lambda i: (0, 0)),
        pl.BlockSpec((ATT_KV_HEADS, ATT_GROUP * blk, 2 * blk), lambda i: (0, 0, 0)),
        pl.BlockSpec((ATT_KV_HEADS, ATT_GROUP * blk, 1), lambda i: (0, 0, 0)),
    ]


def attn_fwd(q_raw, k, v, q_norm, bias, sink_col, nm):
    t_dim = q_raw.shape[0]
    blk, hd = ATT_WINDOW, ATT_HEAD_DIM
    nb = t_dim // blk

    def body(q_ref, kp_ref, kc_ref, vp_ref, vc_ref, qn_ref, bias_ref, sink_ref, o_ref):
        for kv in range(ATT_KV_HEADS):
            kk, vv, xh, rinv, q8, prob, p_sink = _attn_scores(q_ref, kp_ref, kc_ref, vp_ref, vc_ref, qn_ref,
                                                              bias_ref, sink_ref, kv)
            o8 = _dot(prob, vv)
            for r in range(ATT_GROUP):
                hh = kv * ATT_GROUP + r
                o_ref[:, hh * hd:(hh + 1) * hd] = o8[r * blk:(r + 1) * blk].astype(o_ref.dtype)

    out, = pcall(body, name=nm, grid=(nb,), in_specs=_attn_specs(nb),
                 out_specs=[pl.BlockSpec((blk, D_MODEL), lambda i: (i, 0))],
                 out_shape=[jax.ShapeDtypeStruct((t_dim, D_MODEL), BF16)],
                 args=[q_raw, k, k, v, v, q_norm, bias, sink_col])
    return out


def attn_bwd(do, q_raw, k, v, q_norm, bias, sink_col, nm, hook=None):
    t_dim = q_raw.shape[0]
    blk, hd = ATT_WINDOW, ATT_HEAD_DIM
    nb = t_dim // blk
    scale = hd ** -0.5

    def body(do_ref, q_ref, kp_ref, kc_ref, vp_ref, vc_ref, qn_ref, bias_ref, sink_ref,
             dq_ref, dkc_ref, dkp_ref, dvc_ref, dvp_ref, dbias_ref, dsink_ref, dqn_ref):
        @pl.when(pl.program_id(0) == 0)
        def _():
            dbias_ref[...] = jnp.zeros_like(dbias_ref)
            dsink_ref[...] = jnp.zeros_like(dsink_ref)
            dqn_ref[...] = jnp.zeros_like(dqn_ref)

        gq = qn_ref[...]
        for kv in range(ATT_KV_HEADS):
            kk, vv, xh, rinv, q8, prob, p_sink = _attn_scores(q_ref, kp_ref, kc_ref, vp_ref, vc_ref, qn_ref,
                                                              bias_ref, sink_ref, kv)
            do8 = jnp.concatenate([do_ref[:, (kv * ATT_GROUP + r) * hd:(kv * ATT_GROUP + r + 1) * hd]
                                   for r in range(ATT_GROUP)], axis=0)
            dp = _dot(do8, vv, NT)
            delta = jnp.sum(prob * dp, axis=-1, keepdims=True)
            ds = prob * (dp - delta)
            dsink_ref[kv] += -p_sink * delta
            dbias_ref[kv] += ds
            ds_s = ds * scale
            dq8 = _dot(ds_s, kk)
            dkk = _dot(ds_s, q8, TN)
            dvv = _dot(prob, do8, TN)
            dqn_ref[...] += jnp.sum(dq8 * xh, axis=0, keepdims=True)
            dxh = dq8 * gq
            dq_raw8 = rinv * (dxh - xh * jnp.mean(dxh * xh, axis=-1, keepdims=True))
            for r in range(ATT_GROUP):
                hh = kv * ATT_GROUP + r
                dq_ref[:, hh * hd:(hh + 1) * hd] = dq_raw8[r * blk:(r + 1) * blk].astype(dq_ref.dtype)
            sl = slice(kv * hd, (kv + 1) * hd)
            dkp_ref[:, sl] = dkk[:blk]
            dkc_ref[:, sl] = dkk[blk:]
            dvp_ref[:, sl] = dvv[:blk]
            dvc_ref[:, sl] = dvv[blk:]

    cur = lambda i: (i, 0)
    row_spec = pl.BlockSpec((blk, KV_W), cur)
    out_specs = [
        pl.BlockSpec((blk, D_MODEL), cur), row_spec, row_spec, row_spec, row_spec,
        pl.BlockSpec((ATT_KV_HEADS, ATT_GROUP * blk, 2 * blk), lambda i: (0, 0, 0)),
        pl.BlockSpec((ATT_KV_HEADS, ATT_GROUP * blk, 1), lambda i: (0, 0, 0)),
        pl.BlockSpec((1, hd), lambda i: (0, 0)),
    ]
    kvs = jax.ShapeDtypeStruct((t_dim, KV_W), F32)
    out_shape = [
        jax.ShapeDtypeStruct((t_dim, D_MODEL), BF16), kvs, kvs, kvs, kvs,
        jax.ShapeDtypeStruct((ATT_KV_HEADS, ATT_GROUP * blk, 2 * blk), F32),
        jax.ShapeDtypeStruct((ATT_KV_HEADS, ATT_GROUP * blk, 1), F32),
        jax.ShapeDtypeStruct((1, hd), F32),
    ]
    return pcall(body, name=nm, grid=(nb,), in_specs=[pl.BlockSpec((blk, D_MODEL), cur)] + _attn_specs(nb),
                 out_specs=out_specs, out_shape=out_shape,
                 args=[do, q_raw, k, k, v, v, q_norm, bias, sink_col], hook=hook)


def _t5_bucket_np():
    blk = ATT_WINDOW
    qi = np.arange(blk)[:, None] + blk
    kj = np.arange(2 * blk)[None, :]
    dist = qi - kj
    n = np.maximum(dist, 0)
    max_exact = REL_BUCKETS // 2
    nf = np.maximum(n, 1).astype(np.float32)
    large = max_exact + (np.log(nf / max_exact) / math.log(ATT_WINDOW / max_exact)
                         * (REL_BUCKETS - max_exact)).astype(np.int32)
    large = np.minimum(large, REL_BUCKETS - 1)
    bucket = np.where(n < max_exact, n, large)
    in_window = (dist >= 0) & (dist < ATT_WINDOW)
    return bucket, in_window


def attn_block_fwd(h, k, v, p, nm):
    def q_proj(x, gg, w_q):
        uu = _rms_fwd(x, gg).astype(BF16)
        return uu, _dot(uu, w_q)

    u, q_raw = rowmap(q_proj, [h], [p["attn_norm"], p["w_q"]], [(D_MODEL, BF16), (D_MODEL, F32)], tm=256,
                      name=nm + "_q")
    o = attn_fwd(q_raw, k, v, p["q_norm"], p["bias"], p["sink_col"], nm + "_core")
    out = mm(o, p["w_o"], res=h, name=nm + "_o")
    return out, (u, q_raw, o)


def attn_block_bwd(h, k, v, p, saved, dout, dout_bf, nm, io):
    u, q_raw, o = saved
    g = {}
    io.put("w_o", mm(o, dout_bf, ta=True, out_dtype=BF16, name=nm + "_dwo", hook=io.hook(nm + "_dwo")))
    do = mm(dout_bf, p["w_o"], tb=True, name=nm + "_do")
    dq_raw, dkc, dkp, dvc, dvp, g["bias"], g["sink_col"], g["q_norm"] = attn_bwd(
        do, q_raw, k, v, p["q_norm"], p["bias"], p["sink_col"], nm + "_dcore", hook=io.hook(nm + "_dcore"))
    io.put("w_q", mm(u, dq_raw, ta=True, out_dtype=BF16, name=nm + "_dwq"))
    dh, dh_bf, g["attn_norm"] = du_norm_bwd([(dq_raw, p["w_q"], True)], h, p["attn_norm"], dout, nm + "_du")
    return dh, dh_bf, g, (dkc, dkp, dvc, dvp)


FFN_TAGS = ["00", "01", "10", "11"]


def local_step(x, target, small, io):
    bucket, in_window = _t5_bucket_np()
    blk = ATT_WINDOW
    w = small

    fnorm = {tag: w["ffn_norm"][int(tag[0]), int(tag[1])][None, :] for tag in FFN_TAGS}
    a_neg = -jnp.exp(w["ssm_a_log"][0])

    def mamba_p():
        w_int = io.w("w_int")
        return dict(ssm_norm=w["ssm_norm"], w_zt=w_int[:SSM_D_INNER],
                    w_xbct=w_int[SSM_D_INNER:SSM_D_INNER + SSM_CONV_DIM], w_dtt=w_int[SSM_D_INNER + SSM_CONV_DIM:],
                    conv_w=w["ssm_conv_w"][0], conv_b=w["ssm_conv_b"], dt_bias=w["ssm_dt_bias"],
                    a_heads=a_neg[None, :], a_row=a_neg.reshape(SSM_GROUPS, 1, HPG),
                    a_col=a_neg.reshape(SSM_GROUPS, HPG, 1),
                    dvec=jnp.repeat(w["ssm_d"][0], SSM_HEAD_DIM)[None, :],
                    gate_norm=w["ssm_gate_norm"], w_out=io.w("w_out"))

    rb = w["rel_bias"]
    onehot3 = (np.arange(REL_BUCKETS)[:, None, None] == bucket[None]).astype(np.float32)
    bias = jnp.einsum("bh,bqk->hqk", rb, onehot3, precision=lax.Precision.HIGHEST)
    bias = jnp.where(in_window[None], bias, NEG)
    bias = bias.reshape(ATT_KV_HEADS, ATT_GROUP * blk, 2 * blk)
    sink_col = jnp.repeat(w["sinks"][0], blk).reshape(ATT_KV_HEADS, ATT_GROUP * blk, 1)

    def attn_p():
        return dict(attn_norm=w["attn_norm"], w_q=io.w("w_q"), q_norm=w["q_norm"], bias=bias, sink_col=sink_col,
                    w_o=io.w("w_o"))

    def kv_p():
        return dict(kv_norm=w["kv_norm"][None, :], w_kv=io.w("w_kv"), k_norm=w["k_norm"][None, :])

    h0 = x
    h0a, s_f00 = ffn_fwd(h0, fnorm["00"], "00", io)
    mp = mamba_p()
    h0b, s_m = mamba_fwd(h0a, mp, "ssm", io)
    h1, s_f01 = ffn_fwd(h0b, fnorm["01"], "01", io)
    kp = kv_p()
    k, v, s_kv = kv_fwd(h1, kp, "kv")
    h1a, s_f10 = ffn_fwd(h1, fnorm["10"], "10", io)
    ap = attn_p()
    h1b, s_a = attn_block_fwd(h1a, k, v, ap, "att")
    (dh, dh_bf, sq), s_f11 = ffn_fwd(h1b, fnorm["11"], "11", io, target=target)
    loss_part = jnp.sum(sq) * (0.5 / D_MODEL)

    fg = {}

    def ffn_back(tag, h_in, saved, dh, dh_bf):
        dh, dh_bf, dg = ffn_bwd(h_in, fnorm[tag], tag, saved, dh, dh_bf, io)
        fg[tag] = dg[0]
        return dh, dh_bf

    dh, dh_bf = ffn_back("11", h1b, s_f11, dh, dh_bf)
    dh, dh_bf, ga, dkv = attn_block_bwd(h1a, k, v, ap, s_a, dh, dh_bf, "att", io)
    dh, dh_bf = ffn_back("10", h1, s_f10, dh, dh_bf)
    dh, dh_bf, gk = kv_bwd(h1, kp, s_kv, *dkv, dh, "kv", io)
    dh, dh_bf = ffn_back("01", h0b, s_f01, dh, dh_bf)
    dh, dh_bf, gm = mamba_bwd(h0a, mp, s_m, dh, dh_bf, "ssm", io)
    dh, dh_bf = ffn_back("00", h0, s_f00, dh, dh_bf)
    grad_x = dh

    grads = {}
    grads["ffn_norm"] = jnp.stack([fg[tag] for tag in FFN_TAGS]).reshape(2, 2, D_MODEL)
    grads["ssm_norm"] = gm["ssm_norm"]
    grads["ssm_conv_w"] = gm["conv_w"][None]
    grads["ssm_conv_b"] = gm["conv_b"]
    grads["ssm_dt_bias"] = gm["dt_bias"]
    grads["ssm_a_log"] = gm["a"] * a_neg[None, :]
    grads["ssm_d"] = jnp.sum(gm["dvec"].reshape(SSM_HEADS, SSM_HEAD_DIM), axis=1)[None, :]
    grads["ssm_gate_norm"] = gm["gate_norm"]
    grads["kv_norm"] = gk["kv_norm"][0]
    grads["k_norm"] = gk["k_norm"][0]
    grads["attn_norm"] = ga["attn_norm"]
    grads["q_norm"] = ga["q_norm"]
    grads["sinks"] = jnp.sum(ga["sink_col"].reshape(ATT_HEADS, blk), axis=1)[None, :]
    onehot = (np.arange(REL_BUCKETS)[:, None] == bucket.reshape(1, -1)).astype(np.float32)
    dbias2d = ga["bias"].reshape(ATT_HEADS, blk * 2 * blk)
    grads["rel_bias"] = mm(jnp.asarray(onehot, BF16), dbias2d, tb=True, name="drelbias")
    return loss_part, grad_x, grads


def _adamw(g, w, m, v):
    m = ADAM_B1 * m + (1.0 - ADAM_B1) * g
    v = ADAM_B2 * v + (1.0 - ADAM_B2) * (g * g)
    m_hat = m / (1.0 - ADAM_B1 ** ADAM_STEP)
    v_hat = v / (1.0 - ADAM_B2 ** ADAM_STEP)
    delta = -ADAM_LR * (m_hat / (jnp.sqrt(v_hat) + ADAM_EPS) + ADAM_WD * w)
    return delta, m, v


def _slot_sum(r):
    g = r[0].astype(F32)
    for d in range(1, r.shape[0]):
        g = g + r[d].astype(F32)
    return g


def adamw_rows(recvs, w, m, v, name):
    n_l, rows, width = w.shape
    n_slots = recvs[0].shape[0]
    tr = 32
    assert rows % tr == 0, rows
    nt = rows // tr

    def body(*refs):
        r_refs = refs[:n_l]
        w_ref, m_ref, v_ref, g_o, d_o, m_o, v_o = refs[n_l:]
        li = pl.program_id(0)
        for k in range(n_l):
            @pl.when(li == k)
            def _(k=k):
                g = _slot_sum(r_refs[k])
                delta, m2, v2 = _adamw(g, w_ref[0], m_ref[0], v_ref[0])
                g_o[0] = g
                d_o[0] = delta
                m_o[0] = m2
                v_o[0] = v2

    def r_spec(k):
        return pl.BlockSpec((n_slots, tr, width),
                            lambda li, j: (0, jnp.where(li == k, j, jnp.where(li > k, nt - 1, 0)), 0))

    w_spec = pl.BlockSpec((1, tr, width), lambda li, j: (li, j, 0))
    shp = jax.ShapeDtypeStruct(w.shape, F32)
    return pcall(body, name=name, grid=(n_l, nt), in_specs=[r_spec(k) for k in range(n_l)] + [w_spec] * 3,
                 out_specs=[w_spec] * 4, out_shape=[shp] * 4, args=list(recvs) + [w, m, v])


def adamw_cols(recvs, w, m, v, name):
    n_l, rows, n = w.shape
    n_slots = recvs[0].shape[0]
    tr = 256
    nt = rows // tr

    def body(*refs):
        r_refs = refs[:n_l]
        w_ref, m_ref, v_ref, g_o, d_o, m_o, v_o = refs[n_l:]
        li = pl.program_id(0)
        for k in range(n_l):
            @pl.when(li == k)
            def _(k=k):
                g = _slot_sum(r_refs[k]).T
                delta, m2, v2 = _adamw(g, w_ref[0], m_ref[0], v_ref[0])
                g_o[0] = g
                d_o[0] = delta
                m_o[0] = m2
                v_o[0] = v2

    def r_spec(k):
        return pl.BlockSpec((n_slots, n, tr),
                            lambda li, j: (0, 0, jnp.where(li == k, j, jnp.where(li > k, nt - 1, 0))))

    w_spec = pl.BlockSpec((1, tr, n), lambda li, j: (li, j, 0))
    shp = jax.ShapeDtypeStruct(w.shape, F32)
    return pcall(body, name=name, grid=(n_l, nt), in_specs=[r_spec(k) for k in range(n_l)] + [w_spec] * 3,
                 out_specs=[w_spec] * 4, out_shape=[shp] * 4, args=list(recvs) + [w, m, v])


WEIGHT_NAMES = ["ffn_norm", "ffn_w1", "ffn_w3", "ffn_w2", "ssm_norm", "ssm_w_in", "ssm_conv_w", "ssm_conv_b",
                "ssm_dt_bias", "ssm_a_log", "ssm_d", "ssm_gate_norm", "ssm_w_out", "kv_norm", "w_kv", "k_norm",
                "attn_norm", "w_q", "q_norm", "sinks", "w_o", "rel_bias"]

SMALL = [
    ("ffn_norm", (2, 2, 1024), 2), ("ssm_norm", (1, 1024), 1), ("ssm_conv_w", (1, 4, 3072), 2),
    ("ssm_conv_b", (1, 3072), 1), ("ssm_gate_norm", (1, 2048), 1),
    ("ssm_dt_bias", (1, 32), None), ("ssm_a_log", (1, 32), None), ("ssm_d", (1, 32), None),
    ("kv_norm", (1024,), None), ("k_norm", (64,), None), ("attn_norm", (1, 1024), None),
    ("q_norm", (1, 64), None), ("sinks", (1, 16), None), ("rel_bias", (32, 16), None),
]
SMALL_W = 1024
SMALL_FULL_ROWS = 32
SMALL_LOCAL_ROWS = 48

MAT_GROUPS = {
    "f00_up": ["w1t_00", "w3t_00"], "f00_down": ["w2_00"], "f01": ["w1t_01", "w3t_01", "w2_01"],
    "f10": ["w1t_10", "w3t_10", "w2_10"], "f11": ["w1t_11", "w3t_11", "w2_11"],
    "ssm": ["w_int", "w_out"], "att": ["w_q", "w_o", "w_kv"],
    "f00_early": ["w2_00", "w1t_00"], "f00_late": ["w3t_00"],
}
FIRST_GATHER = "f00_up"
GATHER_PLAN = {"f00_upgate": ["f00_down", "ssm"], "ssm_in": ["f01"], "ssm_ssd": ["att", "f10"],
               "f01_upgate": ["f11"]}
SCATTER_A_PLAN = {"att_dwo": "f11", "kv_dknorm": "f10", "kv_du": "att", "f01_du": "f01", "ssm_du": "ssm",
                  "f00_dw3": "f00_early", "f00_du": "f00_late"}
SCATTER_B_PLAN = {"att_dcore": "f11", "f01_dw2": "att", "f01_dgate": "f10", "ssm_dssd": "f01", "f00_dgate": "ssm",
                  "f00_du": "f00_early"}
LAST_SCATTER = "f00_late"
SLOT_MAJOR = ("w_int",)


def _shard_shape(s, a):
    return s[:a] + (s[a] // N_DEV,) + s[a + 1:]


def _unshard_view(stack, shard_shape, axis):
    moved = jnp.moveaxis(stack, 0, axis)
    return moved.reshape(shard_shape[:axis] + (N_DEV * shard_shape[axis],) + shard_shape[axis + 1:])


def _small_local(arrs):
    flat = jnp.concatenate([arrs[n].reshape(-1) for n, _, _ in SMALL])
    return jnp.pad(flat, (0, SMALL_LOCAL_ROWS * LANES - flat.shape[0])).reshape(SMALL_LOCAL_ROWS, LANES)


def chip_partial(g4, ra, name):
    _, _, n, width = g4.shape

    def body(g_ref, r_ref, o_ref):
        core = lax.axis_index("c")
        own = g_ref[0, pl.ds(core, 1)]
        o_ref[0] = (own[0].astype(F32) + r_ref[0, 0].astype(F32)).astype(o_ref.dtype)

    out, = pcall(body, name=name, grid=(N_CHIPS,),
                 in_specs=[pl.BlockSpec((1, 2, n, width), lambda q: (q, 0, 0, 0)),
                           pl.BlockSpec((1, 1, n, width), lambda q: (q, 0, 0, 0))],
                 out_specs=[pl.BlockSpec((1, n, width), lambda q: (q, 0, 0))],
                 out_shape=[jax.ShapeDtypeStruct((N_CHIPS, n, width), g4.dtype)], args=[g4, ra])
    return out


class StepIO:
    def __init__(self, pieces):
        self.pieces = pieces
        self.full = {}
        self.grad = {}
        self.from_sibling = {}
        self.recv = {}

    def w(self, name):
        return self.full[name]

    def put(self, name, g):
        self.grad[name] = g

    def _by_chip_core(self, name):
        g = self.grad[name]
        return g.reshape((N_CHIPS, 2, g.shape[0] // N_DEV) + g.shape[1:])

    def gather_items(self, groups):
        names = [n for grp in groups for n in MAT_GROUPS[grp]]
        items = [("g2", self.pieces[n], None if n in SLOT_MAJOR else 0) for n in names]

        def sink(outs):
            for n, o in zip(names, outs):
                self.full[n] = o.reshape((-1,) + o.shape[2:]) if n in SLOT_MAJOR else o

        return items, sink

    def scatter_a_items(self, group):
        names = MAT_GROUPS[group]
        items = [("sa", self._by_chip_core(n), None) for n in names]

        def sink(outs):
            for n, o in zip(names, outs):
                self.from_sibling[n] = o

        return items, sink

    def scatter_b_items(self, group):
        names = MAT_GROUPS[group]
        items = [("sb", chip_partial(self._by_chip_core(n), self.from_sibling[n], "partial_" + n), None)
                 for n in names]

        def sink(outs):
            for n, o in zip(names, outs):
                self.recv[n] = o

        return items, sink

    def hook(self, site):
        parts = []
        if site in GATHER_PLAN:
            parts.append(self.gather_items(GATHER_PLAN[site]))
        if site in SCATTER_A_PLAN:
            parts.append(self.scatter_a_items(SCATTER_A_PLAN[site]))
        if site in SCATTER_B_PLAN:
            parts.append(self.scatter_b_items(SCATTER_B_PLAN[site]))
        if not parts:
            return None
        return combine_hooks(parts)


def combine_hooks(parts):
    items = [it for its, _ in parts for it in its]

    def sink(outs):
        p = 0
        for its, snk in parts:
            snk(outs[p:p + len(its)])
            p += len(its)

    return Comm(items), sink


def step(x, target, wts, ms, vs):
    me = _my_index()

    pieces = {}
    for li in range(2):
        for hi in range(2):
            tag = "%d%d" % (li, hi)
            pieces["w1t_" + tag] = wts["ffn_w1"][li, hi].T.astype(BF16)
            pieces["w3t_" + tag] = wts["ffn_w3"][li, hi].T.astype(BF16)
            pieces["w2_" + tag] = wts["ffn_w2"][li, hi].astype(BF16)
    pieces["w_int"] = wts["ssm_w_in"][0].T.astype(BF16)
    pieces["w_out"] = wts["ssm_w_out"][0].astype(BF16)
    pieces["w_kv"] = wts["w_kv"].astype(BF16)
    pieces["w_q"] = wts["w_q"][0].astype(BF16)
    pieces["w_o"] = wts["w_o"][0].astype(BF16)
    io = StepIO(pieces)

    small_sharded = [(n, s, a) for n, s, a in SMALL if a is not None]
    loc = jnp.concatenate([wts[n].reshape(-1) for n, _, _ in small_sharded])
    loc_rows = -(-loc.shape[0] // (8 * LANES)) * 8
    loc = jnp.pad(loc, (0, loc_rows * LANES - loc.shape[0])).reshape(loc_rows, LANES)
    got_small = []
    comm, sink = combine_hooks([io.gather_items([FIRST_GATHER]), ([("g", loc, None)], got_small.extend)])
    sink(comm_only(comm, "gather_first"))
    gath_small = got_small[0].reshape(N_DEV, -1)
    small = {}
    off = 0
    for n, s, a in small_sharded:
        shard = _shard_shape(s, a)
        cnt = int(np.prod(shard))
        small[n] = _unshard_view(gath_small[:, off:off + cnt].reshape((N_DEV,) + shard), shard, a)
        off += cnt
    for n, s, a in SMALL:
        if a is None:
            small[n] = wts[n]

    loss_part, grad_x, g_small_local = local_step(x[0], target[0], small, io)
    loss = lax.psum(loss_part, ("x", "y", "c"))

    small_flat = jnp.concatenate([g_small_local[n].reshape(-1) for n, _, _ in SMALL])
    small_buf = jnp.pad(small_flat, (0, SMALL_FULL_ROWS * SMALL_W - small_flat.shape[0]))
    small_buf = small_buf.reshape(SMALL_FULL_ROWS, SMALL_W)
    got_small = []
    comm, sink = combine_hooks([io.scatter_b_items(LAST_SCATTER), ([("g", small_buf, None)], got_small.extend)])
    sink(comm_only(comm, "exchange_last"))
    small_all = got_small[0]

    def sum_body(r_ref, o_ref):
        o_ref[...] = _slot_sum(r_ref)

    vmem = pl.BlockSpec(memory_space=pltpu.VMEM)
    small_sum, = pcall(sum_body, name="sum_small", grid=(), in_specs=[vmem], out_specs=[vmem],
                       out_shape=[jax.ShapeDtypeStruct((SMALL_FULL_ROWS, SMALL_W), F32)], args=[small_all])
    small_sum = small_sum.reshape(-1)
    g_small = {}
    off = 0
    for n, s, a in SMALL:
        cnt = int(np.prod(s))
        gfull = small_sum[off:off + cnt].reshape(s)
        off += cnt
        if a is None:
            g_small[n] = gfull
        else:
            width = s[a] // N_DEV
            g_small[n] = lax.dynamic_slice_in_dim(gfull, me * width, width, axis=a)

    out = {}

    def emit(name, res, shape):
        for kind, arr in zip(("grad", "delta", "new_m", "new_v"), res):
            out[kind + "_" + name] = arr.reshape(shape)

    for name, key in (("ffn_w1", "w1t_"), ("ffn_w3", "w3t_")):
        shp = wts[name].shape
        view = lambda t: t.reshape((4,) + shp[2:])
        res = adamw_cols([io.recv[key + tag] for tag in FFN_TAGS], view(wts[name]), view(ms[name]), view(vs[name]),
                         "adamw_" + name)
        emit(name, res, shp)
    shp = wts["ffn_w2"].shape
    view = lambda t: t.reshape((4,) + shp[2:])
    res = adamw_rows([io.recv["w2_" + tag] for tag in FFN_TAGS], view(wts["ffn_w2"]), view(ms["ffn_w2"]),
                     view(vs["ffn_w2"]), "adamw_ffn_w2")
    emit("ffn_w2", res, shp)
    res = adamw_cols([io.recv["w_int"]], wts["ssm_w_in"], ms["ssm_w_in"], vs["ssm_w_in"], "adamw_ssm_w_in")
    emit("ssm_w_in", res, wts["ssm_w_in"].shape)
    for name, key in (("ssm_w_out", "w_out"), ("w_kv", "w_kv"), ("w_q", "w_q"), ("w_o", "w_o")):
        shp = wts[name].shape
        view = lambda t: t.reshape((1,) + shp[-2:])
        res = adamw_rows([io.recv[key]], view(wts[name]), view(ms[name]), view(vs[name]), "adamw_" + name)
        emit(name, res, shp)

    res_s = rowmap(lambda gg, ww, mm_, vv: _adamw(gg, ww, mm_, vv),
                   [_small_local(g_small), _small_local(wts), _small_local(ms), _small_local(vs)], [],
                   [(LANES, F32)] * 3, tm=SMALL_LOCAL_ROWS, name="adamw_small")
    flat_s = [r.reshape(-1) for r in res_s]
    off = 0
    for n, s, a in SMALL:
        shard = s if a is None else _shard_shape(s, a)
        cnt = int(np.prod(shard))
        out["grad_" + n] = g_small[n]
        for kind, arr in zip(("delta", "new_m", "new_v"), flat_s):
            out[kind + "_" + n] = arr[off:off + cnt].reshape(shard)
        off += cnt
    out["loss"] = loss
    out["grad_x"] = grad_x[None]
    return out


def kernel(x, ffn_norm, ffn_w1, ffn_w3, ffn_w2, ssm_norm, ssm_w_in, ssm_conv_w, ssm_conv_b, ssm_dt_bias, ssm_a_log, ssm_d, ssm_gate_norm, ssm_w_out, kv_norm, w_kv, k_norm, attn_norm, w_q, q_norm, sinks, w_o, rel_bias, loss_target, m_ffn_norm, m_ffn_w1, m_ffn_w3, m_ffn_w2, m_ssm_norm, m_ssm_w_in, m_ssm_conv_w, m_ssm_conv_b, m_ssm_dt_bias, m_ssm_a_log, m_ssm_d, m_ssm_gate_norm, m_ssm_w_out, m_kv_norm, m_w_kv, m_k_norm, m_attn_norm, m_w_q, m_q_norm, m_sinks, m_w_o, m_rel_bias, v_ffn_norm, v_ffn_w1, v_ffn_w3, v_ffn_w2, v_ssm_norm, v_ssm_w_in, v_ssm_conv_w, v_ssm_conv_b, v_ssm_dt_bias, v_ssm_a_log, v_ssm_d, v_ssm_gate_norm, v_ssm_w_out, v_kv_norm, v_w_kv, v_k_norm, v_attn_norm, v_w_q, v_q_norm, v_sinks, v_w_o, v_rel_bias):
    args = locals()
    wts = {n: args[n] for n in WEIGHT_NAMES}
    ms = {n: args["m_" + n] for n in WEIGHT_NAMES}
    vs = {n: args["v_" + n] for n in WEIGHT_NAMES}
    out = step(x, loss_target, wts, ms, vs)
    result = [out["loss"], out["grad_x"]]
    for kind in ("grad", "delta", "new_m", "new_v"):
        result += [out[kind + "_" + n] for n in WEIGHT_NAMES]
    return tuple(result)
```

```python
import functools
import math
import operator

import numpy as np
import jax
import jax.numpy as jnp
from jax import lax
from jax.experimental import pallas as pl
from jax.experimental.pallas import tpu as pltpu

F32 = jnp.float32
BF16 = jnp.bfloat16

D_MODEL = 1024
D_FF = 2816
N_DEV = 8
SSM_D_INNER = 2048
SSM_HEAD_DIM = 64
SSM_HEADS = 32
SSM_GROUPS = 4
SSM_STATE = 128
SSM_CONV = 4
SSM_CHUNK = 256
SSM_CONV_DIM = SSM_D_INNER + 2 * SSM_GROUPS * SSM_STATE
SSM_IN_DIM = SSM_D_INNER + SSM_CONV_DIM + SSM_HEADS
ATT_HEAD_DIM = 64
ATT_HEADS = 16
ATT_KV_HEADS = 2
ATT_GROUP = 8
ATT_WINDOW = 128
REL_BUCKETS = 32
EPS = 1e-6
NEG = -1e30

ADAM_LR = 0.001
ADAM_B1 = 0.9
ADAM_B2 = 0.999
ADAM_EPS = 1e-08
ADAM_WD = 0.01
ADAM_STEP = 10

VMEM_LIMIT_BYTES = 52 * 1024 * 1024
LANES = 128
MESH_ID = pl.DeviceIdType.MESH
ANY_SPEC = pl.BlockSpec(memory_space=pl.ANY)

NT = (((1,), (1,)), ((), ()))
TN = (((0,), (0,)), ((), ()))
NN = (((1,), (0,)), ((), ()))


def _pick(dim, cands):
    for c in cands:
        if dim % c == 0:
            return c
    return dim


def _my_index():
    return 4 * lax.axis_index("x") + 2 * lax.axis_index("y") + lax.axis_index("c")


def _peer(k):
    x, y, c = lax.axis_index("x"), lax.axis_index("y"), lax.axis_index("c")
    px = 1 - x if (k >> 2) & 1 else x
    py = 1 - y if (k >> 1) & 1 else y
    pc = 1 - c if k & 1 else c
    return (px, py, pc), 4 * px + 2 * py + pc


def _piece(ref, axis, d, n):
    if axis is None:
        return ref.at[d]
    return ref.at[(slice(None),) * axis + (pl.ds(pl.multiple_of(d * n, 8), n),)]


SIBLING = 1
CHIP_PEERS = (4, 2, 6)
N_CHIPS = 4
SEMS_PER_ITEM = N_DEV - 1


def _my_chip():
    return 2 * lax.axis_index("x") + lax.axis_index("y")


class Comm:
    def __init__(self, items):
        self.items = list(items)

    def dst_shapes(self):
        out = []
        for kind, src, axis in self.items:
            s = tuple(src.shape)
            if kind == "g":
                shp = (N_DEV,) + s
            elif kind == "g2":
                shp = (N_DEV,) + s if axis is None else s[:axis] + (N_DEV * s[axis],) + s[axis + 1:]
            elif kind == "sa":
                shp = (s[0], 1) + s[2:]
            else:
                shp = s
            out.append(jax.ShapeDtypeStruct(shp, src.dtype))
        return out

    def scratch(self):
        n = len(self.items)
        return [pltpu.SemaphoreType.DMA((n * SEMS_PER_ITEM,)), pltpu.SemaphoreType.DMA((n * SEMS_PER_ITEM,)),
                pltpu.SemaphoreType.DMA((n,))]

    def _run(self, srcs, dsts, sems, starting):
        send_sems, recv_sems, local_sems = sems
        me = _my_index()
        core = lax.axis_index("c")
        chip = _my_chip()
        for i, (kind, src, axis) in enumerate(self.items):
            s_ref, d_ref = srcs[i], dsts[i]
            base = i * SEMS_PER_ITEM

            def rdma(src_ref, dst_ref, j, peer):
                return pltpu.make_async_remote_copy(
                    src_ref=src_ref, dst_ref=dst_ref, send_sem=send_sems.at[base + j], recv_sem=recv_sems.at[base + j],
                    device_id=peer, device_id_type=MESH_ID)

            if kind == "g":
                local = pltpu.make_async_copy(s_ref, d_ref.at[me], local_sems.at[i])
                outs = [rdma(s_ref, d_ref.at[me], k - 1, _peer(k)[0]) for k in range(1, N_DEV)]
                if starting:
                    local.start()
                    for cp in outs:
                        cp.start()
                else:
                    for k in range(1, N_DEV):
                        rdma(s_ref, d_ref.at[_peer(k)[1]], k - 1, _peer(k)[0]).wait_recv()
                    for cp in outs:
                        cp.wait_send()
                    local.wait()
            elif kind == "g2":
                n = None if axis is None else src.shape[axis]
                mine = _piece(d_ref, axis, me, n)
                sib = _peer(SIBLING)[0]
                local = pltpu.make_async_copy(s_ref, mine, local_sems.at[i])
                outs = [rdma(s_ref, mine, 0, sib)] + [rdma(s_ref, mine, 1 + j, _peer(k)[0])
                                                      for j, k in enumerate(CHIP_PEERS)]
                if starting:
                    local.start()
                    for cp in outs:
                        cp.start()
                else:
                    passed = []
                    for j, k in enumerate(CHIP_PEERS):
                        theirs = _piece(d_ref, axis, _peer(k)[1], n)
                        rdma(s_ref, theirs, 1 + j, _peer(k)[0]).wait_recv()
                        fwd = rdma(theirs, theirs, 4 + j, sib)
                        fwd.start()
                        passed.append(fwd)
                    rdma(s_ref, _piece(d_ref, axis, _peer(SIBLING)[1], n), 0, sib).wait_recv()
                    for j, k in enumerate(CHIP_PEERS):
                        rdma(s_ref, _piece(d_ref, axis, _peer(k ^ SIBLING)[1], n), 4 + j, sib).wait_recv()
                    for cp in outs + passed:
                        cp.wait_send()
                    local.wait()
            elif kind == "sa":
                cp = rdma(s_ref.at[(slice(None), pl.ds(1 - core, 1))], d_ref, 0, _peer(SIBLING)[0])
                if starting:
                    cp.start()
                else:
                    cp.wait_recv()
                    cp.wait_send()
            else:
                local = pltpu.make_async_copy(s_ref.at[chip], d_ref.at[chip], local_sems.at[i])
                outs = [rdma(s_ref.at[_peer(k)[1] >> 1], d_ref.at[chip], 1 + j, _peer(k)[0])
                        for j, k in enumerate(CHIP_PEERS)]
                if starting:
                    local.start()
                    for cp in outs:
                        cp.start()
                else:
                    for j, k in enumerate(CHIP_PEERS):
                        rdma(s_ref.at[chip], d_ref.at[_peer(k)[1] >> 1], 1 + j, _peer(k)[0]).wait_recv()
                    for cp in outs:
                        cp.wait_send()
                    local.wait()

    def start(self, srcs, dsts, sems):
        self._run(srcs, dsts, sems, True)

    def wait(self, srcs, dsts, sems):
        self._run(srcs, dsts, sems, False)


def pcall(body, *, name, grid, in_specs, out_specs, out_shape, args, scratch=(), hook=None):
    cparams = pltpu.CompilerParams(dimension_semantics=("arbitrary",) * len(grid), vmem_limit_bytes=VMEM_LIMIT_BYTES)
    if hook is None:
        outs = pl.pallas_call(body, name=name, grid=grid, in_specs=list(in_specs), out_specs=list(out_specs),
                              out_shape=list(out_shape), scratch_shapes=list(scratch), compiler_params=cparams)(*args)
        return list(outs)
    comm, sink = hook
    n_in, n_out, n_scr, n_it = len(args), len(out_shape), len(scratch), len(comm.items)
    dims = tuple(grid)

    def wrapped(*refs):
        p = 0
        ins = refs[p:p + n_in]
        p += n_in
        csrc = refs[p:p + n_it]
        p += n_it
        outs = refs[p:p + n_out]
        p += n_out
        cdst = refs[p:p + n_it]
        p += n_it
        scr = refs[p:p + n_scr]
        p += n_scr
        sems = refs[p:p + 3]
        if dims:
            ids = [pl.program_id(a) for a in range(len(dims))]
            first = functools.reduce(operator.and_, [i == 0 for i in ids])
            last = functools.reduce(operator.and_, [i == d - 1 for i, d in zip(ids, dims)])

            @pl.when(first)
            def _():
                comm.start(csrc, cdst, sems)

            body(*ins, *outs, *scr)

            @pl.when(last)
            def _():
                comm.wait(csrc, cdst, sems)
        else:
            comm.start(csrc, cdst, sems)
            body(*ins, *outs, *scr)
            comm.wait(csrc, cdst, sems)

    res = pl.pallas_call(
        wrapped, name=name, grid=grid,
        in_specs=list(in_specs) + [ANY_SPEC] * n_it, out_specs=list(out_specs) + [ANY_SPEC] * n_it,
        out_shape=list(out_shape) + comm.dst_shapes(), scratch_shapes=list(scratch) + comm.scratch(),
        compiler_params=cparams,
    )(*args, *[src for _, src, _ in comm.items])
    res = list(res)
    sink(res[n_out:])
    return res[:n_out]


def comm_only(comm, name):
    got = []
    pcall(lambda *refs: None, name=name, grid=(), in_specs=[], out_specs=[], out_shape=[], args=[],
          hook=(comm, got.extend))
    return got


def mm(a, b, *, ta=False, tb=False, out_dtype=F32, res=None, alpha=1.0, name, hook=None):
    if ta:
        k_dim, m_dim = a.shape
    else:
        m_dim, k_dim = a.shape
    if tb:
        n_dim, k2 = b.shape
    else:
        k2, n_dim = b.shape
    assert k_dim == k2, (a.shape, b.shape, ta, tb)
    tn = _pick(n_dim, (1024, 1408, 512, 256, 128))
    tm = _pick(m_dim, (1024, 1408, 512, 256, 128)) if tn <= 1024 else _pick(m_dim, (512, 256, 128))
    tk = _pick(k_dim, (1024, 512, 256, 128)) if ta else _pick(k_dim, (512, 1408, 256, 128))
    nk = k_dim // tk
    has_res = res is not None
    dn = (((0 if ta else 1,), (1 if tb else 0,)), ((), ()))

    def body(*refs):
        if has_res:
            a_ref, b_ref, r_ref, o_ref, acc_ref = refs
        else:
            a_ref, b_ref, o_ref, acc_ref = refs
        k = pl.program_id(2)

        @pl.when(k == 0)
        def _():
            acc_ref[...] = jnp.zeros_like(acc_ref)

        acc_ref[...] += lax.dot_general(a_ref[...].astype(BF16), b_ref[...].astype(BF16), dn,
                                        preferred_element_type=F32)

        @pl.when(k == nk - 1)
        def _():
            r = acc_ref[...]
            if alpha != 1.0:
                r = r * alpha
            if has_res:
                r = r_ref[...] + r
            o_ref[...] = r.astype(o_ref.dtype)

    a_spec = pl.BlockSpec((tk, tm), lambda i, j, k: (k, i)) if ta else pl.BlockSpec((tm, tk), lambda i, j, k: (i, k))
    b_spec = pl.BlockSpec((tn, tk), lambda i, j, k: (j, k)) if tb else pl.BlockSpec((tk, tn), lambda i, j, k: (k, j))
    o_spec = pl.BlockSpec((tm, tn), lambda i, j, k: (i, j))
    in_specs = [a_spec, b_spec] + ([o_spec] if has_res else [])
    args = [a, b] + ([res] if has_res else [])
    out, = pcall(body, name=name, grid=(m_dim // tm, n_dim // tn, nk), in_specs=in_specs, out_specs=[o_spec],
                 out_shape=[jax.ShapeDtypeStruct((m_dim, n_dim), out_dtype)], args=args,
                 scratch=[pltpu.VMEM((tm, tn), F32)], hook=hook)
    return out


def rowmap(fn, rows, consts=(), out_rows=(), out_accs=(), *, tm, name, hook=None):
    first = rows[0][0] if isinstance(rows[0], tuple) else rows[0]
    t_dim = first.shape[0]
    assert t_dim % tm == 0, (t_dim, tm)
    n_r, n_c, n_o = len(rows), len(consts), len(out_rows)

    def body(*refs):
        ins = [r[...] for r in refs[:n_r + n_c]]
        o_refs = refs[n_r + n_c:]
        outs = tuple(fn(*ins))
        for o_ref, val in zip(o_refs[:n_o], outs[:n_o]):
            o_ref[...] = val.astype(o_ref.dtype)
        if out_accs:
            @pl.when(pl.program_id(0) == 0)
            def _():
                for o_ref in o_refs[n_o:]:
                    o_ref[...] = jnp.zeros_like(o_ref)

            for o_ref, val in zip(o_refs[n_o:], outs[n_o:]):
                o_ref[...] += val

    in_specs, args = [], []
    for r in rows:
        if isinstance(r, tuple):
            args.append(r[0])
            in_specs.append(r[1])
        else:
            args.append(r)
            in_specs.append(pl.BlockSpec((tm, r.shape[1]), lambda i: (i, 0)))
    for c in consts:
        args.append(c)
        in_specs.append(pl.BlockSpec(c.shape, lambda i, nd=c.ndim: (0,) * nd))
    out_specs = [pl.BlockSpec((tm, w), lambda i: (i, 0)) for (w, _) in out_rows]
    out_specs += [pl.BlockSpec(s, lambda i, nd=len(s): (0,) * nd) for s in out_accs]
    out_shape = [jax.ShapeDtypeStruct((t_dim, w), dt) for (w, dt) in out_rows]
    out_shape += [jax.ShapeDtypeStruct(s, F32) for s in out_accs]
    return pcall(body, name=name, grid=(t_dim // tm,), in_specs=in_specs, out_specs=out_specs, out_shape=out_shape,
                 args=args, hook=hook)


def _rms_fwd(x, g):
    r = lax.rsqrt(jnp.mean(x * x, axis=-1, keepdims=True) + EPS)
    return x * r * g


def _rms_bwd(x, g, dy):
    r = lax.rsqrt(jnp.mean(x * x, axis=-1, keepdims=True) + EPS)
    xh = x * r
    dg = jnp.sum(dy * xh, axis=0, keepdims=True)
    dxh = dy * g
    dx = r * (dxh - xh * jnp.mean(dxh * xh, axis=-1, keepdims=True))
    return dx, dg


def _sigmoid(x):
    return 1.0 / (1.0 + jnp.exp(-x))


def _silu(x):
    return x * _sigmoid(x)


def _silu_grad(x):
    s = _sigmoid(x)
    return s * (1.0 + x * (1.0 - s))


def _split3(x):
    hi = x.astype(BF16)
    r1 = x - hi.astype(F32)
    mid = r1.astype(BF16)
    lo = (r1 - mid.astype(F32)).astype(BF16)
    return hi, mid, lo


def _dot(a, b, dn=NN):
    return lax.dot_general(a.astype(BF16), b.astype(BF16), dn, preferred_element_type=F32)


def _col_of(mat, h):
    lane = lax.broadcasted_iota(jnp.int32, mat.shape, 1)
    return jnp.sum(jnp.where(lane == h, mat, 0.0), axis=1, keepdims=True)


FFN_TN = 1408


def ffn_upgate(h, g, w1t, w3t, nm, hook=None):
    t_dim = h.shape[0]
    tm = _pick(t_dim, (512, 256, 128))
    tn = FFN_TN

    n_j = D_FF // tn
    u_w = D_MODEL // n_j

    def body(h_ref, g_ref, w1_ref, w3_ref, u_ref, a_ref, b_ref, hm_ref):
        uu = _rms_fwd(h_ref[...], g_ref[...]).astype(BF16)
        for j in range(n_j):
            @pl.when(pl.program_id(0) == j)
            def _(j=j):
                u_ref[...] = uu[:, j * u_w:(j + 1) * u_w]

        a = lax.dot_general(uu, w1_ref[...], NT, preferred_element_type=F32)
        b = lax.dot_general(uu, w3_ref[...], NT, preferred_element_type=F32)
        a_ref[...] = a.astype(a_ref.dtype)
        b_ref[...] = b.astype(b_ref.dtype)
        hm_ref[...] = (_silu(a) * b).astype(hm_ref.dtype)

    row_spec = pl.BlockSpec((tm, D_MODEL), lambda j, i: (i, 0))
    w_spec = pl.BlockSpec((tn, D_MODEL), lambda j, i: (j, 0))
    o_spec = pl.BlockSpec((tm, tn), lambda j, i: (i, j))
    o_shape = jax.ShapeDtypeStruct((t_dim, D_FF), BF16)
    return pcall(body, name=nm, grid=(D_FF // tn, t_dim // tm),
                 in_specs=[row_spec, pl.BlockSpec((1, D_MODEL), lambda j, i: (0, 0)), w_spec, w_spec],
                 out_specs=[pl.BlockSpec((tm, u_w), lambda j, i: (i, j))] + [o_spec] * 3,
                 out_shape=[jax.ShapeDtypeStruct((t_dim, D_MODEL), BF16)] + [o_shape] * 3,
                 args=[h, g, w1t, w3t], hook=hook)


def ffn_dgate(dout_bf, w2, a, b, nm, hook=None):
    t_dim = dout_bf.shape[0]
    tm = _pick(t_dim, (512, 256, 128))
    tn = FFN_TN

    def body(d_ref, w2_ref, a_ref, b_ref, da_ref, db_ref):
        dhm = 0.5 * lax.dot_general(d_ref[...], w2_ref[...], NT, preferred_element_type=F32)
        av = a_ref[...].astype(F32)
        bv = b_ref[...].astype(F32)
        sg = _sigmoid(av)
        da_ref[...] = (dhm * bv * (sg * (1.0 + av * (1.0 - sg)))).astype(da_ref.dtype)
        db_ref[...] = (dhm * (av * sg)).astype(db_ref.dtype)

    t_spec = pl.BlockSpec((tm, tn), lambda j, i: (i, j))
    o_shape = jax.ShapeDtypeStruct((t_dim, D_FF), BF16)
    return pcall(body, name=nm, grid=(D_FF // tn, t_dim // tm),
                 in_specs=[pl.BlockSpec((tm, D_MODEL), lambda j, i: (i, 0)),
                           pl.BlockSpec((tn, D_MODEL), lambda j, i: (j, 0)), t_spec, t_spec],
                 out_specs=[t_spec] * 2, out_shape=[o_shape] * 2, args=[dout_bf, w2, a, b], hook=hook)


def ffn_fwd(h, g, tag, io, target=None):
    nm = "f" + tag
    u, a, b, hm = ffn_upgate(h, g, io.w("w1t_" + tag), io.w("w3t_" + tag), nm + "_upgate",
                             hook=io.hook(nm + "_upgate"))
    if target is None:
        return mm(hm, io.w("w2_" + tag), res=h, alpha=0.5, name=nm + "_down"), (u, a, b, hm)

    def down_loss(hmv, hv, t, w2):
        e = hv + 0.5 * _dot(hmv, w2) - t
        d = e * (1.0 / D_MODEL)
        return d, d, jnp.sum(e * e, axis=0, keepdims=True)

    res = rowmap(down_loss, [hm, h, target], [io.w("w2_" + tag)], [(D_MODEL, F32), (D_MODEL, BF16)],
                 [(1, D_MODEL)], tm=256, name=nm + "_down_loss")
    return res, (u, a, b, hm)


def du_norm_bwd(pairs, h, g, dout, nm, hook=None):
    t_dim = h.shape[0]
    tm = 256
    n_p = len(pairs)

    def body(*refs):
        h_ref, d_ref, g_ref = refs[2 * n_p:2 * n_p + 3]
        dh_ref, dhb_ref, dg_ref = refs[2 * n_p + 3:]
        du = None
        for p, (_, _, tb) in enumerate(pairs):
            t = lax.dot_general(refs[2 * p][...].astype(BF16), refs[2 * p + 1][...].astype(BF16), NT if tb else NN,
                                preferred_element_type=F32)
            du = t if du is None else du + t
        dx, dg = _rms_bwd(h_ref[...], g_ref[...], du)
        dh = d_ref[...] + dx
        dh_ref[...] = dh
        dhb_ref[...] = dh.astype(dhb_ref.dtype)

        @pl.when(pl.program_id(0) == 0)
        def _():
            dg_ref[...] = jnp.zeros_like(dg_ref)

        dg_ref[...] += dg

    in_specs, args = [], []
    for a, b, _ in pairs:
        in_specs += [pl.BlockSpec((tm, a.shape[1]), lambda i: (i, 0)), pl.BlockSpec(b.shape, lambda i: (0, 0))]
        args += [a, b]
    row_spec = pl.BlockSpec((tm, D_MODEL), lambda i: (i, 0))
    vec_spec = pl.BlockSpec((1, D_MODEL), lambda i: (0, 0))
    return pcall(body, name=nm, grid=(t_dim // tm,), in_specs=in_specs + [row_spec, row_spec, vec_spec],
                 out_specs=[row_spec, row_spec, vec_spec],
                 out_shape=[jax.ShapeDtypeStruct((t_dim, D_MODEL), F32), jax.ShapeDtypeStruct((t_dim, D_MODEL), BF16),
                            jax.ShapeDtypeStruct((1, D_MODEL), F32)],
                 args=args + [h, dout, g], hook=hook)


def ffn_bwd(h, g, tag, saved, dout, dout_bf, io):
    nm = "f" + tag
    w1t, w3t, w2 = io.w("w1t_" + tag), io.w("w3t_" + tag), io.w("w2_" + tag)
    u, a, b, hm = saved
    io.put("w2_" + tag, mm(hm, dout_bf, ta=True, alpha=0.5, out_dtype=BF16, name=nm + "_dw2",
                           hook=io.hook(nm + "_dw2")))
    da, db = ffn_dgate(dout_bf, w2, a, b, nm + "_dgate", hook=io.hook(nm + "_dgate"))
    io.put("w1t_" + tag, mm(da, u, ta=True, out_dtype=BF16, name=nm + "_dw1"))
    io.put("w3t_" + tag, mm(db, u, ta=True, out_dtype=BF16, name=nm + "_dw3", hook=io.hook(nm + "_dw3")))
    return du_norm_bwd([(da, w1t, False), (db, w3t, False)], h, g, dout, nm + "_du", hook=io.hook(nm + "_du"))


def conv_input_grad(d_parts, w, nm):
    tm = 256
    t_dim = d_parts[0].shape[0]
    n_tiles = t_dim // tm

    def fn(d1, n1, d2, n2, d3, n3, ww):
        d = jnp.concatenate([d1, d2, d3], axis=1)
        nxt = jnp.concatenate([n1, n2, n3], axis=1)
        nxt = jnp.where(pl.program_id(0) < n_tiles - 1, nxt, 0.0)
        dd = jnp.concatenate([d, nxt], axis=0)
        out = dd[3:3 + tm] * ww[0:1]
        for k in range(1, SSM_CONV):
            out = out + dd[3 - k:3 - k + tm] * ww[k:k + 1]
        return (out,)

    rows = []
    for d in d_parts:
        below = pl.BlockSpec((8, d.shape[1]), lambda i: (jnp.minimum((i + 1) * (tm // 8), t_dim // 8 - 1), 0))
        rows += [d, (d, below)]
    dx, = rowmap(fn, rows, [w], [(SSM_CONV_DIM, BF16)], tm=tm, name=nm)
    return dx


GRP_W = SSM_D_INNER // SSM_GROUPS
HPG = SSM_HEADS // SSM_GROUPS
HEAD_SHIFT = 6


def _split2(x):
    hi = x.astype(BF16)
    return hi, (x - hi.astype(F32)).astype(BF16)


def _expand_mats():
    e = ((lax.broadcasted_iota(jnp.int32, (HPG, GRP_W), 1) >> HEAD_SHIFT)
         == lax.broadcasted_iota(jnp.int32, (HPG, GRP_W), 0)).astype(BF16)
    et = ((lax.broadcasted_iota(jnp.int32, (GRP_W, HPG), 0) >> HEAD_SHIFT)
          == lax.broadcasted_iota(jnp.int32, (GRP_W, HPG), 1)).astype(BF16)
    return e, et


def _expand(v, e_m):
    hi, lo = _split2(v)
    return jnp.dot(hi, e_m, preferred_element_type=F32) + jnp.dot(lo, e_m, preferred_element_type=F32)


def _reduce8(v, et_m):
    acc = None
    for p in _split3(v):
        t = jnp.dot(p, et_m, preferred_element_type=F32)
        acc = t if acc is None else acc + t
    return acc


def _ssd_group_terms(dt_ref, dtT_ref, arow_ref, acol_ref):
    L = SSM_CHUNK
    r = lax.broadcasted_iota(jnp.int32, (L, L), 0)
    c = lax.broadcasted_iota(jnp.int32, (L, L), 1)
    tril = (r >= c).astype(BF16)
    triu = (r <= c).astype(BF16)
    dtg = dt_ref[0]
    acol = None
    for p in _split3(dtg * arow_ref[0]):
        t = jnp.dot(tril, p, preferred_element_type=F32)
        acol = t if acol is None else acol + t
    arowT = None
    for p in _split3(dtT_ref[0] * acol_ref[0]):
        t = jnp.dot(p, triu, preferred_element_type=F32)
        arowT = t if arowT is None else arowT + t
    return dtg, acol, arowT, r >= c


def _state_decay(a_last_col, et_m):
    hi, lo = _split2(jnp.broadcast_to(jnp.exp(a_last_col), (HPG, SSM_STATE)))
    return jnp.dot(et_m, hi, preferred_element_type=F32) + jnp.dot(et_m, lo, preferred_element_type=F32)


def _conv_block(x_ref, halo_ref, w_ref, b_ref, first):
    L = SSM_CHUNK
    xx = jnp.concatenate([jnp.where(first, 0.0, halo_ref[...]), x_ref[...]], axis=0)
    w = w_ref[...]
    shifted = [xx[5 + k:5 + k + L] for k in range(SSM_CONV)]
    acc = b_ref[...] + shifted[0] * w[0:1]
    for k in range(1, SSM_CONV):
        acc = acc + shifted[k] * w[k:k + 1]
    return acc, shifted


def _ssd_specs(nc, rev):
    L, N = SSM_CHUNK, SSM_STATE
    xcols = SSM_D_INNER // LANES
    ch = (lambda c: nc - 1 - c) if rev else (lambda c: c)
    above = lambda c: jnp.maximum(ch(c) * (L // 8) - 1, 0)
    specs = []
    for width, col in ((GRP_W, lambda g: g), (N, lambda g: xcols + g), (N, lambda g: xcols + SSM_GROUPS + g)):
        specs += [
            pl.BlockSpec((L, width), lambda c, g, col=col: (ch(c), col(g))),
            pl.BlockSpec((8, width), lambda c, g, col=col: (above(c), col(g))),
            pl.BlockSpec((SSM_CONV, width), lambda c, g, col=col: (0, col(g))),
            pl.BlockSpec((1, width), lambda c, g, col=col: (0, col(g))),
        ]
    return specs + [
        pl.BlockSpec((1, L, HPG), lambda c, g: (g, ch(c), 0)),
        pl.BlockSpec((1, HPG, L), lambda c, g: (g, 0, ch(c))),
        pl.BlockSpec((1, 1, HPG), lambda c, g: (g, 0, 0)),
        pl.BlockSpec((1, HPG, 1), lambda c, g: (g, 0, 0)),
        pl.BlockSpec((1, GRP_W), lambda c, g: (0, g)),
    ]


def ssd_fwd(xbc_raw, conv_w, conv_b, dt_g, dtT_g, a_row, a_col, dvec, nm, hook=None):
    t_dim = xbc_raw.shape[0]
    L, P, N = SSM_CHUNK, SSM_HEAD_DIM, SSM_STATE
    nc = t_dim // L

    def body(x_ref, xh_ref, xw_ref, xb_ref, b_ref, bh_ref, bw_ref, bb_ref, c_ref, ch_ref, cw_ref, cb_ref,
             dt_ref, dtT_ref, arow_ref, acol_ref, dvec_ref, y_ref, st_ref, s_s):
        ci = pl.program_id(0)
        g = pl.program_id(1)

        @pl.when((ci == 0) & (g == 0))
        def _():
            s_s[...] = jnp.zeros_like(s_s)

        e_m, et_m = _expand_mats()
        dtg, acol, arowT, causal = _ssd_group_terms(dt_ref, dtT_ref, arow_ref, acol_ref)
        a_last_row = acol[L - 1:L, :]
        x = _silu(_conv_block(x_ref, xh_ref, xw_ref, xb_ref, ci == 0)[0])
        bm = _silu(_conv_block(b_ref, bh_ref, bw_ref, bb_ref, ci == 0)[0])
        cm = _silu(_conv_block(c_ref, ch_ref, cw_ref, cb_ref, ci == 0)[0])
        cb = _dot(cm, bm, NT)
        s = s_s[g]
        st_ref[0, 0] = s
        ea_x = _expand(jnp.exp(acol), e_m)
        dt_x = _expand(dtg, e_m)
        w_x = _expand(jnp.exp(a_last_row - acol) * dtg, e_m)
        yb = ea_x * _dot(cm, s, NT) + dvec_ref[...] * x
        xd = (x * dt_x).astype(BF16)
        for e in range(HPG):
            sl = slice(e * P, (e + 1) * P)
            lm = jnp.exp(jnp.where(causal, acol[:, e:e + 1] - arowT[e:e + 1, :], NEG))
            m = (cb * lm).astype(BF16)
            y_ref[:, sl] = yb[:, sl] + jnp.dot(m, xd[:, sl], preferred_element_type=F32)
        s_s[g] = _state_decay(arowT[:, L - 1:L], et_m) * s + _dot(x * w_x, bm, TN)

    out_specs = [
        pl.BlockSpec((L, GRP_W), lambda c, g: (c, g)),
        pl.BlockSpec((1, 1, GRP_W, N), lambda c, g: (c, g, 0, 0)),
    ]
    return pcall(
        body, name=nm, grid=(nc, SSM_GROUPS), in_specs=_ssd_specs(nc, False), out_specs=out_specs,
        out_shape=[jax.ShapeDtypeStruct((t_dim, SSM_D_INNER), F32),
                   jax.ShapeDtypeStruct((nc, SSM_GROUPS, GRP_W, N), F32)],
        scratch=[pltpu.VMEM((SSM_GROUPS, GRP_W, N), F32)],
        args=[xbc_raw, xbc_raw, conv_w, conv_b] * 3 + [dt_g, dtT_g, a_row, a_col, dvec], hook=hook)


def ssd_bwd(dy, xbc_raw, conv_w, conv_b, dt_g, dtT_g, a_row, a_col, dvec, states, nm, hook=None):
    t_dim = xbc_raw.shape[0]
    L, P, N = SSM_CHUNK, SSM_HEAD_DIM, SSM_STATE
    nc = t_dim // L

    def body(dy_ref, x_ref, xh_ref, xw_ref, xb_ref, b_ref, bh_ref, bw_ref, bb_ref, c_ref, ch_ref, cw_ref, cb_ref,
             dt_ref, dtT_ref, arow_ref, acol_ref, dvec_ref, st_ref,
             dx_ref, db_ref, dc_ref, da_ref, ddt_ref, dd_ref, dwx_ref, dwb_ref, dwc_ref, dbx_ref, dbb_ref, dbc_ref,
             ds_s, yd_s, dxd_s):
        ci = pl.program_id(0)
        g = pl.program_id(1)

        @pl.when((ci == 0) & (g == 0))
        def _():
            ds_s[...] = jnp.zeros_like(ds_s)
            for r in (dd_ref, dwx_ref, dwb_ref, dwc_ref, dbx_ref, dbb_ref, dbc_ref):
                r[...] = jnp.zeros_like(r)

        e_m, et_m = _expand_mats()
        dtg, acol, arowT, causal = _ssd_group_terms(dt_ref, dtT_ref, arow_ref, acol_ref)
        a_last_row = acol[L - 1:L, :]
        first = ci == nc - 1
        pre_x, sh_x = _conv_block(x_ref, xh_ref, xw_ref, xb_ref, first)
        pre_b, sh_b = _conv_block(b_ref, bh_ref, bw_ref, bb_ref, first)
        pre_c, sh_c = _conv_block(c_ref, ch_ref, cw_ref, cb_ref, first)
        sg_x, sg_b, sg_c = _sigmoid(pre_x), _sigmoid(pre_b), _sigmoid(pre_c)
        x = pre_x * sg_x
        dy = dy_ref[...]
        bm = pre_b * sg_b
        cm = pre_c * sg_c
        cb = _dot(cm, bm, NT)
        s = st_ref[0, 0]
        dsp = ds_s[g]
        ew8 = jnp.exp(a_last_row - acol)
        ea_x = _expand(jnp.exp(acol), e_m)
        dt_x = _expand(dtg, e_m)
        ew_x = _expand(ew8, e_m)
        w_x = ew_x * dt_x
        z = _dot(cm, s, NT)
        dz = ea_x * dy
        dc = _dot(dz, s)
        ds_y = _dot(dz, cm, TN)
        du = _dot(bm, dsp, NT)
        u = x * w_x
        db = _dot(u, dsp)
        xd = (x * dt_x).astype(BF16)
        dyb = dy.astype(BF16)
        dcb = jnp.zeros((L, L), F32)
        for e in range(HPG):
            sl = slice(e * P, (e + 1) * P)
            lm = jnp.exp(jnp.where(causal, acol[:, e:e + 1] - arowT[e:e + 1, :], NEG))
            m = (cb * lm).astype(BF16)
            yd_s[:, sl] = jnp.dot(m, xd[:, sl], preferred_element_type=F32)
            dxd_s[:, sl] = lax.dot_general(m, dyb[:, sl], TN, preferred_element_type=F32)
            dcb = dcb + lax.dot_general(dyb[:, sl], xd[:, sl], NT, preferred_element_type=F32) * lm
        dxd = dxd_s[...]

        def through_conv(d_act, pre, sg, shifted, d_ref, dw_ref, dbias_ref):
            d_pre = d_act * (sg * (1.0 + pre * (1.0 - sg)))
            d_ref[...] = d_pre
            dw_ref[g] += jnp.concatenate([jnp.sum(d_pre * sh, axis=0, keepdims=True) for sh in shifted], axis=0)
            dbias_ref[g] += jnp.sum(d_pre, axis=0, keepdims=True)

        through_conv(dvec_ref[...] * dy + du * w_x + dt_x * dxd, pre_x, sg_x, sh_x, dx_ref, dwx_ref, dbx_ref)
        ddt = _reduce8(x * (ew_x * du + dxd), et_m)
        da = (_reduce8(dz * z + dyb.astype(F32) * yd_s[...], et_m)
              - _reduce8(xd.astype(F32) * dxd + du * u, et_m))
        dwa_row = _reduce8(jnp.broadcast_to(jnp.sum(du * u, axis=0, keepdims=True), (8, GRP_W)), et_m)[0:1]
        t_nh = None
        for p in _split3(dsp * s):
            t = lax.dot_general(p, et_m, TN, preferred_element_type=F32)
            t_nh = t if t_nh is None else t_nh + t
        d_last = dwa_row + jnp.exp(a_last_row) * jnp.sum(t_nh, axis=0, keepdims=True)
        row_l = lax.broadcasted_iota(jnp.int32, (L, 1), 0)
        da_ref[0] = da + jnp.where(row_l == L - 1, d_last, 0.0)
        ddt_ref[0] = ddt
        dd_ref[g] += jnp.sum(dy * x, axis=0, keepdims=True)
        through_conv(dc + _dot(dcb, bm), pre_c, sg_c, sh_c, dc_ref, dwc_ref, dbc_ref)
        through_conv(db + _dot(dcb, cm, TN), pre_b, sg_b, sh_b, db_ref, dwb_ref, dbb_ref)
        ds_s[g] = _state_decay(arowT[:, L - 1:L], et_m) * dsp + ds_y

    rc = lambda c: nc - 1 - c
    in_specs = ([pl.BlockSpec((L, GRP_W), lambda c, g: (rc(c), g))] + _ssd_specs(nc, True)
                + [pl.BlockSpec((1, 1, GRP_W, N), lambda c, g: (rc(c), g, 0, 0))])
    whole = lambda *shape: pl.BlockSpec(shape, lambda c, g: (0,) * len(shape))
    out_specs = [
        pl.BlockSpec((L, GRP_W), lambda c, g: (rc(c), g)),
        pl.BlockSpec((L, N), lambda c, g: (rc(c), g)),
        pl.BlockSpec((L, N), lambda c, g: (rc(c), g)),
        pl.BlockSpec((1, L, HPG), lambda c, g: (g, rc(c), 0)),
        pl.BlockSpec((1, L, HPG), lambda c, g: (g, rc(c), 0)),
        whole(SSM_GROUPS, 1, GRP_W),
        whole(SSM_GROUPS, SSM_CONV, GRP_W), whole(SSM_GROUPS, SSM_CONV, N), whole(SSM_GROUPS, SSM_CONV, N),
        whole(SSM_GROUPS, 1, GRP_W), whole(SSM_GROUPS, 1, N), whole(SSM_GROUPS, 1, N),
    ]
    gn = SSM_GROUPS * N
    acc = lambda *shape: jax.ShapeDtypeStruct(shape, F32)
    out_shape = [
        acc(t_dim, SSM_D_INNER), acc(t_dim, gn), acc(t_dim, gn), acc(SSM_GROUPS, t_dim, HPG),
        acc(SSM_GROUPS, t_dim, HPG), acc(SSM_GROUPS, 1, GRP_W),
        acc(SSM_GROUPS, SSM_CONV, GRP_W), acc(SSM_GROUPS, SSM_CONV, N), acc(SSM_GROUPS, SSM_CONV, N),
        acc(SSM_GROUPS, 1, GRP_W), acc(SSM_GROUPS, 1, N), acc(SSM_GROUPS, 1, N),
    ]
    return pcall(
        body, name=nm, grid=(nc, SSM_GROUPS), in_specs=in_specs, out_specs=out_specs, out_shape=out_shape,
        scratch=[pltpu.VMEM((SSM_GROUPS, GRP_W, N), F32), pltpu.VMEM((L, GRP_W), F32), pltpu.VMEM((L, GRP_W), F32)],
        args=[dy] + [xbc_raw, xbc_raw, conv_w, conv_b] * 3 + [dt_g, dtT_g, a_row, a_col, dvec, states], hook=hook)


def _softplus(x):
    return jnp.maximum(x, 0.0) + jnp.log(1.0 + jnp.exp(-jnp.abs(x)))


def ssd_dt_bwd(da, ddt, dt, dt_raw, a_row, dt_bias, nm):
    L = SSM_CHUNK

    def fn(d_a, d_dt, dtv, raw, ar, bias):
        r = lax.broadcasted_iota(jnp.int32, (L, L), 0)
        c = lax.broadcasted_iota(jnp.int32, (L, L), 1)
        triu = (r <= c).astype(BF16)
        acc = None
        for p in _split3(d_a):
            t = jnp.dot(triu, p, preferred_element_type=F32)
            acc = t if acc is None else acc + t
        d_dt = d_dt + acc * ar
        d_a_h = jnp.sum(acc * dtv, axis=0, keepdims=True)
        d_raw = d_dt * _sigmoid(raw + bias)
        return d_raw, d_a_h, jnp.sum(d_raw, axis=0, keepdims=True)

    return rowmap(fn, [da, ddt, dt, dt_raw], [a_row, dt_bias], [(SSM_HEADS, BF16)],
                  [(1, SSM_HEADS), (1, SSM_HEADS)], tm=L, name=nm)


GN_W = SSM_D_INNER // SSM_GROUPS


def mamba_fwd(h, p, nm, io):
    def in_proj(x, gg, w_zt, w_xbct, w_dtt):
        uu = _rms_fwd(x, gg).astype(BF16)
        return uu, _dot(uu, w_zt, NT), _dot(uu, w_xbct, NT), _dot(uu, w_dtt, NT)

    u, z, xbc_raw, dt_raw = rowmap(in_proj, [h], [p["ssm_norm"], p["w_zt"], p["w_xbct"], p["w_dtt"]],
                                   [(D_MODEL, BF16), (SSM_D_INNER, F32), (SSM_CONV_DIM, F32), (SSM_HEADS, F32)],
                                   tm=256, name=nm + "_in", hook=io.hook(nm + "_in"))
    dt, = rowmap(lambda r, b: (_softplus(r + b),), [dt_raw], [p["dt_bias"]], [(SSM_HEADS, F32)], tm=256,
                 name=nm + "_softplus")
    dt_g = dt.reshape(-1, SSM_GROUPS, HPG).transpose(1, 0, 2)
    dtT_g = dt_g.transpose(0, 2, 1)
    y, states = ssd_fwd(xbc_raw, p["conv_w"], p["conv_b"], dt_g, dtT_g, p["a_row"], p["a_col"], p["dvec"],
                        nm + "_ssd", hook=io.hook(nm + "_ssd"))

    def gate_norm_out(yv, zv, hv, gg, w_out):
        t = yv * _silu(zv)
        yn = jnp.concatenate([_rms_fwd(t[:, k * GN_W:(k + 1) * GN_W], gg[:, k * GN_W:(k + 1) * GN_W])
                              for k in range(SSM_GROUPS)], axis=1).astype(BF16)
        return yn, hv + _dot(yn, w_out)

    yn, out = rowmap(gate_norm_out, [y, z, h], [p["gate_norm"], p["w_out"]],
                     [(SSM_D_INNER, BF16), (D_MODEL, F32)], tm=256, name=nm + "_out")
    return out, (u, z, xbc_raw, dt_raw, dt, dt_g, dtT_g, y, states, yn)


def mamba_bwd(h, p, saved, dout, dout_bf, nm, io):
    u, z, xbc_raw, dt_raw, dt, dt_g, dtT_g, y, states, yn = saved
    g = {}
    io.put("w_out", mm(yn, dout_bf, ta=True, out_dtype=BF16, name=nm + "_dwout"))

    def gate_norm_bwd(d_o, yv, zv, gg, w_out):
        d = _dot(d_o, w_out, NT)
        sz = _silu(zv)
        t = yv * sz
        dts, dgs = [], []
        for k in range(SSM_GROUPS):
            sl = slice(k * GN_W, (k + 1) * GN_W)
            dt_k, dg_k = _rms_bwd(t[:, sl], gg[:, sl], d[:, sl])
            dts.append(dt_k)
            dgs.append(dg_k)
        d_t = jnp.concatenate(dts, axis=1)
        return d_t * sz, d_t * yv * _silu_grad(zv), jnp.concatenate(dgs, axis=1)

    dy, dz, g["gate_norm"] = rowmap(gate_norm_bwd, [dout_bf, y, z], [p["gate_norm"], p["w_out"]],
                                    [(SSM_D_INNER, F32), (SSM_D_INNER, BF16)], [(1, SSM_D_INNER)], tm=256,
                                    name=nm + "_dgatenorm")
    d_x, d_b, d_c, da_g, ddt_g, dd, dwx, dwb, dwc, dbx, dbb, dbc = ssd_bwd(
        dy, xbc_raw, p["conv_w"], p["conv_b"], dt_g, dtT_g, p["a_row"], p["a_col"], p["dvec"], states, nm + "_dssd",
        hook=io.hook(nm + "_dssd"))
    g["dvec"] = dd
    by_lane = lambda t: t.transpose(1, 0, 2).reshape(t.shape[1], -1)
    g["conv_w"] = jnp.concatenate([by_lane(dwx), by_lane(dwb), by_lane(dwc)], axis=1)
    g["conv_b"] = jnp.concatenate([by_lane(dbx), by_lane(dbb), by_lane(dbc)], axis=1)
    per_head = lambda t: t.transpose(1, 0, 2).reshape(-1, SSM_HEADS)
    ddt_raw, g["a"], g["dt_bias"] = ssd_dt_bwd(per_head(da_g), per_head(ddt_g), dt, dt_raw, p["a_heads"],
                                               p["dt_bias"], nm + "_ddt")
    dxbc_raw = conv_input_grad([d_x, d_b, d_c], p["conv_w"], nm + "_dconv")
    io.put("w_int", jnp.concatenate([mm(dz, u, ta=True, out_dtype=BF16, name=nm + "_dwz"),
                                     mm(dxbc_raw, u, ta=True, out_dtype=BF16, name=nm + "_dwxbc"),
                                     mm(ddt_raw, u, ta=True, out_dtype=BF16, name=nm + "_dwdt")], axis=0))
    dh, dh_bf, g["ssm_norm"] = du_norm_bwd(
        [(dz, p["w_zt"], False), (dxbc_raw, p["w_xbct"], False), (ddt_raw, p["w_dtt"], False)],
        h, p["ssm_norm"], dout, nm + "_du", hook=io.hook(nm + "_du"))
    return dh, dh_bf, g


KV_W = ATT_KV_HEADS * ATT_HEAD_DIM


def kv_fwd(h, p, nm):
    def kv_proj(x, gg, w_kv, gk):
        uu = _rms_fwd(x, gg).astype(BF16)
        t = _dot(uu, w_kv)
        ks = [_rms_fwd(t[:, j * ATT_HEAD_DIM:(j + 1) * ATT_HEAD_DIM], gk) for j in range(ATT_KV_HEADS)]
        return uu, t, jnp.concatenate(ks, axis=1), t[:, KV_W:]

    u, kv_raw, k, v = rowmap(kv_proj, [h], [p["kv_norm"], p["w_kv"], p["k_norm"]],
                             [(D_MODEL, BF16), (2 * KV_W, F32), (KV_W, F32), (KV_W, F32)], tm=256, name=nm + "_proj")
    return k, v, (u, kv_raw)


def kv_bwd(h, p, saved, dk_cur, dk_prev, dv_cur, dv_prev, dout, nm, io):
    u, kv_raw = saved
    t_dim = h.shape[0]
    tm = ATT_WINDOW
    nb = t_dim // tm
    nxt = pl.BlockSpec((tm, KV_W), lambda i: (jnp.minimum(i + 1, nb - 1), 0))

    def fn(dkc, dkp, dvc, dvp, t, gg):
        live = pl.program_id(0) < nb - 1
        dk = dkc + jnp.where(live, dkp, 0.0)
        dv = dvc + jnp.where(live, dvp, 0.0)
        outs, dgs = [], None
        for j in range(ATT_KV_HEADS):
            sl = slice(j * ATT_HEAD_DIM, (j + 1) * ATT_HEAD_DIM)
            dx, dg = _rms_bwd(t[:, sl], gg, dk[:, sl])
            outs.append(dx)
            dgs = dg if dgs is None else dgs + dg
        return jnp.concatenate(outs + [dv], axis=1), dgs

    dkv_raw, dknorm = rowmap(fn, [dk_cur, (dk_prev, nxt), dv_cur, (dv_prev, nxt), kv_raw], [p["k_norm"]],
                             [(2 * KV_W, BF16)], [(1, ATT_HEAD_DIM)], tm=tm, name=nm + "_dknorm",
                             hook=io.hook(nm + "_dknorm"))
    g = {"k_norm": dknorm}
    io.put("w_kv", mm(u, dkv_raw, ta=True, out_dtype=BF16, name=nm + "_dwkv"))
    dh, dh_bf, g["kv_norm"] = du_norm_bwd([(dkv_raw, p["w_kv"], True)], h, p["kv_norm"], dout, nm + "_du",
                                          hook=io.hook(nm + "_du"))
    return dh, dh_bf, g


def _attn_scores(q_ref, kp_ref, kc_ref, vp_ref, vc_ref, qn_ref, bias_ref, sink_ref, kv):
    hd = ATT_HEAD_DIM
    blk = ATT_WINDOW
    sl = slice(kv * hd, (kv + 1) * hd)
    kk = jnp.concatenate([kp_ref[:, sl], kc_ref[:, sl]], axis=0)
    vv = jnp.concatenate([vp_ref[:, sl], vc_ref[:, sl]], axis=0)
    gq = qn_ref[...]
    raws, rinvs = [], []
    for r in range(ATT_GROUP):
        hh = kv * ATT_GROUP + r
        x = q_ref[:, hh * hd:(hh + 1) * hd]
        raws.append(x)
        rinvs.append(lax.rsqrt(jnp.mean(x * x, axis=-1, keepdims=True) + EPS))
    xh = jnp.concatenate([x * ri for x, ri in zip(raws, rinvs)], axis=0)
    rinv = jnp.concatenate(rinvs, axis=0)
    q8 = xh * gq
    s = _dot(q8, kk, NT) * (hd ** -0.5) + bias_ref[kv]
    colk = lax.broadcasted_iota(jnp.int32, (1, 2 * blk), 1)
    s = jnp.where((pl.program_id(0) > 0) | (colk >= blk), s, NEG)
    sink = sink_ref[kv]
    m = jnp.maximum(jnp.max(s, axis=-1, keepdims=True), sink)
    pexp = jnp.exp(s - m)
    e_sink = jnp.exp(sink - m)
    inv_den = 1.0 / (jnp.sum(pexp, axis=-1, keepdims=True) + e_sink)
    return kk, vv, xh, rinv, q8, pexp * inv_den, e_sink * inv_den


def _attn_specs(nb):
    blk = ATT_WINDOW
    cur = lambda i: (i, 0)
    prev = lambda i: (jnp.maximum(i - 1, 0), 0)
    return [
        pl.BlockSpec((blk, D_MODEL), cur),
        pl.BlockSpec((blk, KV_W), prev), pl.BlockSpec((blk, KV_W), cur),
        pl.BlockSpec((blk, KV_W), prev), pl.BlockSpec((blk, KV_W), cur),
        pl.BlockSpec((1, ATT_HEAD_DIM), lambda i: (0, 0)),
        pl.BlockSpec((ATT_KV_HEADS, ATT_GROUP * blk, 2 * blk), lambda i: (0, 0, 0)),
        pl.BlockSpec((ATT_KV_HEADS, ATT_GROUP * blk, 1), lambda i: (0, 0, 0)),
    ]


def attn_fwd(q_raw, k, v, q_norm, bias, sink_col, nm):
    t_dim = q_raw.shape[0]
    blk, hd = ATT_WINDOW, ATT_HEAD_DIM
    nb = t_dim // blk

    def body(q_ref, kp_ref, kc_ref, vp_ref, vc_ref, qn_ref, bias_ref, sink_ref, o_ref):
        for kv in range(ATT_KV_HEADS):
            kk, vv, xh, rinv, q8, prob, p_sink = _attn_scores(q_ref, kp_ref, kc_ref, vp_ref, vc_ref, qn_ref,
                                                              bias_ref, sink_ref, kv)
            o8 = _dot(prob, vv)
            for r in range(ATT_GROUP):
                hh = kv * ATT_GROUP + r
                o_ref[:, hh * hd:(hh + 1) * hd] = o8[r * blk:(r + 1) * blk].astype(o_ref.dtype)

    out, = pcall(body, name=nm, grid=(nb,), in_specs=_attn_specs(nb),
                 out_specs=[pl.BlockSpec((blk, D_MODEL), lambda i: (i, 0))],
                 out_shape=[jax.ShapeDtypeStruct((t_dim, D_MODEL), BF16)],
                 args=[q_raw, k, k, v, v, q_norm, bias, sink_col])
    return out


def attn_bwd(do, q_raw, k, v, q_norm, bias, sink_col, nm, hook=None):
    t_dim = q_raw.shape[0]
    blk, hd = ATT_WINDOW, ATT_HEAD_DIM
    nb = t_dim // blk
    scale = hd ** -0.5

    def body(do_ref, q_ref, kp_ref, kc_ref, vp_ref, vc_ref, qn_ref, bias_ref, sink_ref,
             dq_ref, dkc_ref, dkp_ref, dvc_ref, dvp_ref, dbias_ref, dsink_ref, dqn_ref):
        @pl.when(pl.program_id(0) == 0)
        def _():
            dbias_ref[...] = jnp.zeros_like(dbias_ref)
            dsink_ref[...] = jnp.zeros_like(dsink_ref)
            dqn_ref[...] = jnp.zeros_like(dqn_ref)

        gq = qn_ref[...]
        for kv in range(ATT_KV_HEADS):
            kk, vv, xh, rinv, q8, prob, p_sink = _attn_scores(q_ref, kp_ref, kc_ref, vp_ref, vc_ref, qn_ref,
                                                              bias_ref, sink_ref, kv)
            do8 = jnp.concatenate([do_ref[:, (kv * ATT_GROUP + r) * hd:(kv * ATT_GROUP + r + 1) * hd]
                                   for r in range(ATT_GROUP)], axis=0)
            dp = _dot(do8, vv, NT)
            delta = jnp.sum(prob * dp, axis=-1, keepdims=True)
            ds = prob * (dp - delta)
            dsink_ref[kv] += -p_sink * delta
            dbias_ref[kv] += ds
            ds_s = ds * scale
            dq8 = _dot(ds_s, kk)
            dkk = _dot(ds_s, q8, TN)
            dvv = _dot(prob, do8, TN)
            dqn_ref[...] += jnp.sum(dq8 * xh, axis=0, keepdims=True)
            dxh = dq8 * gq
            dq_raw8 = rinv * (dxh - xh * jnp.mean(dxh * xh, axis=-1, keepdims=True))
            for r in range(ATT_GROUP):
                hh = kv * ATT_GROUP + r
                dq_ref[:, hh * hd:(hh + 1) * hd] = dq_raw8[r * blk:(r + 1) * blk].astype(dq_ref.dtype)
            sl = slice(kv * hd, (kv + 1) * hd)
            dkp_ref[:, sl] = dkk[:blk]
            dkc_ref[:, sl] = dkk[blk:]
            dvp_ref[:, sl] = dvv[:blk]
            dvc_ref[:, sl] = dvv[blk:]

    cur = lambda i: (i, 0)
    row_spec = pl.BlockSpec((blk, KV_W), cur)
    out_specs = [
        pl.BlockSpec((blk, D_MODEL), cur), row_spec, row_spec, row_spec, row_spec,
        pl.BlockSpec((ATT_KV_HEADS, ATT_GROUP * blk, 2 * blk), lambda i: (0, 0, 0)),
        pl.BlockSpec((ATT_KV_HEADS, ATT_GROUP * blk, 1), lambda i: (0, 0, 0)),
        pl.BlockSpec((1, hd), lambda i: (0, 0)),
    ]
    kvs = jax.ShapeDtypeStruct((t_dim, KV_W), F32)
    out_shape = [
        jax.ShapeDtypeStruct((t_dim, D_MODEL), BF16), kvs, kvs, kvs, kvs,
        jax.ShapeDtypeStruct((ATT_KV_HEADS, ATT_GROUP * blk, 2 * blk), F32),
        jax.ShapeDtypeStruct((ATT_KV_HEADS, ATT_GROUP * blk, 1), F32),
        jax.ShapeDtypeStruct((1, hd), F32),
    ]
    return pcall(body, name=nm, grid=(nb,), in_specs=[pl.BlockSpec((blk, D_MODEL), cur)] + _attn_specs(nb),
                 out_specs=out_specs, out_shape=out_shape,
                 args=[do, q_raw, k, k, v, v, q_norm, bias, sink_col], hook=hook)


def _t5_bucket_np():
    blk = ATT_WINDOW
    qi = np.arange(blk)[:, None] + blk
    kj = np.arange(2 * blk)[None, :]
    dist = qi - kj
    n = np.maximum(dist, 0)
    max_exact = REL_BUCKETS // 2
    nf = np.maximum(n, 1).astype(np.float32)
    large = max_exact + (np.log(nf / max_exact) / math.log(ATT_WINDOW / max_exact)
                         * (REL_BUCKETS - max_exact)).astype(np.int32)
    large = np.minimum(large, REL_BUCKETS - 1)
    bucket = np.where(n < max_exact, n, large)
    in_window = (dist >= 0) & (dist < ATT_WINDOW)
    return bucket, in_window


def attn_block_fwd(h, k, v, p, nm):
    def q_proj(x, gg, w_q):
        uu = _rms_fwd(x, gg).astype(BF16)
        return uu, _dot(uu, w_q)

    u, q_raw = rowmap(q_proj, [h], [p["attn_norm"], p["w_q"]], [(D_MODEL, BF16), (D_MODEL, F32)], tm=256,
                      name=nm + "_q")
    o = attn_fwd(q_raw, k, v, p["q_norm"], p["bias"], p["sink_col"], nm + "_core")
    out = mm(o, p["w_o"], res=h, name=nm + "_o")
    return out, (u, q_raw, o)


def attn_block_bwd(h, k, v, p, saved, dout, dout_bf, nm, io):
    u, q_raw, o = saved
    g = {}
    io.put("w_o", mm(o, dout_bf, ta=True, out_dtype=BF16, name=nm + "_dwo", hook=io.hook(nm + "_dwo")))
    do = mm(dout_bf, p["w_o"], tb=True, name=nm + "_do")
    dq_raw, dkc, dkp, dvc, dvp, g["bias"], g["sink_col"], g["q_norm"] = attn_bwd(
        do, q_raw, k, v, p["q_norm"], p["bias"], p["sink_col"], nm + "_dcore", hook=io.hook(nm + "_dcore"))
    io.put("w_q", mm(u, dq_raw, ta=True, out_dtype=BF16, name=nm + "_dwq"))
    dh, dh_bf, g["attn_norm"] = du_norm_bwd([(dq_raw, p["w_q"], True)], h, p["attn_norm"], dout, nm + "_du")
    return dh, dh_bf, g, (dkc, dkp, dvc, dvp)


FFN_TAGS = ["00", "01", "10", "11"]


def local_step(x, target, small, io):
    bucket, in_window = _t5_bucket_np()
    blk = ATT_WINDOW
    w = small

    fnorm = {tag: w["ffn_norm"][int(tag[0]), int(tag[1])][None, :] for tag in FFN_TAGS}
    a_neg = -jnp.exp(w["ssm_a_log"][0])

    def mamba_p():
        w_int = io.w("w_int")
        return dict(ssm_norm=w["ssm_norm"], w_zt=w_int[:SSM_D_INNER],
                    w_xbct=w_int[SSM_D_INNER:SSM_D_INNER + SSM_CONV_DIM], w_dtt=w_int[SSM_D_INNER + SSM_CONV_DIM:],
                    conv_w=w["ssm_conv_w"][0], conv_b=w["ssm_conv_b"], dt_bias=w["ssm_dt_bias"],
                    a_heads=a_neg[None, :], a_row=a_neg.reshape(SSM_GROUPS, 1, HPG),
                    a_col=a_neg.reshape(SSM_GROUPS, HPG, 1),
                    dvec=jnp.repeat(w["ssm_d"][0], SSM_HEAD_DIM)[None, :],
                    gate_norm=w["ssm_gate_norm"], w_out=io.w("w_out"))

    rb = w["rel_bias"]
    onehot3 = (np.arange(REL_BUCKETS)[:, None, None] == bucket[None]).astype(np.float32)
    bias = jnp.einsum("bh,bqk->hqk", rb, onehot3, precision=lax.Precision.HIGHEST)
    bias = jnp.where(in_window[None], bias, NEG)
    bias = bias.reshape(ATT_KV_HEADS, ATT_GROUP * blk, 2 * blk)
    sink_col = jnp.repeat(w["sinks"][0], blk).reshape(ATT_KV_HEADS, ATT_GROUP * blk, 1)

    def attn_p():
        return dict(attn_norm=w["attn_norm"], w_q=io.w("w_q"), q_norm=w["q_norm"], bias=bias, sink_col=sink_col,
                    w_o=io.w("w_o"))

    def kv_p():
        return dict(kv_norm=w["kv_norm"][None, :], w_kv=io.w("w_kv"), k_norm=w["k_norm"][None, :])

    h0 = x
    h0a, s_f00 = ffn_fwd(h0, fnorm["00"], "00", io)
    mp = mamba_p()
    h0b, s_m = mamba_fwd(h0a, mp, "ssm", io)
    h1, s_f01 = ffn_fwd(h0b, fnorm["01"], "01", io)
    kp = kv_p()
    k, v, s_kv = kv_fwd(h1, kp, "kv")
    h1a, s_f10 = ffn_fwd(h1, fnorm["10"], "10", io)
    ap = attn_p()
    h1b, s_a = attn_block_fwd(h1a, k, v, ap, "att")
    (dh, dh_bf, sq), s_f11 = ffn_fwd(h1b, fnorm["11"], "11", io, target=target)
    loss_part = jnp.sum(sq) * (0.5 / D_MODEL)

    fg = {}

    def ffn_back(tag, h_in, saved, dh, dh_bf):
        dh, dh_bf, dg = ffn_bwd(h_in, fnorm[tag], tag, saved, dh, dh_bf, io)
        fg[tag] = dg[0]
        return dh, dh_bf

    dh, dh_bf = ffn_back("11", h1b, s_f11, dh, dh_bf)
    dh, dh_bf, ga, dkv = attn_block_bwd(h1a, k, v, ap, s_a, dh, dh_bf, "att", io)
    dh, dh_bf = ffn_back("10", h1, s_f10, dh, dh_bf)
    dh, dh_bf, gk = kv_bwd(h1, kp, s_kv, *dkv, dh, "kv", io)
    dh, dh_bf = ffn_back("01", h0b, s_f01, dh, dh_bf)
    dh, dh_bf, gm = mamba_bwd(h0a, mp, s_m, dh, dh_bf, "ssm", io)
    dh, dh_bf = ffn_back("00", h0, s_f00, dh, dh_bf)
    grad_x = dh

    grads = {}
    grads["ffn_norm"] = jnp.stack([fg[tag] for tag in FFN_TAGS]).reshape(2, 2, D_MODEL)
    grads["ssm_norm"] = gm["ssm_norm"]
    grads["ssm_conv_w"] = gm["conv_w"][None]
    grads["ssm_conv_b"] = gm["conv_b"]
    grads["ssm_dt_bias"] = gm["dt_bias"]
    grads["ssm_a_log"] = gm["a"] * a_neg[None, :]
    grads["ssm_d"] = jnp.sum(gm["dvec"].reshape(SSM_HEADS, SSM_HEAD_DIM), axis=1)[None, :]
    grads["ssm_gate_norm"] = gm["gate_norm"]
    grads["kv_norm"] = gk["kv_norm"][0]
    grads["k_norm"] = gk["k_norm"][0]
    grads["attn_norm"] = ga["attn_norm"]
    grads["q_norm"] = ga["q_norm"]
    grads["sinks"] = jnp.sum(ga["sink_col"].reshape(ATT_HEADS, blk), axis=1)[None, :]
    onehot = (np.arange(REL_BUCKETS)[:, None] == bucket.reshape(1, -1)).astype(np.float32)
    dbias2d = ga["bias"].reshape(ATT_HEADS, blk * 2 * blk)
    grads["rel_bias"] = mm(jnp.asarray(onehot, BF16), dbias2d, tb=True, name="drelbias")
    return loss_part, grad_x, grads


def _adamw(g, w, m, v):
    m = ADAM_B1 * m + (1.0 - ADAM_B1) * g
    v = ADAM_B2 * v + (1.0 - ADAM_B2) * (g * g)
    m_hat = m / (1.0 - ADAM_B1 ** ADAM_STEP)
    v_hat = v / (1.0 - ADAM_B2 ** ADAM_STEP)
    delta = -ADAM_LR * (m_hat / (jnp.sqrt(v_hat) + ADAM_EPS) + ADAM_WD * w)
    return delta, m, v


def _slot_sum(r):
    g = r[0].astype(F32)
    for d in range(1, r.shape[0]):
        g = g + r[d].astype(F32)
    return g


def adamw_rows(recvs, w, m, v, name):
    n_l, rows, width = w.shape
    n_slots = recvs[0].shape[0]
    tr = 32
    assert rows % tr == 0, rows
    nt = rows // tr

    def body(*refs):
        r_refs = refs[:n_l]
        w_ref, m_ref, v_ref, g_o, d_o, m_o, v_o = refs[n_l:]
        li = pl.program_id(0)
        for k in range(n_l):
            @pl.when(li == k)
            def _(k=k):
                g = _slot_sum(r_refs[k])
                delta, m2, v2 = _adamw(g, w_ref[0], m_ref[0], v_ref[0])
                g_o[0] = g
                d_o[0] = delta
                m_o[0] = m2
                v_o[0] = v2

    def r_spec(k):
        return pl.BlockSpec((n_slots, tr, width),
                            lambda li, j: (0, jnp.where(li == k, j, jnp.where(li > k, nt - 1, 0)), 0))

    w_spec = pl.BlockSpec((1, tr, width), lambda li, j: (li, j, 0))
    shp = jax.ShapeDtypeStruct(w.shape, F32)
    return pcall(body, name=name, grid=(n_l, nt), in_specs=[r_spec(k) for k in range(n_l)] + [w_spec] * 3,
                 out_specs=[w_spec] * 4, out_shape=[shp] * 4, args=list(recvs) + [w, m, v])


def adamw_cols(recvs, w, m, v, name):
    n_l, rows, n = w.shape
    n_slots = recvs[0].shape[0]
    tr = 256
    nt = rows // tr

    def body(*refs):
        r_refs = refs[:n_l]
        w_ref, m_ref, v_ref, g_o, d_o, m_o, v_o = refs[n_l:]
        li = pl.program_id(0)
        for k in range(n_l):
            @pl.when(li == k)
            def _(k=k):
                g = _slot_sum(r_refs[k]).T
                delta, m2, v2 = _adamw(g, w_ref[0], m_ref[0], v_ref[0])
                g_o[0] = g
                d_o[0] = delta
                m_o[0] = m2
                v_o[0] = v2

    def r_spec(k):
        return pl.BlockSpec((n_slots, n, tr),
                            lambda li, j: (0, 0, jnp.where(li == k, j, jnp.where(li > k, nt - 1, 0))))

    w_spec = pl.BlockSpec((1, tr, n), lambda li, j: (li, j, 0))
    shp = jax.ShapeDtypeStruct(w.shape, F32)
    return pcall(body, name=name, grid=(n_l, nt), in_specs=[r_spec(k) for k in range(n_l)] + [w_spec] * 3,
                 out_specs=[w_spec] * 4, out_shape=[shp] * 4, args=list(recvs) + [w, m, v])


WEIGHT_NAMES = ["ffn_norm", "ffn_w1", "ffn_w3", "ffn_w2", "ssm_norm", "ssm_w_in", "ssm_conv_w", "ssm_conv_b",
                "ssm_dt_bias", "ssm_a_log", "ssm_d", "ssm_gate_norm", "ssm_w_out", "kv_norm", "w_kv", "k_norm",
                "attn_norm", "w_q", "q_norm", "sinks", "w_o", "rel_bias"]

SMALL = [
    ("ffn_norm", (2, 2, 1024), 2), ("ssm_norm", (1, 1024), 1), ("ssm_conv_w", (1, 4, 3072), 2),
    ("ssm_conv_b", (1, 3072), 1), ("ssm_gate_norm", (1, 2048), 1),
    ("ssm_dt_bias", (1, 32), None), ("ssm_a_log", (1, 32), None), ("ssm_d", (1, 32), None),
    ("kv_norm", (1024,), None), ("k_norm", (64,), None), ("attn_norm", (1, 1024), None),
    ("q_norm", (1, 64), None), ("sinks", (1, 16), None), ("rel_bias", (32, 16), None),
]
SMALL_W = 1024
SMALL_FULL_ROWS = 32
SMALL_LOCAL_ROWS = 48

MAT_GROUPS = {
    "f00_up": ["w1t_00", "w3t_00"], "f00_down": ["w2_00"], "f01": ["w1t_01", "w3t_01", "w2_01"],
    "f10": ["w1t_10", "w3t_10", "w2_10"], "f11": ["w1t_11", "w3t_11", "w2_11"],
    "ssm": ["w_int", "w_out"], "att": ["w_q", "w_o", "w_kv"],
    "f00_early": ["w2_00", "w1t_00"], "f00_late": ["w3t_00"],
}
FIRST_GATHER = "f00_up"
GATHER_PLAN = {"f00_upgate": ["f00_down", "ssm"], "ssm_in": ["f01"], "ssm_ssd": ["att", "f10"],
               "f01_upgate": ["f11"]}
SCATTER_A_PLAN = {"att_dwo": "f11", "kv_dknorm": "f10", "kv_du": "att", "f01_du": "f01", "ssm_du": "ssm",
                  "f00_dw3": "f00_early", "f00_du": "f00_late"}
SCATTER_B_PLAN = {"att_dcore": "f11", "f01_dw2": "att", "f01_dgate": "f10", "ssm_dssd": "f01", "f00_dgate": "ssm",
                  "f00_du": "f00_early"}
LAST_SCATTER = "f00_late"
SLOT_MAJOR = ("w_int",)


def _shard_shape(s, a):
    return s[:a] + (s[a] // N_DEV,) + s[a + 1:]


def _unshard_view(stack, shard_shape, axis):
    moved = jnp.moveaxis(stack, 0, axis)
    return moved.reshape(shard_shape[:axis] + (N_DEV * shard_shape[axis],) + shard_shape[axis + 1:])


def _small_local(arrs):
    flat = jnp.concatenate([arrs[n].reshape(-1) for n, _, _ in SMALL])
    return jnp.pad(flat, (0, SMALL_LOCAL_ROWS * LANES - flat.shape[0])).reshape(SMALL_LOCAL_ROWS, LANES)


def chip_partial(g4, ra, name):
    _, _, n, width = g4.shape

    def body(g_ref, r_ref, o_ref):
        core = lax.axis_index("c")
        own = g_ref[0, pl.ds(core, 1)]
        o_ref[0] = (own[0].astype(F32) + r_ref[0, 0].astype(F32)).astype(o_ref.dtype)

    out, = pcall(body, name=name, grid=(N_CHIPS,),
                 in_specs=[pl.BlockSpec((1, 2, n, width), lambda q: (q, 0, 0, 0)),
                           pl.BlockSpec((1, 1, n, width), lambda q: (q, 0, 0, 0))],
                 out_specs=[pl.BlockSpec((1, n, width), lambda q: (q, 0, 0))],
                 out_shape=[jax.ShapeDtypeStruct((N_CHIPS, n, width), g4.dtype)], args=[g4, ra])
    return out


class StepIO:
    def __init__(self, pieces):
        self.pieces = pieces
        self.full = {}
        self.grad = {}
        self.from_sibling = {}
        self.recv = {}

    def w(self, name):
        return self.full[name]

    def put(self, name, g):
        self.grad[name] = g

    def _by_chip_core(self, name):
        g = self.grad[name]
        return g.reshape((N_CHIPS, 2, g.shape[0] // N_DEV) + g.shape[1:])

    def gather_items(self, groups):
        names = [n for grp in groups for n in MAT_GROUPS[grp]]
        items = [("g2", self.pieces[n], None if n in SLOT_MAJOR else 0) for n in names]

        def sink(outs):
            for n, o in zip(names, outs):
                self.full[n] = o.reshape((-1,) + o.shape[2:]) if n in SLOT_MAJOR else o

        return items, sink

    def scatter_a_items(self, group):
        names = MAT_GROUPS[group]
        items = [("sa", self._by_chip_core(n), None) for n in names]

        def sink(outs):
            for n, o in zip(names, outs):
                self.from_sibling[n] = o

        return items, sink

    def scatter_b_items(self, group):
        names = MAT_GROUPS[group]
        items = [("sb", chip_partial(self._by_chip_core(n), self.from_sibling[n], "partial_" + n), None)
                 for n in names]

        def sink(outs):
            for n, o in zip(names, outs):
                self.recv[n] = o

        return items, sink

    def hook(self, site):
        parts = []
        if site in GATHER_PLAN:
            parts.append(self.gather_items(GATHER_PLAN[site]))
        if site in SCATTER_A_PLAN:
            parts.append(self.scatter_a_items(SCATTER_A_PLAN[site]))
        if site in SCATTER_B_PLAN:
            parts.append(self.scatter_b_items(SCATTER_B_PLAN[site]))
        if not parts:
            return None
        return combine_hooks(parts)


def combine_hooks(parts):
    items = [it for its, _ in parts for it in its]

    def sink(outs):
        p = 0
        for its, snk in parts:
            snk(outs[p:p + len(its)])
            p += len(its)

    return Comm(items), sink


def step(x, target, wts, ms, vs):
    me = _my_index()

    pieces = {}
    for li in range(2):
        for hi in range(2):
            tag = "%d%d" % (li, hi)
            pieces["w1t_" + tag] = wts["ffn_w1"][li, hi].T.astype(BF16)
            pieces["w3t_" + tag] = wts["ffn_w3"][li, hi].T.astype(BF16)
            pieces["w2_" + tag] = wts["ffn_w2"][li, hi].astype(BF16)
    pieces["w_int"] = wts["ssm_w_in"][0].T.astype(BF16)
    pieces["w_out"] = wts["ssm_w_out"][0].astype(BF16)
    pieces["w_kv"] = wts["w_kv"].astype(BF16)
    pieces["w_q"] = wts["w_q"][0].astype(BF16)
    pieces["w_o"] = wts["w_o"][0].astype(BF16)
    io = StepIO(pieces)

    small_sharded = [(n, s, a) for n, s, a in SMALL if a is not None]
    loc = jnp.concatenate([wts[n].reshape(-1) for n, _, _ in small_sharded])
    loc_rows = -(-loc.shape[0] // (8 * LANES)) * 8
    loc = jnp.pad(loc, (0, loc_rows * LANES - loc.shape[0])).reshape(loc_rows, LANES)
    got_small = []
    comm, sink = combine_hooks([io.gather_items([FIRST_GATHER]), ([("g", loc, None)], got_small.extend)])
    sink(comm_only(comm, "gather_first"))
    gath_small = got_small[0].reshape(N_DEV, -1)
    small = {}
    off = 0
    for n, s, a in small_sharded:
        shard = _shard_shape(s, a)
        cnt = int(np.prod(shard))
        small[n] = _unshard_view(gath_small[:, off:off + cnt].reshape((N_DEV,) + shard), shard, a)
        off += cnt
    for n, s, a in SMALL:
        if a is None:
            small[n] = wts[n]

    loss_part, grad_x, g_small_local = local_step(x[0], target[0], small, io)
    loss = lax.psum(loss_part, ("x", "y", "c"))

    small_flat = jnp.concatenate([g_small_local[n].reshape(-1) for n, _, _ in SMALL])
    small_buf = jnp.pad(small_flat, (0, SMALL_FULL_ROWS * SMALL_W - small_flat.shape[0]))
    small_buf = small_buf.reshape(SMALL_FULL_ROWS, SMALL_W)
    got_small = []
    comm, sink = combine_hooks([io.scatter_b_items(LAST_SCATTER), ([("g", small_buf, None)], got_small.extend)])
    sink(comm_only(comm, "exchange_last"))
    small_all = got_small[0]

    def sum_body(r_ref, o_ref):
        o_ref[...] = _slot_sum(r_ref)

    vmem = pl.BlockSpec(memory_space=pltpu.VMEM)
    small_sum, = pcall(sum_body, name="sum_small", grid=(), in_specs=[vmem], out_specs=[vmem],
                       out_shape=[jax.ShapeDtypeStruct((SMALL_FULL_ROWS, SMALL_W), F32)], args=[small_all])
    small_sum = small_sum.reshape(-1)
    g_small = {}
    off = 0
    for n, s, a in SMALL:
        cnt = int(np.prod(s))
        gfull = small_sum[off:off + cnt].reshape(s)
        off += cnt
        if a is None:
            g_small[n] = gfull
        else:
            width = s[a] // N_DEV
            g_small[n] = lax.dynamic_slice_in_dim(gfull, me * width, width, axis=a)

    out = {}

    def emit(name, res, shape):
        for kind, arr in zip(("grad", "delta", "new_m", "new_v"), res):
            out[kind + "_" + name] = arr.reshape(shape)

    for name, key in (("ffn_w1", "w1t_"), ("ffn_w3", "w3t_")):
        shp = wts[name].shape
        view = lambda t: t.reshape((4,) + shp[2:])
        res = adamw_cols([io.recv[key + tag] for tag in FFN_TAGS], view(wts[name]), view(ms[name]), view(vs[name]),
                         "adamw_" + name)
        emit(name, res, shp)
    shp = wts["ffn_w2"].shape
    view = lambda t: t.reshape((4,) + shp[2:])
    res = adamw_rows([io.recv["w2_" + tag] for tag in FFN_TAGS], view(wts["ffn_w2"]), view(ms["ffn_w2"]),
                     view(vs["ffn_w2"]), "adamw_ffn_w2")
    emit("ffn_w2", res, shp)
    res = adamw_cols([io.recv["w_int"]], wts["ssm_w_in"], ms["ssm_w_in"], vs["ssm_w_in"], "adamw_ssm_w_in")
    emit("ssm_w_in", res, wts["ssm_w_in"].shape)
    for name, key in (("ssm_w_out", "w_out"), ("w_kv", "w_kv"), ("w_q", "w_q"), ("w_o", "w_o")):
        shp = wts[name].shape
        view = lambda t: t.reshape((1,) + shp[-2:])
        res = adamw_rows([io.recv[key]], view(wts[name]), view(ms[name]), view(vs[name]), "adamw_" + name)
        emit(name, res, shp)

    res_s = rowmap(lambda gg, ww, mm_, vv: _adamw(gg, ww, mm_, vv),
                   [_small_local(g_small), _small_local(wts), _small_local(ms), _small_local(vs)], [],
                   [(LANES, F32)] * 3, tm=SMALL_LOCAL_ROWS, name="adamw_small")
    flat_s = [r.reshape(-1) for r in res_s]
    off = 0
    for n, s, a in SMALL:
        shard = s if a is None else _shard_shape(s, a)
        cnt = int(np.prod(shard))
        out["grad_" + n] = g_small[n]
        for kind, arr in zip(("delta", "new_m", "new_v"), flat_s):
            out[kind + "_" + n] = arr[off:off + cnt].reshape(shard)
        off += cnt
    out["loss"] = loss
    out["grad_x"] = grad_x[None]
    return out


def kernel(x, ffn_norm, ffn_w1, ffn_w3, ffn_w2, ssm_norm, ssm_w_in, ssm_conv_w, ssm_conv_b, ssm_dt_bias, ssm_a_log, ssm_d, ssm_gate_norm, ssm_w_out, kv_norm, w_kv, k_norm, attn_norm, w_q, q_norm, sinks, w_o, rel_bias, loss_target, m_ffn_norm, m_ffn_w1, m_ffn_w3, m_ffn_w2, m_ssm_norm, m_ssm_w_in, m_ssm_conv_w, m_ssm_conv_b, m_ssm_dt_bias, m_ssm_a_log, m_ssm_d, m_ssm_gate_norm, m_ssm_w_out, m_kv_norm, m_w_kv, m_k_norm, m_attn_norm, m_w_q, m_q_norm, m_sinks, m_w_o, m_rel_bias, v_ffn_norm, v_ffn_w1, v_ffn_w3, v_ffn_w2, v_ssm_norm, v_ssm_w_in, v_ssm_conv_w, v_ssm_conv_b, v_ssm_dt_bias, v_ssm_a_log, v_ssm_d, v_ssm_gate_norm, v_ssm_w_out, v_kv_norm, v_w_kv, v_k_norm, v_attn_norm, v_w_q, v_q_norm, v_sinks, v_w_o, v_rel_bias):
    args = locals()
    wts = {n: args[n] for n in WEIGHT_NAMES}
    ms = {n: args["m_" + n] for n in WEIGHT_NAMES}
    vs = {n: args["v_" + n] for n in WEIGHT_NAMES}
    out = step(x, loss_target, wts, ms, vs)
    result = [out["loss"], out["grad_x"]]
    for kind in ("grad", "delta", "new_m", "new_v"):
        result += [out[kind + "_" + n] for n in WEIGHT_NAMES]
    return tuple(result)
```

```python
import functools
import math
import operator

import numpy as np
import jax
import jax.numpy as jnp
from jax import lax
from jax.experimental import pallas as pl
from jax.experimental.pallas import tpu as pltpu

F32 = jnp.float32
BF16 = jnp.bfloat16

D_MODEL = 1024
D_FF = 2816
N_DEV = 8
SSM_D_INNER = 2048
SSM_HEAD_DIM = 64
SSM_HEADS = 32
SSM_GROUPS = 4
SSM_STATE = 128
SSM_CONV = 4
SSM_CHUNK = 256
SSM_CONV_DIM = SSM_D_INNER + 2 * SSM_GROUPS * SSM_STATE
SSM_IN_DIM = SSM_D_INNER + SSM_CONV_DIM + SSM_HEADS
ATT_HEAD_DIM = 64
ATT_HEADS = 16
ATT_KV_HEADS = 2
ATT_GROUP = 8
ATT_WINDOW = 128
REL_BUCKETS = 32
EPS = 1e-6
NEG = -1e30

ADAM_LR = 0.001
ADAM_B1 = 0.9
ADAM_B2 = 0.999
ADAM_EPS = 1e-08
ADAM_WD = 0.01
ADAM_STEP = 10

VMEM_LIMIT_BYTES = 52 * 1024 * 1024
LANES = 128
MESH_ID = pl.DeviceIdType.MESH
ANY_SPEC = pl.BlockSpec(memory_space=pl.ANY)

NT = (((1,), (1,)), ((), ()))
TN = (((0,), (0,)), ((), ()))
NN = (((1,), (0,)), ((), ()))


def _pick(dim, cands):
    for c in cands:
        if dim % c == 0:
            return c
    return dim


def _my_index():
    return 4 * lax.axis_index("x") + 2 * lax.axis_index("y") + lax.axis_index("c")


def _peer(k):
    x, y, c = lax.axis_index("x"), lax.axis_index("y"), lax.axis_index("c")
    px = 1 - x if (k >> 2) & 1 else x
    py = 1 - y if (k >> 1) & 1 else y
    pc = 1 - c if k & 1 else c
    return (px, py, pc), 4 * px + 2 * py + pc


def _piece(ref, axis, d, n):
    if axis is None:
        return ref.at[d]
    return ref.at[(slice(None),) * axis + (pl.ds(pl.multiple_of(d * n, 8), n),)]


SIBLING = 1
CHIP_PEERS = (4, 2, 6)
N_CHIPS = 4
SEMS_PER_ITEM = N_DEV - 1


def _my_chip():
    return 2 * lax.axis_index("x") + lax.axis_index("y")


class Comm:
    def __init__(self, items):
        self.items = list(items)

    def dst_shapes(self):
        out = []
        for kind, src, axis in self.items:
            s = tuple(src.shape)
            if kind == "g":
                shp = (N_DEV,) + s
            elif kind == "g2":
                shp = (N_DEV,) + s if axis is None else s[:axis] + (N_DEV * s[axis],) + s[axis + 1:]
            elif kind == "sa":
                shp = (s[0], 1) + s[2:]
            else:
                shp = s
            out.append(jax.ShapeDtypeStruct(shp, src.dtype))
        return out

    def scratch(self):
        n = len(self.items)
        return [pltpu.SemaphoreType.DMA((n * SEMS_PER_ITEM,)), pltpu.SemaphoreType.DMA((n * SEMS_PER_ITEM,)),
                pltpu.SemaphoreType.DMA((n,))]

    def _run(self, srcs, dsts, sems, starting):
        send_sems, recv_sems, local_sems = sems
        me = _my_index()
        core = lax.axis_index("c")
        chip = _my_chip()
        for i, (kind, src, axis) in enumerate(self.items):
            s_ref, d_ref = srcs[i], dsts[i]
            base = i * SEMS_PER_ITEM

            def rdma(src_ref, dst_ref, j, peer):
                return pltpu.make_async_remote_copy(
                    src_ref=src_ref, dst_ref=dst_ref, send_sem=send_sems.at[base + j], recv_sem=recv_sems.at[base + j],
                    device_id=peer, device_id_type=MESH_ID)

            if kind == "g":
                local = pltpu.make_async_copy(s_ref, d_ref.at[me], local_sems.at[i])
                outs = [rdma(s_ref, d_ref.at[me], k - 1, _peer(k)[0]) for k in range(1, N_DEV)]
                if starting:
                    local.start()
                    for cp in outs:
                        cp.start()
                else:
                    for k in range(1, N_DEV):
                        rdma(s_ref, d_ref.at[_peer(k)[1]], k - 1, _peer(k)[0]).wait_recv()
                    for cp in outs:
                        cp.wait_send()
                    local.wait()
            elif kind == "g2":
                n = None if axis is None else src.shape[axis]
                mine = _piece(d_ref, axis, me, n)
                sib = _peer(SIBLING)[0]
                local = pltpu.make_async_copy(s_ref, mine, local_sems.at[i])
                outs = [rdma(s_ref, mine, 0, sib)] + [rdma(s_ref, mine, 1 + j, _peer(k)[0])
                                                      for j, k in enumerate(CHIP_PEERS)]
                if starting:
                    local.start()
                    for cp in outs:
                        cp.start()
                else:
                    passed = []
                    for j, k in enumerate(CHIP_PEERS):
                        theirs = _piece(d_ref, axis, _peer(k)[1], n)
                        rdma(s_ref, theirs, 1 + j, _peer(k)[0]).wait_recv()
                        fwd = rdma(theirs, theirs, 4 + j, sib)
                        fwd.start()
                        passed.append(fwd)
                    rdma(s_ref, _piece(d_ref, axis, _peer(SIBLING)[1], n), 0, sib).wait_recv()
                    for j, k in enumerate(CHIP_PEERS):
                        rdma(s_ref, _piece(d_ref, axis, _peer(k ^ SIBLING)[1], n), 4 + j, sib).wait_recv()
                    for cp in outs + passed:
                        cp.wait_send()
                    local.wait()
            elif kind == "sa":
                cp = rdma(s_ref.at[(slice(None), pl.ds(1 - core, 1))], d_ref, 0, _peer(SIBLING)[0])
                if starting:
                    cp.start()
                else:
                    cp.wait_recv()
                    cp.wait_send()
            else:
                local = pltpu.make_async_copy(s_ref.at[chip], d_ref.at[chip], local_sems.at[i])
                outs = [rdma(s_ref.at[_peer(k)[1] >> 1], d_ref.at[chip], 1 + j, _peer(k)[0])
                        for j, k in enumerate(CHIP_PEERS)]
                if starting:
                    local.start()
                    for cp in outs:
                        cp.start()
                else:
                    for j, k in enumerate(CHIP_PEERS):
                        rdma(s_ref.at[chip], d_ref.at[_peer(k)[1] >> 1], 1 + j, _peer(k)[0]).wait_recv()
                    for cp in outs:
                        cp.wait_send()
                    local.wait()

    def start(self, srcs, dsts, sems):
        self._run(srcs, dsts, sems, True)

    def wait(self, srcs, dsts, sems):
        self._run(srcs, dsts, sems, False)


def pcall(body, *, name, grid, in_specs, out_specs, out_shape, args, scratch=(), hook=None):
    cparams = pltpu.CompilerParams(dimension_semantics=("arbitrary",) * len(grid), vmem_limit_bytes=VMEM_LIMIT_BYTES)
    if hook is None:
        outs = pl.pallas_call(body, name=name, grid=grid, in_specs=list(in_specs), out_specs=list(out_specs),
                              out_shape=list(out_shape), scratch_shapes=list(scratch), compiler_params=cparams)(*args)
        return list(outs)
    comm, sink = hook
    n_in, n_out, n_scr, n_it = len(args), len(out_shape), len(scratch), len(comm.items)
    dims = tuple(grid)

    def wrapped(*refs):
        p = 0
        ins = refs[p:p + n_in]
        p += n_in
        csrc = refs[p:p + n_it]
        p += n_it
        outs = refs[p:p + n_out]
        p += n_out
        cdst = refs[p:p + n_it]
        p += n_it
        scr = refs[p:p + n_scr]
        p += n_scr
        sems = refs[p:p + 3]
        if dims:
            ids = [pl.program_id(a) for a in range(len(dims))]
            first = functools.reduce(operator.and_, [i == 0 for i in ids])
            last = functools.reduce(operator.and_, [i == d - 1 for i, d in zip(ids, dims)])

            @pl.when(first)
            def _():
                comm.start(csrc, cdst, sems)

            body(*ins, *outs, *scr)

            @pl.when(last)
            def _():
                comm.wait(csrc, cdst, sems)
        else:
            comm.start(csrc, cdst, sems)
            body(*ins, *outs, *scr)
            comm.wait(csrc, cdst, sems)

    res = pl.pallas_call(
        wrapped, name=name, grid=grid,
        in_specs=list(in_specs) + [ANY_SPEC] * n_it, out_specs=list(out_specs) + [ANY_SPEC] * n_it,
        out_shape=list(out_shape) + comm.dst_shapes(), scratch_shapes=list(scratch) + comm.scratch(),
        compiler_params=cparams,
    )(*args, *[src for _, src, _ in comm.items])
    res = list(res)
    sink(res[n_out:])
    return res[:n_out]


def comm_only(comm, name):
    got = []
    pcall(lambda *refs: None, name=name, grid=(), in_specs=[], out_specs=[], out_shape=[], args=[],
          hook=(comm, got.extend))
    return got


def mm(a, b, *, ta=False, tb=False, out_dtype=F32, res=None, alpha=1.0, name, hook=None):
    if ta:
        k_dim, m_dim = a.shape
    else:
        m_dim, k_dim = a.shape
    if tb:
        n_dim, k2 = b.shape
    else:
        k2, n_dim = b.shape
    assert k_dim == k2, (a.shape, b.shape, ta, tb)
    tn = _pick(n_dim, (1024, 1408, 512, 256, 128))
    tm = _pick(m_dim, (1024, 1408, 512, 256, 128)) if tn <= 1024 else _pick(m_dim, (512, 256, 128))
    if not ta and m_dim % 2048 == 0 and tn == 1024:
        tm, tn = 2048, 512
    tk = _pick(k_dim, (1024, 512, 256, 128)) if ta else _pick(k_dim, (512, 1408, 256, 128))
    nk = k_dim // tk
    has_res = res is not None
    dn = (((0 if ta else 1,), (1 if tb else 0,)), ((), ()))

    def body(*refs):
        if has_res:
            a_ref, b_ref, r_ref, o_ref, acc_ref = refs
        else:
            a_ref, b_ref, o_ref, acc_ref = refs
        k = pl.program_id(2)

        @pl.when(k == 0)
        def _():
            acc_ref[...] = jnp.zeros_like(acc_ref)

        acc_ref[...] += lax.dot_general(a_ref[...].astype(BF16), b_ref[...].astype(BF16), dn,
                                        preferred_element_type=F32)

        @pl.when(k == nk - 1)
        def _():
            r = acc_ref[...]
            if alpha != 1.0:
                r = r * alpha
            if has_res:
                r = r_ref[...] + r
            o_ref[...] = r.astype(o_ref.dtype)

    a_spec = pl.BlockSpec((tk, tm), lambda i, j, k: (k, i)) if ta else pl.BlockSpec((tm, tk), lambda i, j, k: (i, k))
    b_spec = pl.BlockSpec((tn, tk), lambda i, j, k: (j, k)) if tb else pl.BlockSpec((tk, tn), lambda i, j, k: (k, j))
    o_spec = pl.BlockSpec((tm, tn), lambda i, j, k: (i, j))
    in_specs = [a_spec, b_spec] + ([o_spec] if has_res else [])
    args = [a, b] + ([res] if has_res else [])
    out, = pcall(body, name=name, grid=(m_dim // tm, n_dim // tn, nk), in_specs=in_specs, out_specs=[o_spec],
                 out_shape=[jax.ShapeDtypeStruct((m_dim, n_dim), out_dtype)], args=args,
                 scratch=[pltpu.VMEM((tm, tn), F32)], hook=hook)
    return out


def rowmap(fn, rows, consts=(), out_rows=(), out_accs=(), *, tm, name, hook=None):
    first = rows[0][0] if isinstance(rows[0], tuple) else rows[0]
    t_dim = first.shape[0]
    assert t_dim % tm == 0, (t_dim, tm)
    n_r, n_c, n_o = len(rows), len(consts), len(out_rows)

    def body(*refs):
        ins = [r[...] for r in refs[:n_r + n_c]]
        o_refs = refs[n_r + n_c:]
        outs = tuple(fn(*ins))
        for o_ref, val in zip(o_refs[:n_o], outs[:n_o]):
            o_ref[...] = val.astype(o_ref.dtype)
        if out_accs:
            @pl.when(pl.program_id(0) == 0)
            def _():
                for o_ref in o_refs[n_o:]:
                    o_ref[...] = jnp.zeros_like(o_ref)

            for o_ref, val in zip(o_refs[n_o:], outs[n_o:]):
                o_ref[...] += val

    in_specs, args = [], []
    for r in rows:
        if isinstance(r, tuple):
            args.append(r[0])
            in_specs.append(r[1])
        else:
            args.append(r)
            in_specs.append(pl.BlockSpec((tm, r.shape[1]), lambda i: (i, 0)))
    for c in consts:
        args.append(c)
        in_specs.append(pl.BlockSpec(c.shape, lambda i, nd=c.ndim: (0,) * nd))
    out_specs = [pl.BlockSpec((tm, w), lambda i: (i, 0)) for (w, _) in out_rows]
    out_specs += [pl.BlockSpec(s, lambda i, nd=len(s): (0,) * nd) for s in out_accs]
    out_shape = [jax.ShapeDtypeStruct((t_dim, w), dt) for (w, dt) in out_rows]
    out_shape += [jax.ShapeDtypeStruct(s, F32) for s in out_accs]
    return pcall(body, name=name, grid=(t_dim // tm,), in_specs=in_specs, out_specs=out_specs, out_shape=out_shape,
                 args=args, hook=hook)


def _rms_fwd(x, g):
    r = lax.rsqrt(jnp.mean(x * x, axis=-1, keepdims=True) + EPS)
    return x * r * g


def _rms_bwd(x, g, dy):
    r = lax.rsqrt(jnp.mean(x * x, axis=-1, keepdims=True) + EPS)
    xh = x * r
    dg = jnp.sum(dy * xh, axis=0, keepdims=True)
    dxh = dy * g
    dx = r * (dxh - xh * jnp.mean(dxh * xh, axis=-1, keepdims=True))
    return dx, dg


def _sigmoid(x):
    return 1.0 / (1.0 + jnp.exp(-x))


def _silu(x):
    return x * _sigmoid(x)


def _silu_grad(x):
    s = _sigmoid(x)
    return s * (1.0 + x * (1.0 - s))


def _split3(x):
    hi = x.astype(BF16)
    r1 = x - hi.astype(F32)
    mid = r1.astype(BF16)
    lo = (r1 - mid.astype(F32)).astype(BF16)
    return hi, mid, lo


def _dot(a, b, dn=NN):
    return lax.dot_general(a.astype(BF16), b.astype(BF16), dn, preferred_element_type=F32)


def _col_of(mat, h):
    lane = lax.broadcasted_iota(jnp.int32, mat.shape, 1)
    return jnp.sum(jnp.where(lane == h, mat, 0.0), axis=1, keepdims=True)


FFN_TN = 1408
RESIDENT_TM = 512


def ffn_upgate(h, g, w1t, w3t, nm, hook=None):
    t_dim = h.shape[0]
    tm = _pick(t_dim, (512, 256, 128))
    tn = FFN_TN

    n_j = D_FF // tn
    u_w = D_MODEL // n_j

    def body(h_ref, g_ref, w1_ref, w3_ref, u_ref, a_ref, b_ref, hm_ref):
        uu = _rms_fwd(h_ref[...], g_ref[...]).astype(BF16)
        for j in range(n_j):
            @pl.when(pl.program_id(0) == j)
            def _(j=j):
                u_ref[...] = uu[:, j * u_w:(j + 1) * u_w]

        a = lax.dot_general(uu, w1_ref[...], NT, preferred_element_type=F32)
        b = lax.dot_general(uu, w3_ref[...], NT, preferred_element_type=F32)
        a_ref[...] = a.astype(a_ref.dtype)
        b_ref[...] = b.astype(b_ref.dtype)
        hm_ref[...] = (_silu(a) * b).astype(hm_ref.dtype)

    row_spec = pl.BlockSpec((tm, D_MODEL), lambda j, i: (i, 0))
    w_spec = pl.BlockSpec((tn, D_MODEL), lambda j, i: (j, 0))
    o_spec = pl.BlockSpec((tm, tn), lambda j, i: (i, j))
    o_shape = jax.ShapeDtypeStruct((t_dim, D_FF), BF16)
    return pcall(body, name=nm, grid=(D_FF // tn, t_dim // tm),
                 in_specs=[row_spec, pl.BlockSpec((1, D_MODEL), lambda j, i: (0, 0)), w_spec, w_spec],
                 out_specs=[pl.BlockSpec((tm, u_w), lambda j, i: (i, j))] + [o_spec] * 3,
                 out_shape=[jax.ShapeDtypeStruct((t_dim, D_MODEL), BF16)] + [o_shape] * 3,
                 args=[h, g, w1t, w3t], hook=hook)


def ffn_dgate(dout_bf, w2, a, b, nm, hook=None):
    t_dim = dout_bf.shape[0]
    tm = _pick(t_dim, (512, 256, 128))
    tn = FFN_TN

    def body(d_ref, w2_ref, a_ref, b_ref, da_ref, db_ref):
        dhm = 0.5 * lax.dot_general(d_ref[...], w2_ref[...], NT, preferred_element_type=F32)
        av = a_ref[...].astype(F32)
        bv = b_ref[...].astype(F32)
        sg = _sigmoid(av)
        da_ref[...] = (dhm * bv * (sg * (1.0 + av * (1.0 - sg)))).astype(da_ref.dtype)
        db_ref[...] = (dhm * (av * sg)).astype(db_ref.dtype)

    t_spec = pl.BlockSpec((tm, tn), lambda j, i: (i, j))
    o_shape = jax.ShapeDtypeStruct((t_dim, D_FF), BF16)
    return pcall(body, name=nm, grid=(D_FF // tn, t_dim // tm),
                 in_specs=[pl.BlockSpec((tm, D_MODEL), lambda j, i: (i, 0)),
                           pl.BlockSpec((tn, D_MODEL), lambda j, i: (j, 0)), t_spec, t_spec],
                 out_specs=[t_spec] * 2, out_shape=[o_shape] * 2, args=[dout_bf, w2, a, b], hook=hook)


def ffn_fwd(h, g, tag, io, target=None):
    nm = "f" + tag
    u, a, b, hm = ffn_upgate(h, g, io.w("w1t_" + tag), io.w("w3t_" + tag), nm + "_upgate",
                             hook=io.hook(nm + "_upgate"))
    if target is None:
        return mm(hm, io.w("w2_" + tag), res=h, alpha=0.5, name=nm + "_down"), (u, a, b, hm)

    def down_loss(hmv, hv, t, w2):
        e = hv + 0.5 * _dot(hmv, w2) - t
        d = e * (1.0 / D_MODEL)
        return d, d, jnp.sum(e * e, axis=0, keepdims=True)

    res = rowmap(down_loss, [hm, h, target], [io.w("w2_" + tag)], [(D_MODEL, F32), (D_MODEL, BF16)],
                 [(1, D_MODEL)], tm=RESIDENT_TM, name=nm + "_down_loss")
    return res, (u, a, b, hm)


def du_norm_bwd(pairs, h, g, dout, nm, hook=None):
    t_dim = h.shape[0]
    tm = RESIDENT_TM
    n_p = len(pairs)

    def body(*refs):
        h_ref, d_ref, g_ref = refs[2 * n_p:2 * n_p + 3]
        dh_ref, dhb_ref, dg_ref = refs[2 * n_p + 3:]
        du = None
        for p, (_, _, tb) in enumerate(pairs):
            t = lax.dot_general(refs[2 * p][...].astype(BF16), refs[2 * p + 1][...].astype(BF16), NT if tb else NN,
                                preferred_element_type=F32)
            du = t if du is None else du + t
        dx, dg = _rms_bwd(h_ref[...], g_ref[...], du)
        dh = d_ref[...] + dx
        dh_ref[...] = dh
        dhb_ref[...] = dh.astype(dhb_ref.dtype)

        @pl.when(pl.program_id(0) == 0)
        def _():
            dg_ref[...] = jnp.zeros_like(dg_ref)

        dg_ref[...] += dg

    in_specs, args = [], []
    for a, b, _ in pairs:
        in_specs += [pl.BlockSpec((tm, a.shape[1]), lambda i: (i, 0)), pl.BlockSpec(b.shape, lambda i: (0, 0))]
        args += [a, b]
    row_spec = pl.BlockSpec((tm, D_MODEL), lambda i: (i, 0))
    vec_spec = pl.BlockSpec((1, D_MODEL), lambda i: (0, 0))
    return pcall(body, name=nm, grid=(t_dim // tm,), in_specs=in_specs + [row_spec, row_spec, vec_spec],
                 out_specs=[row_spec, row_spec, vec_spec],
                 out_shape=[jax.ShapeDtypeStruct((t_dim, D_MODEL), F32), jax.ShapeDtypeStruct((t_dim, D_MODEL), BF16),
                            jax.ShapeDtypeStruct((1, D_MODEL), F32)],
                 args=args + [h, dout, g], hook=hook)


def ffn_bwd(h, g, tag, saved, dout, dout_bf, io):
    nm = "f" + tag
    w1t, w3t, w2 = io.w("w1t_" + tag), io.w("w3t_" + tag), io.w("w2_" + tag)
    u, a, b, hm = saved
    io.put("w2_" + tag, mm(hm, dout_bf, ta=True, alpha=0.5, out_dtype=BF16, name=nm + "_dw2",
                           hook=io.hook(nm + "_dw2")))
    da, db = ffn_dgate(dout_bf, w2, a, b, nm + "_dgate", hook=io.hook(nm + "_dgate"))
    io.put("w1t_" + tag, mm(da, u, ta=True, out_dtype=BF16, name=nm + "_dw1"))
    io.put("w3t_" + tag, mm(db, u, ta=True, out_dtype=BF16, name=nm + "_dw3", hook=io.hook(nm + "_dw3")))
    return du_norm_bwd([(da, w1t, False), (db, w3t, False)], h, g, dout, nm + "_du", hook=io.hook(nm + "_du"))


def conv_input_grad(d_parts, w, nm):
    tm = 256
    t_dim = d_parts[0].shape[0]
    n_tiles = t_dim // tm

    def fn(d1, n1, d2, n2, d3, n3, ww):
        d = jnp.concatenate([d1, d2, d3], axis=1)
        nxt = jnp.concatenate([n1, n2, n3], axis=1)
        nxt = jnp.where(pl.program_id(0) < n_tiles - 1, nxt, 0.0)
        dd = jnp.concatenate([d, nxt], axis=0)
        out = dd[3:3 + tm] * ww[0:1]
        for k in range(1, SSM_CONV):
            out = out + dd[3 - k:3 - k + tm] * ww[k:k + 1]
        return (out,)

    rows = []
    for d in d_parts:
        below = pl.BlockSpec((8, d.shape[1]), lambda i: (jnp.minimum((i + 1) * (tm // 8), t_dim // 8 - 1), 0))
        rows += [d, (d, below)]
    dx, = rowmap(fn, rows, [w], [(SSM_CONV_DIM, BF16)], tm=tm, name=nm)
    return dx


GRP_W = SSM_D_INNER // SSM_GROUPS
HPG = SSM_HEADS // SSM_GROUPS
HEAD_SHIFT = 6


def _split2(x):
    hi = x.astype(BF16)
    return hi, (x - hi.astype(F32)).astype(BF16)


def _expand_mats():
    e = ((lax.broadcasted_iota(jnp.int32, (HPG, GRP_W), 1) >> HEAD_SHIFT)
         == lax.broadcasted_iota(jnp.int32, (HPG, GRP_W), 0)).astype(BF16)
    et = ((lax.broadcasted_iota(jnp.int32, (GRP_W, HPG), 0) >> HEAD_SHIFT)
          == lax.broadcasted_iota(jnp.int32, (GRP_W, HPG), 1)).astype(BF16)
    return e, et


def _expand(v, e_m):
    hi, lo = _split2(v)
    return jnp.dot(hi, e_m, preferred_element_type=F32) + jnp.dot(lo, e_m, preferred_element_type=F32)


def _reduce8(v, et_m):
    acc = None
    for p in _split3(v):
        t = jnp.dot(p, et_m, preferred_element_type=F32)
        acc = t if acc is None else acc + t
    return acc


def _ssd_group_terms(dt_ref, dtT_ref, arow_ref, acol_ref):
    L = SSM_CHUNK
    r = lax.broadcasted_iota(jnp.int32, (L, L), 0)
    c = lax.broadcasted_iota(jnp.int32, (L, L), 1)
    tril = (r >= c).astype(BF16)
    triu = (r <= c).astype(BF16)
    dtg = dt_ref[0]
    acol = None
    for p in _split3(dtg * arow_ref[0]):
        t = jnp.dot(tril, p, preferred_element_type=F32)
        acol = t if acol is None else acol + t
    arowT = None
    for p in _split3(dtT_ref[0] * acol_ref[0]):
        t = jnp.dot(p, triu, preferred_element_type=F32)
        arowT = t if arowT is None else arowT + t
    return dtg, acol, arowT, r >= c


def _state_decay(a_last_col, et_m):
    hi, lo = _split2(jnp.broadcast_to(jnp.exp(a_last_col), (HPG, SSM_STATE)))
    return jnp.dot(et_m, hi, preferred_element_type=F32) + jnp.dot(et_m, lo, preferred_element_type=F32)


def _conv_block(x_ref, halo_ref, w_ref, b_ref, first):
    L = SSM_CHUNK
    xx = jnp.concatenate([jnp.where(first, 0.0, halo_ref[...]), x_ref[...]], axis=0)
    w = w_ref[...]
    shifted = [xx[5 + k:5 + k + L] for k in range(SSM_CONV)]
    acc = b_ref[...] + shifted[0] * w[0:1]
    for k in range(1, SSM_CONV):
        acc = acc + shifted[k] * w[k:k + 1]
    return acc, shifted


def _ssd_specs(nc, rev):
    L, N = SSM_CHUNK, SSM_STATE
    xcols = SSM_D_INNER // LANES
    ch = (lambda c: nc - 1 - c) if rev else (lambda c: c)
    above = lambda c: jnp.maximum(ch(c) * (L // 8) - 1, 0)
    specs = []
    for width, col in ((GRP_W, lambda g: g), (N, lambda g: xcols + g), (N, lambda g: xcols + SSM_GROUPS + g)):
        specs += [
            pl.BlockSpec((L, width), lambda c, g, col=col: (ch(c), col(g))),
            pl.BlockSpec((8, width), lambda c, g, col=col: (above(c), col(g))),
            pl.BlockSpec((SSM_CONV, width), lambda c, g, col=col: (0, col(g))),
            pl.BlockSpec((1, width), lambda c, g, col=col: (0, col(g))),
        ]
    return specs + [
        pl.BlockSpec((1, L, HPG), lambda c, g: (g, ch(c), 0)),
        pl.BlockSpec((1, HPG, L), lambda c, g: (g, 0, ch(c))),
        pl.BlockSpec((1, 1, HPG), lambda c, g: (g, 0, 0)),
        pl.BlockSpec((1, HPG, 1), lambda c, g: (g, 0, 0)),
        pl.BlockSpec((1, GRP_W), lambda c, g: (0, g)),
    ]


def ssd_fwd(xbc_raw, conv_w, conv_b, dt_g, dtT_g, a_row, a_col, dvec, nm, hook=None):
    t_dim = xbc_raw.shape[0]
    L, P, N = SSM_CHUNK, SSM_HEAD_DIM, SSM_STATE
    nc = t_dim // L

    def body(x_ref, xh_ref, xw_ref, xb_ref, b_ref, bh_ref, bw_ref, bb_ref, c_ref, ch_ref, cw_ref, cb_ref,
             dt_ref, dtT_ref, arow_ref, acol_ref, dvec_ref, y_ref, st_ref, s_s):
        ci = pl.program_id(0)
        g = pl.program_id(1)

        @pl.when((ci == 0) & (g == 0))
        def _():
            s_s[...] = jnp.zeros_like(s_s)

        e_m, et_m = _expand_mats()
        dtg, acol, arowT, causal = _ssd_group_terms(dt_ref, dtT_ref, arow_ref, acol_ref)
        a_last_row = acol[L - 1:L, :]
        x = _silu(_conv_block(x_ref, xh_ref, xw_ref, xb_ref, ci == 0)[0])
        bm = _silu(_conv_block(b_ref, bh_ref, bw_ref, bb_ref, ci == 0)[0])
        cm = _silu(_conv_block(c_ref, ch_ref, cw_ref, cb_ref, ci == 0)[0])
        cb = _dot(cm, bm, NT)
        s = s_s[g]
        st_ref[0, 0] = s
        ea_x = _expand(jnp.exp(acol), e_m)
        dt_x = _expand(dtg, e_m)
        w_x = _expand(jnp.exp(a_last_row - acol) * dtg, e_m)
        yb = ea_x * _dot(cm, s, NT) + dvec_ref[...] * x
        xd = (x * dt_x).astype(BF16)
        for e in range(HPG):
            sl = slice(e * P, (e + 1) * P)
            lm = jnp.exp(jnp.where(causal, acol[:, e:e + 1] - arowT[e:e + 1, :], NEG))
            m = (cb * lm).astype(BF16)
            y_ref[:, sl] = yb[:, sl] + jnp.dot(m, xd[:, sl], preferred_element_type=F32)
        s_s[g] = _state_decay(arowT[:, L - 1:L], et_m) * s + _dot(x * w_x, bm, TN)

    out_specs = [
        pl.BlockSpec((L, GRP_W), lambda c, g: (c, g)),
        pl.BlockSpec((1, 1, GRP_W, N), lambda c, g: (c, g, 0, 0)),
    ]
    return pcall(
        body, name=nm, grid=(nc, SSM_GROUPS), in_specs=_ssd_specs(nc, False), out_specs=out_specs,
        out_shape=[jax.ShapeDtypeStruct((t_dim, SSM_D_INNER), F32),
                   jax.ShapeDtypeStruct((nc, SSM_GROUPS, GRP_W, N), F32)],
        scratch=[pltpu.VMEM((SSM_GROUPS, GRP_W, N), F32)],
        args=[xbc_raw, xbc_raw, conv_w, conv_b] * 3 + [dt_g, dtT_g, a_row, a_col, dvec], hook=hook)


def ssd_bwd(dy, xbc_raw, conv_w, conv_b, dt_g, dtT_g, a_row, a_col, dvec, states, nm, hook=None):
    t_dim = xbc_raw.shape[0]
    L, P, N = SSM_CHUNK, SSM_HEAD_DIM, SSM_STATE
    nc = t_dim // L

    def body(dy_ref, x_ref, xh_ref, xw_ref, xb_ref, b_ref, bh_ref, bw_ref, bb_ref, c_ref, ch_ref, cw_ref, cb_ref,
             dt_ref, dtT_ref, arow_ref, acol_ref, dvec_ref, st_ref,
             dx_ref, db_ref, dc_ref, da_ref, ddt_ref, dd_ref, dwx_ref, dwb_ref, dwc_ref, dbx_ref, dbb_ref, dbc_ref,
             ds_s, yd_s, dxd_s):
        ci = pl.program_id(0)
        g = pl.program_id(1)

        @pl.when((ci == 0) & (g == 0))
        def _():
            ds_s[...] = jnp.zeros_like(ds_s)
            for r in (dd_ref, dwx_ref, dwb_ref, dwc_ref, dbx_ref, dbb_ref, dbc_ref):
                r[...] = jnp.zeros_like(r)

        e_m, et_m = _expand_mats()
        dtg, acol, arowT, causal = _ssd_group_terms(dt_ref, dtT_ref, arow_ref, acol_ref)
        a_last_row = acol[L - 1:L, :]
        first = ci == nc - 1
        pre_x, sh_x = _conv_block(x_ref, xh_ref, xw_ref, xb_ref, first)
        pre_b, sh_b = _conv_block(b_ref, bh_ref, bw_ref, bb_ref, first)
        pre_c, sh_c = _conv_block(c_ref, ch_ref, cw_ref, cb_ref, first)
        sg_x, sg_b, sg_c = _sigmoid(pre_x), _sigmoid(pre_b), _sigmoid(pre_c)
        x = pre_x * sg_x
        dy = dy_ref[...]
        bm = pre_b * sg_b
        cm = pre_c * sg_c
        cb = _dot(cm, bm, NT)
        s = st_ref[0, 0]
        dsp = ds_s[g]
        ew8 = jnp.exp(a_last_row - acol)
        ea_x = _expand(jnp.exp(acol), e_m)
        dt_x = _expand(dtg, e_m)
        ew_x = _expand(ew8, e_m)
        w_x = ew_x * dt_x
        z = _dot(cm, s, NT)
        dz = ea_x * dy
        dc = _dot(dz, s)
        ds_y = _dot(dz, cm, TN)
        du = _dot(bm, dsp, NT)
        u = x * w_x
        db = _dot(u, dsp)
        xd = (x * dt_x).astype(BF16)
        dyb = dy.astype(BF16)
        dcb = jnp.zeros((L, L), F32)
        for e in range(HPG):
            sl = slice(e * P, (e + 1) * P)
            lm = jnp.exp(jnp.where(causal, acol[:, e:e + 1] - arowT[e:e + 1, :], NEG))
            m = (cb * lm).astype(BF16)
            yd_s[:, sl] = jnp.dot(m, xd[:, sl], preferred_element_type=F32)
            dxd_s[:, sl] = lax.dot_general(m, dyb[:, sl], TN, preferred_element_type=F32)
            dcb = dcb + lax.dot_general(dyb[:, sl], xd[:, sl], NT, preferred_element_type=F32) * lm
        dxd = dxd_s[...]

        def through_conv(d_act, pre, sg, shifted, d_ref, dw_ref, dbias_ref):
            d_pre = d_act * (sg * (1.0 + pre * (1.0 - sg)))
            d_ref[...] = d_pre
            dw_ref[g] += jnp.concatenate([jnp.sum(d_pre * sh, axis=0, keepdims=True) for sh in shifted], axis=0)
            dbias_ref[g] += jnp.sum(d_pre, axis=0, keepdims=True)

        through_conv(dvec_ref[...] * dy + du * w_x + dt_x * dxd, pre_x, sg_x, sh_x, dx_ref, dwx_ref, dbx_ref)
        ddt = _reduce8(x * (ew_x * du + dxd), et_m)
        da = (_reduce8(dz * z + dyb.astype(F32) * yd_s[...], et_m)
              - _reduce8(xd.astype(F32) * dxd + du * u, et_m))
        dwa_row = _reduce8(jnp.broadcast_to(jnp.sum(du * u, axis=0, keepdims=True), (8, GRP_W)), et_m)[0:1]
        t_nh = None
        for p in _split3(dsp * s):
            t = lax.dot_general(p, et_m, TN, preferred_element_type=F32)
            t_nh = t if t_nh is None else t_nh + t
        d_last = dwa_row + jnp.exp(a_last_row) * jnp.sum(t_nh, axis=0, keepdims=True)
        row_l = lax.broadcasted_iota(jnp.int32, (L, 1), 0)
        da_ref[0] = da + jnp.where(row_l == L - 1, d_last, 0.0)
        ddt_ref[0] = ddt
        dd_ref[g] += jnp.sum(dy * x, axis=0, keepdims=True)
        through_conv(dc + _dot(dcb, bm), pre_c, sg_c, sh_c, dc_ref, dwc_ref, dbc_ref)
        through_conv(db + _dot(dcb, cm, TN), pre_b, sg_b, sh_b, db_ref, dwb_ref, dbb_ref)
        ds_s[g] = _state_decay(arowT[:, L - 1:L], et_m) * dsp + ds_y

    rc = lambda c: nc - 1 - c
    in_specs = ([pl.BlockSpec((L, GRP_W), lambda c, g: (rc(c), g))] + _ssd_specs(nc, True)
                + [pl.BlockSpec((1, 1, GRP_W, N), lambda c, g: (rc(c), g, 0, 0))])
    whole = lambda *shape: pl.BlockSpec(shape, lambda c, g: (0,) * len(shape))
    out_specs = [
        pl.BlockSpec((L, GRP_W), lambda c, g: (rc(c), g)),
        pl.BlockSpec((L, N), lambda c, g: (rc(c), g)),
        pl.BlockSpec((L, N), lambda c, g: (rc(c), g)),
        pl.BlockSpec((1, L, HPG), lambda c, g: (g, rc(c), 0)),
        pl.BlockSpec((1, L, HPG), lambda c, g: (g, rc(c), 0)),
        whole(SSM_GROUPS, 1, GRP_W),
        whole(SSM_GROUPS, SSM_CONV, GRP_W), whole(SSM_GROUPS, SSM_CONV, N), whole(SSM_GROUPS, SSM_CONV, N),
        whole(SSM_GROUPS, 1, GRP_W), whole(SSM_GROUPS, 1, N), whole(SSM_GROUPS, 1, N),
    ]
    gn = SSM_GROUPS * N
    acc = lambda *shape: jax.ShapeDtypeStruct(shape, F32)
    out_shape = [
        acc(t_dim, SSM_D_INNER), acc(t_dim, gn), acc(t_dim, gn), acc(SSM_GROUPS, t_dim, HPG),
        acc(SSM_GROUPS, t_dim, HPG), acc(SSM_GROUPS, 1, GRP_W),
        acc(SSM_GROUPS, SSM_CONV, GRP_W), acc(SSM_GROUPS, SSM_CONV, N), acc(SSM_GROUPS, SSM_CONV, N),
        acc(SSM_GROUPS, 1, GRP_W), acc(SSM_GROUPS, 1, N), acc(SSM_GROUPS, 1, N),
    ]
    return pcall(
        body, name=nm, grid=(nc, SSM_GROUPS), in_specs=in_specs, out_specs=out_specs, out_shape=out_shape,
        scratch=[pltpu.VMEM((SSM_GROUPS, GRP_W, N), F32), pltpu.VMEM((L, GRP_W), F32), pltpu.VMEM((L, GRP_W), F32)],
        args=[dy] + [xbc_raw, xbc_raw, conv_w, conv_b] * 3 + [dt_g, dtT_g, a_row, a_col, dvec, states], hook=hook)


def _softplus(x):
    return jnp.maximum(x, 0.0) + jnp.log(1.0 + jnp.exp(-jnp.abs(x)))


def ssd_dt_bwd(da, ddt, dt, dt_raw, a_row, dt_bias, nm):
    L = SSM_CHUNK

    def fn(d_a, d_dt, dtv, raw, ar, bias):
        r = lax.broadcasted_iota(jnp.int32, (L, L), 0)
        c = lax.broadcasted_iota(jnp.int32, (L, L), 1)
        triu = (r <= c).astype(BF16)
        acc = None
        for p in _split3(d_a):
            t = jnp.dot(triu, p, preferred_element_type=F32)
            acc = t if acc is None else acc + t
        d_dt = d_dt + acc * ar
        d_a_h = jnp.sum(acc * dtv, axis=0, keepdims=True)
        d_raw = d_dt * _sigmoid(raw + bias)
        return d_raw, d_a_h, jnp.sum(d_raw, axis=0, keepdims=True)

    return rowmap(fn, [da, ddt, dt, dt_raw], [a_row, dt_bias], [(SSM_HEADS, BF16)],
                  [(1, SSM_HEADS), (1, SSM_HEADS)], tm=L, name=nm)


GN_W = SSM_D_INNER // SSM_GROUPS


def mamba_fwd(h, p, nm, io):
    def in_proj(x, gg, w_zt, w_xbct, w_dtt):
        uu = _rms_fwd(x, gg).astype(BF16)
        return uu, _dot(uu, w_zt, NT), _dot(uu, w_xbct, NT), _dot(uu, w_dtt, NT)

    u, z, xbc_raw, dt_raw = rowmap(in_proj, [h], [p["ssm_norm"], p["w_zt"], p["w_xbct"], p["w_dtt"]],
                                   [(D_MODEL, BF16), (SSM_D_INNER, F32), (SSM_CONV_DIM, F32), (SSM_HEADS, F32)],
                                   tm=RESIDENT_TM, name=nm + "_in", hook=io.hook(nm + "_in"))
    dt, = rowmap(lambda r, b: (_softplus(r + b),), [dt_raw], [p["dt_bias"]], [(SSM_HEADS, F32)], tm=256,
                 name=nm + "_softplus")
    dt_g = dt.reshape(-1, SSM_GROUPS, HPG).transpose(1, 0, 2)
    dtT_g = dt_g.transpose(0, 2, 1)
    y, states = ssd_fwd(xbc_raw, p["conv_w"], p["conv_b"], dt_g, dtT_g, p["a_row"], p["a_col"], p["dvec"],
                        nm + "_ssd", hook=io.hook(nm + "_ssd"))

    def gate_norm_out(yv, zv, hv, gg, w_out):
        t = yv * _silu(zv)
        yn = jnp.concatenate([_rms_fwd(t[:, k * GN_W:(k + 1) * GN_W], gg[:, k * GN_W:(k + 1) * GN_W])
                              for k in range(SSM_GROUPS)], axis=1).astype(BF16)
        return yn, hv + _dot(yn, w_out)

    yn, out = rowmap(gate_norm_out, [y, z, h], [p["gate_norm"], p["w_out"]],
                     [(SSM_D_INNER, BF16), (D_MODEL, F32)], tm=RESIDENT_TM, name=nm + "_out")
    return out, (u, z, xbc_raw, dt_raw, dt, dt_g, dtT_g, y, states, yn)


def mamba_bwd(h, p, saved, dout, dout_bf, nm, io):
    u, z, xbc_raw, dt_raw, dt, dt_g, dtT_g, y, states, yn = saved
    g = {}
    io.put("w_out", mm(yn, dout_bf, ta=True, out_dtype=BF16, name=nm + "_dwout"))

    def gate_norm_bwd(d_o, yv, zv, gg, w_out):
        d = _dot(d_o, w_out, NT)
        sz = _silu(zv)
        t = yv * sz
        dts, dgs = [], []
        for k in range(SSM_GROUPS):
            sl = slice(k * GN_W, (k + 1) * GN_W)
            dt_k, dg_k = _rms_bwd(t[:, sl], gg[:, sl], d[:, sl])
            dts.append(dt_k)
            dgs.append(dg_k)
        d_t = jnp.concatenate(dts, axis=1)
        return d_t * sz, d_t * yv * _silu_grad(zv), jnp.concatenate(dgs, axis=1)

    dy, dz, g["gate_norm"] = rowmap(gate_norm_bwd, [dout_bf, y, z], [p["gate_norm"], p["w_out"]],
                                    [(SSM_D_INNER, F32), (SSM_D_INNER, BF16)], [(1, SSM_D_INNER)], tm=256,
                                    name=nm + "_dgatenorm")
    d_x, d_b, d_c, da_g, ddt_g, dd, dwx, dwb, dwc, dbx, dbb, dbc = ssd_bwd(
        dy, xbc_raw, p["conv_w"], p["conv_b"], dt_g, dtT_g, p["a_row"], p["a_col"], p["dvec"], states, nm + "_dssd",
        hook=io.hook(nm + "_dssd"))
    g["dvec"] = dd
    by_lane = lambda t: t.transpose(1, 0, 2).reshape(t.shape[1], -1)
    g["conv_w"] = jnp.concatenate([by_lane(dwx), by_lane(dwb), by_lane(dwc)], axis=1)
    g["conv_b"] = jnp.concatenate([by_lane(dbx), by_lane(dbb), by_lane(dbc)], axis=1)
    per_head = lambda t: t.transpose(1, 0, 2).reshape(-1, SSM_HEADS)
    ddt_raw, g["a"], g["dt_bias"] = ssd_dt_bwd(per_head(da_g), per_head(ddt_g), dt, dt_raw, p["a_heads"],
                                               p["dt_bias"], nm + "_ddt")
    dxbc_raw = conv_input_grad([d_x, d_b, d_c], p["conv_w"], nm + "_dconv")
    io.put("w_int", jnp.concatenate([mm(dz, u, ta=True, out_dtype=BF16, name=nm + "_dwz"),
                                     mm(dxbc_raw, u, ta=True, out_dtype=BF16, name=nm + "_dwxbc"),
                                     mm(ddt_raw, u, ta=True, out_dtype=BF16, name=nm + "_dwdt")], axis=0))
    dh, dh_bf, g["ssm_norm"] = du_norm_bwd(
        [(dz, p["w_zt"], False), (dxbc_raw, p["w_xbct"], False), (ddt_raw, p["w_dtt"], False)],
        h, p["ssm_norm"], dout, nm + "_du", hook=io.hook(nm + "_du"))
    return dh, dh_bf, g


KV_W = ATT_KV_HEADS * ATT_HEAD_DIM


def kv_fwd(h, p, nm):
    def kv_proj(x, gg, w_kv, gk):
        uu = _rms_fwd(x, gg).astype(BF16)
        t = _dot(uu, w_kv)
        ks = [_rms_fwd(t[:, j * ATT_HEAD_DIM:(j + 1) * ATT_HEAD_DIM], gk) for j in range(ATT_KV_HEADS)]
        return uu, t, jnp.concatenate(ks, axis=1), t[:, KV_W:]

    u, kv_raw, k, v = rowmap(kv_proj, [h], [p["kv_norm"], p["w_kv"], p["k_norm"]],
                             [(D_MODEL, BF16), (2 * KV_W, F32), (KV_W, F32), (KV_W, F32)], tm=RESIDENT_TM,
                             name=nm + "_proj")
    return k, v, (u, kv_raw)


def kv_bwd(h, p, saved, dk_cur, dk_prev, dv_cur, dv_prev, dout, nm, io):
    u, kv_raw = saved
    t_dim = h.shape[0]
    tm = ATT_WINDOW
    nb = t_dim // tm
    nxt = pl.BlockSpec((tm, KV_W), lambda i: (jnp.minimum(i + 1, nb - 1), 0))

    def fn(dkc, dkp, dvc, dvp, t, gg):
        live = pl.program_id(0) < nb - 1
        dk = dkc + jnp.where(live, dkp, 0.0)
        dv = dvc + jnp.where(live, dvp, 0.0)
        outs, dgs = [], None
        for j in range(ATT_KV_HEADS):
            sl = slice(j * ATT_HEAD_DIM, (j + 1) * ATT_HEAD_DIM)
            dx, dg = _rms_bwd(t[:, sl], gg, dk[:, sl])
            outs.append(dx)
            dgs = dg if dgs is None else dgs + dg
        return jnp.concatenate(outs + [dv], axis=1), dgs

    dkv_raw, dknorm = rowmap(fn, [dk_cur, (dk_prev, nxt), dv_cur, (dv_prev, nxt), kv_raw], [p["k_norm"]],
                             [(2 * KV_W, BF16)], [(1, ATT_HEAD_DIM)], tm=tm, name=nm + "_dknorm",
                             hook=io.hook(nm + "_dknorm"))
    g = {"k_norm": dknorm}
    io.put("w_kv", mm(u, dkv_raw, ta=True, out_dtype=BF16, name=nm + "_dwkv"))
    dh, dh_bf, g["kv_norm"] = du_norm_bwd([(dkv_raw, p["w_kv"], True)], h, p["kv_norm"], dout, nm + "_du",
                                          hook=io.hook(nm + "_du"))
    return dh, dh_bf, g


def _attn_scores(q_ref, kp_ref, kc_ref, vp_ref, vc_ref, qn_ref, bias_ref, sink_ref, kv):
    hd = ATT_HEAD_DIM
    blk = ATT_WINDOW
    sl = slice(kv * hd, (kv + 1) * hd)
    kk = jnp.concatenate([kp_ref[:, sl], kc_ref[:, sl]], axis=0)
    vv = jnp.concatenate([vp_ref[:, sl], vc_ref[:, sl]], axis=0)
    gq = qn_ref[...]
    raws, rinvs = [], []
    for r in range(ATT_GROUP):
        hh = kv * ATT_GROUP + r
        x = q_ref[:, hh * hd:(hh + 1) * hd]
        raws.append(x)
        rinvs.append(lax.rsqrt(jnp.mean(x * x, axis=-1, keepdims=True) + EPS))
    xh = jnp.concatenate([x * ri for x, ri in zip(raws, rinvs)], axis=0)
    rinv = jnp.concatenate(rinvs, axis=0)
    q8 = xh * gq
    s = _dot(q8, kk, NT) * (hd ** -0.5) + bias_ref[kv]
    colk = lax.broadcasted_iota(jnp.int32, (1, 2 * blk), 1)
    s = jnp.where((pl.program_id(0) > 0) | (colk >= blk), s, NEG)
    sink = sink_ref[kv]
    m = jnp.maximum(jnp.max(s, axis=-1, keepdims=True), sink)
    pexp = jnp.exp(s - m)
    e_sink = jnp.exp(sink - m)
    inv_den = 1.0 / (jnp.sum(pexp, axis=-1, keepdims=True) + e_sink)
    return kk, vv, xh, rinv, q8, pexp * inv_den, e_sink * inv_den


def _attn_specs(nb):
    blk = ATT_WINDOW
    cur = lambda i: (i, 0)
    prev = lambda i: (jnp.maximum(i - 1, 0), 0)
    return [
        pl.BlockSpec((blk, D_MODEL), cur),
        pl.BlockSpec((blk, KV_W), prev), pl.BlockSpec((blk, KV_W), cur),
        pl.BlockSpec((blk, KV_W), prev), pl.BlockSpec((blk, KV_W), cur),
        pl.BlockSpec((1, ATT_HEAD_DIM), lambda i: (0, 0)),
        pl.BlockSpec((ATT_KV_HEADS, ATT_GROUP * blk, 2 * blk), lambda i: (0, 0, 0)),
        pl.BlockSpec((ATT_KV_HEADS, ATT_GROUP * blk, 1), lambda i: (0, 0, 0)),
    ]


def attn_fwd(q_raw, k, v, q_norm, bias, sink_col, nm):
    t_dim = q_raw.shape[0]
    blk, hd = ATT_WINDOW, ATT_HEAD_DIM
    nb = t_dim // blk

    def body(q_ref, kp_ref, kc_ref, vp_ref, vc_ref, qn_ref, bias_ref, sink_ref, o_ref):
        for kv in range(ATT_KV_HEADS):
            kk, vv, xh, rinv, q8, prob, p_sink = _attn_scores(q_ref, kp_ref, kc_ref, vp_ref, vc_ref, qn_ref,
                                                              bias_ref, sink_ref, kv)
            o8 = _dot(prob, vv)
            for r in range(ATT_GROUP):
                hh = kv * ATT_GROUP + r
                o_ref[:, hh * hd:(hh + 1) * hd] = o8[r * blk:(r + 1) * blk].astype(o_ref.dtype)

    out, = pcall(body, name=nm, grid=(nb,), in_specs=_attn_specs(nb),
                 out_specs=[pl.BlockSpec((blk, D_MODEL), lambda i: (i, 0))],
                 out_shape=[jax.ShapeDtypeStruct((t_dim, D_MODEL), BF16)],
                 args=[q_raw, k, k, v, v, q_norm, bias, sink_col])
    return out


def attn_bwd(do, q_raw, k, v, q_norm, bias, sink_col, nm, hook=None):
    t_dim = q_raw.shape[0]
    blk, hd = ATT_WINDOW, ATT_HEAD_DIM
    nb = t_dim // blk
    scale = hd ** -0.5

    def body(do_ref, q_ref, kp_ref, kc_ref, vp_ref, vc_ref, qn_ref, bias_ref, sink_ref,
             dq_ref, dkc_ref, dkp_ref, dvc_ref, dvp_ref, dbias_ref, dsink_ref, dqn_ref):
        @pl.when(pl.program_id(0) == 0)
        def _():
            dbias_ref[...] = jnp.zeros_like(dbias_ref)
            dsink_ref[...] = jnp.zeros_like(dsink_ref)
            dqn_ref[...] = jnp.zeros_like(dqn_ref)

        gq = qn_ref[...]
        for kv in range(ATT_KV_HEADS):
            kk, vv, xh, rinv, q8, prob, p_sink = _attn_scores(q_ref, kp_ref, kc_ref, vp_ref, vc_ref, qn_ref,
                                                              bias_ref, sink_ref, kv)
            do8 = jnp.concatenate([do_ref[:, (kv * ATT_GROUP + r) * hd:(kv * ATT_GROUP + r + 1) * hd]
                                   for r in range(ATT_GROUP)], axis=0)
            dp = _dot(do8, vv, NT)
            delta = jnp.sum(prob * dp, axis=-1, keepdims=True)
            ds = prob * (dp - delta)
            dsink_ref[kv] += -p_sink * delta
            dbias_ref[kv] += ds
            ds_s = ds * scale
            dq8 = _dot(ds_s, kk)
            dkk = _dot(ds_s, q8, TN)
            dvv = _dot(prob, do8, TN)
            dqn_ref[...] += jnp.sum(dq8 * xh, axis=0, keepdims=True)
            dxh = dq8 * gq
            dq_raw8 = rinv * (dxh - xh * jnp.mean(dxh * xh, axis=-1, keepdims=True))
            for r in range(ATT_GROUP):
                hh = kv * ATT_GROUP + r
                dq_ref[:, hh * hd:(hh + 1) * hd] = dq_raw8[r * blk:(r + 1) * blk].astype(dq_ref.dtype)
            sl = slice(kv * hd, (kv + 1) * hd)
            dkp_ref[:, sl] = dkk[:blk]
            dkc_ref[:, sl] = dkk[blk:]
            dvp_ref[:, sl] = dvv[:blk]
            dvc_ref[:, sl] = dvv[blk:]

    cur = lambda i: (i, 0)
    row_spec = pl.BlockSpec((blk, KV_W), cur)
    out_specs = [
        pl.BlockSpec((blk, D_MODEL), cur), row_spec, row_spec, row_spec, row_spec,
        pl.BlockSpec((ATT_KV_HEADS, ATT_GROUP * blk, 2 * blk), lambda i: (0, 0, 0)),
        pl.BlockSpec((ATT_KV_HEADS, ATT_GROUP * blk, 1), lambda i: (0, 0, 0)),
        pl.BlockSpec((1, hd), lambda i: (0, 0)),
    ]
    kvs = jax.ShapeDtypeStruct((t_dim, KV_W), F32)
    out_shape = [
        jax.ShapeDtypeStruct((t_dim, D_MODEL), BF16), kvs, kvs, kvs, kvs,
        jax.ShapeDtypeStruct((ATT_KV_HEADS, ATT_GROUP * blk, 2 * blk), F32),
        jax.ShapeDtypeStruct((ATT_KV_HEADS, ATT_GROUP * blk, 1), F32),
        jax.ShapeDtypeStruct((1, hd), F32),
    ]
    return pcall(body, name=nm, grid=(nb,), in_specs=[pl.BlockSpec((blk, D_MODEL), cur)] + _attn_specs(nb),
                 out_specs=out_specs, out_shape=out_shape,
                 args=[do, q_raw, k, k, v, v, q_norm, bias, sink_col], hook=hook)


def _t5_bucket_np():
    blk = ATT_WINDOW
    qi = np.arange(blk)[:, None] + blk
    kj = np.arange(2 * blk)[None, :]
    dist = qi - kj
    n = np.maximum(dist, 0)
    max_exact = REL_BUCKETS // 2
    nf = np.maximum(n, 1).astype(np.float32)
    large = max_exact + (np.log(nf / max_exact) / math.log(ATT_WINDOW / max_exact)
                         * (REL_BUCKETS - max_exact)).astype(np.int32)
    large = np.minimum(large, REL_BUCKETS - 1)
    bucket = np.where(n < max_exact, n, large)
    in_window = (dist >= 0) & (dist < ATT_WINDOW)
    return bucket, in_window


def attn_block_fwd(h, k, v, p, nm):
    def q_proj(x, gg, w_q):
        uu = _rms_fwd(x, gg).astype(BF16)
        return uu, _dot(uu, w_q)

    u, q_raw = rowmap(q_proj, [h], [p["attn_norm"], p["w_q"]], [(D_MODEL, BF16), (D_MODEL, F32)], tm=RESIDENT_TM,
                      name=nm + "_q")
    o = attn_fwd(q_raw, k, v, p["q_norm"], p["bias"], p["sink_col"], nm + "_core")
    out = mm(o, p["w_o"], res=h, name=nm + "_o")
    return out, (u, q_raw, o)


def attn_block_bwd(h, k, v, p, saved, dout, dout_bf, nm, io):
    u, q_raw, o = saved
    g = {}
    io.put("w_o", mm(o, dout_bf, ta=True, out_dtype=BF16, name=nm + "_dwo", hook=io.hook(nm + "_dwo")))
    do = mm(dout_bf, p["w_o"], tb=True, name=nm + "_do")
    dq_raw, dkc, dkp, dvc, dvp, g["bias"], g["sink_col"], g["q_norm"] = attn_bwd(
        do, q_raw, k, v, p["q_norm"], p["bias"], p["sink_col"], nm + "_dcore", hook=io.hook(nm + "_dcore"))
    io.put("w_q", mm(u, dq_raw, ta=True, out_dtype=BF16, name=nm + "_dwq"))
    dh, dh_bf, g["attn_norm"] = du_norm_bwd([(dq_raw, p["w_q"], True)], h, p["attn_norm"], dout, nm + "_du")
    return dh, dh_bf, g, (dkc, dkp, dvc, dvp)


FFN_TAGS = ["00", "01", "10", "11"]


def local_step(x, target, small, io):
    bucket, in_window = _t5_bucket_np()
    blk = ATT_WINDOW
    w = small

    fnorm = {tag: w["ffn_norm"][int(tag[0]), int(tag[1])][None, :] for tag in FFN_TAGS}
    a_neg = -jnp.exp(w["ssm_a_log"][0])

    def mamba_p():
        w_int = io.w("w_int")
        return dict(ssm_norm=w["ssm_norm"], w_zt=w_int[:SSM_D_INNER],
                    w_xbct=w_int[SSM_D_INNER:SSM_D_INNER + SSM_CONV_DIM], w_dtt=w_int[SSM_D_INNER + SSM_CONV_DIM:],
                    conv_w=w["ssm_conv_w"][0], conv_b=w["ssm_conv_b"], dt_bias=w["ssm_dt_bias"],
                    a_heads=a_neg[None, :], a_row=a_neg.reshape(SSM_GROUPS, 1, HPG),
                    a_col=a_neg.reshape(SSM_GROUPS, HPG, 1),
                    dvec=jnp.repeat(w["ssm_d"][0], SSM_HEAD_DIM)[None, :],
                    gate_norm=w["ssm_gate_norm"], w_out=io.w("w_out"))

    rb = w["rel_bias"]
    onehot3 = (np.arange(REL_BUCKETS)[:, None, None] == bucket[None]).astype(np.float32)
    bias = jnp.einsum("bh,bqk->hqk", rb, onehot3, precision=lax.Precision.HIGHEST)
    bias = jnp.where(in_window[None], bias, NEG)
    bias = bias.reshape(ATT_KV_HEADS, ATT_GROUP * blk, 2 * blk)
    sink_col = jnp.repeat(w["sinks"][0], blk).reshape(ATT_KV_HEADS, ATT_GROUP * blk, 1)

    def attn_p():
        return dict(attn_norm=w["attn_norm"], w_q=io.w("w_q"), q_norm=w["q_norm"], bias=bias, sink_col=sink_col,
                    w_o=io.w("w_o"))

    def kv_p():
        return dict(kv_norm=w["kv_norm"][None, :], w_kv=io.w("w_kv"), k_norm=w["k_norm"][None, :])

    h0 = x
    h0a, s_f00 = ffn_fwd(h0, fnorm["00"], "00", io)
    mp = mamba_p()
    h0b, s_m = mamba_fwd(h0a, mp, "ssm", io)
    h1, s_f01 = ffn_fwd(h0b, fnorm["01"], "01", io)
    kp = kv_p()
    k, v, s_kv = kv_fwd(h1, kp, "kv")
    h1a, s_f10 = ffn_fwd(h1, fnorm["10"], "10", io)
    ap = attn_p()
    h1b, s_a = attn_block_fwd(h1a, k, v, ap, "att")
    (dh, dh_bf, sq), s_f11 = ffn_fwd(h1b, fnorm["11"], "11", io, target=target)
    loss_part = jnp.sum(sq) * (0.5 / D_MODEL)

    fg = {}

    def ffn_back(tag, h_in, saved, dh, dh_bf):
        dh, dh_bf, dg = ffn_bwd(h_in, fnorm[tag], tag, saved, dh, dh_bf, io)
        fg[tag] = dg[0]
        return dh, dh_bf

    dh, dh_bf = ffn_back("11", h1b, s_f11, dh, dh_bf)
    dh, dh_bf, ga, dkv = attn_block_bwd(h1a, k, v, ap, s_a, dh, dh_bf, "att", io)
    dh, dh_bf = ffn_back("10", h1, s_f10, dh, dh_bf)
    dh, dh_bf, gk = kv_bwd(h1, kp, s_kv, *dkv, dh, "kv", io)
    dh, dh_bf = ffn_back("01", h0b, s_f01, dh, dh_bf)
    dh, dh_bf, gm = mamba_bwd(h0a, mp, s_m, dh, dh_bf, "ssm", io)
    dh, dh_bf = ffn_back("00", h0, s_f00, dh, dh_bf)
    grad_x = dh

    grads = {}
    grads["ffn_norm"] = jnp.stack([fg[tag] for tag in FFN_TAGS]).reshape(2, 2, D_MODEL)
    grads["ssm_norm"] = gm["ssm_norm"]
    grads["ssm_conv_w"] = gm["conv_w"][None]
    grads["ssm_conv_b"] = gm["conv_b"]
    grads["ssm_dt_bias"] = gm["dt_bias"]
    grads["ssm_a_log"] = gm["a"] * a_neg[None, :]
    grads["ssm_d"] = jnp.sum(gm["dvec"].reshape(SSM_HEADS, SSM_HEAD_DIM), axis=1)[None, :]
    grads["ssm_gate_norm"] = gm["gate_norm"]
    grads["kv_norm"] = gk["kv_norm"][0]
    grads["k_norm"] = gk["k_norm"][0]
    grads["attn_norm"] = ga["attn_norm"]
    grads["q_norm"] = ga["q_norm"]
    grads["sinks"] = jnp.sum(ga["sink_col"].reshape(ATT_HEADS, blk), axis=1)[None, :]
    onehot = (np.arange(REL_BUCKETS)[:, None] == bucket.reshape(1, -1)).astype(np.float32)
    dbias2d = ga["bias"].reshape(ATT_HEADS, blk * 2 * blk)
    grads["rel_bias"] = mm(jnp.asarray(onehot, BF16), dbias2d, tb=True, name="drelbias")
    return loss_part, grad_x, grads


def _adamw(g, w, m, v):
    m = ADAM_B1 * m + (1.0 - ADAM_B1) * g
    v = ADAM_B2 * v + (1.0 - ADAM_B2) * (g * g)
    m_hat = m / (1.0 - ADAM_B1 ** ADAM_STEP)
    v_hat = v / (1.0 - ADAM_B2 ** ADAM_STEP)
    delta = -ADAM_LR * (m_hat / (jnp.sqrt(v_hat) + ADAM_EPS) + ADAM_WD * w)
    return delta, m, v


def _slot_sum(r):
    g = r[0].astype(F32)
    for d in range(1, r.shape[0]):
        g = g + r[d].astype(F32)
    return g


def adamw_rows(recvs, w, m, v, name):
    n_l, rows, width = w.shape
    n_slots = recvs[0].shape[0]
    tr = 32
    assert rows % tr == 0, rows
    nt = rows // tr

    def body(*refs):
        r_refs = refs[:n_l]
        w_ref, m_ref, v_ref, g_o, d_o, m_o, v_o = refs[n_l:]
        li = pl.program_id(0)
        for k in range(n_l):
            @pl.when(li == k)
            def _(k=k):
                g = _slot_sum(r_refs[k])
                delta, m2, v2 = _adamw(g, w_ref[0], m_ref[0], v_ref[0])
                g_o[0] = g
                d_o[0] = delta
                m_o[0] = m2
                v_o[0] = v2

    def r_spec(k):
        return pl.BlockSpec((n_slots, tr, width),
                            lambda li, j: (0, jnp.where(li == k, j, jnp.where(li > k, nt - 1, 0)), 0))

    w_spec = pl.BlockSpec((1, tr, width), lambda li, j: (li, j, 0))
    shp = jax.ShapeDtypeStruct(w.shape, F32)
    return pcall(body, name=name, grid=(n_l, nt), in_specs=[r_spec(k) for k in range(n_l)] + [w_spec] * 3,
                 out_specs=[w_spec] * 4, out_shape=[shp] * 4, args=list(recvs) + [w, m, v])


def adamw_cols(recvs, w, m, v, name):
    n_l, rows, n = w.shape
    n_slots = recvs[0].shape[0]
    tr = 256
    nt = rows // tr

    def body(*refs):
        r_refs = refs[:n_l]
        w_ref, m_ref, v_ref, g_o, d_o, m_o, v_o = refs[n_l:]
        li = pl.program_id(0)
        for k in range(n_l):
            @pl.when(li == k)
            def _(k=k):
                g = _slot_sum(r_refs[k]).T
                delta, m2, v2 = _adamw(g, w_ref[0], m_ref[0], v_ref[0])
                g_o[0] = g
                d_o[0] = delta
                m_o[0] = m2
                v_o[0] = v2

    def r_spec(k):
        return pl.BlockSpec((n_slots, n, tr),
                            lambda li, j: (0, 0, jnp.where(li == k, j, jnp.where(li > k, nt - 1, 0))))

    w_spec = pl.BlockSpec((1, tr, n), lambda li, j: (li, j, 0))
    shp = jax.ShapeDtypeStruct(w.shape, F32)
    return pcall(body, name=name, grid=(n_l, nt), in_specs=[r_spec(k) for k in range(n_l)] + [w_spec] * 3,
                 out_specs=[w_spec] * 4, out_shape=[shp] * 4, args=list(recvs) + [w, m, v])


WEIGHT_NAMES = ["ffn_norm", "ffn_w1", "ffn_w3", "ffn_w2", "ssm_norm", "ssm_w_in", "ssm_conv_w", "ssm_conv_b",
                "ssm_dt_bias", "ssm_a_log", "ssm_d", "ssm_gate_norm", "ssm_w_out", "kv_norm", "w_kv", "k_norm",
                "attn_norm", "w_q", "q_norm", "sinks", "w_o", "rel_bias"]

SMALL = [
    ("ffn_norm", (2, 2, 1024), 2), ("ssm_norm", (1, 1024), 1), ("ssm_conv_w", (1, 4, 3072), 2),
    ("ssm_conv_b", (1, 3072), 1), ("ssm_gate_norm", (1, 2048), 1),
    ("ssm_dt_bias", (1, 32), None), ("ssm_a_log", (1, 32), None), ("ssm_d", (1, 32), None),
    ("kv_norm", (1024,), None), ("k_norm", (64,), None), ("attn_norm", (1, 1024), None),
    ("q_norm", (1, 64), None), ("sinks", (1, 16), None), ("rel_bias", (32, 16), None),
]
SMALL_W = 1024
SMALL_FULL_ROWS = 32
SMALL_LOCAL_ROWS = 48

MAT_GROUPS = {
    "f00_up": ["w1t_00", "w3t_00"], "f00_down": ["w2_00"], "f01": ["w1t_01", "w3t_01", "w2_01"],
    "f10": ["w1t_10", "w3t_10", "w2_10"], "f11": ["w1t_11", "w3t_11", "w2_11"],
    "ssm": ["w_int", "w_out"], "att": ["w_q", "w_o", "w_kv"],
    "f00_early": ["w2_00", "w1t_00"], "f00_late": ["w3t_00"],
}
FIRST_GATHER = "f00_up"
GATHER_PLAN = {"f00_upgate": ["f00_down", "ssm"], "ssm_in": ["f01"], "ssm_ssd": ["att", "f10"],
               "f01_upgate": ["f11"]}
SCATTER_A_PLAN = {"att_dwo": "f11", "kv_dknorm": "f10", "kv_du": "att", "f01_du": "f01", "ssm_du": "ssm",
                  "f00_dw3": "f00_early", "f00_du": "f00_late"}
SCATTER_B_PLAN = {"att_dcore": "f11", "f01_dw2": "att", "f01_dgate": "f10", "ssm_dssd": "f01", "f00_dgate": "ssm",
                  "f00_du": "f00_early"}
LAST_SCATTER = "f00_late"
SLOT_MAJOR = ("w_int",)


def _shard_shape(s, a):
    return s[:a] + (s[a] // N_DEV,) + s[a + 1:]


def _unshard_view(stack, shard_shape, axis):
    moved = jnp.moveaxis(stack, 0, axis)
    return moved.reshape(shard_shape[:axis] + (N_DEV * shard_shape[axis],) + shard_shape[axis + 1:])


def _small_local(arrs):
    flat = jnp.concatenate([arrs[n].reshape(-1) for n, _, _ in SMALL])
    return jnp.pad(flat, (0, SMALL_LOCAL_ROWS * LANES - flat.shape[0])).reshape(SMALL_LOCAL_ROWS, LANES)


def chip_partial(g4, ra, name):
    _, _, n, width = g4.shape

    def body(g_ref, r_ref, o_ref):
        core = lax.axis_index("c")
        own = g_ref[0, pl.ds(core, 1)]
        o_ref[0] = (own[0].astype(F32) + r_ref[0, 0].astype(F32)).astype(o_ref.dtype)

    out, = pcall(body, name=name, grid=(N_CHIPS,),
                 in_specs=[pl.BlockSpec((1, 2, n, width), lambda q: (q, 0, 0, 0)),
                           pl.BlockSpec((1, 1, n, width), lambda q: (q, 0, 0, 0))],
                 out_specs=[pl.BlockSpec((1, n, width), lambda q: (q, 0, 0))],
                 out_shape=[jax.ShapeDtypeStruct((N_CHIPS, n, width), g4.dtype)], args=[g4, ra])
    return out


class StepIO:
    def __init__(self, pieces):
        self.pieces = pieces
        self.full = {}
        self.grad = {}
        self.from_sibling = {}
        self.recv = {}

    def w(self, name):
        return self.full[name]

    def put(self, name, g):
        self.grad[name] = g

    def _by_chip_core(self, name):
        g = self.grad[name]
        return g.reshape((N_CHIPS, 2, g.shape[0] // N_DEV) + g.shape[1:])

    def gather_items(self, groups):
        names = [n for grp in groups for n in MAT_GROUPS[grp]]
        items = [("g2", self.pieces[n], None if n in SLOT_MAJOR else 0) for n in names]

        def sink(outs):
            for n, o in zip(names, outs):
                self.full[n] = o.reshape((-1,) + o.shape[2:]) if n in SLOT_MAJOR else o

        return items, sink

    def scatter_a_items(self, group):
        names = MAT_GROUPS[group]
        items = [("sa", self._by_chip_core(n), None) for n in names]

        def sink(outs):
            for n, o in zip(names, outs):
                self.from_sibling[n] = o

        return items, sink

    def scatter_b_items(self, group):
        names = MAT_GROUPS[group]
        items = [("sb", chip_partial(self._by_chip_core(n), self.from_sibling[n], "partial_" + n), None)
                 for n in names]

        def sink(outs):
            for n, o in zip(names, outs):
                self.recv[n] = o

        return items, sink

    def hook(self, site):
        parts = []
        if site in GATHER_PLAN:
            parts.append(self.gather_items(GATHER_PLAN[site]))
        if site in SCATTER_A_PLAN:
            parts.append(self.scatter_a_items(SCATTER_A_PLAN[site]))
        if site in SCATTER_B_PLAN:
            parts.append(self.scatter_b_items(SCATTER_B_PLAN[site]))
        if not parts:
            return None
        return combine_hooks(parts)


def combine_hooks(parts):
    items = [it for its, _ in parts for it in its]

    def sink(outs):
        p = 0
        for its, snk in parts:
            snk(outs[p:p + len(its)])
            p += len(its)

    return Comm(items), sink


def step(x, target, wts, ms, vs):
    me = _my_index()

    pieces = {}
    for li in range(2):
        for hi in range(2):
            tag = "%d%d" % (li, hi)
            pieces["w1t_" + tag] = wts["ffn_w1"][li, hi].T.astype(BF16)
            pieces["w3t_" + tag] = wts["ffn_w3"][li, hi].T.astype(BF16)
            pieces["w2_" + tag] = wts["ffn_w2"][li, hi].astype(BF16)
    pieces["w_int"] = wts["ssm_w_in"][0].T.astype(BF16)
    pieces["w_out"] = wts["ssm_w_out"][0].astype(BF16)
    pieces["w_kv"] = wts["w_kv"].astype(BF16)
    pieces["w_q"] = wts["w_q"][0].astype(BF16)
    pieces["w_o"] = wts["w_o"][0].astype(BF16)
    io = StepIO(pieces)

    small_sharded = [(n, s, a) for n, s, a in SMALL if a is not None]
    loc = jnp.concatenate([wts[n].reshape(-1) for n, _, _ in small_sharded])
    loc_rows = -(-loc.shape[0] // (8 * LANES)) * 8
    loc = jnp.pad(loc, (0, loc_rows * LANES - loc.shape[0])).reshape(loc_rows, LANES)
    got_small = []
    comm, sink = combine_hooks([io.gather_items([FIRST_GATHER]), ([("g", loc, None)], got_small.extend)])
    sink(comm_only(comm, "gather_first"))
    gath_small = got_small[0].reshape(N_DEV, -1)
    small = {}
    off = 0
    for n, s, a in small_sharded:
        shard = _shard_shape(s, a)
        cnt = int(np.prod(shard))
        small[n] = _unshard_view(gath_small[:, off:off + cnt].reshape((N_DEV,) + shard), shard, a)
        off += cnt
    for n, s, a in SMALL:
        if a is None:
            small[n] = wts[n]

    loss_part, grad_x, g_small_local = local_step(x[0], target[0], small, io)
    loss = lax.psum(loss_part, ("x", "y", "c"))

    small_flat = jnp.concatenate([g_small_local[n].reshape(-1) for n, _, _ in SMALL])
    small_buf = jnp.pad(small_flat, (0, SMALL_FULL_ROWS * SMALL_W - small_flat.shape[0]))
    small_buf = small_buf.reshape(SMALL_FULL_ROWS, SMALL_W)
    got_small = []
    comm, sink = combine_hooks([io.scatter_b_items(LAST_SCATTER), ([("g", small_buf, None)], got_small.extend)])
    sink(comm_only(comm, "exchange_last"))
    small_all = got_small[0]

    def sum_body(r_ref, o_ref):
        o_ref[...] = _slot_sum(r_ref)

    vmem = pl.BlockSpec(memory_space=pltpu.VMEM)
    small_sum, = pcall(sum_body, name="sum_small", grid=(), in_specs=[vmem], out_specs=[vmem],
                       out_shape=[jax.ShapeDtypeStruct((SMALL_FULL_ROWS, SMALL_W), F32)], args=[small_all])
    small_sum = small_sum.reshape(-1)
    g_small = {}
    off = 0
    for n, s, a in SMALL:
        cnt = int(np.prod(s))
        gfull = small_sum[off:off + cnt].reshape(s)
        off += cnt
        if a is None:
            g_small[n] = gfull
        else:
            width = s[a] // N_DEV
            g_small[n] = lax.dynamic_slice_in_dim(gfull, me * width, width, axis=a)

    out = {}

    def emit(name, res, shape):
        for kind, arr in zip(("grad", "delta", "new_m", "new_v"), res):
            out[kind + "_" + name] = arr.reshape(shape)

    for name, key in (("ffn_w1", "w1t_"), ("ffn_w3", "w3t_")):
        shp = wts[name].shape
        view = lambda t: t.reshape((4,) + shp[2:])
        res = adamw_cols([io.recv[key + tag] for tag in FFN_TAGS], view(wts[name]), view(ms[name]), view(vs[name]),
                         "adamw_" + name)
        emit(name, res, shp)
    shp = wts["ffn_w2"].shape
    view = lambda t: t.reshape((4,) + shp[2:])
    res = adamw_rows([io.recv["w2_" + tag] for tag in FFN_TAGS], view(wts["ffn_w2"]), view(ms["ffn_w2"]),
                     view(vs["ffn_w2"]), "adamw_ffn_w2")
    emit("ffn_w2", res, shp)
    res = adamw_cols([io.recv["w_int"]], wts["ssm_w_in"], ms["ssm_w_in"], vs["ssm_w_in"], "adamw_ssm_w_in")
    emit("ssm_w_in", res, wts["ssm_w_in"].shape)
    for name, key in (("ssm_w_out", "w_out"), ("w_kv", "w_kv"), ("w_q", "w_q"), ("w_o", "w_o")):
        shp = wts[name].shape
        view = lambda t: t.reshape((1,) + shp[-2:])
        res = adamw_rows([io.recv[key]], view(wts[name]), view(ms[name]), view(vs[name]), "adamw_" + name)
        emit(name, res, shp)

    res_s = rowmap(lambda gg, ww, mm_, vv: _adamw(gg, ww, mm_, vv),
                   [_small_local(g_small), _small_local(wts), _small_local(ms), _small_local(vs)], [],
                   [(LANES, F32)] * 3, tm=SMALL_LOCAL_ROWS, name="adamw_small")
    flat_s = [r.reshape(-1) for r in res_s]
    off = 0
    for n, s, a in SMALL:
        shard = s if a is None else _shard_shape(s, a)
        cnt = int(np.prod(shard))
        out["grad_" + n] = g_small[n]
        for kind, arr in zip(("delta", "new_m", "new_v"), flat_s):
            out[kind + "_" + n] = arr[off:off + cnt].reshape(shard)
        off += cnt
    out["loss"] = loss
    out["grad_x"] = grad_x[None]
    return out


def kernel(x, ffn_norm, ffn_w1, ffn_w3, ffn_w2, ssm_norm, ssm_w_in, ssm_conv_w, ssm_conv_b, ssm_dt_bias, ssm_a_log, ssm_d, ssm_gate_norm, ssm_w_out, kv_norm, w_kv, k_norm, attn_norm, w_q, q_norm, sinks, w_o, rel_bias, loss_target, m_ffn_norm, m_ffn_w1, m_ffn_w3, m_ffn_w2, m_ssm_norm, m_ssm_w_in, m_ssm_conv_w, m_ssm_conv_b, m_ssm_dt_bias, m_ssm_a_log, m_ssm_d, m_ssm_gate_norm, m_ssm_w_out, m_kv_norm, m_w_kv, m_k_norm, m_attn_norm, m_w_q, m_q_norm, m_sinks, m_w_o, m_rel_bias, v_ffn_norm, v_ffn_w1, v_ffn_w3, v_ffn_w2, v_ssm_norm, v_ssm_w_in, v_ssm_conv_w, v_ssm_conv_b, v_ssm_dt_bias, v_ssm_a_log, v_ssm_d, v_ssm_gate_norm, v_ssm_w_out, v_kv_norm, v_w_kv, v_k_norm, v_attn_norm, v_w_q, v_q_norm, v_sinks, v_w_o, v_rel_bias):
    args = locals()
    wts = {n: args[n] for n in WEIGHT_NAMES}
    ms = {n: args["m_" + n] for n in WEIGHT_NAMES}
    vs = {n: args["v_" + n] for n in WEIGHT_NAMES}
    out = step(x, loss_target, wts, ms, vs)
    result = [out["loss"], out["grad_x"]]
    for kind in ("grad", "delta", "new_m", "new_v"):
        result += [out[kind + "_" + n] for n in WEIGHT_NAMES]
    return tuple(result)
```

```python
import functools
import math
import operator

import numpy as np
import jax
import jax.numpy as jnp
from jax import lax
from jax.experimental import pallas as pl
from jax.experimental.pallas import tpu as pltpu

F32 = jnp.float32
BF16 = jnp.bfloat16

D_MODEL = 1024
D_FF = 2816
N_DEV = 8
SSM_D_INNER = 2048
SSM_HEAD_DIM = 64
SSM_HEADS = 32
SSM_GROUPS = 4
SSM_STATE = 128
SSM_CONV = 4
SSM_CHUNK = 256
SSM_CONV_DIM = SSM_D_INNER + 2 * SSM_GROUPS * SSM_STATE
SSM_IN_DIM = SSM_D_INNER + SSM_CONV_DIM + SSM_HEADS
ATT_HEAD_DIM = 64
ATT_HEADS = 16
ATT_KV_HEADS = 2
ATT_GROUP = 8
ATT_WINDOW = 128
REL_BUCKETS = 32
EPS = 1e-6
NEG = -1e30

ADAM_LR = 0.001
ADAM_B1 = 0.9
ADAM_B2 = 0.999
ADAM_EPS = 1e-08
ADAM_WD = 0.01
ADAM_STEP = 10

VMEM_LIMIT_BYTES = 52 * 1024 * 1024
LANES = 128
MESH_ID = pl.DeviceIdType.MESH
ANY_SPEC = pl.BlockSpec(memory_space=pl.ANY)

NT = (((1,), (1,)), ((), ()))
TN = (((0,), (0,)), ((), ()))
NN = (((1,), (0,)), ((), ()))


def _pick(dim, cands):
    for c in cands:
        if dim % c == 0:
            return c
    return dim


def _my_index():
    return 4 * lax.axis_index("x") + 2 * lax.axis_index("y") + lax.axis_index("c")


def _peer(k):
    x, y, c = lax.axis_index("x"), lax.axis_index("y"), lax.axis_index("c")
    px = 1 - x if (k >> 2) & 1 else x
    py = 1 - y if (k >> 1) & 1 else y
    pc = 1 - c if k & 1 else c
    return (px, py, pc), 4 * px + 2 * py + pc


def _piece(ref, axis, d, n):
    if axis is None:
        return ref.at[d]
    return ref.at[(slice(None),) * axis + (pl.ds(pl.multiple_of(d * n, 8), n),)]


SIBLING = 1
CHIP_PEERS = (4, 2, 6)
N_CHIPS = 4
SEMS_PER_ITEM = N_DEV - 1


def _my_chip():
    return 2 * lax.axis_index("x") + lax.axis_index("y")


class Comm:
    def __init__(self, items):
        self.items = list(items)

    def dst_shapes(self):
        out = []
        for kind, src, axis in self.items:
            s = tuple(src.shape)
            if kind == "g":
                shp = (N_DEV,) + s
            elif kind == "g2":
                shp = (N_DEV,) + s if axis is None else s[:axis] + (N_DEV * s[axis],) + s[axis + 1:]
            elif kind == "sa":
                shp = (s[0], 1) + s[2:]
            else:
                shp = s
            out.append(jax.ShapeDtypeStruct(shp, src.dtype))
        return out

    def scratch(self):
        n = len(self.items)
        return [pltpu.SemaphoreType.DMA((n * SEMS_PER_ITEM,)), pltpu.SemaphoreType.DMA((n * SEMS_PER_ITEM,)),
                pltpu.SemaphoreType.DMA((n,))]

    def _run(self, srcs, dsts, sems, starting):
        send_sems, recv_sems, local_sems = sems
        me = _my_index()
        core = lax.axis_index("c")
        chip = _my_chip()
        for i, (kind, src, axis) in enumerate(self.items):
            s_ref, d_ref = srcs[i], dsts[i]
            base = i * SEMS_PER_ITEM

            def rdma(src_ref, dst_ref, j, peer):
                return pltpu.make_async_remote_copy(
                    src_ref=src_ref, dst_ref=dst_ref, send_sem=send_sems.at[base + j], recv_sem=recv_sems.at[base + j],
                    device_id=peer, device_id_type=MESH_ID)

            if kind == "g":
                local = pltpu.make_async_copy(s_ref, d_ref.at[me], local_sems.at[i])
                outs = [rdma(s_ref, d_ref.at[me], k - 1, _peer(k)[0]) for k in range(1, N_DEV)]
                if starting:
                    local.start()
                    for cp in outs:
                        cp.start()
                else:
                    for k in range(1, N_DEV):
                        rdma(s_ref, d_ref.at[_peer(k)[1]], k - 1, _peer(k)[0]).wait_recv()
                    for cp in outs:
                        cp.wait_send()
                    local.wait()
            elif kind == "g2":
                n = None if axis is None else src.shape[axis]
                mine = _piece(d_ref, axis, me, n)
                sib = _peer(SIBLING)[0]
                local = pltpu.make_async_copy(s_ref, mine, local_sems.at[i])
                outs = [rdma(s_ref, mine, 0, sib)] + [rdma(s_ref, mine, 1 + j, _peer(k)[0])
                                                      for j, k in enumerate(CHIP_PEERS)]
                if starting:
                    local.start()
                    for cp in outs:
                        cp.start()
                else:
                    passed = []
                    for j, k in enumerate(CHIP_PEERS):
                        theirs = _piece(d_ref, axis, _peer(k)[1], n)
                        rdma(s_ref, theirs, 1 + j, _peer(k)[0]).wait_recv()
                        fwd = rdma(theirs, theirs, 4 + j, sib)
                        fwd.start()
                        passed.append(fwd)
                    rdma(s_ref, _piece(d_ref, axis, _peer(SIBLING)[1], n), 0, sib).wait_recv()
                    for j, k in enumerate(CHIP_PEERS):
                        rdma(s_ref, _piece(d_ref, axis, _peer(k ^ SIBLING)[1], n), 4 + j, sib).wait_recv()
                    for cp in outs + passed:
                        cp.wait_send()
                    local.wait()
            elif kind == "sa":
                cp = rdma(s_ref.at[(slice(None), pl.ds(1 - core, 1))], d_ref, 0, _peer(SIBLING)[0])
                if starting:
                    cp.start()
                else:
                    cp.wait_recv()
                    cp.wait_send()
            else:
                local = pltpu.make_async_copy(s_ref.at[chip], d_ref.at[chip], local_sems.at[i])
                outs = [rdma(s_ref.at[_peer(k)[1] >> 1], d_ref.at[chip], 1 + j, _peer(k)[0])
                        for j, k in enumerate(CHIP_PEERS)]
                if starting:
                    local.start()
                    for cp in outs:
                        cp.start()
                else:
                    for j, k in enumerate(CHIP_PEERS):
                        rdma(s_ref.at[chip], d_ref.at[_peer(k)[1] >> 1], 1 + j, _peer(k)[0]).wait_recv()
                    for cp in outs:
                        cp.wait_send()
                    local.wait()

    def start(self, srcs, dsts, sems):
        self._run(srcs, dsts, sems, True)

    def wait(self, srcs, dsts, sems):
        self._run(srcs, dsts, sems, False)


def pcall(body, *, name, grid, in_specs, out_specs, out_shape, args, scratch=(), hook=None):
    cparams = pltpu.CompilerParams(dimension_semantics=("arbitrary",) * len(grid), vmem_limit_bytes=VMEM_LIMIT_BYTES)
    if hook is None:
        outs = pl.pallas_call(body, name=name, grid=grid, in_specs=list(in_specs), out_specs=list(out_specs),
                              out_shape=list(out_shape), scratch_shapes=list(scratch), compiler_params=cparams)(*args)
        return list(outs)
    comm, sink = hook
    n_in, n_out, n_scr, n_it = len(args), len(out_shape), len(scratch), len(comm.items)
    dims = tuple(grid)

    def wrapped(*refs):
        p = 0
        ins = refs[p:p + n_in]
        p += n_in
        csrc = refs[p:p + n_it]
        p += n_it
        outs = refs[p:p + n_out]
        p += n_out
        cdst = refs[p:p + n_it]
        p += n_it
        scr = refs[p:p + n_scr]
        p += n_scr
        sems = refs[p:p + 3]
        if dims:
            ids = [pl.program_id(a) for a in range(len(dims))]
            first = functools.reduce(operator.and_, [i == 0 for i in ids])
            last = functools.reduce(operator.and_, [i == d - 1 for i, d in zip(ids, dims)])

            @pl.when(first)
            def _():
                comm.start(csrc, cdst, sems)

            body(*ins, *outs, *scr)

            @pl.when(last)
            def _():
                comm.wait(csrc, cdst, sems)
        else:
            comm.start(csrc, cdst, sems)
            body(*ins, *outs, *scr)
            comm.wait(csrc, cdst, sems)

    res = pl.pallas_call(
        wrapped, name=name, grid=grid,
        in_specs=list(in_specs) + [ANY_SPEC] * n_it, out_specs=list(out_specs) + [ANY_SPEC] * n_it,
        out_shape=list(out_shape) + comm.dst_shapes(), scratch_shapes=list(scratch) + comm.scratch(),
        compiler_params=cparams,
    )(*args, *[src for _, src, _ in comm.items])
    res = list(res)
    sink(res[n_out:])
    return res[:n_out]


def comm_only(comm, name):
    got = []
    pcall(lambda *refs: None, name=name, grid=(), in_specs=[], out_specs=[], out_shape=[], args=[],
          hook=(comm, got.extend))
    return got


def mm(a, b, *, ta=False, tb=False, out_dtype=F32, res=None, alpha=1.0, name, hook=None):
    if ta:
        k_dim, m_dim = a.shape
    else:
        m_dim, k_dim = a.shape
    if tb:
        n_dim, k2 = b.shape
    else:
        k2, n_dim = b.shape
    assert k_dim == k2, (a.shape, b.shape, ta, tb)
    tn = _pick(n_dim, (1024, 1408, 512, 256, 128))
    tm = _pick(m_dim, (1024, 1408, 512, 256, 128)) if tn <= 1024 else _pick(m_dim, (512, 256, 128))
    if not ta and m_dim % 2048 == 0 and tn == 1024:
        tm, tn = 2048, 512
    tk = _pick(k_dim, (1024, 512, 256, 128)) if ta else _pick(k_dim, (512, 1408, 256, 128))
    nk = k_dim // tk
    has_res = res is not None
    dn = (((0 if ta else 1,), (1 if tb else 0,)), ((), ()))

    def body(*refs):
        if has_res:
            a_ref, b_ref, r_ref, o_ref, acc_ref = refs
        else:
            a_ref, b_ref, o_ref, acc_ref = refs
        k = pl.program_id(2)

        @pl.when(k == 0)
        def _():
            acc_ref[...] = jnp.zeros_like(acc_ref)

        acc_ref[...] += lax.dot_general(a_ref[...].astype(BF16), b_ref[...].astype(BF16), dn,
                                        preferred_element_type=F32)

        @pl.when(k == nk - 1)
        def _():
            r = acc_ref[...]
            if alpha != 1.0:
                r = r * alpha
            if has_res:
                r = r_ref[...] + r
            o_ref[...] = r.astype(o_ref.dtype)

    a_spec = pl.BlockSpec((tk, tm), lambda i, j, k: (k, i)) if ta else pl.BlockSpec((tm, tk), lambda i, j, k: (i, k))
    b_spec = pl.BlockSpec((tn, tk), lambda i, j, k: (j, k)) if tb else pl.BlockSpec((tk, tn), lambda i, j, k: (k, j))
    o_spec = pl.BlockSpec((tm, tn), lambda i, j, k: (i, j))
    in_specs = [a_spec, b_spec] + ([o_spec] if has_res else [])
    args = [a, b] + ([res] if has_res else [])
    out, = pcall(body, name=name, grid=(m_dim // tm, n_dim // tn, nk), in_specs=in_specs, out_specs=[o_spec],
                 out_shape=[jax.ShapeDtypeStruct((m_dim, n_dim), out_dtype)], args=args,
                 scratch=[pltpu.VMEM((tm, tn), F32)], hook=hook)
    return out


def rowmap(fn, rows, consts=(), out_rows=(), out_accs=(), *, tm, name, hook=None):
    first = rows[0][0] if isinstance(rows[0], tuple) else rows[0]
    t_dim = first.shape[0]
    assert t_dim % tm == 0, (t_dim, tm)
    n_r, n_c, n_o = len(rows), len(consts), len(out_rows)

    def body(*refs):
        ins = [r[...] for r in refs[:n_r + n_c]]
        o_refs = refs[n_r + n_c:]
        outs = tuple(fn(*ins))
        for o_ref, val in zip(o_refs[:n_o], outs[:n_o]):
            o_ref[...] = val.astype(o_ref.dtype)
        if out_accs:
            @pl.when(pl.program_id(0) == 0)
            def _():
                for o_ref in o_refs[n_o:]:
                    o_ref[...] = jnp.zeros_like(o_ref)

            for o_ref, val in zip(o_refs[n_o:], outs[n_o:]):
                o_ref[...] += val

    in_specs, args = [], []
    for r in rows:
        if isinstance(r, tuple):
            args.append(r[0])
            in_specs.append(r[1])
        else:
            args.append(r)
            in_specs.append(pl.BlockSpec((tm, r.shape[1]), lambda i: (i, 0)))
    for c in consts:
        args.append(c)
        in_specs.append(pl.BlockSpec(c.shape, lambda i, nd=c.ndim: (0,) * nd))
    out_specs = [pl.BlockSpec((tm, w), lambda i: (i, 0)) for (w, _) in out_rows]
    out_specs += [pl.BlockSpec(s, lambda i, nd=len(s): (0,) * nd) for s in out_accs]
    out_shape = [jax.ShapeDtypeStruct((t_dim, w), dt) for (w, dt) in out_rows]
    out_shape += [jax.ShapeDtypeStruct(s, F32) for s in out_accs]
    return pcall(body, name=name, grid=(t_dim // tm,), in_specs=in_specs, out_specs=out_specs, out_shape=out_shape,
                 args=args, hook=hook)


def _rms_fwd(x, g):
    r = lax.rsqrt(jnp.mean(x * x, axis=-1, keepdims=True) + EPS)
    return x * r * g


def _rms_bwd(x, g, dy):
    r = lax.rsqrt(jnp.mean(x * x, axis=-1, keepdims=True) + EPS)
    xh = x * r
    dg = jnp.sum(dy * xh, axis=0, keepdims=True)
    dxh = dy * g
    dx = r * (dxh - xh * jnp.mean(dxh * xh, axis=-1, keepdims=True))
    return dx, dg


def _sigmoid(x):
    return 1.0 / (1.0 + jnp.exp(-x))


def _silu(x):
    return x * _sigmoid(x)


def _silu_grad(x):
    s = _sigmoid(x)
    return s * (1.0 + x * (1.0 - s))


def _split3(x):
    hi = x.astype(BF16)
    r1 = x - hi.astype(F32)
    mid = r1.astype(BF16)
    lo = (r1 - mid.astype(F32)).astype(BF16)
    return hi, mid, lo


def _dot(a, b, dn=NN):
    return lax.dot_general(a.astype(BF16), b.astype(BF16), dn, preferred_element_type=F32)


FFN_TN = 1408
RESIDENT_TM = 512


def ffn_upgate(h, g, w1t, w3t, nm, hook=None):
    t_dim = h.shape[0]
    tm = _pick(t_dim, (512, 256, 128))
    tn = FFN_TN

    n_j = D_FF // tn
    u_w = D_MODEL // n_j

    def body(h_ref, g_ref, w1_ref, w3_ref, u_ref, a_ref, b_ref, hm_ref):
        uu = _rms_fwd(h_ref[...], g_ref[...]).astype(BF16)
        for j in range(n_j):
            @pl.when(pl.program_id(0) == j)
            def _(j=j):
                u_ref[...] = uu[:, j * u_w:(j + 1) * u_w]

        a = lax.dot_general(uu, w1_ref[...], NT, preferred_element_type=F32)
        b = lax.dot_general(uu, w3_ref[...], NT, preferred_element_type=F32)
        a_ref[...] = a.astype(a_ref.dtype)
        b_ref[...] = b.astype(b_ref.dtype)
        hm_ref[...] = (_silu(a) * b).astype(hm_ref.dtype)

    row_spec = pl.BlockSpec((tm, D_MODEL), lambda j, i: (i, 0))
    w_spec = pl.BlockSpec((tn, D_MODEL), lambda j, i: (j, 0))
    o_spec = pl.BlockSpec((tm, tn), lambda j, i: (i, j))
    o_shape = jax.ShapeDtypeStruct((t_dim, D_FF), BF16)
    return pcall(body, name=nm, grid=(D_FF // tn, t_dim // tm),
                 in_specs=[row_spec, pl.BlockSpec((1, D_MODEL), lambda j, i: (0, 0)), w_spec, w_spec],
                 out_specs=[pl.BlockSpec((tm, u_w), lambda j, i: (i, j))] + [o_spec] * 3,
                 out_shape=[jax.ShapeDtypeStruct((t_dim, D_MODEL), BF16)] + [o_shape] * 3,
                 args=[h, g, w1t, w3t], hook=hook)


def ffn_dgate(dout_bf, w2, a, b, nm, hook=None):
    t_dim = dout_bf.shape[0]
    tm = _pick(t_dim, (512, 256, 128))
    tn = FFN_TN

    def body(d_ref, w2_ref, a_ref, b_ref, da_ref, db_ref):
        dhm = 0.5 * lax.dot_general(d_ref[...], w2_ref[...], NT, preferred_element_type=F32)
        av = a_ref[...].astype(F32)
        bv = b_ref[...].astype(F32)
        sg = _sigmoid(av)
        da_ref[...] = (dhm * bv * (sg * (1.0 + av * (1.0 - sg)))).astype(da_ref.dtype)
        db_ref[...] = (dhm * (av * sg)).astype(db_ref.dtype)

    t_spec = pl.BlockSpec((tm, tn), lambda j, i: (i, j))
    o_shape = jax.ShapeDtypeStruct((t_dim, D_FF), BF16)
    return pcall(body, name=nm, grid=(D_FF // tn, t_dim // tm),
                 in_specs=[pl.BlockSpec((tm, D_MODEL), lambda j, i: (i, 0)),
                           pl.BlockSpec((tn, D_MODEL), lambda j, i: (j, 0)), t_spec, t_spec],
                 out_specs=[t_spec] * 2, out_shape=[o_shape] * 2, args=[dout_bf, w2, a, b], hook=hook)


def ffn_fwd(h, g, tag, io, target=None):
    nm = "f" + tag
    u, a, b, hm = ffn_upgate(h, g, io.w("w1t_" + tag), io.w("w3t_" + tag), nm + "_upgate",
                             hook=io.hook(nm + "_upgate"))
    if target is None:
        return mm(hm, io.w("w2_" + tag), res=h, alpha=0.5, name=nm + "_down"), (u, a, b, hm)

    def down_loss(hmv, hv, t, w2):
        e = hv + 0.5 * _dot(hmv, w2) - t
        d = e * (1.0 / D_MODEL)
        return d, d, jnp.sum(e * e, axis=0, keepdims=True)

    res = rowmap(down_loss, [hm, h, target], [io.w("w2_" + tag)], [(D_MODEL, F32), (D_MODEL, BF16)],
                 [(1, D_MODEL)], tm=RESIDENT_TM, name=nm + "_down_loss")
    return res, (u, a, b, hm)


def du_norm_bwd(pairs, h, g, dout, nm, hook=None):
    t_dim = h.shape[0]
    tm = RESIDENT_TM
    n_p = len(pairs)

    def body(*refs):
        h_ref, d_ref, g_ref = refs[2 * n_p:2 * n_p + 3]
        dh_ref, dhb_ref, dg_ref = refs[2 * n_p + 3:]
        du = None
        for p, (_, _, tb) in enumerate(pairs):
            t = lax.dot_general(refs[2 * p][...].astype(BF16), refs[2 * p + 1][...].astype(BF16), NT if tb else NN,
                                preferred_element_type=F32)
            du = t if du is None else du + t
        dx, dg = _rms_bwd(h_ref[...], g_ref[...], du)
        dh = d_ref[...] + dx
        dh_ref[...] = dh
        dhb_ref[...] = dh.astype(dhb_ref.dtype)

        @pl.when(pl.program_id(0) == 0)
        def _():
            dg_ref[...] = jnp.zeros_like(dg_ref)

        dg_ref[...] += dg

    in_specs, args = [], []
    for a, b, _ in pairs:
        in_specs += [pl.BlockSpec((tm, a.shape[1]), lambda i: (i, 0)), pl.BlockSpec(b.shape, lambda i: (0, 0))]
        args += [a, b]
    row_spec = pl.BlockSpec((tm, D_MODEL), lambda i: (i, 0))
    vec_spec = pl.BlockSpec((1, D_MODEL), lambda i: (0, 0))
    return pcall(body, name=nm, grid=(t_dim // tm,), in_specs=in_specs + [row_spec, row_spec, vec_spec],
                 out_specs=[row_spec, row_spec, vec_spec],
                 out_shape=[jax.ShapeDtypeStruct((t_dim, D_MODEL), F32), jax.ShapeDtypeStruct((t_dim, D_MODEL), BF16),
                            jax.ShapeDtypeStruct((1, D_MODEL), F32)],
                 args=args + [h, dout, g], hook=hook)


def ffn_bwd(h, g, tag, saved, dout, dout_bf, io):
    nm = "f" + tag
    w1t, w3t, w2 = io.w("w1t_" + tag), io.w("w3t_" + tag), io.w("w2_" + tag)
    u, a, b, hm = saved
    io.put("w2_" + tag, mm(hm, dout_bf, ta=True, alpha=0.5, out_dtype=BF16, name=nm + "_dw2",
                           hook=io.hook(nm + "_dw2")))
    da, db = ffn_dgate(dout_bf, w2, a, b, nm + "_dgate", hook=io.hook(nm + "_dgate"))
    io.put("w1t_" + tag, mm(da, u, ta=True, out_dtype=BF16, name=nm + "_dw1"))
    io.put("w3t_" + tag, mm(db, u, ta=True, out_dtype=BF16, name=nm + "_dw3", hook=io.hook(nm + "_dw3")))
    return du_norm_bwd([(da, w1t, False), (db, w3t, False)], h, g, dout, nm + "_du", hook=io.hook(nm + "_du"))


def conv_input_grad(d_parts, w, nm):
    tm = 256
    t_dim = d_parts[0].shape[0]
    n_tiles = t_dim // tm

    def fn(d1, n1, d2, n2, d3, n3, ww):
        d = jnp.concatenate([d1, d2, d3], axis=1)
        nxt = jnp.concatenate([n1, n2, n3], axis=1)
        nxt = jnp.where(pl.program_id(0) < n_tiles - 1, nxt, 0.0)
        dd = jnp.concatenate([d, nxt], axis=0)
        out = dd[3:3 + tm] * ww[0:1]
        for k in range(1, SSM_CONV):
            out = out + dd[3 - k:3 - k + tm] * ww[k:k + 1]
        return (out,)

    rows = []
    for d in d_parts:
        below = pl.BlockSpec((8, d.shape[1]), lambda i: (jnp.minimum((i + 1) * (tm // 8), t_dim // 8 - 1), 0))
        rows += [d, (d, below)]
    dx, = rowmap(fn, rows, [w], [(SSM_CONV_DIM, BF16)], tm=tm, name=nm)
    return dx


GRP_W = SSM_D_INNER // SSM_GROUPS
HPG = SSM_HEADS // SSM_GROUPS
HEAD_SHIFT = 6


def _split2(x):
    hi = x.astype(BF16)
    return hi, (x - hi.astype(F32)).astype(BF16)


def _expand_mats():
    e = ((lax.broadcasted_iota(jnp.int32, (HPG, GRP_W), 1) >> HEAD_SHIFT)
         == lax.broadcasted_iota(jnp.int32, (HPG, GRP_W), 0)).astype(BF16)
    et = ((lax.broadcasted_iota(jnp.int32, (GRP_W, HPG), 0) >> HEAD_SHIFT)
          == lax.broadcasted_iota(jnp.int32, (GRP_W, HPG), 1)).astype(BF16)
    return e, et


def _expand(v, e_m):
    hi, lo = _split2(v)
    return jnp.dot(hi, e_m, preferred_element_type=F32) + jnp.dot(lo, e_m, preferred_element_type=F32)


def _reduce8(v, et_m):
    hi, lo = _split2(v)
    return jnp.dot(hi, et_m, preferred_element_type=F32) + jnp.dot(lo, et_m, preferred_element_type=F32)


def _ssd_group_terms(dt_ref, dtT_ref, arow_ref, acol_ref):
    L = SSM_CHUNK
    r = lax.broadcasted_iota(jnp.int32, (L, L), 0)
    c = lax.broadcasted_iota(jnp.int32, (L, L), 1)
    tril = (r >= c).astype(BF16)
    triu = (r <= c).astype(BF16)
    dtg = dt_ref[0]
    acol = None
    for p in _split3(dtg * arow_ref[0]):
        t = jnp.dot(tril, p, preferred_element_type=F32)
        acol = t if acol is None else acol + t
    arowT = None
    for p in _split3(dtT_ref[0] * acol_ref[0]):
        t = jnp.dot(p, triu, preferred_element_type=F32)
        arowT = t if arowT is None else arowT + t
    return dtg, acol, arowT, r >= c


def _state_decay(a_last_col, et_m):
    hi, lo = _split2(jnp.broadcast_to(jnp.exp(a_last_col), (HPG, SSM_STATE)))
    return jnp.dot(et_m, hi, preferred_element_type=F32) + jnp.dot(et_m, lo, preferred_element_type=F32)


def _conv_block(x_ref, halo_ref, w_ref, b_ref, first):
    L = SSM_CHUNK
    xx = jnp.concatenate([jnp.where(first, 0.0, halo_ref[...]), x_ref[...]], axis=0)
    w = w_ref[...]
    shifted = [xx[5 + k:5 + k + L] for k in range(SSM_CONV)]
    acc = b_ref[...] + shifted[0] * w[0:1]
    for k in range(1, SSM_CONV):
        acc = acc + shifted[k] * w[k:k + 1]
    return acc, shifted


def _ssd_specs(nc, rev):
    L, N = SSM_CHUNK, SSM_STATE
    xcols = SSM_D_INNER // LANES
    ch = (lambda c: nc - 1 - c) if rev else (lambda c: c)
    above = lambda c: jnp.maximum(ch(c) * (L // 8) - 1, 0)
    specs = []
    for width, col in ((GRP_W, lambda g: g), (N, lambda g: xcols + g), (N, lambda g: xcols + SSM_GROUPS + g)):
        specs += [
            pl.BlockSpec((L, width), lambda c, g, col=col: (ch(c), col(g))),
            pl.BlockSpec((8, width), lambda c, g, col=col: (above(c), col(g))),
            pl.BlockSpec((SSM_CONV, width), lambda c, g, col=col: (0, col(g))),
            pl.BlockSpec((1, width), lambda c, g, col=col: (0, col(g))),
        ]
    return specs + [
        pl.BlockSpec((1, L, HPG), lambda c, g: (g, ch(c), 0)),
        pl.BlockSpec((1, HPG, L), lambda c, g: (g, 0, ch(c))),
        pl.BlockSpec((1, 1, HPG), lambda c, g: (g, 0, 0)),
        pl.BlockSpec((1, HPG, 1), lambda c, g: (g, 0, 0)),
        pl.BlockSpec((1, GRP_W), lambda c, g: (0, g)),
    ]


def ssd_fwd(xbc_raw, conv_w, conv_b, dt_g, dtT_g, a_row, a_col, dvec, nm, hook=None):
    t_dim = xbc_raw.shape[0]
    L, P, N = SSM_CHUNK, SSM_HEAD_DIM, SSM_STATE
    nc = t_dim // L

    def body(x_ref, xh_ref, xw_ref, xb_ref, b_ref, bh_ref, bw_ref, bb_ref, c_ref, ch_ref, cw_ref, cb_ref,
             dt_ref, dtT_ref, arow_ref, acol_ref, dvec_ref, y_ref, st_ref, s_s):
        ci = pl.program_id(0)
        g = pl.program_id(1)

        @pl.when((ci == 0) & (g == 0))
        def _():
            s_s[...] = jnp.zeros_like(s_s)

        e_m, et_m = _expand_mats()
        dtg, acol, arowT, causal = _ssd_group_terms(dt_ref, dtT_ref, arow_ref, acol_ref)
        a_last_row = acol[L - 1:L, :]
        x = _silu(_conv_block(x_ref, xh_ref, xw_ref, xb_ref, ci == 0)[0])
        bm = _silu(_conv_block(b_ref, bh_ref, bw_ref, bb_ref, ci == 0)[0])
        cm = _silu(_conv_block(c_ref, ch_ref, cw_ref, cb_ref, ci == 0)[0])
        cb = _dot(cm, bm, NT)
        s = s_s[g]
        st_ref[0, 0] = s
        ea_x = _expand(jnp.exp(acol), e_m)
        dt_x = _expand(dtg, e_m)
        w_x = _expand(jnp.exp(a_last_row - acol) * dtg, e_m)
        yb = ea_x * _dot(cm, s, NT) + dvec_ref[...] * x
        xd = (x * dt_x).astype(BF16)
        for e in range(HPG):
            sl = slice(e * P, (e + 1) * P)
            lm = jnp.exp(jnp.where(causal, acol[:, e:e + 1] - arowT[e:e + 1, :], NEG))
            m = (cb * lm).astype(BF16)
            y_ref[:, sl] = yb[:, sl] + jnp.dot(m, xd[:, sl], preferred_element_type=F32)
        s_s[g] = _state_decay(arowT[:, L - 1:L], et_m) * s + _dot(x * w_x, bm, TN)

    out_specs = [
        pl.BlockSpec((L, GRP_W), lambda c, g: (c, g)),
        pl.BlockSpec((1, 1, GRP_W, N), lambda c, g: (c, g, 0, 0)),
    ]
    return pcall(
        body, name=nm, grid=(nc, SSM_GROUPS), in_specs=_ssd_specs(nc, False), out_specs=out_specs,
        out_shape=[jax.ShapeDtypeStruct((t_dim, SSM_D_INNER), F32),
                   jax.ShapeDtypeStruct((nc, SSM_GROUPS, GRP_W, N), F32)],
        scratch=[pltpu.VMEM((SSM_GROUPS, GRP_W, N), F32)],
        args=[xbc_raw, xbc_raw, conv_w, conv_b] * 3 + [dt_g, dtT_g, a_row, a_col, dvec], hook=hook)


def ssd_bwd(dy, xbc_raw, conv_w, conv_b, dt_g, dtT_g, a_row, a_col, dvec, states, nm, hook=None):
    t_dim = xbc_raw.shape[0]
    L, P, N = SSM_CHUNK, SSM_HEAD_DIM, SSM_STATE
    nc = t_dim // L

    def body(dy_ref, x_ref, xh_ref, xw_ref, xb_ref, b_ref, bh_ref, bw_ref, bb_ref, c_ref, ch_ref, cw_ref, cb_ref,
             dt_ref, dtT_ref, arow_ref, acol_ref, dvec_ref, st_ref,
             dx_ref, db_ref, dc_ref, da_ref, ddt_ref, dd_ref, dwx_ref, dwb_ref, dwc_ref, dbx_ref, dbb_ref, dbc_ref,
             ds_s, yd_s, dxd_s):
        ci = pl.program_id(0)
        g = pl.program_id(1)

        @pl.when((ci == 0) & (g == 0))
        def _():
            ds_s[...] = jnp.zeros_like(ds_s)
            for r in (dd_ref, dwx_ref, dwb_ref, dwc_ref, dbx_ref, dbb_ref, dbc_ref):
                r[...] = jnp.zeros_like(r)

        e_m, et_m = _expand_mats()
        dtg, acol, arowT, causal = _ssd_group_terms(dt_ref, dtT_ref, arow_ref, acol_ref)
        a_last_row = acol[L - 1:L, :]
        first = ci == nc - 1
        pre_x, sh_x = _conv_block(x_ref, xh_ref, xw_ref, xb_ref, first)
        pre_b, sh_b = _conv_block(b_ref, bh_ref, bw_ref, bb_ref, first)
        pre_c, sh_c = _conv_block(c_ref, ch_ref, cw_ref, cb_ref, first)
        sg_x, sg_b, sg_c = _sigmoid(pre_x), _sigmoid(pre_b), _sigmoid(pre_c)
        x = pre_x * sg_x
        dy = dy_ref[...]
        bm = pre_b * sg_b
        cm = pre_c * sg_c
        cb = _dot(cm, bm, NT)
        s = st_ref[0, 0]
        dsp = ds_s[g]
        ew8 = jnp.exp(a_last_row - acol)
        ea_x = _expand(jnp.exp(acol), e_m)
        dt_x = _expand(dtg, e_m)
        ew_x = _expand(ew8, e_m)
        w_x = ew_x * dt_x
        z = _dot(cm, s, NT)
        dz = ea_x * dy
        dc = _dot(dz, s)
        ds_y = _dot(dz, cm, TN)
        du = _dot(bm, dsp, NT)
        u = x * w_x
        db = _dot(u, dsp)
        xd = (x * dt_x).astype(BF16)
        dyb = dy.astype(BF16)
        dcb = jnp.zeros((L, L), F32)
        for e in range(HPG):
            sl = slice(e * P, (e + 1) * P)
            lm = jnp.exp(jnp.where(causal, acol[:, e:e + 1] - arowT[e:e + 1, :], NEG))
            m = (cb * lm).astype(BF16)
            yd_s[:, sl] = jnp.dot(m, xd[:, sl], preferred_element_type=F32)
            dxd_s[:, sl] = lax.dot_general(m, dyb[:, sl], TN, preferred_element_type=F32)
            dcb = dcb + lax.dot_general(dyb[:, sl], xd[:, sl], NT, preferred_element_type=F32) * lm
        dxd = dxd_s[...]

        def through_conv(d_act, pre, sg, shifted, d_ref, dw_ref, dbias_ref):
            d_pre = d_act * (sg * (1.0 + pre * (1.0 - sg)))
            d_ref[...] = d_pre
            dw_ref[g] += jnp.concatenate([jnp.sum(d_pre * sh, axis=0, keepdims=True) for sh in shifted], axis=0)
            dbias_ref[g] += jnp.sum(d_pre, axis=0, keepdims=True)

        through_conv(dvec_ref[...] * dy + du * w_x + dt_x * dxd, pre_x, sg_x, sh_x, dx_ref, dwx_ref, dbx_ref)
        ddt = _reduce8(x * (ew_x * du + dxd), et_m)
        da = (_reduce8(dz * z + dyb.astype(F32) * yd_s[...], et_m)
              - _reduce8(xd.astype(F32) * dxd + du * u, et_m))
        dwa_row = _reduce8(jnp.broadcast_to(jnp.sum(du * u, axis=0, keepdims=True), (8, GRP_W)), et_m)[0:1]
        t_nh = None
        for p in _split3(dsp * s):
            t = lax.dot_general(p, et_m, TN, preferred_element_type=F32)
            t_nh = t if t_nh is None else t_nh + t
        d_last = dwa_row + jnp.exp(a_last_row) * jnp.sum(t_nh, axis=0, keepdims=True)
        row_l = lax.broadcasted_iota(jnp.int32, (L, 1), 0)
        da_ref[0] = da + jnp.where(row_l == L - 1, d_last, 0.0)
        ddt_ref[0] = ddt
        dd_ref[g] += jnp.sum(dy * x, axis=0, keepdims=True)
        through_conv(dc + _dot(dcb, bm), pre_c, sg_c, sh_c, dc_ref, dwc_ref, dbc_ref)
        through_conv(db + _dot(dcb, cm, TN), pre_b, sg_b, sh_b, db_ref, dwb_ref, dbb_ref)
        ds_s[g] = _state_decay(arowT[:, L - 1:L], et_m) * dsp + ds_y

    rc = lambda c: nc - 1 - c
    in_specs = ([pl.BlockSpec((L, GRP_W), lambda c, g: (rc(c), g))] + _ssd_specs(nc, True)
                + [pl.BlockSpec((1, 1, GRP_W, N), lambda c, g: (rc(c), g, 0, 0))])
    whole = lambda *shape: pl.BlockSpec(shape, lambda c, g: (0,) * len(shape))
    out_specs = [
        pl.BlockSpec((L, GRP_W), lambda c, g: (rc(c), g)),
        pl.BlockSpec((L, N), lambda c, g: (rc(c), g)),
        pl.BlockSpec((L, N), lambda c, g: (rc(c), g)),
        pl.BlockSpec((1, L, HPG), lambda c, g: (g, rc(c), 0)),
        pl.BlockSpec((1, L, HPG), lambda c, g: (g, rc(c), 0)),
        whole(SSM_GROUPS, 1, GRP_W),
        whole(SSM_GROUPS, SSM_CONV, GRP_W), whole(SSM_GROUPS, SSM_CONV, N), whole(SSM_GROUPS, SSM_CONV, N),
        whole(SSM_GROUPS, 1, GRP_W), whole(SSM_GROUPS, 1, N), whole(SSM_GROUPS, 1, N),
    ]
    gn = SSM_GROUPS * N
    acc = lambda *shape: jax.ShapeDtypeStruct(shape, F32)
    out_shape = [
        acc(t_dim, SSM_D_INNER), acc(t_dim, gn), acc(t_dim, gn), acc(SSM_GROUPS, t_dim, HPG),
        acc(SSM_GROUPS, t_dim, HPG), acc(SSM_GROUPS, 1, GRP_W),
        acc(SSM_GROUPS, SSM_CONV, GRP_W), acc(SSM_GROUPS, SSM_CONV, N), acc(SSM_GROUPS, SSM_CONV, N),
        acc(SSM_GROUPS, 1, GRP_W), acc(SSM_GROUPS, 1, N), acc(SSM_GROUPS, 1, N),
    ]
    return pcall(
        body, name=nm, grid=(nc, SSM_GROUPS), in_specs=in_specs, out_specs=out_specs, out_shape=out_shape,
        scratch=[pltpu.VMEM((SSM_GROUPS, GRP_W, N), F32), pltpu.VMEM((L, GRP_W), F32), pltpu.VMEM((L, GRP_W), F32)],
        args=[dy] + [xbc_raw, xbc_raw, conv_w, conv_b] * 3 + [dt_g, dtT_g, a_row, a_col, dvec, states], hook=hook)


def _softplus(x):
    return jnp.maximum(x, 0.0) + jnp.log(1.0 + jnp.exp(-jnp.abs(x)))


def ssd_dt_bwd(da, ddt, dt, dt_raw, a_row, dt_bias, nm):
    L = SSM_CHUNK

    def fn(d_a, d_dt, dtv, raw, ar, bias):
        r = lax.broadcasted_iota(jnp.int32, (L, L), 0)
        c = lax.broadcasted_iota(jnp.int32, (L, L), 1)
        triu = (r <= c).astype(BF16)
        acc = None
        for p in _split3(d_a):
            t = jnp.dot(triu, p, preferred_element_type=F32)
            acc = t if acc is None else acc + t
        d_dt = d_dt + acc * ar
        d_a_h = jnp.sum(acc * dtv, axis=0, keepdims=True)
        d_raw = d_dt * _sigmoid(raw + bias)
        return d_raw, d_a_h, jnp.sum(d_raw, axis=0, keepdims=True)

    return rowmap(fn, [da, ddt, dt, dt_raw], [a_row, dt_bias], [(SSM_HEADS, BF16)],
                  [(1, SSM_HEADS), (1, SSM_HEADS)], tm=L, name=nm)


GN_W = SSM_D_INNER // SSM_GROUPS


def mamba_fwd(h, p, nm, io):
    def in_proj(x, gg, w_zt, w_xbct, w_dtt):
        uu = _rms_fwd(x, gg).astype(BF16)
        return uu, _dot(uu, w_zt, NT), _dot(uu, w_xbct, NT), _dot(uu, w_dtt, NT)

    u, z, xbc_raw, dt_raw = rowmap(in_proj, [h], [p["ssm_norm"], p["w_zt"], p["w_xbct"], p["w_dtt"]],
                                   [(D_MODEL, BF16), (SSM_D_INNER, F32), (SSM_CONV_DIM, F32), (SSM_HEADS, F32)],
                                   tm=RESIDENT_TM, name=nm + "_in", hook=io.hook(nm + "_in"))
    dt, = rowmap(lambda r, b: (_softplus(r + b),), [dt_raw], [p["dt_bias"]], [(SSM_HEADS, F32)], tm=256,
                 name=nm + "_softplus")
    dt_g = dt.reshape(-1, SSM_GROUPS, HPG).transpose(1, 0, 2)
    dtT_g = dt_g.transpose(0, 2, 1)
    y, states = ssd_fwd(xbc_raw, p["conv_w"], p["conv_b"], dt_g, dtT_g, p["a_row"], p["a_col"], p["dvec"],
                        nm + "_ssd", hook=io.hook(nm + "_ssd"))

    def gate_norm_out(yv, zv, hv, gg, w_out):
        t = yv * _silu(zv)
        yn = jnp.concatenate([_rms_fwd(t[:, k * GN_W:(k + 1) * GN_W], gg[:, k * GN_W:(k + 1) * GN_W])
                              for k in range(SSM_GROUPS)], axis=1).astype(BF16)
        return yn, hv + _dot(yn, w_out)

    yn, out = rowmap(gate_norm_out, [y, z, h], [p["gate_norm"], p["w_out"]],
                     [(SSM_D_INNER, BF16), (D_MODEL, F32)], tm=RESIDENT_TM, name=nm + "_out")
    return out, (u, z, xbc_raw, dt_raw, dt, dt_g, dtT_g, y, states, yn)


def mamba_bwd(h, p, saved, dout, dout_bf, nm, io):
    u, z, xbc_raw, dt_raw, dt, dt_g, dtT_g, y, states, yn = saved
    g = {}
    io.put("w_out", mm(yn, dout_bf, ta=True, out_dtype=BF16, name=nm + "_dwout"))

    def gate_norm_bwd(d_o, yv, zv, gg, w_out):
        d = _dot(d_o, w_out, NT)
        sz = _silu(zv)
        t = yv * sz
        dts, dgs = [], []
        for k in range(SSM_GROUPS):
            sl = slice(k * GN_W, (k + 1) * GN_W)
            dt_k, dg_k = _rms_bwd(t[:, sl], gg[:, sl], d[:, sl])
            dts.append(dt_k)
            dgs.append(dg_k)
        d_t = jnp.concatenate(dts, axis=1)
        return d_t * sz, d_t * yv * _silu_grad(zv), jnp.concatenate(dgs, axis=1)

    dy, dz, g["gate_norm"] = rowmap(gate_norm_bwd, [dout_bf, y, z], [p["gate_norm"], p["w_out"]],
                                    [(SSM_D_INNER, F32), (SSM_D_INNER, BF16)], [(1, SSM_D_INNER)], tm=256,
                                    name=nm + "_dgatenorm")
    d_x, d_b, d_c, da_g, ddt_g, dd, dwx, dwb, dwc, dbx, dbb, dbc = ssd_bwd(
        dy, xbc_raw, p["conv_w"], p["conv_b"], dt_g, dtT_g, p["a_row"], p["a_col"], p["dvec"], states, nm + "_dssd",
        hook=io.hook(nm + "_dssd"))
    g["dvec"] = dd
    by_lane = lambda t: t.transpose(1, 0, 2).reshape(t.shape[1], -1)
    g["conv_w"] = jnp.concatenate([by_lane(dwx), by_lane(dwb), by_lane(dwc)], axis=1)
    g["conv_b"] = jnp.concatenate([by_lane(dbx), by_lane(dbb), by_lane(dbc)], axis=1)
    per_head = lambda t: t.transpose(1, 0, 2).reshape(-1, SSM_HEADS)
    ddt_raw, g["a"], g["dt_bias"] = ssd_dt_bwd(per_head(da_g), per_head(ddt_g), dt, dt_raw, p["a_heads"],
                                               p["dt_bias"], nm + "_ddt")
    dxbc_raw = conv_input_grad([d_x, d_b, d_c], p["conv_w"], nm + "_dconv")
    io.put("w_int", jnp.concatenate([mm(dz, u, ta=True, out_dtype=BF16, name=nm + "_dwz"),
                                     mm(dxbc_raw, u, ta=True, out_dtype=BF16, name=nm + "_dwxbc"),
                                     mm(ddt_raw, u, ta=True, out_dtype=BF16, name=nm + "_dwdt")], axis=0))
    dh, dh_bf, g["ssm_norm"] = du_norm_bwd(
        [(dz, p["w_zt"], False), (dxbc_raw, p["w_xbct"], False), (ddt_raw, p["w_dtt"], False)],
        h, p["ssm_norm"], dout, nm + "_du", hook=io.hook(nm + "_du"))
    return dh, dh_bf, g


KV_W = ATT_KV_HEADS * ATT_HEAD_DIM


def kv_fwd(h, p, nm):
    def kv_proj(x, gg, w_kv, gk):
        uu = _rms_fwd(x, gg).astype(BF16)
        t = _dot(uu, w_kv)
        ks = [_rms_fwd(t[:, j * ATT_HEAD_DIM:(j + 1) * ATT_HEAD_DIM], gk) for j in range(ATT_KV_HEADS)]
        return uu, t, jnp.concatenate(ks, axis=1), t[:, KV_W:]

    u, kv_raw, k, v = rowmap(kv_proj, [h], [p["kv_norm"], p["w_kv"], p["k_norm"]],
                             [(D_MODEL, BF16), (2 * KV_W, F32), (KV_W, F32), (KV_W, F32)], tm=RESIDENT_TM,
                             name=nm + "_proj")
    return k, v, (u, kv_raw)


def kv_bwd(h, p, saved, dk_cur, dk_prev, dv_cur, dv_prev, dout, nm, io):
    u, kv_raw = saved
    t_dim = h.shape[0]
    tm = ATT_WINDOW
    nb = t_dim // tm
    nxt = pl.BlockSpec((tm, KV_W), lambda i: (jnp.minimum(i + 1, nb - 1), 0))

    def fn(dkc, dkp, dvc, dvp, t, gg):
        live = pl.program_id(0) < nb - 1
        dk = dkc + jnp.where(live, dkp, 0.0)
        dv = dvc + jnp.where(live, dvp, 0.0)
        outs, dgs = [], None
        for j in range(ATT_KV_HEADS):
            sl = slice(j * ATT_HEAD_DIM, (j + 1) * ATT_HEAD_DIM)
            dx, dg = _rms_bwd(t[:, sl], gg, dk[:, sl])
            outs.append(dx)
            dgs = dg if dgs is None else dgs + dg
        return jnp.concatenate(outs + [dv], axis=1), dgs

    dkv_raw, dknorm = rowmap(fn, [dk_cur, (dk_prev, nxt), dv_cur, (dv_prev, nxt), kv_raw], [p["k_norm"]],
                             [(2 * KV_W, BF16)], [(1, ATT_HEAD_DIM)], tm=tm, name=nm + "_dknorm",
                             hook=io.hook(nm + "_dknorm"))
    g = {"k_norm": dknorm}
    io.put("w_kv", mm(u, dkv_raw, ta=True, out_dtype=BF16, name=nm + "_dwkv"))
    dh, dh_bf, g["kv_norm"] = du_norm_bwd([(dkv_raw, p["w_kv"], True)], h, p["kv_norm"], dout, nm + "_du",
                                          hook=io.hook(nm + "_du"))
    return dh, dh_bf, g


def _attn_scores(q_ref, kp_ref, kc_ref, vp_ref, vc_ref, qn_ref, bias_ref, sink_ref, kv):
    hd = ATT_HEAD_DIM
    blk = ATT_WINDOW
    sl = slice(kv * hd, (kv + 1) * hd)
    kk = jnp.concatenate([kp_ref[:, sl], kc_ref[:, sl]], axis=0)
    vv = jnp.concatenate([vp_ref[:, sl], vc_ref[:, sl]], axis=0)
    gq = qn_ref[...]
    raws, rinvs = [], []
    for r in range(ATT_GROUP):
        hh = kv * ATT_GROUP + r
        x = q_ref[:, hh * hd:(hh + 1) * hd]
        raws.append(x)
        rinvs.append(lax.rsqrt(jnp.mean(x * x, axis=-1, keepdims=True) + EPS))
    xh = jnp.concatenate([x * ri for x, ri in zip(raws, rinvs)], axis=0)
    rinv = jnp.concatenate(rinvs, axis=0)
    q8 = xh * gq
    s = _dot(q8, kk, NT) * (hd ** -0.5) + bias_ref[kv]
    colk = lax.broadcasted_iota(jnp.int32, (1, 2 * blk), 1)
    s = jnp.where((pl.program_id(0) > 0) | (colk >= blk), s, NEG)
    sink = sink_ref[kv]
    m = jnp.maximum(jnp.max(s, axis=-1, keepdims=True), sink)
    pexp = jnp.exp(s - m)
    e_sink = jnp.exp(sink - m)
    inv_den = 1.0 / (jnp.sum(pexp, axis=-1, keepdims=True) + e_sink)
    return kk, vv, xh, rinv, q8, pexp * inv_den, e_sink * inv_den


def _attn_specs(nb):
    blk = ATT_WINDOW
    cur = lambda i: (i, 0)
    prev = lambda i: (jnp.maximum(i - 1, 0), 0)
    return [
        pl.BlockSpec((blk, D_MODEL), cur),
        pl.BlockSpec((blk, KV_W), prev), pl.BlockSpec((blk, KV_W), cur),
        pl.BlockSpec((blk, KV_W), prev), pl.BlockSpec((blk, KV_W), cur),
        pl.BlockSpec((1, ATT_HEAD_DIM), lambda i: (0, 0)),
        pl.BlockSpec((ATT_KV_HEADS, ATT_GROUP * blk, 2 * blk), lambda i: (0, 0, 0)),
        pl.BlockSpec((ATT_KV_HEADS, ATT_GROUP * blk, 1), lambda i: (0, 0, 0)),
    ]


def attn_fwd(q_raw, k, v, q_norm, bias, sink_col, nm):
    t_dim = q_raw.shape[0]
    blk, hd = ATT_WINDOW, ATT_HEAD_DIM
    nb = t_dim // blk

    def body(q_ref, kp_ref, kc_ref, vp_ref, vc_ref, qn_ref, bias_ref, sink_ref, o_ref):
        for kv in range(ATT_KV_HEADS):
            kk, vv, xh, rinv, q8, prob, p_sink = _attn_scores(q_ref, kp_ref, kc_ref, vp_ref, vc_ref, qn_ref,
                                                              bias_ref, sink_ref, kv)
            o8 = _dot(prob, vv)
            for r in range(ATT_GROUP):
                hh = kv * ATT_GROUP + r
                o_ref[:, hh * hd:(hh + 1) * hd] = o8[r * blk:(r + 1) * blk].astype(o_ref.dtype)

    out, = pcall(body, name=nm, grid=(nb,), in_specs=_attn_specs(nb),
                 out_specs=[pl.BlockSpec((blk, D_MODEL), lambda i: (i, 0))],
                 out_shape=[jax.ShapeDtypeStruct((t_dim, D_MODEL), BF16)],
                 args=[q_raw, k, k, v, v, q_norm, bias, sink_col])
    return out


def attn_bwd(do, q_raw, k, v, q_norm, bias, sink_col, nm, hook=None):
    t_dim = q_raw.shape[0]
    blk, hd = ATT_WINDOW, ATT_HEAD_DIM
    nb = t_dim // blk
    scale = hd ** -0.5

    def body(do_ref, q_ref, kp_ref, kc_ref, vp_ref, vc_ref, qn_ref, bias_ref, sink_ref,
             dq_ref, dkc_ref, dkp_ref, dvc_ref, dvp_ref, dbias_ref, dsink_ref, dqn_ref):
        @pl.when(pl.program_id(0) == 0)
        def _():
            dbias_ref[...] = jnp.zeros_like(dbias_ref)
            dsink_ref[...] = jnp.zeros_like(dsink_ref)
            dqn_ref[...] = jnp.zeros_like(dqn_ref)

        gq = qn_ref[...]
        for kv in range(ATT_KV_HEADS):
            kk, vv, xh, rinv, q8, prob, p_sink = _attn_scores(q_ref, kp_ref, kc_ref, vp_ref, vc_ref, qn_ref,
                                                              bias_ref, sink_ref, kv)
            do8 = jnp.concatenate([do_ref[:, (kv * ATT_GROUP + r) * hd:(kv * ATT_GROUP + r + 1) * hd]
                                   for r in range(ATT_GROUP)], axis=0)
            dp = _dot(do8, vv, NT)
            delta = jnp.sum(prob * dp, axis=-1, keepdims=True)
            ds = prob * (dp - delta)
            dsink_ref[kv] += -p_sink * delta
            dbias_ref[kv] += ds
            ds_s = ds * scale
            dq8 = _dot(ds_s, kk)
            dkk = _dot(ds_s, q8, TN)
            dvv = _dot(prob, do8, TN)
            dqn_ref[...] += jnp.sum(dq8 * xh, axis=0, keepdims=True)
            dxh = dq8 * gq
            dq_raw8 = rinv * (dxh - xh * jnp.mean(dxh * xh, axis=-1, keepdims=True))
            for r in range(ATT_GROUP):
                hh = kv * ATT_GROUP + r
                dq_ref[:, hh * hd:(hh + 1) * hd] = dq_raw8[r * blk:(r + 1) * blk].astype(dq_ref.dtype)
            sl = slice(kv * hd, (kv + 1) * hd)
            dkp_ref[:, sl] = dkk[:blk]
            dkc_ref[:, sl] = dkk[blk:]
            dvp_ref[:, sl] = dvv[:blk]
            dvc_ref[:, sl] = dvv[blk:]

    cur = lambda i: (i, 0)
    row_spec = pl.BlockSpec((blk, KV_W), cur)
    out_specs = [
        pl.BlockSpec((blk, D_MODEL), cur), row_spec, row_spec, row_spec, row_spec,
        pl.BlockSpec((ATT_KV_HEADS, ATT_GROUP * blk, 2 * blk), lambda i: (0, 0, 0)),
        pl.BlockSpec((ATT_KV_HEADS, ATT_GROUP * blk, 1), lambda i: (0, 0, 0)),
        pl.BlockSpec((1, hd), lambda i: (0, 0)),
    ]
    kvs = jax.ShapeDtypeStruct((t_dim, KV_W), F32)
    out_shape = [
        jax.ShapeDtypeStruct((t_dim, D_MODEL), BF16), kvs, kvs, kvs, kvs,
        jax.ShapeDtypeStruct((ATT_KV_HEADS, ATT_GROUP * blk, 2 * blk), F32),
        jax.ShapeDtypeStruct((ATT_KV_HEADS, ATT_GROUP * blk, 1), F32),
        jax.ShapeDtypeStruct((1, hd), F32),
    ]
    return pcall(body, name=nm, grid=(nb,), in_specs=[pl.BlockSpec((blk, D_MODEL), cur)] + _attn_specs(nb),
                 out_specs=out_specs, out_shape=out_shape,
                 args=[do, q_raw, k, k, v, v, q_norm, bias, sink_col], hook=hook)


def _t5_bucket_np():
    blk = ATT_WINDOW
    qi = np.arange(blk)[:, None] + blk
    kj = np.arange(2 * blk)[None, :]
    dist = qi - kj
    n = np.maximum(dist, 0)
    max_exact = REL_BUCKETS // 2
    nf = np.maximum(n, 1).astype(np.float32)
    large = max_exact + (np.log(nf / max_exact) / math.log(ATT_WINDOW / max_exact)
                         * (REL_BUCKETS - max_exact)).astype(np.int32)
    large = np.minimum(large, REL_BUCKETS - 1)
    bucket = np.where(n < max_exact, n, large)
    in_window = (dist >= 0) & (dist < ATT_WINDOW)
    return bucket, in_window


def attn_block_fwd(h, k, v, p, nm):
    def q_proj(x, gg, w_q):
        uu = _rms_fwd(x, gg).astype(BF16)
        return uu, _dot(uu, w_q)

    u, q_raw = rowmap(q_proj, [h], [p["attn_norm"], p["w_q"]], [(D_MODEL, BF16), (D_MODEL, F32)], tm=RESIDENT_TM,
                      name=nm + "_q")
    o = attn_fwd(q_raw, k, v, p["q_norm"], p["bias"], p["sink_col"], nm + "_core")
    out = mm(o, p["w_o"], res=h, name=nm + "_o")
    return out, (u, q_raw, o)


def attn_block_bwd(h, k, v, p, saved, dout, dout_bf, nm, io):
    u, q_raw, o = saved
    g = {}
    io.put("w_o", mm(o, dout_bf, ta=True, out_dtype=BF16, name=nm + "_dwo", hook=io.hook(nm + "_dwo")))
    do = mm(dout_bf, p["w_o"], tb=True, name=nm + "_do")
    dq_raw, dkc, dkp, dvc, dvp, g["bias"], g["sink_col"], g["q_norm"] = attn_bwd(
        do, q_raw, k, v, p["q_norm"], p["bias"], p["sink_col"], nm + "_dcore", hook=io.hook(nm + "_dcore"))
    io.put("w_q", mm(u, dq_raw, ta=True, out_dtype=BF16, name=nm + "_dwq"))
    dh, dh_bf, g["attn_norm"] = du_norm_bwd([(dq_raw, p["w_q"], True)], h, p["attn_norm"], dout, nm + "_du")
    return dh, dh_bf, g, (dkc, dkp, dvc, dvp)


FFN_TAGS = ["00", "01", "10", "11"]


def local_step(x, target, small, io):
    bucket, in_window = _t5_bucket_np()
    blk = ATT_WINDOW
    w = small

    fnorm = {tag: w["ffn_norm"][int(tag[0]), int(tag[1])][None, :] for tag in FFN_TAGS}
    a_neg = -jnp.exp(w["ssm_a_log"][0])

    def mamba_p():
        w_int = io.w("w_int")
        return dict(ssm_norm=w["ssm_norm"], w_zt=w_int[:SSM_D_INNER],
                    w_xbct=w_int[SSM_D_INNER:SSM_D_INNER + SSM_CONV_DIM], w_dtt=w_int[SSM_D_INNER + SSM_CONV_DIM:],
                    conv_w=w["ssm_conv_w"][0], conv_b=w["ssm_conv_b"], dt_bias=w["ssm_dt_bias"],
                    a_heads=a_neg[None, :], a_row=a_neg.reshape(SSM_GROUPS, 1, HPG),
                    a_col=a_neg.reshape(SSM_GROUPS, HPG, 1),
                    dvec=jnp.repeat(w["ssm_d"][0], SSM_HEAD_DIM)[None, :],
                    gate_norm=w["ssm_gate_norm"], w_out=io.w("w_out"))

    rb = w["rel_bias"]
    onehot3 = (np.arange(REL_BUCKETS)[:, None, None] == bucket[None]).astype(np.float32)
    bias = jnp.einsum("bh,bqk->hqk", rb, onehot3, precision=lax.Precision.HIGHEST)
    bias = jnp.where(in_window[None], bias, NEG)
    bias = bias.reshape(ATT_KV_HEADS, ATT_GROUP * blk, 2 * blk)
    sink_col = jnp.repeat(w["sinks"][0], blk).reshape(ATT_KV_HEADS, ATT_GROUP * blk, 1)

    def attn_p():
        return dict(attn_norm=w["attn_norm"], w_q=io.w("w_q"), q_norm=w["q_norm"], bias=bias, sink_col=sink_col,
                    w_o=io.w("w_o"))

    def kv_p():
        return dict(kv_norm=w["kv_norm"][None, :], w_kv=io.w("w_kv"), k_norm=w["k_norm"][None, :])

    h0 = x
    h0a, s_f00 = ffn_fwd(h0, fnorm["00"], "00", io)
    mp = mamba_p()
    h0b, s_m = mamba_fwd(h0a, mp, "ssm", io)
    h1, s_f01 = ffn_fwd(h0b, fnorm["01"], "01", io)
    kp = kv_p()
    k, v, s_kv = kv_fwd(h1, kp, "kv")
    h1a, s_f10 = ffn_fwd(h1, fnorm["10"], "10", io)
    ap = attn_p()
    h1b, s_a = attn_block_fwd(h1a, k, v, ap, "att")
    (dh, dh_bf, sq), s_f11 = ffn_fwd(h1b, fnorm["11"], "11", io, target=target)
    loss_part = jnp.sum(sq) * (0.5 / D_MODEL)

    fg = {}

    def ffn_back(tag, h_in, saved, dh, dh_bf):
        dh, dh_bf, dg = ffn_bwd(h_in, fnorm[tag], tag, saved, dh, dh_bf, io)
        fg[tag] = dg[0]
        return dh, dh_bf

    dh, dh_bf = ffn_back("11", h1b, s_f11, dh, dh_bf)
    dh, dh_bf, ga, dkv = attn_block_bwd(h1a, k, v, ap, s_a, dh, dh_bf, "att", io)
    dh, dh_bf = ffn_back("10", h1, s_f10, dh, dh_bf)
    dh, dh_bf, gk = kv_bwd(h1, kp, s_kv, *dkv, dh, "kv", io)
    dh, dh_bf = ffn_back("01", h0b, s_f01, dh, dh_bf)
    dh, dh_bf, gm = mamba_bwd(h0a, mp, s_m, dh, dh_bf, "ssm", io)
    dh, dh_bf = ffn_back("00", h0, s_f00, dh, dh_bf)
    grad_x = dh

    grads = {}
    grads["ffn_norm"] = jnp.stack([fg[tag] for tag in FFN_TAGS]).reshape(2, 2, D_MODEL)
    grads["ssm_norm"] = gm["ssm_norm"]
    grads["ssm_conv_w"] = gm["conv_w"][None]
    grads["ssm_conv_b"] = gm["conv_b"]
    grads["ssm_dt_bias"] = gm["dt_bias"]
    grads["ssm_a_log"] = gm["a"] * a_neg[None, :]
    grads["ssm_d"] = jnp.sum(gm["dvec"].reshape(SSM_HEADS, SSM_HEAD_DIM), axis=1)[None, :]
    grads["ssm_gate_norm"] = gm["gate_norm"]
    grads["kv_norm"] = gk["kv_norm"][0]
    grads["k_norm"] = gk["k_norm"][0]
    grads["attn_norm"] = ga["attn_norm"]
    grads["q_norm"] = ga["q_norm"]
    grads["sinks"] = jnp.sum(ga["sink_col"].reshape(ATT_HEADS, blk), axis=1)[None, :]
    onehot = (np.arange(REL_BUCKETS)[:, None] == bucket.reshape(1, -1)).astype(np.float32)
    dbias2d = ga["bias"].reshape(ATT_HEADS, blk * 2 * blk)
    grads["rel_bias"] = mm(jnp.asarray(onehot, BF16), dbias2d, tb=True, name="drelbias")
    return loss_part, grad_x, grads


def _adamw(g, w, m, v):
    m = ADAM_B1 * m + (1.0 - ADAM_B1) * g
    v = ADAM_B2 * v + (1.0 - ADAM_B2) * (g * g)
    m_hat = m / (1.0 - ADAM_B1 ** ADAM_STEP)
    v_hat = v / (1.0 - ADAM_B2 ** ADAM_STEP)
    delta = -ADAM_LR * (m_hat / (jnp.sqrt(v_hat) + ADAM_EPS) + ADAM_WD * w)
    return delta, m, v


def _slot_sum(r):
    g = r[0].astype(F32)
    for d in range(1, r.shape[0]):
        g = g + r[d].astype(F32)
    return g


def adamw_rows(recvs, w, m, v, name):
    n_l, rows, width = w.shape
    n_slots = recvs[0].shape[0]
    tr = 32
    assert rows % tr == 0, rows
    nt = rows // tr

    def body(*refs):
        r_refs = refs[:n_l]
        w_ref, m_ref, v_ref, g_o, d_o, m_o, v_o = refs[n_l:]
        li = pl.program_id(0)
        for k in range(n_l):
            @pl.when(li == k)
            def _(k=k):
                g = _slot_sum(r_refs[k])
                delta, m2, v2 = _adamw(g, w_ref[0], m_ref[0], v_ref[0])
                g_o[0] = g
                d_o[0] = delta
                m_o[0] = m2
                v_o[0] = v2

    def r_spec(k):
        return pl.BlockSpec((n_slots, tr, width),
                            lambda li, j: (0, jnp.where(li == k, j, jnp.where(li > k, nt - 1, 0)), 0))

    w_spec = pl.BlockSpec((1, tr, width), lambda li, j: (li, j, 0))
    shp = jax.ShapeDtypeStruct(w.shape, F32)
    return pcall(body, name=name, grid=(n_l, nt), in_specs=[r_spec(k) for k in range(n_l)] + [w_spec] * 3,
                 out_specs=[w_spec] * 4, out_shape=[shp] * 4, args=list(recvs) + [w, m, v])


def adamw_cols(recvs, w, m, v, name):
    n_l, rows, n = w.shape
    n_slots = recvs[0].shape[0]
    tr = 256
    nt = rows // tr

    def body(*refs):
        r_refs = refs[:n_l]
        w_ref, m_ref, v_ref, g_o, d_o, m_o, v_o = refs[n_l:]
        li = pl.program_id(0)
        for k in range(n_l):
            @pl.when(li == k)
            def _(k=k):
                g = _slot_sum(r_refs[k]).T
                delta, m2, v2 = _adamw(g, w_ref[0], m_ref[0], v_ref[0])
                g_o[0] = g
                d_o[0] = delta
                m_o[0] = m2
                v_o[0] = v2

    def r_spec(k):
        return pl.BlockSpec((n_slots, n, tr),
                            lambda li, j: (0, 0, jnp.where(li == k, j, jnp.where(li > k, nt - 1, 0))))

    w_spec = pl.BlockSpec((1, tr, n), lambda li, j: (li, j, 0))
    shp = jax.ShapeDtypeStruct(w.shape, F32)
    return pcall(body, name=name, grid=(n_l, nt), in_specs=[r_spec(k) for k in range(n_l)] + [w_spec] * 3,
                 out_specs=[w_spec] * 4, out_shape=[shp] * 4, args=list(recvs) + [w, m, v])


WEIGHT_NAMES = ["ffn_norm", "ffn_w1", "ffn_w3", "ffn_w2", "ssm_norm", "ssm_w_in", "ssm_conv_w", "ssm_conv_b",
                "ssm_dt_bias", "ssm_a_log", "ssm_d", "ssm_gate_norm", "ssm_w_out", "kv_norm", "w_kv", "k_norm",
                "attn_norm", "w_q", "q_norm", "sinks", "w_o", "rel_bias"]

SMALL = [
    ("ffn_norm", (2, 2, 1024), 2), ("ssm_norm", (1, 1024), 1), ("ssm_conv_w", (1, 4, 3072), 2),
    ("ssm_conv_b", (1, 3072), 1), ("ssm_gate_norm", (1, 2048), 1),
    ("ssm_dt_bias", (1, 32), None), ("ssm_a_log", (1, 32), None), ("ssm_d", (1, 32), None),
    ("kv_norm", (1024,), None), ("k_norm", (64,), None), ("attn_norm", (1, 1024), None),
    ("q_norm", (1, 64), None), ("sinks", (1, 16), None), ("rel_bias", (32, 16), None),
]
SMALL_W = 1024
SMALL_FULL_ROWS = 32
SMALL_LOCAL_ROWS = 48

MAT_GROUPS = {
    "f00_up": ["w1t_00", "w3t_00"], "f00_down": ["w2_00"], "f01": ["w1t_01", "w3t_01", "w2_01"],
    "f10": ["w1t_10", "w3t_10", "w2_10"], "f11": ["w1t_11", "w3t_11", "w2_11"],
    "ssm": ["w_int", "w_out"], "att": ["w_q", "w_o", "w_kv"],
    "f00_early": ["w2_00", "w1t_00"], "f00_late": ["w3t_00"],
}
FIRST_GATHER = "f00_up"
GATHER_PLAN = {"f00_upgate": ["f00_down", "ssm"], "ssm_in": ["f01"], "ssm_ssd": ["att", "f10"],
               "f01_upgate": ["f11"]}
SCATTER_A_PLAN = {"att_dwo": "f11", "kv_dknorm": "f10", "kv_du": "att", "f01_du": "f01", "ssm_du": "ssm",
                  "f00_dw3": "f00_early", "f00_du": "f00_late"}
SCATTER_B_PLAN = {"att_dcore": "f11", "f01_dw2": "att", "f01_dgate": "f10", "ssm_dssd": "f01", "f00_dgate": "ssm",
                  "f00_du": "f00_early"}
LAST_SCATTER = "f00_late"
SLOT_MAJOR = ("w_int",)


def _shard_shape(s, a):
    return s[:a] + (s[a] // N_DEV,) + s[a + 1:]


def _unshard_view(stack, shard_shape, axis):
    moved = jnp.moveaxis(stack, 0, axis)
    return moved.reshape(shard_shape[:axis] + (N_DEV * shard_shape[axis],) + shard_shape[axis + 1:])


def _small_local(arrs):
    flat = jnp.concatenate([arrs[n].reshape(-1) for n, _, _ in SMALL])
    return jnp.pad(flat, (0, SMALL_LOCAL_ROWS * LANES - flat.shape[0])).reshape(SMALL_LOCAL_ROWS, LANES)


def chip_partial(g4, ra, name):
    _, _, n, width = g4.shape

    def body(core_ref, g_ref, r_ref, o_ref):
        o_ref[0] = (g_ref[0, 0].astype(F32) + r_ref[0, 0].astype(F32)).astype(o_ref.dtype)

    grid_spec = pltpu.PrefetchScalarGridSpec(
        num_scalar_prefetch=1, grid=(N_CHIPS,),
        in_specs=[pl.BlockSpec((1, 1, n, width), lambda q, core: (q, core[0], 0, 0)),
                  pl.BlockSpec((1, 1, n, width), lambda q, core: (q, 0, 0, 0))],
        out_specs=pl.BlockSpec((1, n, width), lambda q, core: (q, 0, 0)))
    core = jnp.reshape(lax.axis_index("c"), (1,)).astype(jnp.int32)
    return pl.pallas_call(
        body, name=name, grid_spec=grid_spec, out_shape=jax.ShapeDtypeStruct((N_CHIPS, n, width), g4.dtype),
        compiler_params=pltpu.CompilerParams(dimension_semantics=("arbitrary",), vmem_limit_bytes=VMEM_LIMIT_BYTES),
    )(core, g4, ra)


class StepIO:
    def __init__(self, pieces):
        self.pieces = pieces
        self.full = {}
        self.grad = {}
        self.from_sibling = {}
        self.recv = {}

    def w(self, name):
        return self.full[name]

    def put(self, name, g):
        self.grad[name] = g

    def _by_chip_core(self, name):
        g = self.grad[name]
        return g.reshape((N_CHIPS, 2, g.shape[0] // N_DEV) + g.shape[1:])

    def gather_items(self, groups):
        names = [n for grp in groups for n in MAT_GROUPS[grp]]
        items = [("g2", self.pieces[n], None if n in SLOT_MAJOR else 0) for n in names]

        def sink(outs):
            for n, o in zip(names, outs):
                self.full[n] = o.reshape((-1,) + o.shape[2:]) if n in SLOT_MAJOR else o

        return items, sink

    def scatter_a_items(self, group):
        names = MAT_GROUPS[group]
        items = [("sa", self._by_chip_core(n), None) for n in names]

        def sink(outs):
            for n, o in zip(names, outs):
                self.from_sibling[n] = o

        return items, sink

    def scatter_b_items(self, group):
        names = MAT_GROUPS[group]
        items = [("sb", chip_partial(self._by_chip_core(n), self.from_sibling[n], "partial_" + n), None)
                 for n in names]

        def sink(outs):
            for n, o in zip(names, outs):
                self.recv[n] = o

        return items, sink

    def hook(self, site):
        parts = []
        if site in GATHER_PLAN:
            parts.append(self.gather_items(GATHER_PLAN[site]))
        if site in SCATTER_A_PLAN:
            parts.append(self.scatter_a_items(SCATTER_A_PLAN[site]))
        if site in SCATTER_B_PLAN:
            parts.append(self.scatter_b_items(SCATTER_B_PLAN[site]))
        if not parts:
            return None
        return combine_hooks(parts)


def combine_hooks(parts):
    items = [it for its, _ in parts for it in its]

    def sink(outs):
        p = 0
        for its, snk in parts:
            snk(outs[p:p + len(its)])
            p += len(its)

    return Comm(items), sink


def step(x, target, wts, ms, vs):
    me = _my_index()

    pieces = {}
    for li in range(2):
        for hi in range(2):
            tag = "%d%d" % (li, hi)
            pieces["w1t_" + tag] = wts["ffn_w1"][li, hi].T.astype(BF16)
            pieces["w3t_" + tag] = wts["ffn_w3"][li, hi].T.astype(BF16)
            pieces["w2_" + tag] = wts["ffn_w2"][li, hi].astype(BF16)
    pieces["w_int"] = wts["ssm_w_in"][0].T.astype(BF16)
    pieces["w_out"] = wts["ssm_w_out"][0].astype(BF16)
    pieces["w_kv"] = wts["w_kv"].astype(BF16)
    pieces["w_q"] = wts["w_q"][0].astype(BF16)
    pieces["w_o"] = wts["w_o"][0].astype(BF16)
    io = StepIO(pieces)

    small_sharded = [(n, s, a) for n, s, a in SMALL if a is not None]
    loc = jnp.concatenate([wts[n].reshape(-1) for n, _, _ in small_sharded])
    loc_rows = -(-loc.shape[0] // (8 * LANES)) * 8
    loc = jnp.pad(loc, (0, loc_rows * LANES - loc.shape[0])).reshape(loc_rows, LANES)
    got_small = []
    comm, sink = combine_hooks([io.gather_items([FIRST_GATHER]), ([("g", loc, None)], got_small.extend)])
    sink(comm_only(comm, "gather_first"))
    gath_small = got_small[0].reshape(N_DEV, -1)
    small = {}
    off = 0
    for n, s, a in small_sharded:
        shard = _shard_shape(s, a)
        cnt = int(np.prod(shard))
        small[n] = _unshard_view(gath_small[:, off:off + cnt].reshape((N_DEV,) + shard), shard, a)
        off += cnt
    for n, s, a in SMALL:
        if a is None:
            small[n] = wts[n]

    loss_part, grad_x, g_small_local = local_step(x[0], target[0], small, io)
    loss = lax.psum(loss_part, ("x", "y", "c"))

    small_flat = jnp.concatenate([g_small_local[n].reshape(-1) for n, _, _ in SMALL])
    small_buf = jnp.pad(small_flat, (0, SMALL_FULL_ROWS * SMALL_W - small_flat.shape[0]))
    small_buf = small_buf.reshape(SMALL_FULL_ROWS, SMALL_W)
    got_small = []
    comm, sink = combine_hooks([io.scatter_b_items(LAST_SCATTER), ([("g", small_buf, None)], got_small.extend)])
    sink(comm_only(comm, "exchange_last"))
    small_all = got_small[0]

    def sum_body(r_ref, o_ref):
        o_ref[...] = _slot_sum(r_ref)

    vmem = pl.BlockSpec(memory_space=pltpu.VMEM)
    small_sum, = pcall(sum_body, name="sum_small", grid=(), in_specs=[vmem], out_specs=[vmem],
                       out_shape=[jax.ShapeDtypeStruct((SMALL_FULL_ROWS, SMALL_W), F32)], args=[small_all])
    small_sum = small_sum.reshape(-1)
    g_small = {}
    off = 0
    for n, s, a in SMALL:
        cnt = int(np.prod(s))
        gfull = small_sum[off:off + cnt].reshape(s)
        off += cnt
        if a is None:
            g_small[n] = gfull
        else:
            width = s[a] // N_DEV
            g_small[n] = lax.dynamic_slice_in_dim(gfull, me * width, width, axis=a)

    out = {}

    def emit(name, res, shape):
        for kind, arr in zip(("grad", "delta", "new_m", "new_v"), res):
            out[kind + "_" + name] = arr.reshape(shape)

    for name, key in (("ffn_w1", "w1t_"), ("ffn_w3", "w3t_")):
        shp = wts[name].shape
        view = lambda t: t.reshape((4,) + shp[2:])
        res = adamw_cols([io.recv[key + tag] for tag in FFN_TAGS], view(wts[name]), view(ms[name]), view(vs[name]),
                         "adamw_" + name)
        emit(name, res, shp)
    shp = wts["ffn_w2"].shape
    view = lambda t: t.reshape((4,) + shp[2:])
    res = adamw_rows([io.recv["w2_" + tag] for tag in FFN_TAGS], view(wts["ffn_w2"]), view(ms["ffn_w2"]),
                     view(vs["ffn_w2"]), "adamw_ffn_w2")
    emit("ffn_w2", res, shp)
    res = adamw_cols([io.recv["w_int"]], wts["ssm_w_in"], ms["ssm_w_in"], vs["ssm_w_in"], "adamw_ssm_w_in")
    emit("ssm_w_in", res, wts["ssm_w_in"].shape)
    for name, key in (("ssm_w_out", "w_out"), ("w_kv", "w_kv"), ("w_q", "w_q"), ("w_o", "w_o")):
        shp = wts[name].shape
        view = lambda t: t.reshape((1,) + shp[-2:])
        res = adamw_rows([io.recv[key]], view(wts[name]), view(ms[name]), view(vs[name]), "adamw_" + name)
        emit(name, res, shp)

    res_s = rowmap(lambda gg, ww, mm_, vv: _adamw(gg, ww, mm_, vv),
                   [_small_local(g_small), _small_local(wts), _small_local(ms), _small_local(vs)], [],
                   [(LANES, F32)] * 3, tm=SMALL_LOCAL_ROWS, name="adamw_small")
    flat_s = [r.reshape(-1) for r in res_s]
    off = 0
    for n, s, a in SMALL:
        shard = s if a is None else _shard_shape(s, a)
        cnt = int(np.prod(shard))
        out["grad_" + n] = g_small[n]
        for kind, arr in zip(("delta", "new_m", "new_v"), flat_s):
            out[kind + "_" + n] = arr[off:off + cnt].reshape(shard)
        off += cnt
    out["loss"] = loss
    out["grad_x"] = grad_x[None]
    return out


def kernel(x, ffn_norm, ffn_w1, ffn_w3, ffn_w2, ssm_norm, ssm_w_in, ssm_conv_w, ssm_conv_b, ssm_dt_bias, ssm_a_log, ssm_d, ssm_gate_norm, ssm_w_out, kv_norm, w_kv, k_norm, attn_norm, w_q, q_norm, sinks, w_o, rel_bias, loss_target, m_ffn_norm, m_ffn_w1, m_ffn_w3, m_ffn_w2, m_ssm_norm, m_ssm_w_in, m_ssm_conv_w, m_ssm_conv_b, m_ssm_dt_bias, m_ssm_a_log, m_ssm_d, m_ssm_gate_norm, m_ssm_w_out, m_kv_norm, m_w_kv, m_k_norm, m_attn_norm, m_w_q, m_q_norm, m_sinks, m_w_o, m_rel_bias, v_ffn_norm, v_ffn_w1, v_ffn_w3, v_ffn_w2, v_ssm_norm, v_ssm_w_in, v_ssm_conv_w, v_ssm_conv_b, v_ssm_dt_bias, v_ssm_a_log, v_ssm_d, v_ssm_gate_norm, v_ssm_w_out, v_kv_norm, v_w_kv, v_k_norm, v_attn_norm, v_w_q, v_q_norm, v_sinks, v_w_o, v_rel_bias):
    args = locals()
    wts = {n: args[n] for n in WEIGHT_NAMES}
    ms = {n: args["m_" + n] for n in WEIGHT_NAMES}
    vs = {n: args["v_" + n] for n in WEIGHT_NAMES}
    out = step(x, loss_target, wts, ms, vs)
    result = [out["loss"], out["grad_x"]]
    for kind in ("grad", "delta", "new_m", "new_v"):
        result += [out[kind + "_" + n] for n in WEIGHT_NAMES]
    return tuple(result)
```

```python
import functools
import math
import operator

import numpy as np
import jax
import jax.numpy as jnp
from jax import lax
from jax.experimental import pallas as pl
from jax.experimental.pallas import tpu as pltpu

F32 = jnp.float32
BF16 = jnp.bfloat16

D_MODEL = 1024
D_FF = 2816
N_DEV = 8
SSM_D_INNER = 2048
SSM_HEAD_DIM = 64
SSM_HEADS = 32
SSM_GROUPS = 4
SSM_STATE = 128
SSM_CONV = 4
SSM_CHUNK = 256
SSM_CONV_DIM = SSM_D_INNER + 2 * SSM_GROUPS * SSM_STATE
SSM_IN_DIM = SSM_D_INNER + SSM_CONV_DIM + SSM_HEADS
ATT_HEAD_DIM = 64
ATT_HEADS = 16
ATT_KV_HEADS = 2
ATT_GROUP = 8
ATT_WINDOW = 128
REL_BUCKETS = 32
EPS = 1e-6
NEG = -1e30

ADAM_LR = 0.001
ADAM_B1 = 0.9
ADAM_B2 = 0.999
ADAM_EPS = 1e-08
ADAM_WD = 0.01
ADAM_STEP = 10

VMEM_LIMIT_BYTES = 52 * 1024 * 1024
LANES = 128
MESH_ID = pl.DeviceIdType.MESH
ANY_SPEC = pl.BlockSpec(memory_space=pl.ANY)

NT = (((1,), (1,)), ((), ()))
TN = (((0,), (0,)), ((), ()))
NN = (((1,), (0,)), ((), ()))


def _pick(dim, cands):
    for c in cands:
        if dim % c == 0:
            return c
    return dim


def _my_index():
    return 4 * lax.axis_index("x") + 2 * lax.axis_index("y") + lax.axis_index("c")


def _peer(k):
    x, y, c = lax.axis_index("x"), lax.axis_index("y"), lax.axis_index("c")
    px = 1 - x if (k >> 2) & 1 else x
    py = 1 - y if (k >> 1) & 1 else y
    pc = 1 - c if k & 1 else c
    return (px, py, pc), 4 * px + 2 * py + pc


def _piece(ref, axis, d, n):
    if axis is None:
        return ref.at[d]
    return ref.at[(slice(None),) * axis + (pl.ds(pl.multiple_of(d * n, 8), n),)]


SIBLING = 1
CHIP_PEERS = (4, 2, 6)
N_CHIPS = 4
SEMS_PER_ITEM = N_DEV - 1


def _my_chip():
    return 2 * lax.axis_index("x") + lax.axis_index("y")


class Comm:
    def __init__(self, items):
        self.items = list(items)

    def dst_shapes(self):
        out = []
        for kind, src, axis in self.items:
            s = tuple(src.shape)
            if kind == "g":
                shp = (N_DEV,) + s
            elif kind == "g2":
                shp = (N_DEV,) + s if axis is None else s[:axis] + (N_DEV * s[axis],) + s[axis + 1:]
            elif kind == "sa":
                shp = (s[0], 1) + s[2:]
            else:
                shp = s
            out.append(jax.ShapeDtypeStruct(shp, src.dtype))
        return out

    def scratch(self):
        n = len(self.items)
        return [pltpu.SemaphoreType.DMA((n * SEMS_PER_ITEM,)), pltpu.SemaphoreType.DMA((n * SEMS_PER_ITEM,)),
                pltpu.SemaphoreType.DMA((n,))]

    def _run(self, srcs, dsts, sems, starting):
        send_sems, recv_sems, local_sems = sems
        me = _my_index()
        core = lax.axis_index("c")
        chip = _my_chip()
        for i, (kind, src, axis) in enumerate(self.items):
            s_ref, d_ref = srcs[i], dsts[i]
            base = i * SEMS_PER_ITEM

            def rdma(src_ref, dst_ref, j, peer):
                return pltpu.make_async_remote_copy(
                    src_ref=src_ref, dst_ref=dst_ref, send_sem=send_sems.at[base + j], recv_sem=recv_sems.at[base + j],
                    device_id=peer, device_id_type=MESH_ID)

            if kind == "g":
                local = pltpu.make_async_copy(s_ref, d_ref.at[me], local_sems.at[i])
                outs = [rdma(s_ref, d_ref.at[me], k - 1, _peer(k)[0]) for k in range(1, N_DEV)]
                if starting:
                    local.start()
                    for cp in outs:
                        cp.start()
                else:
                    for k in range(1, N_DEV):
                        rdma(s_ref, d_ref.at[_peer(k)[1]], k - 1, _peer(k)[0]).wait_recv()
                    for cp in outs:
                        cp.wait_send()
                    local.wait()
            elif kind == "g2":
                n = None if axis is None else src.shape[axis]
                mine = _piece(d_ref, axis, me, n)
                sib = _peer(SIBLING)[0]
                local = pltpu.make_async_copy(s_ref, mine, local_sems.at[i])
                outs = [rdma(s_ref, mine, 0, sib)] + [rdma(s_ref, mine, 1 + j, _peer(k)[0])
                                                      for j, k in enumerate(CHIP_PEERS)]
                if starting:
                    local.start()
                    for cp in outs:
                        cp.start()
                else:
                    passed = []
                    for j, k in enumerate(CHIP_PEERS):
                        theirs = _piece(d_ref, axis, _peer(k)[1], n)
                        rdma(s_ref, theirs, 1 + j, _peer(k)[0]).wait_recv()
                        fwd = rdma(theirs, theirs, 4 + j, sib)
                        fwd.start()
                        passed.append(fwd)
                    rdma(s_ref, _piece(d_ref, axis, _peer(SIBLING)[1], n), 0, sib).wait_recv()
                    for j, k in enumerate(CHIP_PEERS):
                        rdma(s_ref, _piece(d_ref, axis, _peer(k ^ SIBLING)[1], n), 4 + j, sib).wait_recv()
                    for cp in outs + passed:
                        cp.wait_send()
                    local.wait()
            elif kind == "sa":
                cp = rdma(s_ref.at[(slice(None), pl.ds(1 - core, 1))], d_ref, 0, _peer(SIBLING)[0])
                if starting:
                    cp.start()
                else:
                    cp.wait_recv()
                    cp.wait_send()
            else:
                local = pltpu.make_async_copy(s_ref.at[chip], d_ref.at[chip], local_sems.at[i])
                outs = [rdma(s_ref.at[_peer(k)[1] >> 1], d_ref.at[chip], 1 + j, _peer(k)[0])
                        for j, k in enumerate(CHIP_PEERS)]
                if starting:
                    local.start()
                    for cp in outs:
                        cp.start()
                else:
                    for j, k in enumerate(CHIP_PEERS):
                        rdma(s_ref.at[chip], d_ref.at[_peer(k)[1] >> 1], 1 + j, _peer(k)[0]).wait_recv()
                    for cp in outs:
                        cp.wait_send()
                    local.wait()

    def start(self, srcs, dsts, sems):
        self._run(srcs, dsts, sems, True)

    def wait(self, srcs, dsts, sems):
        self._run(srcs, dsts, sems, False)


def pcall(body, *, name, grid, in_specs, out_specs, out_shape, args, scratch=(), hook=None):
    cparams = pltpu.CompilerParams(dimension_semantics=("arbitrary",) * len(grid), vmem_limit_bytes=VMEM_LIMIT_BYTES)
    if hook is None:
        outs = pl.pallas_call(body, name=name, grid=grid, in_specs=list(in_specs), out_specs=list(out_specs),
                              out_shape=list(out_shape), scratch_shapes=list(scratch), compiler_params=cparams)(*args)
        return list(outs)
    comm, sink = hook
    n_in, n_out, n_scr, n_it = len(args), len(out_shape), len(scratch), len(comm.items)
    dims = tuple(grid)

    def wrapped(*refs):
        p = 0
        ins = refs[p:p + n_in]
        p += n_in
        csrc = refs[p:p + n_it]
        p += n_it
        outs = refs[p:p + n_out]
        p += n_out
        cdst = refs[p:p + n_it]
        p += n_it
        scr = refs[p:p + n_scr]
        p += n_scr
        sems = refs[p:p + 3]
        if dims:
            ids = [pl.program_id(a) for a in range(len(dims))]
            first = functools.reduce(operator.and_, [i == 0 for i in ids])
            last = functools.reduce(operator.and_, [i == d - 1 for i, d in zip(ids, dims)])

            @pl.when(first)
            def _():
                comm.start(csrc, cdst, sems)

            body(*ins, *outs, *scr)

            @pl.when(last)
            def _():
                comm.wait(csrc, cdst, sems)
        else:
            comm.start(csrc, cdst, sems)
            body(*ins, *outs, *scr)
            comm.wait(csrc, cdst, sems)

    res = pl.pallas_call(
        wrapped, name=name, grid=grid,
        in_specs=list(in_specs) + [ANY_SPEC] * n_it, out_specs=list(out_specs) + [ANY_SPEC] * n_it,
        out_shape=list(out_shape) + comm.dst_shapes(), scratch_shapes=list(scratch) + comm.scratch(),
        compiler_params=cparams,
    )(*args, *[src for _, src, _ in comm.items])
    res = list(res)
    sink(res[n_out:])
    return res[:n_out]


def comm_only(comm, name):
    got = []
    pcall(lambda *refs: None, name=name, grid=(), in_specs=[], out_specs=[], out_shape=[], args=[],
          hook=(comm, got.extend))
    return got


def mm(a, b, *, ta=False, tb=False, out_dtype=F32, res=None, alpha=1.0, name, hook=None):
    if ta:
        k_dim, m_dim = a.shape
    else:
        m_dim, k_dim = a.shape
    if tb:
        n_dim, k2 = b.shape
    else:
        k2, n_dim = b.shape
    assert k_dim == k2, (a.shape, b.shape, ta, tb)
    tn = _pick(n_dim, (1024, 1408, 512, 256, 128))
    tm = _pick(m_dim, (1024, 1408, 512, 256, 128)) if tn <= 1024 else _pick(m_dim, (512, 256, 128))
    tk = _pick(k_dim, (1024, 512, 256, 128)) if ta else _pick(k_dim, (512, 1408, 256, 128))
    nk = k_dim // tk
    has_res = res is not None
    dn = (((0 if ta else 1,), (1 if tb else 0,)), ((), ()))

    def body(*refs):
        if has_res:
            a_ref, b_ref, r_ref, o_ref, acc_ref = refs
        else:
            a_ref, b_ref, o_ref, acc_ref = refs
        k = pl.program_id(2)

        @pl.when(k == 0)
        def _():
            acc_ref[...] = jnp.zeros_like(acc_ref)

        acc_ref[...] += lax.dot_general(a_ref[...].astype(BF16), b_ref[...].astype(BF16), dn,
                                        preferred_element_type=F32)

        @pl.when(k == nk - 1)
        def _():
            r = acc_ref[...]
            if alpha != 1.0:
                r = r * alpha
            if has_res:
                r = r_ref[...] + r
            o_ref[...] = r.astype(o_ref.dtype)

    a_spec = pl.BlockSpec((tk, tm), lambda i, j, k: (k, i)) if ta else pl.BlockSpec((tm, tk), lambda i, j, k: (i, k))
    b_spec = pl.BlockSpec((tn, tk), lambda i, j, k: (j, k)) if tb else pl.BlockSpec((tk, tn), lambda i, j, k: (k, j))
    o_spec = pl.BlockSpec((tm, tn), lambda i, j, k: (i, j))
    in_specs = [a_spec, b_spec] + ([o_spec] if has_res else [])
    args = [a, b] + ([res] if has_res else [])
    out, = pcall(body, name=name, grid=(m_dim // tm, n_dim // tn, nk), in_specs=in_specs, out_specs=[o_spec],
                 out_shape=[jax.ShapeDtypeStruct((m_dim, n_dim), out_dtype)], args=args,
                 scratch=[pltpu.VMEM((tm, tn), F32)], hook=hook)
    return out


def rowmap(fn, rows, consts=(), out_rows=(), out_accs=(), *, tm, name, hook=None):
    first = rows[0][0] if isinstance(rows[0], tuple) else rows[0]
    t_dim = first.shape[0]
    assert t_dim % tm == 0, (t_dim, tm)
    n_r, n_c, n_o = len(rows), len(consts), len(out_rows)

    def body(*refs):
        ins = [r[...] for r in refs[:n_r + n_c]]
        o_refs = refs[n_r + n_c:]
        outs = tuple(fn(*ins))
        for o_ref, val in zip(o_refs[:n_o], outs[:n_o]):
            o_ref[...] = val.astype(o_ref.dtype)
        if out_accs:
            @pl.when(pl.program_id(0) == 0)
            def _():
                for o_ref in o_refs[n_o:]:
                    o_ref[...] = jnp.zeros_like(o_ref)

            for o_ref, val in zip(o_refs[n_o:], outs[n_o:]):
                o_ref[...] += val

    in_specs, args = [], []
    for r in rows:
        if isinstance(r, tuple):
            args.append(r[0])
            in_specs.append(r[1])
        else:
            args.append(r)
            in_specs.append(pl.BlockSpec((tm, r.shape[1]), lambda i: (i, 0)))
    for c in consts:
        args.append(c)
        in_specs.append(pl.BlockSpec(c.shape, lambda i, nd=c.ndim: (0,) * nd))
    out_specs = [pl.BlockSpec((tm, w), lambda i: (i, 0)) for (w, _) in out_rows]
    out_specs += [pl.BlockSpec(s, lambda i, nd=len(s): (0,) * nd) for s in out_accs]
    out_shape = [jax.ShapeDtypeStruct((t_dim, w), dt) for (w, dt) in out_rows]
    out_shape += [jax.ShapeDtypeStruct(s, F32) for s in out_accs]
    return pcall(body, name=name, grid=(t_dim // tm,), in_specs=in_specs, out_specs=out_specs, out_shape=out_shape,
                 args=args, hook=hook)


def _rms_fwd(x, g):
    r = lax.rsqrt(jnp.mean(x * x, axis=-1, keepdims=True) + EPS)
    return x * r * g


def _rms_bwd(x, g, dy):
    r = lax.rsqrt(jnp.mean(x * x, axis=-1, keepdims=True) + EPS)
    xh = x * r
    dg = jnp.sum(dy * xh, axis=0, keepdims=True)
    dxh = dy * g
    dx = r * (dxh - xh * jnp.mean(dxh * xh, axis=-1, keepdims=True))
    return dx, dg


def _sigmoid(x):
    return 1.0 / (1.0 + jnp.exp(-x))


def _silu(x):
    return x * _sigmoid(x)


def _silu_grad(x):
    s = _sigmoid(x)
    return s * (1.0 + x * (1.0 - s))


def _split3(x):
    hi = x.astype(BF16)
    r1 = x - hi.astype(F32)
    mid = r1.astype(BF16)
    lo = (r1 - mid.astype(F32)).astype(BF16)
    return hi, mid, lo


def _dot(a, b, dn=NN):
    return lax.dot_general(a.astype(BF16), b.astype(BF16), dn, preferred_element_type=F32)


FFN_TN = 1408
RESIDENT_TM = 512


def ffn_upgate(h, g, w1t, w3t, nm, hook=None):
    t_dim = h.shape[0]
    tm = _pick(t_dim, (512, 256, 128))
    tn = FFN_TN

    n_j = D_FF // tn
    u_w = D_MODEL // n_j

    def body(h_ref, g_ref, w1_ref, w3_ref, u_ref, a_ref, b_ref, hm_ref):
        uu = _rms_fwd(h_ref[...], g_ref[...]).astype(BF16)
        for j in range(n_j):
            @pl.when(pl.program_id(0) == j)
            def _(j=j):
                u_ref[...] = uu[:, j * u_w:(j + 1) * u_w]

        a = lax.dot_general(uu, w1_ref[...], NT, preferred_element_type=F32)
        b = lax.dot_general(uu, w3_ref[...], NT, preferred_element_type=F32)
        a_ref[...] = a.astype(a_ref.dtype)
        b_ref[...] = b.astype(b_ref.dtype)
        hm_ref[...] = (_silu(a) * b).astype(hm_ref.dtype)

    row_spec = pl.BlockSpec((tm, D_MODEL), lambda j, i: (i, 0))
    w_spec = pl.BlockSpec((tn, D_MODEL), lambda j, i: (j, 0))
    o_spec = pl.BlockSpec((tm, tn), lambda j, i: (i, j))
    o_shape = jax.ShapeDtypeStruct((t_dim, D_FF), BF16)
    return pcall(body, name=nm, grid=(D_FF // tn, t_dim // tm),
                 in_specs=[row_spec, pl.BlockSpec((1, D_MODEL), lambda j, i: (0, 0)), w_spec, w_spec],
                 out_specs=[pl.BlockSpec((tm, u_w), lambda j, i: (i, j))] + [o_spec] * 3,
                 out_shape=[jax.ShapeDtypeStruct((t_dim, D_MODEL), BF16)] + [o_shape] * 3,
                 args=[h, g, w1t, w3t], hook=hook)


def ffn_dgate(dout_bf, w2, a, b, nm, hook=None):
    t_dim = dout_bf.shape[0]
    tm = _pick(t_dim, (512, 256, 128))
    tn = FFN_TN

    def body(d_ref, w2_ref, a_ref, b_ref, da_ref, db_ref):
        dhm = 0.5 * lax.dot_general(d_ref[...], w2_ref[...], NT, preferred_element_type=F32)
        av = a_ref[...].astype(F32)
        bv = b_ref[...].astype(F32)
        sg = _sigmoid(av)
        da_ref[...] = (dhm * bv * (sg * (1.0 + av * (1.0 - sg)))).astype(da_ref.dtype)
        db_ref[...] = (dhm * (av * sg)).astype(db_ref.dtype)

    t_spec = pl.BlockSpec((tm, tn), lambda j, i: (i, j))
    o_shape = jax.ShapeDtypeStruct((t_dim, D_FF), BF16)
    return pcall(body, name=nm, grid=(D_FF // tn, t_dim // tm),
                 in_specs=[pl.BlockSpec((tm, D_MODEL), lambda j, i: (i, 0)),
                           pl.BlockSpec((tn, D_MODEL), lambda j, i: (j, 0)), t_spec, t_spec],
                 out_specs=[t_spec] * 2, out_shape=[o_shape] * 2, args=[dout_bf, w2, a, b], hook=hook)


def ffn_fwd(h, g, tag, io, target=None):
    nm = "f" + tag
    u, a, b, hm = ffn_upgate(h, g, io.w("w1t_" + tag), io.w("w3t_" + tag), nm + "_upgate",
                             hook=io.hook(nm + "_upgate"))
    if target is None:
        return mm(hm, io.w("w2_" + tag), res=h, alpha=0.5, name=nm + "_down"), (u, a, b, hm)

    def down_loss(hmv, hv, t, w2):
        e = hv + 0.5 * _dot(hmv, w2) - t
        d = e * (1.0 / D_MODEL)
        return d, d, jnp.sum(e * e, axis=0, keepdims=True)

    res = rowmap(down_loss, [hm, h, target], [io.w("w2_" + tag)], [(D_MODEL, F32), (D_MODEL, BF16)],
                 [(1, D_MODEL)], tm=RESIDENT_TM, name=nm + "_down_loss")
    return res, (u, a, b, hm)


def du_norm_bwd(pairs, h, g, dout, nm, hook=None):
    t_dim = h.shape[0]
    tm = RESIDENT_TM
    n_p = len(pairs)

    def body(*refs):
        h_ref, d_ref, g_ref = refs[2 * n_p:2 * n_p + 3]
        dh_ref, dhb_ref, dg_ref = refs[2 * n_p + 3:]
        du = None
        for p, (_, _, tb) in enumerate(pairs):
            t = lax.dot_general(refs[2 * p][...].astype(BF16), refs[2 * p + 1][...].astype(BF16), NT if tb else NN,
                                preferred_element_type=F32)
            du = t if du is None else du + t
        dx, dg = _rms_bwd(h_ref[...], g_ref[...], du)
        dh = d_ref[...] + dx
        dh_ref[...] = dh
        dhb_ref[...] = dh.astype(dhb_ref.dtype)

        @pl.when(pl.program_id(0) == 0)
        def _():
            dg_ref[...] = jnp.zeros_like(dg_ref)

        dg_ref[...] += dg

    in_specs, args = [], []
    for a, b, _ in pairs:
        in_specs += [pl.BlockSpec((tm, a.shape[1]), lambda i: (i, 0)), pl.BlockSpec(b.shape, lambda i: (0, 0))]
        args += [a, b]
    row_spec = pl.BlockSpec((tm, D_MODEL), lambda i: (i, 0))
    vec_spec = pl.BlockSpec((1, D_MODEL), lambda i: (0, 0))
    return pcall(body, name=nm, grid=(t_dim // tm,), in_specs=in_specs + [row_spec, row_spec, vec_spec],
                 out_specs=[row_spec, row_spec, vec_spec],
                 out_shape=[jax.ShapeDtypeStruct((t_dim, D_MODEL), F32), jax.ShapeDtypeStruct((t_dim, D_MODEL), BF16),
                            jax.ShapeDtypeStruct((1, D_MODEL), F32)],
                 args=args + [h, dout, g], hook=hook)


def ffn_bwd(h, g, tag, saved, dout, dout_bf, io):
    nm = "f" + tag
    w1t, w3t, w2 = io.w("w1t_" + tag), io.w("w3t_" + tag), io.w("w2_" + tag)
    u, a, b, hm = saved
    io.put("w2_" + tag, mm(hm, dout_bf, ta=True, alpha=0.5, out_dtype=BF16, name=nm + "_dw2",
                           hook=io.hook(nm + "_dw2")))
    da, db = ffn_dgate(dout_bf, w2, a, b, nm + "_dgate", hook=io.hook(nm + "_dgate"))
    io.put("w1t_" + tag, mm(da, u, ta=True, out_dtype=BF16, name=nm + "_dw1"))
    io.put("w3t_" + tag, mm(db, u, ta=True, out_dtype=BF16, name=nm + "_dw3", hook=io.hook(nm + "_dw3")))
    return du_norm_bwd([(da, w1t, False), (db, w3t, False)], h, g, dout, nm + "_du", hook=io.hook(nm + "_du"))


def conv_input_grad(d_parts, w, nm):
    tm = 256
    t_dim = d_parts[0].shape[0]
    n_tiles = t_dim // tm

    def fn(d1, n1, d2, n2, d3, n3, ww):
        d = jnp.concatenate([d1, d2, d3], axis=1)
        nxt = jnp.concatenate([n1, n2, n3], axis=1)
        nxt = jnp.where(pl.program_id(0) < n_tiles - 1, nxt, 0.0)
        dd = jnp.concatenate([d, nxt], axis=0)
        out = dd[3:3 + tm] * ww[0:1]
        for k in range(1, SSM_CONV):
            out = out + dd[3 - k:3 - k + tm] * ww[k:k + 1]
        return (out,)

    rows = []
    for d in d_parts:
        below = pl.BlockSpec((8, d.shape[1]), lambda i: (jnp.minimum((i + 1) * (tm // 8), t_dim // 8 - 1), 0))
        rows += [d, (d, below)]
    dx, = rowmap(fn, rows, [w], [(SSM_CONV_DIM, BF16)], tm=tm, name=nm)
    return dx


GRP_W = SSM_D_INNER // SSM_GROUPS
HPG = SSM_HEADS // SSM_GROUPS
HEAD_SHIFT = 6


def _split2(x):
    hi = x.astype(BF16)
    return hi, (x - hi.astype(F32)).astype(BF16)


def _expand_mats():
    e = ((lax.broadcasted_iota(jnp.int32, (HPG, GRP_W), 1) >> HEAD_SHIFT)
         == lax.broadcasted_iota(jnp.int32, (HPG, GRP_W), 0)).astype(BF16)
    et = ((lax.broadcasted_iota(jnp.int32, (GRP_W, HPG), 0) >> HEAD_SHIFT)
          == lax.broadcasted_iota(jnp.int32, (GRP_W, HPG), 1)).astype(BF16)
    return e, et


def _expand(v, e_m):
    hi, lo = _split2(v)
    return jnp.dot(hi, e_m, preferred_element_type=F32) + jnp.dot(lo, e_m, preferred_element_type=F32)


def _reduce8(v, et_m):
    hi, lo = _split2(v)
    return jnp.dot(hi, et_m, preferred_element_type=F32) + jnp.dot(lo, et_m, preferred_element_type=F32)


def _ssd_group_terms(dt_ref, dtT_ref, arow_ref, acol_ref):
    L = SSM_CHUNK
    r = lax.broadcasted_iota(jnp.int32, (L, L), 0)
    c = lax.broadcasted_iota(jnp.int32, (L, L), 1)
    tril = (r >= c).astype(BF16)
    triu = (r <= c).astype(BF16)
    dtg = dt_ref[0]
    acol = None
    for p in _split3(dtg * arow_ref[0]):
        t = jnp.dot(tril, p, preferred_element_type=F32)
        acol = t if acol is None else acol + t
    arowT = None
    for p in _split3(dtT_ref[0] * acol_ref[0]):
        t = jnp.dot(p, triu, preferred_element_type=F32)
        arowT = t if arowT is None else arowT + t
    return dtg, acol, arowT, r >= c


def _state_decay(a_last_col, et_m):
    hi, lo = _split2(jnp.broadcast_to(jnp.exp(a_last_col), (HPG, SSM_STATE)))
    return jnp.dot(et_m, hi, preferred_element_type=F32) + jnp.dot(et_m, lo, preferred_element_type=F32)


def _conv_block(x_ref, halo_ref, w_ref, b_ref, first):
    L = SSM_CHUNK
    xx = jnp.concatenate([jnp.where(first, 0.0, halo_ref[...]), x_ref[...]], axis=0)
    w = w_ref[...]
    shifted = [xx[5 + k:5 + k + L] for k in range(SSM_CONV)]
    acc = b_ref[...] + shifted[0] * w[0:1]
    for k in range(1, SSM_CONV):
        acc = acc + shifted[k] * w[k:k + 1]
    return acc, shifted


def _ssd_specs(nc, rev):
    L, N = SSM_CHUNK, SSM_STATE
    xcols = SSM_D_INNER // LANES
    ch = (lambda c: nc - 1 - c) if rev else (lambda c: c)
    above = lambda c: jnp.maximum(ch(c) * (L // 8) - 1, 0)
    specs = []
    for width, col in ((GRP_W, lambda g: g), (N, lambda g: xcols + g), (N, lambda g: xcols + SSM_GROUPS + g)):
        specs += [
            pl.BlockSpec((L, width), lambda c, g, col=col: (ch(c), col(g))),
            pl.BlockSpec((8, width), lambda c, g, col=col: (above(c), col(g))),
            pl.BlockSpec((SSM_CONV, width), lambda c, g, col=col: (0, col(g))),
            pl.BlockSpec((1, width), lambda c, g, col=col: (0, col(g))),
        ]
    return specs + [
        pl.BlockSpec((1, L, HPG), lambda c, g: (g, ch(c), 0)),
        pl.BlockSpec((1, HPG, L), lambda c, g: (g, 0, ch(c))),
        pl.BlockSpec((1, 1, HPG), lambda c, g: (g, 0, 0)),
        pl.BlockSpec((1, HPG, 1), lambda c, g: (g, 0, 0)),
        pl.BlockSpec((1, GRP_W), lambda c, g: (0, g)),
    ]


def ssd_fwd(xbc_raw, conv_w, conv_b, dt_g, dtT_g, a_row, a_col, dvec, nm, hook=None):
    t_dim = xbc_raw.shape[0]
    L, P, N = SSM_CHUNK, SSM_HEAD_DIM, SSM_STATE
    nc = t_dim // L

    def body(x_ref, xh_ref, xw_ref, xb_ref, b_ref, bh_ref, bw_ref, bb_ref, c_ref, ch_ref, cw_ref, cb_ref,
             dt_ref, dtT_ref, arow_ref, acol_ref, dvec_ref, y_ref, st_ref, s_s):
        ci = pl.program_id(0)
        g = pl.program_id(1)

        @pl.when((ci == 0) & (g == 0))
        def _():
            s_s[...] = jnp.zeros_like(s_s)

        e_m, et_m = _expand_mats()
        dtg, acol, arowT, causal = _ssd_group_terms(dt_ref, dtT_ref, arow_ref, acol_ref)
        a_last_row = acol[L - 1:L, :]
        x = _silu(_conv_block(x_ref, xh_ref, xw_ref, xb_ref, ci == 0)[0])
        bm = _silu(_conv_block(b_ref, bh_ref, bw_ref, bb_ref, ci == 0)[0])
        cm = _silu(_conv_block(c_ref, ch_ref, cw_ref, cb_ref, ci == 0)[0])
        cb = _dot(cm, bm, NT)
        s = s_s[g]
        st_ref[0, 0] = s
        ea_x = _expand(jnp.exp(acol), e_m)
        dt_x = _expand(dtg, e_m)
        w_x = _expand(jnp.exp(a_last_row - acol) * dtg, e_m)
        yb = ea_x * _dot(cm, s, NT) + dvec_ref[...] * x
        xd = (x * dt_x).astype(BF16)
        for e in range(HPG):
            sl = slice(e * P, (e + 1) * P)
            lm = jnp.exp(jnp.where(causal, acol[:, e:e + 1] - arowT[e:e + 1, :], NEG))
            m = (cb * lm).astype(BF16)
            y_ref[:, sl] = yb[:, sl] + jnp.dot(m, xd[:, sl], preferred_element_type=F32)
        s_s[g] = _state_decay(arowT[:, L - 1:L], et_m) * s + _dot(x * w_x, bm, TN)

    out_specs = [
        pl.BlockSpec((L, GRP_W), lambda c, g: (c, g)),
        pl.BlockSpec((1, 1, GRP_W, N), lambda c, g: (c, g, 0, 0)),
    ]
    return pcall(
        body, name=nm, grid=(nc, SSM_GROUPS), in_specs=_ssd_specs(nc, False), out_specs=out_specs,
        out_shape=[jax.ShapeDtypeStruct((t_dim, SSM_D_INNER), F32),
                   jax.ShapeDtypeStruct((nc, SSM_GROUPS, GRP_W, N), F32)],
        scratch=[pltpu.VMEM((SSM_GROUPS, GRP_W, N), F32)],
        args=[xbc_raw, xbc_raw, conv_w, conv_b] * 3 + [dt_g, dtT_g, a_row, a_col, dvec], hook=hook)


def ssd_bwd(dy, xbc_raw, conv_w, conv_b, dt_g, dtT_g, a_row, a_col, dvec, states, nm, hook=None):
    t_dim = xbc_raw.shape[0]
    L, P, N = SSM_CHUNK, SSM_HEAD_DIM, SSM_STATE
    nc = t_dim // L

    def body(dy_ref, x_ref, xh_ref, xw_ref, xb_ref, b_ref, bh_ref, bw_ref, bb_ref, c_ref, ch_ref, cw_ref, cb_ref,
             dt_ref, dtT_ref, arow_ref, acol_ref, dvec_ref, st_ref,
             dx_ref, db_ref, dc_ref, da_ref, ddt_ref, dd_ref, dwx_ref, dwb_ref, dwc_ref, dbx_ref, dbb_ref, dbc_ref,
             ds_s, yd_s, dxd_s):
        ci = pl.program_id(0)
        g = pl.program_id(1)

        @pl.when((ci == 0) & (g == 0))
        def _():
            ds_s[...] = jnp.zeros_like(ds_s)
            for r in (dd_ref, dwx_ref, dwb_ref, dwc_ref, dbx_ref, dbb_ref, dbc_ref):
                r[...] = jnp.zeros_like(r)

        e_m, et_m = _expand_mats()
        dtg, acol, arowT, causal = _ssd_group_terms(dt_ref, dtT_ref, arow_ref, acol_ref)
        a_last_row = acol[L - 1:L, :]
        first = ci == nc - 1
        pre_x, sh_x = _conv_block(x_ref, xh_ref, xw_ref, xb_ref, first)
        pre_b, sh_b = _conv_block(b_ref, bh_ref, bw_ref, bb_ref, first)
        pre_c, sh_c = _conv_block(c_ref, ch_ref, cw_ref, cb_ref, first)
        sg_x, sg_b, sg_c = _sigmoid(pre_x), _sigmoid(pre_b), _sigmoid(pre_c)
        x = pre_x * sg_x
        dy = dy_ref[...]
        bm = pre_b * sg_b
        cm = pre_c * sg_c
        cb = _dot(cm, bm, NT)
        s = st_ref[0, 0]
        dsp = ds_s[g]
        ew8 = jnp.exp(a_last_row - acol)
        ea_x = _expand(jnp.exp(acol), e_m)
        dt_x = _expand(dtg, e_m)
        ew_x = _expand(ew8, e_m)
        w_x = ew_x * dt_x
        z = _dot(cm, s, NT)
        dz = ea_x * dy
        dc = _dot(dz, s)
        ds_y = _dot(dz, cm, TN)
        du = _dot(bm, dsp, NT)
        u = x * w_x
        db = _dot(u, dsp)
        xd = (x * dt_x).astype(BF16)
        dyb = dy.astype(BF16)
        dcb = jnp.zeros((L, L), F32)
        for e in range(HPG):
            sl = slice(e * P, (e + 1) * P)
            lm = jnp.exp(jnp.where(causal, acol[:, e:e + 1] - arowT[e:e + 1, :], NEG))
            m = (cb * lm).astype(BF16)
            yd_s[:, sl] = jnp.dot(m, xd[:, sl], preferred_element_type=F32)
            dxd_s[:, sl] = lax.dot_general(m, dyb[:, sl], TN, preferred_element_type=F32)
            dcb = dcb + lax.dot_general(dyb[:, sl], xd[:, sl], NT, preferred_element_type=F32) * lm
        dxd = dxd_s[...]

        def through_conv(d_act, pre, sg, shifted, d_ref, dw_ref, dbias_ref):
            d_pre = d_act * (sg * (1.0 + pre * (1.0 - sg)))
            d_ref[...] = d_pre
            dw_ref[g] += jnp.concatenate([jnp.sum(d_pre * sh, axis=0, keepdims=True) for sh in shifted], axis=0)
            dbias_ref[g] += jnp.sum(d_pre, axis=0, keepdims=True)

        through_conv(dvec_ref[...] * dy + du * w_x + dt_x * dxd, pre_x, sg_x, sh_x, dx_ref, dwx_ref, dbx_ref)
        ddt = _reduce8(x * (ew_x * du + dxd), et_m)
        da = (_reduce8(dz * z + dyb.astype(F32) * yd_s[...], et_m)
              - _reduce8(xd.astype(F32) * dxd + du * u, et_m))
        dwa_row = _reduce8(jnp.broadcast_to(jnp.sum(du * u, axis=0, keepdims=True), (8, GRP_W)), et_m)[0:1]
        t_nh = None
        for p in _split3(dsp * s):
            t = lax.dot_general(p, et_m, TN, preferred_element_type=F32)
            t_nh = t if t_nh is None else t_nh + t
        d_last = dwa_row + jnp.exp(a_last_row) * jnp.sum(t_nh, axis=0, keepdims=True)
        row_l = lax.broadcasted_iota(jnp.int32, (L, 1), 0)
        da_ref[0] = da + jnp.where(row_l == L - 1, d_last, 0.0)
        ddt_ref[0] = ddt
        dd_ref[g] += jnp.sum(dy * x, axis=0, keepdims=True)
        through_conv(dc + _dot(dcb, bm), pre_c, sg_c, sh_c, dc_ref, dwc_ref, dbc_ref)
        through_conv(db + _dot(dcb, cm, TN), pre_b, sg_b, sh_b, db_ref, dwb_ref, dbb_ref)
        ds_s[g] = _state_decay(arowT[:, L - 1:L], et_m) * dsp + ds_y

    rc = lambda c: nc - 1 - c
    in_specs = ([pl.BlockSpec((L, GRP_W), lambda c, g: (rc(c), g))] + _ssd_specs(nc, True)
                + [pl.BlockSpec((1, 1, GRP_W, N), lambda c, g: (rc(c), g, 0, 0))])
    whole = lambda *shape: pl.BlockSpec(shape, lambda c, g: (0,) * len(shape))
    out_specs = [
        pl.BlockSpec((L, GRP_W), lambda c, g: (rc(c), g)),
        pl.BlockSpec((L, N), lambda c, g: (rc(c), g)),
        pl.BlockSpec((L, N), lambda c, g: (rc(c), g)),
        pl.BlockSpec((1, L, HPG), lambda c, g: (g, rc(c), 0)),
        pl.BlockSpec((1, L, HPG), lambda c, g: (g, rc(c), 0)),
        whole(SSM_GROUPS, 1, GRP_W),
        whole(SSM_GROUPS, SSM_CONV, GRP_W), whole(SSM_GROUPS, SSM_CONV, N), whole(SSM_GROUPS, SSM_CONV, N),
        whole(SSM_GROUPS, 1, GRP_W), whole(SSM_GROUPS, 1, N), whole(SSM_GROUPS, 1, N),
    ]
    gn = SSM_GROUPS * N
    acc = lambda *shape: jax.ShapeDtypeStruct(shape, F32)
    out_shape = [
        acc(t_dim, SSM_D_INNER), acc(t_dim, gn), acc(t_dim, gn), acc(SSM_GROUPS, t_dim, HPG),
        acc(SSM_GROUPS, t_dim, HPG), acc(SSM_GROUPS, 1, GRP_W),
        acc(SSM_GROUPS, SSM_CONV, GRP_W), acc(SSM_GROUPS, SSM_CONV, N), acc(SSM_GROUPS, SSM_CONV, N),
        acc(SSM_GROUPS, 1, GRP_W), acc(SSM_GROUPS, 1, N), acc(SSM_GROUPS, 1, N),
    ]
    return pcall(
        body, name=nm, grid=(nc, SSM_GROUPS), in_specs=in_specs, out_specs=out_specs, out_shape=out_shape,
        scratch=[pltpu.VMEM((SSM_GROUPS, GRP_W, N), F32), pltpu.VMEM((L, GRP_W), F32), pltpu.VMEM((L, GRP_W), F32)],
        args=[dy] + [xbc_raw, xbc_raw, conv_w, conv_b] * 3 + [dt_g, dtT_g, a_row, a_col, dvec, states], hook=hook)


def _softplus(x):
    return jnp.maximum(x, 0.0) + jnp.log(1.0 + jnp.exp(-jnp.abs(x)))


def ssd_dt_bwd(da, ddt, dt, dt_raw, a_row, dt_bias, nm):
    L = SSM_CHUNK

    def fn(d_a, d_dt, dtv, raw, ar, bias):
        r = lax.broadcasted_iota(jnp.int32, (L, L), 0)
        c = lax.broadcasted_iota(jnp.int32, (L, L), 1)
        triu = (r <= c).astype(BF16)
        acc = None
        for p in _split3(d_a):
            t = jnp.dot(triu, p, preferred_element_type=F32)
            acc = t if acc is None else acc + t
        d_dt = d_dt + acc * ar
        d_a_h = jnp.sum(acc * dtv, axis=0, keepdims=True)
        d_raw = d_dt * _sigmoid(raw + bias)
        return d_raw, d_a_h, jnp.sum(d_raw, axis=0, keepdims=True)

    return rowmap(fn, [da, ddt, dt, dt_raw], [a_row, dt_bias], [(SSM_HEADS, BF16)],
                  [(1, SSM_HEADS), (1, SSM_HEADS)], tm=L, name=nm)


GN_W = SSM_D_INNER // SSM_GROUPS


def mamba_fwd(h, p, nm, io):
    def in_proj(x, gg, w_zt, w_xbct, w_dtt):
        uu = _rms_fwd(x, gg).astype(BF16)
        return uu, _dot(uu, w_zt, NT), _dot(uu, w_xbct, NT), _dot(uu, w_dtt, NT)

    u, z, xbc_raw, dt_raw = rowmap(in_proj, [h], [p["ssm_norm"], p["w_zt"], p["w_xbct"], p["w_dtt"]],
                                   [(D_MODEL, BF16), (SSM_D_INNER, F32), (SSM_CONV_DIM, F32), (SSM_HEADS, F32)],
                                   tm=RESIDENT_TM, name=nm + "_in", hook=io.hook(nm + "_in"))
    dt, = rowmap(lambda r, b: (_softplus(r + b),), [dt_raw], [p["dt_bias"]], [(SSM_HEADS, F32)], tm=256,
                 name=nm + "_softplus")
    dt_g = dt.reshape(-1, SSM_GROUPS, HPG).transpose(1, 0, 2)
    dtT_g = dt_g.transpose(0, 2, 1)
    y, states = ssd_fwd(xbc_raw, p["conv_w"], p["conv_b"], dt_g, dtT_g, p["a_row"], p["a_col"], p["dvec"],
                        nm + "_ssd", hook=io.hook(nm + "_ssd"))

    def gate_norm_out(yv, zv, hv, gg, w_out):
        t = yv * _silu(zv)
        yn = jnp.concatenate([_rms_fwd(t[:, k * GN_W:(k + 1) * GN_W], gg[:, k * GN_W:(k + 1) * GN_W])
                              for k in range(SSM_GROUPS)], axis=1).astype(BF16)
        return yn, hv + _dot(yn, w_out)

    yn, out = rowmap(gate_norm_out, [y, z, h], [p["gate_norm"], p["w_out"]],
                     [(SSM_D_INNER, BF16), (D_MODEL, F32)], tm=RESIDENT_TM, name=nm + "_out")
    return out, (u, z, xbc_raw, dt_raw, dt, dt_g, dtT_g, y, states, yn)


def mamba_bwd(h, p, saved, dout, dout_bf, nm, io):
    u, z, xbc_raw, dt_raw, dt, dt_g, dtT_g, y, states, yn = saved
    g = {}
    io.put("w_out", mm(yn, dout_bf, ta=True, out_dtype=BF16, name=nm + "_dwout"))

    def gate_norm_bwd(d_o, yv, zv, gg, w_out):
        d = _dot(d_o, w_out, NT)
        sz = _silu(zv)
        t = yv * sz
        dts, dgs = [], []
        for k in range(SSM_GROUPS):
            sl = slice(k * GN_W, (k + 1) * GN_W)
            dt_k, dg_k = _rms_bwd(t[:, sl], gg[:, sl], d[:, sl])
            dts.append(dt_k)
            dgs.append(dg_k)
        d_t = jnp.concatenate(dts, axis=1)
        return d_t * sz, d_t * yv * _silu_grad(zv), jnp.concatenate(dgs, axis=1)

    dy, dz, g["gate_norm"] = rowmap(gate_norm_bwd, [dout_bf, y, z], [p["gate_norm"], p["w_out"]],
                                    [(SSM_D_INNER, F32), (SSM_D_INNER, BF16)], [(1, SSM_D_INNER)], tm=256,
                                    name=nm + "_dgatenorm")
    d_x, d_b, d_c, da_g, ddt_g, dd, dwx, dwb, dwc, dbx, dbb, dbc = ssd_bwd(
        dy, xbc_raw, p["conv_w"], p["conv_b"], dt_g, dtT_g, p["a_row"], p["a_col"], p["dvec"], states, nm + "_dssd",
        hook=io.hook(nm + "_dssd"))
    g["dvec"] = dd
    by_lane = lambda t: t.transpose(1, 0, 2).reshape(t.shape[1], -1)
    g["conv_w"] = jnp.concatenate([by_lane(dwx), by_lane(dwb), by_lane(dwc)], axis=1)
    g["conv_b"] = jnp.concatenate([by_lane(dbx), by_lane(dbb), by_lane(dbc)], axis=1)
    per_head = lambda t: t.transpose(1, 0, 2).reshape(-1, SSM_HEADS)
    ddt_raw, g["a"], g["dt_bias"] = ssd_dt_bwd(per_head(da_g), per_head(ddt_g), dt, dt_raw, p["a_heads"],
                                               p["dt_bias"], nm + "_ddt")
    dxbc_raw = conv_input_grad([d_x, d_b, d_c], p["conv_w"], nm + "_dconv")
    io.put("w_int", jnp.concatenate([mm(dz, u, ta=True, out_dtype=BF16, name=nm + "_dwz"),
                                     mm(dxbc_raw, u, ta=True, out_dtype=BF16, name=nm + "_dwxbc"),
                                     mm(ddt_raw, u, ta=True, out_dtype=BF16, name=nm + "_dwdt")], axis=0))
    dh, dh_bf, g["ssm_norm"] = du_norm_bwd(
        [(dz, p["w_zt"], False), (dxbc_raw, p["w_xbct"], False), (ddt_raw, p["w_dtt"], False)],
        h, p["ssm_norm"], dout, nm + "_du", hook=io.hook(nm + "_du"))
    return dh, dh_bf, g


KV_W = ATT_KV_HEADS * ATT_HEAD_DIM


def kv_fwd(h, p, nm):
    def kv_proj(x, gg, w_kv, gk):
        uu = _rms_fwd(x, gg).astype(BF16)
        t = _dot(uu, w_kv)
        ks = [_rms_fwd(t[:, j * ATT_HEAD_DIM:(j + 1) * ATT_HEAD_DIM], gk) for j in range(ATT_KV_HEADS)]
        return uu, t, jnp.concatenate(ks, axis=1), t[:, KV_W:]

    u, kv_raw, k, v = rowmap(kv_proj, [h], [p["kv_norm"], p["w_kv"], p["k_norm"]],
                             [(D_MODEL, BF16), (2 * KV_W, F32), (KV_W, F32), (KV_W, F32)], tm=RESIDENT_TM,
                             name=nm + "_proj")
    return k, v, (u, kv_raw)


def kv_bwd(h, p, saved, dk_cur, dk_prev, dv_cur, dv_prev, dout, nm, io):
    u, kv_raw = saved
    t_dim = h.shape[0]
    tm = ATT_WINDOW
    nb = t_dim // tm
    nxt = pl.BlockSpec((tm, KV_W), lambda i: (jnp.minimum(i + 1, nb - 1), 0))

    def fn(dkc, dkp, dvc, dvp, t, gg):
        live = pl.program_id(0) < nb - 1
        dk = dkc + jnp.where(live, dkp, 0.0)
        dv = dvc + jnp.where(live, dvp, 0.0)
        outs, dgs = [], None
        for j in range(ATT_KV_HEADS):
            sl = slice(j * ATT_HEAD_DIM, (j + 1) * ATT_HEAD_DIM)
            dx, dg = _rms_bwd(t[:, sl], gg, dk[:, sl])
            outs.append(dx)
            dgs = dg if dgs is None else dgs + dg
        return jnp.concatenate(outs + [dv], axis=1), dgs

    dkv_raw, dknorm = rowmap(fn, [dk_cur, (dk_prev, nxt), dv_cur, (dv_prev, nxt), kv_raw], [p["k_norm"]],
                             [(2 * KV_W, BF16)], [(1, ATT_HEAD_DIM)], tm=tm, name=nm + "_dknorm",
                             hook=io.hook(nm + "_dknorm"))
    g = {"k_norm": dknorm}
    io.put("w_kv", mm(u, dkv_raw, ta=True, out_dtype=BF16, name=nm + "_dwkv"))
    dh, dh_bf, g["kv_norm"] = du_norm_bwd([(dkv_raw, p["w_kv"], True)], h, p["kv_norm"], dout, nm + "_du",
                                          hook=io.hook(nm + "_du"))
    return dh, dh_bf, g


def _attn_scores(q_ref, kp_ref, kc_ref, vp_ref, vc_ref, qn_ref, bias_ref, sink_ref, kv, mxu_sum):
    hd = ATT_HEAD_DIM
    blk = ATT_WINDOW
    sl = slice(kv * hd, (kv + 1) * hd)
    kk = jnp.concatenate([kp_ref[:, sl], kc_ref[:, sl]], axis=0)
    vv = jnp.concatenate([vp_ref[:, sl], vc_ref[:, sl]], axis=0)
    gq = qn_ref[...]
    raws, rinvs = [], []
    for r in range(ATT_GROUP):
        hh = kv * ATT_GROUP + r
        x = q_ref[:, hh * hd:(hh + 1) * hd]
        raws.append(x)
        rinvs.append(lax.rsqrt(jnp.mean(x * x, axis=-1, keepdims=True) + EPS))
    xh = jnp.concatenate([x * ri for x, ri in zip(raws, rinvs)], axis=0)
    rinv = jnp.concatenate(rinvs, axis=0)
    q8 = xh * gq
    s = _dot(q8, kk, NT) * (hd ** -0.5) + bias_ref[kv]
    colk = lax.broadcasted_iota(jnp.int32, (1, 2 * blk), 1)
    s = jnp.where((pl.program_id(0) > 0) | (colk >= blk), s, NEG)
    sink = sink_ref[kv]
    m = jnp.maximum(jnp.max(s, axis=-1, keepdims=True), sink)
    pexp = jnp.exp(s - m)
    e_sink = jnp.exp(sink - m)
    if not mxu_sum:
        inv_den = 1.0 / (jnp.sum(pexp, axis=-1, keepdims=True) + e_sink)
        return kk, vv, xh, rinv, q8, pexp * inv_den, e_sink * inv_den
    ones = jnp.ones((2 * blk, LANES), BF16)
    inv_den = 1.0 / (jnp.dot(pexp.astype(BF16), ones, preferred_element_type=F32) + e_sink)
    return kk, vv, xh, rinv, q8, pexp * jnp.concatenate([inv_den, inv_den], axis=1), e_sink * inv_den[:, :1]


def _attn_specs(nb):
    blk = ATT_WINDOW
    cur = lambda i: (i, 0)
    prev = lambda i: (jnp.maximum(i - 1, 0), 0)
    return [
        pl.BlockSpec((blk, D_MODEL), cur),
        pl.BlockSpec((blk, KV_W), prev), pl.BlockSpec((blk, KV_W), cur),
        pl.BlockSpec((blk, KV_W), prev), pl.BlockSpec((blk, KV_W), cur),
        pl.BlockSpec((1, ATT_HEAD_DIM), lambda i: (0, 0)),
        pl.BlockSpec((ATT_KV_HEADS, ATT_GROUP * blk, 2 * blk), lambda i: (0, 0, 0)),
        pl.BlockSpec((ATT_KV_HEADS, ATT_GROUP * blk, 1), lambda i: (0, 0, 0)),
    ]


def attn_fwd(q_raw, k, v, q_norm, bias, sink_col, nm):
    t_dim = q_raw.shape[0]
    blk, hd = ATT_WINDOW, ATT_HEAD_DIM
    nb = t_dim // blk

    def body(q_ref, kp_ref, kc_ref, vp_ref, vc_ref, qn_ref, bias_ref, sink_ref, o_ref):
        for kv in range(ATT_KV_HEADS):
            kk, vv, xh, rinv, q8, prob, p_sink = _attn_scores(q_ref, kp_ref, kc_ref, vp_ref, vc_ref, qn_ref,
                                                              bias_ref, sink_ref, kv, False)
            o8 = _dot(prob, vv)
            for r in range(ATT_GROUP):
                hh = kv * ATT_GROUP + r
                o_ref[:, hh * hd:(hh + 1) * hd] = o8[r * blk:(r + 1) * blk].astype(o_ref.dtype)

    out, = pcall(body, name=nm, grid=(nb,), in_specs=_attn_specs(nb),
                 out_specs=[pl.BlockSpec((blk, D_MODEL), lambda i: (i, 0))],
                 out_shape=[jax.ShapeDtypeStruct((t_dim, D_MODEL), BF16)],
                 args=[q_raw, k, k, v, v, q_norm, bias, sink_col])
    return out


def attn_bwd(do, q_raw, k, v, q_norm, bias, sink_col, nm, hook=None):
    t_dim = q_raw.shape[0]
    blk, hd = ATT_WINDOW, ATT_HEAD_DIM
    nb = t_dim // blk
    scale = hd ** -0.5

    def body(do_ref, q_ref, kp_ref, kc_ref, vp_ref, vc_ref, qn_ref, bias_ref, sink_ref,
             dq_ref, dkc_ref, dkp_ref, dvc_ref, dvp_ref, dbias_ref, dsink_ref, dqn_ref):
        @pl.when(pl.program_id(0) == 0)
        def _():
            dbias_ref[...] = jnp.zeros_like(dbias_ref)
            dsink_ref[...] = jnp.zeros_like(dsink_ref)
            dqn_ref[...] = jnp.zeros_like(dqn_ref)

        gq = qn_ref[...]
        for kv in range(ATT_KV_HEADS):
            kk, vv, xh, rinv, q8, prob, p_sink = _attn_scores(q_ref, kp_ref, kc_ref, vp_ref, vc_ref, qn_ref,
                                                              bias_ref, sink_ref, kv, True)
            do8 = jnp.concatenate([do_ref[:, (kv * ATT_GROUP + r) * hd:(kv * ATT_GROUP + r + 1) * hd]
                                   for r in range(ATT_GROUP)], axis=0)
            dp = _dot(do8, vv, NT)
            delta = jnp.sum(prob * dp, axis=-1, keepdims=True)
            ds = prob * (dp - delta)
            dsink_ref[kv] += -p_sink * delta
            dbias_ref[kv] += ds
            ds_s = ds * scale
            dq8 = _dot(ds_s, kk)
            dkk = _dot(ds_s, q8, TN)
            dvv = _dot(prob, do8, TN)
            dqn_ref[...] += jnp.sum(dq8 * xh, axis=0, keepdims=True)
            dxh = dq8 * gq
            dq_raw8 = rinv * (dxh - xh * jnp.mean(dxh * xh, axis=-1, keepdims=True))
            for r in range(ATT_GROUP):
                hh = kv * ATT_GROUP + r
                dq_ref[:, hh * hd:(hh + 1) * hd] = dq_raw8[r * blk:(r + 1) * blk].astype(dq_ref.dtype)
            sl = slice(kv * hd, (kv + 1) * hd)
            dkp_ref[:, sl] = dkk[:blk]
            dkc_ref[:, sl] = dkk[blk:]
            dvp_ref[:, sl] = dvv[:blk]
            dvc_ref[:, sl] = dvv[blk:]

    cur = lambda i: (i, 0)
    row_spec = pl.BlockSpec((blk, KV_W), cur)
    out_specs = [
        pl.BlockSpec((blk, D_MODEL), cur), row_spec, row_spec, row_spec, row_spec,
        pl.BlockSpec((ATT_KV_HEADS, ATT_GROUP * blk, 2 * blk), lambda i: (0, 0, 0)),
        pl.BlockSpec((ATT_KV_HEADS, ATT_GROUP * blk, 1), lambda i: (0, 0, 0)),
        pl.BlockSpec((1, hd), lambda i: (0, 0)),
    ]
    kvs = jax.ShapeDtypeStruct((t_dim, KV_W), F32)
    out_shape = [
        jax.ShapeDtypeStruct((t_dim, D_MODEL), BF16), kvs, kvs, kvs, kvs,
        jax.ShapeDtypeStruct((ATT_KV_HEADS, ATT_GROUP * blk, 2 * blk), F32),
        jax.ShapeDtypeStruct((ATT_KV_HEADS, ATT_GROUP * blk, 1), F32),
        jax.ShapeDtypeStruct((1, hd), F32),
    ]
    return pcall(body, name=nm, grid=(nb,), in_specs=[pl.BlockSpec((blk, D_MODEL), cur)] + _attn_specs(nb),
                 out_specs=out_specs, out_shape=out_shape,
                 args=[do, q_raw, k, k, v, v, q_norm, bias, sink_col], hook=hook)


def _t5_bucket_np():
    blk = ATT_WINDOW
    qi = np.arange(blk)[:, None] + blk
    kj = np.arange(2 * blk)[None, :]
    dist = qi - kj
    n = np.maximum(dist, 0)
    max_exact = REL_BUCKETS // 2
    nf = np.maximum(n, 1).astype(np.float32)
    large = max_exact + (np.log(nf / max_exact) / math.log(ATT_WINDOW / max_exact)
                         * (REL_BUCKETS - max_exact)).astype(np.int32)
    large = np.minimum(large, REL_BUCKETS - 1)
    bucket = np.where(n < max_exact, n, large)
    in_window = (dist >= 0) & (dist < ATT_WINDOW)
    return bucket, in_window


def attn_block_fwd(h, k, v, p, nm):
    def q_proj(x, gg, w_q):
        uu = _rms_fwd(x, gg).astype(BF16)
        return uu, _dot(uu, w_q)

    u, q_raw = rowmap(q_proj, [h], [p["attn_norm"], p["w_q"]], [(D_MODEL, BF16), (D_MODEL, F32)], tm=RESIDENT_TM,
                      name=nm + "_q")
    o = attn_fwd(q_raw, k, v, p["q_norm"], p["bias"], p["sink_col"], nm + "_core")
    out = mm(o, p["w_o"], res=h, name=nm + "_o")
    return out, (u, q_raw, o)


def attn_block_bwd(h, k, v, p, saved, dout, dout_bf, nm, io):
    u, q_raw, o = saved
    g = {}
    io.put("w_o", mm(o, dout_bf, ta=True, out_dtype=BF16, name=nm + "_dwo", hook=io.hook(nm + "_dwo")))
    do = mm(dout_bf, p["w_o"], tb=True, name=nm + "_do")
    dq_raw, dkc, dkp, dvc, dvp, g["bias"], g["sink_col"], g["q_norm"] = attn_bwd(
        do, q_raw, k, v, p["q_norm"], p["bias"], p["sink_col"], nm + "_dcore", hook=io.hook(nm + "_dcore"))
    io.put("w_q", mm(u, dq_raw, ta=True, out_dtype=BF16, name=nm + "_dwq"))
    dh, dh_bf, g["attn_norm"] = du_norm_bwd([(dq_raw, p["w_q"], True)], h, p["attn_norm"], dout, nm + "_du")
    return dh, dh_bf, g, (dkc, dkp, dvc, dvp)


FFN_TAGS = ["00", "01", "10", "11"]


def local_step(x, target, small, io):
    bucket, in_window = _t5_bucket_np()
    blk = ATT_WINDOW
    w = small

    fnorm = {tag: w["ffn_norm"][int(tag[0]), int(tag[1])][None, :] for tag in FFN_TAGS}
    a_neg = -jnp.exp(w["ssm_a_log"][0])

    def mamba_p():
        w_int = io.w("w_int")
        return dict(ssm_norm=w["ssm_norm"], w_zt=w_int[:SSM_D_INNER],
                    w_xbct=w_int[SSM_D_INNER:SSM_D_INNER + SSM_CONV_DIM], w_dtt=w_int[SSM_D_INNER + SSM_CONV_DIM:],
                    conv_w=w["ssm_conv_w"][0], conv_b=w["ssm_conv_b"], dt_bias=w["ssm_dt_bias"],
                    a_heads=a_neg[None, :], a_row=a_neg.reshape(SSM_GROUPS, 1, HPG),
                    a_col=a_neg.reshape(SSM_GROUPS, HPG, 1),
                    dvec=jnp.repeat(w["ssm_d"][0], SSM_HEAD_DIM)[None, :],
                    gate_norm=w["ssm_gate_norm"], w_out=io.w("w_out"))

    rb = w["rel_bias"]
    onehot3 = (np.arange(REL_BUCKETS)[:, None, None] == bucket[None]).astype(np.float32)
    bias = jnp.einsum("bh,bqk->hqk", rb, onehot3, precision=lax.Precision.HIGHEST)
    bias = jnp.where(in_window[None], bias, NEG)
    bias = bias.reshape(ATT_KV_HEADS, ATT_GROUP * blk, 2 * blk)
    sink_col = jnp.repeat(w["sinks"][0], blk).reshape(ATT_KV_HEADS, ATT_GROUP * blk, 1)

    def attn_p():
        return dict(attn_norm=w["attn_norm"], w_q=io.w("w_q"), q_norm=w["q_norm"], bias=bias, sink_col=sink_col,
                    w_o=io.w("w_o"))

    def kv_p():
        return dict(kv_norm=w["kv_norm"][None, :], w_kv=io.w("w_kv"), k_norm=w["k_norm"][None, :])

    h0 = x
    h0a, s_f00 = ffn_fwd(h0, fnorm["00"], "00", io)
    mp = mamba_p()
    h0b, s_m = mamba_fwd(h0a, mp, "ssm", io)
    h1, s_f01 = ffn_fwd(h0b, fnorm["01"], "01", io)
    kp = kv_p()
    k, v, s_kv = kv_fwd(h1, kp, "kv")
    h1a, s_f10 = ffn_fwd(h1, fnorm["10"], "10", io)
    ap = attn_p()
    h1b, s_a = attn_block_fwd(h1a, k, v, ap, "att")
    (dh, dh_bf, sq), s_f11 = ffn_fwd(h1b, fnorm["11"], "11", io, target=target)
    loss_part = jnp.sum(sq) * (0.5 / D_MODEL)

    fg = {}

    def ffn_back(tag, h_in, saved, dh, dh_bf):
        dh, dh_bf, dg = ffn_bwd(h_in, fnorm[tag], tag, saved, dh, dh_bf, io)
        fg[tag] = dg[0]
        return dh, dh_bf

    dh, dh_bf = ffn_back("11", h1b, s_f11, dh, dh_bf)
    dh, dh_bf, ga, dkv = attn_block_bwd(h1a, k, v, ap, s_a, dh, dh_bf, "att", io)
    dh, dh_bf = ffn_back("10", h1, s_f10, dh, dh_bf)
    dh, dh_bf, gk = kv_bwd(h1, kp, s_kv, *dkv, dh, "kv", io)
    dh, dh_bf = ffn_back("01", h0b, s_f01, dh, dh_bf)
    dh, dh_bf, gm = mamba_bwd(h0a, mp, s_m, dh, dh_bf, "ssm", io)
    dh, dh_bf = ffn_back("00", h0, s_f00, dh, dh_bf)
    grad_x = dh

    grads = {}
    grads["ffn_norm"] = jnp.stack([fg[tag] for tag in FFN_TAGS]).reshape(2, 2, D_MODEL)
    grads["ssm_norm"] = gm["ssm_norm"]
    grads["ssm_conv_w"] = gm["conv_w"][None]
    grads["ssm_conv_b"] = gm["conv_b"]
    grads["ssm_dt_bias"] = gm["dt_bias"]
    grads["ssm_a_log"] = gm["a"] * a_neg[None, :]
    grads["ssm_d"] = jnp.sum(gm["dvec"].reshape(SSM_HEADS, SSM_HEAD_DIM), axis=1)[None, :]
    grads["ssm_gate_norm"] = gm["gate_norm"]
    grads["kv_norm"] = gk["kv_norm"][0]
    grads["k_norm"] = gk["k_norm"][0]
    grads["attn_norm"] = ga["attn_norm"]
    grads["q_norm"] = ga["q_norm"]
    grads["sinks"] = jnp.sum(ga["sink_col"].reshape(ATT_HEADS, blk), axis=1)[None, :]
    onehot = (np.arange(REL_BUCKETS)[:, None] == bucket.reshape(1, -1)).astype(np.float32)
    dbias2d = ga["bias"].reshape(ATT_HEADS, blk * 2 * blk)
    grads["rel_bias"] = mm(jnp.asarray(onehot, BF16), dbias2d, tb=True, name="drelbias")
    return loss_part, grad_x, grads


def _adamw(g, w, m, v):
    m = ADAM_B1 * m + (1.0 - ADAM_B1) * g
    v = ADAM_B2 * v + (1.0 - ADAM_B2) * (g * g)
    m_hat = m / (1.0 - ADAM_B1 ** ADAM_STEP)
    v_hat = v / (1.0 - ADAM_B2 ** ADAM_STEP)
    delta = -ADAM_LR * (m_hat / (jnp.sqrt(v_hat) + ADAM_EPS) + ADAM_WD * w)
    return delta, m, v


def _slot_sum(r):
    g = r[0].astype(F32)
    for d in range(1, r.shape[0]):
        g = g + r[d].astype(F32)
    return g


def adamw_rows(recvs, w, m, v, name):
    n_l, rows, width = w.shape
    n_slots = recvs[0].shape[0]
    tr = 32
    assert rows % tr == 0, rows
    nt = rows // tr

    def body(*refs):
        r_refs = refs[:n_l]
        w_ref, m_ref, v_ref, g_o, d_o, m_o, v_o = refs[n_l:]
        li = pl.program_id(0)
        for k in range(n_l):
            @pl.when(li == k)
            def _(k=k):
                g = _slot_sum(r_refs[k])
                delta, m2, v2 = _adamw(g, w_ref[0], m_ref[0], v_ref[0])
                g_o[0] = g
                d_o[0] = delta
                m_o[0] = m2
                v_o[0] = v2

    def r_spec(k):
        return pl.BlockSpec((n_slots, tr, width),
                            lambda li, j: (0, jnp.where(li == k, j, jnp.where(li > k, nt - 1, 0)), 0))

    w_spec = pl.BlockSpec((1, tr, width), lambda li, j: (li, j, 0))
    shp = jax.ShapeDtypeStruct(w.shape, F32)
    return pcall(body, name=name, grid=(n_l, nt), in_specs=[r_spec(k) for k in range(n_l)] + [w_spec] * 3,
                 out_specs=[w_spec] * 4, out_shape=[shp] * 4, args=list(recvs) + [w, m, v])


def adamw_cols(recvs, w, m, v, name):
    n_l, rows, n = w.shape
    n_slots = recvs[0].shape[0]
    tr = 256
    nt = rows // tr

    def body(*refs):
        r_refs = refs[:n_l]
        w_ref, m_ref, v_ref, g_o, d_o, m_o, v_o = refs[n_l:]
        li = pl.program_id(0)
        for k in range(n_l):
            @pl.when(li == k)
            def _(k=k):
                g = _slot_sum(r_refs[k]).T
                delta, m2, v2 = _adamw(g, w_ref[0], m_ref[0], v_ref[0])
                g_o[0] = g
                d_o[0] = delta
                m_o[0] = m2
                v_o[0] = v2

    def r_spec(k):
        return pl.BlockSpec((n_slots, n, tr),
                            lambda li, j: (0, 0, jnp.where(li == k, j, jnp.where(li > k, nt - 1, 0))))

    w_spec = pl.BlockSpec((1, tr, n), lambda li, j: (li, j, 0))
    shp = jax.ShapeDtypeStruct(w.shape, F32)
    return pcall(body, name=name, grid=(n_l, nt), in_specs=[r_spec(k) for k in range(n_l)] + [w_spec] * 3,
                 out_specs=[w_spec] * 4, out_shape=[shp] * 4, args=list(recvs) + [w, m, v])


WEIGHT_NAMES = ["ffn_norm", "ffn_w1", "ffn_w3", "ffn_w2", "ssm_norm", "ssm_w_in", "ssm_conv_w", "ssm_conv_b",
                "ssm_dt_bias", "ssm_a_log", "ssm_d", "ssm_gate_norm", "ssm_w_out", "kv_norm", "w_kv", "k_norm",
                "attn_norm", "w_q", "q_norm", "sinks", "w_o", "rel_bias"]

SMALL = [
    ("ffn_norm", (2, 2, 1024), 2), ("ssm_norm", (1, 1024), 1), ("ssm_conv_w", (1, 4, 3072), 2),
    ("ssm_conv_b", (1, 3072), 1), ("ssm_gate_norm", (1, 2048), 1),
    ("ssm_dt_bias", (1, 32), None), ("ssm_a_log", (1, 32), None), ("ssm_d", (1, 32), None),
    ("kv_norm", (1024,), None), ("k_norm", (64,), None), ("attn_norm", (1, 1024), None),
    ("q_norm", (1, 64), None), ("sinks", (1, 16), None), ("rel_bias", (32, 16), None),
]
SMALL_W = 1024
SMALL_FULL_ROWS = 32
SMALL_LOCAL_ROWS = 48

MAT_GROUPS = {
    "f00_up": ["w1t_00", "w3t_00"], "f00_down": ["w2_00"], "f01": ["w1t_01", "w3t_01", "w2_01"],
    "f10": ["w1t_10", "w3t_10", "w2_10"], "f11": ["w1t_11", "w3t_11", "w2_11"],
    "ssm": ["w_int", "w_out"], "att": ["w_q", "w_o", "w_kv"],
    "f00_early": ["w2_00", "w1t_00"], "f00_late": ["w3t_00"],
}
FIRST_GATHER = "f00_up"
GATHER_PLAN = {"f00_upgate": ["f00_down", "ssm"], "ssm_in": ["f01"], "ssm_ssd": ["att", "f10"],
               "f01_upgate": ["f11"]}
SCATTER_A_PLAN = {"att_dwo": "f11", "kv_dknorm": "f10", "kv_du": "att", "f01_du": "f01", "ssm_du": "ssm",
                  "f00_dw3": "f00_early", "f00_du": "f00_late"}
SCATTER_B_PLAN = {"att_dcore": "f11", "f01_dw2": "att", "f01_dgate": "f10", "ssm_dssd": "f01", "f00_dgate": "ssm",
                  "f00_du": "f00_early"}
LAST_SCATTER = "f00_late"
SLOT_MAJOR = ("w_int",)


def _shard_shape(s, a):
    return s[:a] + (s[a] // N_DEV,) + s[a + 1:]


def _unshard_view(stack, shard_shape, axis):
    moved = jnp.moveaxis(stack, 0, axis)
    return moved.reshape(shard_shape[:axis] + (N_DEV * shard_shape[axis],) + shard_shape[axis + 1:])


def _small_local(arrs):
    flat = jnp.concatenate([arrs[n].reshape(-1) for n, _, _ in SMALL])
    return jnp.pad(flat, (0, SMALL_LOCAL_ROWS * LANES - flat.shape[0])).reshape(SMALL_LOCAL_ROWS, LANES)


def chip_partial(g4, ra, name):
    _, _, n, width = g4.shape

    def body(core_ref, g_ref, r_ref, o_ref):
        o_ref[0] = (g_ref[0, 0].astype(F32) + r_ref[0, 0].astype(F32)).astype(o_ref.dtype)

    grid_spec = pltpu.PrefetchScalarGridSpec(
        num_scalar_prefetch=1, grid=(N_CHIPS,),
        in_specs=[pl.BlockSpec((1, 1, n, width), lambda q, core: (q, core[0], 0, 0)),
                  pl.BlockSpec((1, 1, n, width), lambda q, core: (q, 0, 0, 0))],
        out_specs=pl.BlockSpec((1, n, width), lambda q, core: (q, 0, 0)))
    core = jnp.reshape(lax.axis_index("c"), (1,)).astype(jnp.int32)
    return pl.pallas_call(
        body, name=name, grid_spec=grid_spec, out_shape=jax.ShapeDtypeStruct((N_CHIPS, n, width), g4.dtype),
        compiler_params=pltpu.CompilerParams(dimension_semantics=("arbitrary",), vmem_limit_bytes=VMEM_LIMIT_BYTES),
    )(core, g4, ra)


class StepIO:
    def __init__(self, pieces):
        self.pieces = pieces
        self.full = {}
        self.grad = {}
        self.from_sibling = {}
        self.recv = {}

    def w(self, name):
        return self.full[name]

    def put(self, name, g):
        self.grad[name] = g

    def _by_chip_core(self, name):
        g = self.grad[name]
        return g.reshape((N_CHIPS, 2, g.shape[0] // N_DEV) + g.shape[1:])

    def gather_items(self, groups):
        names = [n for grp in groups for n in MAT_GROUPS[grp]]
        items = [("g2", self.pieces[n], None if n in SLOT_MAJOR else 0) for n in names]

        def sink(outs):
            for n, o in zip(names, outs):
                self.full[n] = o.reshape((-1,) + o.shape[2:]) if n in SLOT_MAJOR else o

        return items, sink

    def scatter_a_items(self, group):
        names = MAT_GROUPS[group]
        items = [("sa", self._by_chip_core(n), None) for n in names]

        def sink(outs):
            for n, o in zip(names, outs):
                self.from_sibling[n] = o

        return items, sink

    def scatter_b_items(self, group):
        names = MAT_GROUPS[group]
        items = [("sb", chip_partial(self._by_chip_core(n), self.from_sibling[n], "partial_" + n), None)
                 for n in names]

        def sink(outs):
            for n, o in zip(names, outs):
                self.recv[n] = o

        return items, sink

    def hook(self, site):
        parts = []
        if site in GATHER_PLAN:
            parts.append(self.gather_items(GATHER_PLAN[site]))
        if site in SCATTER_A_PLAN:
            parts.append(self.scatter_a_items(SCATTER_A_PLAN[site]))
        if site in SCATTER_B_PLAN:
            parts.append(self.scatter_b_items(SCATTER_B_PLAN[site]))
        if not parts:
            return None
        return combine_hooks(parts)


def combine_hooks(parts):
    items = [it for its, _ in parts for it in its]

    def sink(outs):
        p = 0
        for its, snk in parts:
            snk(outs[p:p + len(its)])
            p += len(its)

    return Comm(items), sink


def step(x, target, wts, ms, vs):
    me = _my_index()

    pieces = {}
    for li in range(2):
        for hi in range(2):
            tag = "%d%d" % (li, hi)
            pieces["w1t_" + tag] = wts["ffn_w1"][li, hi].T.astype(BF16)
            pieces["w3t_" + tag] = wts["ffn_w3"][li, hi].T.astype(BF16)
            pieces["w2_" + tag] = wts["ffn_w2"][li, hi].astype(BF16)
    pieces["w_int"] = wts["ssm_w_in"][0].T.astype(BF16)
    pieces["w_out"] = wts["ssm_w_out"][0].astype(BF16)
    pieces["w_kv"] = wts["w_kv"].astype(BF16)
    pieces["w_q"] = wts["w_q"][0].astype(BF16)
    pieces["w_o"] = wts["w_o"][0].astype(BF16)
    io = StepIO(pieces)

    small_sharded = [(n, s, a) for n, s, a in SMALL if a is not None]
    loc = jnp.concatenate([wts[n].reshape(-1) for n, _, _ in small_sharded])
    loc_rows = -(-loc.shape[0] // (8 * LANES)) * 8
    loc = jnp.pad(loc, (0, loc_rows * LANES - loc.shape[0])).reshape(loc_rows, LANES)
    got_small = []
    comm, sink = combine_hooks([io.gather_items([FIRST_GATHER]), ([("g", loc, None)], got_small.extend)])
    sink(comm_only(comm, "gather_first"))
    gath_small = got_small[0].reshape(N_DEV, -1)
    small = {}
    off = 0
    for n, s, a in small_sharded:
        shard = _shard_shape(s, a)
        cnt = int(np.prod(shard))
        small[n] = _unshard_view(gath_small[:, off:off + cnt].reshape((N_DEV,) + shard), shard, a)
        off += cnt
    for n, s, a in SMALL:
        if a is None:
            small[n] = wts[n]

    loss_part, grad_x, g_small_local = local_step(x[0], target[0], small, io)
    loss = lax.psum(loss_part, ("x", "y", "c"))

    small_flat = jnp.concatenate([g_small_local[n].reshape(-1) for n, _, _ in SMALL])
    small_buf = jnp.pad(small_flat, (0, SMALL_FULL_ROWS * SMALL_W - small_flat.shape[0]))
    small_buf = small_buf.reshape(SMALL_FULL_ROWS, SMALL_W)
    got_small = []
    comm, sink = combine_hooks([io.scatter_b_items(LAST_SCATTER), ([("g", small_buf, None)], got_small.extend)])
    sink(comm_only(comm, "exchange_last"))
    small_all = got_small[0]

    def sum_body(r_ref, o_ref):
        o_ref[...] = _slot_sum(r_ref)

    vmem = pl.BlockSpec(memory_space=pltpu.VMEM)
    small_sum, = pcall(sum_body, name="sum_small", grid=(), in_specs=[vmem], out_specs=[vmem],
                       out_shape=[jax.ShapeDtypeStruct((SMALL_FULL_ROWS, SMALL_W), F32)], args=[small_all])
    small_sum = small_sum.reshape(-1)
    g_small = {}
    off = 0
    for n, s, a in SMALL:
        cnt = int(np.prod(s))
        gfull = small_sum[off:off + cnt].reshape(s)
        off += cnt
        if a is None:
            g_small[n] = gfull
        else:
            width = s[a] // N_DEV
            g_small[n] = lax.dynamic_slice_in_dim(gfull, me * width, width, axis=a)

    out = {}

    def emit(name, res, shape):
        for kind, arr in zip(("grad", "delta", "new_m", "new_v"), res):
            out[kind + "_" + name] = arr.reshape(shape)

    for name, key in (("ffn_w1", "w1t_"), ("ffn_w3", "w3t_")):
        shp = wts[name].shape
        view = lambda t: t.reshape((4,) + shp[2:])
        res = adamw_cols([io.recv[key + tag] for tag in FFN_TAGS], view(wts[name]), view(ms[name]), view(vs[name]),
                         "adamw_" + name)
        emit(name, res, shp)
    shp = wts["ffn_w2"].shape
    view = lambda t: t.reshape((4,) + shp[2:])
    res = adamw_rows([io.recv["w2_" + tag] for tag in FFN_TAGS], view(wts["ffn_w2"]), view(ms["ffn_w2"]),
                     view(vs["ffn_w2"]), "adamw_ffn_w2")
    emit("ffn_w2", res, shp)
    res = adamw_cols([io.recv["w_int"]], wts["ssm_w_in"], ms["ssm_w_in"], vs["ssm_w_in"], "adamw_ssm_w_in")
    emit("ssm_w_in", res, wts["ssm_w_in"].shape)
    for name, key in (("ssm_w_out", "w_out"), ("w_kv", "w_kv"), ("w_q", "w_q"), ("w_o", "w_o")):
        shp = wts[name].shape
        view = lambda t: t.reshape((1,) + shp[-2:])
        res = adamw_rows([io.recv[key]], view(wts[name]), view(ms[name]), view(vs[name]), "adamw_" + name)
        emit(name, res, shp)

    res_s = rowmap(lambda gg, ww, mm_, vv: _adamw(gg, ww, mm_, vv),
                   [_small_local(g_small), _small_local(wts), _small_local(ms), _small_local(vs)], [],
                   [(LANES, F32)] * 3, tm=SMALL_LOCAL_ROWS, name="adamw_small")
    flat_s = [r.reshape(-1) for r in res_s]
    off = 0
    for n, s, a in SMALL:
        shard = s if a is None else _shard_shape(s, a)
        cnt = int(np.prod(shard))
        out["grad_" + n] = g_small[n]
        for kind, arr in zip(("delta", "new_m", "new_v"), flat_s):
            out[kind + "_" + n] = arr[off:off + cnt].reshape(shard)
        off += cnt
    out["loss"] = loss
    out["grad_x"] = grad_x[None]
    return out


def kernel(x, ffn_norm, ffn_w1, ffn_w3, ffn_w2, ssm_norm, ssm_w_in, ssm_conv_w, ssm_conv_b, ssm_dt_bias, ssm_a_log, ssm_d, ssm_gate_norm, ssm_w_out, kv_norm, w_kv, k_norm, attn_norm, w_q, q_norm, sinks, w_o, rel_bias, loss_target, m_ffn_norm, m_ffn_w1, m_ffn_w3, m_ffn_w2, m_ssm_norm, m_ssm_w_in, m_ssm_conv_w, m_ssm_conv_b, m_ssm_dt_bias, m_ssm_a_log, m_ssm_d, m_ssm_gate_norm, m_ssm_w_out, m_kv_norm, m_w_kv, m_k_norm, m_attn_norm, m_w_q, m_q_norm, m_sinks, m_w_o, m_rel_bias, v_ffn_norm, v_ffn_w1, v_ffn_w3, v_ffn_w2, v_ssm_norm, v_ssm_w_in, v_ssm_conv_w, v_ssm_conv_b, v_ssm_dt_bias, v_ssm_a_log, v_ssm_d, v_ssm_gate_norm, v_ssm_w_out, v_kv_norm, v_w_kv, v_k_norm, v_attn_norm, v_w_q, v_q_norm, v_sinks, v_w_o, v_rel_bias):
    args = locals()
    wts = {n: args[n] for n in WEIGHT_NAMES}
    ms = {n: args["m_" + n] for n in WEIGHT_NAMES}
    vs = {n: args["v_" + n] for n in WEIGHT_NAMES}
    out = step(x, loss_target, wts, ms, vs)
    result = [out["loss"], out["grad_x"]]
    for kind in ("grad", "delta", "new_m", "new_v"):
        result += [out[kind + "_" + n] for n in WEIGHT_NAMES]
    return tuple(result)
```

```python
import functools
import math
import operator

import numpy as np
import jax
import jax.numpy as jnp
from jax import lax
from jax.experimental import pallas as pl
from jax.experimental.pallas import tpu as pltpu

F32 = jnp.float32
BF16 = jnp.bfloat16

D_MODEL = 1024
D_FF = 2816
N_DEV = 8
SSM_D_INNER = 2048
SSM_HEAD_DIM = 64
SSM_HEADS = 32
SSM_GROUPS = 4
SSM_STATE = 128
SSM_CONV = 4
SSM_CHUNK = 256
SSM_CONV_DIM = SSM_D_INNER + 2 * SSM_GROUPS * SSM_STATE
SSM_IN_DIM = SSM_D_INNER + SSM_CONV_DIM + SSM_HEADS
ATT_HEAD_DIM = 64
ATT_HEADS = 16
ATT_KV_HEADS = 2
ATT_GROUP = 8
ATT_WINDOW = 128
REL_BUCKETS = 32
EPS = 1e-6
NEG = -1e30

ADAM_LR = 0.001
ADAM_B1 = 0.9
ADAM_B2 = 0.999
ADAM_EPS = 1e-08
ADAM_WD = 0.01
ADAM_STEP = 10

VMEM_LIMIT_BYTES = 52 * 1024 * 1024
LANES = 128
MESH_ID = pl.DeviceIdType.MESH
ANY_SPEC = pl.BlockSpec(memory_space=pl.ANY)

NT = (((1,), (1,)), ((), ()))
TN = (((0,), (0,)), ((), ()))
NN = (((1,), (0,)), ((), ()))


def _pick(dim, cands):
    for c in cands:
        if dim % c == 0:
            return c
    return dim


def _my_index():
    return 4 * lax.axis_index("x") + 2 * lax.axis_index("y") + lax.axis_index("c")


def _peer(k):
    x, y, c = lax.axis_index("x"), lax.axis_index("y"), lax.axis_index("c")
    px = 1 - x if (k >> 2) & 1 else x
    py = 1 - y if (k >> 1) & 1 else y
    pc = 1 - c if k & 1 else c
    return (px, py, pc), 4 * px + 2 * py + pc


def _piece(ref, axis, d, n):
    if axis is None:
        return ref.at[d]
    return ref.at[(slice(None),) * axis + (pl.ds(pl.multiple_of(d * n, 8), n),)]


SIBLING = 1
CHIP_PEERS = (4, 2, 6)
N_CHIPS = 4
SEMS_PER_ITEM = N_DEV - 1


def _my_chip():
    return 2 * lax.axis_index("x") + lax.axis_index("y")


class Comm:
    def __init__(self, items):
        self.items = list(items)

    def dst_shapes(self):
        out = []
        for kind, src, axis in self.items:
            s = tuple(src.shape)
            if kind == "g":
                shp = (N_DEV,) + s
            elif kind == "g2":
                shp = (N_DEV,) + s if axis is None else s[:axis] + (N_DEV * s[axis],) + s[axis + 1:]
            elif kind == "sa":
                shp = (s[0], 1) + s[2:]
            else:
                shp = s
            out.append(jax.ShapeDtypeStruct(shp, src.dtype))
        return out

    def scratch(self):
        n = len(self.items)
        return [pltpu.SemaphoreType.DMA((n * SEMS_PER_ITEM,)), pltpu.SemaphoreType.DMA((n * SEMS_PER_ITEM,)),
                pltpu.SemaphoreType.DMA((n,))]

    def _run(self, srcs, dsts, sems, starting):
        send_sems, recv_sems, local_sems = sems
        me = _my_index()
        core = lax.axis_index("c")
        chip = _my_chip()
        for i, (kind, src, axis) in enumerate(self.items):
            s_ref, d_ref = srcs[i], dsts[i]
            base = i * SEMS_PER_ITEM

            def rdma(src_ref, dst_ref, j, peer):
                return pltpu.make_async_remote_copy(
                    src_ref=src_ref, dst_ref=dst_ref, send_sem=send_sems.at[base + j], recv_sem=recv_sems.at[base + j],
                    device_id=peer, device_id_type=MESH_ID)

            if kind == "g":
                local = pltpu.make_async_copy(s_ref, d_ref.at[me], local_sems.at[i])
                outs = [rdma(s_ref, d_ref.at[me], k - 1, _peer(k)[0]) for k in range(1, N_DEV)]
                if starting:
                    local.start()
                    for cp in outs:
                        cp.start()
                else:
                    for k in range(1, N_DEV):
                        rdma(s_ref, d_ref.at[_peer(k)[1]], k - 1, _peer(k)[0]).wait_recv()
                    for cp in outs:
                        cp.wait_send()
                    local.wait()
            elif kind == "g2":
                n = None if axis is None else src.shape[axis]
                mine = _piece(d_ref, axis, me, n)
                sib = _peer(SIBLING)[0]
                local = pltpu.make_async_copy(s_ref, mine, local_sems.at[i])
                outs = [rdma(s_ref, mine, 0, sib)] + [rdma(s_ref, mine, 1 + j, _peer(k)[0])
                                                      for j, k in enumerate(CHIP_PEERS)]
                if starting:
                    local.start()
                    for cp in outs:
                        cp.start()
                else:
                    passed = []
                    for j, k in enumerate(CHIP_PEERS):
                        theirs = _piece(d_ref, axis, _peer(k)[1], n)
                        rdma(s_ref, theirs, 1 + j, _peer(k)[0]).wait_recv()
                        fwd = rdma(theirs, theirs, 4 + j, sib)
                        fwd.start()
                        passed.append(fwd)
                    rdma(s_ref, _piece(d_ref, axis, _peer(SIBLING)[1], n), 0, sib).wait_recv()
                    for j, k in enumerate(CHIP_PEERS):
                        rdma(s_ref, _piece(d_ref, axis, _peer(k ^ SIBLING)[1], n), 4 + j, sib).wait_recv()
                    for cp in outs + passed:
                        cp.wait_send()
                    local.wait()
            elif kind == "sa":
                cp = rdma(s_ref.at[(slice(None), pl.ds(1 - core, 1))], d_ref, 0, _peer(SIBLING)[0])
                if starting:
                    cp.start()
                else:
                    cp.wait_recv()
                    cp.wait_send()
            else:
                local = pltpu.make_async_copy(s_ref.at[chip], d_ref.at[chip], local_sems.at[i])
                outs = [rdma(s_ref.at[_peer(k)[1] >> 1], d_ref.at[chip], 1 + j, _peer(k)[0])
                        for j, k in enumerate(CHIP_PEERS)]
                if starting:
                    local.start()
                    for cp in outs:
                        cp.start()
                else:
                    for j, k in enumerate(CHIP_PEERS):
                        rdma(s_ref.at[chip], d_ref.at[_peer(k)[1] >> 1], 1 + j, _peer(k)[0]).wait_recv()
                    for cp in outs:
                        cp.wait_send()
                    local.wait()

    def start(self, srcs, dsts, sems):
        self._run(srcs, dsts, sems, True)

    def wait(self, srcs, dsts, sems):
        self._run(srcs, dsts, sems, False)


def pcall(body, *, name, grid, in_specs, out_specs, out_shape, args, scratch=(), hook=None):
    cparams = pltpu.CompilerParams(dimension_semantics=("arbitrary",) * len(grid), vmem_limit_bytes=VMEM_LIMIT_BYTES)
    if hook is None:
        outs = pl.pallas_call(body, name=name, grid=grid, in_specs=list(in_specs), out_specs=list(out_specs),
                              out_shape=list(out_shape), scratch_shapes=list(scratch), compiler_params=cparams)(*args)
        return list(outs)
    comm, sink = hook
    n_in, n_out, n_scr, n_it = len(args), len(out_shape), len(scratch), len(comm.items)
    dims = tuple(grid)

    def wrapped(*refs):
        p = 0
        ins = refs[p:p + n_in]
        p += n_in
        csrc = refs[p:p + n_it]
        p += n_it
        outs = refs[p:p + n_out]
        p += n_out
        cdst = refs[p:p + n_it]
        p += n_it
        scr = refs[p:p + n_scr]
        p += n_scr
        sems = refs[p:p + 3]
        if dims:
            ids = [pl.program_id(a) for a in range(len(dims))]
            first = functools.reduce(operator.and_, [i == 0 for i in ids])
            last = functools.reduce(operator.and_, [i == d - 1 for i, d in zip(ids, dims)])

            @pl.when(first)
            def _():
                comm.start(csrc, cdst, sems)

            body(*ins, *outs, *scr)

            @pl.when(last)
            def _():
                comm.wait(csrc, cdst, sems)
        else:
            comm.start(csrc, cdst, sems)
            body(*ins, *outs, *scr)
            comm.wait(csrc, cdst, sems)

    res = pl.pallas_call(
        wrapped, name=name, grid=grid,
        in_specs=list(in_specs) + [ANY_SPEC] * n_it, out_specs=list(out_specs) + [ANY_SPEC] * n_it,
        out_shape=list(out_shape) + comm.dst_shapes(), scratch_shapes=list(scratch) + comm.scratch(),
        compiler_params=cparams,
    )(*args, *[src for _, src, _ in comm.items])
    res = list(res)
    sink(res[n_out:])
    return res[:n_out]


def comm_only(comm, name):
    got = []
    pcall(lambda *refs: None, name=name, grid=(), in_specs=[], out_specs=[], out_shape=[], args=[],
          hook=(comm, got.extend))
    return got


def mm(a, b, *, ta=False, tb=False, out_dtype=F32, res=None, alpha=1.0, name, hook=None):
    if ta:
        k_dim, m_dim = a.shape
    else:
        m_dim, k_dim = a.shape
    if tb:
        n_dim, k2 = b.shape
    else:
        k2, n_dim = b.shape
    assert k_dim == k2, (a.shape, b.shape, ta, tb)
    tn = _pick(n_dim, (1024, 1408, 512, 256, 128))
    tm = _pick(m_dim, (1024, 1408, 512, 256, 128)) if tn <= 1024 else _pick(m_dim, (512, 256, 128))
    tk = _pick(k_dim, (1024, 512, 256, 128)) if ta else _pick(k_dim, (512, 1408, 256, 128))
    nk = k_dim // tk
    has_res = res is not None
    dn = (((0 if ta else 1,), (1 if tb else 0,)), ((), ()))

    def body(*refs):
        if has_res:
            a_ref, b_ref, r_ref, o_ref, acc_ref = refs
        else:
            a_ref, b_ref, o_ref, acc_ref = refs
        k = pl.program_id(2)

        @pl.when(k == 0)
        def _():
            acc_ref[...] = jnp.zeros_like(acc_ref)

        acc_ref[...] += lax.dot_general(a_ref[...].astype(BF16), b_ref[...].astype(BF16), dn,
                                        preferred_element_type=F32)

        @pl.when(k == nk - 1)
        def _():
            r = acc_ref[...]
            if alpha != 1.0:
                r = r * alpha
            if has_res:
                r = r_ref[...] + r
            o_ref[...] = r.astype(o_ref.dtype)

    a_spec = pl.BlockSpec((tk, tm), lambda i, j, k: (k, i)) if ta else pl.BlockSpec((tm, tk), lambda i, j, k: (i, k))
    b_spec = pl.BlockSpec((tn, tk), lambda i, j, k: (j, k)) if tb else pl.BlockSpec((tk, tn), lambda i, j, k: (k, j))
    o_spec = pl.BlockSpec((tm, tn), lambda i, j, k: (i, j))
    in_specs = [a_spec, b_spec] + ([o_spec] if has_res else [])
    args = [a, b] + ([res] if has_res else [])
    out, = pcall(body, name=name, grid=(m_dim // tm, n_dim // tn, nk), in_specs=in_specs, out_specs=[o_spec],
                 out_shape=[jax.ShapeDtypeStruct((m_dim, n_dim), out_dtype)], args=args,
                 scratch=[pltpu.VMEM((tm, tn), F32)], hook=hook)
    return out


def rowmap(fn, rows, consts=(), out_rows=(), out_accs=(), *, tm, name, hook=None):
    first = rows[0][0] if isinstance(rows[0], tuple) else rows[0]
    t_dim = first.shape[0]
    assert t_dim % tm == 0, (t_dim, tm)
    n_r, n_c, n_o = len(rows), len(consts), len(out_rows)

    def body(*refs):
        ins = [r[...] for r in refs[:n_r + n_c]]
        o_refs = refs[n_r + n_c:]
        outs = tuple(fn(*ins))
        for o_ref, val in zip(o_refs[:n_o], outs[:n_o]):
            o_ref[...] = val.astype(o_ref.dtype)
        if out_accs:
            @pl.when(pl.program_id(0) == 0)
            def _():
                for o_ref in o_refs[n_o:]:
                    o_ref[...] = jnp.zeros_like(o_ref)

            for o_ref, val in zip(o_refs[n_o:], outs[n_o:]):
                o_ref[...] += val

    in_specs, args = [], []
    for r in rows:
        if isinstance(r, tuple):
            args.append(r[0])
            in_specs.append(r[1])
        else:
            args.append(r)
            in_specs.append(pl.BlockSpec((tm, r.shape[1]), lambda i: (i, 0)))
    for c in consts:
        args.append(c)
        in_specs.append(pl.BlockSpec(c.shape, lambda i, nd=c.ndim: (0,) * nd))
    out_specs = [pl.BlockSpec((tm, w), lambda i: (i, 0)) for (w, _) in out_rows]
    out_specs += [pl.BlockSpec(s, lambda i, nd=len(s): (0,) * nd) for s in out_accs]
    out_shape = [jax.ShapeDtypeStruct((t_dim, w), dt) for (w, dt) in out_rows]
    out_shape += [jax.ShapeDtypeStruct(s, F32) for s in out_accs]
    return pcall(body, name=name, grid=(t_dim // tm,), in_specs=in_specs, out_specs=out_specs, out_shape=out_shape,
                 args=args, hook=hook)


def _rms_fwd(x, g):
    r = lax.rsqrt(jnp.mean(x * x, axis=-1, keepdims=True) + EPS)
    return x * r * g


def _rms_bwd(x, g, dy):
    r = lax.rsqrt(jnp.mean(x * x, axis=-1, keepdims=True) + EPS)
    xh = x * r
    dg = jnp.sum(dy * xh, axis=0, keepdims=True)
    dxh = dy * g
    dx = r * (dxh - xh * jnp.mean(dxh * xh, axis=-1, keepdims=True))
    return dx, dg


def _sigmoid(x):
    return 1.0 / (1.0 + jnp.exp(-x))


def _silu(x):
    return x * _sigmoid(x)


def _silu_grad(x):
    s = _sigmoid(x)
    return s * (1.0 + x * (1.0 - s))


def _split3(x):
    hi = x.astype(BF16)
    r1 = x - hi.astype(F32)
    mid = r1.astype(BF16)
    lo = (r1 - mid.astype(F32)).astype(BF16)
    return hi, mid, lo


def _dot(a, b, dn=NN):
    return lax.dot_general(a.astype(BF16), b.astype(BF16), dn, preferred_element_type=F32)


FFN_TN = 1408
RESIDENT_TM = 512


def ffn_upgate(h, g, w1t, w3t, nm, hook=None):
    t_dim = h.shape[0]
    tm = _pick(t_dim, (512, 256, 128))
    tn = FFN_TN

    n_j = D_FF // tn
    u_w = D_MODEL // n_j

    def body(h_ref, g_ref, w1_ref, w3_ref, u_ref, a_ref, b_ref, hm_ref):
        uu = _rms_fwd(h_ref[...], g_ref[...]).astype(BF16)
        for j in range(n_j):
            @pl.when(pl.program_id(0) == j)
            def _(j=j):
                u_ref[...] = uu[:, j * u_w:(j + 1) * u_w]

        a = lax.dot_general(uu, w1_ref[...], NT, preferred_element_type=F32)
        b = lax.dot_general(uu, w3_ref[...], NT, preferred_element_type=F32)
        a_ref[...] = a.astype(a_ref.dtype)
        b_ref[...] = b.astype(b_ref.dtype)
        hm_ref[...] = (_silu(a) * b).astype(hm_ref.dtype)

    row_spec = pl.BlockSpec((tm, D_MODEL), lambda j, i: (i, 0))
    w_spec = pl.BlockSpec((tn, D_MODEL), lambda j, i: (j, 0))
    o_spec = pl.BlockSpec((tm, tn), lambda j, i: (i, j))
    o_shape = jax.ShapeDtypeStruct((t_dim, D_FF), BF16)
    return pcall(body, name=nm, grid=(D_FF // tn, t_dim // tm),
                 in_specs=[row_spec, pl.BlockSpec((1, D_MODEL), lambda j, i: (0, 0)), w_spec, w_spec],
                 out_specs=[pl.BlockSpec((tm, u_w), lambda j, i: (i, j))] + [o_spec] * 3,
                 out_shape=[jax.ShapeDtypeStruct((t_dim, D_MODEL), BF16)] + [o_shape] * 3,
                 args=[h, g, w1t, w3t], hook=hook)


def ffn_dgate(dout_bf, w2, a, b, nm, hook=None):
    t_dim = dout_bf.shape[0]
    tm = _pick(t_dim, (512, 256, 128))
    tn = FFN_TN

    def body(d_ref, w2_ref, a_ref, b_ref, da_ref, db_ref):
        dhm = 0.5 * lax.dot_general(d_ref[...], w2_ref[...], NT, preferred_element_type=F32)
        av = a_ref[...].astype(F32)
        bv = b_ref[...].astype(F32)
        sg = _sigmoid(av)
        da_ref[...] = (dhm * bv * (sg * (1.0 + av * (1.0 - sg)))).astype(da_ref.dtype)
        db_ref[...] = (dhm * (av * sg)).astype(db_ref.dtype)

    t_spec = pl.BlockSpec((tm, tn), lambda j, i: (i, j))
    o_shape = jax.ShapeDtypeStruct((t_dim, D_FF), BF16)
    return pcall(body, name=nm, grid=(D_FF // tn, t_dim // tm),
                 in_specs=[pl.BlockSpec((tm, D_MODEL), lambda j, i: (i, 0)),
                           pl.BlockSpec((tn, D_MODEL), lambda j, i: (j, 0)), t_spec, t_spec],
                 out_specs=[t_spec] * 2, out_shape=[o_shape] * 2, args=[dout_bf, w2, a, b], hook=hook)


def ffn_fwd(h, g, tag, io, target=None):
    nm = "f" + tag
    u, a, b, hm = ffn_upgate(h, g, io.w("w1t_" + tag), io.w("w3t_" + tag), nm + "_upgate",
                             hook=io.hook(nm + "_upgate"))
    if target is None:
        return mm(hm, io.w("w2_" + tag), res=h, alpha=0.5, name=nm + "_down"), (u, a, b, hm)

    def down_loss(hmv, hv, t, w2):
        e = hv + 0.5 * _dot(hmv, w2) - t
        d = e * (1.0 / D_MODEL)
        return d, d, jnp.sum(e * e, axis=0, keepdims=True)

    res = rowmap(down_loss, [hm, h, target], [io.w("w2_" + tag)], [(D_MODEL, F32), (D_MODEL, BF16)],
                 [(1, D_MODEL)], tm=RESIDENT_TM, name=nm + "_down_loss")
    return res, (u, a, b, hm)


def du_norm_bwd(pairs, h, g, dout, nm, hook=None):
    t_dim = h.shape[0]
    tm = RESIDENT_TM
    n_p = len(pairs)

    def body(*refs):
        h_ref, d_ref, g_ref = refs[2 * n_p:2 * n_p + 3]
        dh_ref, dhb_ref, dg_ref = refs[2 * n_p + 3:]
        du = None
        for p, (_, _, tb) in enumerate(pairs):
            t = lax.dot_general(refs[2 * p][...].astype(BF16), refs[2 * p + 1][...].astype(BF16), NT if tb else NN,
                                preferred_element_type=F32)
            du = t if du is None else du + t
        dx, dg = _rms_bwd(h_ref[...], g_ref[...], du)
        dh = d_ref[...] + dx
        dh_ref[...] = dh
        dhb_ref[...] = dh.astype(dhb_ref.dtype)

        @pl.when(pl.program_id(0) == 0)
        def _():
            dg_ref[...] = jnp.zeros_like(dg_ref)

        dg_ref[...] += dg

    in_specs, args = [], []
    for a, b, _ in pairs:
        in_specs += [pl.BlockSpec((tm, a.shape[1]), lambda i: (i, 0)), pl.BlockSpec(b.shape, lambda i: (0, 0))]
        args += [a, b]
    row_spec = pl.BlockSpec((tm, D_MODEL), lambda i: (i, 0))
    vec_spec = pl.BlockSpec((1, D_MODEL), lambda i: (0, 0))
    return pcall(body, name=nm, grid=(t_dim // tm,), in_specs=in_specs + [row_spec, row_spec, vec_spec],
                 out_specs=[row_spec, row_spec, vec_spec],
                 out_shape=[jax.ShapeDtypeStruct((t_dim, D_MODEL), F32), jax.ShapeDtypeStruct((t_dim, D_MODEL), BF16),
                            jax.ShapeDtypeStruct((1, D_MODEL), F32)],
                 args=args + [h, dout, g], hook=hook)


def ffn_bwd(h, g, tag, saved, dout, dout_bf, io):
    nm = "f" + tag
    w1t, w3t, w2 = io.w("w1t_" + tag), io.w("w3t_" + tag), io.w("w2_" + tag)
    u, a, b, hm = saved
    io.put("w2_" + tag, mm(hm, dout_bf, ta=True, alpha=0.5, out_dtype=BF16, name=nm + "_dw2",
                           hook=io.hook(nm + "_dw2")))
    da, db = ffn_dgate(dout_bf, w2, a, b, nm + "_dgate", hook=io.hook(nm + "_dgate"))
    io.put("w1t_" + tag, mm(da, u, ta=True, out_dtype=BF16, name=nm + "_dw1"))
    io.put("w3t_" + tag, mm(db, u, ta=True, out_dtype=BF16, name=nm + "_dw3", hook=io.hook(nm + "_dw3")))
    return du_norm_bwd([(da, w1t, False), (db, w3t, False)], h, g, dout, nm + "_du", hook=io.hook(nm + "_du"))


def conv_input_grad(d_parts, w, nm):
    tm = 256
    halo = 16
    t_dim = d_parts[0].shape[0]
    n_tiles = t_dim // tm

    def fn(d1, n1, d2, n2, d3, n3, ww):
        d = jnp.concatenate([d1, d2, d3], axis=1).astype(F32)
        nxt = jnp.concatenate([n1, n2, n3], axis=1).astype(F32)
        nxt = jnp.where(pl.program_id(0) < n_tiles - 1, nxt, 0.0)
        dd = jnp.concatenate([d, nxt], axis=0)
        out = dd[3:3 + tm] * ww[0:1]
        for k in range(1, SSM_CONV):
            out = out + dd[3 - k:3 - k + tm] * ww[k:k + 1]
        return (out,)

    rows = []
    for d in d_parts:
        below = pl.BlockSpec((halo, d.shape[1]),
                             lambda i: (jnp.minimum((i + 1) * (tm // halo), t_dim // halo - 1), 0))
        rows += [d, (d, below)]
    dx, = rowmap(fn, rows, [w], [(SSM_CONV_DIM, BF16)], tm=tm, name=nm)
    return dx


GRP_W = SSM_D_INNER // SSM_GROUPS
HPG = SSM_HEADS // SSM_GROUPS
HEAD_SHIFT = 6


def _split2(x):
    hi = x.astype(BF16)
    return hi, (x - hi.astype(F32)).astype(BF16)


def _expand_mats():
    e = ((lax.broadcasted_iota(jnp.int32, (HPG, GRP_W), 1) >> HEAD_SHIFT)
         == lax.broadcasted_iota(jnp.int32, (HPG, GRP_W), 0)).astype(BF16)
    et = ((lax.broadcasted_iota(jnp.int32, (GRP_W, HPG), 0) >> HEAD_SHIFT)
          == lax.broadcasted_iota(jnp.int32, (GRP_W, HPG), 1)).astype(BF16)
    return e, et


def _expand(v, e_m):
    hi, lo = _split2(v)
    return jnp.dot(hi, e_m, preferred_element_type=F32) + jnp.dot(lo, e_m, preferred_element_type=F32)


def _reduce8(v, et_m):
    hi, lo = _split2(v)
    return jnp.dot(hi, et_m, preferred_element_type=F32) + jnp.dot(lo, et_m, preferred_element_type=F32)


def _ssd_group_terms(dt_ref, dtT_ref, arow_ref, acol_ref):
    L = SSM_CHUNK
    r = lax.broadcasted_iota(jnp.int32, (L, L), 0)
    c = lax.broadcasted_iota(jnp.int32, (L, L), 1)
    tril = (r >= c).astype(BF16)
    triu = (r <= c).astype(BF16)
    dtg = dt_ref[0]
    acol = None
    for p in _split3(dtg * arow_ref[0]):
        t = jnp.dot(tril, p, preferred_element_type=F32)
        acol = t if acol is None else acol + t
    arowT = None
    for p in _split3(dtT_ref[0] * acol_ref[0]):
        t = jnp.dot(p, triu, preferred_element_type=F32)
        arowT = t if arowT is None else arowT + t
    return dtg, acol, arowT, r >= c


def _state_decay(a_last_col, et_m):
    hi, lo = _split2(jnp.broadcast_to(jnp.exp(a_last_col), (HPG, SSM_STATE)))
    return jnp.dot(et_m, hi, preferred_element_type=F32) + jnp.dot(et_m, lo, preferred_element_type=F32)


def _conv_block(x_ref, halo_ref, w_ref, b_ref, first):
    L = SSM_CHUNK
    xx = jnp.concatenate([jnp.where(first, 0.0, halo_ref[...]), x_ref[...]], axis=0)
    w = w_ref[...]
    shifted = [xx[5 + k:5 + k + L] for k in range(SSM_CONV)]
    acc = b_ref[...] + shifted[0] * w[0:1]
    for k in range(1, SSM_CONV):
        acc = acc + shifted[k] * w[k:k + 1]
    return acc, shifted


def _ssd_specs(nc, rev):
    L, N = SSM_CHUNK, SSM_STATE
    xcols = SSM_D_INNER // LANES
    ch = (lambda c: nc - 1 - c) if rev else (lambda c: c)
    above = lambda c: jnp.maximum(ch(c) * (L // 8) - 1, 0)
    specs = []
    for width, col in ((GRP_W, lambda g: g), (N, lambda g: xcols + g), (N, lambda g: xcols + SSM_GROUPS + g)):
        specs += [
            pl.BlockSpec((L, width), lambda c, g, col=col: (ch(c), col(g))),
            pl.BlockSpec((8, width), lambda c, g, col=col: (above(c), col(g))),
            pl.BlockSpec((SSM_CONV, width), lambda c, g, col=col: (0, col(g))),
            pl.BlockSpec((1, width), lambda c, g, col=col: (0, col(g))),
        ]
    return specs + [
        pl.BlockSpec((1, L, HPG), lambda c, g: (g, ch(c), 0)),
        pl.BlockSpec((1, HPG, L), lambda c, g: (g, 0, ch(c))),
        pl.BlockSpec((1, 1, HPG), lambda c, g: (g, 0, 0)),
        pl.BlockSpec((1, HPG, 1), lambda c, g: (g, 0, 0)),
        pl.BlockSpec((1, GRP_W), lambda c, g: (0, g)),
    ]


def ssd_fwd(xbc_raw, conv_w, conv_b, dt_g, dtT_g, a_row, a_col, dvec, nm, hook=None):
    t_dim = xbc_raw.shape[0]
    L, P, N = SSM_CHUNK, SSM_HEAD_DIM, SSM_STATE
    nc = t_dim // L

    def body(x_ref, xh_ref, xw_ref, xb_ref, b_ref, bh_ref, bw_ref, bb_ref, c_ref, ch_ref, cw_ref, cb_ref,
             dt_ref, dtT_ref, arow_ref, acol_ref, dvec_ref, y_ref, st_ref, s_s):
        ci = pl.program_id(0)
        g = pl.program_id(1)

        @pl.when((ci == 0) & (g == 0))
        def _():
            s_s[...] = jnp.zeros_like(s_s)

        e_m, et_m = _expand_mats()
        dtg, acol, arowT, causal = _ssd_group_terms(dt_ref, dtT_ref, arow_ref, acol_ref)
        a_last_row = acol[L - 1:L, :]
        x = _silu(_conv_block(x_ref, xh_ref, xw_ref, xb_ref, ci == 0)[0])
        bm = _silu(_conv_block(b_ref, bh_ref, bw_ref, bb_ref, ci == 0)[0])
        cm = _silu(_conv_block(c_ref, ch_ref, cw_ref, cb_ref, ci == 0)[0])
        cb = _dot(cm, bm, NT)
        s = s_s[g]
        st_ref[0, 0] = s
        ea_x = _expand(jnp.exp(acol), e_m)
        dt_x = _expand(dtg, e_m)
        w_x = _expand(jnp.exp(a_last_row - acol) * dtg, e_m)
        yb = ea_x * _dot(cm, s, NT) + dvec_ref[...] * x
        xd = (x * dt_x).astype(BF16)
        for e in range(HPG):
            sl = slice(e * P, (e + 1) * P)
            lm = jnp.exp(jnp.where(causal, acol[:, e:e + 1] - arowT[e:e + 1, :], NEG))
            m = (cb * lm).astype(BF16)
            y_ref[:, sl] = (yb[:, sl] + jnp.dot(m, xd[:, sl], preferred_element_type=F32)).astype(y_ref.dtype)
        s_s[g] = _state_decay(arowT[:, L - 1:L], et_m) * s + _dot(x * w_x, bm, TN)

    out_specs = [
        pl.BlockSpec((L, GRP_W), lambda c, g: (c, g)),
        pl.BlockSpec((1, 1, GRP_W, N), lambda c, g: (c, g, 0, 0)),
    ]
    return pcall(
        body, name=nm, grid=(nc, SSM_GROUPS), in_specs=_ssd_specs(nc, False), out_specs=out_specs,
        out_shape=[jax.ShapeDtypeStruct((t_dim, SSM_D_INNER), BF16),
                   jax.ShapeDtypeStruct((nc, SSM_GROUPS, GRP_W, N), F32)],
        scratch=[pltpu.VMEM((SSM_GROUPS, GRP_W, N), F32)],
        args=[xbc_raw, xbc_raw, conv_w, conv_b] * 3 + [dt_g, dtT_g, a_row, a_col, dvec], hook=hook)


def ssd_bwd(dy, xbc_raw, conv_w, conv_b, dt_g, dtT_g, a_row, a_col, dvec, states, nm, hook=None):
    t_dim = xbc_raw.shape[0]
    L, P, N = SSM_CHUNK, SSM_HEAD_DIM, SSM_STATE
    nc = t_dim // L

    def body(dy_ref, x_ref, xh_ref, xw_ref, xb_ref, b_ref, bh_ref, bw_ref, bb_ref, c_ref, ch_ref, cw_ref, cb_ref,
             dt_ref, dtT_ref, arow_ref, acol_ref, dvec_ref, st_ref,
             dx_ref, db_ref, dc_ref, da_ref, ddt_ref, dd_ref, dwx_ref, dwb_ref, dwc_ref, dbx_ref, dbb_ref, dbc_ref,
             ds_s, yd_s, dxd_s):
        ci = pl.program_id(0)
        g = pl.program_id(1)

        @pl.when((ci == 0) & (g == 0))
        def _():
            ds_s[...] = jnp.zeros_like(ds_s)
            for r in (dd_ref, dwx_ref, dwb_ref, dwc_ref, dbx_ref, dbb_ref, dbc_ref):
                r[...] = jnp.zeros_like(r)

        e_m, et_m = _expand_mats()
        dtg, acol, arowT, causal = _ssd_group_terms(dt_ref, dtT_ref, arow_ref, acol_ref)
        a_last_row = acol[L - 1:L, :]
        first = ci == nc - 1
        pre_x, sh_x = _conv_block(x_ref, xh_ref, xw_ref, xb_ref, first)
        pre_b, sh_b = _conv_block(b_ref, bh_ref, bw_ref, bb_ref, first)
        pre_c, sh_c = _conv_block(c_ref, ch_ref, cw_ref, cb_ref, first)
        sg_x, sg_b, sg_c = _sigmoid(pre_x), _sigmoid(pre_b), _sigmoid(pre_c)
        x = pre_x * sg_x
        dy = dy_ref[...]
        bm = pre_b * sg_b
        cm = pre_c * sg_c
        cb = _dot(cm, bm, NT)
        s = st_ref[0, 0]
        dsp = ds_s[g]
        ew8 = jnp.exp(a_last_row - acol)
        ea_x = _expand(jnp.exp(acol), e_m)
        dt_x = _expand(dtg, e_m)
        ew_x = _expand(ew8, e_m)
        w_x = ew_x * dt_x
        z = _dot(cm, s, NT)
        dz = ea_x * dy
        dc = _dot(dz, s)
        ds_y = _dot(dz, cm, TN)
        du = _dot(bm, dsp, NT)
        u = x * w_x
        db = _dot(u, dsp)
        xd = (x * dt_x).astype(BF16)
        dyb = dy.astype(BF16)
        dcb = jnp.zeros((L, L), F32)
        for e in range(HPG):
            sl = slice(e * P, (e + 1) * P)
            lm = jnp.exp(jnp.where(causal, acol[:, e:e + 1] - arowT[e:e + 1, :], NEG))
            m = (cb * lm).astype(BF16)
            yd_s[:, sl] = jnp.dot(m, xd[:, sl], preferred_element_type=F32)
            dxd_s[:, sl] = lax.dot_general(m, dyb[:, sl], TN, preferred_element_type=F32)
            dcb = dcb + lax.dot_general(dyb[:, sl], xd[:, sl], NT, preferred_element_type=F32) * lm
        dxd = dxd_s[...]

        def through_conv(d_act, pre, sg, shifted, d_ref, dw_ref, dbias_ref):
            d_pre = d_act * (sg * (1.0 + pre * (1.0 - sg)))
            d_ref[...] = d_pre.astype(d_ref.dtype)
            dw_ref[g] += jnp.concatenate([jnp.sum(d_pre * sh, axis=0, keepdims=True) for sh in shifted], axis=0)
            dbias_ref[g] += jnp.sum(d_pre, axis=0, keepdims=True)

        through_conv(dvec_ref[...] * dy + du * w_x + dt_x * dxd, pre_x, sg_x, sh_x, dx_ref, dwx_ref, dbx_ref)
        ddt = _reduce8(x * (ew_x * du + dxd), et_m)
        da = (_reduce8(dz * z + dyb.astype(F32) * yd_s[...], et_m)
              - _reduce8(xd.astype(F32) * dxd + du * u, et_m))
        dwa_row = _reduce8(jnp.broadcast_to(jnp.sum(du * u, axis=0, keepdims=True), (8, GRP_W)), et_m)[0:1]
        t_nh = None
        for p in _split3(dsp * s):
            t = lax.dot_general(p, et_m, TN, preferred_element_type=F32)
            t_nh = t if t_nh is None else t_nh + t
        d_last = dwa_row + jnp.exp(a_last_row) * jnp.sum(t_nh, axis=0, keepdims=True)
        row_l = lax.broadcasted_iota(jnp.int32, (L, 1), 0)
        da_ref[0] = da + jnp.where(row_l == L - 1, d_last, 0.0)
        ddt_ref[0] = ddt
        dd_ref[g] += jnp.sum(dy * x, axis=0, keepdims=True)
        through_conv(dc + _dot(dcb, bm), pre_c, sg_c, sh_c, dc_ref, dwc_ref, dbc_ref)
        through_conv(db + _dot(dcb, cm, TN), pre_b, sg_b, sh_b, db_ref, dwb_ref, dbb_ref)
        ds_s[g] = _state_decay(arowT[:, L - 1:L], et_m) * dsp + ds_y

    rc = lambda c: nc - 1 - c
    in_specs = ([pl.BlockSpec((L, GRP_W), lambda c, g: (rc(c), g))] + _ssd_specs(nc, True)
                + [pl.BlockSpec((1, 1, GRP_W, N), lambda c, g: (rc(c), g, 0, 0))])
    whole = lambda *shape: pl.BlockSpec(shape, lambda c, g: (0,) * len(shape))
    out_specs = [
        pl.BlockSpec((L, GRP_W), lambda c, g: (rc(c), g)),
        pl.BlockSpec((L, N), lambda c, g: (rc(c), g)),
        pl.BlockSpec((L, N), lambda c, g: (rc(c), g)),
        pl.BlockSpec((1, L, HPG), lambda c, g: (g, rc(c), 0)),
        pl.BlockSpec((1, L, HPG), lambda c, g: (g, rc(c), 0)),
        whole(SSM_GROUPS, 1, GRP_W),
        whole(SSM_GROUPS, SSM_CONV, GRP_W), whole(SSM_GROUPS, SSM_CONV, N), whole(SSM_GROUPS, SSM_CONV, N),
        whole(SSM_GROUPS, 1, GRP_W), whole(SSM_GROUPS, 1, N), whole(SSM_GROUPS, 1, N),
    ]
    gn = SSM_GROUPS * N
    acc = lambda *shape: jax.ShapeDtypeStruct(shape, F32)
    out_shape = [
        jax.ShapeDtypeStruct((t_dim, SSM_D_INNER), BF16), jax.ShapeDtypeStruct((t_dim, gn), BF16),
        jax.ShapeDtypeStruct((t_dim, gn), BF16), acc(SSM_GROUPS, t_dim, HPG),
        acc(SSM_GROUPS, t_dim, HPG), acc(SSM_GROUPS, 1, GRP_W),
        acc(SSM_GROUPS, SSM_CONV, GRP_W), acc(SSM_GROUPS, SSM_CONV, N), acc(SSM_GROUPS, SSM_CONV, N),
        acc(SSM_GROUPS, 1, GRP_W), acc(SSM_GROUPS, 1, N), acc(SSM_GROUPS, 1, N),
    ]
    return pcall(
        body, name=nm, grid=(nc, SSM_GROUPS), in_specs=in_specs, out_specs=out_specs, out_shape=out_shape,
        scratch=[pltpu.VMEM((SSM_GROUPS, GRP_W, N), F32), pltpu.VMEM((L, GRP_W), F32), pltpu.VMEM((L, GRP_W), F32)],
        args=[dy] + [xbc_raw, xbc_raw, conv_w, conv_b] * 3 + [dt_g, dtT_g, a_row, a_col, dvec, states], hook=hook)


def _softplus(x):
    return jnp.maximum(x, 0.0) + jnp.log(1.0 + jnp.exp(-jnp.abs(x)))


def ssd_dt_bwd(da, ddt, dt, dt_raw, a_row, dt_bias, nm):
    L = SSM_CHUNK

    def fn(d_a, d_dt, dtv, raw, ar, bias):
        r = lax.broadcasted_iota(jnp.int32, (L, L), 0)
        c = lax.broadcasted_iota(jnp.int32, (L, L), 1)
        triu = (r <= c).astype(BF16)
        acc = None
        for p in _split3(d_a):
            t = jnp.dot(triu, p, preferred_element_type=F32)
            acc = t if acc is None else acc + t
        d_dt = d_dt + acc * ar
        d_a_h = jnp.sum(acc * dtv, axis=0, keepdims=True)
        d_raw = d_dt * _sigmoid(raw + bias)
        return d_raw, d_a_h, jnp.sum(d_raw, axis=0, keepdims=True)

    return rowmap(fn, [da, ddt, dt, dt_raw], [a_row, dt_bias], [(SSM_HEADS, BF16)],
                  [(1, SSM_HEADS), (1, SSM_HEADS)], tm=L, name=nm)


GN_W = SSM_D_INNER // SSM_GROUPS


def mamba_fwd(h, p, nm, io):
    def in_proj(x, gg, w_zt, w_xbct, w_dtt):
        uu = _rms_fwd(x, gg).astype(BF16)
        return uu, _dot(uu, w_zt, NT), _dot(uu, w_xbct, NT), _dot(uu, w_dtt, NT)

    u, z, xbc_raw, dt_raw = rowmap(in_proj, [h], [p["ssm_norm"], p["w_zt"], p["w_xbct"], p["w_dtt"]],
                                   [(D_MODEL, BF16), (SSM_D_INNER, BF16), (SSM_CONV_DIM, F32), (SSM_HEADS, F32)],
                                   tm=RESIDENT_TM, name=nm + "_in", hook=io.hook(nm + "_in"))
    dt, = rowmap(lambda r, b: (_softplus(r + b),), [dt_raw], [p["dt_bias"]], [(SSM_HEADS, F32)], tm=256,
                 name=nm + "_softplus")
    dt_g = dt.reshape(-1, SSM_GROUPS, HPG).transpose(1, 0, 2)
    dtT_g = dt_g.transpose(0, 2, 1)
    y, states = ssd_fwd(xbc_raw, p["conv_w"], p["conv_b"], dt_g, dtT_g, p["a_row"], p["a_col"], p["dvec"],
                        nm + "_ssd", hook=io.hook(nm + "_ssd"))

    def gate_norm_out(yv, zv, hv, gg, w_out):
        t = yv.astype(F32) * _silu(zv.astype(F32))
        yn = jnp.concatenate([_rms_fwd(t[:, k * GN_W:(k + 1) * GN_W], gg[:, k * GN_W:(k + 1) * GN_W])
                              for k in range(SSM_GROUPS)], axis=1).astype(BF16)
        return yn, hv + _dot(yn, w_out)

    yn, out = rowmap(gate_norm_out, [y, z, h], [p["gate_norm"], p["w_out"]],
                     [(SSM_D_INNER, BF16), (D_MODEL, F32)], tm=RESIDENT_TM, name=nm + "_out")
    return out, (u, z, xbc_raw, dt_raw, dt, dt_g, dtT_g, y, states, yn)


def mamba_bwd(h, p, saved, dout, dout_bf, nm, io):
    u, z, xbc_raw, dt_raw, dt, dt_g, dtT_g, y, states, yn = saved
    g = {}
    io.put("w_out", mm(yn, dout_bf, ta=True, out_dtype=BF16, name=nm + "_dwout"))

    def gate_norm_bwd(d_o, yv, zv, gg, w_out):
        yv = yv.astype(F32)
        zv = zv.astype(F32)
        d = _dot(d_o, w_out, NT)
        sz = _silu(zv)
        t = yv * sz
        dts, dgs = [], []
        for k in range(SSM_GROUPS):
            sl = slice(k * GN_W, (k + 1) * GN_W)
            dt_k, dg_k = _rms_bwd(t[:, sl], gg[:, sl], d[:, sl])
            dts.append(dt_k)
            dgs.append(dg_k)
        d_t = jnp.concatenate(dts, axis=1)
        return d_t * sz, d_t * yv * _silu_grad(zv), jnp.concatenate(dgs, axis=1)

    dy, dz, g["gate_norm"] = rowmap(gate_norm_bwd, [dout_bf, y, z], [p["gate_norm"], p["w_out"]],
                                    [(SSM_D_INNER, F32), (SSM_D_INNER, BF16)], [(1, SSM_D_INNER)], tm=256,
                                    name=nm + "_dgatenorm")
    d_x, d_b, d_c, da_g, ddt_g, dd, dwx, dwb, dwc, dbx, dbb, dbc = ssd_bwd(
        dy, xbc_raw, p["conv_w"], p["conv_b"], dt_g, dtT_g, p["a_row"], p["a_col"], p["dvec"], states, nm + "_dssd",
        hook=io.hook(nm + "_dssd"))
    g["dvec"] = dd
    by_lane = lambda t: t.transpose(1, 0, 2).reshape(t.shape[1], -1)
    g["conv_w"] = jnp.concatenate([by_lane(dwx), by_lane(dwb), by_lane(dwc)], axis=1)
    g["conv_b"] = jnp.concatenate([by_lane(dbx), by_lane(dbb), by_lane(dbc)], axis=1)
    per_head = lambda t: t.transpose(1, 0, 2).reshape(-1, SSM_HEADS)
    ddt_raw, g["a"], g["dt_bias"] = ssd_dt_bwd(per_head(da_g), per_head(ddt_g), dt, dt_raw, p["a_heads"],
                                               p["dt_bias"], nm + "_ddt")
    dxbc_raw = conv_input_grad([d_x, d_b, d_c], p["conv_w"], nm + "_dconv")
    io.put("w_int", jnp.concatenate([mm(dz, u, ta=True, out_dtype=BF16, name=nm + "_dwz"),
                                     mm(dxbc_raw, u, ta=True, out_dtype=BF16, name=nm + "_dwxbc"),
                                     mm(ddt_raw, u, ta=True, out_dtype=BF16, name=nm + "_dwdt")], axis=0))
    dh, dh_bf, g["ssm_norm"] = du_norm_bwd(
        [(dz, p["w_zt"], False), (dxbc_raw, p["w_xbct"], False), (ddt_raw, p["w_dtt"], False)],
        h, p["ssm_norm"], dout, nm + "_du", hook=io.hook(nm + "_du"))
    return dh, dh_bf, g


KV_W = ATT_KV_HEADS * ATT_HEAD_DIM


def kv_fwd(h, p, nm):
    def kv_proj(x, gg, w_kv, gk):
        uu = _rms_fwd(x, gg).astype(BF16)
        t = _dot(uu, w_kv)
        ks = [_rms_fwd(t[:, j * ATT_HEAD_DIM:(j + 1) * ATT_HEAD_DIM], gk) for j in range(ATT_KV_HEADS)]
        return uu, t, jnp.concatenate(ks, axis=1), t[:, KV_W:]

    u, kv_raw, k, v = rowmap(kv_proj, [h], [p["kv_norm"], p["w_kv"], p["k_norm"]],
                             [(D_MODEL, BF16), (2 * KV_W, F32), (KV_W, F32), (KV_W, F32)], tm=RESIDENT_TM,
                             name=nm + "_proj")
    return k, v, (u, kv_raw)


def kv_bwd(h, p, saved, dk_cur, dk_prev, dv_cur, dv_prev, dout, nm, io):
    u, kv_raw = saved
    t_dim = h.shape[0]
    tm = ATT_WINDOW
    nb = t_dim // tm
    nxt = pl.BlockSpec((tm, KV_W), lambda i: (jnp.minimum(i + 1, nb - 1), 0))

    def fn(dkc, dkp, dvc, dvp, t, gg):
        live = pl.program_id(0) < nb - 1
        dk = dkc + jnp.where(live, dkp, 0.0)
        dv = dvc + jnp.where(live, dvp, 0.0)
        outs, dgs = [], None
        for j in range(ATT_KV_HEADS):
            sl = slice(j * ATT_HEAD_DIM, (j + 1) * ATT_HEAD_DIM)
            dx, dg = _rms_bwd(t[:, sl], gg, dk[:, sl])
            outs.append(dx)
            dgs = dg if dgs is None else dgs + dg
        return jnp.concatenate(outs + [dv], axis=1), dgs

    dkv_raw, dknorm = rowmap(fn, [dk_cur, (dk_prev, nxt), dv_cur, (dv_prev, nxt), kv_raw], [p["k_norm"]],
                             [(2 * KV_W, BF16)], [(1, ATT_HEAD_DIM)], tm=tm, name=nm + "_dknorm",
                             hook=io.hook(nm + "_dknorm"))
    g = {"k_norm": dknorm}
    io.put("w_kv", mm(u, dkv_raw, ta=True, out_dtype=BF16, name=nm + "_dwkv"))
    dh, dh_bf, g["kv_norm"] = du_norm_bwd([(dkv_raw, p["w_kv"], True)], h, p["kv_norm"], dout, nm + "_du",
                                          hook=io.hook(nm + "_du"))
    return dh, dh_bf, g


def _attn_scores(q_ref, kp_ref, kc_ref, vp_ref, vc_ref, qn_ref, bias_ref, sink_ref, kv, mxu_sum):
    hd = ATT_HEAD_DIM
    blk = ATT_WINDOW
    sl = slice(kv * hd, (kv + 1) * hd)
    kk = jnp.concatenate([kp_ref[:, sl], kc_ref[:, sl]], axis=0)
    vv = jnp.concatenate([vp_ref[:, sl], vc_ref[:, sl]], axis=0)
    gq = qn_ref[...]
    raws, rinvs = [], []
    for r in range(ATT_GROUP):
        hh = kv * ATT_GROUP + r
        x = q_ref[:, hh * hd:(hh + 1) * hd]
        raws.append(x)
        rinvs.append(lax.rsqrt(jnp.mean(x * x, axis=-1, keepdims=True) + EPS))
    xh = jnp.concatenate([x * ri for x, ri in zip(raws, rinvs)], axis=0)
    rinv = jnp.concatenate(rinvs, axis=0)
    q8 = xh * gq
    s = _dot(q8, kk, NT) * (hd ** -0.5) + bias_ref[kv]
    colk = lax.broadcasted_iota(jnp.int32, (1, 2 * blk), 1)
    s = jnp.where((pl.program_id(0) > 0) | (colk >= blk), s, NEG)
    sink = sink_ref[kv]
    m = jnp.maximum(jnp.max(s, axis=-1, keepdims=True), sink)
    pexp = jnp.exp(s - m)
    e_sink = jnp.exp(sink - m)
    if not mxu_sum:
        inv_den = 1.0 / (jnp.sum(pexp, axis=-1, keepdims=True) + e_sink)
        return kk, vv, xh, rinv, q8, pexp * inv_den, e_sink * inv_den
    ones = jnp.ones((2 * blk, LANES), BF16)
    inv_den = 1.0 / (jnp.dot(pexp.astype(BF16), ones, preferred_element_type=F32) + e_sink)
    return kk, vv, xh, rinv, q8, pexp * jnp.concatenate([inv_den, inv_den], axis=1), e_sink * inv_den[:, :1]


def _attn_specs(nb):
    blk = ATT_WINDOW
    cur = lambda i: (i, 0)
    prev = lambda i: (jnp.maximum(i - 1, 0), 0)
    return [
        pl.BlockSpec((blk, D_MODEL), cur),
        pl.BlockSpec((blk, KV_W), prev), pl.BlockSpec((blk, KV_W), cur),
        pl.BlockSpec((blk, KV_W), prev), pl.BlockSpec((blk, KV_W), cur),
        pl.BlockSpec((1, ATT_HEAD_DIM), lambda i: (0, 0)),
        pl.BlockSpec((ATT_KV_HEADS, ATT_GROUP * blk, 2 * blk), lambda i: (0, 0, 0)),
        pl.BlockSpec((ATT_KV_HEADS, ATT_GROUP * blk, 1), lambda i: (0, 0, 0)),
    ]


def attn_fwd(q_raw, k, v, q_norm, bias, sink_col, nm):
    t_dim = q_raw.shape[0]
    blk, hd = ATT_WINDOW, ATT_HEAD_DIM
    nb = t_dim // blk

    def body(q_ref, kp_ref, kc_ref, vp_ref, vc_ref, qn_ref, bias_ref, sink_ref, o_ref):
        for kv in range(ATT_KV_HEADS):
            kk, vv, xh, rinv, q8, prob, p_sink = _attn_scores(q_ref, kp_ref, kc_ref, vp_ref, vc_ref, qn_ref,
                                                              bias_ref, sink_ref, kv, False)
            o8 = _dot(prob, vv)
            for r in range(ATT_GROUP):
                hh = kv * ATT_GROUP + r
                o_ref[:, hh * hd:(hh + 1) * hd] = o8[r * blk:(r + 1) * blk].astype(o_ref.dtype)

    out, = pcall(body, name=nm, grid=(nb,), in_specs=_attn_specs(nb),
                 out_specs=[pl.BlockSpec((blk, D_MODEL), lambda i: (i, 0))],
                 out_shape=[jax.ShapeDtypeStruct((t_dim, D_MODEL), BF16)],
                 args=[q_raw, k, k, v, v, q_norm, bias, sink_col])
    return out


def attn_bwd(do, q_raw, k, v, q_norm, bias, sink_col, nm, hook=None):
    t_dim = q_raw.shape[0]
    blk, hd = ATT_WINDOW, ATT_HEAD_DIM
    nb = t_dim // blk
    scale = hd ** -0.5

    def body(do_ref, q_ref, kp_ref, kc_ref, vp_ref, vc_ref, qn_ref, bias_ref, sink_ref,
             dq_ref, dkc_ref, dkp_ref, dvc_ref, dvp_ref, dbias_ref, dsink_ref, dqn_ref):
        @pl.when(pl.program_id(0) == 0)
        def _():
            dbias_ref[...] = jnp.zeros_like(dbias_ref)
            dsink_ref[...] = jnp.zeros_like(dsink_ref)
            dqn_ref[...] = jnp.zeros_like(dqn_ref)

        gq = qn_ref[...]
        for kv in range(ATT_KV_HEADS):
            kk, vv, xh, rinv, q8, prob, p_sink = _attn_scores(q_ref, kp_ref, kc_ref, vp_ref, vc_ref, qn_ref,
                                                              bias_ref, sink_ref, kv, True)
            do8 = jnp.concatenate([do_ref[:, (kv * ATT_GROUP + r) * hd:(kv * ATT_GROUP + r + 1) * hd]
                                   for r in range(ATT_GROUP)], axis=0)
            dp = _dot(do8, vv, NT)
            delta = jnp.sum(prob * dp, axis=-1, keepdims=True)
            ds = prob * (dp - delta)
            dsink_ref[kv] += -p_sink * delta
            dbias_ref[kv] += ds
            ds_s = ds * scale
            dq8 = _dot(ds_s, kk)
            dkk = _dot(ds_s, q8, TN)
            dvv = _dot(prob, do8, TN)
            dqn_ref[...] += jnp.sum(dq8 * xh, axis=0, keepdims=True)
            dxh = dq8 * gq
            dq_raw8 = rinv * (dxh - xh * jnp.mean(dxh * xh, axis=-1, keepdims=True))
            for r in range(ATT_GROUP):
                hh = kv * ATT_GROUP + r
                dq_ref[:, hh * hd:(hh + 1) * hd] = dq_raw8[r * blk:(r + 1) * blk].astype(dq_ref.dtype)
            sl = slice(kv * hd, (kv + 1) * hd)
            dkp_ref[:, sl] = dkk[:blk]
            dkc_ref[:, sl] = dkk[blk:]
            dvp_ref[:, sl] = dvv[:blk]
            dvc_ref[:, sl] = dvv[blk:]

    cur = lambda i: (i, 0)
    row_spec = pl.BlockSpec((blk, KV_W), cur)
    out_specs = [
        pl.BlockSpec((blk, D_MODEL), cur), row_spec, row_spec, row_spec, row_spec,
        pl.BlockSpec((ATT_KV_HEADS, ATT_GROUP * blk, 2 * blk), lambda i: (0, 0, 0)),
        pl.BlockSpec((ATT_KV_HEADS, ATT_GROUP * blk, 1), lambda i: (0, 0, 0)),
        pl.BlockSpec((1, hd), lambda i: (0, 0)),
    ]
    kvs = jax.ShapeDtypeStruct((t_dim, KV_W), F32)
    out_shape = [
        jax.ShapeDtypeStruct((t_dim, D_MODEL), BF16), kvs, kvs, kvs, kvs,
        jax.ShapeDtypeStruct((ATT_KV_HEADS, ATT_GROUP * blk, 2 * blk), F32),
        jax.ShapeDtypeStruct((ATT_KV_HEADS, ATT_GROUP * blk, 1), F32),
        jax.ShapeDtypeStruct((1, hd), F32),
    ]
    return pcall(body, name=nm, grid=(nb,), in_specs=[pl.BlockSpec((blk, D_MODEL), cur)] + _attn_specs(nb),
                 out_specs=out_specs, out_shape=out_shape,
                 args=[do, q_raw, k, k, v, v, q_norm, bias, sink_col], hook=hook)


def _t5_bucket_np():
    blk = ATT_WINDOW
    qi = np.arange(blk)[:, None] + blk
    kj = np.arange(2 * blk)[None, :]
    dist = qi - kj
    n = np.maximum(dist, 0)
    max_exact = REL_BUCKETS // 2
    nf = np.maximum(n, 1).astype(np.float32)
    large = max_exact + (np.log(nf / max_exact) / math.log(ATT_WINDOW / max_exact)
                         * (REL_BUCKETS - max_exact)).astype(np.int32)
    large = np.minimum(large, REL_BUCKETS - 1)
    bucket = np.where(n < max_exact, n, large)
    in_window = (dist >= 0) & (dist < ATT_WINDOW)
    return bucket, in_window


def attn_block_fwd(h, k, v, p, nm):
    def q_proj(x, gg, w_q):
        uu = _rms_fwd(x, gg).astype(BF16)
        return uu, _dot(uu, w_q)

    u, q_raw = rowmap(q_proj, [h], [p["attn_norm"], p["w_q"]], [(D_MODEL, BF16), (D_MODEL, F32)], tm=RESIDENT_TM,
                      name=nm + "_q")
    o = attn_fwd(q_raw, k, v, p["q_norm"], p["bias"], p["sink_col"], nm + "_core")
    out = mm(o, p["w_o"], res=h, name=nm + "_o")
    return out, (u, q_raw, o)


def attn_block_bwd(h, k, v, p, saved, dout, dout_bf, nm, io):
    u, q_raw, o = saved
    g = {}
    io.put("w_o", mm(o, dout_bf, ta=True, out_dtype=BF16, name=nm + "_dwo", hook=io.hook(nm + "_dwo")))
    do = mm(dout_bf, p["w_o"], tb=True, name=nm + "_do")
    dq_raw, dkc, dkp, dvc, dvp, g["bias"], g["sink_col"], g["q_norm"] = attn_bwd(
        do, q_raw, k, v, p["q_norm"], p["bias"], p["sink_col"], nm + "_dcore", hook=io.hook(nm + "_dcore"))
    io.put("w_q", mm(u, dq_raw, ta=True, out_dtype=BF16, name=nm + "_dwq"))
    dh, dh_bf, g["attn_norm"] = du_norm_bwd([(dq_raw, p["w_q"], True)], h, p["attn_norm"], dout, nm + "_du")
    return dh, dh_bf, g, (dkc, dkp, dvc, dvp)


FFN_TAGS = ["00", "01", "10", "11"]


def local_step(x, target, small, io):
    bucket, in_window = _t5_bucket_np()
    blk = ATT_WINDOW
    w = small

    fnorm = {tag: w["ffn_norm"][int(tag[0]), int(tag[1])][None, :] for tag in FFN_TAGS}
    a_neg = -jnp.exp(w["ssm_a_log"][0])

    def mamba_p():
        w_int = io.w("w_int")
        return dict(ssm_norm=w["ssm_norm"], w_zt=w_int[:SSM_D_INNER],
                    w_xbct=w_int[SSM_D_INNER:SSM_D_INNER + SSM_CONV_DIM], w_dtt=w_int[SSM_D_INNER + SSM_CONV_DIM:],
                    conv_w=w["ssm_conv_w"][0], conv_b=w["ssm_conv_b"], dt_bias=w["ssm_dt_bias"],
                    a_heads=a_neg[None, :], a_row=a_neg.reshape(SSM_GROUPS, 1, HPG),
                    a_col=a_neg.reshape(SSM_GROUPS, HPG, 1),
                    dvec=jnp.repeat(w["ssm_d"][0], SSM_HEAD_DIM)[None, :],
                    gate_norm=w["ssm_gate_norm"], w_out=io.w("w_out"))

    rb = w["rel_bias"]
    onehot3 = (np.arange(REL_BUCKETS)[:, None, None] == bucket[None]).astype(np.float32)
    bias = jnp.einsum("bh,bqk->hqk", rb, onehot3, precision=lax.Precision.HIGHEST)
    bias = jnp.where(in_window[None], bias, NEG)
    bias = bias.reshape(ATT_KV_HEADS, ATT_GROUP * blk, 2 * blk)
    sink_col = jnp.repeat(w["sinks"][0], blk).reshape(ATT_KV_HEADS, ATT_GROUP * blk, 1)

    def attn_p():
        return dict(attn_norm=w["attn_norm"], w_q=io.w("w_q"), q_norm=w["q_norm"], bias=bias, sink_col=sink_col,
                    w_o=io.w("w_o"))

    def kv_p():
        return dict(kv_norm=w["kv_norm"][None, :], w_kv=io.w("w_kv"), k_norm=w["k_norm"][None, :])

    h0 = x
    h0a, s_f00 = ffn_fwd(h0, fnorm["00"], "00", io)
    mp = mamba_p()
    h0b, s_m = mamba_fwd(h0a, mp, "ssm", io)
    h1, s_f01 = ffn_fwd(h0b, fnorm["01"], "01", io)
    kp = kv_p()
    k, v, s_kv = kv_fwd(h1, kp, "kv")
    h1a, s_f10 = ffn_fwd(h1, fnorm["10"], "10", io)
    ap = attn_p()
    h1b, s_a = attn_block_fwd(h1a, k, v, ap, "att")
    (dh, dh_bf, sq), s_f11 = ffn_fwd(h1b, fnorm["11"], "11", io, target=target)
    loss_part = jnp.sum(sq) * (0.5 / D_MODEL)

    fg = {}

    def ffn_back(tag, h_in, saved, dh, dh_bf):
        dh, dh_bf, dg = ffn_bwd(h_in, fnorm[tag], tag, saved, dh, dh_bf, io)
        fg[tag] = dg[0]
        return dh, dh_bf

    dh, dh_bf = ffn_back("11", h1b, s_f11, dh, dh_bf)
    dh, dh_bf, ga, dkv = attn_block_bwd(h1a, k, v, ap, s_a, dh, dh_bf, "att", io)
    dh, dh_bf = ffn_back("10", h1, s_f10, dh, dh_bf)
    dh, dh_bf, gk = kv_bwd(h1, kp, s_kv, *dkv, dh, "kv", io)
    dh, dh_bf = ffn_back("01", h0b, s_f01, dh, dh_bf)
    dh, dh_bf, gm = mamba_bwd(h0a, mp, s_m, dh, dh_bf, "ssm", io)
    dh, dh_bf = ffn_back("00", h0, s_f00, dh, dh_bf)
    grad_x = dh

    grads = {}
    grads["ffn_norm"] = jnp.stack([fg[tag] for tag in FFN_TAGS]).reshape(2, 2, D_MODEL)
    grads["ssm_norm"] = gm["ssm_norm"]
    grads["ssm_conv_w"] = gm["conv_w"][None]
    grads["ssm_conv_b"] = gm["conv_b"]
    grads["ssm_dt_bias"] = gm["dt_bias"]
    grads["ssm_a_log"] = gm["a"] * a_neg[None, :]
    grads["ssm_d"] = jnp.sum(gm["dvec"].reshape(SSM_HEADS, SSM_HEAD_DIM), axis=1)[None, :]
    grads["ssm_gate_norm"] = gm["gate_norm"]
    grads["kv_norm"] = gk["kv_norm"][0]
    grads["k_norm"] = gk["k_norm"][0]
    grads["attn_norm"] = ga["attn_norm"]
    grads["q_norm"] = ga["q_norm"]
    grads["sinks"] = jnp.sum(ga["sink_col"].reshape(ATT_HEADS, blk), axis=1)[None, :]
    onehot = (np.arange(REL_BUCKETS)[:, None] == bucket.reshape(1, -1)).astype(np.float32)
    dbias2d = ga["bias"].reshape(ATT_HEADS, blk * 2 * blk)
    grads["rel_bias"] = mm(jnp.asarray(onehot, BF16), dbias2d, tb=True, name="drelbias")
    return loss_part, grad_x, grads


def _adamw(g, w, m, v):
    m = ADAM_B1 * m + (1.0 - ADAM_B1) * g
    v = ADAM_B2 * v + (1.0 - ADAM_B2) * (g * g)
    m_hat = m / (1.0 - ADAM_B1 ** ADAM_STEP)
    v_hat = v / (1.0 - ADAM_B2 ** ADAM_STEP)
    delta = -ADAM_LR * (m_hat / (jnp.sqrt(v_hat) + ADAM_EPS) + ADAM_WD * w)
    return delta, m, v


def _slot_sum(r):
    g = r[0].astype(F32)
    for d in range(1, r.shape[0]):
        g = g + r[d].astype(F32)
    return g


def adamw_rows(recvs, w, m, v, name):
    n_l, rows, width = w.shape
    n_slots = recvs[0].shape[0]
    tr = 32
    assert rows % tr == 0, rows
    nt = rows // tr

    def body(*refs):
        r_refs = refs[:n_l]
        w_ref, m_ref, v_ref, g_o, d_o, m_o, v_o = refs[n_l:]
        li = pl.program_id(0)
        for k in range(n_l):
            @pl.when(li == k)
            def _(k=k):
                g = _slot_sum(r_refs[k])
                delta, m2, v2 = _adamw(g, w_ref[0], m_ref[0], v_ref[0])
                g_o[0] = g
                d_o[0] = delta
                m_o[0] = m2
                v_o[0] = v2

    def r_spec(k):
        return pl.BlockSpec((n_slots, tr, width),
                            lambda li, j: (0, jnp.where(li == k, j, jnp.where(li > k, nt - 1, 0)), 0))

    w_spec = pl.BlockSpec((1, tr, width), lambda li, j: (li, j, 0))
    shp = jax.ShapeDtypeStruct(w.shape, F32)
    return pcall(body, name=name, grid=(n_l, nt), in_specs=[r_spec(k) for k in range(n_l)] + [w_spec] * 3,
                 out_specs=[w_spec] * 4, out_shape=[shp] * 4, args=list(recvs) + [w, m, v])


def adamw_cols(recvs, w, m, v, name):
    n_l, rows, n = w.shape
    n_slots = recvs[0].shape[0]
    tr = 256
    nt = rows // tr

    def body(*refs):
        r_refs = refs[:n_l]
        w_ref, m_ref, v_ref, g_o, d_o, m_o, v_o = refs[n_l:]
        li = pl.program_id(0)
        for k in range(n_l):
            @pl.when(li == k)
            def _(k=k):
                g = _slot_sum(r_refs[k]).T
                delta, m2, v2 = _adamw(g, w_ref[0], m_ref[0], v_ref[0])
                g_o[0] = g
                d_o[0] = delta
                m_o[0] = m2
                v_o[0] = v2

    def r_spec(k):
        return pl.BlockSpec((n_slots, n, tr),
                            lambda li, j: (0, 0, jnp.where(li == k, j, jnp.where(li > k, nt - 1, 0))))

    w_spec = pl.BlockSpec((1, tr, n), lambda li, j: (li, j, 0))
    shp = jax.ShapeDtypeStruct(w.shape, F32)
    return pcall(body, name=name, grid=(n_l, nt), in_specs=[r_spec(k) for k in range(n_l)] + [w_spec] * 3,
                 out_specs=[w_spec] * 4, out_shape=[shp] * 4, args=list(recvs) + [w, m, v])


WEIGHT_NAMES = ["ffn_norm", "ffn_w1", "ffn_w3", "ffn_w2", "ssm_norm", "ssm_w_in", "ssm_conv_w", "ssm_conv_b",
                "ssm_dt_bias", "ssm_a_log", "ssm_d", "ssm_gate_norm", "ssm_w_out", "kv_norm", "w_kv", "k_norm",
                "attn_norm", "w_q", "q_norm", "sinks", "w_o", "rel_bias"]

SMALL = [
    ("ffn_norm", (2, 2, 1024), 2), ("ssm_norm", (1, 1024), 1), ("ssm_conv_w", (1, 4, 3072), 2),
    ("ssm_conv_b", (1, 3072), 1), ("ssm_gate_norm", (1, 2048), 1),
    ("ssm_dt_bias", (1, 32), None), ("ssm_a_log", (1, 32), None), ("ssm_d", (1, 32), None),
    ("kv_norm", (1024,), None), ("k_norm", (64,), None), ("attn_norm", (1, 1024), None),
    ("q_norm", (1, 64), None), ("sinks", (1, 16), None), ("rel_bias", (32, 16), None),
]
SMALL_W = 1024
SMALL_FULL_ROWS = 32
SMALL_LOCAL_ROWS = 48

MAT_GROUPS = {
    "f00_up": ["w1t_00", "w3t_00"], "f00_down": ["w2_00"], "f01": ["w1t_01", "w3t_01", "w2_01"],
    "f10": ["w1t_10", "w3t_10", "w2_10"], "f11": ["w1t_11", "w3t_11", "w2_11"],
    "ssm": ["w_int", "w_out"], "att": ["w_q", "w_o", "w_kv"],
    "f00_early": ["w2_00", "w1t_00"], "f00_late": ["w3t_00"],
}
FIRST_GATHER = "f00_up"
GATHER_PLAN = {"f00_upgate": ["f00_down", "ssm"], "ssm_in": ["f01"], "ssm_ssd": ["att", "f10"],
               "f01_upgate": ["f11"]}
SCATTER_A_PLAN = {"att_dwo": "f11", "kv_dknorm": "f10", "kv_du": "att", "f01_du": "f01", "ssm_du": "ssm",
                  "f00_dw3": "f00_early", "f00_du": "f00_late"}
SCATTER_B_PLAN = {"att_dcore": "f11", "f01_dw2": "att", "f01_dgate": "f10", "ssm_dssd": "f01", "f00_dgate": "ssm",
                  "f00_du": "f00_early"}
LAST_SCATTER = "f00_late"
SLOT_MAJOR = ("w_int",)


def _shard_shape(s, a):
    return s[:a] + (s[a] // N_DEV,) + s[a + 1:]


def _unshard_view(stack, shard_shape, axis):
    moved = jnp.moveaxis(stack, 0, axis)
    return moved.reshape(shard_shape[:axis] + (N_DEV * shard_shape[axis],) + shard_shape[axis + 1:])


def _small_local(arrs):
    flat = jnp.concatenate([arrs[n].reshape(-1) for n, _, _ in SMALL])
    return jnp.pad(flat, (0, SMALL_LOCAL_ROWS * LANES - flat.shape[0])).reshape(SMALL_LOCAL_ROWS, LANES)


def chip_partial(g4, ra, name):
    _, _, n, width = g4.shape

    def body(core_ref, g_ref, r_ref, o_ref):
        o_ref[0] = (g_ref[0, 0].astype(F32) + r_ref[0, 0].astype(F32)).astype(o_ref.dtype)

    grid_spec = pltpu.PrefetchScalarGridSpec(
        num_scalar_prefetch=1, grid=(N_CHIPS,),
        in_specs=[pl.BlockSpec((1, 1, n, width), lambda q, core: (q, core[0], 0, 0)),
                  pl.BlockSpec((1, 1, n, width), lambda q, core: (q, 0, 0, 0))],
        out_specs=pl.BlockSpec((1, n, width), lambda q, core: (q, 0, 0)))
    core = jnp.reshape(lax.axis_index("c"), (1,)).astype(jnp.int32)
    return pl.pallas_call(
        body, name=name, grid_spec=grid_spec, out_shape=jax.ShapeDtypeStruct((N_CHIPS, n, width), g4.dtype),
        compiler_params=pltpu.CompilerParams(dimension_semantics=("arbitrary",), vmem_limit_bytes=VMEM_LIMIT_BYTES),
    )(core, g4, ra)


class StepIO:
    def __init__(self, pieces):
        self.pieces = pieces
        self.full = {}
        self.grad = {}
        self.from_sibling = {}
        self.recv = {}

    def w(self, name):
        return self.full[name]

    def put(self, name, g):
        self.grad[name] = g

    def _by_chip_core(self, name):
        g = self.grad[name]
        return g.reshape((N_CHIPS, 2, g.shape[0] // N_DEV) + g.shape[1:])

    def gather_items(self, groups):
        names = [n for grp in groups for n in MAT_GROUPS[grp]]
        items = [("g2", self.pieces[n], None if n in SLOT_MAJOR else 0) for n in names]

        def sink(outs):
            for n, o in zip(names, outs):
                self.full[n] = o.reshape((-1,) + o.shape[2:]) if n in SLOT_MAJOR else o

        return items, sink

    def scatter_a_items(self, group):
        names = MAT_GROUPS[group]
        items = [("sa", self._by_chip_core(n), None) for n in names]

        def sink(outs):
            for n, o in zip(names, outs):
                self.from_sibling[n] = o

        return items, sink

    def scatter_b_items(self, group):
        names = MAT_GROUPS[group]
        items = [("sb", chip_partial(self._by_chip_core(n), self.from_sibling[n], "partial_" + n), None)
                 for n in names]

        def sink(outs):
            for n, o in zip(names, outs):
                self.recv[n] = o

        return items, sink

    def hook(self, site):
        parts = []
        if site in GATHER_PLAN:
            parts.append(self.gather_items(GATHER_PLAN[site]))
        if site in SCATTER_A_PLAN:
            parts.append(self.scatter_a_items(SCATTER_A_PLAN[site]))
        if site in SCATTER_B_PLAN:
            parts.append(self.scatter_b_items(SCATTER_B_PLAN[site]))
        if not parts:
            return None
        return combine_hooks(parts)


def combine_hooks(parts):
    items = [it for its, _ in parts for it in its]

    def sink(outs):
        p = 0
        for its, snk in parts:
            snk(outs[p:p + len(its)])
            p += len(its)

    return Comm(items), sink


def step(x, target, wts, ms, vs):
    me = _my_index()

    pieces = {}
    for li in range(2):
        for hi in range(2):
            tag = "%d%d" % (li, hi)
            pieces["w1t_" + tag] = wts["ffn_w1"][li, hi].T.astype(BF16)
            pieces["w3t_" + tag] = wts["ffn_w3"][li, hi].T.astype(BF16)
            pieces["w2_" + tag] = wts["ffn_w2"][li, hi].astype(BF16)
    pieces["w_int"] = wts["ssm_w_in"][0].T.astype(BF16)
    pieces["w_out"] = wts["ssm_w_out"][0].astype(BF16)
    pieces["w_kv"] = wts["w_kv"].astype(BF16)
    pieces["w_q"] = wts["w_q"][0].astype(BF16)
    pieces["w_o"] = wts["w_o"][0].astype(BF16)
    io = StepIO(pieces)

    small_sharded = [(n, s, a) for n, s, a in SMALL if a is not None]
    loc = jnp.concatenate([wts[n].reshape(-1) for n, _, _ in small_sharded])
    loc_rows = -(-loc.shape[0] // (8 * LANES)) * 8
    loc = jnp.pad(loc, (0, loc_rows * LANES - loc.shape[0])).reshape(loc_rows, LANES)
    got_small = []
    comm, sink = combine_hooks([io.gather_items([FIRST_GATHER]), ([("g", loc, None)], got_small.extend)])
    sink(comm_only(comm, "gather_first"))
    gath_small = got_small[0].reshape(N_DEV, -1)
    small = {}
    off = 0
    for n, s, a in small_sharded:
        shard = _shard_shape(s, a)
        cnt = int(np.prod(shard))
        small[n] = _unshard_view(gath_small[:, off:off + cnt].reshape((N_DEV,) + shard), shard, a)
        off += cnt
    for n, s, a in SMALL:
        if a is None:
            small[n] = wts[n]

    loss_part, grad_x, g_small_local = local_step(x[0], target[0], small, io)
    loss = lax.psum(loss_part, ("x", "y", "c"))

    small_flat = jnp.concatenate([g_small_local[n].reshape(-1) for n, _, _ in SMALL])
    small_buf = jnp.pad(small_flat, (0, SMALL_FULL_ROWS * SMALL_W - small_flat.shape[0]))
    small_buf = small_buf.reshape(SMALL_FULL_ROWS, SMALL_W)
    got_small = []
    comm, sink = combine_hooks([io.scatter_b_items(LAST_SCATTER), ([("g", small_buf, None)], got_small.extend)])
    sink(comm_only(comm, "exchange_last"))
    small_all = got_small[0]

    def sum_body(r_ref, o_ref):
        o_ref[...] = _slot_sum(r_ref)

    vmem = pl.BlockSpec(memory_space=pltpu.VMEM)
    small_sum, = pcall(sum_body, name="sum_small", grid=(), in_specs=[vmem], out_specs=[vmem],
                       out_shape=[jax.ShapeDtypeStruct((SMALL_FULL_ROWS, SMALL_W), F32)], args=[small_all])
    small_sum = small_sum.reshape(-1)
    g_small = {}
    off = 0
    for n, s, a in SMALL:
        cnt = int(np.prod(s))
        gfull = small_sum[off:off + cnt].reshape(s)
        off += cnt
        if a is None:
            g_small[n] = gfull
        else:
            width = s[a] // N_DEV
            g_small[n] = lax.dynamic_slice_in_dim(gfull, me * width, width, axis=a)

    out = {}

    def emit(name, res, shape):
        for kind, arr in zip(("grad", "delta", "new_m", "new_v"), res):
            out[kind + "_" + name] = arr.reshape(shape)

    for name, key in (("ffn_w1", "w1t_"), ("ffn_w3", "w3t_")):
        shp = wts[name].shape
        view = lambda t: t.reshape((4,) + shp[2:])
        res = adamw_cols([io.recv[key + tag] for tag in FFN_TAGS], view(wts[name]), view(ms[name]), view(vs[name]),
                         "adamw_" + name)
        emit(name, res, shp)
    shp = wts["ffn_w2"].shape
    view = lambda t: t.reshape((4,) + shp[2:])
    res = adamw_rows([io.recv["w2_" + tag] for tag in FFN_TAGS], view(wts["ffn_w2"]), view(ms["ffn_w2"]),
                     view(vs["ffn_w2"]), "adamw_ffn_w2")
    emit("ffn_w2", res, shp)
    res = adamw_cols([io.recv["w_int"]], wts["ssm_w_in"], ms["ssm_w_in"], vs["ssm_w_in"], "adamw_ssm_w_in")
    emit("ssm_w_in", res, wts["ssm_w_in"].shape)
    for name, key in (("ssm_w_out", "w_out"), ("w_kv", "w_kv"), ("w_q", "w_q"), ("w_o", "w_o")):
        shp = wts[name].shape
        view = lambda t: t.reshape((1,) + shp[-2:])
        res = adamw_rows([io.recv[key]], view(wts[name]), view(ms[name]), view(vs[name]), "adamw_" + name)
        emit(name, res, shp)

    res_s = rowmap(lambda gg, ww, mm_, vv: _adamw(gg, ww, mm_, vv),
                   [_small_local(g_small), _small_local(wts), _small_local(ms), _small_local(vs)], [],
                   [(LANES, F32)] * 3, tm=SMALL_LOCAL_ROWS, name="adamw_small")
    flat_s = [r.reshape(-1) for r in res_s]
    off = 0
    for n, s, a in SMALL:
        shard = s if a is None else _shard_shape(s, a)
        cnt = int(np.prod(shard))
        out["grad_" + n] = g_small[n]
        for kind, arr in zip(("delta", "new_m", "new_v"), flat_s):
            out[kind + "_" + n] = arr[off:off + cnt].reshape(shard)
        off += cnt
    out["loss"] = loss
    out["grad_x"] = grad_x[None]
    return out


def kernel(x, ffn_norm, ffn_w1, ffn_w3, ffn_w2, ssm_norm, ssm_w_in, ssm_conv_w, ssm_conv_b, ssm_dt_bias, ssm_a_log, ssm_d, ssm_gate_norm, ssm_w_out, kv_norm, w_kv, k_norm, attn_norm, w_q, q_norm, sinks, w_o, rel_bias, loss_target, m_ffn_norm, m_ffn_w1, m_ffn_w3, m_ffn_w2, m_ssm_norm, m_ssm_w_in, m_ssm_conv_w, m_ssm_conv_b, m_ssm_dt_bias, m_ssm_a_log, m_ssm_d, m_ssm_gate_norm, m_ssm_w_out, m_kv_norm, m_w_kv, m_k_norm, m_attn_norm, m_w_q, m_q_norm, m_sinks, m_w_o, m_rel_bias, v_ffn_norm, v_ffn_w1, v_ffn_w3, v_ffn_w2, v_ssm_norm, v_ssm_w_in, v_ssm_conv_w, v_ssm_conv_b, v_ssm_dt_bias, v_ssm_a_log, v_ssm_d, v_ssm_gate_norm, v_ssm_w_out, v_kv_norm, v_w_kv, v_k_norm, v_attn_norm, v_w_q, v_q_norm, v_sinks, v_w_o, v_rel_bias):
    args = locals()
    wts = {n: args[n] for n in WEIGHT_NAMES}
    ms = {n: args["m_" + n] for n in WEIGHT_NAMES}
    vs = {n: args["v_" + n] for n in WEIGHT_NAMES}
    out = step(x, loss_target, wts, ms, vs)
    result = [out["loss"], out["grad_x"]]
    for kind in ("grad", "delta", "new_m", "new_v"):
        result += [out[kind + "_" + n] for n in WEIGHT_NAMES]
    return tuple(result)
```

```python
import functools
import math
import operator

import numpy as np
import jax
import jax.numpy as jnp
from jax import lax
from jax.experimental import pallas as pl
from jax.experimental.pallas import tpu as pltpu

F32 = jnp.float32
BF16 = jnp.bfloat16

D_MODEL = 1024
D_FF = 2816
N_DEV = 8
SSM_D_INNER = 2048
SSM_HEAD_DIM = 64
SSM_HEADS = 32
SSM_GROUPS = 4
SSM_STATE = 128
SSM_CONV = 4
SSM_CHUNK = 256
SSM_CONV_DIM = SSM_D_INNER + 2 * SSM_GROUPS * SSM_STATE
SSM_IN_DIM = SSM_D_INNER + SSM_CONV_DIM + SSM_HEADS
ATT_HEAD_DIM = 64
ATT_HEADS = 16
ATT_KV_HEADS = 2
ATT_GROUP = 8
ATT_WINDOW = 128
REL_BUCKETS = 32
EPS = 1e-6
NEG = -1e30

ADAM_LR = 0.001
ADAM_B1 = 0.9
ADAM_B2 = 0.999
ADAM_EPS = 1e-08
ADAM_WD = 0.01
ADAM_STEP = 10

VMEM_LIMIT_BYTES = 52 * 1024 * 1024
LANES = 128
MESH_ID = pl.DeviceIdType.MESH
ANY_SPEC = pl.BlockSpec(memory_space=pl.ANY)

NT = (((1,), (1,)), ((), ()))
TN = (((0,), (0,)), ((), ()))
NN = (((1,), (0,)), ((), ()))


def _pick(dim, cands):
    for c in cands:
        if dim % c == 0:
            return c
    return dim


def _my_index():
    return 4 * lax.axis_index("x") + 2 * lax.axis_index("y") + lax.axis_index("c")


def _peer(k):
    x, y, c = lax.axis_index("x"), lax.axis_index("y"), lax.axis_index("c")
    px = 1 - x if (k >> 2) & 1 else x
    py = 1 - y if (k >> 1) & 1 else y
    pc = 1 - c if k & 1 else c
    return (px, py, pc), 4 * px + 2 * py + pc


def _piece(ref, axis, d, n):
    if axis is None:
        return ref.at[d]
    return ref.at[(slice(None),) * axis + (pl.ds(pl.multiple_of(d * n, 8), n),)]


SIBLING = 1
CHIP_PEERS = (4, 2, 6)
N_CHIPS = 4
SEMS_PER_ITEM = N_DEV - 1


def _my_chip():
    return 2 * lax.axis_index("x") + lax.axis_index("y")


class Comm:
    def __init__(self, items):
        self.items = list(items)

    def dst_shapes(self):
        out = []
        for kind, src, axis in self.items:
            s = tuple(src.shape)
            if kind == "g":
                shp = (N_DEV,) + s
            elif kind == "g2":
                shp = (N_DEV,) + s if axis is None else s[:axis] + (N_DEV * s[axis],) + s[axis + 1:]
            elif kind == "sa":
                shp = (s[0], 1) + s[2:]
            else:
                shp = s
            out.append(jax.ShapeDtypeStruct(shp, src.dtype))
        return out

    def scratch(self):
        n = len(self.items)
        return [pltpu.SemaphoreType.DMA((n * SEMS_PER_ITEM,)), pltpu.SemaphoreType.DMA((n * SEMS_PER_ITEM,)),
                pltpu.SemaphoreType.DMA((n,))]

    def _run(self, srcs, dsts, sems, starting):
        send_sems, recv_sems, local_sems = sems
        me = _my_index()
        core = lax.axis_index("c")
        chip = _my_chip()
        for i, (kind, src, axis) in enumerate(self.items):
            s_ref, d_ref = srcs[i], dsts[i]
            base = i * SEMS_PER_ITEM

            def rdma(src_ref, dst_ref, j, peer):
                return pltpu.make_async_remote_copy(
                    src_ref=src_ref, dst_ref=dst_ref, send_sem=send_sems.at[base + j], recv_sem=recv_sems.at[base + j],
                    device_id=peer, device_id_type=MESH_ID)

            if kind == "g":
                local = pltpu.make_async_copy(s_ref, d_ref.at[me], local_sems.at[i])
                outs = [rdma(s_ref, d_ref.at[me], k - 1, _peer(k)[0]) for k in range(1, N_DEV)]
                if starting:
                    local.start()
                    for cp in outs:
                        cp.start()
                else:
                    for k in range(1, N_DEV):
                        rdma(s_ref, d_ref.at[_peer(k)[1]], k - 1, _peer(k)[0]).wait_recv()
                    for cp in outs:
                        cp.wait_send()
                    local.wait()
            elif kind == "g2":
                n = None if axis is None else src.shape[axis]
                mine = _piece(d_ref, axis, me, n)
                sib = _peer(SIBLING)[0]
                local = pltpu.make_async_copy(s_ref, mine, local_sems.at[i])
                outs = [rdma(s_ref, mine, 0, sib)] + [rdma(s_ref, mine, 1 + j, _peer(k)[0])
                                                      for j, k in enumerate(CHIP_PEERS)]
                if starting:
                    local.start()
                    for cp in outs:
                        cp.start()
                else:
                    passed = []
                    for j, k in enumerate(CHIP_PEERS):
                        theirs = _piece(d_ref, axis, _peer(k)[1], n)
                        rdma(s_ref, theirs, 1 + j, _peer(k)[0]).wait_recv()
                        fwd = rdma(theirs, theirs, 4 + j, sib)
                        fwd.start()
                        passed.append(fwd)
                    rdma(s_ref, _piece(d_ref, axis, _peer(SIBLING)[1], n), 0, sib).wait_recv()
                    for j, k in enumerate(CHIP_PEERS):
                        rdma(s_ref, _piece(d_ref, axis, _peer(k ^ SIBLING)[1], n), 4 + j, sib).wait_recv()
                    for cp in outs + passed:
                        cp.wait_send()
                    local.wait()
            elif kind == "sa":
                cp = rdma(s_ref.at[(slice(None), pl.ds(1 - core, 1))], d_ref, 0, _peer(SIBLING)[0])
                if starting:
                    cp.start()
                else:
                    cp.wait_recv()
                    cp.wait_send()
            else:
                local = pltpu.make_async_copy(s_ref.at[chip], d_ref.at[chip], local_sems.at[i])
                outs = [rdma(s_ref.at[_peer(k)[1] >> 1], d_ref.at[chip], 1 + j, _peer(k)[0])
                        for j, k in enumerate(CHIP_PEERS)]
                if starting:
                    local.start()
                    for cp in outs:
                        cp.start()
                else:
                    for j, k in enumerate(CHIP_PEERS):
                        rdma(s_ref.at[chip], d_ref.at[_peer(k)[1] >> 1], 1 + j, _peer(k)[0]).wait_recv()
                    for cp in outs:
                        cp.wait_send()
                    local.wait()

    def start(self, srcs, dsts, sems):
        self._run(srcs, dsts, sems, True)

    def wait(self, srcs, dsts, sems):
        self._run(srcs, dsts, sems, False)


def pcall(body, *, name, grid, in_specs, out_specs, out_shape, args, scratch=(), hook=None):
    cparams = pltpu.CompilerParams(dimension_semantics=("arbitrary",) * len(grid), vmem_limit_bytes=VMEM_LIMIT_BYTES)
    if hook is None:
        outs = pl.pallas_call(body, name=name, grid=grid, in_specs=list(in_specs), out_specs=list(out_specs),
                              out_shape=list(out_shape), scratch_shapes=list(scratch), compiler_params=cparams)(*args)
        return list(outs)
    comm, sink = hook
    n_in, n_out, n_scr, n_it = len(args), len(out_shape), len(scratch), len(comm.items)
    dims = tuple(grid)

    def wrapped(*refs):
        p = 0
        ins = refs[p:p + n_in]
        p += n_in
        csrc = refs[p:p + n_it]
        p += n_it
        outs = refs[p:p + n_out]
        p += n_out
        cdst = refs[p:p + n_it]
        p += n_it
        scr = refs[p:p + n_scr]
        p += n_scr
        sems = refs[p:p + 3]
        if dims:
            ids = [pl.program_id(a) for a in range(len(dims))]
            first = functools.reduce(operator.and_, [i == 0 for i in ids])
            last = functools.reduce(operator.and_, [i == d - 1 for i, d in zip(ids, dims)])

            @pl.when(first)
            def _():
                comm.start(csrc, cdst, sems)

            body(*ins, *outs, *scr)

            @pl.when(last)
            def _():
                comm.wait(csrc, cdst, sems)
        else:
            comm.start(csrc, cdst, sems)
            body(*ins, *outs, *scr)
            comm.wait(csrc, cdst, sems)

    res = pl.pallas_call(
        wrapped, name=name, grid=grid,
        in_specs=list(in_specs) + [ANY_SPEC] * n_it, out_specs=list(out_specs) + [ANY_SPEC] * n_it,
        out_shape=list(out_shape) + comm.dst_shapes(), scratch_shapes=list(scratch) + comm.scratch(),
        compiler_params=cparams,
    )(*args, *[src for _, src, _ in comm.items])
    res = list(res)
    sink(res[n_out:])
    return res[:n_out]


def comm_only(comm, name):
    got = []
    pcall(lambda *refs: None, name=name, grid=(), in_specs=[], out_specs=[], out_shape=[], args=[],
          hook=(comm, got.extend))
    return got


def mm(a, b, *, ta=False, tb=False, out_dtype=F32, res=None, alpha=1.0, name, hook=None):
    if ta:
        k_dim, m_dim = a.shape
    else:
        m_dim, k_dim = a.shape
    if tb:
        n_dim, k2 = b.shape
    else:
        k2, n_dim = b.shape
    assert k_dim == k2, (a.shape, b.shape, ta, tb)
    tn = _pick(n_dim, (1024, 1408, 512, 256, 128))
    tm = _pick(m_dim, (1024, 1408, 512, 256, 128)) if tn <= 1024 else _pick(m_dim, (512, 256, 128))
    tk = _pick(k_dim, (1024, 512, 256, 128)) if ta else _pick(k_dim, (512, 1408, 256, 128))
    nk = k_dim // tk
    has_res = res is not None
    dn = (((0 if ta else 1,), (1 if tb else 0,)), ((), ()))

    def body(*refs):
        if has_res:
            a_ref, b_ref, r_ref, o_ref, acc_ref = refs
        else:
            a_ref, b_ref, o_ref, acc_ref = refs
        k = pl.program_id(2)

        @pl.when(k == 0)
        def _():
            acc_ref[...] = jnp.zeros_like(acc_ref)

        acc_ref[...] += lax.dot_general(a_ref[...].astype(BF16), b_ref[...].astype(BF16), dn,
                                        preferred_element_type=F32)

        @pl.when(k == nk - 1)
        def _():
            r = acc_ref[...]
            if alpha != 1.0:
                r = r * alpha
            if has_res:
                r = r_ref[...] + r
            o_ref[...] = r.astype(o_ref.dtype)

    a_spec = pl.BlockSpec((tk, tm), lambda i, j, k: (k, i)) if ta else pl.BlockSpec((tm, tk), lambda i, j, k: (i, k))
    b_spec = pl.BlockSpec((tn, tk), lambda i, j, k: (j, k)) if tb else pl.BlockSpec((tk, tn), lambda i, j, k: (k, j))
    o_spec = pl.BlockSpec((tm, tn), lambda i, j, k: (i, j))
    in_specs = [a_spec, b_spec] + ([o_spec] if has_res else [])
    args = [a, b] + ([res] if has_res else [])
    out, = pcall(body, name=name, grid=(m_dim // tm, n_dim // tn, nk), in_specs=in_specs, out_specs=[o_spec],
                 out_shape=[jax.ShapeDtypeStruct((m_dim, n_dim), out_dtype)], args=args,
                 scratch=[pltpu.VMEM((tm, tn), F32)], hook=hook)
    return out


def rowmap(fn, rows, consts=(), out_rows=(), out_accs=(), *, tm, name, hook=None):
    first = rows[0][0] if isinstance(rows[0], tuple) else rows[0]
    t_dim = first.shape[0]
    assert t_dim % tm == 0, (t_dim, tm)
    n_r, n_c, n_o = len(rows), len(consts), len(out_rows)

    def body(*refs):
        ins = [r[...] for r in refs[:n_r + n_c]]
        o_refs = refs[n_r + n_c:]
        outs = tuple(fn(*ins))
        for o_ref, val in zip(o_refs[:n_o], outs[:n_o]):
            o_ref[...] = val.astype(o_ref.dtype)
        if out_accs:
            @pl.when(pl.program_id(0) == 0)
            def _():
                for o_ref in o_refs[n_o:]:
                    o_ref[...] = jnp.zeros_like(o_ref)

            for o_ref, val in zip(o_refs[n_o:], outs[n_o:]):
                o_ref[...] += val

    in_specs, args = [], []
    for r in rows:
        if isinstance(r, tuple):
            args.append(r[0])
            in_specs.append(r[1])
        else:
            args.append(r)
            in_specs.append(pl.BlockSpec((tm, r.shape[1]), lambda i: (i, 0)))
    for c in consts:
        args.append(c)
        in_specs.append(pl.BlockSpec(c.shape, lambda i, nd=c.ndim: (0,) * nd))
    out_specs = [pl.BlockSpec((tm, w), lambda i: (i, 0)) for (w, _) in out_rows]
    out_specs += [pl.BlockSpec(s, lambda i, nd=len(s): (0,) * nd) for s in out_accs]
    out_shape = [jax.ShapeDtypeStruct((t_dim, w), dt) for (w, dt) in out_rows]
    out_shape += [jax.ShapeDtypeStruct(s, F32) for s in out_accs]
    return pcall(body, name=name, grid=(t_dim // tm,), in_specs=in_specs, out_specs=out_specs, out_shape=out_shape,
                 args=args, hook=hook)


def _rms_fwd(x, g):
    r = lax.rsqrt(jnp.mean(x * x, axis=-1, keepdims=True) + EPS)
    return x * r * g


def _rms_bwd(x, g, dy):
    r = lax.rsqrt(jnp.mean(x * x, axis=-1, keepdims=True) + EPS)
    xh = x * r
    dg = jnp.sum(dy * xh, axis=0, keepdims=True)
    dxh = dy * g
    dx = r * (dxh - xh * jnp.mean(dxh * xh, axis=-1, keepdims=True))
    return dx, dg


def _sigmoid(x):
    return 1.0 / (1.0 + jnp.exp(-x))


def _silu(x):
    return x * _sigmoid(x)


def _silu_grad(x):
    s = _sigmoid(x)
    return s * (1.0 + x * (1.0 - s))


def _split3(x):
    hi = x.astype(BF16)
    r1 = x - hi.astype(F32)
    mid = r1.astype(BF16)
    lo = (r1 - mid.astype(F32)).astype(BF16)
    return hi, mid, lo


def _dot(a, b, dn=NN):
    return lax.dot_general(a.astype(BF16), b.astype(BF16), dn, preferred_element_type=F32)


FFN_TN = 1408
RESIDENT_TM = 512


def ffn_upgate(h, g, w1t, w3t, nm, hook=None):
    t_dim = h.shape[0]
    tm = _pick(t_dim, (512, 256, 128))
    tn = FFN_TN

    n_j = D_FF // tn
    u_w = D_MODEL // n_j

    def body(h_ref, g_ref, w1_ref, w3_ref, u_ref, a_ref, b_ref, hm_ref):
        uu = _rms_fwd(h_ref[...], g_ref[...]).astype(BF16)
        for j in range(n_j):
            @pl.when(pl.program_id(0) == j)
            def _(j=j):
                u_ref[...] = uu[:, j * u_w:(j + 1) * u_w]

        a = lax.dot_general(uu, w1_ref[...], NT, preferred_element_type=F32)
        b = lax.dot_general(uu, w3_ref[...], NT, preferred_element_type=F32)
        a_ref[...] = a.astype(a_ref.dtype)
        b_ref[...] = b.astype(b_ref.dtype)
        hm_ref[...] = (_silu(a) * b).astype(hm_ref.dtype)

    row_spec = pl.BlockSpec((tm, D_MODEL), lambda j, i: (i, 0))
    w_spec = pl.BlockSpec((tn, D_MODEL), lambda j, i: (j, 0))
    o_spec = pl.BlockSpec((tm, tn), lambda j, i: (i, j))
    o_shape = jax.ShapeDtypeStruct((t_dim, D_FF), BF16)
    return pcall(body, name=nm, grid=(D_FF // tn, t_dim // tm),
                 in_specs=[row_spec, pl.BlockSpec((1, D_MODEL), lambda j, i: (0, 0)), w_spec, w_spec],
                 out_specs=[pl.BlockSpec((tm, u_w), lambda j, i: (i, j))] + [o_spec] * 3,
                 out_shape=[jax.ShapeDtypeStruct((t_dim, D_MODEL), BF16)] + [o_shape] * 3,
                 args=[h, g, w1t, w3t], hook=hook)


def ffn_dgate(dout_bf, w2, a, b, nm, hook=None):
    t_dim = dout_bf.shape[0]
    tm = _pick(t_dim, (512, 256, 128))
    tn = FFN_TN

    def body(d_ref, w2_ref, a_ref, b_ref, da_ref, db_ref):
        dhm = 0.5 * lax.dot_general(d_ref[...], w2_ref[...], NT, preferred_element_type=F32)
        av = a_ref[...].astype(F32)
        bv = b_ref[...].astype(F32)
        sg = _sigmoid(av)
        da_ref[...] = (dhm * bv * (sg * (1.0 + av * (1.0 - sg)))).astype(da_ref.dtype)
        db_ref[...] = (dhm * (av * sg)).astype(db_ref.dtype)

    t_spec = pl.BlockSpec((tm, tn), lambda j, i: (i, j))
    o_shape = jax.ShapeDtypeStruct((t_dim, D_FF), BF16)
    return pcall(body, name=nm, grid=(D_FF // tn, t_dim // tm),
                 in_specs=[pl.BlockSpec((tm, D_MODEL), lambda j, i: (i, 0)),
                           pl.BlockSpec((tn, D_MODEL), lambda j, i: (j, 0)), t_spec, t_spec],
                 out_specs=[t_spec] * 2, out_shape=[o_shape] * 2, args=[dout_bf, w2, a, b], hook=hook)


def ffn_fwd(h, g, tag, io, target=None):
    nm = "f" + tag
    u, a, b, hm = ffn_upgate(h, g, io.w("w1t_" + tag), io.w("w3t_" + tag), nm + "_upgate",
                             hook=io.hook(nm + "_upgate"))
    if target is None:
        return mm(hm, io.w("w2_" + tag), res=h, alpha=0.5, name=nm + "_down"), (u, a, b, hm)

    def down_loss(hmv, hv, t, w2):
        e = hv + 0.5 * _dot(hmv, w2) - t
        d = e * (1.0 / D_MODEL)
        return d, d, jnp.sum(e * e, axis=0, keepdims=True)

    res = rowmap(down_loss, [hm, h, target], [io.w("w2_" + tag)], [(D_MODEL, F32), (D_MODEL, BF16)],
                 [(1, D_MODEL)], tm=RESIDENT_TM, name=nm + "_down_loss")
    return res, (u, a, b, hm)


def du_norm_bwd(pairs, h, g, dout, nm, hook=None):
    t_dim = h.shape[0]
    tm = RESIDENT_TM
    n_p = len(pairs)

    def body(*refs):
        h_ref, d_ref, g_ref = refs[2 * n_p:2 * n_p + 3]
        dh_ref, dhb_ref, dg_ref = refs[2 * n_p + 3:]
        du = None
        for p, (_, _, tb) in enumerate(pairs):
            t = lax.dot_general(refs[2 * p][...].astype(BF16), refs[2 * p + 1][...].astype(BF16), NT if tb else NN,
                                preferred_element_type=F32)
            du = t if du is None else du + t
        dx, dg = _rms_bwd(h_ref[...], g_ref[...], du)
        dh = d_ref[...] + dx
        dh_ref[...] = dh
        dhb_ref[...] = dh.astype(dhb_ref.dtype)

        @pl.when(pl.program_id(0) == 0)
        def _():
            dg_ref[...] = jnp.zeros_like(dg_ref)

        dg_ref[...] += dg

    in_specs, args = [], []
    for a, b, _ in pairs:
        in_specs += [pl.BlockSpec((tm, a.shape[1]), lambda i: (i, 0)), pl.BlockSpec(b.shape, lambda i: (0, 0))]
        args += [a, b]
    row_spec = pl.BlockSpec((tm, D_MODEL), lambda i: (i, 0))
    vec_spec = pl.BlockSpec((1, D_MODEL), lambda i: (0, 0))
    return pcall(body, name=nm, grid=(t_dim // tm,), in_specs=in_specs + [row_spec, row_spec, vec_spec],
                 out_specs=[row_spec, row_spec, vec_spec],
                 out_shape=[jax.ShapeDtypeStruct((t_dim, D_MODEL), F32), jax.ShapeDtypeStruct((t_dim, D_MODEL), BF16),
                            jax.ShapeDtypeStruct((1, D_MODEL), F32)],
                 args=args + [h, dout, g], hook=hook)


def ffn_bwd(h, g, tag, saved, dout, dout_bf, io):
    nm = "f" + tag
    w1t, w3t, w2 = io.w("w1t_" + tag), io.w("w3t_" + tag), io.w("w2_" + tag)
    u, a, b, hm = saved
    io.put("w2_" + tag, mm(hm, dout_bf, ta=True, alpha=0.5, out_dtype=BF16, name=nm + "_dw2",
                           hook=io.hook(nm + "_dw2")))
    da, db = ffn_dgate(dout_bf, w2, a, b, nm + "_dgate", hook=io.hook(nm + "_dgate"))
    io.put("w1t_" + tag, mm(da, u, ta=True, out_dtype=BF16, name=nm + "_dw1"))
    io.put("w3t_" + tag, mm(db, u, ta=True, out_dtype=BF16, name=nm + "_dw3", hook=io.hook(nm + "_dw3")))
    return du_norm_bwd([(da, w1t, False), (db, w3t, False)], h, g, dout, nm + "_du", hook=io.hook(nm + "_du"))


def conv_input_grad(d_parts, w, nm):
    tm = 256
    t_dim = d_parts[0].shape[0]
    n_tiles = t_dim // tm

    def fn(d1, n1, d2, n2, d3, n3, ww):
        d = jnp.concatenate([d1, d2, d3], axis=1)
        nxt = jnp.concatenate([n1, n2, n3], axis=1)
        nxt = jnp.where(pl.program_id(0) < n_tiles - 1, nxt, 0.0)
        dd = jnp.concatenate([d, nxt], axis=0)
        out = dd[3:3 + tm] * ww[0:1]
        for k in range(1, SSM_CONV):
            out = out + dd[3 - k:3 - k + tm] * ww[k:k + 1]
        return (out,)

    rows = []
    for d in d_parts:
        below = pl.BlockSpec((8, d.shape[1]), lambda i: (jnp.minimum((i + 1) * (tm // 8), t_dim // 8 - 1), 0))
        rows += [d, (d, below)]
    dx, = rowmap(fn, rows, [w], [(SSM_CONV_DIM, BF16)], tm=tm, name=nm)
    return dx


GRP_W = SSM_D_INNER // SSM_GROUPS
HPG = SSM_HEADS // SSM_GROUPS
HEAD_SHIFT = 6


def _split2(x):
    hi = x.astype(BF16)
    return hi, (x - hi.astype(F32)).astype(BF16)


def _expand_mats():
    e = ((lax.broadcasted_iota(jnp.int32, (HPG, GRP_W), 1) >> HEAD_SHIFT)
         == lax.broadcasted_iota(jnp.int32, (HPG, GRP_W), 0)).astype(BF16)
    et = ((lax.broadcasted_iota(jnp.int32, (GRP_W, HPG), 0) >> HEAD_SHIFT)
          == lax.broadcasted_iota(jnp.int32, (GRP_W, HPG), 1)).astype(BF16)
    return e, et


def _expand(v, e_m):
    hi, lo = _split2(v)
    return jnp.dot(hi, e_m, preferred_element_type=F32) + jnp.dot(lo, e_m, preferred_element_type=F32)


def _reduce8(v, et_m):
    hi, lo = _split2(v)
    return jnp.dot(hi, et_m, preferred_element_type=F32) + jnp.dot(lo, et_m, preferred_element_type=F32)


def _ssd_group_terms(dt_ref, dtT_ref, arow_ref, acol_ref):
    L = SSM_CHUNK
    r = lax.broadcasted_iota(jnp.int32, (L, L), 0)
    c = lax.broadcasted_iota(jnp.int32, (L, L), 1)
    tril = (r >= c).astype(BF16)
    triu = (r <= c).astype(BF16)
    dtg = dt_ref[0]
    acol = None
    for p in _split3(dtg * arow_ref[0]):
        t = jnp.dot(tril, p, preferred_element_type=F32)
        acol = t if acol is None else acol + t
    arowT = None
    for p in _split3(dtT_ref[0] * acol_ref[0]):
        t = jnp.dot(p, triu, preferred_element_type=F32)
        arowT = t if arowT is None else arowT + t
    return dtg, acol, arowT, r >= c


def _state_decay(a_last_col, et_m):
    hi, lo = _split2(jnp.broadcast_to(jnp.exp(a_last_col), (HPG, SSM_STATE)))
    return jnp.dot(et_m, hi, preferred_element_type=F32) + jnp.dot(et_m, lo, preferred_element_type=F32)


def _conv_block(x_ref, halo_ref, w_ref, b_ref, first):
    L = SSM_CHUNK
    xx = jnp.concatenate([jnp.where(first, 0.0, halo_ref[...]), x_ref[...]], axis=0)
    w = w_ref[...]
    shifted = [pltpu.roll(xx, SSM_CONV - 1 - k, 0)[8:8 + L] if k < SSM_CONV - 1 else xx[8:8 + L]
               for k in range(SSM_CONV)]
    acc = b_ref[...] + shifted[0] * w[0:1]
    for k in range(1, SSM_CONV):
        acc = acc + shifted[k] * w[k:k + 1]
    return acc, shifted


def _ssd_specs(nc, rev):
    L, N = SSM_CHUNK, SSM_STATE
    xcols = SSM_D_INNER // LANES
    ch = (lambda c: nc - 1 - c) if rev else (lambda c: c)
    above = lambda c: jnp.maximum(ch(c) * (L // 8) - 1, 0)
    specs = []
    for width, col in ((GRP_W, lambda g: g), (N, lambda g: xcols + g), (N, lambda g: xcols + SSM_GROUPS + g)):
        specs += [
            pl.BlockSpec((L, width), lambda c, g, col=col: (ch(c), col(g))),
            pl.BlockSpec((8, width), lambda c, g, col=col: (above(c), col(g))),
            pl.BlockSpec((SSM_CONV, width), lambda c, g, col=col: (0, col(g))),
            pl.BlockSpec((1, width), lambda c, g, col=col: (0, col(g))),
        ]
    return specs + [
        pl.BlockSpec((1, L, HPG), lambda c, g: (g, ch(c), 0)),
        pl.BlockSpec((1, HPG, L), lambda c, g: (g, 0, ch(c))),
        pl.BlockSpec((1, 1, HPG), lambda c, g: (g, 0, 0)),
        pl.BlockSpec((1, HPG, 1), lambda c, g: (g, 0, 0)),
        pl.BlockSpec((1, GRP_W), lambda c, g: (0, g)),
    ]


def ssd_fwd(xbc_raw, conv_w, conv_b, dt_g, dtT_g, a_row, a_col, dvec, nm, hook=None):
    t_dim = xbc_raw.shape[0]
    L, P, N = SSM_CHUNK, SSM_HEAD_DIM, SSM_STATE
    nc = t_dim // L

    def body(x_ref, xh_ref, xw_ref, xb_ref, b_ref, bh_ref, bw_ref, bb_ref, c_ref, ch_ref, cw_ref, cb_ref,
             dt_ref, dtT_ref, arow_ref, acol_ref, dvec_ref, y_ref, st_ref, s_s):
        ci = pl.program_id(0)
        g = pl.program_id(1)

        @pl.when((ci == 0) & (g == 0))
        def _():
            s_s[...] = jnp.zeros_like(s_s)

        e_m, et_m = _expand_mats()
        dtg, acol, arowT, causal = _ssd_group_terms(dt_ref, dtT_ref, arow_ref, acol_ref)
        a_last_row = acol[L - 1:L, :]
        x = _silu(_conv_block(x_ref, xh_ref, xw_ref, xb_ref, ci == 0)[0])
        bm = _silu(_conv_block(b_ref, bh_ref, bw_ref, bb_ref, ci == 0)[0])
        cm = _silu(_conv_block(c_ref, ch_ref, cw_ref, cb_ref, ci == 0)[0])
        cb = _dot(cm, bm, NT)
        s = s_s[g]
        st_ref[0, 0] = s
        ea_x = _expand(jnp.exp(acol), e_m)
        dt_x = _expand(dtg, e_m)
        w_x = _expand(jnp.exp(a_last_row - acol) * dtg, e_m)
        yb = ea_x * _dot(cm, s, NT) + dvec_ref[...] * x
        xd = (x * dt_x).astype(BF16)
        for e in range(HPG):
            sl = slice(e * P, (e + 1) * P)
            lm = jnp.exp(jnp.where(causal, acol[:, e:e + 1] - arowT[e:e + 1, :], NEG))
            m = (cb * lm).astype(BF16)
            y_ref[:, sl] = yb[:, sl] + jnp.dot(m, xd[:, sl], preferred_element_type=F32)
        s_s[g] = _state_decay(arowT[:, L - 1:L], et_m) * s + _dot(x * w_x, bm, TN)

    out_specs = [
        pl.BlockSpec((L, GRP_W), lambda c, g: (c, g)),
        pl.BlockSpec((1, 1, GRP_W, N), lambda c, g: (c, g, 0, 0)),
    ]
    return pcall(
        body, name=nm, grid=(nc, SSM_GROUPS), in_specs=_ssd_specs(nc, False), out_specs=out_specs,
        out_shape=[jax.ShapeDtypeStruct((t_dim, SSM_D_INNER), F32),
                   jax.ShapeDtypeStruct((nc, SSM_GROUPS, GRP_W, N), F32)],
        scratch=[pltpu.VMEM((SSM_GROUPS, GRP_W, N), F32)],
        args=[xbc_raw, xbc_raw, conv_w, conv_b] * 3 + [dt_g, dtT_g, a_row, a_col, dvec], hook=hook)


def ssd_bwd(dy, xbc_raw, conv_w, conv_b, dt_g, dtT_g, a_row, a_col, dvec, states, nm, hook=None):
    t_dim = xbc_raw.shape[0]
    L, P, N = SSM_CHUNK, SSM_HEAD_DIM, SSM_STATE
    nc = t_dim // L

    def body(dy_ref, x_ref, xh_ref, xw_ref, xb_ref, b_ref, bh_ref, bw_ref, bb_ref, c_ref, ch_ref, cw_ref, cb_ref,
             dt_ref, dtT_ref, arow_ref, acol_ref, dvec_ref, st_ref,
             dx_ref, db_ref, dc_ref, da_ref, ddt_ref, dd_ref, dwx_ref, dwb_ref, dwc_ref, dbx_ref, dbb_ref, dbc_ref,
             ds_s, yd_s, dxd_s):
        ci = pl.program_id(0)
        g = pl.program_id(1)

        @pl.when((ci == 0) & (g == 0))
        def _():
            ds_s[...] = jnp.zeros_like(ds_s)
            for r in (dd_ref, dwx_ref, dwb_ref, dwc_ref, dbx_ref, dbb_ref, dbc_ref):
                r[...] = jnp.zeros_like(r)

        e_m, et_m = _expand_mats()
        dtg, acol, arowT, causal = _ssd_group_terms(dt_ref, dtT_ref, arow_ref, acol_ref)
        a_last_row = acol[L - 1:L, :]
        first = ci == nc - 1
        pre_x, sh_x = _conv_block(x_ref, xh_ref, xw_ref, xb_ref, first)
        pre_b, sh_b = _conv_block(b_ref, bh_ref, bw_ref, bb_ref, first)
        pre_c, sh_c = _conv_block(c_ref, ch_ref, cw_ref, cb_ref, first)
        sg_x, sg_b, sg_c = _sigmoid(pre_x), _sigmoid(pre_b), _sigmoid(pre_c)
        x = pre_x * sg_x
        dy = dy_ref[...]
        bm = pre_b * sg_b
        cm = pre_c * sg_c
        cb = _dot(cm, bm, NT)
        s = st_ref[0, 0]
        dsp = ds_s[g]
        ew8 = jnp.exp(a_last_row - acol)
        ea_x = _expand(jnp.exp(acol), e_m)
        dt_x = _expand(dtg, e_m)
        ew_x = _expand(ew8, e_m)
        w_x = ew_x * dt_x
        z = _dot(cm, s, NT)
        dz = ea_x * dy
        dc = _dot(dz, s)
        ds_y = _dot(dz, cm, TN)
        du = _dot(bm, dsp, NT)
        u = x * w_x
        db = _dot(u, dsp)
        xd = (x * dt_x).astype(BF16)
        dyb = dy.astype(BF16)
        dcb = jnp.zeros((L, L), F32)
        for e in range(HPG):
            sl = slice(e * P, (e + 1) * P)
            lm = jnp.exp(jnp.where(causal, acol[:, e:e + 1] - arowT[e:e + 1, :], NEG))
            m = (cb * lm).astype(BF16)
            yd_s[:, sl] = jnp.dot(m, xd[:, sl], preferred_element_type=F32)
            dxd_s[:, sl] = lax.dot_general(m, dyb[:, sl], TN, preferred_element_type=F32)
            dcb = dcb + lax.dot_general(dyb[:, sl], xd[:, sl], NT, preferred_element_type=F32) * lm
        dxd = dxd_s[...]

        def through_conv(d_act, pre, sg, shifted, d_ref, dw_ref, dbias_ref):
            d_pre = d_act * (sg * (1.0 + pre * (1.0 - sg)))
            d_ref[...] = d_pre
            dw_ref[g] += jnp.concatenate([jnp.sum(d_pre * sh, axis=0, keepdims=True) for sh in shifted], axis=0)
            dbias_ref[g] += jnp.sum(d_pre, axis=0, keepdims=True)

        through_conv(dvec_ref[...] * dy + du * w_x + dt_x * dxd, pre_x, sg_x, sh_x, dx_ref, dwx_ref, dbx_ref)
        ddt = _reduce8(x * (ew_x * du + dxd), et_m)
        da = (_reduce8(dz * z + dyb.astype(F32) * yd_s[...], et_m)
              - _reduce8(xd.astype(F32) * dxd + du * u, et_m))
        dwa_row = _reduce8(jnp.broadcast_to(jnp.sum(du * u, axis=0, keepdims=True), (8, GRP_W)), et_m)[0:1]
        t_nh = None
        for p in _split3(dsp * s):
            t = lax.dot_general(p, et_m, TN, preferred_element_type=F32)
            t_nh = t if t_nh is None else t_nh + t
        d_last = dwa_row + jnp.exp(a_last_row) * jnp.sum(t_nh, axis=0, keepdims=True)
        row_l = lax.broadcasted_iota(jnp.int32, (L, 1), 0)
        da_ref[0] = da + jnp.where(row_l == L - 1, d_last, 0.0)
        ddt_ref[0] = ddt
        dd_ref[g] += jnp.sum(dy * x, axis=0, keepdims=True)
        through_conv(dc + _dot(dcb, bm), pre_c, sg_c, sh_c, dc_ref, dwc_ref, dbc_ref)
        through_conv(db + _dot(dcb, cm, TN), pre_b, sg_b, sh_b, db_ref, dwb_ref, dbb_ref)
        ds_s[g] = _state_decay(arowT[:, L - 1:L], et_m) * dsp + ds_y

    rc = lambda c: nc - 1 - c
    in_specs = ([pl.BlockSpec((L, GRP_W), lambda c, g: (rc(c), g))] + _ssd_specs(nc, True)
                + [pl.BlockSpec((1, 1, GRP_W, N), lambda c, g: (rc(c), g, 0, 0))])
    whole = lambda *shape: pl.BlockSpec(shape, lambda c, g: (0,) * len(shape))
    out_specs = [
        pl.BlockSpec((L, GRP_W), lambda c, g: (rc(c), g)),
        pl.BlockSpec((L, N), lambda c, g: (rc(c), g)),
        pl.BlockSpec((L, N), lambda c, g: (rc(c), g)),
        pl.BlockSpec((1, L, HPG), lambda c, g: (g, rc(c), 0)),
        pl.BlockSpec((1, L, HPG), lambda c, g: (g, rc(c), 0)),
        whole(SSM_GROUPS, 1, GRP_W),
        whole(SSM_GROUPS, SSM_CONV, GRP_W), whole(SSM_GROUPS, SSM_CONV, N), whole(SSM_GROUPS, SSM_CONV, N),
        whole(SSM_GROUPS, 1, GRP_W), whole(SSM_GROUPS, 1, N), whole(SSM_GROUPS, 1, N),
    ]
    gn = SSM_GROUPS * N
    acc = lambda *shape: jax.ShapeDtypeStruct(shape, F32)
    out_shape = [
        acc(t_dim, SSM_D_INNER), acc(t_dim, gn), acc(t_dim, gn), acc(SSM_GROUPS, t_dim, HPG),
        acc(SSM_GROUPS, t_dim, HPG), acc(SSM_GROUPS, 1, GRP_W),
        acc(SSM_GROUPS, SSM_CONV, GRP_W), acc(SSM_GROUPS, SSM_CONV, N), acc(SSM_GROUPS, SSM_CONV, N),
        acc(SSM_GROUPS, 1, GRP_W), acc(SSM_GROUPS, 1, N), acc(SSM_GROUPS, 1, N),
    ]
    return pcall(
        body, name=nm, grid=(nc, SSM_GROUPS), in_specs=in_specs, out_specs=out_specs, out_shape=out_shape,
        scratch=[pltpu.VMEM((SSM_GROUPS, GRP_W, N), F32), pltpu.VMEM((L, GRP_W), F32), pltpu.VMEM((L, GRP_W), F32)],
        args=[dy] + [xbc_raw, xbc_raw, conv_w, conv_b] * 3 + [dt_g, dtT_g, a_row, a_col, dvec, states], hook=hook)


def _softplus(x):
    return jnp.maximum(x, 0.0) + jnp.log(1.0 + jnp.exp(-jnp.abs(x)))


def ssd_dt_bwd(da, ddt, dt, dt_raw, a_row, dt_bias, nm):
    L = SSM_CHUNK

    def fn(d_a, d_dt, dtv, raw, ar, bias):
        r = lax.broadcasted_iota(jnp.int32, (L, L), 0)
        c = lax.broadcasted_iota(jnp.int32, (L, L), 1)
        triu = (r <= c).astype(BF16)
        acc = None
        for p in _split3(d_a):
            t = jnp.dot(triu, p, preferred_element_type=F32)
            acc = t if acc is None else acc + t
        d_dt = d_dt + acc * ar
        d_a_h = jnp.sum(acc * dtv, axis=0, keepdims=True)
        d_raw = d_dt * _sigmoid(raw + bias)
        return d_raw, d_a_h, jnp.sum(d_raw, axis=0, keepdims=True)

    return rowmap(fn, [da, ddt, dt, dt_raw], [a_row, dt_bias], [(SSM_HEADS, BF16)],
                  [(1, SSM_HEADS), (1, SSM_HEADS)], tm=L, name=nm)


GN_W = SSM_D_INNER // SSM_GROUPS


def mamba_fwd(h, p, nm, io):
    def in_proj(x, gg, w_zt, w_xbct, w_dtt):
        uu = _rms_fwd(x, gg).astype(BF16)
        return uu, _dot(uu, w_zt, NT), _dot(uu, w_xbct, NT), _dot(uu, w_dtt, NT)

    u, z, xbc_raw, dt_raw = rowmap(in_proj, [h], [p["ssm_norm"], p["w_zt"], p["w_xbct"], p["w_dtt"]],
                                   [(D_MODEL, BF16), (SSM_D_INNER, F32), (SSM_CONV_DIM, F32), (SSM_HEADS, F32)],
                                   tm=RESIDENT_TM, name=nm + "_in", hook=io.hook(nm + "_in"))
    dt, = rowmap(lambda r, b: (_softplus(r + b),), [dt_raw], [p["dt_bias"]], [(SSM_HEADS, F32)], tm=256,
                 name=nm + "_softplus")
    dt_g = dt.reshape(-1, SSM_GROUPS, HPG).transpose(1, 0, 2)
    dtT_g = dt_g.transpose(0, 2, 1)
    y, states = ssd_fwd(xbc_raw, p["conv_w"], p["conv_b"], dt_g, dtT_g, p["a_row"], p["a_col"], p["dvec"],
                        nm + "_ssd", hook=io.hook(nm + "_ssd"))

    def gate_norm_out(yv, zv, hv, gg, w_out):
        t = yv * _silu(zv)
        yn = jnp.concatenate([_rms_fwd(t[:, k * GN_W:(k + 1) * GN_W], gg[:, k * GN_W:(k + 1) * GN_W])
                              for k in range(SSM_GROUPS)], axis=1).astype(BF16)
        return yn, hv + _dot(yn, w_out)

    yn, out = rowmap(gate_norm_out, [y, z, h], [p["gate_norm"], p["w_out"]],
                     [(SSM_D_INNER, BF16), (D_MODEL, F32)], tm=RESIDENT_TM, name=nm + "_out")
    return out, (u, z, xbc_raw, dt_raw, dt, dt_g, dtT_g, y, states, yn)


def mamba_bwd(h, p, saved, dout, dout_bf, nm, io):
    u, z, xbc_raw, dt_raw, dt, dt_g, dtT_g, y, states, yn = saved
    g = {}
    io.put("w_out", mm(yn, dout_bf, ta=True, out_dtype=BF16, name=nm + "_dwout"))

    def gate_norm_bwd(d_o, yv, zv, gg, w_out):
        d = _dot(d_o, w_out, NT)
        sz = _silu(zv)
        t = yv * sz
        dts, dgs = [], []
        for k in range(SSM_GROUPS):
            sl = slice(k * GN_W, (k + 1) * GN_W)
            dt_k, dg_k = _rms_bwd(t[:, sl], gg[:, sl], d[:, sl])
            dts.append(dt_k)
            dgs.append(dg_k)
        d_t = jnp.concatenate(dts, axis=1)
        return d_t * sz, d_t * yv * _silu_grad(zv), jnp.concatenate(dgs, axis=1)

    dy, dz, g["gate_norm"] = rowmap(gate_norm_bwd, [dout_bf, y, z], [p["gate_norm"], p["w_out"]],
                                    [(SSM_D_INNER, F32), (SSM_D_INNER, BF16)], [(1, SSM_D_INNER)], tm=256,
                                    name=nm + "_dgatenorm")
    d_x, d_b, d_c, da_g, ddt_g, dd, dwx, dwb, dwc, dbx, dbb, dbc = ssd_bwd(
        dy, xbc_raw, p["conv_w"], p["conv_b"], dt_g, dtT_g, p["a_row"], p["a_col"], p["dvec"], states, nm + "_dssd",
        hook=io.hook(nm + "_dssd"))
    g["dvec"] = dd
    by_lane = lambda t: t.transpose(1, 0, 2).reshape(t.shape[1], -1)
    g["conv_w"] = jnp.concatenate([by_lane(dwx), by_lane(dwb), by_lane(dwc)], axis=1)
    g["conv_b"] = jnp.concatenate([by_lane(dbx), by_lane(dbb), by_lane(dbc)], axis=1)
    per_head = lambda t: t.transpose(1, 0, 2).reshape(-1, SSM_HEADS)
    ddt_raw, g["a"], g["dt_bias"] = ssd_dt_bwd(per_head(da_g), per_head(ddt_g), dt, dt_raw, p["a_heads"],
                                               p["dt_bias"], nm + "_ddt")
    dxbc_raw = conv_input_grad([d_x, d_b, d_c], p["conv_w"], nm + "_dconv")
    io.put("w_int", jnp.concatenate([mm(dz, u, ta=True, out_dtype=BF16, name=nm + "_dwz"),
                                     mm(dxbc_raw, u, ta=True, out_dtype=BF16, name=nm + "_dwxbc"),
                                     mm(ddt_raw, u, ta=True, out_dtype=BF16, name=nm + "_dwdt")], axis=0))
    dh, dh_bf, g["ssm_norm"] = du_norm_bwd(
        [(dz, p["w_zt"], False), (dxbc_raw, p["w_xbct"], False), (ddt_raw, p["w_dtt"], False)],
        h, p["ssm_norm"], dout, nm + "_du", hook=io.hook(nm + "_du"))
    return dh, dh_bf, g


KV_W = ATT_KV_HEADS * ATT_HEAD_DIM


def kv_fwd(h, p, nm):
    def kv_proj(x, gg, w_kv, gk):
        uu = _rms_fwd(x, gg).astype(BF16)
        t = _dot(uu, w_kv)
        ks = [_rms_fwd(t[:, j * ATT_HEAD_DIM:(j + 1) * ATT_HEAD_DIM], gk) for j in range(ATT_KV_HEADS)]
        return uu, t, jnp.concatenate(ks, axis=1), t[:, KV_W:]

    u, kv_raw, k, v = rowmap(kv_proj, [h], [p["kv_norm"], p["w_kv"], p["k_norm"]],
                             [(D_MODEL, BF16), (2 * KV_W, F32), (KV_W, F32), (KV_W, F32)], tm=RESIDENT_TM,
                             name=nm + "_proj")
    return k, v, (u, kv_raw)


def kv_bwd(h, p, saved, dk_cur, dk_prev, dv_cur, dv_prev, dout, nm, io):
    u, kv_raw = saved
    t_dim = h.shape[0]
    tm = ATT_WINDOW
    nb = t_dim // tm
    nxt = pl.BlockSpec((tm, KV_W), lambda i: (jnp.minimum(i + 1, nb - 1), 0))

    def fn(dkc, dkp, dvc, dvp, t, gg):
        live = pl.program_id(0) < nb - 1
        dk = dkc + jnp.where(live, dkp, 0.0)
        dv = dvc + jnp.where(live, dvp, 0.0)
        outs, dgs = [], None
        for j in range(ATT_KV_HEADS):
            sl = slice(j * ATT_HEAD_DIM, (j + 1) * ATT_HEAD_DIM)
            dx, dg = _rms_bwd(t[:, sl], gg, dk[:, sl])
            outs.append(dx)
            dgs = dg if dgs is None else dgs + dg
        return jnp.concatenate(outs + [dv], axis=1), dgs

    dkv_raw, dknorm = rowmap(fn, [dk_cur, (dk_prev, nxt), dv_cur, (dv_prev, nxt), kv_raw], [p["k_norm"]],
                             [(2 * KV_W, BF16)], [(1, ATT_HEAD_DIM)], tm=tm, name=nm + "_dknorm",
                             hook=io.hook(nm + "_dknorm"))
    g = {"k_norm": dknorm}
    io.put("w_kv", mm(u, dkv_raw, ta=True, out_dtype=BF16, name=nm + "_dwkv"))
    dh, dh_bf, g["kv_norm"] = du_norm_bwd([(dkv_raw, p["w_kv"], True)], h, p["kv_norm"], dout, nm + "_du",
                                          hook=io.hook(nm + "_du"))
    return dh, dh_bf, g


def _attn_scores(q_ref, kp_ref, kc_ref, vp_ref, vc_ref, qn_ref, bias_ref, sink_ref, kv, mxu_sum):
    hd = ATT_HEAD_DIM
    blk = ATT_WINDOW
    sl = slice(kv * hd, (kv + 1) * hd)
    kk = jnp.concatenate([kp_ref[:, sl], kc_ref[:, sl]], axis=0)
    vv = jnp.concatenate([vp_ref[:, sl], vc_ref[:, sl]], axis=0)
    gq = qn_ref[...]
    raws, rinvs = [], []
    for r in range(ATT_GROUP):
        hh = kv * ATT_GROUP + r
        x = q_ref[:, hh * hd:(hh + 1) * hd]
        raws.append(x)
        rinvs.append(lax.rsqrt(jnp.mean(x * x, axis=-1, keepdims=True) + EPS))
    xh = jnp.concatenate([x * ri for x, ri in zip(raws, rinvs)], axis=0)
    rinv = jnp.concatenate(rinvs, axis=0)
    q8 = xh * gq
    s = _dot(q8, kk, NT) * (hd ** -0.5) + bias_ref[kv]
    colk = lax.broadcasted_iota(jnp.int32, (1, 2 * blk), 1)
    s = jnp.where((pl.program_id(0) > 0) | (colk >= blk), s, NEG)
    sink = sink_ref[kv]
    m = jnp.maximum(jnp.max(s, axis=-1, keepdims=True), sink)
    pexp = jnp.exp(s - m)
    e_sink = jnp.exp(sink - m)
    if not mxu_sum:
        inv_den = 1.0 / (jnp.sum(pexp, axis=-1, keepdims=True) + e_sink)
        return kk, vv, xh, rinv, q8, pexp * inv_den, e_sink * inv_den
    ones = jnp.ones((2 * blk, LANES), BF16)
    inv_den = 1.0 / (jnp.dot(pexp.astype(BF16), ones, preferred_element_type=F32) + e_sink)
    return kk, vv, xh, rinv, q8, pexp * jnp.concatenate([inv_den, inv_den], axis=1), e_sink * inv_den[:, :1]


def _attn_specs(nb):
    blk = ATT_WINDOW
    cur = lambda i: (i, 0)
    prev = lambda i: (jnp.maximum(i - 1, 0), 0)
    return [
        pl.BlockSpec((blk, D_MODEL), cur),
        pl.BlockSpec((blk, KV_W), prev), pl.BlockSpec((blk, KV_W), cur),
        pl.BlockSpec((blk, KV_W), prev), pl.BlockSpec((blk, KV_W), cur),
        pl.BlockSpec((1, ATT_HEAD_DIM), lambda i: (0, 0)),
        pl.BlockSpec((ATT_KV_HEADS, ATT_GROUP * blk, 2 * blk), lambda i: (0, 0, 0)),
        pl.BlockSpec((ATT_KV_HEADS, ATT_GROUP * blk, 1), lambda i: (0, 0, 0)),
    ]


def attn_fwd(q_raw, k, v, q_norm, bias, sink_col, nm):
    t_dim = q_raw.shape[0]
    blk, hd = ATT_WINDOW, ATT_HEAD_DIM
    nb = t_dim // blk

    def body(q_ref, kp_ref, kc_ref, vp_ref, vc_ref, qn_ref, bias_ref, sink_ref, o_ref):
        for kv in range(ATT_KV_HEADS):
            kk, vv, xh, rinv, q8, prob, p_sink = _attn_scores(q_ref, kp_ref, kc_ref, vp_ref, vc_ref, qn_ref,
                                                              bias_ref, sink_ref, kv, False)
            o8 = _dot(prob, vv)
            for r in range(ATT_GROUP):
                hh = kv * ATT_GROUP + r
                o_ref[:, hh * hd:(hh + 1) * hd] = o8[r * blk:(r + 1) * blk].astype(o_ref.dtype)

    out, = pcall(body, name=nm, grid=(nb,), in_specs=_attn_specs(nb),
                 out_specs=[pl.BlockSpec((blk, D_MODEL), lambda i: (i, 0))],
                 out_shape=[jax.ShapeDtypeStruct((t_dim, D_MODEL), BF16)],
                 args=[q_raw, k, k, v, v, q_norm, bias, sink_col])
    return out


def attn_bwd(do, q_raw, k, v, q_norm, bias, sink_col, nm, hook=None):
    t_dim = q_raw.shape[0]
    blk, hd = ATT_WINDOW, ATT_HEAD_DIM
    nb = t_dim // blk
    scale = hd ** -0.5

    def body(do_ref, q_ref, kp_ref, kc_ref, vp_ref, vc_ref, qn_ref, bias_ref, sink_ref,
             dq_ref, dkc_ref, dkp_ref, dvc_ref, dvp_ref, dbias_ref, dsink_ref, dqn_ref):
        @pl.when(pl.program_id(0) == 0)
        def _():
            dbias_ref[...] = jnp.zeros_like(dbias_ref)
            dsink_ref[...] = jnp.zeros_like(dsink_ref)
            dqn_ref[...] = jnp.zeros_like(dqn_ref)

        gq = qn_ref[...]
        for kv in range(ATT_KV_HEADS):
            kk, vv, xh, rinv, q8, prob, p_sink = _attn_scores(q_ref, kp_ref, kc_ref, vp_ref, vc_ref, qn_ref,
                                                              bias_ref, sink_ref, kv, True)
            do8 = jnp.concatenate([do_ref[:, (kv * ATT_GROUP + r) * hd:(kv * ATT_GROUP + r + 1) * hd]
                                   for r in range(ATT_GROUP)], axis=0)
            dp = _dot(do8, vv, NT)
            delta = jnp.sum(prob * dp, axis=-1, keepdims=True)
            ds = prob * (dp - delta)
            dsink_ref[kv] += -p_sink * delta
            dbias_ref[kv] += ds
            ds_s = ds * scale
            dq8 = _dot(ds_s, kk)
            dkk = _dot(ds_s, q8, TN)
            dvv = _dot(prob, do8, TN)
            dqn_ref[...] += jnp.sum(dq8 * xh, axis=0, keepdims=True)
            dxh = dq8 * gq
            dq_raw8 = rinv * (dxh - xh * jnp.mean(dxh * xh, axis=-1, keepdims=True))
            for r in range(ATT_GROUP):
                hh = kv * ATT_GROUP + r
                dq_ref[:, hh * hd:(hh + 1) * hd] = dq_raw8[r * blk:(r + 1) * blk].astype(dq_ref.dtype)
            sl = slice(kv * hd, (kv + 1) * hd)
            dkp_ref[:, sl] = dkk[:blk]
            dkc_ref[:, sl] = dkk[blk:]
            dvp_ref[:, sl] = dvv[:blk]
            dvc_ref[:, sl] = dvv[blk:]

    cur = lambda i: (i, 0)
    row_spec = pl.BlockSpec((blk, KV_W), cur)
    out_specs = [
        pl.BlockSpec((blk, D_MODEL), cur), row_spec, row_spec, row_spec, row_spec,
        pl.BlockSpec((ATT_KV_HEADS, ATT_GROUP * blk, 2 * blk), lambda i: (0, 0, 0)),
        pl.BlockSpec((ATT_KV_HEADS, ATT_GROUP * blk, 1), lambda i: (0, 0, 0)),
        pl.BlockSpec((1, hd), lambda i: (0, 0)),
    ]
    kvs = jax.ShapeDtypeStruct((t_dim, KV_W), F32)
    out_shape = [
        jax.ShapeDtypeStruct((t_dim, D_MODEL), BF16), kvs, kvs, kvs, kvs,
        jax.ShapeDtypeStruct((ATT_KV_HEADS, ATT_GROUP * blk, 2 * blk), F32),
        jax.ShapeDtypeStruct((ATT_KV_HEADS, ATT_GROUP * blk, 1), F32),
        jax.ShapeDtypeStruct((1, hd), F32),
    ]
    return pcall(body, name=nm, grid=(nb,), in_specs=[pl.BlockSpec((blk, D_MODEL), cur)] + _attn_specs(nb),
                 out_specs=out_specs, out_shape=out_shape,
                 args=[do, q_raw, k, k, v, v, q_norm, bias, sink_col], hook=hook)


def _t5_bucket_np():
    blk = ATT_WINDOW
    qi = np.arange(blk)[:, None] + blk
    kj = np.arange(2 * blk)[None, :]
    dist = qi - kj
    n = np.maximum(dist, 0)
    max_exact = REL_BUCKETS // 2
    nf = np.maximum(n, 1).astype(np.float32)
    large = max_exact + (np.log(nf / max_exact) / math.log(ATT_WINDOW / max_exact)
                         * (REL_BUCKETS - max_exact)).astype(np.int32)
    large = np.minimum(large, REL_BUCKETS - 1)
    bucket = np.where(n < max_exact, n, large)
    in_window = (dist >= 0) & (dist < ATT_WINDOW)
    return bucket, in_window


def attn_block_fwd(h, k, v, p, nm):
    def q_proj(x, gg, w_q):
        uu = _rms_fwd(x, gg).astype(BF16)
        return uu, _dot(uu, w_q)

    u, q_raw = rowmap(q_proj, [h], [p["attn_norm"], p["w_q"]], [(D_MODEL, BF16), (D_MODEL, F32)], tm=RESIDENT_TM,
                      name=nm + "_q")
    o = attn_fwd(q_raw, k, v, p["q_norm"], p["bias"], p["sink_col"], nm + "_core")
    out = mm(o, p["w_o"], res=h, name=nm + "_o")
    return out, (u, q_raw, o)


def attn_block_bwd(h, k, v, p, saved, dout, dout_bf, nm, io):
    u, q_raw, o = saved
    g = {}
    io.put("w_o", mm(o, dout_bf, ta=True, out_dtype=BF16, name=nm + "_dwo", hook=io.hook(nm + "_dwo")))
    do = mm(dout_bf, p["w_o"], tb=True, name=nm + "_do")
    dq_raw, dkc, dkp, dvc, dvp, g["bias"], g["sink_col"], g["q_norm"] = attn_bwd(
        do, q_raw, k, v, p["q_norm"], p["bias"], p["sink_col"], nm + "_dcore", hook=io.hook(nm + "_dcore"))
    io.put("w_q", mm(u, dq_raw, ta=True, out_dtype=BF16, name=nm + "_dwq"))
    dh, dh_bf, g["attn_norm"] = du_norm_bwd([(dq_raw, p["w_q"], True)], h, p["attn_norm"], dout, nm + "_du")
    return dh, dh_bf, g, (dkc, dkp, dvc, dvp)


FFN_TAGS = ["00", "01", "10", "11"]


def local_step(x, target, small, io):
    bucket, in_window = _t5_bucket_np()
    blk = ATT_WINDOW
    w = small

    fnorm = {tag: w["ffn_norm"][int(tag[0]), int(tag[1])][None, :] for tag in FFN_TAGS}
    a_neg = -jnp.exp(w["ssm_a_log"][0])

    def mamba_p():
        w_int = io.w("w_int")
        return dict(ssm_norm=w["ssm_norm"], w_zt=w_int[:SSM_D_INNER],
                    w_xbct=w_int[SSM_D_INNER:SSM_D_INNER + SSM_CONV_DIM], w_dtt=w_int[SSM_D_INNER + SSM_CONV_DIM:],
                    conv_w=w["ssm_conv_w"][0], conv_b=w["ssm_conv_b"], dt_bias=w["ssm_dt_bias"],
                    a_heads=a_neg[None, :], a_row=a_neg.reshape(SSM_GROUPS, 1, HPG),
                    a_col=a_neg.reshape(SSM_GROUPS, HPG, 1),
                    dvec=jnp.repeat(w["ssm_d"][0], SSM_HEAD_DIM)[None, :],
                    gate_norm=w["ssm_gate_norm"], w_out=io.w("w_out"))

    rb = w["rel_bias"]
    onehot3 = (np.arange(REL_BUCKETS)[:, None, None] == bucket[None]).astype(np.float32)
    bias = jnp.einsum("bh,bqk->hqk", rb, onehot3, precision=lax.Precision.HIGHEST)
    bias = jnp.where(in_window[None], bias, NEG)
    bias = bias.reshape(ATT_KV_HEADS, ATT_GROUP * blk, 2 * blk)
    sink_col = jnp.repeat(w["sinks"][0], blk).reshape(ATT_KV_HEADS, ATT_GROUP * blk, 1)

    def attn_p():
        return dict(attn_norm=w["attn_norm"], w_q=io.w("w_q"), q_norm=w["q_norm"], bias=bias, sink_col=sink_col,
                    w_o=io.w("w_o"))

    def kv_p():
        return dict(kv_norm=w["kv_norm"][None, :], w_kv=io.w("w_kv"), k_norm=w["k_norm"][None, :])

    h0 = x
    h0a, s_f00 = ffn_fwd(h0, fnorm["00"], "00", io)
    mp = mamba_p()
    h0b, s_m = mamba_fwd(h0a, mp, "ssm", io)
    h1, s_f01 = ffn_fwd(h0b, fnorm["01"], "01", io)
    kp = kv_p()
    k, v, s_kv = kv_fwd(h1, kp, "kv")
    h1a, s_f10 = ffn_fwd(h1, fnorm["10"], "10", io)
    ap = attn_p()
    h1b, s_a = attn_block_fwd(h1a, k, v, ap, "att")
    (dh, dh_bf, sq), s_f11 = ffn_fwd(h1b, fnorm["11"], "11", io, target=target)
    loss_part = jnp.sum(sq) * (0.5 / D_MODEL)

    fg = {}

    def ffn_back(tag, h_in, saved, dh, dh_bf):
        dh, dh_bf, dg = ffn_bwd(h_in, fnorm[tag], tag, saved, dh, dh_bf, io)
        fg[tag] = dg[0]
        return dh, dh_bf

    dh, dh_bf = ffn_back("11", h1b, s_f11, dh, dh_bf)
    dh, dh_bf, ga, dkv = attn_block_bwd(h1a, k, v, ap, s_a, dh, dh_bf, "att", io)
    dh, dh_bf = ffn_back("10", h1, s_f10, dh, dh_bf)
    dh, dh_bf, gk = kv_bwd(h1, kp, s_kv, *dkv, dh, "kv", io)
    dh, dh_bf = ffn_back("01", h0b, s_f01, dh, dh_bf)
    dh, dh_bf, gm = mamba_bwd(h0a, mp, s_m, dh, dh_bf, "ssm", io)
    dh, dh_bf = ffn_back("00", h0, s_f00, dh, dh_bf)
    grad_x = dh

    grads = {}
    grads["ffn_norm"] = jnp.stack([fg[tag] for tag in FFN_TAGS]).reshape(2, 2, D_MODEL)
    grads["ssm_norm"] = gm["ssm_norm"]
    grads["ssm_conv_w"] = gm["conv_w"][None]
    grads["ssm_conv_b"] = gm["conv_b"]
    grads["ssm_dt_bias"] = gm["dt_bias"]
    grads["ssm_a_log"] = gm["a"] * a_neg[None, :]
    grads["ssm_d"] = jnp.sum(gm["dvec"].reshape(SSM_HEADS, SSM_HEAD_DIM), axis=1)[None, :]
    grads["ssm_gate_norm"] = gm["gate_norm"]
    grads["kv_norm"] = gk["kv_norm"][0]
    grads["k_norm"] = gk["k_norm"][0]
    grads["attn_norm"] = ga["attn_norm"]
    grads["q_norm"] = ga["q_norm"]
    grads["sinks"] = jnp.sum(ga["sink_col"].reshape(ATT_HEADS, blk), axis=1)[None, :]
    onehot = (np.arange(REL_BUCKETS)[:, None] == bucket.reshape(1, -1)).astype(np.float32)
    dbias2d = ga["bias"].reshape(ATT_HEADS, blk * 2 * blk)
    grads["rel_bias"] = mm(jnp.asarray(onehot, BF16), dbias2d, tb=True, name="drelbias")
    return loss_part, grad_x, grads


def _adamw(g, w, m, v):
    m = ADAM_B1 * m + (1.0 - ADAM_B1) * g
    v = ADAM_B2 * v + (1.0 - ADAM_B2) * (g * g)
    m_hat = m / (1.0 - ADAM_B1 ** ADAM_STEP)
    v_hat = v / (1.0 - ADAM_B2 ** ADAM_STEP)
    delta = -ADAM_LR * (m_hat / (jnp.sqrt(v_hat) + ADAM_EPS) + ADAM_WD * w)
    return delta, m, v


def _slot_sum(r):
    g = r[0].astype(F32)
    for d in range(1, r.shape[0]):
        g = g + r[d].astype(F32)
    return g


def adamw_rows(recvs, w, m, v, name):
    n_l, rows, width = w.shape
    n_slots = recvs[0].shape[0]
    tr = 32
    assert rows % tr == 0, rows
    nt = rows // tr

    def body(*refs):
        r_refs = refs[:n_l]
        w_ref, m_ref, v_ref, g_o, d_o, m_o, v_o = refs[n_l:]
        li = pl.program_id(0)
        for k in range(n_l):
            @pl.when(li == k)
            def _(k=k):
                g = _slot_sum(r_refs[k])
                delta, m2, v2 = _adamw(g, w_ref[0], m_ref[0], v_ref[0])
                g_o[0] = g
                d_o[0] = delta
                m_o[0] = m2
                v_o[0] = v2

    def r_spec(k):
        return pl.BlockSpec((n_slots, tr, width),
                            lambda li, j: (0, jnp.where(li == k, j, jnp.where(li > k, nt - 1, 0)), 0))

    w_spec = pl.BlockSpec((1, tr, width), lambda li, j: (li, j, 0))
    shp = jax.ShapeDtypeStruct(w.shape, F32)
    return pcall(body, name=name, grid=(n_l, nt), in_specs=[r_spec(k) for k in range(n_l)] + [w_spec] * 3,
                 out_specs=[w_spec] * 4, out_shape=[shp] * 4, args=list(recvs) + [w, m, v])


def adamw_cols(recvs, w, m, v, name):
    n_l, rows, n = w.shape
    n_slots = recvs[0].shape[0]
    tr = 256
    nt = rows // tr

    def body(*refs):
        r_refs = refs[:n_l]
        w_ref, m_ref, v_ref, g_o, d_o, m_o, v_o = refs[n_l:]
        li = pl.program_id(0)
        for k in range(n_l):
            @pl.when(li == k)
            def _(k=k):
                g = _slot_sum(r_refs[k]).T
                delta, m2, v2 = _adamw(g, w_ref[0], m_ref[0], v_ref[0])
                g_o[0] = g
                d_o[0] = delta
                m_o[0] = m2
                v_o[0] = v2

    def r_spec(k):
        return pl.BlockSpec((n_slots, n, tr),
                            lambda li, j: (0, 0, jnp.where(li == k, j, jnp.where(li > k, nt - 1, 0))))

    w_spec = pl.BlockSpec((1, tr, n), lambda li, j: (li, j, 0))
    shp = jax.ShapeDtypeStruct(w.shape, F32)
    return pcall(body, name=name, grid=(n_l, nt), in_specs=[r_spec(k) for k in range(n_l)] + [w_spec] * 3,
                 out_specs=[w_spec] * 4, out_shape=[shp] * 4, args=list(recvs) + [w, m, v])


WEIGHT_NAMES = ["ffn_norm", "ffn_w1", "ffn_w3", "ffn_w2", "ssm_norm", "ssm_w_in", "ssm_conv_w", "ssm_conv_b",
                "ssm_dt_bias", "ssm_a_log", "ssm_d", "ssm_gate_norm", "ssm_w_out", "kv_norm", "w_kv", "k_norm",
                "attn_norm", "w_q", "q_norm", "sinks", "w_o", "rel_bias"]

SMALL = [
    ("ffn_norm", (2, 2, 1024), 2), ("ssm_norm", (1, 1024), 1), ("ssm_conv_w", (1, 4, 3072), 2),
    ("ssm_conv_b", (1, 3072), 1), ("ssm_gate_norm", (1, 2048), 1),
    ("ssm_dt_bias", (1, 32), None), ("ssm_a_log", (1, 32), None), ("ssm_d", (1, 32), None),
    ("kv_norm", (1024,), None), ("k_norm", (64,), None), ("attn_norm", (1, 1024), None),
    ("q_norm", (1, 64), None), ("sinks", (1, 16), None), ("rel_bias", (32, 16), None),
]
SMALL_W = 1024
SMALL_FULL_ROWS = 32
SMALL_LOCAL_ROWS = 48

MAT_GROUPS = {
    "f00_up": ["w1t_00", "w3t_00"], "f00_down": ["w2_00"], "f01": ["w1t_01", "w3t_01", "w2_01"],
    "f10": ["w1t_10", "w3t_10", "w2_10"], "f11": ["w1t_11", "w3t_11", "w2_11"],
    "ssm": ["w_int", "w_out"], "att": ["w_q", "w_o", "w_kv"],
    "f00_early": ["w2_00", "w1t_00"], "f00_late": ["w3t_00"],
}
FIRST_GATHER = "f00_up"
GATHER_PLAN = {"f00_upgate": ["f00_down", "ssm"], "ssm_in": ["f01"], "ssm_ssd": ["att", "f10"],
               "f01_upgate": ["f11"]}
SCATTER_A_PLAN = {"att_dwo": "f11", "kv_dknorm": "f10", "kv_du": "att", "f01_du": "f01", "ssm_du": "ssm",
                  "f00_dw3": "f00_early", "f00_du": "f00_late"}
SCATTER_B_PLAN = {"att_dcore": "f11", "f01_dw2": "att", "f01_dgate": "f10", "ssm_dssd": "f01", "f00_dgate": "ssm",
                  "f00_du": "f00_early"}
LAST_SCATTER = "f00_late"
SLOT_MAJOR = ("w_int",)


def _shard_shape(s, a):
    return s[:a] + (s[a] // N_DEV,) + s[a + 1:]


def _unshard_view(stack, shard_shape, axis):
    moved = jnp.moveaxis(stack, 0, axis)
    return moved.reshape(shard_shape[:axis] + (N_DEV * shard_shape[axis],) + shard_shape[axis + 1:])


def _small_local(arrs):
    flat = jnp.concatenate([arrs[n].reshape(-1) for n, _, _ in SMALL])
    return jnp.pad(flat, (0, SMALL_LOCAL_ROWS * LANES - flat.shape[0])).reshape(SMALL_LOCAL_ROWS, LANES)


def chip_partial(g4, ra, name):
    _, _, n, width = g4.shape

    def body(core_ref, g_ref, r_ref, o_ref):
        o_ref[0] = (g_ref[0, 0].astype(F32) + r_ref[0, 0].astype(F32)).astype(o_ref.dtype)

    grid_spec = pltpu.PrefetchScalarGridSpec(
        num_scalar_prefetch=1, grid=(N_CHIPS,),
        in_specs=[pl.BlockSpec((1, 1, n, width), lambda q, core: (q, core[0], 0, 0)),
                  pl.BlockSpec((1, 1, n, width), lambda q, core: (q, 0, 0, 0))],
        out_specs=pl.BlockSpec((1, n, width), lambda q, core: (q, 0, 0)))
    core = jnp.reshape(lax.axis_index("c"), (1,)).astype(jnp.int32)
    return pl.pallas_call(
        body, name=name, grid_spec=grid_spec, out_shape=jax.ShapeDtypeStruct((N_CHIPS, n, width), g4.dtype),
        compiler_params=pltpu.CompilerParams(dimension_semantics=("arbitrary",), vmem_limit_bytes=VMEM_LIMIT_BYTES),
    )(core, g4, ra)


class StepIO:
    def __init__(self, pieces):
        self.pieces = pieces
        self.full = {}
        self.grad = {}
        self.from_sibling = {}
        self.recv = {}

    def w(self, name):
        return self.full[name]

    def put(self, name, g):
        self.grad[name] = g

    def _by_chip_core(self, name):
        g = self.grad[name]
        return g.reshape((N_CHIPS, 2, g.shape[0] // N_DEV) + g.shape[1:])

    def gather_items(self, groups):
        names = [n for grp in groups for n in MAT_GROUPS[grp]]
        items = [("g2", self.pieces[n], None if n in SLOT_MAJOR else 0) for n in names]

        def sink(outs):
            for n, o in zip(names, outs):
                self.full[n] = o.reshape((-1,) + o.shape[2:]) if n in SLOT_MAJOR else o

        return items, sink

    def scatter_a_items(self, group):
        names = MAT_GROUPS[group]
        items = [("sa", self._by_chip_core(n), None) for n in names]

        def sink(outs):
            for n, o in zip(names, outs):
                self.from_sibling[n] = o

        return items, sink

    def scatter_b_items(self, group):
        names = MAT_GROUPS[group]
        items = [("sb", chip_partial(self._by_chip_core(n), self.from_sibling[n], "partial_" + n), None)
                 for n in names]

        def sink(outs):
            for n, o in zip(names, outs):
                self.recv[n] = o

        return items, sink

    def hook(self, site):
        parts = []
        if site in GATHER_PLAN:
            parts.append(self.gather_items(GATHER_PLAN[site]))
        if site in SCATTER_A_PLAN:
            parts.append(self.scatter_a_items(SCATTER_A_PLAN[site]))
        if site in SCATTER_B_PLAN:
            parts.append(self.scatter_b_items(SCATTER_B_PLAN[site]))
        if not parts:
            return None
        return combine_hooks(parts)


def combine_hooks(parts):
    items = [it for its, _ in parts for it in its]

    def sink(outs):
        p = 0
        for its, snk in parts:
            snk(outs[p:p + len(its)])
            p += len(its)

    return Comm(items), sink


def step(x, target, wts, ms, vs):
    me = _my_index()

    pieces = {}
    for li in range(2):
        for hi in range(2):
            tag = "%d%d" % (li, hi)
            pieces["w1t_" + tag] = wts["ffn_w1"][li, hi].T.astype(BF16)
            pieces["w3t_" + tag] = wts["ffn_w3"][li, hi].T.astype(BF16)
            pieces["w2_" + tag] = wts["ffn_w2"][li, hi].astype(BF16)
    pieces["w_int"] = wts["ssm_w_in"][0].T.astype(BF16)
    pieces["w_out"] = wts["ssm_w_out"][0].astype(BF16)
    pieces["w_kv"] = wts["w_kv"].astype(BF16)
    pieces["w_q"] = wts["w_q"][0].astype(BF16)
    pieces["w_o"] = wts["w_o"][0].astype(BF16)
    io = StepIO(pieces)

    small_sharded = [(n, s, a) for n, s, a in SMALL if a is not None]
    loc = jnp.concatenate([wts[n].reshape(-1) for n, _, _ in small_sharded])
    loc_rows = -(-loc.shape[0] // (8 * LANES)) * 8
    loc = jnp.pad(loc, (0, loc_rows * LANES - loc.shape[0])).reshape(loc_rows, LANES)
    got_small = []
    comm, sink = combine_hooks([io.gather_items([FIRST_GATHER]), ([("g", loc, None)], got_small.extend)])
    sink(comm_only(comm, "gather_first"))
    gath_small = got_small[0].reshape(N_DEV, -1)
    small = {}
    off = 0
    for n, s, a in small_sharded:
        shard = _shard_shape(s, a)
        cnt = int(np.prod(shard))
        small[n] = _unshard_view(gath_small[:, off:off + cnt].reshape((N_DEV,) + shard), shard, a)
        off += cnt
    for n, s, a in SMALL:
        if a is None:
            small[n] = wts[n]

    loss_part, grad_x, g_small_local = local_step(x[0], target[0], small, io)
    loss = lax.psum(loss_part, ("x", "y", "c"))

    small_flat = jnp.concatenate([g_small_local[n].reshape(-1) for n, _, _ in SMALL])
    small_buf = jnp.pad(small_flat, (0, SMALL_FULL_ROWS * SMALL_W - small_flat.shape[0]))
    small_buf = small_buf.reshape(SMALL_FULL_ROWS, SMALL_W)
    got_small = []
    comm, sink = combine_hooks([io.scatter_b_items(LAST_SCATTER), ([("g", small_buf, None)], got_small.extend)])
    sink(comm_only(comm, "exchange_last"))
    small_all = got_small[0]

    def sum_body(r_ref, o_ref):
        o_ref[...] = _slot_sum(r_ref)

    vmem = pl.BlockSpec(memory_space=pltpu.VMEM)
    small_sum, = pcall(sum_body, name="sum_small", grid=(), in_specs=[vmem], out_specs=[vmem],
                       out_shape=[jax.ShapeDtypeStruct((SMALL_FULL_ROWS, SMALL_W), F32)], args=[small_all])
    small_sum = small_sum.reshape(-1)
    g_small = {}
    off = 0
    for n, s, a in SMALL:
        cnt = int(np.prod(s))
        gfull = small_sum[off:off + cnt].reshape(s)
        off += cnt
        if a is None:
            g_small[n] = gfull
        else:
            width = s[a] // N_DEV
            g_small[n] = lax.dynamic_slice_in_dim(gfull, me * width, width, axis=a)

    out = {}

    def emit(name, res, shape):
        for kind, arr in zip(("grad", "delta", "new_m", "new_v"), res):
            out[kind + "_" + name] = arr.reshape(shape)

    for name, key in (("ffn_w1", "w1t_"), ("ffn_w3", "w3t_")):
        shp = wts[name].shape
        view = lambda t: t.reshape((4,) + shp[2:])
        res = adamw_cols([io.recv[key + tag] for tag in FFN_TAGS], view(wts[name]), view(ms[name]), view(vs[name]),
                         "adamw_" + name)
        emit(name, res, shp)
    shp = wts["ffn_w2"].shape
    view = lambda t: t.reshape((4,) + shp[2:])
    res = adamw_rows([io.recv["w2_" + tag] for tag in FFN_TAGS], view(wts["ffn_w2"]), view(ms["ffn_w2"]),
                     view(vs["ffn_w2"]), "adamw_ffn_w2")
    emit("ffn_w2", res, shp)
    res = adamw_cols([io.recv["w_int"]], wts["ssm_w_in"], ms["ssm_w_in"], vs["ssm_w_in"], "adamw_ssm_w_in")
    emit("ssm_w_in", res, wts["ssm_w_in"].shape)
    for name, key in (("ssm_w_out", "w_out"), ("w_kv", "w_kv"), ("w_q", "w_q"), ("w_o", "w_o")):
        shp = wts[name].shape
        view = lambda t: t.reshape((1,) + shp[-2:])
        res = adamw_rows([io.recv[key]], view(wts[name]), view(ms[name]), view(vs[name]), "adamw_" + name)
        emit(name, res, shp)

    res_s = rowmap(lambda gg, ww, mm_, vv: _adamw(gg, ww, mm_, vv),
                   [_small_local(g_small), _small_local(wts), _small_local(ms), _small_local(vs)], [],
                   [(LANES, F32)] * 3, tm=SMALL_LOCAL_ROWS, name="adamw_small")
    flat_s = [r.reshape(-1) for r in res_s]
    off = 0
    for n, s, a in SMALL:
        shard = s if a is None else _shard_shape(s, a)
        cnt = int(np.prod(shard))
        out["grad_" + n] = g_small[n]
        for kind, arr in zip(("delta", "new_m", "new_v"), flat_s):
            out[kind + "_" + n] = arr[off:off + cnt].reshape(shard)
        off += cnt
    out["loss"] = loss
    out["grad_x"] = grad_x[None]
    return out


def kernel(x, ffn_norm, ffn_w1, ffn_w3, ffn_w2, ssm_norm, ssm_w_in, ssm_conv_w, ssm_conv_b, ssm_dt_bias, ssm_a_log, ssm_d, ssm_gate_norm, ssm_w_out, kv_norm, w_kv, k_norm, attn_norm, w_q, q_norm, sinks, w_o, rel_bias, loss_target, m_ffn_norm, m_ffn_w1, m_ffn_w3, m_ffn_w2, m_ssm_norm, m_ssm_w_in, m_ssm_conv_w, m_ssm_conv_b, m_ssm_dt_bias, m_ssm_a_log, m_ssm_d, m_ssm_gate_norm, m_ssm_w_out, m_kv_norm, m_w_kv, m_k_norm, m_attn_norm, m_w_q, m_q_norm, m_sinks, m_w_o, m_rel_bias, v_ffn_norm, v_ffn_w1, v_ffn_w3, v_ffn_w2, v_ssm_norm, v_ssm_w_in, v_ssm_conv_w, v_ssm_conv_b, v_ssm_dt_bias, v_ssm_a_log, v_ssm_d, v_ssm_gate_norm, v_ssm_w_out, v_kv_norm, v_w_kv, v_k_norm, v_attn_norm, v_w_q, v_q_norm, v_sinks, v_w_o, v_rel_bias):
    args = locals()
    wts = {n: args[n] for n in WEIGHT_NAMES}
    ms = {n: args["m_" + n] for n in WEIGHT_NAMES}
    vs = {n: args["v_" + n] for n in WEIGHT_NAMES}
    out = step(x, loss_target, wts, ms, vs)
    result = [out["loss"], out["grad_x"]]
    for kind in ("grad", "delta", "new_m", "new_v"):
        result += [out[kind + "_" + n] for n in WEIGHT_NAMES]
    return tuple(result)
```

```python
import functools
import math
import operator

import numpy as np
import jax
import jax.numpy as jnp
from jax import lax
from jax.experimental import pallas as pl
from jax.experimental.pallas import tpu as pltpu

F32 = jnp.float32
BF16 = jnp.bfloat16

D_MODEL = 1024
D_FF = 2816
N_DEV = 8
SSM_D_INNER = 2048
SSM_HEAD_DIM = 64
SSM_HEADS = 32
SSM_GROUPS = 4
SSM_STATE = 128
SSM_CONV = 4
SSM_CHUNK = 256
SSM_CONV_DIM = SSM_D_INNER + 2 * SSM_GROUPS * SSM_STATE
SSM_IN_DIM = SSM_D_INNER + SSM_CONV_DIM + SSM_HEADS
ATT_HEAD_DIM = 64
ATT_HEADS = 16
ATT_KV_HEADS = 2
ATT_GROUP = 8
ATT_WINDOW = 128
REL_BUCKETS = 32
EPS = 1e-6
NEG = -1e30

ADAM_LR = 0.001
ADAM_B1 = 0.9
ADAM_B2 = 0.999
ADAM_EPS = 1e-08
ADAM_WD = 0.01
ADAM_STEP = 10

VMEM_LIMIT_BYTES = 52 * 1024 * 1024
LANES = 128
MESH_ID = pl.DeviceIdType.MESH
ANY_SPEC = pl.BlockSpec(memory_space=pl.ANY)

NT = (((1,), (1,)), ((), ()))
TN = (((0,), (0,)), ((), ()))
NN = (((1,), (0,)), ((), ()))


def _pick(dim, cands):
    for c in cands:
        if dim % c == 0:
            return c
    return dim


def _my_index():
    return 4 * lax.axis_index("x") + 2 * lax.axis_index("y") + lax.axis_index("c")


def _peer(k):
    x, y, c = lax.axis_index("x"), lax.axis_index("y"), lax.axis_index("c")
    px = 1 - x if (k >> 2) & 1 else x
    py = 1 - y if (k >> 1) & 1 else y
    pc = 1 - c if k & 1 else c
    return (px, py, pc), 4 * px + 2 * py + pc


def _piece(ref, axis, d, n):
    if axis is None:
        return ref.at[d]
    return ref.at[(slice(None),) * axis + (pl.ds(pl.multiple_of(d * n, 8), n),)]


SIBLING = 1
CHIP_PEERS = (4, 2, 6)
N_CHIPS = 4
SEMS_PER_ITEM = N_DEV - 1


def _my_chip():
    return 2 * lax.axis_index("x") + lax.axis_index("y")


class Comm:
    def __init__(self, items):
        self.items = list(items)

    def dst_shapes(self):
        out = []
        for kind, src, axis in self.items:
            s = tuple(src.shape)
            if kind == "g":
                shp = (N_DEV,) + s
            elif kind == "g2":
                shp = (N_DEV,) + s if axis is None else s[:axis] + (N_DEV * s[axis],) + s[axis + 1:]
            elif kind == "sa":
                shp = (s[0], 1) + s[2:]
            else:
                shp = s
            out.append(jax.ShapeDtypeStruct(shp, src.dtype))
        return out

    def scratch(self):
        n = len(self.items)
        return [pltpu.SemaphoreType.DMA((n * SEMS_PER_ITEM,)), pltpu.SemaphoreType.DMA((n * SEMS_PER_ITEM,)),
                pltpu.SemaphoreType.DMA((n,))]

    def _run(self, srcs, dsts, sems, starting):
        send_sems, recv_sems, local_sems = sems
        me = _my_index()
        core = lax.axis_index("c")
        chip = _my_chip()
        for i, (kind, src, axis) in enumerate(self.items):
            s_ref, d_ref = srcs[i], dsts[i]
            base = i * SEMS_PER_ITEM

            def rdma(src_ref, dst_ref, j, peer):
                return pltpu.make_async_remote_copy(
                    src_ref=src_ref, dst_ref=dst_ref, send_sem=send_sems.at[base + j], recv_sem=recv_sems.at[base + j],
                    device_id=peer, device_id_type=MESH_ID)

            if kind == "g":
                local = pltpu.make_async_copy(s_ref, d_ref.at[me], local_sems.at[i])
                outs = [rdma(s_ref, d_ref.at[me], k - 1, _peer(k)[0]) for k in range(1, N_DEV)]
                if starting:
                    local.start()
                    for cp in outs:
                        cp.start()
                else:
                    for k in range(1, N_DEV):
                        rdma(s_ref, d_ref.at[_peer(k)[1]], k - 1, _peer(k)[0]).wait_recv()
                    for cp in outs:
                        cp.wait_send()
                    local.wait()
            elif kind == "g2":
                n = None if axis is None else src.shape[axis]
                mine = _piece(d_ref, axis, me, n)
                sib = _peer(SIBLING)[0]
                local = pltpu.make_async_copy(s_ref, mine, local_sems.at[i])
                outs = [rdma(s_ref, mine, 0, sib)] + [rdma(s_ref, mine, 1 + j, _peer(k)[0])
                                                      for j, k in enumerate(CHIP_PEERS)]
                if starting:
                    local.start()
                    for cp in outs:
                        cp.start()
                else:
                    passed = []
                    for j, k in enumerate(CHIP_PEERS):
                        theirs = _piece(d_ref, axis, _peer(k)[1], n)
                        rdma(s_ref, theirs, 1 + j, _peer(k)[0]).wait_recv()
                        fwd = rdma(theirs, theirs, 4 + j, sib)
                        fwd.start()
                        passed.append(fwd)
                    rdma(s_ref, _piece(d_ref, axis, _peer(SIBLING)[1], n), 0, sib).wait_recv()
                    for j, k in enumerate(CHIP_PEERS):
                        rdma(s_ref, _piece(d_ref, axis, _peer(k ^ SIBLING)[1], n), 4 + j, sib).wait_recv()
                    for cp in outs + passed:
                        cp.wait_send()
                    local.wait()
            elif kind == "sa":
                cp = rdma(s_ref.at[(slice(None), pl.ds(1 - core, 1))], d_ref, 0, _peer(SIBLING)[0])
                if starting:
                    cp.start()
                else:
                    cp.wait_recv()
                    cp.wait_send()
            else:
                local = pltpu.make_async_copy(s_ref.at[chip], d_ref.at[chip], local_sems.at[i])
                outs = [rdma(s_ref.at[_peer(k)[1] >> 1], d_ref.at[chip], 1 + j, _peer(k)[0])
                        for j, k in enumerate(CHIP_PEERS)]
                if starting:
                    local.start()
                    for cp in outs:
                        cp.start()
                else:
                    for j, k in enumerate(CHIP_PEERS):
                        rdma(s_ref.at[chip], d_ref.at[_peer(k)[1] >> 1], 1 + j, _peer(k)[0]).wait_recv()
                    for cp in outs:
                        cp.wait_send()
                    local.wait()

    def start(self, srcs, dsts, sems):
        self._run(srcs, dsts, sems, True)

    def wait(self, srcs, dsts, sems):
        self._run(srcs, dsts, sems, False)


def pcall(body, *, name, grid, in_specs, out_specs, out_shape, args, scratch=(), hook=None):
    cparams = pltpu.CompilerParams(dimension_semantics=("arbitrary",) * len(grid), vmem_limit_bytes=VMEM_LIMIT_BYTES)
    if hook is None:
        outs = pl.pallas_call(body, name=name, grid=grid, in_specs=list(in_specs), out_specs=list(out_specs),
                              out_shape=list(out_shape), scratch_shapes=list(scratch), compiler_params=cparams)(*args)
        return list(outs)
    comm, sink = hook
    n_in, n_out, n_scr, n_it = len(args), len(out_shape), len(scratch), len(comm.items)
    dims = tuple(grid)

    def wrapped(*refs):
        p = 0
        ins = refs[p:p + n_in]
        p += n_in
        csrc = refs[p:p + n_it]
        p += n_it
        outs = refs[p:p + n_out]
        p += n_out
        cdst = refs[p:p + n_it]
        p += n_it
        scr = refs[p:p + n_scr]
        p += n_scr
        sems = refs[p:p + 3]
        if dims:
            ids = [pl.program_id(a) for a in range(len(dims))]
            first = functools.reduce(operator.and_, [i == 0 for i in ids])
            last = functools.reduce(operator.and_, [i == d - 1 for i, d in zip(ids, dims)])

            @pl.when(first)
            def _():
                comm.start(csrc, cdst, sems)

            body(*ins, *outs, *scr)

            @pl.when(last)
            def _():
                comm.wait(csrc, cdst, sems)
        else:
            comm.start(csrc, cdst, sems)
            body(*ins, *outs, *scr)
            comm.wait(csrc, cdst, sems)

    res = pl.pallas_call(
        wrapped, name=name, grid=grid,
        in_specs=list(in_specs) + [ANY_SPEC] * n_it, out_specs=list(out_specs) + [ANY_SPEC] * n_it,
        out_shape=list(out_shape) + comm.dst_shapes(), scratch_shapes=list(scratch) + comm.scratch(),
        compiler_params=cparams,
    )(*args, *[src for _, src, _ in comm.items])
    res = list(res)
    sink(res[n_out:])
    return res[:n_out]


def comm_only(comm, name):
    got = []
    pcall(lambda *refs: None, name=name, grid=(), in_specs=[], out_specs=[], out_shape=[], args=[],
          hook=(comm, got.extend))
    return got


def mm(a, b, *, ta=False, tb=False, out_dtype=F32, res=None, alpha=1.0, name, hook=None):
    if ta:
        k_dim, m_dim = a.shape
    else:
        m_dim, k_dim = a.shape
    if tb:
        n_dim, k2 = b.shape
    else:
        k2, n_dim = b.shape
    assert k_dim == k2, (a.shape, b.shape, ta, tb)
    tn = _pick(n_dim, (1024, 1408, 512, 256, 128))
    tm = _pick(m_dim, (1024, 1408, 512, 256, 128)) if tn <= 1024 else _pick(m_dim, (512, 256, 128))
    tk = _pick(k_dim, (1024, 512, 256, 128)) if ta else _pick(k_dim, (512, 1408, 256, 128))
    nk = k_dim // tk
    has_res = res is not None
    dn = (((0 if ta else 1,), (1 if tb else 0,)), ((), ()))

    def body(*refs):
        if has_res:
            a_ref, b_ref, r_ref, o_ref, acc_ref = refs
        else:
            a_ref, b_ref, o_ref, acc_ref = refs
        k = pl.program_id(2)

        @pl.when(k == 0)
        def _():
            acc_ref[...] = jnp.zeros_like(acc_ref)

        acc_ref[...] += lax.dot_general(a_ref[...].astype(BF16), b_ref[...].astype(BF16), dn,
                                        preferred_element_type=F32)

        @pl.when(k == nk - 1)
        def _():
            r = acc_ref[...]
            if alpha != 1.0:
                r = r * alpha
            if has_res:
                r = r_ref[...] + r
            o_ref[...] = r.astype(o_ref.dtype)

    a_spec = pl.BlockSpec((tk, tm), lambda i, j, k: (k, i)) if ta else pl.BlockSpec((tm, tk), lambda i, j, k: (i, k))
    b_spec = pl.BlockSpec((tn, tk), lambda i, j, k: (j, k)) if tb else pl.BlockSpec((tk, tn), lambda i, j, k: (k, j))
    o_spec = pl.BlockSpec((tm, tn), lambda i, j, k: (i, j))
    in_specs = [a_spec, b_spec] + ([o_spec] if has_res else [])
    args = [a, b] + ([res] if has_res else [])
    out, = pcall(body, name=name, grid=(m_dim // tm, n_dim // tn, nk), in_specs=in_specs, out_specs=[o_spec],
                 out_shape=[jax.ShapeDtypeStruct((m_dim, n_dim), out_dtype)], args=args,
                 scratch=[pltpu.VMEM((tm, tn), F32)], hook=hook)
    return out


def rowmap(fn, rows, consts=(), out_rows=(), out_accs=(), *, tm, name, hook=None):
    first = rows[0][0] if isinstance(rows[0], tuple) else rows[0]
    t_dim = first.shape[0]
    assert t_dim % tm == 0, (t_dim, tm)
    n_r, n_c, n_o = len(rows), len(consts), len(out_rows)

    def body(*refs):
        ins = [r[...] for r in refs[:n_r + n_c]]
        o_refs = refs[n_r + n_c:]
        outs = tuple(fn(*ins))
        for o_ref, val in zip(o_refs[:n_o], outs[:n_o]):
            o_ref[...] = val.astype(o_ref.dtype)
        if out_accs:
            @pl.when(pl.program_id(0) == 0)
            def _():
                for o_ref in o_refs[n_o:]:
                    o_ref[...] = jnp.zeros_like(o_ref)

            for o_ref, val in zip(o_refs[n_o:], outs[n_o:]):
                o_ref[...] += val

    in_specs, args = [], []
    for r in rows:
        if isinstance(r, tuple):
            args.append(r[0])
            in_specs.append(r[1])
        else:
            args.append(r)
            in_specs.append(pl.BlockSpec((tm, r.shape[1]), lambda i: (i, 0)))
    for c in consts:
        args.append(c)
        in_specs.append(pl.BlockSpec(c.shape, lambda i, nd=c.ndim: (0,) * nd))
    out_specs = [pl.BlockSpec((tm, w), lambda i: (i, 0)) for (w, _) in out_rows]
    out_specs += [pl.BlockSpec(s, lambda i, nd=len(s): (0,) * nd) for s in out_accs]
    out_shape = [jax.ShapeDtypeStruct((t_dim, w), dt) for (w, dt) in out_rows]
    out_shape += [jax.ShapeDtypeStruct(s, F32) for s in out_accs]
    return pcall(body, name=name, grid=(t_dim // tm,), in_specs=in_specs, out_specs=out_specs, out_shape=out_shape,
                 args=args, hook=hook)


def _rms_fwd(x, g):
    r = lax.rsqrt(jnp.mean(x * x, axis=-1, keepdims=True) + EPS)
    return x * r * g


def _rms_bwd(x, g, dy):
    r = lax.rsqrt(jnp.mean(x * x, axis=-1, keepdims=True) + EPS)
    xh = x * r
    dg = jnp.sum(dy * xh, axis=0, keepdims=True)
    dxh = dy * g
    dx = r * (dxh - xh * jnp.mean(dxh * xh, axis=-1, keepdims=True))
    return dx, dg


def _sigmoid(x):
    return 1.0 / (1.0 + jnp.exp(-x))


def _silu(x):
    return x * _sigmoid(x)


def _silu_grad(x):
    s = _sigmoid(x)
    return s * (1.0 + x * (1.0 - s))


def _split3(x):
    hi = x.astype(BF16)
    r1 = x - hi.astype(F32)
    mid = r1.astype(BF16)
    lo = (r1 - mid.astype(F32)).astype(BF16)
    return hi, mid, lo


def _dot(a, b, dn=NN):
    return lax.dot_general(a.astype(BF16), b.astype(BF16), dn, preferred_element_type=F32)


FFN_TN = 1408
RESIDENT_TM = 512


def ffn_upgate(h, g, w1t, w3t, nm, hook=None):
    t_dim = h.shape[0]
    tm = _pick(t_dim, (512, 256, 128))
    tn = FFN_TN

    n_j = D_FF // tn
    u_w = D_MODEL // n_j

    def body(h_ref, g_ref, w1_ref, w3_ref, u_ref, a_ref, b_ref, hm_ref):
        uu = _rms_fwd(h_ref[...], g_ref[...]).astype(BF16)
        for j in range(n_j):
            @pl.when(pl.program_id(0) == j)
            def _(j=j):
                u_ref[...] = uu[:, j * u_w:(j + 1) * u_w]

        a = lax.dot_general(uu, w1_ref[...], NT, preferred_element_type=F32)
        b = lax.dot_general(uu, w3_ref[...], NT, preferred_element_type=F32)
        a_ref[...] = a.astype(a_ref.dtype)
        b_ref[...] = b.astype(b_ref.dtype)
        hm_ref[...] = (_silu(a) * b).astype(hm_ref.dtype)

    row_spec = pl.BlockSpec((tm, D_MODEL), lambda j, i: (i, 0))
    w_spec = pl.BlockSpec((tn, D_MODEL), lambda j, i: (j, 0))
    o_spec = pl.BlockSpec((tm, tn), lambda j, i: (i, j))
    o_shape = jax.ShapeDtypeStruct((t_dim, D_FF), BF16)
    return pcall(body, name=nm, grid=(D_FF // tn, t_dim // tm),
                 in_specs=[row_spec, pl.BlockSpec((1, D_MODEL), lambda j, i: (0, 0)), w_spec, w_spec],
                 out_specs=[pl.BlockSpec((tm, u_w), lambda j, i: (i, j))] + [o_spec] * 3,
                 out_shape=[jax.ShapeDtypeStruct((t_dim, D_MODEL), BF16)] + [o_shape] * 3,
                 args=[h, g, w1t, w3t], hook=hook)


def ffn_dgate(dout_bf, w2, a, b, nm, hook=None):
    t_dim = dout_bf.shape[0]
    tm = _pick(t_dim, (512, 256, 128))
    tn = FFN_TN

    def body(d_ref, w2_ref, a_ref, b_ref, da_ref, db_ref):
        dhm = lax.dot_general(d_ref[...] * 0.5, w2_ref[...], NT, preferred_element_type=F32)
        av = a_ref[...].astype(F32)
        bv = b_ref[...].astype(F32)
        sg = _sigmoid(av)
        t = dhm * sg
        da_ref[...] = (t * bv * (1.0 + av * (1.0 - sg))).astype(da_ref.dtype)
        db_ref[...] = (t * av).astype(db_ref.dtype)

    t_spec = pl.BlockSpec((tm, tn), lambda j, i: (i, j))
    o_shape = jax.ShapeDtypeStruct((t_dim, D_FF), BF16)
    return pcall(body, name=nm, grid=(D_FF // tn, t_dim // tm),
                 in_specs=[pl.BlockSpec((tm, D_MODEL), lambda j, i: (i, 0)),
                           pl.BlockSpec((tn, D_MODEL), lambda j, i: (j, 0)), t_spec, t_spec],
                 out_specs=[t_spec] * 2, out_shape=[o_shape] * 2, args=[dout_bf, w2, a, b], hook=hook)


def ffn_fwd(h, g, tag, io, target=None):
    nm = "f" + tag
    u, a, b, hm = ffn_upgate(h, g, io.w("w1t_" + tag), io.w("w3t_" + tag), nm + "_upgate",
                             hook=io.hook(nm + "_upgate"))
    if target is None:
        return mm(hm, io.w("w2_" + tag), res=h, alpha=0.5, name=nm + "_down"), (u, a, b, hm)

    def down_loss(hmv, hv, t, w2):
        e = hv + 0.5 * _dot(hmv, w2) - t
        d = e * (1.0 / D_MODEL)
        return d, d, jnp.sum(e * e, axis=0, keepdims=True)

    res = rowmap(down_loss, [hm, h, target], [io.w("w2_" + tag)], [(D_MODEL, F32), (D_MODEL, BF16)],
                 [(1, D_MODEL)], tm=RESIDENT_TM, name=nm + "_down_loss")
    return res, (u, a, b, hm)


def du_norm_bwd(pairs, h, g, dout, nm, hook=None):
    t_dim = h.shape[0]
    tm = RESIDENT_TM
    n_p = len(pairs)

    def body(*refs):
        h_ref, d_ref, g_ref = refs[2 * n_p:2 * n_p + 3]
        dh_ref, dhb_ref, dg_ref = refs[2 * n_p + 3:]
        du = None
        for p, (_, _, tb) in enumerate(pairs):
            t = lax.dot_general(refs[2 * p][...].astype(BF16), refs[2 * p + 1][...].astype(BF16), NT if tb else NN,
                                preferred_element_type=F32)
            du = t if du is None else du + t
        dx, dg = _rms_bwd(h_ref[...], g_ref[...], du)
        dh = d_ref[...] + dx
        dh_ref[...] = dh
        dhb_ref[...] = dh.astype(dhb_ref.dtype)

        @pl.when(pl.program_id(0) == 0)
        def _():
            dg_ref[...] = jnp.zeros_like(dg_ref)

        dg_ref[...] += dg

    in_specs, args = [], []
    for a, b, _ in pairs:
        in_specs += [pl.BlockSpec((tm, a.shape[1]), lambda i: (i, 0)), pl.BlockSpec(b.shape, lambda i: (0, 0))]
        args += [a, b]
    row_spec = pl.BlockSpec((tm, D_MODEL), lambda i: (i, 0))
    vec_spec = pl.BlockSpec((1, D_MODEL), lambda i: (0, 0))
    return pcall(body, name=nm, grid=(t_dim // tm,), in_specs=in_specs + [row_spec, row_spec, vec_spec],
                 out_specs=[row_spec, row_spec, vec_spec],
                 out_shape=[jax.ShapeDtypeStruct((t_dim, D_MODEL), F32), jax.ShapeDtypeStruct((t_dim, D_MODEL), BF16),
                            jax.ShapeDtypeStruct((1, D_MODEL), F32)],
                 args=args + [h, dout, g], hook=hook)


def ffn_bwd(h, g, tag, saved, dout, dout_bf, io):
    nm = "f" + tag
    w1t, w3t, w2 = io.w("w1t_" + tag), io.w("w3t_" + tag), io.w("w2_" + tag)
    u, a, b, hm = saved
    io.put("w2_" + tag, mm(hm, dout_bf, ta=True, alpha=0.5, out_dtype=BF16, name=nm + "_dw2",
                           hook=io.hook(nm + "_dw2")))
    da, db = ffn_dgate(dout_bf, w2, a, b, nm + "_dgate", hook=io.hook(nm + "_dgate"))
    io.put("w1t_" + tag, mm(da, u, ta=True, out_dtype=BF16, name=nm + "_dw1"))
    io.put("w3t_" + tag, mm(db, u, ta=True, out_dtype=BF16, name=nm + "_dw3", hook=io.hook(nm + "_dw3")))
    return du_norm_bwd([(da, w1t, False), (db, w3t, False)], h, g, dout, nm + "_du", hook=io.hook(nm + "_du"))


def conv_input_grad(d_parts, w, nm):
    tm = 256
    t_dim = d_parts[0].shape[0]
    n_tiles = t_dim // tm

    def fn(d1, n1, d2, n2, d3, n3, ww):
        d = jnp.concatenate([d1, d2, d3], axis=1)
        nxt = jnp.concatenate([n1, n2, n3], axis=1)
        nxt = jnp.where(pl.program_id(0) < n_tiles - 1, nxt, 0.0)
        dd = jnp.concatenate([d, nxt], axis=0)
        n = tm + 8
        out = d * ww[SSM_CONV - 1:SSM_CONV]
        for k in range(SSM_CONV - 1):
            j = SSM_CONV - 1 - k
            out = out + pltpu.roll(dd, n - j, 0)[0:tm] * ww[k:k + 1]
        return (out,)

    rows = []
    for d in d_parts:
        below = pl.BlockSpec((8, d.shape[1]), lambda i: (jnp.minimum((i + 1) * (tm // 8), t_dim // 8 - 1), 0))
        rows += [d, (d, below)]
    dx, = rowmap(fn, rows, [w], [(SSM_CONV_DIM, BF16)], tm=tm, name=nm)
    return dx


GRP_W = SSM_D_INNER // SSM_GROUPS
HPG = SSM_HEADS // SSM_GROUPS
HEAD_SHIFT = 6


def _split2(x):
    hi = x.astype(BF16)
    return hi, (x - hi.astype(F32)).astype(BF16)


def _expand_mats():
    e = ((lax.broadcasted_iota(jnp.int32, (HPG, GRP_W), 1) >> HEAD_SHIFT)
         == lax.broadcasted_iota(jnp.int32, (HPG, GRP_W), 0)).astype(BF16)
    et = ((lax.broadcasted_iota(jnp.int32, (GRP_W, HPG), 0) >> HEAD_SHIFT)
          == lax.broadcasted_iota(jnp.int32, (GRP_W, HPG), 1)).astype(BF16)
    return e, et


def _expand(v, e_m):
    hi, lo = _split2(v)
    return jnp.dot(hi, e_m, preferred_element_type=F32) + jnp.dot(lo, e_m, preferred_element_type=F32)


def _reduce8(v, et_m):
    hi, lo = _split2(v)
    return jnp.dot(hi, et_m, preferred_element_type=F32) + jnp.dot(lo, et_m, preferred_element_type=F32)


def _ssd_group_terms(dt_ref, dtT_ref, arow_ref, acol_ref):
    L = SSM_CHUNK
    r = lax.broadcasted_iota(jnp.int32, (L, L), 0)
    c = lax.broadcasted_iota(jnp.int32, (L, L), 1)
    tril = (r >= c).astype(BF16)
    triu = (r <= c).astype(BF16)
    dtg = dt_ref[0]
    acol = None
    for p in _split3(dtg * arow_ref[0]):
        t = jnp.dot(tril, p, preferred_element_type=F32)
        acol = t if acol is None else acol + t
    arowT = None
    for p in _split3(dtT_ref[0] * acol_ref[0]):
        t = jnp.dot(p, triu, preferred_element_type=F32)
        arowT = t if arowT is None else arowT + t
    return dtg, acol, arowT, r >= c


def _state_decay(a_last_col, et_m):
    hi, lo = _split2(jnp.broadcast_to(jnp.exp(a_last_col), (HPG, SSM_STATE)))
    return jnp.dot(et_m, hi, preferred_element_type=F32) + jnp.dot(et_m, lo, preferred_element_type=F32)


def _conv_block(x_ref, halo_ref, w_ref, b_ref, first):
    L = SSM_CHUNK
    xx = jnp.concatenate([jnp.where(first, 0.0, halo_ref[...]), x_ref[...]], axis=0)
    w = w_ref[...]
    shifted = [pltpu.roll(xx, SSM_CONV - 1 - k, 0)[8:8 + L] if k < SSM_CONV - 1 else xx[8:8 + L]
               for k in range(SSM_CONV)]
    acc = b_ref[...] + shifted[0] * w[0:1]
    for k in range(1, SSM_CONV):
        acc = acc + shifted[k] * w[k:k + 1]
    return acc, shifted


def _ssd_specs(nc, rev):
    L, N = SSM_CHUNK, SSM_STATE
    xcols = SSM_D_INNER // LANES
    ch = (lambda c: nc - 1 - c) if rev else (lambda c: c)
    above = lambda c: jnp.maximum(ch(c) * (L // 8) - 1, 0)
    specs = []
    for width, col in ((GRP_W, lambda g: g), (N, lambda g: xcols + g), (N, lambda g: xcols + SSM_GROUPS + g)):
        specs += [
            pl.BlockSpec((L, width), lambda c, g, col=col: (ch(c), col(g))),
            pl.BlockSpec((8, width), lambda c, g, col=col: (above(c), col(g))),
            pl.BlockSpec((SSM_CONV, width), lambda c, g, col=col: (0, col(g))),
            pl.BlockSpec((1, width), lambda c, g, col=col: (0, col(g))),
        ]
    return specs + [
        pl.BlockSpec((1, L, HPG), lambda c, g: (g, ch(c), 0)),
        pl.BlockSpec((1, HPG, L), lambda c, g: (g, 0, ch(c))),
        pl.BlockSpec((1, 1, HPG), lambda c, g: (g, 0, 0)),
        pl.BlockSpec((1, HPG, 1), lambda c, g: (g, 0, 0)),
        pl.BlockSpec((1, GRP_W), lambda c, g: (0, g)),
    ]


def ssd_fwd(xbc_raw, conv_w, conv_b, dt_g, dtT_g, a_row, a_col, dvec, nm, hook=None):
    t_dim = xbc_raw.shape[0]
    L, P, N = SSM_CHUNK, SSM_HEAD_DIM, SSM_STATE
    nc = t_dim // L

    def body(x_ref, xh_ref, xw_ref, xb_ref, b_ref, bh_ref, bw_ref, bb_ref, c_ref, ch_ref, cw_ref, cb_ref,
             dt_ref, dtT_ref, arow_ref, acol_ref, dvec_ref, y_ref, st_ref, s_s):
        ci = pl.program_id(0)
        g = pl.program_id(1)

        @pl.when((ci == 0) & (g == 0))
        def _():
            s_s[...] = jnp.zeros_like(s_s)

        e_m, et_m = _expand_mats()
        dtg, acol, arowT, causal = _ssd_group_terms(dt_ref, dtT_ref, arow_ref, acol_ref)
        a_last_row = acol[L - 1:L, :]
        x = _silu(_conv_block(x_ref, xh_ref, xw_ref, xb_ref, ci == 0)[0])
        bm = _silu(_conv_block(b_ref, bh_ref, bw_ref, bb_ref, ci == 0)[0])
        cm = _silu(_conv_block(c_ref, ch_ref, cw_ref, cb_ref, ci == 0)[0])
        cb = _dot(cm, bm, NT)
        s = s_s[g]
        st_ref[0, 0] = s
        ea_x = _expand(jnp.exp(acol), e_m)
        dt_x = _expand(dtg, e_m)
        w_x = _expand(jnp.exp(a_last_row - acol) * dtg, e_m)
        yb = ea_x * _dot(cm, s, NT) + dvec_ref[...] * x
        xd = (x * dt_x).astype(BF16)
        for e in range(HPG):
            sl = slice(e * P, (e + 1) * P)
            lm = jnp.exp(jnp.where(causal, acol[:, e:e + 1] - arowT[e:e + 1, :], NEG))
            m = (cb * lm).astype(BF16)
            y_ref[:, sl] = yb[:, sl] + jnp.dot(m, xd[:, sl], preferred_element_type=F32)
        s_s[g] = _state_decay(arowT[:, L - 1:L], et_m) * s + _dot(x * w_x, bm, TN)

    out_specs = [
        pl.BlockSpec((L, GRP_W), lambda c, g: (c, g)),
        pl.BlockSpec((1, 1, GRP_W, N), lambda c, g: (c, g, 0, 0)),
    ]
    return pcall(
        body, name=nm, grid=(nc, SSM_GROUPS), in_specs=_ssd_specs(nc, False), out_specs=out_specs,
        out_shape=[jax.ShapeDtypeStruct((t_dim, SSM_D_INNER), F32),
                   jax.ShapeDtypeStruct((nc, SSM_GROUPS, GRP_W, N), F32)],
        scratch=[pltpu.VMEM((SSM_GROUPS, GRP_W, N), F32)],
        args=[xbc_raw, xbc_raw, conv_w, conv_b] * 3 + [dt_g, dtT_g, a_row, a_col, dvec], hook=hook)


def ssd_bwd(dy, xbc_raw, conv_w, conv_b, dt_g, dtT_g, a_row, a_col, dvec, states, nm, hook=None):
    t_dim = xbc_raw.shape[0]
    L, P, N = SSM_CHUNK, SSM_HEAD_DIM, SSM_STATE
    nc = t_dim // L

    def body(dy_ref, x_ref, xh_ref, xw_ref, xb_ref, b_ref, bh_ref, bw_ref, bb_ref, c_ref, ch_ref, cw_ref, cb_ref,
             dt_ref, dtT_ref, arow_ref, acol_ref, dvec_ref, st_ref,
             dx_ref, db_ref, dc_ref, da_ref, ddt_ref, dd_ref, dwx_ref, dwb_ref, dwc_ref, dbx_ref, dbb_ref, dbc_ref,
             ds_s, yd_s, dxd_s):
        ci = pl.program_id(0)
        g = pl.program_id(1)

        @pl.when((ci == 0) & (g == 0))
        def _():
            ds_s[...] = jnp.zeros_like(ds_s)
            for r in (dd_ref, dwx_ref, dwb_ref, dwc_ref, dbx_ref, dbb_ref, dbc_ref):
                r[...] = jnp.zeros_like(r)

        e_m, et_m = _expand_mats()
        dtg, acol, arowT, causal = _ssd_group_terms(dt_ref, dtT_ref, arow_ref, acol_ref)
        a_last_row = acol[L - 1:L, :]
        first = ci == nc - 1
        pre_x, sh_x = _conv_block(x_ref, xh_ref, xw_ref, xb_ref, first)
        pre_b, sh_b = _conv_block(b_ref, bh_ref, bw_ref, bb_ref, first)
        pre_c, sh_c = _conv_block(c_ref, ch_ref, cw_ref, cb_ref, first)
        sg_x, sg_b, sg_c = _sigmoid(pre_x), _sigmoid(pre_b), _sigmoid(pre_c)
        x = pre_x * sg_x
        dy = dy_ref[...]
        bm = pre_b * sg_b
        cm = pre_c * sg_c
        cb = _dot(cm, bm, NT)
        s = st_ref[0, 0]
        dsp = ds_s[g]
        ew8 = jnp.exp(a_last_row - acol)
        ea_x = _expand(jnp.exp(acol), e_m)
        dt_x = _expand(dtg, e_m)
        ew_x = _expand(ew8, e_m)
        w_x = ew_x * dt_x
        z = _dot(cm, s, NT)
        dz = ea_x * dy
        dc = _dot(dz, s)
        ds_y = _dot(dz, cm, TN)
        du = _dot(bm, dsp, NT)
        u = x * w_x
        db = _dot(u, dsp)
        xd = (x * dt_x).astype(BF16)
        dyb = dy.astype(BF16)
        dcb = jnp.zeros((L, L), F32)
        for e in range(HPG):
            sl = slice(e * P, (e + 1) * P)
            lm = jnp.exp(jnp.where(causal, acol[:, e:e + 1] - arowT[e:e + 1, :], NEG))
            m = (cb * lm).astype(BF16)
            yd_s[:, sl] = jnp.dot(m, xd[:, sl], preferred_element_type=F32)
            dxd_s[:, sl] = lax.dot_general(m, dyb[:, sl], TN, preferred_element_type=F32)
            dcb = dcb + lax.dot_general(dyb[:, sl], xd[:, sl], NT, preferred_element_type=F32) * lm
        dxd = dxd_s[...]

        def through_conv(d_act, pre, sg, shifted, d_ref, dw_ref, dbias_ref):
            d_pre = d_act * (sg * (1.0 + pre * (1.0 - sg)))
            d_ref[...] = d_pre
            dw_ref[g] += jnp.concatenate([jnp.sum(d_pre * sh, axis=0, keepdims=True) for sh in shifted], axis=0)
            dbias_ref[g] += jnp.sum(d_pre, axis=0, keepdims=True)

        through_conv(dvec_ref[...] * dy + du * w_x + dt_x * dxd, pre_x, sg_x, sh_x, dx_ref, dwx_ref, dbx_ref)
        ddt = _reduce8(x * (ew_x * du + dxd), et_m)
        da = (_reduce8(dz * z + dyb.astype(F32) * yd_s[...], et_m)
              - _reduce8(xd.astype(F32) * dxd + du * u, et_m))
        dwa_row = _reduce8(jnp.broadcast_to(jnp.sum(du * u, axis=0, keepdims=True), (8, GRP_W)), et_m)[0:1]
        t_nh = None
        for p in _split3(dsp * s):
            t = lax.dot_general(p, et_m, TN, preferred_element_type=F32)
            t_nh = t if t_nh is None else t_nh + t
        d_last = dwa_row + jnp.exp(a_last_row) * jnp.sum(t_nh, axis=0, keepdims=True)
        row_l = lax.broadcasted_iota(jnp.int32, (L, 1), 0)
        da_ref[0] = da + jnp.where(row_l == L - 1, d_last, 0.0)
        ddt_ref[0] = ddt
        dd_ref[g] += jnp.sum(dy * x, axis=0, keepdims=True)
        through_conv(dc + _dot(dcb, bm), pre_c, sg_c, sh_c, dc_ref, dwc_ref, dbc_ref)
        through_conv(db + _dot(dcb, cm, TN), pre_b, sg_b, sh_b, db_ref, dwb_ref, dbb_ref)
        ds_s[g] = _state_decay(arowT[:, L - 1:L], et_m) * dsp + ds_y

    rc = lambda c: nc - 1 - c
    in_specs = ([pl.BlockSpec((L, GRP_W), lambda c, g: (rc(c), g))] + _ssd_specs(nc, True)
                + [pl.BlockSpec((1, 1, GRP_W, N), lambda c, g: (rc(c), g, 0, 0))])
    whole = lambda *shape: pl.BlockSpec(shape, lambda c, g: (0,) * len(shape))
    out_specs = [
        pl.BlockSpec((L, GRP_W), lambda c, g: (rc(c), g)),
        pl.BlockSpec((L, N), lambda c, g: (rc(c), g)),
        pl.BlockSpec((L, N), lambda c, g: (rc(c), g)),
        pl.BlockSpec((1, L, HPG), lambda c, g: (g, rc(c), 0)),
        pl.BlockSpec((1, L, HPG), lambda c, g: (g, rc(c), 0)),
        whole(SSM_GROUPS, 1, GRP_W),
        whole(SSM_GROUPS, SSM_CONV, GRP_W), whole(SSM_GROUPS, SSM_CONV, N), whole(SSM_GROUPS, SSM_CONV, N),
        whole(SSM_GROUPS, 1, GRP_W), whole(SSM_GROUPS, 1, N), whole(SSM_GROUPS, 1, N),
    ]
    gn = SSM_GROUPS * N
    acc = lambda *shape: jax.ShapeDtypeStruct(shape, F32)
    out_shape = [
        acc(t_dim, SSM_D_INNER), acc(t_dim, gn), acc(t_dim, gn), acc(SSM_GROUPS, t_dim, HPG),
        acc(SSM_GROUPS, t_dim, HPG), acc(SSM_GROUPS, 1, GRP_W),
        acc(SSM_GROUPS, SSM_CONV, GRP_W), acc(SSM_GROUPS, SSM_CONV, N), acc(SSM_GROUPS, SSM_CONV, N),
        acc(SSM_GROUPS, 1, GRP_W), acc(SSM_GROUPS, 1, N), acc(SSM_GROUPS, 1, N),
    ]
    return pcall(
        body, name=nm, grid=(nc, SSM_GROUPS), in_specs=in_specs, out_specs=out_specs, out_shape=out_shape,
        scratch=[pltpu.VMEM((SSM_GROUPS, GRP_W, N), F32), pltpu.VMEM((L, GRP_W), F32), pltpu.VMEM((L, GRP_W), F32)],
        args=[dy] + [xbc_raw, xbc_raw, conv_w, conv_b] * 3 + [dt_g, dtT_g, a_row, a_col, dvec, states], hook=hook)


def _softplus(x):
    return jnp.maximum(x, 0.0) + jnp.log(1.0 + jnp.exp(-jnp.abs(x)))


def ssd_dt_bwd(da, ddt, dt, dt_raw, a_row, dt_bias, nm):
    L = SSM_CHUNK

    def fn(d_a, d_dt, dtv, raw, ar, bias):
        r = lax.broadcasted_iota(jnp.int32, (L, L), 0)
        c = lax.broadcasted_iota(jnp.int32, (L, L), 1)
        triu = (r <= c).astype(BF16)
        acc = None
        for p in _split3(d_a):
            t = jnp.dot(triu, p, preferred_element_type=F32)
            acc = t if acc is None else acc + t
        d_dt = d_dt + acc * ar
        d_a_h = jnp.sum(acc * dtv, axis=0, keepdims=True)
        d_raw = d_dt * _sigmoid(raw + bias)
        return d_raw, d_a_h, jnp.sum(d_raw, axis=0, keepdims=True)

    return rowmap(fn, [da, ddt, dt, dt_raw], [a_row, dt_bias], [(SSM_HEADS, BF16)],
                  [(1, SSM_HEADS), (1, SSM_HEADS)], tm=L, name=nm)


GN_W = SSM_D_INNER // SSM_GROUPS


def mamba_fwd(h, p, nm, io):
    def in_proj(x, gg, w_zt, w_xbct, w_dtt):
        uu = _rms_fwd(x, gg).astype(BF16)
        return uu, _dot(uu, w_zt, NT), _dot(uu, w_xbct, NT), _dot(uu, w_dtt, NT)

    u, z, xbc_raw, dt_raw = rowmap(in_proj, [h], [p["ssm_norm"], p["w_zt"], p["w_xbct"], p["w_dtt"]],
                                   [(D_MODEL, BF16), (SSM_D_INNER, F32), (SSM_CONV_DIM, F32), (SSM_HEADS, F32)],
                                   tm=RESIDENT_TM, name=nm + "_in", hook=io.hook(nm + "_in"))
    dt, = rowmap(lambda r, b: (_softplus(r + b),), [dt_raw], [p["dt_bias"]], [(SSM_HEADS, F32)], tm=256,
                 name=nm + "_softplus")
    dt_g = dt.reshape(-1, SSM_GROUPS, HPG).transpose(1, 0, 2)
    dtT_g = dt_g.transpose(0, 2, 1)
    y, states = ssd_fwd(xbc_raw, p["conv_w"], p["conv_b"], dt_g, dtT_g, p["a_row"], p["a_col"], p["dvec"],
                        nm + "_ssd", hook=io.hook(nm + "_ssd"))

    def gate_norm_out(yv, zv, hv, gg, w_out):
        t = yv * _silu(zv)
        yn = jnp.concatenate([_rms_fwd(t[:, k * GN_W:(k + 1) * GN_W], gg[:, k * GN_W:(k + 1) * GN_W])
                              for k in range(SSM_GROUPS)], axis=1).astype(BF16)
        return yn, hv + _dot(yn, w_out)

    yn, out = rowmap(gate_norm_out, [y, z, h], [p["gate_norm"], p["w_out"]],
                     [(SSM_D_INNER, BF16), (D_MODEL, F32)], tm=RESIDENT_TM, name=nm + "_out")
    return out, (u, z, xbc_raw, dt_raw, dt, dt_g, dtT_g, y, states, yn)


def mamba_bwd(h, p, saved, dout, dout_bf, nm, io):
    u, z, xbc_raw, dt_raw, dt, dt_g, dtT_g, y, states, yn = saved
    g = {}
    io.put("w_out", mm(yn, dout_bf, ta=True, out_dtype=BF16, name=nm + "_dwout"))

    def gate_norm_bwd(d_o, yv, zv, gg, w_out):
        d = _dot(d_o, w_out, NT)
        sz = _silu(zv)
        t = yv * sz
        dts, dgs = [], []
        for k in range(SSM_GROUPS):
            sl = slice(k * GN_W, (k + 1) * GN_W)
            dt_k, dg_k = _rms_bwd(t[:, sl], gg[:, sl], d[:, sl])
            dts.append(dt_k)
            dgs.append(dg_k)
        d_t = jnp.concatenate(dts, axis=1)
        return d_t * sz, d_t * yv * _silu_grad(zv), jnp.concatenate(dgs, axis=1)

    dy, dz, g["gate_norm"] = rowmap(gate_norm_bwd, [dout_bf, y, z], [p["gate_norm"], p["w_out"]],
                                    [(SSM_D_INNER, F32), (SSM_D_INNER, BF16)], [(1, SSM_D_INNER)], tm=256,
                                    name=nm + "_dgatenorm")
    d_x, d_b, d_c, da_g, ddt_g, dd, dwx, dwb, dwc, dbx, dbb, dbc = ssd_bwd(
        dy, xbc_raw, p["conv_w"], p["conv_b"], dt_g, dtT_g, p["a_row"], p["a_col"], p["dvec"], states, nm + "_dssd",
        hook=io.hook(nm + "_dssd"))
    g["dvec"] = dd
    by_lane = lambda t: t.transpose(1, 0, 2).reshape(t.shape[1], -1)
    g["conv_w"] = jnp.concatenate([by_lane(dwx), by_lane(dwb), by_lane(dwc)], axis=1)
    g["conv_b"] = jnp.concatenate([by_lane(dbx), by_lane(dbb), by_lane(dbc)], axis=1)
    per_head = lambda t: t.transpose(1, 0, 2).reshape(-1, SSM_HEADS)
    ddt_raw, g["a"], g["dt_bias"] = ssd_dt_bwd(per_head(da_g), per_head(ddt_g), dt, dt_raw, p["a_heads"],
                                               p["dt_bias"], nm + "_ddt")
    dxbc_raw = conv_input_grad([d_x, d_b, d_c], p["conv_w"], nm + "_dconv")
    io.put("w_int", jnp.concatenate([mm(dz, u, ta=True, out_dtype=BF16, name=nm + "_dwz"),
                                     mm(dxbc_raw, u, ta=True, out_dtype=BF16, name=nm + "_dwxbc"),
                                     mm(ddt_raw, u, ta=True, out_dtype=BF16, name=nm + "_dwdt")], axis=0))
    dh, dh_bf, g["ssm_norm"] = du_norm_bwd(
        [(dz, p["w_zt"], False), (dxbc_raw, p["w_xbct"], False), (ddt_raw, p["w_dtt"], False)],
        h, p["ssm_norm"], dout, nm + "_du", hook=io.hook(nm + "_du"))
    return dh, dh_bf, g


KV_W = ATT_KV_HEADS * ATT_HEAD_DIM


def kv_fwd(h, p, nm):
    def kv_proj(x, gg, w_kv, gk):
        uu = _rms_fwd(x, gg).astype(BF16)
        t = _dot(uu, w_kv)
        ks = [_rms_fwd(t[:, j * ATT_HEAD_DIM:(j + 1) * ATT_HEAD_DIM], gk) for j in range(ATT_KV_HEADS)]
        return uu, t, jnp.concatenate(ks, axis=1), t[:, KV_W:]

    u, kv_raw, k, v = rowmap(kv_proj, [h], [p["kv_norm"], p["w_kv"], p["k_norm"]],
                             [(D_MODEL, BF16), (2 * KV_W, F32), (KV_W, F32), (KV_W, F32)], tm=RESIDENT_TM,
                             name=nm + "_proj")
    return k, v, (u, kv_raw)


def kv_bwd(h, p, saved, dk_cur, dk_prev, dv_cur, dv_prev, dout, nm, io):
    u, kv_raw = saved
    t_dim = h.shape[0]
    tm = ATT_WINDOW
    nb = t_dim // tm
    nxt = pl.BlockSpec((tm, KV_W), lambda i: (jnp.minimum(i + 1, nb - 1), 0))

    def fn(dkc, dkp, dvc, dvp, t, gg):
        live = pl.program_id(0) < nb - 1
        dk = dkc + jnp.where(live, dkp, 0.0)
        dv = dvc + jnp.where(live, dvp, 0.0)
        outs, dgs = [], None
        for j in range(ATT_KV_HEADS):
            sl = slice(j * ATT_HEAD_DIM, (j + 1) * ATT_HEAD_DIM)
            dx, dg = _rms_bwd(t[:, sl], gg, dk[:, sl])
            outs.append(dx)
            dgs = dg if dgs is None else dgs + dg
        return jnp.concatenate(outs + [dv], axis=1), dgs

    dkv_raw, dknorm = rowmap(fn, [dk_cur, (dk_prev, nxt), dv_cur, (dv_prev, nxt), kv_raw], [p["k_norm"]],
                             [(2 * KV_W, BF16)], [(1, ATT_HEAD_DIM)], tm=tm, name=nm + "_dknorm",
                             hook=io.hook(nm + "_dknorm"))
    g = {"k_norm": dknorm}
    io.put("w_kv", mm(u, dkv_raw, ta=True, out_dtype=BF16, name=nm + "_dwkv"))
    dh, dh_bf, g["kv_norm"] = du_norm_bwd([(dkv_raw, p["w_kv"], True)], h, p["kv_norm"], dout, nm + "_du",
                                          hook=io.hook(nm + "_du"))
    return dh, dh_bf, g


def _attn_scores(q_ref, kp_ref, kc_ref, vp_ref, vc_ref, qn_ref, bias_ref, sink_ref, kv, mxu_sum):
    hd = ATT_HEAD_DIM
    blk = ATT_WINDOW
    sl = slice(kv * hd, (kv + 1) * hd)
    kk = jnp.concatenate([kp_ref[:, sl], kc_ref[:, sl]], axis=0)
    vv = jnp.concatenate([vp_ref[:, sl], vc_ref[:, sl]], axis=0)
    gq = qn_ref[...]
    raws, rinvs = [], []
    for r in range(ATT_GROUP):
        hh = kv * ATT_GROUP + r
        x = q_ref[:, hh * hd:(hh + 1) * hd]
        raws.append(x)
        rinvs.append(lax.rsqrt(jnp.mean(x * x, axis=-1, keepdims=True) + EPS))
    xh = jnp.concatenate([x * ri for x, ri in zip(raws, rinvs)], axis=0)
    rinv = jnp.concatenate(rinvs, axis=0)
    q8 = xh * gq
    s = _dot(q8, kk, NT) * (hd ** -0.5) + bias_ref[kv]
    colk = lax.broadcasted_iota(jnp.int32, (1, 2 * blk), 1)
    s = jnp.where((pl.program_id(0) > 0) | (colk >= blk), s, NEG)
    sink = sink_ref[kv]
    m = jnp.maximum(jnp.max(s, axis=-1, keepdims=True), sink)
    pexp = jnp.exp(s - m)
    e_sink = jnp.exp(sink - m)
    if not mxu_sum:
        inv_den = 1.0 / (jnp.sum(pexp, axis=-1, keepdims=True) + e_sink)
        return kk, vv, xh, rinv, q8, pexp * inv_den, e_sink * inv_den
    ones = jnp.ones((2 * blk, LANES), BF16)
    inv_den = 1.0 / (jnp.dot(pexp.astype(BF16), ones, preferred_element_type=F32) + e_sink)
    return kk, vv, xh, rinv, q8, pexp * jnp.concatenate([inv_den, inv_den], axis=1), e_sink * inv_den[:, :1]


def _attn_specs(nb):
    blk = ATT_WINDOW
    cur = lambda i: (i, 0)
    prev = lambda i: (jnp.maximum(i - 1, 0), 0)
    return [
        pl.BlockSpec((blk, D_MODEL), cur),
        pl.BlockSpec((blk, KV_W), prev), pl.BlockSpec((blk, KV_W), cur),
        pl.BlockSpec((blk, KV_W), prev), pl.BlockSpec((blk, KV_W), cur),
        pl.BlockSpec((1, ATT_HEAD_DIM), lambda i: (0, 0)),
        pl.BlockSpec((ATT_KV_HEADS, ATT_GROUP * blk, 2 * blk), lambda i: (0, 0, 0)),
        pl.BlockSpec((ATT_KV_HEADS, ATT_GROUP * blk, 1), lambda i: (0, 0, 0)),
    ]


def attn_fwd(q_raw, k, v, q_norm, bias, sink_col, nm):
    t_dim = q_raw.shape[0]
    blk, hd = ATT_WINDOW, ATT_HEAD_DIM
    nb = t_dim // blk

    def body(q_ref, kp_ref, kc_ref, vp_ref, vc_ref, qn_ref, bias_ref, sink_ref, o_ref):
        for kv in range(ATT_KV_HEADS):
            kk, vv, xh, rinv, q8, prob, p_sink = _attn_scores(q_ref, kp_ref, kc_ref, vp_ref, vc_ref, qn_ref,
                                                              bias_ref, sink_ref, kv, False)
            o8 = _dot(prob, vv)
            for r in range(ATT_GROUP):
                hh = kv * ATT_GROUP + r
                o_ref[:, hh * hd:(hh + 1) * hd] = o8[r * blk:(r + 1) * blk].astype(o_ref.dtype)

    out, = pcall(body, name=nm, grid=(nb,), in_specs=_attn_specs(nb),
                 out_specs=[pl.BlockSpec((blk, D_MODEL), lambda i: (i, 0))],
                 out_shape=[jax.ShapeDtypeStruct((t_dim, D_MODEL), BF16)],
                 args=[q_raw, k, k, v, v, q_norm, bias, sink_col])
    return out


def attn_bwd(do, q_raw, k, v, q_norm, bias, sink_col, nm, hook=None):
    t_dim = q_raw.shape[0]
    blk, hd = ATT_WINDOW, ATT_HEAD_DIM
    nb = t_dim // blk
    scale = hd ** -0.5

    def body(do_ref, q_ref, kp_ref, kc_ref, vp_ref, vc_ref, qn_ref, bias_ref, sink_ref,
             dq_ref, dkc_ref, dkp_ref, dvc_ref, dvp_ref, dbias_ref, dsink_ref, dqn_ref):
        @pl.when(pl.program_id(0) == 0)
        def _():
            dbias_ref[...] = jnp.zeros_like(dbias_ref)
            dsink_ref[...] = jnp.zeros_like(dsink_ref)
            dqn_ref[...] = jnp.zeros_like(dqn_ref)

        gq = qn_ref[...]
        for kv in range(ATT_KV_HEADS):
            kk, vv, xh, rinv, q8, prob, p_sink = _attn_scores(q_ref, kp_ref, kc_ref, vp_ref, vc_ref, qn_ref,
                                                              bias_ref, sink_ref, kv, True)
            do8 = jnp.concatenate([do_ref[:, (kv * ATT_GROUP + r) * hd:(kv * ATT_GROUP + r + 1) * hd]
                                   for r in range(ATT_GROUP)], axis=0)
            dp = _dot(do8, vv, NT)
            delta = jnp.sum(prob * dp, axis=-1, keepdims=True)
            ds = prob * (dp - delta)
            dsink_ref[kv] += -p_sink * delta
            dbias_ref[kv] += ds
            ds_s = ds * scale
            dq8 = _dot(ds_s, kk)
            dkk = _dot(ds_s, q8, TN)
            dvv = _dot(prob, do8, TN)
            dqn_ref[...] += jnp.sum(dq8 * xh, axis=0, keepdims=True)
            dxh = dq8 * gq
            dq_raw8 = rinv * (dxh - xh * jnp.mean(dxh * xh, axis=-1, keepdims=True))
            for r in range(ATT_GROUP):
                hh = kv * ATT_GROUP + r
                dq_ref[:, hh * hd:(hh + 1) * hd] = dq_raw8[r * blk:(r + 1) * blk].astype(dq_ref.dtype)
            sl = slice(kv * hd, (kv + 1) * hd)
            dkp_ref[:, sl] = dkk[:blk]
            dkc_ref[:, sl] = dkk[blk:]
            dvp_ref[:, sl] = dvv[:blk]
            dvc_ref[:, sl] = dvv[blk:]

    cur = lambda i: (i, 0)
    row_spec = pl.BlockSpec((blk, KV_W), cur)
    out_specs = [
        pl.BlockSpec((blk, D_MODEL), cur), row_spec, row_spec, row_spec, row_spec,
        pl.BlockSpec((ATT_KV_HEADS, ATT_GROUP * blk, 2 * blk), lambda i: (0, 0, 0)),
        pl.BlockSpec((ATT_KV_HEADS, ATT_GROUP * blk, 1), lambda i: (0, 0, 0)),
        pl.BlockSpec((1, hd), lambda i: (0, 0)),
    ]
    kvs = jax.ShapeDtypeStruct((t_dim, KV_W), F32)
    out_shape = [
        jax.ShapeDtypeStruct((t_dim, D_MODEL), BF16), kvs, kvs, kvs, kvs,
        jax.ShapeDtypeStruct((ATT_KV_HEADS, ATT_GROUP * blk, 2 * blk), F32),
        jax.ShapeDtypeStruct((ATT_KV_HEADS, ATT_GROUP * blk, 1), F32),
        jax.ShapeDtypeStruct((1, hd), F32),
    ]
    return pcall(body, name=nm, grid=(nb,), in_specs=[pl.BlockSpec((blk, D_MODEL), cur)] + _attn_specs(nb),
                 out_specs=out_specs, out_shape=out_shape,
                 args=[do, q_raw, k, k, v, v, q_norm, bias, sink_col], hook=hook)


def _t5_bucket_np():
    blk = ATT_WINDOW
    qi = np.arange(blk)[:, None] + blk
    kj = np.arange(2 * blk)[None, :]
    dist = qi - kj
    n = np.maximum(dist, 0)
    max_exact = REL_BUCKETS // 2
    nf = np.maximum(n, 1).astype(np.float32)
    large = max_exact + (np.log(nf / max_exact) / math.log(ATT_WINDOW / max_exact)
                         * (REL_BUCKETS - max_exact)).astype(np.int32)
    large = np.minimum(large, REL_BUCKETS - 1)
    bucket = np.where(n < max_exact, n, large)
    in_window = (dist >= 0) & (dist < ATT_WINDOW)
    return bucket, in_window


def attn_block_fwd(h, k, v, p, nm):
    def q_proj(x, gg, w_q):
        uu = _rms_fwd(x, gg).astype(BF16)
        return uu, _dot(uu, w_q)

    u, q_raw = rowmap(q_proj, [h], [p["attn_norm"], p["w_q"]], [(D_MODEL, BF16), (D_MODEL, F32)], tm=RESIDENT_TM,
                      name=nm + "_q")
    o = attn_fwd(q_raw, k, v, p["q_norm"], p["bias"], p["sink_col"], nm + "_core")
    out = mm(o, p["w_o"], res=h, name=nm + "_o")
    return out, (u, q_raw, o)


def attn_block_bwd(h, k, v, p, saved, dout, dout_bf, nm, io):
    u, q_raw, o = saved
    g = {}
    io.put("w_o", mm(o, dout_bf, ta=True, out_dtype=BF16, name=nm + "_dwo", hook=io.hook(nm + "_dwo")))
    do = mm(dout_bf, p["w_o"], tb=True, name=nm + "_do")
    dq_raw, dkc, dkp, dvc, dvp, g["bias"], g["sink_col"], g["q_norm"] = attn_bwd(
        do, q_raw, k, v, p["q_norm"], p["bias"], p["sink_col"], nm + "_dcore", hook=io.hook(nm + "_dcore"))
    io.put("w_q", mm(u, dq_raw, ta=True, out_dtype=BF16, name=nm + "_dwq"))
    dh, dh_bf, g["attn_norm"] = du_norm_bwd([(dq_raw, p["w_q"], True)], h, p["attn_norm"], dout, nm + "_du")
    return dh, dh_bf, g, (dkc, dkp, dvc, dvp)


FFN_TAGS = ["00", "01", "10", "11"]


def local_step(x, target, small, io):
    bucket, in_window = _t5_bucket_np()
    blk = ATT_WINDOW
    w = small

    fnorm = {tag: w["ffn_norm"][int(tag[0]), int(tag[1])][None, :] for tag in FFN_TAGS}
    a_neg = -jnp.exp(w["ssm_a_log"][0])

    def mamba_p():
        w_int = io.w("w_int")
        return dict(ssm_norm=w["ssm_norm"], w_zt=w_int[:SSM_D_INNER],
                    w_xbct=w_int[SSM_D_INNER:SSM_D_INNER + SSM_CONV_DIM], w_dtt=w_int[SSM_D_INNER + SSM_CONV_DIM:],
                    conv_w=w["ssm_conv_w"][0], conv_b=w["ssm_conv_b"], dt_bias=w["ssm_dt_bias"],
                    a_heads=a_neg[None, :], a_row=a_neg.reshape(SSM_GROUPS, 1, HPG),
                    a_col=a_neg.reshape(SSM_GROUPS, HPG, 1),
                    dvec=jnp.repeat(w["ssm_d"][0], SSM_HEAD_DIM)[None, :],
                    gate_norm=w["ssm_gate_norm"], w_out=io.w("w_out"))

    rb = w["rel_bias"]
    onehot3 = (np.arange(REL_BUCKETS)[:, None, None] == bucket[None]).astype(np.float32)
    bias = jnp.einsum("bh,bqk->hqk", rb, onehot3, precision=lax.Precision.HIGHEST)
    bias = jnp.where(in_window[None], bias, NEG)
    bias = bias.reshape(ATT_KV_HEADS, ATT_GROUP * blk, 2 * blk)
    sink_col = jnp.repeat(w["sinks"][0], blk).reshape(ATT_KV_HEADS, ATT_GROUP * blk, 1)

    def attn_p():
        return dict(attn_norm=w["attn_norm"], w_q=io.w("w_q"), q_norm=w["q_norm"], bias=bias, sink_col=sink_col,
                    w_o=io.w("w_o"))

    def kv_p():
        return dict(kv_norm=w["kv_norm"][None, :], w_kv=io.w("w_kv"), k_norm=w["k_norm"][None, :])

    h0 = x
    h0a, s_f00 = ffn_fwd(h0, fnorm["00"], "00", io)
    mp = mamba_p()
    h0b, s_m = mamba_fwd(h0a, mp, "ssm", io)
    h1, s_f01 = ffn_fwd(h0b, fnorm["01"], "01", io)
    kp = kv_p()
    k, v, s_kv = kv_fwd(h1, kp, "kv")
    h1a, s_f10 = ffn_fwd(h1, fnorm["10"], "10", io)
    ap = attn_p()
    h1b, s_a = attn_block_fwd(h1a, k, v, ap, "att")
    (dh, dh_bf, sq), s_f11 = ffn_fwd(h1b, fnorm["11"], "11", io, target=target)
    loss_part = jnp.sum(sq) * (0.5 / D_MODEL)

    fg = {}

    def ffn_back(tag, h_in, saved, dh, dh_bf):
        dh, dh_bf, dg = ffn_bwd(h_in, fnorm[tag], tag, saved, dh, dh_bf, io)
        fg[tag] = dg[0]
        return dh, dh_bf

    dh, dh_bf = ffn_back("11", h1b, s_f11, dh, dh_bf)
    dh, dh_bf, ga, dkv = attn_block_bwd(h1a, k, v, ap, s_a, dh, dh_bf, "att", io)
    dh, dh_bf = ffn_back("10", h1, s_f10, dh, dh_bf)
    dh, dh_bf, gk = kv_bwd(h1, kp, s_kv, *dkv, dh, "kv", io)
    dh, dh_bf = ffn_back("01", h0b, s_f01, dh, dh_bf)
    dh, dh_bf, gm = mamba_bwd(h0a, mp, s_m, dh, dh_bf, "ssm", io)
    dh, dh_bf = ffn_back("00", h0, s_f00, dh, dh_bf)
    grad_x = dh

    grads = {}
    grads["ffn_norm"] = jnp.stack([fg[tag] for tag in FFN_TAGS]).reshape(2, 2, D_MODEL)
    grads["ssm_norm"] = gm["ssm_norm"]
    grads["ssm_conv_w"] = gm["conv_w"][None]
    grads["ssm_conv_b"] = gm["conv_b"]
    grads["ssm_dt_bias"] = gm["dt_bias"]
    grads["ssm_a_log"] = gm["a"] * a_neg[None, :]
    grads["ssm_d"] = jnp.sum(gm["dvec"].reshape(SSM_HEADS, SSM_HEAD_DIM), axis=1)[None, :]
    grads["ssm_gate_norm"] = gm["gate_norm"]
    grads["kv_norm"] = gk["kv_norm"][0]
    grads["k_norm"] = gk["k_norm"][0]
    grads["attn_norm"] = ga["attn_norm"]
    grads["q_norm"] = ga["q_norm"]
    grads["sinks"] = jnp.sum(ga["sink_col"].reshape(ATT_HEADS, blk), axis=1)[None, :]
    onehot = (np.arange(REL_BUCKETS)[:, None] == bucket.reshape(1, -1)).astype(np.float32)
    dbias2d = ga["bias"].reshape(ATT_HEADS, blk * 2 * blk)
    grads["rel_bias"] = mm(jnp.asarray(onehot, BF16), dbias2d, tb=True, name="drelbias")
    return loss_part, grad_x, grads


def _adamw(g, w, m, v):
    m = ADAM_B1 * m + (1.0 - ADAM_B1) * g
    v = ADAM_B2 * v + (1.0 - ADAM_B2) * (g * g)
    m_hat = m / (1.0 - ADAM_B1 ** ADAM_STEP)
    v_hat = v / (1.0 - ADAM_B2 ** ADAM_STEP)
    delta = -ADAM_LR * (m_hat / (jnp.sqrt(v_hat) + ADAM_EPS) + ADAM_WD * w)
    return delta, m, v


def _slot_sum(r):
    g = r[0].astype(F32)
    for d in range(1, r.shape[0]):
        g = g + r[d].astype(F32)
    return g


def adamw_rows(recvs, w, m, v, name):
    n_l, rows, width = w.shape
    n_slots = recvs[0].shape[0]
    tr = 32
    assert rows % tr == 0, rows
    nt = rows // tr

    def body(*refs):
        r_refs = refs[:n_l]
        w_ref, m_ref, v_ref, g_o, d_o, m_o, v_o = refs[n_l:]
        li = pl.program_id(0)
        for k in range(n_l):
            @pl.when(li == k)
            def _(k=k):
                g = _slot_sum(r_refs[k])
                delta, m2, v2 = _adamw(g, w_ref[0], m_ref[0], v_ref[0])
                g_o[0] = g
                d_o[0] = delta
                m_o[0] = m2
                v_o[0] = v2

    def r_spec(k):
        return pl.BlockSpec((n_slots, tr, width),
                            lambda li, j: (0, jnp.where(li == k, j, jnp.where(li > k, nt - 1, 0)), 0))

    w_spec = pl.BlockSpec((1, tr, width), lambda li, j: (li, j, 0))
    shp = jax.ShapeDtypeStruct(w.shape, F32)
    return pcall(body, name=name, grid=(n_l, nt), in_specs=[r_spec(k) for k in range(n_l)] + [w_spec] * 3,
                 out_specs=[w_spec] * 4, out_shape=[shp] * 4, args=list(recvs) + [w, m, v])


def adamw_cols(recvs, w, m, v, name):
    n_l, rows, n = w.shape
    n_slots = recvs[0].shape[0]
    tr = 256
    nt = rows // tr

    def body(*refs):
        r_refs = refs[:n_l]
        w_ref, m_ref, v_ref, g_o, d_o, m_o, v_o = refs[n_l:]
        li = pl.program_id(0)
        for k in range(n_l):
            @pl.when(li == k)
            def _(k=k):
                g = _slot_sum(r_refs[k]).T
                delta, m2, v2 = _adamw(g, w_ref[0], m_ref[0], v_ref[0])
                g_o[0] = g
                d_o[0] = delta
                m_o[0] = m2
                v_o[0] = v2

    def r_spec(k):
        return pl.BlockSpec((n_slots, n, tr),
                            lambda li, j: (0, 0, jnp.where(li == k, j, jnp.where(li > k, nt - 1, 0))))

    w_spec = pl.BlockSpec((1, tr, n), lambda li, j: (li, j, 0))
    shp = jax.ShapeDtypeStruct(w.shape, F32)
    return pcall(body, name=name, grid=(n_l, nt), in_specs=[r_spec(k) for k in range(n_l)] + [w_spec] * 3,
                 out_specs=[w_spec] * 4, out_shape=[shp] * 4, args=list(recvs) + [w, m, v])


WEIGHT_NAMES = ["ffn_norm", "ffn_w1", "ffn_w3", "ffn_w2", "ssm_norm", "ssm_w_in", "ssm_conv_w", "ssm_conv_b",
                "ssm_dt_bias", "ssm_a_log", "ssm_d", "ssm_gate_norm", "ssm_w_out", "kv_norm", "w_kv", "k_norm",
                "attn_norm", "w_q", "q_norm", "sinks", "w_o", "rel_bias"]

SMALL = [
    ("ffn_norm", (2, 2, 1024), 2), ("ssm_norm", (1, 1024), 1), ("ssm_conv_w", (1, 4, 3072), 2),
    ("ssm_conv_b", (1, 3072), 1), ("ssm_gate_norm", (1, 2048), 1),
    ("ssm_dt_bias", (1, 32), None), ("ssm_a_log", (1, 32), None), ("ssm_d", (1, 32), None),
    ("kv_norm", (1024,), None), ("k_norm", (64,), None), ("attn_norm", (1, 1024), None),
    ("q_norm", (1, 64), None), ("sinks", (1, 16), None), ("rel_bias", (32, 16), None),
]
SMALL_W = 1024
SMALL_FULL_ROWS = 32
SMALL_LOCAL_ROWS = 48

MAT_GROUPS = {
    "f00_up": ["w1t_00", "w3t_00"], "f00_down": ["w2_00"], "f01": ["w1t_01", "w3t_01", "w2_01"],
    "f10": ["w1t_10", "w3t_10", "w2_10"], "f11": ["w1t_11", "w3t_11", "w2_11"],
    "ssm": ["w_int", "w_out"], "att": ["w_q", "w_o", "w_kv"],
    "f00_early": ["w2_00", "w1t_00"], "f00_late": ["w3t_00"],
}
FIRST_GATHER = "f00_up"
GATHER_PLAN = {"f00_upgate": ["f00_down", "ssm"], "ssm_in": ["f01"], "ssm_ssd": ["att", "f10"],
               "f01_upgate": ["f11"]}
SCATTER_A_PLAN = {"att_dwo": "f11", "kv_dknorm": "f10", "kv_du": "att", "f01_du": "f01", "ssm_du": "ssm",
                  "f00_dw3": "f00_early", "f00_du": "f00_late"}
SCATTER_B_PLAN = {"att_dcore": "f11", "f01_dw2": "att", "f01_dgate": "f10", "ssm_dssd": "f01", "f00_dgate": "ssm",
                  "f00_du": "f00_early"}
LAST_SCATTER = "f00_late"
SLOT_MAJOR = ("w_int",)


def _shard_shape(s, a):
    return s[:a] + (s[a] // N_DEV,) + s[a + 1:]


def _unshard_view(stack, shard_shape, axis):
    moved = jnp.moveaxis(stack, 0, axis)
    return moved.reshape(shard_shape[:axis] + (N_DEV * shard_shape[axis],) + shard_shape[axis + 1:])


def _small_local(arrs):
    flat = jnp.concatenate([arrs[n].reshape(-1) for n, _, _ in SMALL])
    return jnp.pad(flat, (0, SMALL_LOCAL_ROWS * LANES - flat.shape[0])).reshape(SMALL_LOCAL_ROWS, LANES)


def chip_partial(g4, ra, name):
    _, _, n, width = g4.shape

    def body(core_ref, g_ref, r_ref, o_ref):
        o_ref[0] = (g_ref[0, 0].astype(F32) + r_ref[0, 0].astype(F32)).astype(o_ref.dtype)

    grid_spec = pltpu.PrefetchScalarGridSpec(
        num_scalar_prefetch=1, grid=(N_CHIPS,),
        in_specs=[pl.BlockSpec((1, 1, n, width), lambda q, core: (q, core[0], 0, 0)),
                  pl.BlockSpec((1, 1, n, width), lambda q, core: (q, 0, 0, 0))],
        out_specs=pl.BlockSpec((1, n, width), lambda q, core: (q, 0, 0)))
    core = jnp.reshape(lax.axis_index("c"), (1,)).astype(jnp.int32)
    return pl.pallas_call(
        body, name=name, grid_spec=grid_spec, out_shape=jax.ShapeDtypeStruct((N_CHIPS, n, width), g4.dtype),
        compiler_params=pltpu.CompilerParams(dimension_semantics=("arbitrary",), vmem_limit_bytes=VMEM_LIMIT_BYTES),
    )(core, g4, ra)


class StepIO:
    def __init__(self, pieces):
        self.pieces = pieces
        self.full = {}
        self.grad = {}
        self.from_sibling = {}
        self.recv = {}

    def w(self, name):
        return self.full[name]

    def put(self, name, g):
        self.grad[name] = g

    def _by_chip_core(self, name):
        g = self.grad[name]
        return g.reshape((N_CHIPS, 2, g.shape[0] // N_DEV) + g.shape[1:])

    def gather_items(self, groups):
        names = [n for grp in groups for n in MAT_GROUPS[grp]]
        items = [("g2", self.pieces[n], None if n in SLOT_MAJOR else 0) for n in names]

        def sink(outs):
            for n, o in zip(names, outs):
                self.full[n] = o.reshape((-1,) + o.shape[2:]) if n in SLOT_MAJOR else o

        return items, sink

    def scatter_a_items(self, group):
        names = MAT_GROUPS[group]
        items = [("sa", self._by_chip_core(n), None) for n in names]

        def sink(outs):
            for n, o in zip(names, outs):
                self.from_sibling[n] = o

        return items, sink

    def scatter_b_items(self, group):
        names = MAT_GROUPS[group]
        items = [("sb", chip_partial(self._by_chip_core(n), self.from_sibling[n], "partial_" + n), None)
                 for n in names]

        def sink(outs):
            for n, o in zip(names, outs):
                self.recv[n] = o

        return items, sink

    def hook(self, site):
        parts = []
        if site in GATHER_PLAN:
            parts.append(self.gather_items(GATHER_PLAN[site]))
        if site in SCATTER_A_PLAN:
            parts.append(self.scatter_a_items(SCATTER_A_PLAN[site]))
        if site in SCATTER_B_PLAN:
            parts.append(self.scatter_b_items(SCATTER_B_PLAN[site]))
        if not parts:
            return None
        return combine_hooks(parts)


def combine_hooks(parts):
    items = [it for its, _ in parts for it in its]

    def sink(outs):
        p = 0
        for its, snk in parts:
            snk(outs[p:p + len(its)])
            p += len(its)

    return Comm(items), sink


def step(x, target, wts, ms, vs):
    me = _my_index()

    pieces = {}
    for li in range(2):
        for hi in range(2):
            tag = "%d%d" % (li, hi)
            pieces["w1t_" + tag] = wts["ffn_w1"][li, hi].T.astype(BF16)
            pieces["w3t_" + tag] = wts["ffn_w3"][li, hi].T.astype(BF16)
            pieces["w2_" + tag] = wts["ffn_w2"][li, hi].astype(BF16)
    pieces["w_int"] = wts["ssm_w_in"][0].T.astype(BF16)
    pieces["w_out"] = wts["ssm_w_out"][0].astype(BF16)
    pieces["w_kv"] = wts["w_kv"].astype(BF16)
    pieces["w_q"] = wts["w_q"][0].astype(BF16)
    pieces["w_o"] = wts["w_o"][0].astype(BF16)
    io = StepIO(pieces)

    small_sharded = [(n, s, a) for n, s, a in SMALL if a is not None]
    loc = jnp.concatenate([wts[n].reshape(-1) for n, _, _ in small_sharded])
    loc_rows = -(-loc.shape[0] // (8 * LANES)) * 8
    loc = jnp.pad(loc, (0, loc_rows * LANES - loc.shape[0])).reshape(loc_rows, LANES)
    got_small = []
    comm, sink = combine_hooks([io.gather_items([FIRST_GATHER]), ([("g", loc, None)], got_small.extend)])
    sink(comm_only(comm, "gather_first"))
    gath_small = got_small[0].reshape(N_DEV, -1)
    small = {}
    off = 0
    for n, s, a in small_sharded:
        shard = _shard_shape(s, a)
        cnt = int(np.prod(shard))
        small[n] = _unshard_view(gath_small[:, off:off + cnt].reshape((N_DEV,) + shard), shard, a)
        off += cnt
    for n, s, a in SMALL:
        if a is None:
            small[n] = wts[n]

    loss_part, grad_x, g_small_local = local_step(x[0], target[0], small, io)
    loss = lax.psum(loss_part, ("x", "y", "c"))

    small_flat = jnp.concatenate([g_small_local[n].reshape(-1) for n, _, _ in SMALL])
    small_buf = jnp.pad(small_flat, (0, SMALL_FULL_ROWS * SMALL_W - small_flat.shape[0]))
    small_buf = small_buf.reshape(SMALL_FULL_ROWS, SMALL_W)
    got_small = []
    comm, sink = combine_hooks([io.scatter_b_items(LAST_SCATTER), ([("g", small_buf, None)], got_small.extend)])
    sink(comm_only(comm, "exchange_last"))
    small_all = got_small[0]

    def sum_body(r_ref, o_ref):
        o_ref[...] = _slot_sum(r_ref)

    vmem = pl.BlockSpec(memory_space=pltpu.VMEM)
    small_sum, = pcall(sum_body, name="sum_small", grid=(), in_specs=[vmem], out_specs=[vmem],
                       out_shape=[jax.ShapeDtypeStruct((SMALL_FULL_ROWS, SMALL_W), F32)], args=[small_all])
    small_sum = small_sum.reshape(-1)
    g_small = {}
    off = 0
    for n, s, a in SMALL:
        cnt = int(np.prod(s))
        gfull = small_sum[off:off + cnt].reshape(s)
        off += cnt
        if a is None:
            g_small[n] = gfull
        else:
            width = s[a] // N_DEV
            g_small[n] = lax.dynamic_slice_in_dim(gfull, me * width, width, axis=a)

    out = {}

    def emit(name, res, shape):
        for kind, arr in zip(("grad", "delta", "new_m", "new_v"), res):
            out[kind + "_" + name] = arr.reshape(shape)

    for name, key in (("ffn_w1", "w1t_"), ("ffn_w3", "w3t_")):
        shp = wts[name].shape
        view = lambda t: t.reshape((4,) + shp[2:])
        res = adamw_cols([io.recv[key + tag] for tag in FFN_TAGS], view(wts[name]), view(ms[name]), view(vs[name]),
                         "adamw_" + name)
        emit(name, res, shp)
    shp = wts["ffn_w2"].shape
    view = lambda t: t.reshape((4,) + shp[2:])
    res = adamw_rows([io.recv["w2_" + tag] for tag in FFN_TAGS], view(wts["ffn_w2"]), view(ms["ffn_w2"]),
                     view(vs["ffn_w2"]), "adamw_ffn_w2")
    emit("ffn_w2", res, shp)
    res = adamw_cols([io.recv["w_int"]], wts["ssm_w_in"], ms["ssm_w_in"], vs["ssm_w_in"], "adamw_ssm_w_in")
    emit("ssm_w_in", res, wts["ssm_w_in"].shape)
    for name, key in (("ssm_w_out", "w_out"), ("w_kv", "w_kv"), ("w_q", "w_q"), ("w_o", "w_o")):
        shp = wts[name].shape
        view = lambda t: t.reshape((1,) + shp[-2:])
        res = adamw_rows([io.recv[key]], view(wts[name]), view(ms[name]), view(vs[name]), "adamw_" + name)
        emit(name, res, shp)

    res_s = rowmap(lambda gg, ww, mm_, vv: _adamw(gg, ww, mm_, vv),
                   [_small_local(g_small), _small_local(wts), _small_local(ms), _small_local(vs)], [],
                   [(LANES, F32)] * 3, tm=SMALL_LOCAL_ROWS, name="adamw_small")
    flat_s = [r.reshape(-1) for r in res_s]
    off = 0
    for n, s, a in SMALL:
        shard = s if a is None else _shard_shape(s, a)
        cnt = int(np.prod(shard))
        out["grad_" + n] = g_small[n]
        for kind, arr in zip(("delta", "new_m", "new_v"), flat_s):
            out[kind + "_" + n] = arr[off:off + cnt].reshape(shard)
        off += cnt
    out["loss"] = loss
    out["grad_x"] = grad_x[None]
    return out


def kernel(x, ffn_norm, ffn_w1, ffn_w3, ffn_w2, ssm_norm, ssm_w_in, ssm_conv_w, ssm_conv_b, ssm_dt_bias, ssm_a_log, ssm_d, ssm_gate_norm, ssm_w_out, kv_norm, w_kv, k_norm, attn_norm, w_q, q_norm, sinks, w_o, rel_bias, loss_target, m_ffn_norm, m_ffn_w1, m_ffn_w3, m_ffn_w2, m_ssm_norm, m_ssm_w_in, m_ssm_conv_w, m_ssm_conv_b, m_ssm_dt_bias, m_ssm_a_log, m_ssm_d, m_ssm_gate_norm, m_ssm_w_out, m_kv_norm, m_w_kv, m_k_norm, m_attn_norm, m_w_q, m_q_norm, m_sinks, m_w_o, m_rel_bias, v_ffn_norm, v_ffn_w1, v_ffn_w3, v_ffn_w2, v_ssm_norm, v_ssm_w_in, v_ssm_conv_w, v_ssm_conv_b, v_ssm_dt_bias, v_ssm_a_log, v_ssm_d, v_ssm_gate_norm, v_ssm_w_out, v_kv_norm, v_w_kv, v_k_norm, v_attn_norm, v_w_q, v_q_norm, v_sinks, v_w_o, v_rel_bias):
    args = locals()
    wts = {n: args[n] for n in WEIGHT_NAMES}
    ms = {n: args["m_" + n] for n in WEIGHT_NAMES}
    vs = {n: args["v_" + n] for n in WEIGHT_NAMES}
    out = step(x, loss_target, wts, ms, vs)
    result = [out["loss"], out["grad_x"]]
    for kind in ("grad", "delta", "new_m", "new_v"):
        result += [out[kind + "_" + n] for n in WEIGHT_NAMES]
    return tuple(result)
```

```python
import functools
import math
import operator

import numpy as np
import jax
import jax.numpy as jnp
from jax import lax
from jax.experimental import pallas as pl
from jax.experimental.pallas import tpu as pltpu

F32 = jnp.float32
BF16 = jnp.bfloat16

D_MODEL = 1024
D_FF = 2816
N_DEV = 8
SSM_D_INNER = 2048
SSM_HEAD_DIM = 64
SSM_HEADS = 32
SSM_GROUPS = 4
SSM_STATE = 128
SSM_CONV = 4
SSM_CHUNK = 256
SSM_CONV_DIM = SSM_D_INNER + 2 * SSM_GROUPS * SSM_STATE
SSM_IN_DIM = SSM_D_INNER + SSM_CONV_DIM + SSM_HEADS
ATT_HEAD_DIM = 64
ATT_HEADS = 16
ATT_KV_HEADS = 2
ATT_GROUP = 8
ATT_WINDOW = 128
REL_BUCKETS = 32
EPS = 1e-6
NEG = -1e30

ADAM_LR = 0.001
ADAM_B1 = 0.9
ADAM_B2 = 0.999
ADAM_EPS = 1e-08
ADAM_WD = 0.01
ADAM_STEP = 10

VMEM_LIMIT_BYTES = 52 * 1024 * 1024
LANES = 128
MESH_ID = pl.DeviceIdType.MESH
ANY_SPEC = pl.BlockSpec(memory_space=pl.ANY)

NT = (((1,), (1,)), ((), ()))
TN = (((0,), (0,)), ((), ()))
NN = (((1,), (0,)), ((), ()))


def _pick(dim, cands):
    for c in cands:
        if dim % c == 0:
            return c
    return dim


def _my_index():
    return 4 * lax.axis_index("x") + 2 * lax.axis_index("y") + lax.axis_index("c")


def _peer(k):
    x, y, c = lax.axis_index("x"), lax.axis_index("y"), lax.axis_index("c")
    px = 1 - x if (k >> 2) & 1 else x
    py = 1 - y if (k >> 1) & 1 else y
    pc = 1 - c if k & 1 else c
    return (px, py, pc), 4 * px + 2 * py + pc


def _piece(ref, axis, d, n):
    if axis is None:
        return ref.at[d]
    return ref.at[(slice(None),) * axis + (pl.ds(pl.multiple_of(d * n, 8), n),)]


SIBLING = 1
CHIP_PEERS = (4, 2, 6)
N_CHIPS = 4
SEMS_PER_ITEM = N_DEV - 1


def _my_chip():
    return 2 * lax.axis_index("x") + lax.axis_index("y")


class Comm:
    def __init__(self, items):
        self.items = list(items)

    def dst_shapes(self):
        out = []
        for kind, src, axis in self.items:
            s = tuple(src.shape)
            if kind == "g":
                shp = (N_DEV,) + s
            elif kind == "g2":
                shp = (N_DEV,) + s if axis is None else s[:axis] + (N_DEV * s[axis],) + s[axis + 1:]
            elif kind == "sa":
                shp = (s[0], 1) + s[2:]
            else:
                shp = s
            out.append(jax.ShapeDtypeStruct(shp, src.dtype))
        return out

    def scratch(self):
        n = len(self.items)
        return [pltpu.SemaphoreType.DMA((n * SEMS_PER_ITEM,)), pltpu.SemaphoreType.DMA((n * SEMS_PER_ITEM,)),
                pltpu.SemaphoreType.DMA((n,))]

    def _run(self, srcs, dsts, sems, starting):
        send_sems, recv_sems, local_sems = sems
        me = _my_index()
        core = lax.axis_index("c")
        chip = _my_chip()
        for i, (kind, src, axis) in enumerate(self.items):
            s_ref, d_ref = srcs[i], dsts[i]
            base = i * SEMS_PER_ITEM

            def rdma(src_ref, dst_ref, j, peer):
                return pltpu.make_async_remote_copy(
                    src_ref=src_ref, dst_ref=dst_ref, send_sem=send_sems.at[base + j], recv_sem=recv_sems.at[base + j],
                    device_id=peer, device_id_type=MESH_ID)

            if kind == "g":
                local = pltpu.make_async_copy(s_ref, d_ref.at[me], local_sems.at[i])
                outs = [rdma(s_ref, d_ref.at[me], k - 1, _peer(k)[0]) for k in range(1, N_DEV)]
                if starting:
                    local.start()
                    for cp in outs:
                        cp.start()
                else:
                    for k in range(1, N_DEV):
                        rdma(s_ref, d_ref.at[_peer(k)[1]], k - 1, _peer(k)[0]).wait_recv()
                    for cp in outs:
                        cp.wait_send()
                    local.wait()
            elif kind == "g2":
                n = None if axis is None else src.shape[axis]
                mine = _piece(d_ref, axis, me, n)
                sib = _peer(SIBLING)[0]
                local = pltpu.make_async_copy(s_ref, mine, local_sems.at[i])
                outs = [rdma(s_ref, mine, 0, sib)] + [rdma(s_ref, mine, 1 + j, _peer(k)[0])
                                                      for j, k in enumerate(CHIP_PEERS)]
                if starting:
                    local.start()
                    for cp in outs:
                        cp.start()
                else:
                    passed = []
                    for j, k in enumerate(CHIP_PEERS):
                        theirs = _piece(d_ref, axis, _peer(k)[1], n)
                        rdma(s_ref, theirs, 1 + j, _peer(k)[0]).wait_recv()
                        fwd = rdma(theirs, theirs, 4 + j, sib)
                        fwd.start()
                        passed.append(fwd)
                    rdma(s_ref, _piece(d_ref, axis, _peer(SIBLING)[1], n), 0, sib).wait_recv()
                    for j, k in enumerate(CHIP_PEERS):
                        rdma(s_ref, _piece(d_ref, axis, _peer(k ^ SIBLING)[1], n), 4 + j, sib).wait_recv()
                    for cp in outs + passed:
                        cp.wait_send()
                    local.wait()
            elif kind == "sa":
                cp = rdma(s_ref.at[(slice(None), pl.ds(1 - core, 1))], d_ref, 0, _peer(SIBLING)[0])
                if starting:
                    cp.start()
                else:
                    cp.wait_recv()
                    cp.wait_send()
            else:
                local = pltpu.make_async_copy(s_ref.at[chip], d_ref.at[chip], local_sems.at[i])
                outs = [rdma(s_ref.at[_peer(k)[1] >> 1], d_ref.at[chip], 1 + j, _peer(k)[0])
                        for j, k in enumerate(CHIP_PEERS)]
                if starting:
                    local.start()
                    for cp in outs:
                        cp.start()
                else:
                    for j, k in enumerate(CHIP_PEERS):
                        rdma(s_ref.at[chip], d_ref.at[_peer(k)[1] >> 1], 1 + j, _peer(k)[0]).wait_recv()
                    for cp in outs:
                        cp.wait_send()
                    local.wait()

    def start(self, srcs, dsts, sems):
        self._run(srcs, dsts, sems, True)

    def wait(self, srcs, dsts, sems):
        self._run(srcs, dsts, sems, False)


def pcall(body, *, name, grid, in_specs, out_specs, out_shape, args, scratch=(), hook=None):
    cparams = pltpu.CompilerParams(dimension_semantics=("arbitrary",) * len(grid), vmem_limit_bytes=VMEM_LIMIT_BYTES)
    if hook is None:
        outs = pl.pallas_call(body, name=name, grid=grid, in_specs=list(in_specs), out_specs=list(out_specs),
                              out_shape=list(out_shape), scratch_shapes=list(scratch), compiler_params=cparams)(*args)
        return list(outs)
    comm, sink = hook
    n_in, n_out, n_scr, n_it = len(args), len(out_shape), len(scratch), len(comm.items)
    dims = tuple(grid)

    def wrapped(*refs):
        p = 0
        ins = refs[p:p + n_in]
        p += n_in
        csrc = refs[p:p + n_it]
        p += n_it
        outs = refs[p:p + n_out]
        p += n_out
        cdst = refs[p:p + n_it]
        p += n_it
        scr = refs[p:p + n_scr]
        p += n_scr
        sems = refs[p:p + 3]
        if dims:
            ids = [pl.program_id(a) for a in range(len(dims))]
            first = functools.reduce(operator.and_, [i == 0 for i in ids])
            last = functools.reduce(operator.and_, [i == d - 1 for i, d in zip(ids, dims)])

            @pl.when(first)
            def _():
                comm.start(csrc, cdst, sems)

            body(*ins, *outs, *scr)

            @pl.when(last)
            def _():
                comm.wait(csrc, cdst, sems)
        else:
            comm.start(csrc, cdst, sems)
            body(*ins, *outs, *scr)
            comm.wait(csrc, cdst, sems)

    res = pl.pallas_call(
        wrapped, name=name, grid=grid,
        in_specs=list(in_specs) + [ANY_SPEC] * n_it, out_specs=list(out_specs) + [ANY_SPEC] * n_it,
        out_shape=list(out_shape) + comm.dst_shapes(), scratch_shapes=list(scratch) + comm.scratch(),
        compiler_params=cparams,
    )(*args, *[src for _, src, _ in comm.items])
    res = list(res)
    sink(res[n_out:])
    return res[:n_out]


def comm_only(comm, name):
    got = []
    pcall(lambda *refs: None, name=name, grid=(), in_specs=[], out_specs=[], out_shape=[], args=[],
          hook=(comm, got.extend))
    return got


def mm(a, b, *, ta=False, tb=False, out_dtype=F32, res=None, alpha=1.0, name, hook=None):
    if ta:
        k_dim, m_dim = a.shape
    else:
        m_dim, k_dim = a.shape
    if tb:
        n_dim, k2 = b.shape
    else:
        k2, n_dim = b.shape
    assert k_dim == k2, (a.shape, b.shape, ta, tb)
    tn = _pick(n_dim, (1024, 1408, 512, 256, 128))
    tm = _pick(m_dim, (1024, 1408, 512, 256, 128)) if tn <= 1024 else _pick(m_dim, (512, 256, 128))
    tk = _pick(k_dim, (1024, 512, 256, 128)) if ta else _pick(k_dim, (512, 1408, 256, 128))
    nk = k_dim // tk
    has_res = res is not None
    dn = (((0 if ta else 1,), (1 if tb else 0,)), ((), ()))

    def body(*refs):
        if has_res:
            a_ref, b_ref, r_ref, o_ref, acc_ref = refs
        else:
            a_ref, b_ref, o_ref, acc_ref = refs
        k = pl.program_id(2)

        @pl.when(k == 0)
        def _():
            acc_ref[...] = jnp.zeros_like(acc_ref)

        acc_ref[...] += lax.dot_general(a_ref[...].astype(BF16), b_ref[...].astype(BF16), dn,
                                        preferred_element_type=F32)

        @pl.when(k == nk - 1)
        def _():
            r = acc_ref[...]
            if alpha != 1.0:
                r = r * alpha
            if has_res:
                r = r_ref[...] + r
            o_ref[...] = r.astype(o_ref.dtype)

    a_spec = pl.BlockSpec((tk, tm), lambda i, j, k: (k, i)) if ta else pl.BlockSpec((tm, tk), lambda i, j, k: (i, k))
    b_spec = pl.BlockSpec((tn, tk), lambda i, j, k: (j, k)) if tb else pl.BlockSpec((tk, tn), lambda i, j, k: (k, j))
    o_spec = pl.BlockSpec((tm, tn), lambda i, j, k: (i, j))
    in_specs = [a_spec, b_spec] + ([o_spec] if has_res else [])
    args = [a, b] + ([res] if has_res else [])
    out, = pcall(body, name=name, grid=(m_dim // tm, n_dim // tn, nk), in_specs=in_specs, out_specs=[o_spec],
                 out_shape=[jax.ShapeDtypeStruct((m_dim, n_dim), out_dtype)], args=args,
                 scratch=[pltpu.VMEM((tm, tn), F32)], hook=hook)
    return out


def rowmap(fn, rows, consts=(), out_rows=(), out_accs=(), *, tm, name, hook=None):
    first = rows[0][0] if isinstance(rows[0], tuple) else rows[0]
    t_dim = first.shape[0]
    assert t_dim % tm == 0, (t_dim, tm)
    n_r, n_c, n_o = len(rows), len(consts), len(out_rows)

    def body(*refs):
        ins = [r[...] for r in refs[:n_r + n_c]]
        o_refs = refs[n_r + n_c:]
        outs = tuple(fn(*ins))
        for o_ref, val in zip(o_refs[:n_o], outs[:n_o]):
            o_ref[...] = val.astype(o_ref.dtype)
        if out_accs:
            @pl.when(pl.program_id(0) == 0)
            def _():
                for o_ref in o_refs[n_o:]:
                    o_ref[...] = jnp.zeros_like(o_ref)

            for o_ref, val in zip(o_refs[n_o:], outs[n_o:]):
                o_ref[...] += val

    in_specs, args = [], []
    for r in rows:
        if isinstance(r, tuple):
            args.append(r[0])
            in_specs.append(r[1])
        else:
            args.append(r)
            in_specs.append(pl.BlockSpec((tm, r.shape[1]), lambda i: (i, 0)))
    for c in consts:
        args.append(c)
        in_specs.append(pl.BlockSpec(c.shape, lambda i, nd=c.ndim: (0,) * nd))
    out_specs = [pl.BlockSpec((tm, w), lambda i: (i, 0)) for (w, _) in out_rows]
    out_specs += [pl.BlockSpec(s, lambda i, nd=len(s): (0,) * nd) for s in out_accs]
    out_shape = [jax.ShapeDtypeStruct((t_dim, w), dt) for (w, dt) in out_rows]
    out_shape += [jax.ShapeDtypeStruct(s, F32) for s in out_accs]
    return pcall(body, name=name, grid=(t_dim // tm,), in_specs=in_specs, out_specs=out_specs, out_shape=out_shape,
                 args=args, hook=hook)


def _rms_fwd(x, g):
    r = lax.rsqrt(jnp.mean(x * x, axis=-1, keepdims=True) + EPS)
    return x * r * g


def _rms_bwd(x, g, dy):
    r = lax.rsqrt(jnp.mean(x * x, axis=-1, keepdims=True) + EPS)
    xh = x * r
    dg = jnp.sum(dy * xh, axis=0, keepdims=True)
    dxh = dy * g
    dx = r * (dxh - xh * jnp.mean(dxh * xh, axis=-1, keepdims=True))
    return dx, dg


def _sigmoid(x):
    return 1.0 / (1.0 + jnp.exp(-x))


def _silu(x):
    return x * _sigmoid(x)


def _silu_grad(x):
    s = _sigmoid(x)
    return s * (1.0 + x * (1.0 - s))


def _split3(x):
    hi = x.astype(BF16)
    r1 = x - hi.astype(F32)
    mid = r1.astype(BF16)
    lo = (r1 - mid.astype(F32)).astype(BF16)
    return hi, mid, lo


def _dot(a, b, dn=NN):
    return lax.dot_general(a.astype(BF16), b.astype(BF16), dn, preferred_element_type=F32)


FFN_TN = 1408
RESIDENT_TM = 512


def ffn_upgate(h, g, w1t, w3t, nm, hook=None):
    t_dim = h.shape[0]
    tm = _pick(t_dim, (512, 256, 128))
    tn = FFN_TN

    n_j = D_FF // tn
    u_w = D_MODEL // n_j

    def body(h_ref, g_ref, w1_ref, w3_ref, u_ref, a_ref, b_ref, hm_ref):
        uu = _rms_fwd(h_ref[...], g_ref[...]).astype(BF16)
        for j in range(n_j):
            @pl.when(pl.program_id(0) == j)
            def _(j=j):
                u_ref[...] = uu[:, j * u_w:(j + 1) * u_w]

        a = lax.dot_general(uu, w1_ref[...], NT, preferred_element_type=F32)
        b = lax.dot_general(uu, w3_ref[...], NT, preferred_element_type=F32)
        a_ref[...] = a.astype(a_ref.dtype)
        b_ref[...] = b.astype(b_ref.dtype)
        hm_ref[...] = (_silu(a) * b).astype(hm_ref.dtype)

    row_spec = pl.BlockSpec((tm, D_MODEL), lambda j, i: (i, 0))
    w_spec = pl.BlockSpec((tn, D_MODEL), lambda j, i: (j, 0))
    o_spec = pl.BlockSpec((tm, tn), lambda j, i: (i, j))
    o_shape = jax.ShapeDtypeStruct((t_dim, D_FF), BF16)
    return pcall(body, name=nm, grid=(D_FF // tn, t_dim // tm),
                 in_specs=[row_spec, pl.BlockSpec((1, D_MODEL), lambda j, i: (0, 0)), w_spec, w_spec],
                 out_specs=[pl.BlockSpec((tm, u_w), lambda j, i: (i, j))] + [o_spec] * 3,
                 out_shape=[jax.ShapeDtypeStruct((t_dim, D_MODEL), BF16)] + [o_shape] * 3,
                 args=[h, g, w1t, w3t], hook=hook)


def ffn_dgate(dout_bf, w2, a, b, nm, hook=None):
    t_dim = dout_bf.shape[0]
    tm = _pick(t_dim, (512, 256, 128))
    tn = FFN_TN

    def body(d_ref, w2_ref, a_ref, b_ref, da_ref, db_ref):
        dhm = 0.5 * lax.dot_general(d_ref[...], w2_ref[...], NT, preferred_element_type=F32)
        av = a_ref[...].astype(F32)
        bv = b_ref[...].astype(F32)
        sg = _sigmoid(av)
        da_ref[...] = (dhm * bv * (sg * (1.0 + av * (1.0 - sg)))).astype(da_ref.dtype)
        db_ref[...] = (dhm * (av * sg)).astype(db_ref.dtype)

    t_spec = pl.BlockSpec((tm, tn), lambda j, i: (i, j))
    o_shape = jax.ShapeDtypeStruct((t_dim, D_FF), BF16)
    return pcall(body, name=nm, grid=(D_FF // tn, t_dim // tm),
                 in_specs=[pl.BlockSpec((tm, D_MODEL), lambda j, i: (i, 0)),
                           pl.BlockSpec((tn, D_MODEL), lambda j, i: (j, 0)), t_spec, t_spec],
                 out_specs=[t_spec] * 2, out_shape=[o_shape] * 2, args=[dout_bf, w2, a, b], hook=hook)


def ffn_fwd(h, g, tag, io, target=None):
    nm = "f" + tag
    u, a, b, hm = ffn_upgate(h, g, io.w("w1t_" + tag), io.w("w3t_" + tag), nm + "_upgate",
                             hook=io.hook(nm + "_upgate"))
    if target is None:
        return mm(hm, io.w("w2_" + tag), res=h, alpha=0.5, name=nm + "_down"), (u, a, b, hm)

    def down_loss(hmv, hv, t, w2):
        e = hv + 0.5 * _dot(hmv, w2) - t
        d = e * (1.0 / D_MODEL)
        return d, d, jnp.sum(e * e, axis=0, keepdims=True)

    res = rowmap(down_loss, [hm, h, target], [io.w("w2_" + tag)], [(D_MODEL, F32), (D_MODEL, BF16)],
                 [(1, D_MODEL)], tm=RESIDENT_TM, name=nm + "_down_loss")
    return res, (u, a, b, hm)


def du_norm_bwd(pairs, h, g, dout, nm, hook=None):
    t_dim = h.shape[0]
    tm = RESIDENT_TM
    n_p = len(pairs)

    def body(*refs):
        h_ref, d_ref, g_ref = refs[2 * n_p:2 * n_p + 3]
        dh_ref, dhb_ref, dg_ref = refs[2 * n_p + 3:]
        du = None
        for p, (_, _, tb) in enumerate(pairs):
            t = lax.dot_general(refs[2 * p][...].astype(BF16), refs[2 * p + 1][...].astype(BF16), NT if tb else NN,
                                preferred_element_type=F32)
            du = t if du is None else du + t
        dx, dg = _rms_bwd(h_ref[...], g_ref[...], du)
        dh = d_ref[...] + dx
        dh_ref[...] = dh
        dhb_ref[...] = dh.astype(dhb_ref.dtype)

        @pl.when(pl.program_id(0) == 0)
        def _():
            dg_ref[...] = jnp.zeros_like(dg_ref)

        dg_ref[...] += dg

    in_specs, args = [], []
    for a, b, _ in pairs:
        in_specs += [pl.BlockSpec((tm, a.shape[1]), lambda i: (i, 0)), pl.BlockSpec(b.shape, lambda i: (0, 0))]
        args += [a, b]
    row_spec = pl.BlockSpec((tm, D_MODEL), lambda i: (i, 0))
    vec_spec = pl.BlockSpec((1, D_MODEL), lambda i: (0, 0))
    return pcall(body, name=nm, grid=(t_dim // tm,), in_specs=in_specs + [row_spec, row_spec, vec_spec],
                 out_specs=[row_spec, row_spec, vec_spec],
                 out_shape=[jax.ShapeDtypeStruct((t_dim, D_MODEL), F32), jax.ShapeDtypeStruct((t_dim, D_MODEL), BF16),
                            jax.ShapeDtypeStruct((1, D_MODEL), F32)],
                 args=args + [h, dout, g], hook=hook)


def ffn_bwd(h, g, tag, saved, dout, dout_bf, io):
    nm = "f" + tag
    w1t, w3t, w2 = io.w("w1t_" + tag), io.w("w3t_" + tag), io.w("w2_" + tag)
    u, a, b, hm = saved
    io.put("w2_" + tag, mm(hm, dout_bf, ta=True, alpha=0.5, out_dtype=BF16, name=nm + "_dw2",
                           hook=io.hook(nm + "_dw2")))
    da, db = ffn_dgate(dout_bf, w2, a, b, nm + "_dgate", hook=io.hook(nm + "_dgate"))
    io.put("w1t_" + tag, mm(da, u, ta=True, out_dtype=BF16, name=nm + "_dw1"))
    io.put("w3t_" + tag, mm(db, u, ta=True, out_dtype=BF16, name=nm + "_dw3", hook=io.hook(nm + "_dw3")))
    return du_norm_bwd([(da, w1t, False), (db, w3t, False)], h, g, dout, nm + "_du", hook=io.hook(nm + "_du"))


def conv_input_grad(d_parts, w, nm):
    tm = 256
    t_dim = d_parts[0].shape[0]
    n_tiles = t_dim // tm

    def fn(d1, n1, d2, n2, d3, n3, ww):
        d = jnp.concatenate([d1, d2, d3], axis=1)
        nxt = jnp.concatenate([n1, n2, n3], axis=1)
        nxt = jnp.where(pl.program_id(0) < n_tiles - 1, nxt, 0.0)
        dd = jnp.concatenate([d, nxt], axis=0)
        out = dd[3:3 + tm] * ww[0:1]
        for k in range(1, SSM_CONV):
            out = out + dd[3 - k:3 - k + tm] * ww[k:k + 1]
        return (out,)

    rows = []
    for d in d_parts:
        below = pl.BlockSpec((8, d.shape[1]), lambda i: (jnp.minimum((i + 1) * (tm // 8), t_dim // 8 - 1), 0))
        rows += [d, (d, below)]
    dx, = rowmap(fn, rows, [w], [(SSM_CONV_DIM, BF16)], tm=tm, name=nm)
    return dx


GRP_W = SSM_D_INNER // SSM_GROUPS
HPG = SSM_HEADS // SSM_GROUPS
HEAD_SHIFT = 6


def _split2(x):
    hi = x.astype(BF16)
    return hi, (x - hi.astype(F32)).astype(BF16)


def _expand_mats():
    e = ((lax.broadcasted_iota(jnp.int32, (HPG, GRP_W), 1) >> HEAD_SHIFT)
         == lax.broadcasted_iota(jnp.int32, (HPG, GRP_W), 0)).astype(BF16)
    et = ((lax.broadcasted_iota(jnp.int32, (GRP_W, HPG), 0) >> HEAD_SHIFT)
          == lax.broadcasted_iota(jnp.int32, (GRP_W, HPG), 1)).astype(BF16)
    return e, et


def _expand(v, e_m):
    hi, lo = _split2(v)
    return jnp.dot(hi, e_m, preferred_element_type=F32) + jnp.dot(lo, e_m, preferred_element_type=F32)


def _reduce8(v, et_m):
    hi, lo = _split2(v)
    return jnp.dot(hi, et_m, preferred_element_type=F32) + jnp.dot(lo, et_m, preferred_element_type=F32)


def _ssd_group_terms(dt_ref, dtT_ref, arow_ref, acol_ref):
    L = SSM_CHUNK
    r = lax.broadcasted_iota(jnp.int32, (L, L), 0)
    c = lax.broadcasted_iota(jnp.int32, (L, L), 1)
    tril = (r >= c).astype(BF16)
    triu = (r <= c).astype(BF16)
    dtg = dt_ref[0]
    acol = None
    for p in _split3(dtg * arow_ref[0]):
        t = jnp.dot(tril, p, preferred_element_type=F32)
        acol = t if acol is None else acol + t
    arowT = None
    for p in _split3(dtT_ref[0] * acol_ref[0]):
        t = jnp.dot(p, triu, preferred_element_type=F32)
        arowT = t if arowT is None else arowT + t
    return dtg, acol, arowT, r >= c


def _state_decay(a_last_col, et_m):
    hi, lo = _split2(jnp.broadcast_to(jnp.exp(a_last_col), (HPG, SSM_STATE)))
    return jnp.dot(et_m, hi, preferred_element_type=F32) + jnp.dot(et_m, lo, preferred_element_type=F32)


def _conv_block(x_ref, halo_ref, w_ref, b_ref, first):
    L = SSM_CHUNK
    xx = jnp.concatenate([jnp.where(first, 0.0, halo_ref[...]), x_ref[...]], axis=0)
    w = w_ref[...]
    shifted = [pltpu.roll(xx, SSM_CONV - 1 - k, 0)[8:8 + L] if k < SSM_CONV - 1 else xx[8:8 + L]
               for k in range(SSM_CONV)]
    acc = b_ref[...] + shifted[0] * w[0:1]
    for k in range(1, SSM_CONV):
        acc = acc + shifted[k] * w[k:k + 1]
    return acc, shifted


def _ssd_specs(nc, rev):
    L, N = SSM_CHUNK, SSM_STATE
    xcols = SSM_D_INNER // LANES
    ch = (lambda c: nc - 1 - c) if rev else (lambda c: c)
    above = lambda c: jnp.maximum(ch(c) * (L // 8) - 1, 0)
    specs = []
    for width, col in ((GRP_W, lambda g: g), (N, lambda g: xcols + g), (N, lambda g: xcols + SSM_GROUPS + g)):
        specs += [
            pl.BlockSpec((L, width), lambda c, g, col=col: (ch(c), col(g))),
            pl.BlockSpec((8, width), lambda c, g, col=col: (above(c), col(g))),
            pl.BlockSpec((SSM_CONV, width), lambda c, g, col=col: (0, col(g))),
            pl.BlockSpec((1, width), lambda c, g, col=col: (0, col(g))),
        ]
    return specs + [
        pl.BlockSpec((1, L, HPG), lambda c, g: (g, ch(c), 0)),
        pl.BlockSpec((1, HPG, L), lambda c, g: (g, 0, ch(c))),
        pl.BlockSpec((1, 1, HPG), lambda c, g: (g, 0, 0)),
        pl.BlockSpec((1, HPG, 1), lambda c, g: (g, 0, 0)),
        pl.BlockSpec((1, GRP_W), lambda c, g: (0, g)),
    ]


def ssd_fwd(xbc_raw, conv_w, conv_b, dt_g, dtT_g, a_row, a_col, dvec, nm, hook=None):
    t_dim = xbc_raw.shape[0]
    L, P, N = SSM_CHUNK, SSM_HEAD_DIM, SSM_STATE
    nc = t_dim // L

    def body(x_ref, xh_ref, xw_ref, xb_ref, b_ref, bh_ref, bw_ref, bb_ref, c_ref, ch_ref, cw_ref, cb_ref,
             dt_ref, dtT_ref, arow_ref, acol_ref, dvec_ref, y_ref, st_ref, s_s):
        ci = pl.program_id(0)
        g = pl.program_id(1)

        @pl.when((ci == 0) & (g == 0))
        def _():
            s_s[...] = jnp.zeros_like(s_s)

        e_m, et_m = _expand_mats()
        dtg, acol, arowT, causal = _ssd_group_terms(dt_ref, dtT_ref, arow_ref, acol_ref)
        a_last_row = acol[L - 1:L, :]
        x = _silu(_conv_block(x_ref, xh_ref, xw_ref, xb_ref, ci == 0)[0])
        bm = _silu(_conv_block(b_ref, bh_ref, bw_ref, bb_ref, ci == 0)[0])
        cm = _silu(_conv_block(c_ref, ch_ref, cw_ref, cb_ref, ci == 0)[0])
        cb = _dot(cm, bm, NT)
        s = s_s[g]
        st_ref[0, 0] = s
        ea_x = _expand(jnp.exp(acol), e_m)
        dt_x = _expand(dtg, e_m)
        w_x = _expand(jnp.exp(a_last_row - acol) * dtg, e_m)
        yb = ea_x * _dot(cm, s, NT) + dvec_ref[...] * x
        xd = (x * dt_x).astype(BF16)
        for e in range(HPG):
            sl = slice(e * P, (e + 1) * P)
            lm = jnp.exp(jnp.where(causal, acol[:, e:e + 1] - arowT[e:e + 1, :], NEG))
            m = (cb * lm).astype(BF16)
            y_ref[:, sl] = yb[:, sl] + jnp.dot(m, xd[:, sl], preferred_element_type=F32)
        s_s[g] = _state_decay(arowT[:, L - 1:L], et_m) * s + _dot(x * w_x, bm, TN)

    out_specs = [
        pl.BlockSpec((L, GRP_W), lambda c, g: (c, g)),
        pl.BlockSpec((1, 1, GRP_W, N), lambda c, g: (c, g, 0, 0)),
    ]
    return pcall(
        body, name=nm, grid=(nc, SSM_GROUPS), in_specs=_ssd_specs(nc, False), out_specs=out_specs,
        out_shape=[jax.ShapeDtypeStruct((t_dim, SSM_D_INNER), F32),
                   jax.ShapeDtypeStruct((nc, SSM_GROUPS, GRP_W, N), F32)],
        scratch=[pltpu.VMEM((SSM_GROUPS, GRP_W, N), F32)],
        args=[xbc_raw, xbc_raw, conv_w, conv_b] * 3 + [dt_g, dtT_g, a_row, a_col, dvec], hook=hook)


def ssd_bwd(dy, xbc_raw, conv_w, conv_b, dt_g, dtT_g, a_row, a_col, dvec, states, nm, hook=None):
    t_dim = xbc_raw.shape[0]
    L, P, N = SSM_CHUNK, SSM_HEAD_DIM, SSM_STATE
    nc = t_dim // L

    def body(dy_ref, x_ref, xh_ref, xw_ref, xb_ref, b_ref, bh_ref, bw_ref, bb_ref, c_ref, ch_ref, cw_ref, cb_ref,
             dt_ref, dtT_ref, arow_ref, acol_ref, dvec_ref, st_ref,
             dx_ref, db_ref, dc_ref, da_ref, ddt_ref, dd_ref, dwx_ref, dwb_ref, dwc_ref, dbx_ref, dbb_ref, dbc_ref,
             ds_s, yd_s, dxd_s):
        ci = pl.program_id(0)
        g = pl.program_id(1)

        @pl.when((ci == 0) & (g == 0))
        def _():
            ds_s[...] = jnp.zeros_like(ds_s)
            for r in (dd_ref, dwx_ref, dwb_ref, dwc_ref, dbx_ref, dbb_ref, dbc_ref):
                r[...] = jnp.zeros_like(r)

        e_m, et_m = _expand_mats()
        dtg, acol, arowT, causal = _ssd_group_terms(dt_ref, dtT_ref, arow_ref, acol_ref)
        a_last_row = acol[L - 1:L, :]
        first = ci == nc - 1
        pre_x, sh_x = _conv_block(x_ref, xh_ref, xw_ref, xb_ref, first)
        pre_b, sh_b = _conv_block(b_ref, bh_ref, bw_ref, bb_ref, first)
        pre_c, sh_c = _conv_block(c_ref, ch_ref, cw_ref, cb_ref, first)
        sg_x, sg_b, sg_c = _sigmoid(pre_x), _sigmoid(pre_b), _sigmoid(pre_c)
        x = pre_x * sg_x
        dy = dy_ref[...]
        bm = pre_b * sg_b
        cm = pre_c * sg_c
        cb = _dot(cm, bm, NT)
        s = st_ref[0, 0]
        dsp = ds_s[g]
        ew8 = jnp.exp(a_last_row - acol)
        ea_x = _expand(jnp.exp(acol), e_m)
        dt_x = _expand(dtg, e_m)
        ew_x = _expand(ew8, e_m)
        w_x = ew_x * dt_x
        z = _dot(cm, s, NT)
        dz = ea_x * dy
        dc = _dot(dz, s)
        ds_y = _dot(dz, cm, TN)
        du = _dot(bm, dsp, NT)
        u = x * w_x
        db = _dot(u, dsp)
        xd = (x * dt_x).astype(BF16)
        dyb = dy.astype(BF16)
        dcb = jnp.zeros((L, L), F32)
        for e in range(HPG):
            sl = slice(e * P, (e + 1) * P)
            lm = jnp.exp(jnp.where(causal, acol[:, e:e + 1] - arowT[e:e + 1, :], NEG))
            m = (cb * lm).astype(BF16)
            yd_s[:, sl] = jnp.dot(m, xd[:, sl], preferred_element_type=F32)
            dxd_s[:, sl] = lax.dot_general(m, dyb[:, sl], TN, preferred_element_type=F32)
            dcb = dcb + lax.dot_general(dyb[:, sl], xd[:, sl], NT, preferred_element_type=F32) * lm
        dxd = dxd_s[...]

        def through_conv(d_act, pre, sg, shifted, d_ref, dw_ref, dbias_ref):
            d_pre = d_act * (sg * (1.0 + pre * (1.0 - sg)))
            d_ref[...] = d_pre
            dw_ref[g] += jnp.concatenate([jnp.sum(d_pre * sh, axis=0, keepdims=True) for sh in shifted], axis=0)
            dbias_ref[g] += jnp.sum(d_pre, axis=0, keepdims=True)

        through_conv(dvec_ref[...] * dy + du * w_x + dt_x * dxd, pre_x, sg_x, sh_x, dx_ref, dwx_ref, dbx_ref)
        ddt = _reduce8(x * (ew_x * du + dxd), et_m)
        da = (_reduce8(dz * z + dyb.astype(F32) * yd_s[...], et_m)
              - _reduce8(xd.astype(F32) * dxd + du * u, et_m))
        dwa_row = _reduce8(jnp.broadcast_to(jnp.sum(du * u, axis=0, keepdims=True), (8, GRP_W)), et_m)[0:1]
        t_nh = None
        for p in _split3(dsp * s):
            t = lax.dot_general(p, et_m, TN, preferred_element_type=F32)
            t_nh = t if t_nh is None else t_nh + t
        d_last = dwa_row + jnp.exp(a_last_row) * jnp.sum(t_nh, axis=0, keepdims=True)
        row_l = lax.broadcasted_iota(jnp.int32, (L, 1), 0)
        da_ref[0] = da + jnp.where(row_l == L - 1, d_last, 0.0)
        ddt_ref[0] = ddt
        dd_ref[g] += jnp.sum(dy * x, axis=0, keepdims=True)
        through_conv(dc + _dot(dcb, bm), pre_c, sg_c, sh_c, dc_ref, dwc_ref, dbc_ref)
        through_conv(db + _dot(dcb, cm, TN), pre_b, sg_b, sh_b, db_ref, dwb_ref, dbb_ref)
        ds_s[g] = _state_decay(arowT[:, L - 1:L], et_m) * dsp + ds_y

    rc = lambda c: nc - 1 - c
    in_specs = ([pl.BlockSpec((L, GRP_W), lambda c, g: (rc(c), g))] + _ssd_specs(nc, True)
                + [pl.BlockSpec((1, 1, GRP_W, N), lambda c, g: (rc(c), g, 0, 0))])
    whole = lambda *shape: pl.BlockSpec(shape, lambda c, g: (0,) * len(shape))
    out_specs = [
        pl.BlockSpec((L, GRP_W), lambda c, g: (rc(c), g)),
        pl.BlockSpec((L, N), lambda c, g: (rc(c), g)),
        pl.BlockSpec((L, N), lambda c, g: (rc(c), g)),
        pl.BlockSpec((1, L, HPG), lambda c, g: (g, rc(c), 0)),
        pl.BlockSpec((1, L, HPG), lambda c, g: (g, rc(c), 0)),
        whole(SSM_GROUPS, 1, GRP_W),
        whole(SSM_GROUPS, SSM_CONV, GRP_W), whole(SSM_GROUPS, SSM_CONV, N), whole(SSM_GROUPS, SSM_CONV, N),
        whole(SSM_GROUPS, 1, GRP_W), whole(SSM_GROUPS, 1, N), whole(SSM_GROUPS, 1, N),
    ]
    gn = SSM_GROUPS * N
    acc = lambda *shape: jax.ShapeDtypeStruct(shape, F32)
    out_shape = [
        acc(t_dim, SSM_D_INNER), acc(t_dim, gn), acc(t_dim, gn), acc(SSM_GROUPS, t_dim, HPG),
        acc(SSM_GROUPS, t_dim, HPG), acc(SSM_GROUPS, 1, GRP_W),
        acc(SSM_GROUPS, SSM_CONV, GRP_W), acc(SSM_GROUPS, SSM_CONV, N), acc(SSM_GROUPS, SSM_CONV, N),
        acc(SSM_GROUPS, 1, GRP_W), acc(SSM_GROUPS, 1, N), acc(SSM_GROUPS, 1, N),
    ]
    return pcall(
        body, name=nm, grid=(nc, SSM_GROUPS), in_specs=in_specs, out_specs=out_specs, out_shape=out_shape,
        scratch=[pltpu.VMEM((SSM_GROUPS, GRP_W, N), F32), pltpu.VMEM((L, GRP_W), F32), pltpu.VMEM((L, GRP_W), F32)],
        args=[dy] + [xbc_raw, xbc_raw, conv_w, conv_b] * 3 + [dt_g, dtT_g, a_row, a_col, dvec, states], hook=hook)


def _softplus(x):
    return jnp.maximum(x, 0.0) + jnp.log(1.0 + jnp.exp(-jnp.abs(x)))


def ssd_dt_bwd(da, ddt, dt, dt_raw, a_row, dt_bias, nm):
    L = SSM_CHUNK

    def fn(d_a, d_dt, dtv, raw, ar, bias):
        r = lax.broadcasted_iota(jnp.int32, (L, L), 0)
        c = lax.broadcasted_iota(jnp.int32, (L, L), 1)
        triu = (r <= c).astype(BF16)
        acc = None
        for p in _split3(d_a):
            t = jnp.dot(triu, p, preferred_element_type=F32)
            acc = t if acc is None else acc + t
        d_dt = d_dt + acc * ar
        d_a_h = jnp.sum(acc * dtv, axis=0, keepdims=True)
        d_raw = d_dt * _sigmoid(raw + bias)
        return d_raw, d_a_h, jnp.sum(d_raw, axis=0, keepdims=True)

    return rowmap(fn, [da, ddt, dt, dt_raw], [a_row, dt_bias], [(SSM_HEADS, BF16)],
                  [(1, SSM_HEADS), (1, SSM_HEADS)], tm=L, name=nm)


GN_W = SSM_D_INNER // SSM_GROUPS


def mamba_fwd(h, p, nm, io):
    def in_proj(x, gg, w_zt, w_xbct, w_dtt):
        uu = _rms_fwd(x, gg).astype(BF16)
        return uu, _dot(uu, w_zt, NT), _dot(uu, w_xbct, NT), _dot(uu, w_dtt, NT)

    u, z, xbc_raw, dt_raw = rowmap(in_proj, [h], [p["ssm_norm"], p["w_zt"], p["w_xbct"], p["w_dtt"]],
                                   [(D_MODEL, BF16), (SSM_D_INNER, F32), (SSM_CONV_DIM, F32), (SSM_HEADS, F32)],
                                   tm=RESIDENT_TM, name=nm + "_in", hook=io.hook(nm + "_in"))
    dt, = rowmap(lambda r, b: (_softplus(r + b),), [dt_raw], [p["dt_bias"]], [(SSM_HEADS, F32)], tm=256,
                 name=nm + "_softplus")
    dt_g = dt.reshape(-1, SSM_GROUPS, HPG).transpose(1, 0, 2)
    dtT_g = dt_g.transpose(0, 2, 1)
    y, states = ssd_fwd(xbc_raw, p["conv_w"], p["conv_b"], dt_g, dtT_g, p["a_row"], p["a_col"], p["dvec"],
                        nm + "_ssd", hook=io.hook(nm + "_ssd"))

    def gate_norm_out(yv, zv, hv, gg, w_out):
        t = yv * _silu(zv)
        yn = jnp.concatenate([_rms_fwd(t[:, k * GN_W:(k + 1) * GN_W], gg[:, k * GN_W:(k + 1) * GN_W])
                              for k in range(SSM_GROUPS)], axis=1).astype(BF16)
        return yn, hv + _dot(yn, w_out)

    yn, out = rowmap(gate_norm_out, [y, z, h], [p["gate_norm"], p["w_out"]],
                     [(SSM_D_INNER, BF16), (D_MODEL, F32)], tm=RESIDENT_TM, name=nm + "_out")
    return out, (u, z, xbc_raw, dt_raw, dt, dt_g, dtT_g, y, states, yn)


def mamba_bwd(h, p, saved, dout, dout_bf, nm, io):
    u, z, xbc_raw, dt_raw, dt, dt_g, dtT_g, y, states, yn = saved
    g = {}
    io.put("w_out", mm(yn, dout_bf, ta=True, out_dtype=BF16, name=nm + "_dwout"))

    def gate_norm_bwd(d_o, yv, zv, gg, w_out):
        d = _dot(d_o, w_out, NT)
        sz = _silu(zv)
        t = yv * sz
        dts, dgs = [], []
        for k in range(SSM_GROUPS):
            sl = slice(k * GN_W, (k + 1) * GN_W)
            dt_k, dg_k = _rms_bwd(t[:, sl], gg[:, sl], d[:, sl])
            dts.append(dt_k)
            dgs.append(dg_k)
        d_t = jnp.concatenate(dts, axis=1)
        return d_t * sz, d_t * yv * _silu_grad(zv), jnp.concatenate(dgs, axis=1)

    dy, dz, g["gate_norm"] = rowmap(gate_norm_bwd, [dout_bf, y, z], [p["gate_norm"], p["w_out"]],
                                    [(SSM_D_INNER, F32), (SSM_D_INNER, BF16)], [(1, SSM_D_INNER)], tm=256,
                                    name=nm + "_dgatenorm")
    d_x, d_b, d_c, da_g, ddt_g, dd, dwx, dwb, dwc, dbx, dbb, dbc = ssd_bwd(
        dy, xbc_raw, p["conv_w"], p["conv_b"], dt_g, dtT_g, p["a_row"], p["a_col"], p["dvec"], states, nm + "_dssd",
        hook=io.hook(nm + "_dssd"))
    g["dvec"] = dd
    by_lane = lambda t: t.transpose(1, 0, 2).reshape(t.shape[1], -1)
    g["conv_w"] = jnp.concatenate([by_lane(dwx), by_lane(dwb), by_lane(dwc)], axis=1)
    g["conv_b"] = jnp.concatenate([by_lane(dbx), by_lane(dbb), by_lane(dbc)], axis=1)
    per_head = lambda t: t.transpose(1, 0, 2).reshape(-1, SSM_HEADS)
    ddt_raw, g["a"], g["dt_bias"] = ssd_dt_bwd(per_head(da_g), per_head(ddt_g), dt, dt_raw, p["a_heads"],
                                               p["dt_bias"], nm + "_ddt")
    dxbc_raw = conv_input_grad([d_x, d_b, d_c], p["conv_w"], nm + "_dconv")
    io.put("w_int", jnp.concatenate([mm(dz, u, ta=True, out_dtype=BF16, name=nm + "_dwz"),
                                     mm(dxbc_raw, u, ta=True, out_dtype=BF16, name=nm + "_dwxbc"),
                                     mm(ddt_raw, u, ta=True, out_dtype=BF16, name=nm + "_dwdt")], axis=0))
    dh, dh_bf, g["ssm_norm"] = du_norm_bwd(
        [(dz, p["w_zt"], False), (dxbc_raw, p["w_xbct"], False), (ddt_raw, p["w_dtt"], False)],
        h, p["ssm_norm"], dout, nm + "_du", hook=io.hook(nm + "_du"))
    return dh, dh_bf, g


KV_W = ATT_KV_HEADS * ATT_HEAD_DIM


def kv_fwd(h, p, nm):
    def kv_proj(x, gg, w_kv, gk):
        uu = _rms_fwd(x, gg).astype(BF16)
        t = _dot(uu, w_kv)
        ks = [_rms_fwd(t[:, j * ATT_HEAD_DIM:(j + 1) * ATT_HEAD_DIM], gk) for j in range(ATT_KV_HEADS)]
        return uu, t, jnp.concatenate(ks, axis=1), t[:, KV_W:]

    u, kv_raw, k, v = rowmap(kv_proj, [h], [p["kv_norm"], p["w_kv"], p["k_norm"]],
                             [(D_MODEL, BF16), (2 * KV_W, F32), (KV_W, F32), (KV_W, F32)], tm=RESIDENT_TM,
                             name=nm + "_proj")
    return k, v, (u, kv_raw)


def kv_bwd(h, p, saved, dk_cur, dk_prev, dv_cur, dv_prev, dout, nm, io):
    u, kv_raw = saved
    t_dim = h.shape[0]
    tm = ATT_WINDOW
    nb = t_dim // tm
    nxt = pl.BlockSpec((tm, KV_W), lambda i: (jnp.minimum(i + 1, nb - 1), 0))

    def fn(dkc, dkp, dvc, dvp, t, gg):
        live = pl.program_id(0) < nb - 1
        dk = dkc + jnp.where(live, dkp, 0.0)
        dv = dvc + jnp.where(live, dvp, 0.0)
        outs, dgs = [], None
        for j in range(ATT_KV_HEADS):
            sl = slice(j * ATT_HEAD_DIM, (j + 1) * ATT_HEAD_DIM)
            dx, dg = _rms_bwd(t[:, sl], gg, dk[:, sl])
            outs.append(dx)
            dgs = dg if dgs is None else dgs + dg
        return jnp.concatenate(outs + [dv], axis=1), dgs

    dkv_raw, dknorm = rowmap(fn, [dk_cur, (dk_prev, nxt), dv_cur, (dv_prev, nxt), kv_raw], [p["k_norm"]],
                             [(2 * KV_W, BF16)], [(1, ATT_HEAD_DIM)], tm=tm, name=nm + "_dknorm",
                             hook=io.hook(nm + "_dknorm"))
    g = {"k_norm": dknorm}
    io.put("w_kv", mm(u, dkv_raw, ta=True, out_dtype=BF16, name=nm + "_dwkv"))
    dh, dh_bf, g["kv_norm"] = du_norm_bwd([(dkv_raw, p["w_kv"], True)], h, p["kv_norm"], dout, nm + "_du",
                                          hook=io.hook(nm + "_du"))
    return dh, dh_bf, g


def _attn_specs(nb):
    blk = ATT_WINDOW
    cur = lambda i: (i, 0)
    prev = lambda i: (jnp.maximum(i - 1, 0), 0)
    return [
        pl.BlockSpec((blk, D_MODEL), cur),
        pl.BlockSpec((blk, KV_W), prev), pl.BlockSpec((blk, KV_W), cur),
        pl.BlockSpec((blk, KV_W), prev), pl.BlockSpec((blk, KV_W), cur),
        pl.BlockSpec((1, ATT_HEAD_DIM), lambda i: (0, 0)),
        pl.BlockSpec((ATT_KV_HEADS, ATT_GROUP * blk, 2 * blk), lambda i: (0, 0, 0)),
        pl.BlockSpec((ATT_KV_HEADS, ATT_GROUP * blk, 1), lambda i: (0, 0, 0)),
    ]


def attn_fwd(q_raw, k, v, q_norm, bias, sink_col, nm):
    t_dim = q_raw.shape[0]
    blk, hd = ATT_WINDOW, ATT_HEAD_DIM
    nb = t_dim // blk

    n_pairs = ATT_GROUP // 2

    def body(q_ref, kp_ref, kc_ref, vp_ref, vc_ref, qn_ref, bias_ref, sink_ref, o_ref):
        low = lax.broadcasted_iota(jnp.int32, (1, LANES), 1) < hd
        gq = jnp.concatenate([qn_ref[...], qn_ref[...]], axis=1)
        colk = lax.broadcasted_iota(jnp.int32, (1, 2 * blk), 1)
        live = (pl.program_id(0) > 0) | (colk >= blk)
        for kv in range(ATT_KV_HEADS):
            kraw = jnp.concatenate([kp_ref[...], kc_ref[...]], axis=0)
            vraw = jnp.concatenate([vp_ref[...], vc_ref[...]], axis=0)
            k_mine = jnp.where(low, kraw, 0.0) if kv == 0 else jnp.where(low, 0.0, kraw)
            v_mine = jnp.where(low, vraw, 0.0) if kv == 0 else jnp.where(low, 0.0, vraw)
            k_other = pltpu.roll(k_mine, hd, 1)
            v_other = pltpu.roll(v_mine, hd, 1)
            k_lo, k_hi = (k_mine, k_other) if kv == 0 else (k_other, k_mine)
            v_lo, v_hi = (v_mine, v_other) if kv == 0 else (v_other, v_mine)
            x = jnp.concatenate([q_ref[:, (kv * n_pairs + p) * LANES:(kv * n_pairs + p + 1) * LANES]
                                 for p in range(n_pairs)], axis=0)
            sq = x * x
            ms_lo = jnp.sum(jnp.where(low, sq, 0.0), axis=-1, keepdims=True) * (1.0 / hd)
            ms_hi = jnp.sum(jnp.where(low, 0.0, sq), axis=-1, keepdims=True) * (1.0 / hd)
            q = x * jnp.where(low, lax.rsqrt(ms_lo + EPS), lax.rsqrt(ms_hi + EPS)) * gq
            o_pair = None
            for par, (k_p, v_p) in enumerate(((k_lo, v_lo), (k_hi, v_hi))):
                rows = [slice((2 * p + par) * blk, (2 * p + par + 1) * blk) for p in range(n_pairs)]
                bias = jnp.concatenate([bias_ref[kv, r, :] for r in rows], axis=0)
                sink = jnp.concatenate([sink_ref[kv, r, :] for r in rows], axis=0)
                s = jnp.where(live, _dot(q, k_p, NT) * (hd ** -0.5) + bias, NEG)
                m = jnp.maximum(jnp.max(s, axis=-1, keepdims=True), sink)
                pexp = jnp.exp(s - m)
                inv_den = 1.0 / (jnp.sum(pexp, axis=-1, keepdims=True) + jnp.exp(sink - m))
                o_p = _dot(pexp * inv_den, v_p)
                o_pair = o_p if o_pair is None else o_pair + o_p
            for p in range(n_pairs):
                o_ref[:, (kv * n_pairs + p) * LANES:(kv * n_pairs + p + 1) * LANES] = (
                    o_pair[p * blk:(p + 1) * blk].astype(o_ref.dtype))

    out, = pcall(body, name=nm, grid=(nb,), in_specs=_attn_specs(nb),
                 out_specs=[pl.BlockSpec((blk, D_MODEL), lambda i: (i, 0))],
                 out_shape=[jax.ShapeDtypeStruct((t_dim, D_MODEL), BF16)],
                 args=[q_raw, k, k, v, v, q_norm, bias, sink_col])
    return out


def attn_bwd(do, q_raw, k, v, q_norm, bias, sink_col, nm, hook=None):
    t_dim = q_raw.shape[0]
    blk, hd = ATT_WINDOW, ATT_HEAD_DIM
    nb = t_dim // blk
    scale = hd ** -0.5

    def body(do_ref, q_ref, kp_ref, kc_ref, vp_ref, vc_ref, qn_ref, bias_ref, sink_ref,
             dq_ref, dkc_ref, dkp_ref, dvc_ref, dvp_ref, dbias_ref, dsink_ref, dqn_ref):
        @pl.when(pl.program_id(0) == 0)
        def _():
            dbias_ref[...] = jnp.zeros_like(dbias_ref)
            dsink_ref[...] = jnp.zeros_like(dsink_ref)
            dqn_ref[...] = jnp.zeros_like(dqn_ref)

        n_pairs = ATT_GROUP // 2
        low = lax.broadcasted_iota(jnp.int32, (1, LANES), 1) < hd
        gq = jnp.concatenate([qn_ref[...], qn_ref[...]], axis=1)
        colk = lax.broadcasted_iota(jnp.int32, (1, 2 * blk), 1)
        live = (pl.program_id(0) > 0) | (colk >= blk)
        ones = jnp.ones((2 * blk, LANES), BF16)
        for kv in range(ATT_KV_HEADS):
            kraw = jnp.concatenate([kp_ref[...], kc_ref[...]], axis=0)
            vraw = jnp.concatenate([vp_ref[...], vc_ref[...]], axis=0)
            k_mine = jnp.where(low, kraw, 0.0) if kv == 0 else jnp.where(low, 0.0, kraw)
            v_mine = jnp.where(low, vraw, 0.0) if kv == 0 else jnp.where(low, 0.0, vraw)
            k_other = pltpu.roll(k_mine, hd, 1)
            v_other = pltpu.roll(v_mine, hd, 1)
            k_lo, k_hi = (k_mine, k_other) if kv == 0 else (k_other, k_mine)
            v_lo, v_hi = (v_mine, v_other) if kv == 0 else (v_other, v_mine)
            tiles = [slice((kv * n_pairs + p) * LANES, (kv * n_pairs + p + 1) * LANES) for p in range(n_pairs)]
            x = jnp.concatenate([q_ref[:, t] for t in tiles], axis=0)
            do_pair = jnp.concatenate([do_ref[:, t] for t in tiles], axis=0)

            def head_mean(t):
                lo = jnp.sum(jnp.where(low, t, 0.0), axis=-1, keepdims=True)
                hi = jnp.sum(jnp.where(low, 0.0, t), axis=-1, keepdims=True)
                return jnp.where(low, lo, hi) * (1.0 / hd)

            rinv = lax.rsqrt(head_mean(x * x) + EPS)
            xh = x * rinv
            q = xh * gq
            dq_pair = None
            dk_pair = None
            dv_pair = None
            for par, (k_p, v_p) in enumerate(((k_lo, v_lo), (k_hi, v_hi))):
                rows = [slice((2 * p + par) * blk, (2 * p + par + 1) * blk) for p in range(n_pairs)]
                bias = jnp.concatenate([bias_ref[kv, r, :] for r in rows], axis=0)
                sink = jnp.concatenate([sink_ref[kv, r, :] for r in rows], axis=0)
                s = jnp.where(live, _dot(q, k_p, NT) * scale + bias, NEG)
                m = jnp.maximum(jnp.max(s, axis=-1, keepdims=True), sink)
                pexp = jnp.exp(s - m)
                e_sink = jnp.exp(sink - m)
                inv_den = 1.0 / (jnp.dot(pexp.astype(BF16), ones, preferred_element_type=F32) + e_sink)
                prob = pexp * jnp.concatenate([inv_den, inv_den], axis=1)
                dp = _dot(do_pair, v_p, NT)
                delta = jnp.sum(prob * dp, axis=-1, keepdims=True)
                ds = prob * (dp - delta)
                dsk = -(e_sink * inv_den[:, :1]) * delta
                for p, r in enumerate(rows):
                    dsink_ref[kv, r, :] += dsk[p * blk:(p + 1) * blk]
                    dbias_ref[kv, r, :] += ds[p * blk:(p + 1) * blk]
                ds_s = ds * scale
                mine = low if par == 0 else jnp.logical_not(low)
                dq_p = _dot(ds_s, k_p)
                dk_p = jnp.where(mine, _dot(ds_s, q, TN), 0.0)
                dv_p = jnp.where(mine, _dot(prob, do_pair, TN), 0.0)
                dq_pair = dq_p if dq_pair is None else dq_pair + dq_p
                dk_pair = dk_p if dk_pair is None else dk_pair + dk_p
                dv_pair = dv_p if dv_pair is None else dv_pair + dv_p
            dqn_ref[...] += jnp.sum(dq_pair * xh, axis=0, keepdims=True)
            dxh = dq_pair * gq
            dq_raw = rinv * (dxh - xh * head_mean(dxh * xh))
            for p, t in enumerate(tiles):
                dq_ref[:, t] = dq_raw[p * blk:(p + 1) * blk].astype(dq_ref.dtype)
            dkk = dk_pair + pltpu.roll(dk_pair, hd, 1)
            dvv = dv_pair + pltpu.roll(dv_pair, hd, 1)
            sl = slice(kv * hd, (kv + 1) * hd)
            dkp_ref[:, sl] = dkk[:blk, sl]
            dkc_ref[:, sl] = dkk[blk:, sl]
            dvp_ref[:, sl] = dvv[:blk, sl]
            dvc_ref[:, sl] = dvv[blk:, sl]

    cur = lambda i: (i, 0)
    row_spec = pl.BlockSpec((blk, KV_W), cur)
    out_specs = [
        pl.BlockSpec((blk, D_MODEL), cur), row_spec, row_spec, row_spec, row_spec,
        pl.BlockSpec((ATT_KV_HEADS, ATT_GROUP * blk, 2 * blk), lambda i: (0, 0, 0)),
        pl.BlockSpec((ATT_KV_HEADS, ATT_GROUP * blk, 1), lambda i: (0, 0, 0)),
        pl.BlockSpec((1, LANES), lambda i: (0, 0)),
    ]
    kvs = jax.ShapeDtypeStruct((t_dim, KV_W), F32)
    out_shape = [
        jax.ShapeDtypeStruct((t_dim, D_MODEL), BF16), kvs, kvs, kvs, kvs,
        jax.ShapeDtypeStruct((ATT_KV_HEADS, ATT_GROUP * blk, 2 * blk), F32),
        jax.ShapeDtypeStruct((ATT_KV_HEADS, ATT_GROUP * blk, 1), F32),
        jax.ShapeDtypeStruct((1, LANES), F32),
    ]
    *outs, dqn_pair = pcall(body, name=nm, grid=(nb,), in_specs=[pl.BlockSpec((blk, D_MODEL), cur)] + _attn_specs(nb),
                            out_specs=out_specs, out_shape=out_shape,
                            args=[do, q_raw, k, k, v, v, q_norm, bias, sink_col], hook=hook)
    return (*outs, dqn_pair[:, :hd] + dqn_pair[:, hd:])


def _t5_bucket_np():
    blk = ATT_WINDOW
    qi = np.arange(blk)[:, None] + blk
    kj = np.arange(2 * blk)[None, :]
    dist = qi - kj
    n = np.maximum(dist, 0)
    max_exact = REL_BUCKETS // 2
    nf = np.maximum(n, 1).astype(np.float32)
    large = max_exact + (np.log(nf / max_exact) / math.log(ATT_WINDOW / max_exact)
                         * (REL_BUCKETS - max_exact)).astype(np.int32)
    large = np.minimum(large, REL_BUCKETS - 1)
    bucket = np.where(n < max_exact, n, large)
    in_window = (dist >= 0) & (dist < ATT_WINDOW)
    return bucket, in_window


def attn_block_fwd(h, k, v, p, nm):
    def q_proj(x, gg, w_q):
        uu = _rms_fwd(x, gg).astype(BF16)
        return uu, _dot(uu, w_q)

    u, q_raw = rowmap(q_proj, [h], [p["attn_norm"], p["w_q"]], [(D_MODEL, BF16), (D_MODEL, F32)], tm=RESIDENT_TM,
                      name=nm + "_q")
    o = attn_fwd(q_raw, k, v, p["q_norm"], p["bias"], p["sink_col"], nm + "_core")
    out = mm(o, p["w_o"], res=h, name=nm + "_o")
    return out, (u, q_raw, o)


def attn_block_bwd(h, k, v, p, saved, dout, dout_bf, nm, io):
    u, q_raw, o = saved
    g = {}
    io.put("w_o", mm(o, dout_bf, ta=True, out_dtype=BF16, name=nm + "_dwo", hook=io.hook(nm + "_dwo")))
    do = mm(dout_bf, p["w_o"], tb=True, name=nm + "_do")
    dq_raw, dkc, dkp, dvc, dvp, g["bias"], g["sink_col"], g["q_norm"] = attn_bwd(
        do, q_raw, k, v, p["q_norm"], p["bias"], p["sink_col"], nm + "_dcore", hook=io.hook(nm + "_dcore"))
    io.put("w_q", mm(u, dq_raw, ta=True, out_dtype=BF16, name=nm + "_dwq"))
    dh, dh_bf, g["attn_norm"] = du_norm_bwd([(dq_raw, p["w_q"], True)], h, p["attn_norm"], dout, nm + "_du")
    return dh, dh_bf, g, (dkc, dkp, dvc, dvp)


FFN_TAGS = ["00", "01", "10", "11"]


def local_step(x, target, small, io):
    bucket, in_window = _t5_bucket_np()
    blk = ATT_WINDOW
    w = small

    fnorm = {tag: w["ffn_norm"][int(tag[0]), int(tag[1])][None, :] for tag in FFN_TAGS}
    a_neg = -jnp.exp(w["ssm_a_log"][0])

    def mamba_p():
        w_int = io.w("w_int")
        return dict(ssm_norm=w["ssm_norm"], w_zt=w_int[:SSM_D_INNER],
                    w_xbct=w_int[SSM_D_INNER:SSM_D_INNER + SSM_CONV_DIM], w_dtt=w_int[SSM_D_INNER + SSM_CONV_DIM:],
                    conv_w=w["ssm_conv_w"][0], conv_b=w["ssm_conv_b"], dt_bias=w["ssm_dt_bias"],
                    a_heads=a_neg[None, :], a_row=a_neg.reshape(SSM_GROUPS, 1, HPG),
                    a_col=a_neg.reshape(SSM_GROUPS, HPG, 1),
                    dvec=jnp.repeat(w["ssm_d"][0], SSM_HEAD_DIM)[None, :],
                    gate_norm=w["ssm_gate_norm"], w_out=io.w("w_out"))

    rb = w["rel_bias"]
    onehot3 = (np.arange(REL_BUCKETS)[:, None, None] == bucket[None]).astype(np.float32)
    bias = jnp.einsum("bh,bqk->hqk", rb, onehot3, precision=lax.Precision.HIGHEST)
    bias = jnp.where(in_window[None], bias, NEG)
    bias = bias.reshape(ATT_KV_HEADS, ATT_GROUP * blk, 2 * blk)
    sink_col = jnp.repeat(w["sinks"][0], blk).reshape(ATT_KV_HEADS, ATT_GROUP * blk, 1)

    def attn_p():
        return dict(attn_norm=w["attn_norm"], w_q=io.w("w_q"), q_norm=w["q_norm"], bias=bias, sink_col=sink_col,
                    w_o=io.w("w_o"))

    def kv_p():
        return dict(kv_norm=w["kv_norm"][None, :], w_kv=io.w("w_kv"), k_norm=w["k_norm"][None, :])

    h0 = x
    h0a, s_f00 = ffn_fwd(h0, fnorm["00"], "00", io)
    mp = mamba_p()
    h0b, s_m = mamba_fwd(h0a, mp, "ssm", io)
    h1, s_f01 = ffn_fwd(h0b, fnorm["01"], "01", io)
    kp = kv_p()
    k, v, s_kv = kv_fwd(h1, kp, "kv")
    h1a, s_f10 = ffn_fwd(h1, fnorm["10"], "10", io)
    ap = attn_p()
    h1b, s_a = attn_block_fwd(h1a, k, v, ap, "att")
    (dh, dh_bf, sq), s_f11 = ffn_fwd(h1b, fnorm["11"], "11", io, target=target)
    loss_part = jnp.sum(sq) * (0.5 / D_MODEL)

    fg = {}

    def ffn_back(tag, h_in, saved, dh, dh_bf):
        dh, dh_bf, dg = ffn_bwd(h_in, fnorm[tag], tag, saved, dh, dh_bf, io)
        fg[tag] = dg[0]
        return dh, dh_bf

    dh, dh_bf = ffn_back("11", h1b, s_f11, dh, dh_bf)
    dh, dh_bf, ga, dkv = attn_block_bwd(h1a, k, v, ap, s_a, dh, dh_bf, "att", io)
    dh, dh_bf = ffn_back("10", h1, s_f10, dh, dh_bf)
    dh, dh_bf, gk = kv_bwd(h1, kp, s_kv, *dkv, dh, "kv", io)
    dh, dh_bf = ffn_back("01", h0b, s_f01, dh, dh_bf)
    dh, dh_bf, gm = mamba_bwd(h0a, mp, s_m, dh, dh_bf, "ssm", io)
    dh, dh_bf = ffn_back("00", h0, s_f00, dh, dh_bf)
    grad_x = dh

    grads = {}
    grads["ffn_norm"] = jnp.stack([fg[tag] for tag in FFN_TAGS]).reshape(2, 2, D_MODEL)
    grads["ssm_norm"] = gm["ssm_norm"]
    grads["ssm_conv_w"] = gm["conv_w"][None]
    grads["ssm_conv_b"] = gm["conv_b"]
    grads["ssm_dt_bias"] = gm["dt_bias"]
    grads["ssm_a_log"] = gm["a"] * a_neg[None, :]
    grads["ssm_d"] = jnp.sum(gm["dvec"].reshape(SSM_HEADS, SSM_HEAD_DIM), axis=1)[None, :]
    grads["ssm_gate_norm"] = gm["gate_norm"]
    grads["kv_norm"] = gk["kv_norm"][0]
    grads["k_norm"] = gk["k_norm"][0]
    grads["attn_norm"] = ga["attn_norm"]
    grads["q_norm"] = ga["q_norm"]
    grads["sinks"] = jnp.sum(ga["sink_col"].reshape(ATT_HEADS, blk), axis=1)[None, :]
    onehot = (np.arange(REL_BUCKETS)[:, None] == bucket.reshape(1, -1)).astype(np.float32)
    dbias2d = ga["bias"].reshape(ATT_HEADS, blk * 2 * blk)
    grads["rel_bias"] = mm(jnp.asarray(onehot, BF16), dbias2d, tb=True, name="drelbias")
    return loss_part, grad_x, grads


def _adamw(g, w, m, v):
    m = ADAM_B1 * m + (1.0 - ADAM_B1) * g
    v = ADAM_B2 * v + (1.0 - ADAM_B2) * (g * g)
    m_hat = m / (1.0 - ADAM_B1 ** ADAM_STEP)
    v_hat = v / (1.0 - ADAM_B2 ** ADAM_STEP)
    delta = -ADAM_LR * (m_hat / (jnp.sqrt(v_hat) + ADAM_EPS) + ADAM_WD * w)
    return delta, m, v


def _slot_sum(r):
    g = r[0].astype(F32)
    for d in range(1, r.shape[0]):
        g = g + r[d].astype(F32)
    return g


def adamw_rows(recvs, w, m, v, name):
    n_l, rows, width = w.shape
    n_slots = recvs[0].shape[0]
    tr = 32
    assert rows % tr == 0, rows
    nt = rows // tr

    def body(*refs):
        r_refs = refs[:n_l]
        w_ref, m_ref, v_ref, g_o, d_o, m_o, v_o = refs[n_l:]
        li = pl.program_id(0)
        for k in range(n_l):
            @pl.when(li == k)
            def _(k=k):
                g = _slot_sum(r_refs[k])
                delta, m2, v2 = _adamw(g, w_ref[0], m_ref[0], v_ref[0])
                g_o[0] = g
                d_o[0] = delta
                m_o[0] = m2
                v_o[0] = v2

    def r_spec(k):
        return pl.BlockSpec((n_slots, tr, width),
                            lambda li, j: (0, jnp.where(li == k, j, jnp.where(li > k, nt - 1, 0)), 0))

    w_spec = pl.BlockSpec((1, tr, width), lambda li, j: (li, j, 0))
    shp = jax.ShapeDtypeStruct(w.shape, F32)
    return pcall(body, name=name, grid=(n_l, nt), in_specs=[r_spec(k) for k in range(n_l)] + [w_spec] * 3,
                 out_specs=[w_spec] * 4, out_shape=[shp] * 4, args=list(recvs) + [w, m, v])


def adamw_cols(recvs, w, m, v, name):
    n_l, rows, n = w.shape
    n_slots = recvs[0].shape[0]
    tr = 256
    nt = rows // tr

    def body(*refs):
        r_refs = refs[:n_l]
        w_ref, m_ref, v_ref, g_o, d_o, m_o, v_o = refs[n_l:]
        li = pl.program_id(0)
        for k in range(n_l):
            @pl.when(li == k)
            def _(k=k):
                g = _slot_sum(r_refs[k]).T
                delta, m2, v2 = _adamw(g, w_ref[0], m_ref[0], v_ref[0])
                g_o[0] = g
                d_o[0] = delta
                m_o[0] = m2
                v_o[0] = v2

    def r_spec(k):
        return pl.BlockSpec((n_slots, n, tr),
                            lambda li, j: (0, 0, jnp.where(li == k, j, jnp.where(li > k, nt - 1, 0))))

    w_spec = pl.BlockSpec((1, tr, n), lambda li, j: (li, j, 0))
    shp = jax.ShapeDtypeStruct(w.shape, F32)
    return pcall(body, name=name, grid=(n_l, nt), in_specs=[r_spec(k) for k in range(n_l)] + [w_spec] * 3,
                 out_specs=[w_spec] * 4, out_shape=[shp] * 4, args=list(recvs) + [w, m, v])


WEIGHT_NAMES = ["ffn_norm", "ffn_w1", "ffn_w3", "ffn_w2", "ssm_norm", "ssm_w_in", "ssm_conv_w", "ssm_conv_b",
                "ssm_dt_bias", "ssm_a_log", "ssm_d", "ssm_gate_norm", "ssm_w_out", "kv_norm", "w_kv", "k_norm",
                "attn_norm", "w_q", "q_norm", "sinks", "w_o", "rel_bias"]

SMALL = [
    ("ffn_norm", (2, 2, 1024), 2), ("ssm_norm", (1, 1024), 1), ("ssm_conv_w", (1, 4, 3072), 2),
    ("ssm_conv_b", (1, 3072), 1), ("ssm_gate_norm", (1, 2048), 1),
    ("ssm_dt_bias", (1, 32), None), ("ssm_a_log", (1, 32), None), ("ssm_d", (1, 32), None),
    ("kv_norm", (1024,), None), ("k_norm", (64,), None), ("attn_norm", (1, 1024), None),
    ("q_norm", (1, 64), None), ("sinks", (1, 16), None), ("rel_bias", (32, 16), None),
]
SMALL_W = 1024
SMALL_FULL_ROWS = 32
SMALL_LOCAL_ROWS = 48

MAT_GROUPS = {
    "f00_up": ["w1t_00", "w3t_00"], "f00_down": ["w2_00"], "f01": ["w1t_01", "w3t_01", "w2_01"],
    "f10": ["w1t_10", "w3t_10", "w2_10"], "f11": ["w1t_11", "w3t_11", "w2_11"],
    "ssm": ["w_int", "w_out"], "att": ["w_q", "w_o", "w_kv"],
    "f00_early": ["w2_00", "w1t_00"], "f00_late": ["w3t_00"],
}
FIRST_GATHER = "f00_up"
GATHER_PLAN = {"f00_upgate": ["f00_down", "ssm"], "ssm_in": ["f01"], "ssm_ssd": ["att", "f10"],
               "f01_upgate": ["f11"]}
SCATTER_A_PLAN = {"att_dwo": "f11", "kv_dknorm": "f10", "kv_du": "att", "f01_du": "f01", "ssm_du": "ssm",
                  "f00_dw3": "f00_early", "f00_du": "f00_late"}
SCATTER_B_PLAN = {"att_dcore": "f11", "f01_dw2": "att", "f01_dgate": "f10", "ssm_dssd": "f01", "f00_dgate": "ssm",
                  "f00_du": "f00_early"}
LAST_SCATTER = "f00_late"
SLOT_MAJOR = ("w_int",)


def _shard_shape(s, a):
    return s[:a] + (s[a] // N_DEV,) + s[a + 1:]


def _unshard_view(stack, shard_shape, axis):
    moved = jnp.moveaxis(stack, 0, axis)
    return moved.reshape(shard_shape[:axis] + (N_DEV * shard_shape[axis],) + shard_shape[axis + 1:])


def _small_local(arrs):
    flat = jnp.concatenate([arrs[n].reshape(-1) for n, _, _ in SMALL])
    return jnp.pad(flat, (0, SMALL_LOCAL_ROWS * LANES - flat.shape[0])).reshape(SMALL_LOCAL_ROWS, LANES)


def chip_partial(g4, ra, name):
    _, _, n, width = g4.shape

    def body(core_ref, g_ref, r_ref, o_ref):
        o_ref[0] = (g_ref[0, 0].astype(F32) + r_ref[0, 0].astype(F32)).astype(o_ref.dtype)

    grid_spec = pltpu.PrefetchScalarGridSpec(
        num_scalar_prefetch=1, grid=(N_CHIPS,),
        in_specs=[pl.BlockSpec((1, 1, n, width), lambda q, core: (q, core[0], 0, 0)),
                  pl.BlockSpec((1, 1, n, width), lambda q, core: (q, 0, 0, 0))],
        out_specs=pl.BlockSpec((1, n, width), lambda q, core: (q, 0, 0)))
    core = jnp.reshape(lax.axis_index("c"), (1,)).astype(jnp.int32)
    return pl.pallas_call(
        body, name=name, grid_spec=grid_spec, out_shape=jax.ShapeDtypeStruct((N_CHIPS, n, width), g4.dtype),
        compiler_params=pltpu.CompilerParams(dimension_semantics=("arbitrary",), vmem_limit_bytes=VMEM_LIMIT_BYTES),
    )(core, g4, ra)


class StepIO:
    def __init__(self, pieces):
        self.pieces = pieces
        self.full = {}
        self.grad = {}
        self.from_sibling = {}
        self.recv = {}

    def w(self, name):
        return self.full[name]

    def put(self, name, g):
        self.grad[name] = g

    def _by_chip_core(self, name):
        g = self.grad[name]
        return g.reshape((N_CHIPS, 2, g.shape[0] // N_DEV) + g.shape[1:])

    def gather_items(self, groups):
        names = [n for grp in groups for n in MAT_GROUPS[grp]]
        items = [("g2", self.pieces[n], None if n in SLOT_MAJOR else 0) for n in names]

        def sink(outs):
            for n, o in zip(names, outs):
                self.full[n] = o.reshape((-1,) + o.shape[2:]) if n in SLOT_MAJOR else o

        return items, sink

    def scatter_a_items(self, group):
        names = MAT_GROUPS[group]
        items = [("sa", self._by_chip_core(n), None) for n in names]

        def sink(outs):
            for n, o in zip(names, outs):
                self.from_sibling[n] = o

        return items, sink

    def scatter_b_items(self, group):
        names = MAT_GROUPS[group]
        items = [("sb", chip_partial(self._by_chip_core(n), self.from_sibling[n], "partial_" + n), None)
                 for n in names]

        def sink(outs):
            for n, o in zip(names, outs):
                self.recv[n] = o

        return items, sink

    def hook(self, site):
        parts = []
        if site in GATHER_PLAN:
            parts.append(self.gather_items(GATHER_PLAN[site]))
        if site in SCATTER_A_PLAN:
            parts.append(self.scatter_a_items(SCATTER_A_PLAN[site]))
        if site in SCATTER_B_PLAN:
            parts.append(self.scatter_b_items(SCATTER_B_PLAN[site]))
        if not parts:
            return None
        return combine_hooks(parts)


def combine_hooks(parts):
    items = [it for its, _ in parts for it in its]

    def sink(outs):
        p = 0
        for its, snk in parts:
            snk(outs[p:p + len(its)])
            p += len(its)

    return Comm(items), sink


def step(x, target, wts, ms, vs):
    me = _my_index()

    pieces = {}
    for li in range(2):
        for hi in range(2):
            tag = "%d%d" % (li, hi)
            pieces["w1t_" + tag] = wts["ffn_w1"][li, hi].T.astype(BF16)
            pieces["w3t_" + tag] = wts["ffn_w3"][li, hi].T.astype(BF16)
            pieces["w2_" + tag] = wts["ffn_w2"][li, hi].astype(BF16)
    pieces["w_int"] = wts["ssm_w_in"][0].T.astype(BF16)
    pieces["w_out"] = wts["ssm_w_out"][0].astype(BF16)
    pieces["w_kv"] = wts["w_kv"].astype(BF16)
    pieces["w_q"] = wts["w_q"][0].astype(BF16)
    pieces["w_o"] = wts["w_o"][0].astype(BF16)
    io = StepIO(pieces)

    small_sharded = [(n, s, a) for n, s, a in SMALL if a is not None]
    loc = jnp.concatenate([wts[n].reshape(-1) for n, _, _ in small_sharded])
    loc_rows = -(-loc.shape[0] // (8 * LANES)) * 8
    loc = jnp.pad(loc, (0, loc_rows * LANES - loc.shape[0])).reshape(loc_rows, LANES)
    got_small = []
    comm, sink = combine_hooks([io.gather_items([FIRST_GATHER]), ([("g", loc, None)], got_small.extend)])
    sink(comm_only(comm, "gather_first"))
    gath_small = got_small[0].reshape(N_DEV, -1)
    small = {}
    off = 0
    for n, s, a in small_sharded:
        shard = _shard_shape(s, a)
        cnt = int(np.prod(shard))
        small[n] = _unshard_view(gath_small[:, off:off + cnt].reshape((N_DEV,) + shard), shard, a)
        off += cnt
    for n, s, a in SMALL:
        if a is None:
            small[n] = wts[n]

    loss_part, grad_x, g_small_local = local_step(x[0], target[0], small, io)
    loss = lax.psum(loss_part, ("x", "y", "c"))

    small_flat = jnp.concatenate([g_small_local[n].reshape(-1) for n, _, _ in SMALL])
    small_buf = jnp.pad(small_flat, (0, SMALL_FULL_ROWS * SMALL_W - small_flat.shape[0]))
    small_buf = small_buf.reshape(SMALL_FULL_ROWS, SMALL_W)
    got_small = []
    comm, sink = combine_hooks([io.scatter_b_items(LAST_SCATTER), ([("g", small_buf, None)], got_small.extend)])
    sink(comm_only(comm, "exchange_last"))
    small_all = got_small[0]

    def sum_body(r_ref, o_ref):
        o_ref[...] = _slot_sum(r_ref)

    vmem = pl.BlockSpec(memory_space=pltpu.VMEM)
    small_sum, = pcall(sum_body, name="sum_small", grid=(), in_specs=[vmem], out_specs=[vmem],
                       out_shape=[jax.ShapeDtypeStruct((SMALL_FULL_ROWS, SMALL_W), F32)], args=[small_all])
    small_sum = small_sum.reshape(-1)
    g_small = {}
    off = 0
    for n, s, a in SMALL:
        cnt = int(np.prod(s))
        gfull = small_sum[off:off + cnt].reshape(s)
        off += cnt
        if a is None:
            g_small[n] = gfull
        else:
            width = s[a] // N_DEV
            g_small[n] = lax.dynamic_slice_in_dim(gfull, me * width, width, axis=a)

    out = {}

    def emit(name, res, shape):
        for kind, arr in zip(("grad", "delta", "new_m", "new_v"), res):
            out[kind + "_" + name] = arr.reshape(shape)

    for name, key in (("ffn_w1", "w1t_"), ("ffn_w3", "w3t_")):
        shp = wts[name].shape
        view = lambda t: t.reshape((4,) + shp[2:])
        res = adamw_cols([io.recv[key + tag] for tag in FFN_TAGS], view(wts[name]), view(ms[name]), view(vs[name]),
                         "adamw_" + name)
        emit(name, res, shp)
    shp = wts["ffn_w2"].shape
    view = lambda t: t.reshape((4,) + shp[2:])
    res = adamw_rows([io.recv["w2_" + tag] for tag in FFN_TAGS], view(wts["ffn_w2"]), view(ms["ffn_w2"]),
                     view(vs["ffn_w2"]), "adamw_ffn_w2")
    emit("ffn_w2", res, shp)
    res = adamw_cols([io.recv["w_int"]], wts["ssm_w_in"], ms["ssm_w_in"], vs["ssm_w_in"], "adamw_ssm_w_in")
    emit("ssm_w_in", res, wts["ssm_w_in"].shape)
    for name, key in (("ssm_w_out", "w_out"), ("w_kv", "w_kv"), ("w_q", "w_q"), ("w_o", "w_o")):
        shp = wts[name].shape
        view = lambda t: t.reshape((1,) + shp[-2:])
        res = adamw_rows([io.recv[key]], view(wts[name]), view(ms[name]), view(vs[name]), "adamw_" + name)
        emit(name, res, shp)

    res_s = rowmap(lambda gg, ww, mm_, vv: _adamw(gg, ww, mm_, vv),
                   [_small_local(g_small), _small_local(wts), _small_local(ms), _small_local(vs)], [],
                   [(LANES, F32)] * 3, tm=SMALL_LOCAL_ROWS, name="adamw_small")
    flat_s = [r.reshape(-1) for r in res_s]
    off = 0
    for n, s, a in SMALL:
        shard = s if a is None else _shard_shape(s, a)
        cnt = int(np.prod(shard))
        out["grad_" + n] = g_small[n]
        for kind, arr in zip(("delta", "new_m", "new_v"), flat_s):
            out[kind + "_" + n] = arr[off:off + cnt].reshape(shard)
        off += cnt
    out["loss"] = loss
    out["grad_x"] = grad_x[None]
    return out


def kernel(x, ffn_norm, ffn_w1, ffn_w3, ffn_w2, ssm_norm, ssm_w_in, ssm_conv_w, ssm_conv_b, ssm_dt_bias, ssm_a_log, ssm_d, ssm_gate_norm, ssm_w_out, kv_norm, w_kv, k_norm, attn_norm, w_q, q_norm, sinks, w_o, rel_bias, loss_target, m_ffn_norm, m_ffn_w1, m_ffn_w3, m_ffn_w2, m_ssm_norm, m_ssm_w_in, m_ssm_conv_w, m_ssm_conv_b, m_ssm_dt_bias, m_ssm_a_log, m_ssm_d, m_ssm_gate_norm, m_ssm_w_out, m_kv_norm, m_w_kv, m_k_norm, m_attn_norm, m_w_q, m_q_norm, m_sinks, m_w_o, m_rel_bias, v_ffn_norm, v_ffn_w1, v_ffn_w3, v_ffn_w2, v_ssm_norm, v_ssm_w_in, v_ssm_conv_w, v_ssm_conv_b, v_ssm_dt_bias, v_ssm_a_log, v_ssm_d, v_ssm_gate_norm, v_ssm_w_out, v_kv_norm, v_w_kv, v_k_norm, v_attn_norm, v_w_q, v_q_norm, v_sinks, v_w_o, v_rel_bias):
    args = locals()
    wts = {n: args[n] for n in WEIGHT_NAMES}
    ms = {n: args["m_" + n] for n in WEIGHT_NAMES}
    vs = {n: args["v_" + n] for n in WEIGHT_NAMES}
    out = step(x, loss_target, wts, ms, vs)
    result = [out["loss"], out["grad_x"]]
    for kind in ("grad", "delta", "new_m", "new_v"):
        result += [out[kind + "_" + n] for n in WEIGHT_NAMES]
    return tuple(result)
```

```python
import functools
import math
import operator

import numpy as np
import jax
import jax.numpy as jnp
from jax import lax
from jax.experimental import pallas as pl
from jax.experimental.pallas import tpu as pltpu

F32 = jnp.float32
BF16 = jnp.bfloat16

D_MODEL = 1024
D_FF = 2816
N_DEV = 8
SSM_D_INNER = 2048
SSM_HEAD_DIM = 64
SSM_HEADS = 32
SSM_GROUPS = 4
SSM_STATE = 128
SSM_CONV = 4
SSM_CHUNK = 256
SSM_CONV_DIM = SSM_D_INNER + 2 * SSM_GROUPS * SSM_STATE
SSM_IN_DIM = SSM_D_INNER + SSM_CONV_DIM + SSM_HEADS
ATT_HEAD_DIM = 64
ATT_HEADS = 16
ATT_KV_HEADS = 2
ATT_GROUP = 8
ATT_WINDOW = 128
REL_BUCKETS = 32
EPS = 1e-6
NEG = -1e30

ADAM_LR = 0.001
ADAM_B1 = 0.9
ADAM_B2 = 0.999
ADAM_EPS = 1e-08
ADAM_WD = 0.01
ADAM_STEP = 10

VMEM_LIMIT_BYTES = 52 * 1024 * 1024
LANES = 128
MESH_ID = pl.DeviceIdType.MESH
ANY_SPEC = pl.BlockSpec(memory_space=pl.ANY)

NT = (((1,), (1,)), ((), ()))
TN = (((0,), (0,)), ((), ()))
NN = (((1,), (0,)), ((), ()))


def _pick(dim, cands):
    for c in cands:
        if dim % c == 0:
            return c
    return dim


def _my_index():
    return 4 * lax.axis_index("x") + 2 * lax.axis_index("y") + lax.axis_index("c")


def _peer(k):
    x, y, c = lax.axis_index("x"), lax.axis_index("y"), lax.axis_index("c")
    px = 1 - x if (k >> 2) & 1 else x
    py = 1 - y if (k >> 1) & 1 else y
    pc = 1 - c if k & 1 else c
    return (px, py, pc), 4 * px + 2 * py + pc


def _piece(ref, axis, d, n):
    if axis is None:
        return ref.at[d]
    return ref.at[(slice(None),) * axis + (pl.ds(pl.multiple_of(d * n, 8), n),)]


SIBLING = 1
CHIP_PEERS = (4, 2, 6)
N_CHIPS = 4
SEMS_PER_ITEM = N_DEV - 1


def _my_chip():
    return 2 * lax.axis_index("x") + lax.axis_index("y")


class Comm:
    def __init__(self, items):
        self.items = list(items)

    def dst_shapes(self):
        out = []
        for kind, src, axis in self.items:
            s = tuple(src.shape)
            if kind == "g":
                shp = (N_DEV,) + s
            elif kind == "g2":
                shp = (N_DEV,) + s if axis is None else s[:axis] + (N_DEV * s[axis],) + s[axis + 1:]
            elif kind == "sa":
                shp = (s[0], 1) + s[2:]
            else:
                shp = s
            out.append(jax.ShapeDtypeStruct(shp, src.dtype))
        return out

    def scratch(self):
        n = len(self.items)
        return [pltpu.SemaphoreType.DMA((n * SEMS_PER_ITEM,)), pltpu.SemaphoreType.DMA((n * SEMS_PER_ITEM,)),
                pltpu.SemaphoreType.DMA((n,))]

    def _run(self, srcs, dsts, sems, starting):
        send_sems, recv_sems, local_sems = sems
        me = _my_index()
        core = lax.axis_index("c")
        chip = _my_chip()
        for i, (kind, src, axis) in enumerate(self.items):
            s_ref, d_ref = srcs[i], dsts[i]
            base = i * SEMS_PER_ITEM

            def rdma(src_ref, dst_ref, j, peer):
                return pltpu.make_async_remote_copy(
                    src_ref=src_ref, dst_ref=dst_ref, send_sem=send_sems.at[base + j], recv_sem=recv_sems.at[base + j],
                    device_id=peer, device_id_type=MESH_ID)

            if kind == "g":
                local = pltpu.make_async_copy(s_ref, d_ref.at[me], local_sems.at[i])
                outs = [rdma(s_ref, d_ref.at[me], k - 1, _peer(k)[0]) for k in range(1, N_DEV)]
                if starting:
                    local.start()
                    for cp in outs:
                        cp.start()
                else:
                    for k in range(1, N_DEV):
                        rdma(s_ref, d_ref.at[_peer(k)[1]], k - 1, _peer(k)[0]).wait_recv()
                    for cp in outs:
                        cp.wait_send()
                    local.wait()
            elif kind == "g2":
                n = None if axis is None else src.shape[axis]
                mine = _piece(d_ref, axis, me, n)
                sib = _peer(SIBLING)[0]
                local = pltpu.make_async_copy(s_ref, mine, local_sems.at[i])
                outs = [rdma(s_ref, mine, 0, sib)] + [rdma(s_ref, mine, 1 + j, _peer(k)[0])
                                                      for j, k in enumerate(CHIP_PEERS)]
                if starting:
                    local.start()
                    for cp in outs:
                        cp.start()
                else:
                    passed = []
                    for j, k in enumerate(CHIP_PEERS):
                        theirs = _piece(d_ref, axis, _peer(k)[1], n)
                        rdma(s_ref, theirs, 1 + j, _peer(k)[0]).wait_recv()
                        fwd = rdma(theirs, theirs, 4 + j, sib)
                        fwd.start()
                        passed.append(fwd)
                    rdma(s_ref, _piece(d_ref, axis, _peer(SIBLING)[1], n), 0, sib).wait_recv()
                    for j, k in enumerate(CHIP_PEERS):
                        rdma(s_ref, _piece(d_ref, axis, _peer(k ^ SIBLING)[1], n), 4 + j, sib).wait_recv()
                    for cp in outs + passed:
                        cp.wait_send()
                    local.wait()
            elif kind == "sa":
                cp = rdma(s_ref.at[(slice(None), pl.ds(1 - core, 1))], d_ref, 0, _peer(SIBLING)[0])
                if starting:
                    cp.start()
                else:
                    cp.wait_recv()
                    cp.wait_send()
            else:
                local = pltpu.make_async_copy(s_ref.at[chip], d_ref.at[chip], local_sems.at[i])
                outs = [rdma(s_ref.at[_peer(k)[1] >> 1], d_ref.at[chip], 1 + j, _peer(k)[0])
                        for j, k in enumerate(CHIP_PEERS)]
                if starting:
                    local.start()
                    for cp in outs:
                        cp.start()
                else:
                    for j, k in enumerate(CHIP_PEERS):
                        rdma(s_ref.at[chip], d_ref.at[_peer(k)[1] >> 1], 1 + j, _peer(k)[0]).wait_recv()
                    for cp in outs:
                        cp.wait_send()
                    local.wait()

    def start(self, srcs, dsts, sems):
        self._run(srcs, dsts, sems, True)

    def wait(self, srcs, dsts, sems):
        self._run(srcs, dsts, sems, False)


def pcall(body, *, name, grid, in_specs, out_specs, out_shape, args, scratch=(), hook=None):
    cparams = pltpu.CompilerParams(dimension_semantics=("arbitrary",) * len(grid), vmem_limit_bytes=VMEM_LIMIT_BYTES)
    if hook is None:
        outs = pl.pallas_call(body, name=name, grid=grid, in_specs=list(in_specs), out_specs=list(out_specs),
                              out_shape=list(out_shape), scratch_shapes=list(scratch), compiler_params=cparams)(*args)
        return list(outs)
    comm, sink = hook
    n_in, n_out, n_scr, n_it = len(args), len(out_shape), len(scratch), len(comm.items)
    dims = tuple(grid)

    def wrapped(*refs):
        p = 0
        ins = refs[p:p + n_in]
        p += n_in
        csrc = refs[p:p + n_it]
        p += n_it
        outs = refs[p:p + n_out]
        p += n_out
        cdst = refs[p:p + n_it]
        p += n_it
        scr = refs[p:p + n_scr]
        p += n_scr
        sems = refs[p:p + 3]
        if dims:
            ids = [pl.program_id(a) for a in range(len(dims))]
            first = functools.reduce(operator.and_, [i == 0 for i in ids])
            last = functools.reduce(operator.and_, [i == d - 1 for i, d in zip(ids, dims)])

            @pl.when(first)
            def _():
                comm.start(csrc, cdst, sems)

            body(*ins, *outs, *scr)

            @pl.when(last)
            def _():
                comm.wait(csrc, cdst, sems)
        else:
            comm.start(csrc, cdst, sems)
            body(*ins, *outs, *scr)
            comm.wait(csrc, cdst, sems)

    res = pl.pallas_call(
        wrapped, name=name, grid=grid,
        in_specs=list(in_specs) + [ANY_SPEC] * n_it, out_specs=list(out_specs) + [ANY_SPEC] * n_it,
        out_shape=list(out_shape) + comm.dst_shapes(), scratch_shapes=list(scratch) + comm.scratch(),
        compiler_params=cparams,
    )(*args, *[src for _, src, _ in comm.items])
    res = list(res)
    sink(res[n_out:])
    return res[:n_out]


def comm_only(comm, name):
    got = []
    pcall(lambda *refs: None, name=name, grid=(), in_specs=[], out_specs=[], out_shape=[], args=[],
          hook=(comm, got.extend))
    return got


def mm(a, b, *, ta=False, tb=False, out_dtype=F32, res=None, alpha=1.0, name, hook=None):
    if ta:
        k_dim, m_dim = a.shape
    else:
        m_dim, k_dim = a.shape
    if tb:
        n_dim, k2 = b.shape
    else:
        k2, n_dim = b.shape
    assert k_dim == k2, (a.shape, b.shape, ta, tb)
    tn = _pick(n_dim, (1024, 1408, 512, 256, 128))
    tm = _pick(m_dim, (1024, 1408, 512, 256, 128)) if tn <= 1024 else _pick(m_dim, (512, 256, 128))
    tk = _pick(k_dim, (1024, 512, 256, 128)) if ta else _pick(k_dim, (512, 1408, 256, 128))
    nk = k_dim // tk
    has_res = res is not None
    dn = (((0 if ta else 1,), (1 if tb else 0,)), ((), ()))

    def body(*refs):
        if has_res:
            a_ref, b_ref, r_ref, o_ref, acc_ref = refs
        else:
            a_ref, b_ref, o_ref, acc_ref = refs
        k = pl.program_id(2)

        @pl.when(k == 0)
        def _():
            acc_ref[...] = jnp.zeros_like(acc_ref)

        acc_ref[...] += lax.dot_general(a_ref[...].astype(BF16), b_ref[...].astype(BF16), dn,
                                        preferred_element_type=F32)

        @pl.when(k == nk - 1)
        def _():
            r = acc_ref[...]
            if alpha != 1.0:
                r = r * alpha
            if has_res:
                r = r_ref[...] + r
            o_ref[...] = r.astype(o_ref.dtype)

    a_spec = pl.BlockSpec((tk, tm), lambda i, j, k: (k, i)) if ta else pl.BlockSpec((tm, tk), lambda i, j, k: (i, k))
    b_spec = pl.BlockSpec((tn, tk), lambda i, j, k: (j, k)) if tb else pl.BlockSpec((tk, tn), lambda i, j, k: (k, j))
    o_spec = pl.BlockSpec((tm, tn), lambda i, j, k: (i, j))
    in_specs = [a_spec, b_spec] + ([o_spec] if has_res else [])
    args = [a, b] + ([res] if has_res else [])
    out, = pcall(body, name=name, grid=(m_dim // tm, n_dim // tn, nk), in_specs=in_specs, out_specs=[o_spec],
                 out_shape=[jax.ShapeDtypeStruct((m_dim, n_dim), out_dtype)], args=args,
                 scratch=[pltpu.VMEM((tm, tn), F32)], hook=hook)
    return out


def rowmap(fn, rows, consts=(), out_rows=(), out_accs=(), *, tm, name, hook=None):
    first = rows[0][0] if isinstance(rows[0], tuple) else rows[0]
    t_dim = first.shape[0]
    assert t_dim % tm == 0, (t_dim, tm)
    n_r, n_c, n_o = len(rows), len(consts), len(out_rows)

    def body(*refs):
        ins = [r[...] for r in refs[:n_r + n_c]]
        o_refs = refs[n_r + n_c:]
        outs = tuple(fn(*ins))
        for o_ref, val in zip(o_refs[:n_o], outs[:n_o]):
            o_ref[...] = val.astype(o_ref.dtype)
        if out_accs:
            @pl.when(pl.program_id(0) == 0)
            def _():
                for o_ref in o_refs[n_o:]:
                    o_ref[...] = jnp.zeros_like(o_ref)

            for o_ref, val in zip(o_refs[n_o:], outs[n_o:]):
                o_ref[...] += val

    in_specs, args = [], []
    for r in rows:
        if isinstance(r, tuple):
            args.append(r[0])
            in_specs.append(r[1])
        else:
            args.append(r)
            in_specs.append(pl.BlockSpec((tm, r.shape[1]), lambda i: (i, 0)))
    for c in consts:
        args.append(c)
        in_specs.append(pl.BlockSpec(c.shape, lambda i, nd=c.ndim: (0,) * nd))
    out_specs = [pl.BlockSpec((tm, w), lambda i: (i, 0)) for (w, _) in out_rows]
    out_specs += [pl.BlockSpec(s, lambda i, nd=len(s): (0,) * nd) for s in out_accs]
    out_shape = [jax.ShapeDtypeStruct((t_dim, w), dt) for (w, dt) in out_rows]
    out_shape += [jax.ShapeDtypeStruct(s, F32) for s in out_accs]
    return pcall(body, name=name, grid=(t_dim // tm,), in_specs=in_specs, out_specs=out_specs, out_shape=out_shape,
                 args=args, hook=hook)


def _rms_fwd(x, g):
    r = lax.rsqrt(jnp.mean(x * x, axis=-1, keepdims=True) + EPS)
    return x * r * g


def _rms_bwd(x, g, dy):
    r = lax.rsqrt(jnp.mean(x * x, axis=-1, keepdims=True) + EPS)
    xh = x * r
    dg = jnp.sum(dy * xh, axis=0, keepdims=True)
    dxh = dy * g
    dx = r * (dxh - xh * jnp.mean(dxh * xh, axis=-1, keepdims=True))
    return dx, dg


def _sigmoid(x):
    return 1.0 / (1.0 + jnp.exp(-x))


def _silu(x):
    return x * _sigmoid(x)


def _silu_grad(x):
    s = _sigmoid(x)
    return s * (1.0 + x * (1.0 - s))


def _split3(x):
    hi = x.astype(BF16)
    r1 = x - hi.astype(F32)
    mid = r1.astype(BF16)
    lo = (r1 - mid.astype(F32)).astype(BF16)
    return hi, mid, lo


def _dot(a, b, dn=NN):
    return lax.dot_general(a.astype(BF16), b.astype(BF16), dn, preferred_element_type=F32)


FFN_TN = 1408
RESIDENT_TM = 512


def ffn_upgate(h, g, w1t, w3t, nm, hook=None):
    t_dim = h.shape[0]
    tm = _pick(t_dim, (512, 256, 128))
    tn = FFN_TN

    n_j = D_FF // tn
    u_w = D_MODEL // n_j

    def body(h_ref, g_ref, w1_ref, w3_ref, u_ref, a_ref, b_ref, hm_ref):
        uu = _rms_fwd(h_ref[...], g_ref[...]).astype(BF16)
        for j in range(n_j):
            @pl.when(pl.program_id(0) == j)
            def _(j=j):
                u_ref[...] = uu[:, j * u_w:(j + 1) * u_w]

        a = lax.dot_general(uu, w1_ref[...], NT, preferred_element_type=F32)
        b = lax.dot_general(uu, w3_ref[...], NT, preferred_element_type=F32)
        a_ref[...] = a.astype(a_ref.dtype)
        b_ref[...] = b.astype(b_ref.dtype)
        hm_ref[...] = (_silu(a) * b).astype(hm_ref.dtype)

    row_spec = pl.BlockSpec((tm, D_MODEL), lambda j, i: (i, 0))
    w_spec = pl.BlockSpec((tn, D_MODEL), lambda j, i: (j, 0))
    o_spec = pl.BlockSpec((tm, tn), lambda j, i: (i, j))
    o_shape = jax.ShapeDtypeStruct((t_dim, D_FF), BF16)
    return pcall(body, name=nm, grid=(D_FF // tn, t_dim // tm),
                 in_specs=[row_spec, pl.BlockSpec((1, D_MODEL), lambda j, i: (0, 0)), w_spec, w_spec],
                 out_specs=[pl.BlockSpec((tm, u_w), lambda j, i: (i, j))] + [o_spec] * 3,
                 out_shape=[jax.ShapeDtypeStruct((t_dim, D_MODEL), BF16)] + [o_shape] * 3,
                 args=[h, g, w1t, w3t], hook=hook)


def ffn_dgate(dout_bf, w2, a, b, nm, hook=None):
    t_dim = dout_bf.shape[0]
    tm = _pick(t_dim, (512, 256, 128))
    tn = FFN_TN

    def body(d_ref, w2_ref, a_ref, b_ref, da_ref, db_ref):
        dhm = 0.5 * lax.dot_general(d_ref[...], w2_ref[...], NT, preferred_element_type=F32)
        av = a_ref[...].astype(F32)
        bv = b_ref[...].astype(F32)
        sg = _sigmoid(av)
        da_ref[...] = (dhm * bv * (sg * (1.0 + av * (1.0 - sg)))).astype(da_ref.dtype)
        db_ref[...] = (dhm * (av * sg)).astype(db_ref.dtype)

    t_spec = pl.BlockSpec((tm, tn), lambda j, i: (i, j))
    o_shape = jax.ShapeDtypeStruct((t_dim, D_FF), BF16)
    return pcall(body, name=nm, grid=(D_FF // tn, t_dim // tm),
                 in_specs=[pl.BlockSpec((tm, D_MODEL), lambda j, i: (i, 0)),
                           pl.BlockSpec((tn, D_MODEL), lambda j, i: (j, 0)), t_spec, t_spec],
                 out_specs=[t_spec] * 2, out_shape=[o_shape] * 2, args=[dout_bf, w2, a, b], hook=hook)


def ffn_fwd(h, g, tag, io, target=None):
    nm = "f" + tag
    u, a, b, hm = ffn_upgate(h, g, io.w("w1t_" + tag), io.w("w3t_" + tag), nm + "_upgate",
                             hook=io.hook(nm + "_upgate"))
    if target is None:
        return mm(hm, io.w("w2_" + tag), res=h, alpha=0.5, name=nm + "_down"), (u, a, b, hm)

    def down_loss(hmv, hv, t, w2):
        e = hv + 0.5 * _dot(hmv, w2) - t
        d = e * (1.0 / D_MODEL)
        return d, d, jnp.sum(e * e, axis=0, keepdims=True)

    res = rowmap(down_loss, [hm, h, target], [io.w("w2_" + tag)], [(D_MODEL, F32), (D_MODEL, BF16)],
                 [(1, D_MODEL)], tm=RESIDENT_TM, name=nm + "_down_loss")
    return res, (u, a, b, hm)


def du_norm_bwd(pairs, h, g, dout, nm, hook=None):
    t_dim = h.shape[0]
    tm = RESIDENT_TM
    n_p = len(pairs)

    def body(*refs):
        h_ref, d_ref, g_ref = refs[2 * n_p:2 * n_p + 3]
        dh_ref, dhb_ref, dg_ref = refs[2 * n_p + 3:]
        du = None
        for p, (_, _, tb) in enumerate(pairs):
            t = lax.dot_general(refs[2 * p][...].astype(BF16), refs[2 * p + 1][...].astype(BF16), NT if tb else NN,
                                preferred_element_type=F32)
            du = t if du is None else du + t
        dx, dg = _rms_bwd(h_ref[...], g_ref[...], du)
        dh = d_ref[...] + dx
        dh_ref[...] = dh
        dhb_ref[...] = dh.astype(dhb_ref.dtype)

        @pl.when(pl.program_id(0) == 0)
        def _():
            dg_ref[...] = jnp.zeros_like(dg_ref)

        dg_ref[...] += dg

    in_specs, args = [], []
    for a, b, _ in pairs:
        in_specs += [pl.BlockSpec((tm, a.shape[1]), lambda i: (i, 0)), pl.BlockSpec(b.shape, lambda i: (0, 0))]
        args += [a, b]
    row_spec = pl.BlockSpec((tm, D_MODEL), lambda i: (i, 0))
    vec_spec = pl.BlockSpec((1, D_MODEL), lambda i: (0, 0))
    return pcall(body, name=nm, grid=(t_dim // tm,), in_specs=in_specs + [row_spec, row_spec, vec_spec],
                 out_specs=[row_spec, row_spec, vec_spec],
                 out_shape=[jax.ShapeDtypeStruct((t_dim, D_MODEL), F32), jax.ShapeDtypeStruct((t_dim, D_MODEL), BF16),
                            jax.ShapeDtypeStruct((1, D_MODEL), F32)],
                 args=args + [h, dout, g], hook=hook)


def ffn_bwd(h, g, tag, saved, dout, dout_bf, io):
    nm = "f" + tag
    w1t, w3t, w2 = io.w("w1t_" + tag), io.w("w3t_" + tag), io.w("w2_" + tag)
    u, a, b, hm = saved
    io.put("w2_" + tag, mm(hm, dout_bf, ta=True, alpha=0.5, out_dtype=BF16, name=nm + "_dw2",
                           hook=io.hook(nm + "_dw2")))
    da, db = ffn_dgate(dout_bf, w2, a, b, nm + "_dgate", hook=io.hook(nm + "_dgate"))
    io.put("w1t_" + tag, mm(da, u, ta=True, out_dtype=BF16, name=nm + "_dw1"))
    io.put("w3t_" + tag, mm(db, u, ta=True, out_dtype=BF16, name=nm + "_dw3", hook=io.hook(nm + "_dw3")))
    return du_norm_bwd([(da, w1t, False), (db, w3t, False)], h, g, dout, nm + "_du", hook=io.hook(nm + "_du"))


def conv_input_grad(d_parts, w, nm):
    tm = 256
    t_dim = d_parts[0].shape[0]
    n_tiles = t_dim // tm

    def fn(d1, n1, d2, n2, d3, n3, ww):
        d = jnp.concatenate([d1, d2, d3], axis=1)
        nxt = jnp.concatenate([n1, n2, n3], axis=1)
        nxt = jnp.where(pl.program_id(0) < n_tiles - 1, nxt, 0.0)
        dd = jnp.concatenate([d, nxt], axis=0)
        out = dd[3:3 + tm] * ww[0:1]
        for k in range(1, SSM_CONV):
            out = out + dd[3 - k:3 - k + tm] * ww[k:k + 1]
        return (out,)

    rows = []
    for d in d_parts:
        below = pl.BlockSpec((8, d.shape[1]), lambda i: (jnp.minimum((i + 1) * (tm // 8), t_dim // 8 - 1), 0))
        rows += [d, (d, below)]
    dx, = rowmap(fn, rows, [w], [(SSM_CONV_DIM, BF16)], tm=tm, name=nm)
    return dx


GRP_W = SSM_D_INNER // SSM_GROUPS
HPG = SSM_HEADS // SSM_GROUPS
HEAD_SHIFT = 6


def _split2(x):
    hi = x.astype(BF16)
    return hi, (x - hi.astype(F32)).astype(BF16)


def _expand_mats():
    e = ((lax.broadcasted_iota(jnp.int32, (HPG, GRP_W), 1) >> HEAD_SHIFT)
         == lax.broadcasted_iota(jnp.int32, (HPG, GRP_W), 0)).astype(BF16)
    et = ((lax.broadcasted_iota(jnp.int32, (GRP_W, HPG), 0) >> HEAD_SHIFT)
          == lax.broadcasted_iota(jnp.int32, (GRP_W, HPG), 1)).astype(BF16)
    return e, et


def _expand(v, e_m):
    hi, lo = _split2(v)
    return jnp.dot(hi, e_m, preferred_element_type=F32) + jnp.dot(lo, e_m, preferred_element_type=F32)


def _reduce8(v, et_m):
    hi, lo = _split2(v)
    return jnp.dot(hi, et_m, preferred_element_type=F32) + jnp.dot(lo, et_m, preferred_element_type=F32)


def _ssd_group_terms(dt_ref, dtT_ref, arow_ref, acol_ref):
    L = SSM_CHUNK
    r = lax.broadcasted_iota(jnp.int32, (L, L), 0)
    c = lax.broadcasted_iota(jnp.int32, (L, L), 1)
    tril = (r >= c).astype(BF16)
    triu = (r <= c).astype(BF16)
    dtg = dt_ref[0]
    acol = None
    for p in _split3(dtg * arow_ref[0]):
        t = jnp.dot(tril, p, preferred_element_type=F32)
        acol = t if acol is None else acol + t
    arowT = None
    for p in _split3(dtT_ref[0] * acol_ref[0]):
        t = jnp.dot(p, triu, preferred_element_type=F32)
        arowT = t if arowT is None else arowT + t
    return dtg, acol, arowT, r >= c


def _state_decay(a_last_col, et_m):
    hi, lo = _split2(jnp.broadcast_to(jnp.exp(a_last_col), (HPG, SSM_STATE)))
    return jnp.dot(et_m, hi, preferred_element_type=F32) + jnp.dot(et_m, lo, preferred_element_type=F32)


def _conv_block(x_ref, halo_ref, w_ref, b_ref, first):
    L = SSM_CHUNK
    xx = jnp.concatenate([jnp.where(first, 0.0, halo_ref[...]), x_ref[...]], axis=0)
    w = w_ref[...]
    shifted = [pltpu.roll(xx, SSM_CONV - 1 - k, 0)[8:8 + L] if k < SSM_CONV - 1 else xx[8:8 + L]
               for k in range(SSM_CONV)]
    acc = b_ref[...] + shifted[0] * w[0:1]
    for k in range(1, SSM_CONV):
        acc = acc + shifted[k] * w[k:k + 1]
    return acc, shifted


def _ssd_specs(nc, rev):
    L, N = SSM_CHUNK, SSM_STATE
    xcols = SSM_D_INNER // LANES
    ch = (lambda c: nc - 1 - c) if rev else (lambda c: c)
    above = lambda c: jnp.maximum(ch(c) * (L // 8) - 1, 0)
    specs = []
    for width, col in ((GRP_W, lambda g: g), (N, lambda g: xcols + g), (N, lambda g: xcols + SSM_GROUPS + g)):
        specs += [
            pl.BlockSpec((L, width), lambda c, g, col=col: (ch(c), col(g))),
            pl.BlockSpec((8, width), lambda c, g, col=col: (above(c), col(g))),
            pl.BlockSpec((SSM_CONV, width), lambda c, g, col=col: (0, col(g))),
            pl.BlockSpec((1, width), lambda c, g, col=col: (0, col(g))),
        ]
    return specs + [
        pl.BlockSpec((1, L, HPG), lambda c, g: (g, ch(c), 0)),
        pl.BlockSpec((1, HPG, L), lambda c, g: (g, 0, ch(c))),
        pl.BlockSpec((1, 1, HPG), lambda c, g: (g, 0, 0)),
        pl.BlockSpec((1, HPG, 1), lambda c, g: (g, 0, 0)),
        pl.BlockSpec((1, GRP_W), lambda c, g: (0, g)),
    ]


def ssd_fwd(xbc_raw, conv_w, conv_b, dt_g, dtT_g, a_row, a_col, dvec, nm, hook=None):
    t_dim = xbc_raw.shape[0]
    L, P, N = SSM_CHUNK, SSM_HEAD_DIM, SSM_STATE
    nc = t_dim // L

    def body(x_ref, xh_ref, xw_ref, xb_ref, b_ref, bh_ref, bw_ref, bb_ref, c_ref, ch_ref, cw_ref, cb_ref,
             dt_ref, dtT_ref, arow_ref, acol_ref, dvec_ref, y_ref, st_ref, s_s):
        ci = pl.program_id(0)
        g = pl.program_id(1)

        @pl.when((ci == 0) & (g == 0))
        def _():
            s_s[...] = jnp.zeros_like(s_s)

        e_m, et_m = _expand_mats()
        dtg, acol, arowT, causal = _ssd_group_terms(dt_ref, dtT_ref, arow_ref, acol_ref)
        a_last_row = acol[L - 1:L, :]
        x = _silu(_conv_block(x_ref, xh_ref, xw_ref, xb_ref, ci == 0)[0])
        bm = _silu(_conv_block(b_ref, bh_ref, bw_ref, bb_ref, ci == 0)[0])
        cm = _silu(_conv_block(c_ref, ch_ref, cw_ref, cb_ref, ci == 0)[0])
        cb = _dot(cm, bm, NT)
        s = s_s[g]
        st_ref[0, 0] = s
        ea_x = _expand(jnp.exp(acol), e_m)
        dt_x = _expand(dtg, e_m)
        w_x = _expand(jnp.exp(a_last_row - acol) * dtg, e_m)
        yb = ea_x * _dot(cm, s, NT) + dvec_ref[...] * x
        xd = (x * dt_x).astype(BF16)
        for e in range(HPG):
            sl = slice(e * P, (e + 1) * P)
            lm = jnp.exp(jnp.where(causal, acol[:, e:e + 1] - arowT[e:e + 1, :], NEG))
            m = (cb * lm).astype(BF16)
            y_ref[:, sl] = yb[:, sl] + jnp.dot(m, xd[:, sl], preferred_element_type=F32)
        s_s[g] = _state_decay(arowT[:, L - 1:L], et_m) * s + _dot(x * w_x, bm, TN)

    out_specs = [
        pl.BlockSpec((L, GRP_W), lambda c, g: (c, g)),
        pl.BlockSpec((1, 1, GRP_W, N), lambda c, g: (c, g, 0, 0)),
    ]
    return pcall(
        body, name=nm, grid=(nc, SSM_GROUPS), in_specs=_ssd_specs(nc, False), out_specs=out_specs,
        out_shape=[jax.ShapeDtypeStruct((t_dim, SSM_D_INNER), F32),
                   jax.ShapeDtypeStruct((nc, SSM_GROUPS, GRP_W, N), F32)],
        scratch=[pltpu.VMEM((SSM_GROUPS, GRP_W, N), F32)],
        args=[xbc_raw, xbc_raw, conv_w, conv_b] * 3 + [dt_g, dtT_g, a_row, a_col, dvec], hook=hook)


def ssd_bwd(dy, xbc_raw, conv_w, conv_b, dt_g, dtT_g, a_row, a_col, dvec, states, nm, hook=None):
    t_dim = xbc_raw.shape[0]
    L, P, N = SSM_CHUNK, SSM_HEAD_DIM, SSM_STATE
    nc = t_dim // L

    def body(dy_ref, x_ref, xh_ref, xw_ref, xb_ref, b_ref, bh_ref, bw_ref, bb_ref, c_ref, ch_ref, cw_ref, cb_ref,
             dt_ref, dtT_ref, arow_ref, acol_ref, dvec_ref, st_ref,
             dx_ref, db_ref, dc_ref, da_ref, ddt_ref, dd_ref, dwx_ref, dwb_ref, dwc_ref, dbx_ref, dbb_ref, dbc_ref,
             ds_s, yd_s, dxd_s):
        ci = pl.program_id(0)
        g = pl.program_id(1)

        @pl.when((ci == 0) & (g == 0))
        def _():
            ds_s[...] = jnp.zeros_like(ds_s)
            for r in (dd_ref, dwx_ref, dwb_ref, dwc_ref, dbx_ref, dbb_ref, dbc_ref):
                r[...] = jnp.zeros_like(r)

        e_m, et_m = _expand_mats()
        dtg, acol, arowT, causal = _ssd_group_terms(dt_ref, dtT_ref, arow_ref, acol_ref)
        a_last_row = acol[L - 1:L, :]
        first = ci == nc - 1
        pre_x, sh_x = _conv_block(x_ref, xh_ref, xw_ref, xb_ref, first)
        pre_b, sh_b = _conv_block(b_ref, bh_ref, bw_ref, bb_ref, first)
        pre_c, sh_c = _conv_block(c_ref, ch_ref, cw_ref, cb_ref, first)
        sg_x, sg_b, sg_c = _sigmoid(pre_x), _sigmoid(pre_b), _sigmoid(pre_c)
        x = pre_x * sg_x
        dy = dy_ref[...]
        bm = pre_b * sg_b
        cm = pre_c * sg_c
        cb = _dot(cm, bm, NT)
        s = st_ref[0, 0]
        dsp = ds_s[g]
        ew8 = jnp.exp(a_last_row - acol)
        ea_x = _expand(jnp.exp(acol), e_m)
        dt_x = _expand(dtg, e_m)
        ew_x = _expand(ew8, e_m)
        w_x = ew_x * dt_x
        z = _dot(cm, s, NT)
        dz = ea_x * dy
        dc = _dot(dz, s)
        ds_y = _dot(dz, cm, TN)
        du = _dot(bm, dsp, NT)
        u = x * w_x
        db = _dot(u, dsp)
        xd = (x * dt_x).astype(BF16)
        dyb = dy.astype(BF16)
        dcb = jnp.zeros((L, L), F32)
        for e in range(HPG):
            sl = slice(e * P, (e + 1) * P)
            lm = jnp.exp(jnp.where(causal, acol[:, e:e + 1] - arowT[e:e + 1, :], NEG))
            m = (cb * lm).astype(BF16)
            yd_s[:, sl] = jnp.dot(m, xd[:, sl], preferred_element_type=F32)
            dxd_s[:, sl] = lax.dot_general(m, dyb[:, sl], TN, preferred_element_type=F32)
            dcb = dcb + lax.dot_general(dyb[:, sl], xd[:, sl], NT, preferred_element_type=F32) * lm
        dxd = dxd_s[...]

        def through_conv(d_act, pre, sg, shifted, d_ref, dw_ref, dbias_ref):
            d_pre = d_act * (sg * (1.0 + pre * (1.0 - sg)))
            d_ref[...] = d_pre
            dw_ref[g] += jnp.concatenate([jnp.sum(d_pre * sh, axis=0, keepdims=True) for sh in shifted], axis=0)
            dbias_ref[g] += jnp.sum(d_pre, axis=0, keepdims=True)

        through_conv(dvec_ref[...] * dy + du * w_x + dt_x * dxd, pre_x, sg_x, sh_x, dx_ref, dwx_ref, dbx_ref)
        ddt = _reduce8(x * (ew_x * du + dxd), et_m)
        da = (_reduce8(dz * z + dyb.astype(F32) * yd_s[...], et_m)
              - _reduce8(xd.astype(F32) * dxd + du * u, et_m))
        dwa_row = _reduce8(jnp.broadcast_to(jnp.sum(du * u, axis=0, keepdims=True), (8, GRP_W)), et_m)[0:1]
        t_nh = None
        for p in _split3(dsp * s):
            t = lax.dot_general(p, et_m, TN, preferred_element_type=F32)
            t_nh = t if t_nh is None else t_nh + t
        d_last = dwa_row + jnp.exp(a_last_row) * jnp.sum(t_nh, axis=0, keepdims=True)
        row_l = lax.broadcasted_iota(jnp.int32, (L, 1), 0)
        da_ref[0] = da + jnp.where(row_l == L - 1, d_last, 0.0)
        ddt_ref[0] = ddt
        dd_ref[g] += jnp.sum(dy * x, axis=0, keepdims=True)
        through_conv(dc + _dot(dcb, bm), pre_c, sg_c, sh_c, dc_ref, dwc_ref, dbc_ref)
        through_conv(db + _dot(dcb, cm, TN), pre_b, sg_b, sh_b, db_ref, dwb_ref, dbb_ref)
        ds_s[g] = _state_decay(arowT[:, L - 1:L], et_m) * dsp + ds_y

    rc = lambda c: nc - 1 - c
    in_specs = ([pl.BlockSpec((L, GRP_W), lambda c, g: (rc(c), g))] + _ssd_specs(nc, True)
                + [pl.BlockSpec((1, 1, GRP_W, N), lambda c, g: (rc(c), g, 0, 0))])
    whole = lambda *shape: pl.BlockSpec(shape, lambda c, g: (0,) * len(shape))
    out_specs = [
        pl.BlockSpec((L, GRP_W), lambda c, g: (rc(c), g)),
        pl.BlockSpec((L, N), lambda c, g: (rc(c), g)),
        pl.BlockSpec((L, N), lambda c, g: (rc(c), g)),
        pl.BlockSpec((1, L, HPG), lambda c, g: (g, rc(c), 0)),
        pl.BlockSpec((1, L, HPG), lambda c, g: (g, rc(c), 0)),
        whole(SSM_GROUPS, 1, GRP_W),
        whole(SSM_GROUPS, SSM_CONV, GRP_W), whole(SSM_GROUPS, SSM_CONV, N), whole(SSM_GROUPS, SSM_CONV, N),
        whole(SSM_GROUPS, 1, GRP_W), whole(SSM_GROUPS, 1, N), whole(SSM_GROUPS, 1, N),
    ]
    gn = SSM_GROUPS * N
    acc = lambda *shape: jax.ShapeDtypeStruct(shape, F32)
    out_shape = [
        acc(t_dim, SSM_D_INNER), acc(t_dim, gn), acc(t_dim, gn), acc(SSM_GROUPS, t_dim, HPG),
        acc(SSM_GROUPS, t_dim, HPG), acc(SSM_GROUPS, 1, GRP_W),
        acc(SSM_GROUPS, SSM_CONV, GRP_W), acc(SSM_GROUPS, SSM_CONV, N), acc(SSM_GROUPS, SSM_CONV, N),
        acc(SSM_GROUPS, 1, GRP_W), acc(SSM_GROUPS, 1, N), acc(SSM_GROUPS, 1, N),
    ]
    return pcall(
        body, name=nm, grid=(nc, SSM_GROUPS), in_specs=in_specs, out_specs=out_specs, out_shape=out_shape,
        scratch=[pltpu.VMEM((SSM_GROUPS, GRP_W, N), F32), pltpu.VMEM((L, GRP_W), F32), pltpu.VMEM((L, GRP_W), F32)],
        args=[dy] + [xbc_raw, xbc_raw, conv_w, conv_b] * 3 + [dt_g, dtT_g, a_row, a_col, dvec, states], hook=hook)


def _softplus(x):
    return jnp.maximum(x, 0.0) + jnp.log(1.0 + jnp.exp(-jnp.abs(x)))


def ssd_dt_bwd(da, ddt, dt, dt_raw, a_row, dt_bias, nm):
    L = SSM_CHUNK

    def fn(d_a, d_dt, dtv, raw, ar, bias):
        r = lax.broadcasted_iota(jnp.int32, (L, L), 0)
        c = lax.broadcasted_iota(jnp.int32, (L, L), 1)
        triu = (r <= c).astype(BF16)
        acc = None
        for p in _split3(d_a):
            t = jnp.dot(triu, p, preferred_element_type=F32)
            acc = t if acc is None else acc + t
        d_dt = d_dt + acc * ar
        d_a_h = jnp.sum(acc * dtv, axis=0, keepdims=True)
        d_raw = d_dt * _sigmoid(raw + bias)
        return d_raw, d_a_h, jnp.sum(d_raw, axis=0, keepdims=True)

    return rowmap(fn, [da, ddt, dt, dt_raw], [a_row, dt_bias], [(SSM_HEADS, BF16)],
                  [(1, SSM_HEADS), (1, SSM_HEADS)], tm=L, name=nm)


GN_W = SSM_D_INNER // SSM_GROUPS


def mamba_fwd(h, p, nm, io):
    def in_proj(x, gg, w_zt, w_xbct, w_dtt):
        uu = _rms_fwd(x, gg).astype(BF16)
        return uu, _dot(uu, w_zt, NT), _dot(uu, w_xbct, NT), _dot(uu, w_dtt, NT)

    u, z, xbc_raw, dt_raw = rowmap(in_proj, [h], [p["ssm_norm"], p["w_zt"], p["w_xbct"], p["w_dtt"]],
                                   [(D_MODEL, BF16), (SSM_D_INNER, F32), (SSM_CONV_DIM, F32), (SSM_HEADS, F32)],
                                   tm=RESIDENT_TM, name=nm + "_in", hook=io.hook(nm + "_in"))
    dt, = rowmap(lambda r, b: (_softplus(r + b),), [dt_raw], [p["dt_bias"]], [(SSM_HEADS, F32)], tm=256,
                 name=nm + "_softplus")
    dt_g = dt.reshape(-1, SSM_GROUPS, HPG).transpose(1, 0, 2)
    dtT_g = dt_g.transpose(0, 2, 1)
    y, states = ssd_fwd(xbc_raw, p["conv_w"], p["conv_b"], dt_g, dtT_g, p["a_row"], p["a_col"], p["dvec"],
                        nm + "_ssd", hook=io.hook(nm + "_ssd"))

    def gate_norm_out(yv, zv, hv, gg, w_out):
        t = yv * _silu(zv)
        yn = jnp.concatenate([_rms_fwd(t[:, k * GN_W:(k + 1) * GN_W], gg[:, k * GN_W:(k + 1) * GN_W])
                              for k in range(SSM_GROUPS)], axis=1).astype(BF16)
        return yn, hv + _dot(yn, w_out)

    yn, out = rowmap(gate_norm_out, [y, z, h], [p["gate_norm"], p["w_out"]],
                     [(SSM_D_INNER, BF16), (D_MODEL, F32)], tm=RESIDENT_TM, name=nm + "_out")
    return out, (u, z, xbc_raw, dt_raw, dt, dt_g, dtT_g, y, states, yn)


def mamba_bwd(h, p, saved, dout, dout_bf, nm, io):
    u, z, xbc_raw, dt_raw, dt, dt_g, dtT_g, y, states, yn = saved
    g = {}
    io.put("w_out", mm(yn, dout_bf, ta=True, out_dtype=BF16, name=nm + "_dwout"))

    def gate_norm_bwd(d_o, yv, zv, gg, w_out):
        d = _dot(d_o, w_out, NT)
        sz = _silu(zv)
        t = yv * sz
        dts, dgs = [], []
        for k in range(SSM_GROUPS):
            sl = slice(k * GN_W, (k + 1) * GN_W)
            dt_k, dg_k = _rms_bwd(t[:, sl], gg[:, sl], d[:, sl])
            dts.append(dt_k)
            dgs.append(dg_k)
        d_t = jnp.concatenate(dts, axis=1)
        return d_t * sz, d_t * yv * _silu_grad(zv), jnp.concatenate(dgs, axis=1)

    dy, dz, g["gate_norm"] = rowmap(gate_norm_bwd, [dout_bf, y, z], [p["gate_norm"], p["w_out"]],
                                    [(SSM_D_INNER, F32), (SSM_D_INNER, BF16)], [(1, SSM_D_INNER)], tm=256,
                                    name=nm + "_dgatenorm")
    d_x, d_b, d_c, da_g, ddt_g, dd, dwx, dwb, dwc, dbx, dbb, dbc = ssd_bwd(
        dy, xbc_raw, p["conv_w"], p["conv_b"], dt_g, dtT_g, p["a_row"], p["a_col"], p["dvec"], states, nm + "_dssd",
        hook=io.hook(nm + "_dssd"))
    g["dvec"] = dd
    by_lane = lambda t: t.transpose(1, 0, 2).reshape(t.shape[1], -1)
    g["conv_w"] = jnp.concatenate([by_lane(dwx), by_lane(dwb), by_lane(dwc)], axis=1)
    g["conv_b"] = jnp.concatenate([by_lane(dbx), by_lane(dbb), by_lane(dbc)], axis=1)
    per_head = lambda t: t.transpose(1, 0, 2).reshape(-1, SSM_HEADS)
    ddt_raw, g["a"], g["dt_bias"] = ssd_dt_bwd(per_head(da_g), per_head(ddt_g), dt, dt_raw, p["a_heads"],
                                               p["dt_bias"], nm + "_ddt")
    dxbc_raw = conv_input_grad([d_x, d_b, d_c], p["conv_w"], nm + "_dconv")
    io.put("w_int", jnp.concatenate([mm(dz, u, ta=True, out_dtype=BF16, name=nm + "_dwz"),
                                     mm(dxbc_raw, u, ta=True, out_dtype=BF16, name=nm + "_dwxbc"),
                                     mm(ddt_raw, u, ta=True, out_dtype=BF16, name=nm + "_dwdt")], axis=0))
    dh, dh_bf, g["ssm_norm"] = du_norm_bwd(
        [(dz, p["w_zt"], False), (dxbc_raw, p["w_xbct"], False), (ddt_raw, p["w_dtt"], False)],
        h, p["ssm_norm"], dout, nm + "_du", hook=io.hook(nm + "_du"))
    return dh, dh_bf, g


KV_W = ATT_KV_HEADS * ATT_HEAD_DIM


def kv_fwd(h, p, nm):
    def kv_proj(x, gg, w_kv, gk):
        uu = _rms_fwd(x, gg).astype(BF16)
        t = _dot(uu, w_kv)
        ks = [_rms_fwd(t[:, j * ATT_HEAD_DIM:(j + 1) * ATT_HEAD_DIM], gk) for j in range(ATT_KV_HEADS)]
        return uu, t, jnp.concatenate(ks, axis=1), t[:, KV_W:]

    u, kv_raw, k, v = rowmap(kv_proj, [h], [p["kv_norm"], p["w_kv"], p["k_norm"]],
                             [(D_MODEL, BF16), (2 * KV_W, F32), (KV_W, F32), (KV_W, F32)], tm=RESIDENT_TM,
                             name=nm + "_proj")
    return k, v, (u, kv_raw)


def kv_bwd(h, p, saved, dk_cur, dk_prev, dv_cur, dv_prev, dout, nm, io):
    u, kv_raw = saved
    t_dim = h.shape[0]
    tm = ATT_WINDOW
    nb = t_dim // tm
    nxt = pl.BlockSpec((tm, KV_W), lambda i: (jnp.minimum(i + 1, nb - 1), 0))

    def fn(dkc, dkp, dvc, dvp, t, gg):
        live = pl.program_id(0) < nb - 1
        dk = dkc + jnp.where(live, dkp, 0.0)
        dv = dvc + jnp.where(live, dvp, 0.0)
        outs, dgs = [], None
        for j in range(ATT_KV_HEADS):
            sl = slice(j * ATT_HEAD_DIM, (j + 1) * ATT_HEAD_DIM)
            dx, dg = _rms_bwd(t[:, sl], gg, dk[:, sl])
            outs.append(dx)
            dgs = dg if dgs is None else dgs + dg
        return jnp.concatenate(outs + [dv], axis=1), dgs

    dkv_raw, dknorm = rowmap(fn, [dk_cur, (dk_prev, nxt), dv_cur, (dv_prev, nxt), kv_raw], [p["k_norm"]],
                             [(2 * KV_W, BF16)], [(1, ATT_HEAD_DIM)], tm=tm, name=nm + "_dknorm",
                             hook=io.hook(nm + "_dknorm"))
    g = {"k_norm": dknorm}
    io.put("w_kv", mm(u, dkv_raw, ta=True, out_dtype=BF16, name=nm + "_dwkv"))
    dh, dh_bf, g["kv_norm"] = du_norm_bwd([(dkv_raw, p["w_kv"], True)], h, p["kv_norm"], dout, nm + "_du",
                                          hook=io.hook(nm + "_du"))
    return dh, dh_bf, g


def _attn_specs(nb):
    blk = ATT_WINDOW
    cur = lambda i: (i, 0)
    prev = lambda i: (jnp.maximum(i - 1, 0), 0)
    return [
        pl.BlockSpec((blk, D_MODEL), cur),
        pl.BlockSpec((blk, KV_W), prev), pl.BlockSpec((blk, KV_W), cur),
        pl.BlockSpec((blk, KV_W), prev), pl.BlockSpec((blk, KV_W), cur),
        pl.BlockSpec((1, ATT_HEAD_DIM), lambda i: (0, 0)),
        pl.BlockSpec((ATT_KV_HEADS, ATT_GROUP * blk, 2 * blk), lambda i: (0, 0, 0)),
        pl.BlockSpec((ATT_KV_HEADS, ATT_GROUP * blk, 1), lambda i: (0, 0, 0)),
    ]


def attn_fwd(q_raw, k, v, q_norm, bias, sink_col, nm):
    t_dim = q_raw.shape[0]
    blk, hd = ATT_WINDOW, ATT_HEAD_DIM
    nb = t_dim // blk

    n_pairs = ATT_GROUP // 2

    def body(q_ref, kp_ref, kc_ref, vp_ref, vc_ref, qn_ref, bias_ref, sink_ref, o_ref):
        low = lax.broadcasted_iota(jnp.int32, (1, LANES), 1) < hd
        gq = jnp.concatenate([qn_ref[...], qn_ref[...]], axis=1)
        colk = lax.broadcasted_iota(jnp.int32, (1, 2 * blk), 1)
        live = (pl.program_id(0) > 0) | (colk >= blk)
        for kv in range(ATT_KV_HEADS):
            kraw = jnp.concatenate([kp_ref[...], kc_ref[...]], axis=0)
            vraw = jnp.concatenate([vp_ref[...], vc_ref[...]], axis=0)
            k_mine = jnp.where(low, kraw, 0.0) if kv == 0 else jnp.where(low, 0.0, kraw)
            v_mine = jnp.where(low, vraw, 0.0) if kv == 0 else jnp.where(low, 0.0, vraw)
            k_other = pltpu.roll(k_mine, hd, 1)
            v_other = pltpu.roll(v_mine, hd, 1)
            k_lo, k_hi = (k_mine, k_other) if kv == 0 else (k_other, k_mine)
            v_lo, v_hi = (v_mine, v_other) if kv == 0 else (v_other, v_mine)
            x = jnp.concatenate([q_ref[:, (kv * n_pairs + p) * LANES:(kv * n_pairs + p + 1) * LANES]
                                 for p in range(n_pairs)], axis=0)
            sq = x * x
            ms_lo = jnp.sum(jnp.where(low, sq, 0.0), axis=-1, keepdims=True) * (1.0 / hd)
            ms_hi = jnp.sum(jnp.where(low, 0.0, sq), axis=-1, keepdims=True) * (1.0 / hd)
            q = x * jnp.where(low, lax.rsqrt(ms_lo + EPS), lax.rsqrt(ms_hi + EPS)) * gq
            o_pair = None
            for par, (k_p, v_p) in enumerate(((k_lo, v_lo), (k_hi, v_hi))):
                rows = [slice((2 * p + par) * blk, (2 * p + par + 1) * blk) for p in range(n_pairs)]
                bias = jnp.concatenate([bias_ref[kv, r, :] for r in rows], axis=0)
                sink = jnp.concatenate([sink_ref[kv, r, :] for r in rows], axis=0)
                s = jnp.where(live, _dot(q, k_p, NT) * (hd ** -0.5) + bias, NEG)
                m = jnp.maximum(jnp.max(s, axis=-1, keepdims=True), sink)
                pexp = jnp.exp(s - m)
                inv_den = 1.0 / (jnp.sum(pexp, axis=-1, keepdims=True) + jnp.exp(sink - m))
                o_p = _dot(pexp, v_p) * inv_den
                o_pair = o_p if o_pair is None else o_pair + o_p
            for p in range(n_pairs):
                o_ref[:, (kv * n_pairs + p) * LANES:(kv * n_pairs + p + 1) * LANES] = (
                    o_pair[p * blk:(p + 1) * blk].astype(o_ref.dtype))

    out, = pcall(body, name=nm, grid=(nb,), in_specs=_attn_specs(nb),
                 out_specs=[pl.BlockSpec((blk, D_MODEL), lambda i: (i, 0))],
                 out_shape=[jax.ShapeDtypeStruct((t_dim, D_MODEL), BF16)],
                 args=[q_raw, k, k, v, v, q_norm, bias, sink_col])
    return out


def attn_bwd(do, q_raw, k, v, q_norm, bias, sink_col, nm, hook=None):
    t_dim = q_raw.shape[0]
    blk, hd = ATT_WINDOW, ATT_HEAD_DIM
    nb = t_dim // blk
    scale = hd ** -0.5

    def body(do_ref, q_ref, kp_ref, kc_ref, vp_ref, vc_ref, qn_ref, bias_ref, sink_ref,
             dq_ref, dkc_ref, dkp_ref, dvc_ref, dvp_ref, dbias_ref, dsink_ref, dqn_ref):
        @pl.when(pl.program_id(0) == 0)
        def _():
            dbias_ref[...] = jnp.zeros_like(dbias_ref)
            dsink_ref[...] = jnp.zeros_like(dsink_ref)
            dqn_ref[...] = jnp.zeros_like(dqn_ref)

        n_pairs = ATT_GROUP // 2
        low = lax.broadcasted_iota(jnp.int32, (1, LANES), 1) < hd
        gq = jnp.concatenate([qn_ref[...], qn_ref[...]], axis=1)
        colk = lax.broadcasted_iota(jnp.int32, (1, 2 * blk), 1)
        live = (pl.program_id(0) > 0) | (colk >= blk)
        ones = jnp.ones((2 * blk, LANES), BF16)
        for kv in range(ATT_KV_HEADS):
            kraw = jnp.concatenate([kp_ref[...], kc_ref[...]], axis=0)
            vraw = jnp.concatenate([vp_ref[...], vc_ref[...]], axis=0)
            k_mine = jnp.where(low, kraw, 0.0) if kv == 0 else jnp.where(low, 0.0, kraw)
            v_mine = jnp.where(low, vraw, 0.0) if kv == 0 else jnp.where(low, 0.0, vraw)
            k_other = pltpu.roll(k_mine, hd, 1)
            v_other = pltpu.roll(v_mine, hd, 1)
            k_lo, k_hi = (k_mine, k_other) if kv == 0 else (k_other, k_mine)
            v_lo, v_hi = (v_mine, v_other) if kv == 0 else (v_other, v_mine)
            tiles = [slice((kv * n_pairs + p) * LANES, (kv * n_pairs + p + 1) * LANES) for p in range(n_pairs)]
            x = jnp.concatenate([q_ref[:, t] for t in tiles], axis=0)
            do_pair = jnp.concatenate([do_ref[:, t] for t in tiles], axis=0)

            def head_mean(t):
                lo = jnp.sum(jnp.where(low, t, 0.0), axis=-1, keepdims=True)
                hi = jnp.sum(jnp.where(low, 0.0, t), axis=-1, keepdims=True)
                return jnp.where(low, lo, hi) * (1.0 / hd)

            rinv = lax.rsqrt(head_mean(x * x) + EPS)
            xh = x * rinv
            q = xh * gq
            dq_pair = None
            dk_pair = None
            dv_pair = None
            for par, (k_p, v_p) in enumerate(((k_lo, v_lo), (k_hi, v_hi))):
                rows = [slice((2 * p + par) * blk, (2 * p + par + 1) * blk) for p in range(n_pairs)]
                bias = jnp.concatenate([bias_ref[kv, r, :] for r in rows], axis=0)
                sink = jnp.concatenate([sink_ref[kv, r, :] for r in rows], axis=0)
                s = jnp.where(live, _dot(q, k_p, NT) * scale + bias, NEG)
                m = jnp.maximum(jnp.max(s, axis=-1, keepdims=True), sink)
                pexp = jnp.exp(s - m)
                e_sink = jnp.exp(sink - m)
                inv_den = 1.0 / (jnp.dot(pexp.astype(BF16), ones, preferred_element_type=F32) + e_sink)
                prob = pexp * jnp.concatenate([inv_den, inv_den], axis=1)
                dp = _dot(do_pair, v_p, NT)
                delta = jnp.sum(prob * dp, axis=-1, keepdims=True)
                ds = prob * (dp - delta)
                dsk = -(e_sink * inv_den[:, :1]) * delta
                for p, r in enumerate(rows):
                    dsink_ref[kv, r, :] += dsk[p * blk:(p + 1) * blk]
                    dbias_ref[kv, r, :] += ds[p * blk:(p + 1) * blk]
                ds_s = ds * scale
                mine = low if par == 0 else jnp.logical_not(low)
                dq_p = _dot(ds_s, k_p)
                dk_p = jnp.where(mine, _dot(ds_s, q, TN), 0.0)
                dv_p = jnp.where(mine, _dot(prob, do_pair, TN), 0.0)
                dq_pair = dq_p if dq_pair is None else dq_pair + dq_p
                dk_pair = dk_p if dk_pair is None else dk_pair + dk_p
                dv_pair = dv_p if dv_pair is None else dv_pair + dv_p
            dqn_ref[...] += jnp.sum(dq_pair * xh, axis=0, keepdims=True)
            dxh = dq_pair * gq
            dq_raw = rinv * (dxh - xh * head_mean(dxh * xh))
            for p, t in enumerate(tiles):
                dq_ref[:, t] = dq_raw[p * blk:(p + 1) * blk].astype(dq_ref.dtype)
            dkk = dk_pair + pltpu.roll(dk_pair, hd, 1)
            dvv = dv_pair + pltpu.roll(dv_pair, hd, 1)
            sl = slice(kv * hd, (kv + 1) * hd)
            dkp_ref[:, sl] = dkk[:blk, sl]
            dkc_ref[:, sl] = dkk[blk:, sl]
            dvp_ref[:, sl] = dvv[:blk, sl]
            dvc_ref[:, sl] = dvv[blk:, sl]

    cur = lambda i: (i, 0)
    row_spec = pl.BlockSpec((blk, KV_W), cur)
    out_specs = [
        pl.BlockSpec((blk, D_MODEL), cur), row_spec, row_spec, row_spec, row_spec,
        pl.BlockSpec((ATT_KV_HEADS, ATT_GROUP * blk, 2 * blk), lambda i: (0, 0, 0)),
        pl.BlockSpec((ATT_KV_HEADS, ATT_GROUP * blk, 1), lambda i: (0, 0, 0)),
        pl.BlockSpec((1, LANES), lambda i: (0, 0)),
    ]
    kvs = jax.ShapeDtypeStruct((t_dim, KV_W), F32)
    out_shape = [
        jax.ShapeDtypeStruct((t_dim, D_MODEL), BF16), kvs, kvs, kvs, kvs,
        jax.ShapeDtypeStruct((ATT_KV_HEADS, ATT_GROUP * blk, 2 * blk), F32),
        jax.ShapeDtypeStruct((ATT_KV_HEADS, ATT_GROUP * blk, 1), F32),
        jax.ShapeDtypeStruct((1, LANES), F32),
    ]
    *outs, dqn_pair = pcall(body, name=nm, grid=(nb,), in_specs=[pl.BlockSpec((blk, D_MODEL), cur)] + _attn_specs(nb),
                            out_specs=out_specs, out_shape=out_shape,
                            args=[do, q_raw, k, k, v, v, q_norm, bias, sink_col], hook=hook)
    return (*outs, dqn_pair[:, :hd] + dqn_pair[:, hd:])


def _t5_bucket_np():
    blk = ATT_WINDOW
    qi = np.arange(blk)[:, None] + blk
    kj = np.arange(2 * blk)[None, :]
    dist = qi - kj
    n = np.maximum(dist, 0)
    max_exact = REL_BUCKETS // 2
    nf = np.maximum(n, 1).astype(np.float32)
    large = max_exact + (np.log(nf / max_exact) / math.log(ATT_WINDOW / max_exact)
                         * (REL_BUCKETS - max_exact)).astype(np.int32)
    large = np.minimum(large, REL_BUCKETS - 1)
    bucket = np.where(n < max_exact, n, large)
    in_window = (dist >= 0) & (dist < ATT_WINDOW)
    return bucket, in_window


def attn_block_fwd(h, k, v, p, nm):
    def q_proj(x, gg, w_q):
        uu = _rms_fwd(x, gg).astype(BF16)
        return uu, _dot(uu, w_q)

    u, q_raw = rowmap(q_proj, [h], [p["attn_norm"], p["w_q"]], [(D_MODEL, BF16), (D_MODEL, F32)], tm=RESIDENT_TM,
                      name=nm + "_q")
    o = attn_fwd(q_raw, k, v, p["q_norm"], p["bias"], p["sink_col"], nm + "_core")
    out = mm(o, p["w_o"], res=h, name=nm + "_o")
    return out, (u, q_raw, o)


def attn_block_bwd(h, k, v, p, saved, dout, dout_bf, nm, io):
    u, q_raw, o = saved
    g = {}
    io.put("w_o", mm(o, dout_bf, ta=True, out_dtype=BF16, name=nm + "_dwo", hook=io.hook(nm + "_dwo")))
    do = mm(dout_bf, p["w_o"], tb=True, name=nm + "_do")
    dq_raw, dkc, dkp, dvc, dvp, g["bias"], g["sink_col"], g["q_norm"] = attn_bwd(
        do, q_raw, k, v, p["q_norm"], p["bias"], p["sink_col"], nm + "_dcore", hook=io.hook(nm + "_dcore"))
    io.put("w_q", mm(u, dq_raw, ta=True, out_dtype=BF16, name=nm + "_dwq"))
    dh, dh_bf, g["attn_norm"] = du_norm_bwd([(dq_raw, p["w_q"], True)], h, p["attn_norm"], dout, nm + "_du")
    return dh, dh_bf, g, (dkc, dkp, dvc, dvp)


FFN_TAGS = ["00", "01", "10", "11"]


def local_step(x, target, small, io):
    bucket, in_window = _t5_bucket_np()
    blk = ATT_WINDOW
    w = small

    fnorm = {tag: w["ffn_norm"][int(tag[0]), int(tag[1])][None, :] for tag in FFN_TAGS}
    a_neg = -jnp.exp(w["ssm_a_log"][0])

    def mamba_p():
        w_int = io.w("w_int")
        return dict(ssm_norm=w["ssm_norm"], w_zt=w_int[:SSM_D_INNER],
                    w_xbct=w_int[SSM_D_INNER:SSM_D_INNER + SSM_CONV_DIM], w_dtt=w_int[SSM_D_INNER + SSM_CONV_DIM:],
                    conv_w=w["ssm_conv_w"][0], conv_b=w["ssm_conv_b"], dt_bias=w["ssm_dt_bias"],
                    a_heads=a_neg[None, :], a_row=a_neg.reshape(SSM_GROUPS, 1, HPG),
                    a_col=a_neg.reshape(SSM_GROUPS, HPG, 1),
                    dvec=jnp.repeat(w["ssm_d"][0], SSM_HEAD_DIM)[None, :],
                    gate_norm=w["ssm_gate_norm"], w_out=io.w("w_out"))

    rb = w["rel_bias"]
    onehot3 = (np.arange(REL_BUCKETS)[:, None, None] == bucket[None]).astype(np.float32)
    bias = jnp.einsum("bh,bqk->hqk", rb, onehot3, precision=lax.Precision.HIGHEST)
    bias = jnp.where(in_window[None], bias, NEG)
    bias = bias.reshape(ATT_KV_HEADS, ATT_GROUP * blk, 2 * blk)
    sink_col = jnp.repeat(w["sinks"][0], blk).reshape(ATT_KV_HEADS, ATT_GROUP * blk, 1)

    def attn_p():
        return dict(attn_norm=w["attn_norm"], w_q=io.w("w_q"), q_norm=w["q_norm"], bias=bias, sink_col=sink_col,
                    w_o=io.w("w_o"))

    def kv_p():
        return dict(kv_norm=w["kv_norm"][None, :], w_kv=io.w("w_kv"), k_norm=w["k_norm"][None, :])

    h0 = x
    h0a, s_f00 = ffn_fwd(h0, fnorm["00"], "00", io)
    mp = mamba_p()
    h0b, s_m = mamba_fwd(h0a, mp, "ssm", io)
    h1, s_f01 = ffn_fwd(h0b, fnorm["01"], "01", io)
    kp = kv_p()
    k, v, s_kv = kv_fwd(h1, kp, "kv")
    h1a, s_f10 = ffn_fwd(h1, fnorm["10"], "10", io)
    ap = attn_p()
    h1b, s_a = attn_block_fwd(h1a, k, v, ap, "att")
    (dh, dh_bf, sq), s_f11 = ffn_fwd(h1b, fnorm["11"], "11", io, target=target)
    loss_part = jnp.sum(sq) * (0.5 / D_MODEL)

    fg = {}

    def ffn_back(tag, h_in, saved, dh, dh_bf):
        dh, dh_bf, dg = ffn_bwd(h_in, fnorm[tag], tag, saved, dh, dh_bf, io)
        fg[tag] = dg[0]
        return dh, dh_bf

    dh, dh_bf = ffn_back("11", h1b, s_f11, dh, dh_bf)
    dh, dh_bf, ga, dkv = attn_block_bwd(h1a, k, v, ap, s_a, dh, dh_bf, "att", io)
    dh, dh_bf = ffn_back("10", h1, s_f10, dh, dh_bf)
    dh, dh_bf, gk = kv_bwd(h1, kp, s_kv, *dkv, dh, "kv", io)
    dh, dh_bf = ffn_back("01", h0b, s_f01, dh, dh_bf)
    dh, dh_bf, gm = mamba_bwd(h0a, mp, s_m, dh, dh_bf, "ssm", io)
    dh, dh_bf = ffn_back("00", h0, s_f00, dh, dh_bf)
    grad_x = dh

    grads = {}
    grads["ffn_norm"] = jnp.stack([fg[tag] for tag in FFN_TAGS]).reshape(2, 2, D_MODEL)
    grads["ssm_norm"] = gm["ssm_norm"]
    grads["ssm_conv_w"] = gm["conv_w"][None]
    grads["ssm_conv_b"] = gm["conv_b"]
    grads["ssm_dt_bias"] = gm["dt_bias"]
    grads["ssm_a_log"] = gm["a"] * a_neg[None, :]
    grads["ssm_d"] = jnp.sum(gm["dvec"].reshape(SSM_HEADS, SSM_HEAD_DIM), axis=1)[None, :]
    grads["ssm_gate_norm"] = gm["gate_norm"]
    grads["kv_norm"] = gk["kv_norm"][0]
    grads["k_norm"] = gk["k_norm"][0]
    grads["attn_norm"] = ga["attn_norm"]
    grads["q_norm"] = ga["q_norm"]
    grads["sinks"] = jnp.sum(ga["sink_col"].reshape(ATT_HEADS, blk), axis=1)[None, :]
    onehot = (np.arange(REL_BUCKETS)[:, None] == bucket.reshape(1, -1)).astype(np.float32)
    dbias2d = ga["bias"].reshape(ATT_HEADS, blk * 2 * blk)
    grads["rel_bias"] = mm(jnp.asarray(onehot, BF16), dbias2d, tb=True, name="drelbias")
    return loss_part, grad_x, grads


def _adamw(g, w, m, v):
    m = ADAM_B1 * m + (1.0 - ADAM_B1) * g
    v = ADAM_B2 * v + (1.0 - ADAM_B2) * (g * g)
    m_hat = m / (1.0 - ADAM_B1 ** ADAM_STEP)
    v_hat = v / (1.0 - ADAM_B2 ** ADAM_STEP)
    delta = -ADAM_LR * (m_hat / (jnp.sqrt(v_hat) + ADAM_EPS) + ADAM_WD * w)
    return delta, m, v


def _slot_sum(r):
    g = r[0].astype(F32)
    for d in range(1, r.shape[0]):
        g = g + r[d].astype(F32)
    return g


def adamw_rows(recvs, w, m, v, name):
    n_l, rows, width = w.shape
    n_slots = recvs[0].shape[0]
    tr = 32
    assert rows % tr == 0, rows
    nt = rows // tr

    def body(*refs):
        r_refs = refs[:n_l]
        w_ref, m_ref, v_ref, g_o, d_o, m_o, v_o = refs[n_l:]
        li = pl.program_id(0)
        for k in range(n_l):
            @pl.when(li == k)
            def _(k=k):
                g = _slot_sum(r_refs[k])
                delta, m2, v2 = _adamw(g, w_ref[0], m_ref[0], v_ref[0])
                g_o[0] = g
                d_o[0] = delta
                m_o[0] = m2
                v_o[0] = v2

    def r_spec(k):
        return pl.BlockSpec((n_slots, tr, width),
                            lambda li, j: (0, jnp.where(li == k, j, jnp.where(li > k, nt - 1, 0)), 0))

    w_spec = pl.BlockSpec((1, tr, width), lambda li, j: (li, j, 0))
    shp = jax.ShapeDtypeStruct(w.shape, F32)
    return pcall(body, name=name, grid=(n_l, nt), in_specs=[r_spec(k) for k in range(n_l)] + [w_spec] * 3,
                 out_specs=[w_spec] * 4, out_shape=[shp] * 4, args=list(recvs) + [w, m, v])


def adamw_cols(recvs, w, m, v, name):
    n_l, rows, n = w.shape
    n_slots = recvs[0].shape[0]
    tr = 256
    nt = rows // tr

    def body(*refs):
        r_refs = refs[:n_l]
        w_ref, m_ref, v_ref, g_o, d_o, m_o, v_o = refs[n_l:]
        li = pl.program_id(0)
        for k in range(n_l):
            @pl.when(li == k)
            def _(k=k):
                g = _slot_sum(r_refs[k]).T
                delta, m2, v2 = _adamw(g, w_ref[0], m_ref[0], v_ref[0])
                g_o[0] = g
                d_o[0] = delta
                m_o[0] = m2
                v_o[0] = v2

    def r_spec(k):
        return pl.BlockSpec((n_slots, n, tr),
                            lambda li, j: (0, 0, jnp.where(li == k, j, jnp.where(li > k, nt - 1, 0))))

    w_spec = pl.BlockSpec((1, tr, n), lambda li, j: (li, j, 0))
    shp = jax.ShapeDtypeStruct(w.shape, F32)
    return pcall(body, name=name, grid=(n_l, nt), in_specs=[r_spec(k) for k in range(n_l)] + [w_spec] * 3,
                 out_specs=[w_spec] * 4, out_shape=[shp] * 4, args=list(recvs) + [w, m, v])


WEIGHT_NAMES = ["ffn_norm", "ffn_w1", "ffn_w3", "ffn_w2", "ssm_norm", "ssm_w_in", "ssm_conv_w", "ssm_conv_b",
                "ssm_dt_bias", "ssm_a_log", "ssm_d", "ssm_gate_norm", "ssm_w_out", "kv_norm", "w_kv", "k_norm",
                "attn_norm", "w_q", "q_norm", "sinks", "w_o", "rel_bias"]

SMALL = [
    ("ffn_norm", (2, 2, 1024), 2), ("ssm_norm", (1, 1024), 1), ("ssm_conv_w", (1, 4, 3072), 2),
    ("ssm_conv_b", (1, 3072), 1), ("ssm_gate_norm", (1, 2048), 1),
    ("ssm_dt_bias", (1, 32), None), ("ssm_a_log", (1, 32), None), ("ssm_d", (1, 32), None),
    ("kv_norm", (1024,), None), ("k_norm", (64,), None), ("attn_norm", (1, 1024), None),
    ("q_norm", (1, 64), None), ("sinks", (1, 16), None), ("rel_bias", (32, 16), None),
]
SMALL_W = 1024
SMALL_FULL_ROWS = 32
SMALL_LOCAL_ROWS = 48

MAT_GROUPS = {
    "f00_up": ["w1t_00", "w3t_00"], "f00_down": ["w2_00"], "f01": ["w1t_01", "w3t_01", "w2_01"],
    "f10": ["w1t_10", "w3t_10", "w2_10"], "f11": ["w1t_11", "w3t_11", "w2_11"],
    "ssm": ["w_int", "w_out"], "att": ["w_q", "w_o", "w_kv"],
    "f00_early": ["w2_00", "w1t_00"], "f00_late": ["w3t_00"],
}
FIRST_GATHER = "f00_up"
GATHER_PLAN = {"f00_upgate": ["f00_down", "ssm"], "ssm_in": ["f01"], "ssm_ssd": ["att", "f10"],
               "f01_upgate": ["f11"]}
SCATTER_A_PLAN = {"att_dwo": "f11", "kv_dknorm": "f10", "kv_du": "att", "f01_du": "f01", "ssm_du": "ssm",
                  "f00_dw3": "f00_early", "f00_du": "f00_late"}
SCATTER_B_PLAN = {"att_dcore": "f11", "f01_dw2": "att", "f01_dgate": "f10", "ssm_dssd": "f01", "f00_dgate": "ssm",
                  "f00_du": "f00_early"}
LAST_SCATTER = "f00_late"
SLOT_MAJOR = ("w_int",)


def _shard_shape(s, a):
    return s[:a] + (s[a] // N_DEV,) + s[a + 1:]


def _unshard_view(stack, shard_shape, axis):
    moved = jnp.moveaxis(stack, 0, axis)
    return moved.reshape(shard_shape[:axis] + (N_DEV * shard_shape[axis],) + shard_shape[axis + 1:])


def _small_local(arrs):
    flat = jnp.concatenate([arrs[n].reshape(-1) for n, _, _ in SMALL])
    return jnp.pad(flat, (0, SMALL_LOCAL_ROWS * LANES - flat.shape[0])).reshape(SMALL_LOCAL_ROWS, LANES)


def chip_partial(g4, ra, name):
    _, _, n, width = g4.shape

    def body(core_ref, g_ref, r_ref, o_ref):
        o_ref[0] = (g_ref[0, 0].astype(F32) + r_ref[0, 0].astype(F32)).astype(o_ref.dtype)

    grid_spec = pltpu.PrefetchScalarGridSpec(
        num_scalar_prefetch=1, grid=(N_CHIPS,),
        in_specs=[pl.BlockSpec((1, 1, n, width), lambda q, core: (q, core[0], 0, 0)),
                  pl.BlockSpec((1, 1, n, width), lambda q, core: (q, 0, 0, 0))],
        out_specs=pl.BlockSpec((1, n, width), lambda q, core: (q, 0, 0)))
    core = jnp.reshape(lax.axis_index("c"), (1,)).astype(jnp.int32)
    return pl.pallas_call(
        body, name=name, grid_spec=grid_spec, out_shape=jax.ShapeDtypeStruct((N_CHIPS, n, width), g4.dtype),
        compiler_params=pltpu.CompilerParams(dimension_semantics=("arbitrary",), vmem_limit_bytes=VMEM_LIMIT_BYTES),
    )(core, g4, ra)


class StepIO:
    def __init__(self, pieces):
        self.pieces = pieces
        self.full = {}
        self.grad = {}
        self.from_sibling = {}
        self.recv = {}

    def w(self, name):
        return self.full[name]

    def put(self, name, g):
        self.grad[name] = g

    def _by_chip_core(self, name):
        g = self.grad[name]
        return g.reshape((N_CHIPS, 2, g.shape[0] // N_DEV) + g.shape[1:])

    def gather_items(self, groups):
        names = [n for grp in groups for n in MAT_GROUPS[grp]]
        items = [("g2", self.pieces[n], None if n in SLOT_MAJOR else 0) for n in names]

        def sink(outs):
            for n, o in zip(names, outs):
                self.full[n] = o.reshape((-1,) + o.shape[2:]) if n in SLOT_MAJOR else o

        return items, sink

    def scatter_a_items(self, group):
        names = MAT_GROUPS[group]
        items = [("sa", self._by_chip_core(n), None) for n in names]

        def sink(outs):
            for n, o in zip(names, outs):
                self.from_sibling[n] = o

        return items, sink

    def scatter_b_items(self, group):
        names = MAT_GROUPS[group]
        items = [("sb", chip_partial(self._by_chip_core(n), self.from_sibling[n], "partial_" + n), None)
                 for n in names]

        def sink(outs):
            for n, o in zip(names, outs):
                self.recv[n] = o

        return items, sink

    def hook(self, site):
        parts = []
        if site in GATHER_PLAN:
            parts.append(self.gather_items(GATHER_PLAN[site]))
        if site in SCATTER_A_PLAN:
            parts.append(self.scatter_a_items(SCATTER_A_PLAN[site]))
        if site in SCATTER_B_PLAN:
            parts.append(self.scatter_b_items(SCATTER_B_PLAN[site]))
        if not parts:
            return None
        return combine_hooks(parts)


def combine_hooks(parts):
    items = [it for its, _ in parts for it in its]

    def sink(outs):
        p = 0
        for its, snk in parts:
            snk(outs[p:p + len(its)])
            p += len(its)

    return Comm(items), sink


def step(x, target, wts, ms, vs):
    me = _my_index()

    pieces = {}
    for li in range(2):
        for hi in range(2):
            tag = "%d%d" % (li, hi)
            pieces["w1t_" + tag] = wts["ffn_w1"][li, hi].T.astype(BF16)
            pieces["w3t_" + tag] = wts["ffn_w3"][li, hi].T.astype(BF16)
            pieces["w2_" + tag] = wts["ffn_w2"][li, hi].astype(BF16)
    pieces["w_int"] = wts["ssm_w_in"][0].T.astype(BF16)
    pieces["w_out"] = wts["ssm_w_out"][0].astype(BF16)
    pieces["w_kv"] = wts["w_kv"].astype(BF16)
    pieces["w_q"] = wts["w_q"][0].astype(BF16)
    pieces["w_o"] = wts["w_o"][0].astype(BF16)
    io = StepIO(pieces)

    small_sharded = [(n, s, a) for n, s, a in SMALL if a is not None]
    loc = jnp.concatenate([wts[n].reshape(-1) for n, _, _ in small_sharded])
    loc_rows = -(-loc.shape[0] // (8 * LANES)) * 8
    loc = jnp.pad(loc, (0, loc_rows * LANES - loc.shape[0])).reshape(loc_rows, LANES)
    got_small = []
    comm, sink = combine_hooks([io.gather_items([FIRST_GATHER]), ([("g", loc, None)], got_small.extend)])
    sink(comm_only(comm, "gather_first"))
    gath_small = got_small[0].reshape(N_DEV, -1)
    small = {}
    off = 0
    for n, s, a in small_sharded:
        shard = _shard_shape(s, a)
        cnt = int(np.prod(shard))
        small[n] = _unshard_view(gath_small[:, off:off + cnt].reshape((N_DEV,) + shard), shard, a)
        off += cnt
    for n, s, a in SMALL:
        if a is None:
            small[n] = wts[n]

    loss_part, grad_x, g_small_local = local_step(x[0], target[0], small, io)
    loss = lax.psum(loss_part, ("x", "y", "c"))

    small_flat = jnp.concatenate([g_small_local[n].reshape(-1) for n, _, _ in SMALL])
    small_buf = jnp.pad(small_flat, (0, SMALL_FULL_ROWS * SMALL_W - small_flat.shape[0]))
    small_buf = small_buf.reshape(SMALL_FULL_ROWS, SMALL_W)
    got_small = []
    comm, sink = combine_hooks([io.scatter_b_items(LAST_SCATTER), ([("g", small_buf, None)], got_small.extend)])
    sink(comm_only(comm, "exchange_last"))
    small_all = got_small[0]

    def sum_body(r_ref, o_ref):
        o_ref[...] = _slot_sum(r_ref)

    vmem = pl.BlockSpec(memory_space=pltpu.VMEM)
    small_sum, = pcall(sum_body, name="sum_small", grid=(), in_specs=[vmem], out_specs=[vmem],
                       out_shape=[jax.ShapeDtypeStruct((SMALL_FULL_ROWS, SMALL_W), F32)], args=[small_all])
    small_sum = small_sum.reshape(-1)
    g_small = {}
    off = 0
    for n, s, a in SMALL:
        cnt = int(np.prod(s))
        gfull = small_sum[off:off + cnt].reshape(s)
        off += cnt
        if a is None:
            g_small[n] = gfull
        else:
            width = s[a] // N_DEV
            g_small[n] = lax.dynamic_slice_in_dim(gfull, me * width, width, axis=a)

    out = {}

    def emit(name, res, shape):
        for kind, arr in zip(("grad", "delta", "new_m", "new_v"), res):
            out[kind + "_" + name] = arr.reshape(shape)

    for name, key in (("ffn_w1", "w1t_"), ("ffn_w3", "w3t_")):
        shp = wts[name].shape
        view = lambda t: t.reshape((4,) + shp[2:])
        res = adamw_cols([io.recv[key + tag] for tag in FFN_TAGS], view(wts[name]), view(ms[name]), view(vs[name]),
                         "adamw_" + name)
        emit(name, res, shp)
    shp = wts["ffn_w2"].shape
    view = lambda t: t.reshape((4,) + shp[2:])
    res = adamw_rows([io.recv["w2_" + tag] for tag in FFN_TAGS], view(wts["ffn_w2"]), view(ms["ffn_w2"]),
                     view(vs["ffn_w2"]), "adamw_ffn_w2")
    emit("ffn_w2", res, shp)
    res = adamw_cols([io.recv["w_int"]], wts["ssm_w_in"], ms["ssm_w_in"], vs["ssm_w_in"], "adamw_ssm_w_in")
    emit("ssm_w_in", res, wts["ssm_w_in"].shape)
    for name, key in (("ssm_w_out", "w_out"), ("w_kv", "w_kv"), ("w_q", "w_q"), ("w_o", "w_o")):
        shp = wts[name].shape
        view = lambda t: t.reshape((1,) + shp[-2:])
        res = adamw_rows([io.recv[key]], view(wts[name]), view(ms[name]), view(vs[name]), "adamw_" + name)
        emit(name, res, shp)

    res_s = rowmap(lambda gg, ww, mm_, vv: _adamw(gg, ww, mm_, vv),
                   [_small_local(g_small), _small_local(wts), _small_local(ms), _small_local(vs)], [],
                   [(LANES, F32)] * 3, tm=SMALL_LOCAL_ROWS, name="adamw_small")
    flat_s = [r.reshape(-1) for r in res_s]
    off = 0
    for n, s, a in SMALL:
        shard = s if a is None else _shard_shape(s, a)
        cnt = int(np.prod(shard))
        out["grad_" + n] = g_small[n]
        for kind, arr in zip(("delta", "new_m", "new_v"), flat_s):
            out[kind + "_" + n] = arr[off:off + cnt].reshape(shard)
        off += cnt
    out["loss"] = loss
    out["grad_x"] = grad_x[None]
    return out


def kernel(x, ffn_norm, ffn_w1, ffn_w3, ffn_w2, ssm_norm, ssm_w_in, ssm_conv_w, ssm_conv_b, ssm_dt_bias, ssm_a_log, ssm_d, ssm_gate_norm, ssm_w_out, kv_norm, w_kv, k_norm, attn_norm, w_q, q_norm, sinks, w_o, rel_bias, loss_target, m_ffn_norm, m_ffn_w1, m_ffn_w3, m_ffn_w2, m_ssm_norm, m_ssm_w_in, m_ssm_conv_w, m_ssm_conv_b, m_ssm_dt_bias, m_ssm_a_log, m_ssm_d, m_ssm_gate_norm, m_ssm_w_out, m_kv_norm, m_w_kv, m_k_norm, m_attn_norm, m_w_q, m_q_norm, m_sinks, m_w_o, m_rel_bias, v_ffn_norm, v_ffn_w1, v_ffn_w3, v_ffn_w2, v_ssm_norm, v_ssm_w_in, v_ssm_conv_w, v_ssm_conv_b, v_ssm_dt_bias, v_ssm_a_log, v_ssm_d, v_ssm_gate_norm, v_ssm_w_out, v_kv_norm, v_w_kv, v_k_norm, v_attn_norm, v_w_q, v_q_norm, v_sinks, v_w_o, v_rel_bias):
    args = locals()
    wts = {n: args[n] for n in WEIGHT_NAMES}
    ms = {n: args["m_" + n] for n in WEIGHT_NAMES}
    vs = {n: args["v_" + n] for n in WEIGHT_NAMES}
    out = step(x, loss_target, wts, ms, vs)
    result = [out["loss"], out["grad_x"]]
    for kind in ("grad", "delta", "new_m", "new_v"):
        result += [out[kind + "_" + n] for n in WEIGHT_NAMES]
    return tuple(result)
```

```python
import functools
import math
import operator

import numpy as np
import jax
import jax.numpy as jnp
from jax import lax
from jax.experimental import pallas as pl
from jax.experimental.pallas import tpu as pltpu

F32 = jnp.float32
BF16 = jnp.bfloat16

D_MODEL = 1024
D_FF = 2816
N_DEV = 8
SSM_D_INNER = 2048
SSM_HEAD_DIM = 64
SSM_HEADS = 32
SSM_GROUPS = 4
SSM_STATE = 128
SSM_CONV = 4
SSM_CHUNK = 256
SSM_CONV_DIM = SSM_D_INNER + 2 * SSM_GROUPS * SSM_STATE
SSM_IN_DIM = SSM_D_INNER + SSM_CONV_DIM + SSM_HEADS
ATT_HEAD_DIM = 64
ATT_HEADS = 16
ATT_KV_HEADS = 2
ATT_GROUP = 8
ATT_WINDOW = 128
REL_BUCKETS = 32
EPS = 1e-6
NEG = -1e30

ADAM_LR = 0.001
ADAM_B1 = 0.9
ADAM_B2 = 0.999
ADAM_EPS = 1e-08
ADAM_WD = 0.01
ADAM_STEP = 10

VMEM_LIMIT_BYTES = 52 * 1024 * 1024
LANES = 128
MESH_ID = pl.DeviceIdType.MESH
ANY_SPEC = pl.BlockSpec(memory_space=pl.ANY)

NT = (((1,), (1,)), ((), ()))
TN = (((0,), (0,)), ((), ()))
NN = (((1,), (0,)), ((), ()))


def _pick(dim, cands):
    for c in cands:
        if dim % c == 0:
            return c
    return dim


def _my_index():
    return 4 * lax.axis_index("x") + 2 * lax.axis_index("y") + lax.axis_index("c")


def _peer(k):
    x, y, c = lax.axis_index("x"), lax.axis_index("y"), lax.axis_index("c")
    px = 1 - x if (k >> 2) & 1 else x
    py = 1 - y if (k >> 1) & 1 else y
    pc = 1 - c if k & 1 else c
    return (px, py, pc), 4 * px + 2 * py + pc


def _piece(ref, axis, d, n):
    if axis is None:
        return ref.at[d]
    return ref.at[(slice(None),) * axis + (pl.ds(pl.multiple_of(d * n, 8), n),)]


SIBLING = 1
CHIP_PEERS = (4, 2, 6)
N_CHIPS = 4
SEMS_PER_ITEM = N_DEV - 1


def _my_chip():
    return 2 * lax.axis_index("x") + lax.axis_index("y")


class Comm:
    def __init__(self, items):
        self.items = list(items)

    def dst_shapes(self):
        out = []
        for kind, src, axis in self.items:
            s = tuple(src.shape)
            if kind == "g":
                shp = (N_DEV,) + s
            elif kind == "g2":
                shp = (N_DEV,) + s if axis is None else s[:axis] + (N_DEV * s[axis],) + s[axis + 1:]
            elif kind == "sa":
                shp = (s[0], 1) + s[2:]
            else:
                shp = s
            out.append(jax.ShapeDtypeStruct(shp, src.dtype))
        return out

    def scratch(self):
        n = len(self.items)
        return [pltpu.SemaphoreType.DMA((n * SEMS_PER_ITEM,)), pltpu.SemaphoreType.DMA((n * SEMS_PER_ITEM,)),
                pltpu.SemaphoreType.DMA((n,))]

    def _run(self, srcs, dsts, sems, starting):
        send_sems, recv_sems, local_sems = sems
        me = _my_index()
        core = lax.axis_index("c")
        chip = _my_chip()
        for i, (kind, src, axis) in enumerate(self.items):
            s_ref, d_ref = srcs[i], dsts[i]
            base = i * SEMS_PER_ITEM

            def rdma(src_ref, dst_ref, j, peer):
                return pltpu.make_async_remote_copy(
                    src_ref=src_ref, dst_ref=dst_ref, send_sem=send_sems.at[base + j], recv_sem=recv_sems.at[base + j],
                    device_id=peer, device_id_type=MESH_ID)

            if kind == "g":
                local = pltpu.make_async_copy(s_ref, d_ref.at[me], local_sems.at[i])
                outs = [rdma(s_ref, d_ref.at[me], k - 1, _peer(k)[0]) for k in range(1, N_DEV)]
                if starting:
                    local.start()
                    for cp in outs:
                        cp.start()
                else:
                    for k in range(1, N_DEV):
                        rdma(s_ref, d_ref.at[_peer(k)[1]], k - 1, _peer(k)[0]).wait_recv()
                    for cp in outs:
                        cp.wait_send()
                    local.wait()
            elif kind == "g2":
                n = None if axis is None else src.shape[axis]
                mine = _piece(d_ref, axis, me, n)
                sib = _peer(SIBLING)[0]
                local = pltpu.make_async_copy(s_ref, mine, local_sems.at[i])
                outs = [rdma(s_ref, mine, 0, sib)] + [rdma(s_ref, mine, 1 + j, _peer(k)[0])
                                                      for j, k in enumerate(CHIP_PEERS)]
                if starting:
                    local.start()
                    for cp in outs:
                        cp.start()
                else:
                    passed = []
                    for j, k in enumerate(CHIP_PEERS):
                        theirs = _piece(d_ref, axis, _peer(k)[1], n)
                        rdma(s_ref, theirs, 1 + j, _peer(k)[0]).wait_recv()
                        fwd = rdma(theirs, theirs, 4 + j, sib)
                        fwd.start()
                        passed.append(fwd)
                    rdma(s_ref, _piece(d_ref, axis, _peer(SIBLING)[1], n), 0, sib).wait_recv()
                    for j, k in enumerate(CHIP_PEERS):
                        rdma(s_ref, _piece(d_ref, axis, _peer(k ^ SIBLING)[1], n), 4 + j, sib).wait_recv()
                    for cp in outs + passed:
                        cp.wait_send()
                    local.wait()
            elif kind == "sa":
                cp = rdma(s_ref.at[(slice(None), pl.ds(1 - core, 1))], d_ref, 0, _peer(SIBLING)[0])
                if starting:
                    cp.start()
                else:
                    cp.wait_recv()
                    cp.wait_send()
            else:
                local = pltpu.make_async_copy(s_ref.at[chip], d_ref.at[chip], local_sems.at[i])
                outs = [rdma(s_ref.at[_peer(k)[1] >> 1], d_ref.at[chip], 1 + j, _peer(k)[0])
                        for j, k in enumerate(CHIP_PEERS)]
                if starting:
                    local.start()
                    for cp in outs:
                        cp.start()
                else:
                    for j, k in enumerate(CHIP_PEERS):
                        rdma(s_ref.at[chip], d_ref.at[_peer(k)[1] >> 1], 1 + j, _peer(k)[0]).wait_recv()
                    for cp in outs:
                        cp.wait_send()
                    local.wait()

    def start(self, srcs, dsts, sems):
        self._run(srcs, dsts, sems, True)

    def wait(self, srcs, dsts, sems):
        self._run(srcs, dsts, sems, False)


def pcall(body, *, name, grid, in_specs, out_specs, out_shape, args, scratch=(), hook=None):
    cparams = pltpu.CompilerParams(dimension_semantics=("arbitrary",) * len(grid), vmem_limit_bytes=VMEM_LIMIT_BYTES)
    if hook is None:
        outs = pl.pallas_call(body, name=name, grid=grid, in_specs=list(in_specs), out_specs=list(out_specs),
                              out_shape=list(out_shape), scratch_shapes=list(scratch), compiler_params=cparams)(*args)
        return list(outs)
    comm, sink = hook
    n_in, n_out, n_scr, n_it = len(args), len(out_shape), len(scratch), len(comm.items)
    dims = tuple(grid)

    def wrapped(*refs):
        p = 0
        ins = refs[p:p + n_in]
        p += n_in
        csrc = refs[p:p + n_it]
        p += n_it
        outs = refs[p:p + n_out]
        p += n_out
        cdst = refs[p:p + n_it]
        p += n_it
        scr = refs[p:p + n_scr]
        p += n_scr
        sems = refs[p:p + 3]
        if dims:
            ids = [pl.program_id(a) for a in range(len(dims))]
            first = functools.reduce(operator.and_, [i == 0 for i in ids])
            last = functools.reduce(operator.and_, [i == d - 1 for i, d in zip(ids, dims)])

            @pl.when(first)
            def _():
                comm.start(csrc, cdst, sems)

            body(*ins, *outs, *scr)

            @pl.when(last)
            def _():
                comm.wait(csrc, cdst, sems)
        else:
            comm.start(csrc, cdst, sems)
            body(*ins, *outs, *scr)
            comm.wait(csrc, cdst, sems)

    res = pl.pallas_call(
        wrapped, name=name, grid=grid,
        in_specs=list(in_specs) + [ANY_SPEC] * n_it, out_specs=list(out_specs) + [ANY_SPEC] * n_it,
        out_shape=list(out_shape) + comm.dst_shapes(), scratch_shapes=list(scratch) + comm.scratch(),
        compiler_params=cparams,
    )(*args, *[src for _, src, _ in comm.items])
    res = list(res)
    sink(res[n_out:])
    return res[:n_out]


def comm_only(comm, name):
    got = []
    pcall(lambda *refs: None, name=name, grid=(), in_specs=[], out_specs=[], out_shape=[], args=[],
          hook=(comm, got.extend))
    return got


def mm(a, b, *, ta=False, tb=False, out_dtype=F32, res=None, alpha=1.0, name, hook=None):
    if ta:
        k_dim, m_dim = a.shape
    else:
        m_dim, k_dim = a.shape
    if tb:
        n_dim, k2 = b.shape
    else:
        k2, n_dim = b.shape
    assert k_dim == k2, (a.shape, b.shape, ta, tb)
    tn = _pick(n_dim, (1024, 1408, 512, 256, 128))
    tm = _pick(m_dim, (1024, 1408, 512, 256, 128)) if tn <= 1024 else _pick(m_dim, (512, 256, 128))
    tk = _pick(k_dim, (1024, 512, 256, 128)) if ta else _pick(k_dim, (512, 1408, 256, 128))
    if max(m_dim, n_dim) <= 64:
        tk = _pick(k_dim, (8192, 4096, 2048, 1024, 512))
    nk = k_dim // tk
    has_res = res is not None
    dn = (((0 if ta else 1,), (1 if tb else 0,)), ((), ()))

    def body(*refs):
        if has_res:
            a_ref, b_ref, r_ref, o_ref, acc_ref = refs
        else:
            a_ref, b_ref, o_ref, acc_ref = refs
        k = pl.program_id(2)

        @pl.when(k == 0)
        def _():
            acc_ref[...] = jnp.zeros_like(acc_ref)

        acc_ref[...] += lax.dot_general(a_ref[...].astype(BF16), b_ref[...].astype(BF16), dn,
                                        preferred_element_type=F32)

        @pl.when(k == nk - 1)
        def _():
            r = acc_ref[...]
            if alpha != 1.0:
                r = r * alpha
            if has_res:
                r = r_ref[...] + r
            o_ref[...] = r.astype(o_ref.dtype)

    a_spec = pl.BlockSpec((tk, tm), lambda i, j, k: (k, i)) if ta else pl.BlockSpec((tm, tk), lambda i, j, k: (i, k))
    b_spec = pl.BlockSpec((tn, tk), lambda i, j, k: (j, k)) if tb else pl.BlockSpec((tk, tn), lambda i, j, k: (k, j))
    o_spec = pl.BlockSpec((tm, tn), lambda i, j, k: (i, j))
    in_specs = [a_spec, b_spec] + ([o_spec] if has_res else [])
    args = [a, b] + ([res] if has_res else [])
    out, = pcall(body, name=name, grid=(m_dim // tm, n_dim // tn, nk), in_specs=in_specs, out_specs=[o_spec],
                 out_shape=[jax.ShapeDtypeStruct((m_dim, n_dim), out_dtype)], args=args,
                 scratch=[pltpu.VMEM((tm, tn), F32)], hook=hook)
    return out


def rowmap(fn, rows, consts=(), out_rows=(), out_accs=(), *, tm, name, hook=None):
    first = rows[0][0] if isinstance(rows[0], tuple) else rows[0]
    t_dim = first.shape[0]
    assert t_dim % tm == 0, (t_dim, tm)
    n_r, n_c, n_o = len(rows), len(consts), len(out_rows)

    def body(*refs):
        ins = [r[...] for r in refs[:n_r + n_c]]
        o_refs = refs[n_r + n_c:]
        outs = tuple(fn(*ins))
        for o_ref, val in zip(o_refs[:n_o], outs[:n_o]):
            o_ref[...] = val.astype(o_ref.dtype)
        if out_accs:
            @pl.when(pl.program_id(0) == 0)
            def _():
                for o_ref in o_refs[n_o:]:
                    o_ref[...] = jnp.zeros_like(o_ref)

            for o_ref, val in zip(o_refs[n_o:], outs[n_o:]):
                o_ref[...] += val

    in_specs, args = [], []
    for r in rows:
        if isinstance(r, tuple):
            args.append(r[0])
            in_specs.append(r[1])
        else:
            args.append(r)
            in_specs.append(pl.BlockSpec((tm, r.shape[1]), lambda i: (i, 0)))
    for c in consts:
        args.append(c)
        in_specs.append(pl.BlockSpec(c.shape, lambda i, nd=c.ndim: (0,) * nd))
    out_specs = [pl.BlockSpec((tm, w), lambda i: (i, 0)) for (w, _) in out_rows]
    out_specs += [pl.BlockSpec(s, lambda i, nd=len(s): (0,) * nd) for s in out_accs]
    out_shape = [jax.ShapeDtypeStruct((t_dim, w), dt) for (w, dt) in out_rows]
    out_shape += [jax.ShapeDtypeStruct(s, F32) for s in out_accs]
    return pcall(body, name=name, grid=(t_dim // tm,), in_specs=in_specs, out_specs=out_specs, out_shape=out_shape,
                 args=args, hook=hook)


def _rms_fwd(x, g):
    r = lax.rsqrt(jnp.mean(x * x, axis=-1, keepdims=True) + EPS)
    return x * r * g


def _rms_bwd(x, g, dy):
    r = lax.rsqrt(jnp.mean(x * x, axis=-1, keepdims=True) + EPS)
    xh = x * r
    dg = jnp.sum(dy * xh, axis=0, keepdims=True)
    dxh = dy * g
    dx = r * (dxh - xh * jnp.mean(dxh * xh, axis=-1, keepdims=True))
    return dx, dg


def _sigmoid(x):
    return 1.0 / (1.0 + jnp.exp(-x))


def _silu(x):
    return x * _sigmoid(x)


def _silu_grad(x):
    s = _sigmoid(x)
    return s * (1.0 + x * (1.0 - s))


def _split3(x):
    hi = x.astype(BF16)
    r1 = x - hi.astype(F32)
    mid = r1.astype(BF16)
    lo = (r1 - mid.astype(F32)).astype(BF16)
    return hi, mid, lo


def _dot(a, b, dn=NN):
    return lax.dot_general(a.astype(BF16), b.astype(BF16), dn, preferred_element_type=F32)


FFN_TN = 1408
RESIDENT_TM = 512


def ffn_upgate(h, g, w1t, w3t, nm, hook=None):
    t_dim = h.shape[0]
    tm = _pick(t_dim, (512, 256, 128))
    tn = FFN_TN

    n_j = D_FF // tn
    u_w = D_MODEL // n_j

    def body(h_ref, g_ref, w1_ref, w3_ref, u_ref, a_ref, b_ref, hm_ref):
        uu = _rms_fwd(h_ref[...], g_ref[...]).astype(BF16)
        for j in range(n_j):
            @pl.when(pl.program_id(0) == j)
            def _(j=j):
                u_ref[...] = uu[:, j * u_w:(j + 1) * u_w]

        a = lax.dot_general(uu, w1_ref[...], NT, preferred_element_type=F32)
        b = lax.dot_general(uu, w3_ref[...], NT, preferred_element_type=F32)
        a_ref[...] = a.astype(a_ref.dtype)
        b_ref[...] = b.astype(b_ref.dtype)
        hm_ref[...] = (_silu(a) * b).astype(hm_ref.dtype)

    row_spec = pl.BlockSpec((tm, D_MODEL), lambda j, i: (i, 0))
    w_spec = pl.BlockSpec((tn, D_MODEL), lambda j, i: (j, 0))
    o_spec = pl.BlockSpec((tm, tn), lambda j, i: (i, j))
    o_shape = jax.ShapeDtypeStruct((t_dim, D_FF), BF16)
    return pcall(body, name=nm, grid=(D_FF // tn, t_dim // tm),
                 in_specs=[row_spec, pl.BlockSpec((1, D_MODEL), lambda j, i: (0, 0)), w_spec, w_spec],
                 out_specs=[pl.BlockSpec((tm, u_w), lambda j, i: (i, j))] + [o_spec] * 3,
                 out_shape=[jax.ShapeDtypeStruct((t_dim, D_MODEL), BF16)] + [o_shape] * 3,
                 args=[h, g, w1t, w3t], hook=hook)


def ffn_dgate(dout_bf, w2, a, b, nm, hook=None):
    t_dim = dout_bf.shape[0]
    tm = _pick(t_dim, (512, 256, 128))
    tn = FFN_TN

    def body(d_ref, w2_ref, a_ref, b_ref, da_ref, db_ref):
        dhm = 0.5 * lax.dot_general(d_ref[...], w2_ref[...], NT, preferred_element_type=F32)
        av = a_ref[...].astype(F32)
        bv = b_ref[...].astype(F32)
        sg = _sigmoid(av)
        da_ref[...] = (dhm * bv * (sg * (1.0 + av * (1.0 - sg)))).astype(da_ref.dtype)
        db_ref[...] = (dhm * (av * sg)).astype(db_ref.dtype)

    t_spec = pl.BlockSpec((tm, tn), lambda j, i: (i, j))
    o_shape = jax.ShapeDtypeStruct((t_dim, D_FF), BF16)
    return pcall(body, name=nm, grid=(D_FF // tn, t_dim // tm),
                 in_specs=[pl.BlockSpec((tm, D_MODEL), lambda j, i: (i, 0)),
                           pl.BlockSpec((tn, D_MODEL), lambda j, i: (j, 0)), t_spec, t_spec],
                 out_specs=[t_spec] * 2, out_shape=[o_shape] * 2, args=[dout_bf, w2, a, b], hook=hook)


def ffn_fwd(h, g, tag, io, target=None):
    nm = "f" + tag
    u, a, b, hm = ffn_upgate(h, g, io.w("w1t_" + tag), io.w("w3t_" + tag), nm + "_upgate",
                             hook=io.hook(nm + "_upgate"))
    if target is None:
        return mm(hm, io.w("w2_" + tag), res=h, alpha=0.5, name=nm + "_down"), (u, a, b, hm)

    def down_loss(hmv, hv, t, w2):
        e = hv + 0.5 * _dot(hmv, w2) - t
        d = e * (1.0 / D_MODEL)
        return d, d, jnp.sum(e * e, axis=0, keepdims=True)

    res = rowmap(down_loss, [hm, h, target], [io.w("w2_" + tag)], [(D_MODEL, F32), (D_MODEL, BF16)],
                 [(1, D_MODEL)], tm=RESIDENT_TM, name=nm + "_down_loss")
    return res, (u, a, b, hm)


def du_norm_bwd(pairs, h, g, dout, nm, hook=None):
    t_dim = h.shape[0]
    tm = RESIDENT_TM
    n_p = len(pairs)

    def body(*refs):
        h_ref, d_ref, g_ref = refs[2 * n_p:2 * n_p + 3]
        dh_ref, dhb_ref, dg_ref = refs[2 * n_p + 3:]
        du = None
        for p, (_, _, tb) in enumerate(pairs):
            t = lax.dot_general(refs[2 * p][...].astype(BF16), refs[2 * p + 1][...].astype(BF16), NT if tb else NN,
                                preferred_element_type=F32)
            du = t if du is None else du + t
        dx, dg = _rms_bwd(h_ref[...], g_ref[...], du)
        dh = d_ref[...] + dx
        dh_ref[...] = dh
        dhb_ref[...] = dh.astype(dhb_ref.dtype)

        @pl.when(pl.program_id(0) == 0)
        def _():
            dg_ref[...] = jnp.zeros_like(dg_ref)

        dg_ref[...] += dg

    in_specs, args = [], []
    for a, b, _ in pairs:
        in_specs += [pl.BlockSpec((tm, a.shape[1]), lambda i: (i, 0)), pl.BlockSpec(b.shape, lambda i: (0, 0))]
        args += [a, b]
    row_spec = pl.BlockSpec((tm, D_MODEL), lambda i: (i, 0))
    vec_spec = pl.BlockSpec((1, D_MODEL), lambda i: (0, 0))
    return pcall(body, name=nm, grid=(t_dim // tm,), in_specs=in_specs + [row_spec, row_spec, vec_spec],
                 out_specs=[row_spec, row_spec, vec_spec],
                 out_shape=[jax.ShapeDtypeStruct((t_dim, D_MODEL), F32), jax.ShapeDtypeStruct((t_dim, D_MODEL), BF16),
                            jax.ShapeDtypeStruct((1, D_MODEL), F32)],
                 args=args + [h, dout, g], hook=hook)


def ffn_bwd(h, g, tag, saved, dout, dout_bf, io):
    nm = "f" + tag
    w1t, w3t, w2 = io.w("w1t_" + tag), io.w("w3t_" + tag), io.w("w2_" + tag)
    u, a, b, hm = saved
    io.put("w2_" + tag, mm(hm, dout_bf, ta=True, alpha=0.5, out_dtype=BF16, name=nm + "_dw2",
                           hook=io.hook(nm + "_dw2")))
    da, db = ffn_dgate(dout_bf, w2, a, b, nm + "_dgate", hook=io.hook(nm + "_dgate"))
    io.put("w1t_" + tag, mm(da, u, ta=True, out_dtype=BF16, name=nm + "_dw1"))
    io.put("w3t_" + tag, mm(db, u, ta=True, out_dtype=BF16, name=nm + "_dw3", hook=io.hook(nm + "_dw3")))
    return du_norm_bwd([(da, w1t, False), (db, w3t, False)], h, g, dout, nm + "_du", hook=io.hook(nm + "_du"))


def conv_input_grad(d_parts, w, nm):
    tm = 256
    t_dim = d_parts[0].shape[0]
    n_tiles = t_dim // tm

    def fn(d1, n1, d2, n2, d3, n3, ww):
        d = jnp.concatenate([d1, d2, d3], axis=1)
        nxt = jnp.concatenate([n1, n2, n3], axis=1)
        nxt = jnp.where(pl.program_id(0) < n_tiles - 1, nxt, 0.0)
        dd = jnp.concatenate([d, nxt], axis=0)
        out = dd[3:3 + tm] * ww[0:1]
        for k in range(1, SSM_CONV):
            out = out + dd[3 - k:3 - k + tm] * ww[k:k + 1]
        return (out,)

    rows = []
    for d in d_parts:
        below = pl.BlockSpec((8, d.shape[1]), lambda i: (jnp.minimum((i + 1) * (tm // 8), t_dim // 8 - 1), 0))
        rows += [d, (d, below)]
    dx, = rowmap(fn, rows, [w], [(SSM_CONV_DIM, BF16)], tm=tm, name=nm)
    return dx


GRP_W = SSM_D_INNER // SSM_GROUPS
HPG = SSM_HEADS // SSM_GROUPS
HEAD_SHIFT = 6


def _split2(x):
    hi = x.astype(BF16)
    return hi, (x - hi.astype(F32)).astype(BF16)


def _expand_mats():
    e = ((lax.broadcasted_iota(jnp.int32, (HPG, GRP_W), 1) >> HEAD_SHIFT)
         == lax.broadcasted_iota(jnp.int32, (HPG, GRP_W), 0)).astype(BF16)
    et = ((lax.broadcasted_iota(jnp.int32, (GRP_W, HPG), 0) >> HEAD_SHIFT)
          == lax.broadcasted_iota(jnp.int32, (GRP_W, HPG), 1)).astype(BF16)
    return e, et


def _expand(v, e_m):
    hi, lo = _split2(v)
    return jnp.dot(hi, e_m, preferred_element_type=F32) + jnp.dot(lo, e_m, preferred_element_type=F32)


def _reduce8(v, et_m):
    hi, lo = _split2(v)
    return jnp.dot(hi, et_m, preferred_element_type=F32) + jnp.dot(lo, et_m, preferred_element_type=F32)


def _ssd_group_terms(dt_ref, dtT_ref, arow_ref, acol_ref):
    L = SSM_CHUNK
    r = lax.broadcasted_iota(jnp.int32, (L, L), 0)
    c = lax.broadcasted_iota(jnp.int32, (L, L), 1)
    tril = (r >= c).astype(BF16)
    triu = (r <= c).astype(BF16)
    dtg = dt_ref[0]
    acol = None
    for p in _split3(dtg * arow_ref[0]):
        t = jnp.dot(tril, p, preferred_element_type=F32)
        acol = t if acol is None else acol + t
    arowT = None
    for p in _split3(dtT_ref[0] * acol_ref[0]):
        t = jnp.dot(p, triu, preferred_element_type=F32)
        arowT = t if arowT is None else arowT + t
    return dtg, acol, arowT, r >= c


def _state_decay(a_last_col, et_m):
    hi, lo = _split2(jnp.broadcast_to(jnp.exp(a_last_col), (HPG, SSM_STATE)))
    return jnp.dot(et_m, hi, preferred_element_type=F32) + jnp.dot(et_m, lo, preferred_element_type=F32)


def _conv_block(x_ref, halo_ref, w_ref, b_ref, first):
    L = SSM_CHUNK
    xx = jnp.concatenate([jnp.where(first, 0.0, halo_ref[...]), x_ref[...]], axis=0)
    w = w_ref[...]
    shifted = [pltpu.roll(xx, SSM_CONV - 1 - k, 0)[8:8 + L] if k < SSM_CONV - 1 else xx[8:8 + L]
               for k in range(SSM_CONV)]
    acc = b_ref[...] + shifted[0] * w[0:1]
    for k in range(1, SSM_CONV):
        acc = acc + shifted[k] * w[k:k + 1]
    return acc, shifted


def _ssd_specs(nc, rev):
    L, N = SSM_CHUNK, SSM_STATE
    xcols = SSM_D_INNER // LANES
    ch = (lambda c: nc - 1 - c) if rev else (lambda c: c)
    above = lambda c: jnp.maximum(ch(c) * (L // 8) - 1, 0)
    specs = []
    for width, col in ((GRP_W, lambda g: g), (N, lambda g: xcols + g), (N, lambda g: xcols + SSM_GROUPS + g)):
        specs += [
            pl.BlockSpec((L, width), lambda c, g, col=col: (ch(c), col(g))),
            pl.BlockSpec((8, width), lambda c, g, col=col: (above(c), col(g))),
            pl.BlockSpec((SSM_CONV, width), lambda c, g, col=col: (0, col(g))),
            pl.BlockSpec((1, width), lambda c, g, col=col: (0, col(g))),
        ]
    return specs + [
        pl.BlockSpec((1, L, HPG), lambda c, g: (g, ch(c), 0)),
        pl.BlockSpec((1, HPG, L), lambda c, g: (g, 0, ch(c))),
        pl.BlockSpec((1, 1, HPG), lambda c, g: (g, 0, 0)),
        pl.BlockSpec((1, HPG, 1), lambda c, g: (g, 0, 0)),
        pl.BlockSpec((1, GRP_W), lambda c, g: (0, g)),
    ]


def ssd_fwd(xbc_raw, conv_w, conv_b, dt_g, dtT_g, a_row, a_col, dvec, nm, hook=None):
    t_dim = xbc_raw.shape[0]
    L, P, N = SSM_CHUNK, SSM_HEAD_DIM, SSM_STATE
    nc = t_dim // L

    def body(x_ref, xh_ref, xw_ref, xb_ref, b_ref, bh_ref, bw_ref, bb_ref, c_ref, ch_ref, cw_ref, cb_ref,
             dt_ref, dtT_ref, arow_ref, acol_ref, dvec_ref, y_ref, st_ref, s_s):
        ci = pl.program_id(0)
        g = pl.program_id(1)

        @pl.when((ci == 0) & (g == 0))
        def _():
            s_s[...] = jnp.zeros_like(s_s)

        e_m, et_m = _expand_mats()
        dtg, acol, arowT, causal = _ssd_group_terms(dt_ref, dtT_ref, arow_ref, acol_ref)
        a_last_row = acol[L - 1:L, :]
        x = _silu(_conv_block(x_ref, xh_ref, xw_ref, xb_ref, ci == 0)[0])
        bm = _silu(_conv_block(b_ref, bh_ref, bw_ref, bb_ref, ci == 0)[0])
        cm = _silu(_conv_block(c_ref, ch_ref, cw_ref, cb_ref, ci == 0)[0])
        cb = _dot(cm, bm, NT)
        s = s_s[g]
        st_ref[0, 0] = s
        ea_x = _expand(jnp.exp(acol), e_m)
        dt_x = _expand(dtg, e_m)
        w_x = _expand(jnp.exp(a_last_row - acol) * dtg, e_m)
        yb = ea_x * _dot(cm, s, NT) + dvec_ref[...] * x
        xd = (x * dt_x).astype(BF16)
        for e in range(HPG):
            sl = slice(e * P, (e + 1) * P)
            lm = jnp.exp(jnp.where(causal, acol[:, e:e + 1] - arowT[e:e + 1, :], NEG))
            m = (cb * lm).astype(BF16)
            y_ref[:, sl] = yb[:, sl] + jnp.dot(m, xd[:, sl], preferred_element_type=F32)
        s_s[g] = _state_decay(arowT[:, L - 1:L], et_m) * s + _dot(x * w_x, bm, TN)

    out_specs = [
        pl.BlockSpec((L, GRP_W), lambda c, g: (c, g)),
        pl.BlockSpec((1, 1, GRP_W, N), lambda c, g: (c, g, 0, 0)),
    ]
    return pcall(
        body, name=nm, grid=(nc, SSM_GROUPS), in_specs=_ssd_specs(nc, False), out_specs=out_specs,
        out_shape=[jax.ShapeDtypeStruct((t_dim, SSM_D_INNER), F32),
                   jax.ShapeDtypeStruct((nc, SSM_GROUPS, GRP_W, N), F32)],
        scratch=[pltpu.VMEM((SSM_GROUPS, GRP_W, N), F32)],
        args=[xbc_raw, xbc_raw, conv_w, conv_b] * 3 + [dt_g, dtT_g, a_row, a_col, dvec], hook=hook)


def ssd_bwd(dy, xbc_raw, conv_w, conv_b, dt_g, dtT_g, a_row, a_col, dvec, states, nm, hook=None):
    t_dim = xbc_raw.shape[0]
    L, P, N = SSM_CHUNK, SSM_HEAD_DIM, SSM_STATE
    nc = t_dim // L

    def body(dy_ref, x_ref, xh_ref, xw_ref, xb_ref, b_ref, bh_ref, bw_ref, bb_ref, c_ref, ch_ref, cw_ref, cb_ref,
             dt_ref, dtT_ref, arow_ref, acol_ref, dvec_ref, st_ref,
             dx_ref, db_ref, dc_ref, da_ref, ddt_ref, dd_ref, dwx_ref, dwb_ref, dwc_ref, dbx_ref, dbb_ref, dbc_ref,
             ds_s, yd_s, dxd_s):
        ci = pl.program_id(0)
        g = pl.program_id(1)

        @pl.when((ci == 0) & (g == 0))
        def _():
            ds_s[...] = jnp.zeros_like(ds_s)
            for r in (dd_ref, dwx_ref, dwb_ref, dwc_ref, dbx_ref, dbb_ref, dbc_ref):
                r[...] = jnp.zeros_like(r)

        e_m, et_m = _expand_mats()
        dtg, acol, arowT, causal = _ssd_group_terms(dt_ref, dtT_ref, arow_ref, acol_ref)
        a_last_row = acol[L - 1:L, :]
        first = ci == nc - 1
        pre_x, sh_x = _conv_block(x_ref, xh_ref, xw_ref, xb_ref, first)
        pre_b, sh_b = _conv_block(b_ref, bh_ref, bw_ref, bb_ref, first)
        pre_c, sh_c = _conv_block(c_ref, ch_ref, cw_ref, cb_ref, first)
        sg_x, sg_b, sg_c = _sigmoid(pre_x), _sigmoid(pre_b), _sigmoid(pre_c)
        x = pre_x * sg_x
        dy = dy_ref[...]
        bm = pre_b * sg_b
        cm = pre_c * sg_c
        cb = _dot(cm, bm, NT)
        s = st_ref[0, 0]
        dsp = ds_s[g]
        ew8 = jnp.exp(a_last_row - acol)
        ea_x = _expand(jnp.exp(acol), e_m)
        dt_x = _expand(dtg, e_m)
        ew_x = _expand(ew8, e_m)
        w_x = ew_x * dt_x
        z = _dot(cm, s, NT)
        dz = ea_x * dy
        dc = _dot(dz, s)
        ds_y = _dot(dz, cm, TN)
        du = _dot(bm, dsp, NT)
        u = x * w_x
        db = _dot(u, dsp)
        xd = (x * dt_x).astype(BF16)
        dyb = dy.astype(BF16)
        dcb = jnp.zeros((L, L), F32)
        for e in range(HPG):
            sl = slice(e * P, (e + 1) * P)
            lm = jnp.exp(jnp.where(causal, acol[:, e:e + 1] - arowT[e:e + 1, :], NEG))
            m = (cb * lm).astype(BF16)
            yd_s[:, sl] = jnp.dot(m, xd[:, sl], preferred_element_type=F32)
            dxd_s[:, sl] = lax.dot_general(m, dyb[:, sl], TN, preferred_element_type=F32)
            dcb = dcb + lax.dot_general(dyb[:, sl], xd[:, sl], NT, preferred_element_type=F32) * lm
        dxd = dxd_s[...]

        def through_conv(d_act, pre, sg, shifted, d_ref, dw_ref, dbias_ref):
            d_pre = d_act * (sg * (1.0 + pre * (1.0 - sg)))
            d_ref[...] = d_pre
            dw_ref[g] += jnp.concatenate([jnp.sum(d_pre * sh, axis=0, keepdims=True) for sh in shifted], axis=0)
            dbias_ref[g] += jnp.sum(d_pre, axis=0, keepdims=True)

        through_conv(dvec_ref[...] * dy + du * w_x + dt_x * dxd, pre_x, sg_x, sh_x, dx_ref, dwx_ref, dbx_ref)
        ddt = _reduce8(x * (ew_x * du + dxd), et_m)
        da = (_reduce8(dz * z + dyb.astype(F32) * yd_s[...], et_m)
              - _reduce8(xd.astype(F32) * dxd + du * u, et_m))
        dwa_row = _reduce8(jnp.broadcast_to(jnp.sum(du * u, axis=0, keepdims=True), (8, GRP_W)), et_m)[0:1]
        t_nh = None
        for p in _split3(dsp * s):
            t = lax.dot_general(p, et_m, TN, preferred_element_type=F32)
            t_nh = t if t_nh is None else t_nh + t
        d_last = dwa_row + jnp.exp(a_last_row) * jnp.sum(t_nh, axis=0, keepdims=True)
        row_l = lax.broadcasted_iota(jnp.int32, (L, 1), 0)
        da_ref[0] = da + jnp.where(row_l == L - 1, d_last, 0.0)
        ddt_ref[0] = ddt
        dd_ref[g] += jnp.sum(dy * x, axis=0, keepdims=True)
        through_conv(dc + _dot(dcb, bm), pre_c, sg_c, sh_c, dc_ref, dwc_ref, dbc_ref)
        through_conv(db + _dot(dcb, cm, TN), pre_b, sg_b, sh_b, db_ref, dwb_ref, dbb_ref)
        ds_s[g] = _state_decay(arowT[:, L - 1:L], et_m) * dsp + ds_y

    rc = lambda c: nc - 1 - c
    in_specs = ([pl.BlockSpec((L, GRP_W), lambda c, g: (rc(c), g))] + _ssd_specs(nc, True)
                + [pl.BlockSpec((1, 1, GRP_W, N), lambda c, g: (rc(c), g, 0, 0))])
    whole = lambda *shape: pl.BlockSpec(shape, lambda c, g: (0,) * len(shape))
    out_specs = [
        pl.BlockSpec((L, GRP_W), lambda c, g: (rc(c), g)),
        pl.BlockSpec((L, N), lambda c, g: (rc(c), g)),
        pl.BlockSpec((L, N), lambda c, g: (rc(c), g)),
        pl.BlockSpec((1, L, HPG), lambda c, g: (g, rc(c), 0)),
        pl.BlockSpec((1, L, HPG), lambda c, g: (g, rc(c), 0)),
        whole(SSM_GROUPS, 1, GRP_W),
        whole(SSM_GROUPS, SSM_CONV, GRP_W), whole(SSM_GROUPS, SSM_CONV, N), whole(SSM_GROUPS, SSM_CONV, N),
        whole(SSM_GROUPS, 1, GRP_W), whole(SSM_GROUPS, 1, N), whole(SSM_GROUPS, 1, N),
    ]
    gn = SSM_GROUPS * N
    acc = lambda *shape: jax.ShapeDtypeStruct(shape, F32)
    out_shape = [
        acc(t_dim, SSM_D_INNER), acc(t_dim, gn), acc(t_dim, gn), acc(SSM_GROUPS, t_dim, HPG),
        acc(SSM_GROUPS, t_dim, HPG), acc(SSM_GROUPS, 1, GRP_W),
        acc(SSM_GROUPS, SSM_CONV, GRP_W), acc(SSM_GROUPS, SSM_CONV, N), acc(SSM_GROUPS, SSM_CONV, N),
        acc(SSM_GROUPS, 1, GRP_W), acc(SSM_GROUPS, 1, N), acc(SSM_GROUPS, 1, N),
    ]
    return pcall(
        body, name=nm, grid=(nc, SSM_GROUPS), in_specs=in_specs, out_specs=out_specs, out_shape=out_shape,
        scratch=[pltpu.VMEM((SSM_GROUPS, GRP_W, N), F32), pltpu.VMEM((L, GRP_W), F32), pltpu.VMEM((L, GRP_W), F32)],
        args=[dy] + [xbc_raw, xbc_raw, conv_w, conv_b] * 3 + [dt_g, dtT_g, a_row, a_col, dvec, states], hook=hook)


def _softplus(x):
    return jnp.maximum(x, 0.0) + jnp.log(1.0 + jnp.exp(-jnp.abs(x)))


def ssd_dt_bwd(da, ddt, dt, dt_raw, a_row, dt_bias, nm):
    L = SSM_CHUNK

    def fn(d_a, d_dt, dtv, raw, ar, bias):
        r = lax.broadcasted_iota(jnp.int32, (L, L), 0)
        c = lax.broadcasted_iota(jnp.int32, (L, L), 1)
        triu = (r <= c).astype(BF16)
        acc = None
        for p in _split3(d_a):
            t = jnp.dot(triu, p, preferred_element_type=F32)
            acc = t if acc is None else acc + t
        d_dt = d_dt + acc * ar
        d_a_h = jnp.sum(acc * dtv, axis=0, keepdims=True)
        d_raw = d_dt * _sigmoid(raw + bias)
        return d_raw, d_a_h, jnp.sum(d_raw, axis=0, keepdims=True)

    return rowmap(fn, [da, ddt, dt, dt_raw], [a_row, dt_bias], [(SSM_HEADS, BF16)],
                  [(1, SSM_HEADS), (1, SSM_HEADS)], tm=L, name=nm)


GN_W = SSM_D_INNER // SSM_GROUPS


def mamba_fwd(h, p, nm, io):
    def in_proj(x, gg, w_zt, w_xbct, w_dtt):
        uu = _rms_fwd(x, gg).astype(BF16)
        return uu, _dot(uu, w_zt, NT), _dot(uu, w_xbct, NT), _dot(uu, w_dtt, NT)

    u, z, xbc_raw, dt_raw = rowmap(in_proj, [h], [p["ssm_norm"], p["w_zt"], p["w_xbct"], p["w_dtt"]],
                                   [(D_MODEL, BF16), (SSM_D_INNER, F32), (SSM_CONV_DIM, F32), (SSM_HEADS, F32)],
                                   tm=RESIDENT_TM, name=nm + "_in", hook=io.hook(nm + "_in"))
    dt, = rowmap(lambda r, b: (_softplus(r + b),), [dt_raw], [p["dt_bias"]], [(SSM_HEADS, F32)], tm=256,
                 name=nm + "_softplus")
    dt_g = dt.reshape(-1, SSM_GROUPS, HPG).transpose(1, 0, 2)
    dtT_g = dt_g.transpose(0, 2, 1)
    y, states = ssd_fwd(xbc_raw, p["conv_w"], p["conv_b"], dt_g, dtT_g, p["a_row"], p["a_col"], p["dvec"],
                        nm + "_ssd", hook=io.hook(nm + "_ssd"))

    def gate_norm_out(yv, zv, hv, gg, w_out):
        t = yv * _silu(zv)
        yn = jnp.concatenate([_rms_fwd(t[:, k * GN_W:(k + 1) * GN_W], gg[:, k * GN_W:(k + 1) * GN_W])
                              for k in range(SSM_GROUPS)], axis=1).astype(BF16)
        return yn, hv + _dot(yn, w_out)

    yn, out = rowmap(gate_norm_out, [y, z, h], [p["gate_norm"], p["w_out"]],
                     [(SSM_D_INNER, BF16), (D_MODEL, F32)], tm=RESIDENT_TM, name=nm + "_out")
    return out, (u, z, xbc_raw, dt_raw, dt, dt_g, dtT_g, y, states, yn)


def mamba_bwd(h, p, saved, dout, dout_bf, nm, io):
    u, z, xbc_raw, dt_raw, dt, dt_g, dtT_g, y, states, yn = saved
    g = {}
    io.put("w_out", mm(yn, dout_bf, ta=True, out_dtype=BF16, name=nm + "_dwout"))

    def gate_norm_bwd(d_o, yv, zv, gg, w_out):
        d = _dot(d_o, w_out, NT)
        sz = _silu(zv)
        t = yv * sz
        dts, dgs = [], []
        for k in range(SSM_GROUPS):
            sl = slice(k * GN_W, (k + 1) * GN_W)
            dt_k, dg_k = _rms_bwd(t[:, sl], gg[:, sl], d[:, sl])
            dts.append(dt_k)
            dgs.append(dg_k)
        d_t = jnp.concatenate(dts, axis=1)
        return d_t * sz, d_t * yv * _silu_grad(zv), jnp.concatenate(dgs, axis=1)

    dy, dz, g["gate_norm"] = rowmap(gate_norm_bwd, [dout_bf, y, z], [p["gate_norm"], p["w_out"]],
                                    [(SSM_D_INNER, F32), (SSM_D_INNER, BF16)], [(1, SSM_D_INNER)], tm=256,
                                    name=nm + "_dgatenorm")
    d_x, d_b, d_c, da_g, ddt_g, dd, dwx, dwb, dwc, dbx, dbb, dbc = ssd_bwd(
        dy, xbc_raw, p["conv_w"], p["conv_b"], dt_g, dtT_g, p["a_row"], p["a_col"], p["dvec"], states, nm + "_dssd",
        hook=io.hook(nm + "_dssd"))
    g["dvec"] = dd
    by_lane = lambda t: t.transpose(1, 0, 2).reshape(t.shape[1], -1)
    g["conv_w"] = jnp.concatenate([by_lane(dwx), by_lane(dwb), by_lane(dwc)], axis=1)
    g["conv_b"] = jnp.concatenate([by_lane(dbx), by_lane(dbb), by_lane(dbc)], axis=1)
    per_head = lambda t: t.transpose(1, 0, 2).reshape(-1, SSM_HEADS)
    ddt_raw, g["a"], g["dt_bias"] = ssd_dt_bwd(per_head(da_g), per_head(ddt_g), dt, dt_raw, p["a_heads"],
                                               p["dt_bias"], nm + "_ddt")
    dxbc_raw = conv_input_grad([d_x, d_b, d_c], p["conv_w"], nm + "_dconv")
    io.put("w_int", jnp.concatenate([mm(dz, u, ta=True, out_dtype=BF16, name=nm + "_dwz"),
                                     mm(dxbc_raw, u, ta=True, out_dtype=BF16, name=nm + "_dwxbc"),
                                     mm(ddt_raw, u, ta=True, out_dtype=BF16, name=nm + "_dwdt")], axis=0))
    dh, dh_bf, g["ssm_norm"] = du_norm_bwd(
        [(dz, p["w_zt"], False), (dxbc_raw, p["w_xbct"], False), (ddt_raw, p["w_dtt"], False)],
        h, p["ssm_norm"], dout, nm + "_du", hook=io.hook(nm + "_du"))
    return dh, dh_bf, g


KV_W = ATT_KV_HEADS * ATT_HEAD_DIM


def kv_fwd(h, p, nm):
    def kv_proj(x, gg, w_kv, gk):
        uu = _rms_fwd(x, gg).astype(BF16)
        t = _dot(uu, w_kv)
        ks = [_rms_fwd(t[:, j * ATT_HEAD_DIM:(j + 1) * ATT_HEAD_DIM], gk) for j in range(ATT_KV_HEADS)]
        return uu, t, jnp.concatenate(ks, axis=1), t[:, KV_W:]

    u, kv_raw, k, v = rowmap(kv_proj, [h], [p["kv_norm"], p["w_kv"], p["k_norm"]],
                             [(D_MODEL, BF16), (2 * KV_W, F32), (KV_W, F32), (KV_W, F32)], tm=RESIDENT_TM,
                             name=nm + "_proj")
    return k, v, (u, kv_raw)


def kv_bwd(h, p, saved, dk_cur, dk_prev, dv_cur, dv_prev, dout, nm, io):
    u, kv_raw = saved
    t_dim = h.shape[0]
    tm = ATT_WINDOW
    nb = t_dim // tm
    nxt = pl.BlockSpec((tm, KV_W), lambda i: (jnp.minimum(i + 1, nb - 1), 0))

    def fn(dkc, dkp, dvc, dvp, t, gg):
        live = pl.program_id(0) < nb - 1
        dk = dkc + jnp.where(live, dkp, 0.0)
        dv = dvc + jnp.where(live, dvp, 0.0)
        outs, dgs = [], None
        for j in range(ATT_KV_HEADS):
            sl = slice(j * ATT_HEAD_DIM, (j + 1) * ATT_HEAD_DIM)
            dx, dg = _rms_bwd(t[:, sl], gg, dk[:, sl])
            outs.append(dx)
            dgs = dg if dgs is None else dgs + dg
        return jnp.concatenate(outs + [dv], axis=1), dgs

    dkv_raw, dknorm = rowmap(fn, [dk_cur, (dk_prev, nxt), dv_cur, (dv_prev, nxt), kv_raw], [p["k_norm"]],
                             [(2 * KV_W, BF16)], [(1, ATT_HEAD_DIM)], tm=tm, name=nm + "_dknorm",
                             hook=io.hook(nm + "_dknorm"))
    g = {"k_norm": dknorm}
    io.put("w_kv", mm(u, dkv_raw, ta=True, out_dtype=BF16, name=nm + "_dwkv"))
    dh, dh_bf, g["kv_norm"] = du_norm_bwd([(dkv_raw, p["w_kv"], True)], h, p["kv_norm"], dout, nm + "_du",
                                          hook=io.hook(nm + "_du"))
    return dh, dh_bf, g


def _attn_specs(nb):
    blk = ATT_WINDOW
    cur = lambda i: (i, 0)
    prev = lambda i: (jnp.maximum(i - 1, 0), 0)
    return [
        pl.BlockSpec((blk, D_MODEL), cur),
        pl.BlockSpec((blk, KV_W), prev), pl.BlockSpec((blk, KV_W), cur),
        pl.BlockSpec((blk, KV_W), prev), pl.BlockSpec((blk, KV_W), cur),
        pl.BlockSpec((1, ATT_HEAD_DIM), lambda i: (0, 0)),
        pl.BlockSpec((ATT_KV_HEADS, ATT_GROUP * blk, 2 * blk), lambda i: (0, 0, 0)),
        pl.BlockSpec((ATT_KV_HEADS, ATT_GROUP * blk, 1), lambda i: (0, 0, 0)),
    ]


def attn_fwd(q_raw, k, v, q_norm, bias, sink_col, h, w_o, nm):
    t_dim = q_raw.shape[0]
    blk, hd = ATT_WINDOW, ATT_HEAD_DIM
    nb = t_dim // blk

    n_pairs = ATT_GROUP // 2

    def body(q_ref, kp_ref, kc_ref, vp_ref, vc_ref, qn_ref, bias_ref, sink_ref, h_ref, wo_ref, o_ref, out_ref):
        low = lax.broadcasted_iota(jnp.int32, (1, LANES), 1) < hd
        gq = jnp.concatenate([qn_ref[...], qn_ref[...]], axis=1)
        colk = lax.broadcasted_iota(jnp.int32, (1, 2 * blk), 1)
        live = (pl.program_id(0) > 0) | (colk >= blk)
        for kv in range(ATT_KV_HEADS):
            kraw = jnp.concatenate([kp_ref[...], kc_ref[...]], axis=0)
            vraw = jnp.concatenate([vp_ref[...], vc_ref[...]], axis=0)
            k_mine = jnp.where(low, kraw, 0.0) if kv == 0 else jnp.where(low, 0.0, kraw)
            v_mine = jnp.where(low, vraw, 0.0) if kv == 0 else jnp.where(low, 0.0, vraw)
            k_other = pltpu.roll(k_mine, hd, 1)
            v_other = pltpu.roll(v_mine, hd, 1)
            k_lo, k_hi = (k_mine, k_other) if kv == 0 else (k_other, k_mine)
            v_lo, v_hi = (v_mine, v_other) if kv == 0 else (v_other, v_mine)
            x = jnp.concatenate([q_ref[:, (kv * n_pairs + p) * LANES:(kv * n_pairs + p + 1) * LANES]
                                 for p in range(n_pairs)], axis=0)
            sq = x * x
            ms_lo = jnp.sum(jnp.where(low, sq, 0.0), axis=-1, keepdims=True) * (1.0 / hd)
            ms_hi = jnp.sum(jnp.where(low, 0.0, sq), axis=-1, keepdims=True) * (1.0 / hd)
            q = x * jnp.where(low, lax.rsqrt(ms_lo + EPS), lax.rsqrt(ms_hi + EPS)) * gq
            o_pair = None
            for par, (k_p, v_p) in enumerate(((k_lo, v_lo), (k_hi, v_hi))):
                rows = [slice((2 * p + par) * blk, (2 * p + par + 1) * blk) for p in range(n_pairs)]
                bias = jnp.concatenate([bias_ref[kv, r, :] for r in rows], axis=0)
                sink = jnp.concatenate([sink_ref[kv, r, :] for r in rows], axis=0)
                s = jnp.where(live, _dot(q, k_p, NT) * (hd ** -0.5) + bias, NEG)
                m = jnp.maximum(jnp.max(s, axis=-1, keepdims=True), sink)
                pexp = jnp.exp(s - m)
                inv_den = 1.0 / (jnp.sum(pexp, axis=-1, keepdims=True) + jnp.exp(sink - m))
                o_p = _dot(pexp, v_p) * inv_den
                o_pair = o_p if o_pair is None else o_pair + o_p
            for p in range(n_pairs):
                o_ref[:, (kv * n_pairs + p) * LANES:(kv * n_pairs + p + 1) * LANES] = (
                    o_pair[p * blk:(p + 1) * blk].astype(o_ref.dtype))
        out_ref[...] = h_ref[...] + jnp.dot(o_ref[...], wo_ref[...], preferred_element_type=F32)

    row_spec = pl.BlockSpec((blk, D_MODEL), lambda i: (i, 0))
    return pcall(body, name=nm, grid=(nb,),
                 in_specs=_attn_specs(nb) + [row_spec, pl.BlockSpec((D_MODEL, D_MODEL), lambda i: (0, 0))],
                 out_specs=[row_spec, row_spec],
                 out_shape=[jax.ShapeDtypeStruct((t_dim, D_MODEL), BF16), jax.ShapeDtypeStruct((t_dim, D_MODEL), F32)],
                 args=[q_raw, k, k, v, v, q_norm, bias, sink_col, h, w_o])


def attn_bwd(do, q_raw, k, v, q_norm, bias, sink_col, nm, hook=None):
    t_dim = q_raw.shape[0]
    blk, hd = ATT_WINDOW, ATT_HEAD_DIM
    nb = t_dim // blk
    scale = hd ** -0.5

    def body(do_ref, q_ref, kp_ref, kc_ref, vp_ref, vc_ref, qn_ref, bias_ref, sink_ref,
             dq_ref, dkc_ref, dkp_ref, dvc_ref, dvp_ref, dbias_ref, dsink_ref, dqn_ref):
        @pl.when(pl.program_id(0) == 0)
        def _():
            dbias_ref[...] = jnp.zeros_like(dbias_ref)
            dsink_ref[...] = jnp.zeros_like(dsink_ref)
            dqn_ref[...] = jnp.zeros_like(dqn_ref)

        n_pairs = ATT_GROUP // 2
        low = lax.broadcasted_iota(jnp.int32, (1, LANES), 1) < hd
        gq = jnp.concatenate([qn_ref[...], qn_ref[...]], axis=1)
        colk = lax.broadcasted_iota(jnp.int32, (1, 2 * blk), 1)
        live = (pl.program_id(0) > 0) | (colk >= blk)
        ones = jnp.ones((2 * blk, LANES), BF16)
        for kv in range(ATT_KV_HEADS):
            kraw = jnp.concatenate([kp_ref[...], kc_ref[...]], axis=0)
            vraw = jnp.concatenate([vp_ref[...], vc_ref[...]], axis=0)
            k_mine = jnp.where(low, kraw, 0.0) if kv == 0 else jnp.where(low, 0.0, kraw)
            v_mine = jnp.where(low, vraw, 0.0) if kv == 0 else jnp.where(low, 0.0, vraw)
            k_other = pltpu.roll(k_mine, hd, 1)
            v_other = pltpu.roll(v_mine, hd, 1)
            k_lo, k_hi = (k_mine, k_other) if kv == 0 else (k_other, k_mine)
            v_lo, v_hi = (v_mine, v_other) if kv == 0 else (v_other, v_mine)
            tiles = [slice((kv * n_pairs + p) * LANES, (kv * n_pairs + p + 1) * LANES) for p in range(n_pairs)]
            x = jnp.concatenate([q_ref[:, t] for t in tiles], axis=0)
            do_pair = jnp.concatenate([do_ref[:, t] for t in tiles], axis=0)

            def head_mean(t):
                lo = jnp.sum(jnp.where(low, t, 0.0), axis=-1, keepdims=True)
                hi = jnp.sum(jnp.where(low, 0.0, t), axis=-1, keepdims=True)
                return jnp.where(low, lo, hi) * (1.0 / hd)

            rinv = lax.rsqrt(head_mean(x * x) + EPS)
            xh = x * rinv
            q = xh * gq
            dq_pair = None
            dk_pair = None
            dv_pair = None
            for par, (k_p, v_p) in enumerate(((k_lo, v_lo), (k_hi, v_hi))):
                rows = [slice((2 * p + par) * blk, (2 * p + par + 1) * blk) for p in range(n_pairs)]
                bias = jnp.concatenate([bias_ref[kv, r, :] for r in rows], axis=0)
                sink = jnp.concatenate([sink_ref[kv, r, :] for r in rows], axis=0)
                s = jnp.where(live, _dot(q, k_p, NT) * scale + bias, NEG)
                m = jnp.maximum(jnp.max(s, axis=-1, keepdims=True), sink)
                pexp = jnp.exp(s - m)
                e_sink = jnp.exp(sink - m)
                inv_den = 1.0 / (jnp.dot(pexp.astype(BF16), ones, preferred_element_type=F32) + e_sink)
                prob = pexp * jnp.concatenate([inv_den, inv_den], axis=1)
                dp = _dot(do_pair, v_p, NT)
                delta = jnp.sum(prob * dp, axis=-1, keepdims=True)
                ds = prob * (dp - delta)
                dsk = -(e_sink * inv_den[:, :1]) * delta
                for p, r in enumerate(rows):
                    dsink_ref[kv, r, :] += dsk[p * blk:(p + 1) * blk]
                    dbias_ref[kv, r, :] += ds[p * blk:(p + 1) * blk]
                ds_s = ds * scale
                mine = low if par == 0 else jnp.logical_not(low)
                dq_p = _dot(ds_s, k_p)
                dk_p = jnp.where(mine, _dot(ds_s, q, TN), 0.0)
                dv_p = jnp.where(mine, _dot(prob, do_pair, TN), 0.0)
                dq_pair = dq_p if dq_pair is None else dq_pair + dq_p
                dk_pair = dk_p if dk_pair is None else dk_pair + dk_p
                dv_pair = dv_p if dv_pair is None else dv_pair + dv_p
            dqn_ref[...] += jnp.sum(dq_pair * xh, axis=0, keepdims=True)
            dxh = dq_pair * gq
            dq_raw = rinv * (dxh - xh * head_mean(dxh * xh))
            for p, t in enumerate(tiles):
                dq_ref[:, t] = dq_raw[p * blk:(p + 1) * blk].astype(dq_ref.dtype)
            dkk = dk_pair + pltpu.roll(dk_pair, hd, 1)
            dvv = dv_pair + pltpu.roll(dv_pair, hd, 1)
            sl = slice(kv * hd, (kv + 1) * hd)
            dkp_ref[:, sl] = dkk[:blk, sl]
            dkc_ref[:, sl] = dkk[blk:, sl]
            dvp_ref[:, sl] = dvv[:blk, sl]
            dvc_ref[:, sl] = dvv[blk:, sl]

    cur = lambda i: (i, 0)
    row_spec = pl.BlockSpec((blk, KV_W), cur)
    out_specs = [
        pl.BlockSpec((blk, D_MODEL), cur), row_spec, row_spec, row_spec, row_spec,
        pl.BlockSpec((ATT_KV_HEADS, ATT_GROUP * blk, 2 * blk), lambda i: (0, 0, 0)),
        pl.BlockSpec((ATT_KV_HEADS, ATT_GROUP * blk, 1), lambda i: (0, 0, 0)),
        pl.BlockSpec((1, LANES), lambda i: (0, 0)),
    ]
    kvs = jax.ShapeDtypeStruct((t_dim, KV_W), F32)
    out_shape = [
        jax.ShapeDtypeStruct((t_dim, D_MODEL), BF16), kvs, kvs, kvs, kvs,
        jax.ShapeDtypeStruct((ATT_KV_HEADS, ATT_GROUP * blk, 2 * blk), F32),
        jax.ShapeDtypeStruct((ATT_KV_HEADS, ATT_GROUP * blk, 1), F32),
        jax.ShapeDtypeStruct((1, LANES), F32),
    ]
    *outs, dqn_pair = pcall(body, name=nm, grid=(nb,), in_specs=[pl.BlockSpec((blk, D_MODEL), cur)] + _attn_specs(nb),
                            out_specs=out_specs, out_shape=out_shape,
                            args=[do, q_raw, k, k, v, v, q_norm, bias, sink_col], hook=hook)
    return (*outs, dqn_pair[:, :hd] + dqn_pair[:, hd:])


def _t5_bucket_np():
    blk = ATT_WINDOW
    qi = np.arange(blk)[:, None] + blk
    kj = np.arange(2 * blk)[None, :]
    dist = qi - kj
    n = np.maximum(dist, 0)
    max_exact = REL_BUCKETS // 2
    nf = np.maximum(n, 1).astype(np.float32)
    large = max_exact + (np.log(nf / max_exact) / math.log(ATT_WINDOW / max_exact)
                         * (REL_BUCKETS - max_exact)).astype(np.int32)
    large = np.minimum(large, REL_BUCKETS - 1)
    bucket = np.where(n < max_exact, n, large)
    in_window = (dist >= 0) & (dist < ATT_WINDOW)
    return bucket, in_window


def attn_block_fwd(h, k, v, p, nm):
    def q_proj(x, gg, w_q):
        uu = _rms_fwd(x, gg).astype(BF16)
        return uu, _dot(uu, w_q)

    u, q_raw = rowmap(q_proj, [h], [p["attn_norm"], p["w_q"]], [(D_MODEL, BF16), (D_MODEL, F32)], tm=RESIDENT_TM,
                      name=nm + "_q")
    o, out = attn_fwd(q_raw, k, v, p["q_norm"], p["bias"], p["sink_col"], h, p["w_o"], nm + "_core")
    return out, (u, q_raw, o)


def attn_block_bwd(h, k, v, p, saved, dout, dout_bf, nm, io):
    u, q_raw, o = saved
    g = {}
    io.put("w_o", mm(o, dout_bf, ta=True, out_dtype=BF16, name=nm + "_dwo", hook=io.hook(nm + "_dwo")))
    do = mm(dout_bf, p["w_o"], tb=True, name=nm + "_do")
    dq_raw, dkc, dkp, dvc, dvp, g["bias"], g["sink_col"], g["q_norm"] = attn_bwd(
        do, q_raw, k, v, p["q_norm"], p["bias"], p["sink_col"], nm + "_dcore", hook=io.hook(nm + "_dcore"))
    io.put("w_q", mm(u, dq_raw, ta=True, out_dtype=BF16, name=nm + "_dwq"))
    dh, dh_bf, g["attn_norm"] = du_norm_bwd([(dq_raw, p["w_q"], True)], h, p["attn_norm"], dout, nm + "_du")
    return dh, dh_bf, g, (dkc, dkp, dvc, dvp)


FFN_TAGS = ["00", "01", "10", "11"]


def local_step(x, target, small, io):
    bucket, in_window = _t5_bucket_np()
    blk = ATT_WINDOW
    w = small

    fnorm = {tag: w["ffn_norm"][int(tag[0]), int(tag[1])][None, :] for tag in FFN_TAGS}
    a_neg = -jnp.exp(w["ssm_a_log"][0])

    def mamba_p():
        w_int = io.w("w_int")
        return dict(ssm_norm=w["ssm_norm"], w_zt=w_int[:SSM_D_INNER],
                    w_xbct=w_int[SSM_D_INNER:SSM_D_INNER + SSM_CONV_DIM], w_dtt=w_int[SSM_D_INNER + SSM_CONV_DIM:],
                    conv_w=w["ssm_conv_w"][0], conv_b=w["ssm_conv_b"], dt_bias=w["ssm_dt_bias"],
                    a_heads=a_neg[None, :], a_row=a_neg.reshape(SSM_GROUPS, 1, HPG),
                    a_col=a_neg.reshape(SSM_GROUPS, HPG, 1),
                    dvec=jnp.repeat(w["ssm_d"][0], SSM_HEAD_DIM)[None, :],
                    gate_norm=w["ssm_gate_norm"], w_out=io.w("w_out"))

    rb = w["rel_bias"]
    onehot3 = (np.arange(REL_BUCKETS)[:, None, None] == bucket[None]).astype(np.float32)
    bias = jnp.einsum("bh,bqk->hqk", rb, onehot3, precision=lax.Precision.HIGHEST)
    bias = jnp.where(in_window[None], bias, NEG)
    bias = bias.reshape(ATT_KV_HEADS, ATT_GROUP * blk, 2 * blk)
    sink_col = jnp.repeat(w["sinks"][0], blk).reshape(ATT_KV_HEADS, ATT_GROUP * blk, 1)

    def attn_p():
        return dict(attn_norm=w["attn_norm"], w_q=io.w("w_q"), q_norm=w["q_norm"], bias=bias, sink_col=sink_col,
                    w_o=io.w("w_o"))

    def kv_p():
        return dict(kv_norm=w["kv_norm"][None, :], w_kv=io.w("w_kv"), k_norm=w["k_norm"][None, :])

    h0 = x
    h0a, s_f00 = ffn_fwd(h0, fnorm["00"], "00", io)
    mp = mamba_p()
    h0b, s_m = mamba_fwd(h0a, mp, "ssm", io)
    h1, s_f01 = ffn_fwd(h0b, fnorm["01"], "01", io)
    kp = kv_p()
    k, v, s_kv = kv_fwd(h1, kp, "kv")
    h1a, s_f10 = ffn_fwd(h1, fnorm["10"], "10", io)
    ap = attn_p()
    h1b, s_a = attn_block_fwd(h1a, k, v, ap, "att")
    (dh, dh_bf, sq), s_f11 = ffn_fwd(h1b, fnorm["11"], "11", io, target=target)
    loss_part = jnp.sum(sq) * (0.5 / D_MODEL)

    fg = {}

    def ffn_back(tag, h_in, saved, dh, dh_bf):
        dh, dh_bf, dg = ffn_bwd(h_in, fnorm[tag], tag, saved, dh, dh_bf, io)
        fg[tag] = dg[0]
        return dh, dh_bf

    dh, dh_bf = ffn_back("11", h1b, s_f11, dh, dh_bf)
    dh, dh_bf, ga, dkv = attn_block_bwd(h1a, k, v, ap, s_a, dh, dh_bf, "att", io)
    dh, dh_bf = ffn_back("10", h1, s_f10, dh, dh_bf)
    dh, dh_bf, gk = kv_bwd(h1, kp, s_kv, *dkv, dh, "kv", io)
    dh, dh_bf = ffn_back("01", h0b, s_f01, dh, dh_bf)
    dh, dh_bf, gm = mamba_bwd(h0a, mp, s_m, dh, dh_bf, "ssm", io)
    dh, dh_bf = ffn_back("00", h0, s_f00, dh, dh_bf)
    grad_x = dh

    grads = {}
    grads["ffn_norm"] = jnp.stack([fg[tag] for tag in FFN_TAGS]).reshape(2, 2, D_MODEL)
    grads["ssm_norm"] = gm["ssm_norm"]
    grads["ssm_conv_w"] = gm["conv_w"][None]
    grads["ssm_conv_b"] = gm["conv_b"]
    grads["ssm_dt_bias"] = gm["dt_bias"]
    grads["ssm_a_log"] = gm["a"] * a_neg[None, :]
    grads["ssm_d"] = jnp.sum(gm["dvec"].reshape(SSM_HEADS, SSM_HEAD_DIM), axis=1)[None, :]
    grads["ssm_gate_norm"] = gm["gate_norm"]
    grads["kv_norm"] = gk["kv_norm"][0]
    grads["k_norm"] = gk["k_norm"][0]
    grads["attn_norm"] = ga["attn_norm"]
    grads["q_norm"] = ga["q_norm"]
    grads["sinks"] = jnp.sum(ga["sink_col"].reshape(ATT_HEADS, blk), axis=1)[None, :]
    onehot = (np.arange(REL_BUCKETS)[:, None] == bucket.reshape(1, -1)).astype(np.float32)
    dbias2d = ga["bias"].reshape(ATT_HEADS, blk * 2 * blk)
    grads["rel_bias"] = mm(jnp.asarray(onehot, BF16), dbias2d, tb=True, name="drelbias")
    return loss_part, grad_x, grads


def _adamw(g, w, m, v):
    m = ADAM_B1 * m + (1.0 - ADAM_B1) * g
    v = ADAM_B2 * v + (1.0 - ADAM_B2) * (g * g)
    m_hat = m / (1.0 - ADAM_B1 ** ADAM_STEP)
    v_hat = v / (1.0 - ADAM_B2 ** ADAM_STEP)
    delta = -ADAM_LR * (m_hat / (jnp.sqrt(v_hat) + ADAM_EPS) + ADAM_WD * w)
    return delta, m, v


def _slot_sum(r):
    g = r[0].astype(F32)
    for d in range(1, r.shape[0]):
        g = g + r[d].astype(F32)
    return g


def adamw_rows(recvs, w, m, v, name):
    n_l, rows, width = w.shape
    n_slots = recvs[0].shape[0]
    tr = 32
    assert rows % tr == 0, rows
    nt = rows // tr

    def body(*refs):
        r_refs = refs[:n_l]
        w_ref, m_ref, v_ref, g_o, d_o, m_o, v_o = refs[n_l:]
        li = pl.program_id(0)
        for k in range(n_l):
            @pl.when(li == k)
            def _(k=k):
                g = _slot_sum(r_refs[k])
                delta, m2, v2 = _adamw(g, w_ref[0], m_ref[0], v_ref[0])
                g_o[0] = g
                d_o[0] = delta
                m_o[0] = m2
                v_o[0] = v2

    def r_spec(k):
        return pl.BlockSpec((n_slots, tr, width),
                            lambda li, j: (0, jnp.where(li == k, j, jnp.where(li > k, nt - 1, 0)), 0))

    w_spec = pl.BlockSpec((1, tr, width), lambda li, j: (li, j, 0))
    shp = jax.ShapeDtypeStruct(w.shape, F32)
    return pcall(body, name=name, grid=(n_l, nt), in_specs=[r_spec(k) for k in range(n_l)] + [w_spec] * 3,
                 out_specs=[w_spec] * 4, out_shape=[shp] * 4, args=list(recvs) + [w, m, v])


def adamw_cols(recvs, w, m, v, name):
    n_l, rows, n = w.shape
    n_slots = recvs[0].shape[0]
    tr = 256
    nt = rows // tr

    def body(*refs):
        r_refs = refs[:n_l]
        w_ref, m_ref, v_ref, g_o, d_o, m_o, v_o = refs[n_l:]
        li = pl.program_id(0)
        for k in range(n_l):
            @pl.when(li == k)
            def _(k=k):
                g = _slot_sum(r_refs[k]).T
                delta, m2, v2 = _adamw(g, w_ref[0], m_ref[0], v_ref[0])
                g_o[0] = g
                d_o[0] = delta
                m_o[0] = m2
                v_o[0] = v2

    def r_spec(k):
        return pl.BlockSpec((n_slots, n, tr),
                            lambda li, j: (0, 0, jnp.where(li == k, j, jnp.where(li > k, nt - 1, 0))))

    w_spec = pl.BlockSpec((1, tr, n), lambda li, j: (li, j, 0))
    shp = jax.ShapeDtypeStruct(w.shape, F32)
    return pcall(body, name=name, grid=(n_l, nt), in_specs=[r_spec(k) for k in range(n_l)] + [w_spec] * 3,
                 out_specs=[w_spec] * 4, out_shape=[shp] * 4, args=list(recvs) + [w, m, v])


WEIGHT_NAMES = ["ffn_norm", "ffn_w1", "ffn_w3", "ffn_w2", "ssm_norm", "ssm_w_in", "ssm_conv_w", "ssm_conv_b",
                "ssm_dt_bias", "ssm_a_log", "ssm_d", "ssm_gate_norm", "ssm_w_out", "kv_norm", "w_kv", "k_norm",
                "attn_norm", "w_q", "q_norm", "sinks", "w_o", "rel_bias"]

SMALL = [
    ("ffn_norm", (2, 2, 1024), 2), ("ssm_norm", (1, 1024), 1), ("ssm_conv_w", (1, 4, 3072), 2),
    ("ssm_conv_b", (1, 3072), 1), ("ssm_gate_norm", (1, 2048), 1),
    ("ssm_dt_bias", (1, 32), None), ("ssm_a_log", (1, 32), None), ("ssm_d", (1, 32), None),
    ("kv_norm", (1024,), None), ("k_norm", (64,), None), ("attn_norm", (1, 1024), None),
    ("q_norm", (1, 64), None), ("sinks", (1, 16), None), ("rel_bias", (32, 16), None),
]
SMALL_W = 1024
SMALL_FULL_ROWS = 32
SMALL_LOCAL_ROWS = 48

MAT_GROUPS = {
    "f00_up": ["w1t_00", "w3t_00"], "f00_down": ["w2_00"], "f01": ["w1t_01", "w3t_01", "w2_01"],
    "f10": ["w1t_10", "w3t_10", "w2_10"], "f11": ["w1t_11", "w3t_11", "w2_11"],
    "ssm": ["w_int", "w_out"], "att": ["w_q", "w_o", "w_kv"],
    "f00_early": ["w2_00", "w1t_00"], "f00_late": ["w3t_00"],
}
FIRST_GATHER = "f00_up"
GATHER_PLAN = {"f00_upgate": ["f00_down", "ssm"], "ssm_in": ["f01"], "ssm_ssd": ["att", "f10"],
               "f01_upgate": ["f11"]}
SCATTER_A_PLAN = {"att_dwo": "f11", "kv_dknorm": "f10", "kv_du": "att", "f01_du": "f01", "ssm_du": "ssm",
                  "f00_dw3": "f00_early", "f00_du": "f00_late"}
SCATTER_B_PLAN = {"att_dcore": "f11", "f01_dw2": "att", "f01_dgate": "f10", "ssm_dssd": "f01", "f00_dgate": "ssm",
                  "f00_du": "f00_early"}
LAST_SCATTER = "f00_late"
SLOT_MAJOR = ("w_int",)


def _shard_shape(s, a):
    return s[:a] + (s[a] // N_DEV,) + s[a + 1:]


def _unshard_view(stack, shard_shape, axis):
    moved = jnp.moveaxis(stack, 0, axis)
    return moved.reshape(shard_shape[:axis] + (N_DEV * shard_shape[axis],) + shard_shape[axis + 1:])


def _small_local(arrs):
    flat = jnp.concatenate([arrs[n].reshape(-1) for n, _, _ in SMALL])
    return jnp.pad(flat, (0, SMALL_LOCAL_ROWS * LANES - flat.shape[0])).reshape(SMALL_LOCAL_ROWS, LANES)


def chip_partial(g4, ra, name):
    _, _, n, width = g4.shape

    def body(core_ref, g_ref, r_ref, o_ref):
        o_ref[0] = (g_ref[0, 0].astype(F32) + r_ref[0, 0].astype(F32)).astype(o_ref.dtype)

    grid_spec = pltpu.PrefetchScalarGridSpec(
        num_scalar_prefetch=1, grid=(N_CHIPS,),
        in_specs=[pl.BlockSpec((1, 1, n, width), lambda q, core: (q, core[0], 0, 0)),
                  pl.BlockSpec((1, 1, n, width), lambda q, core: (q, 0, 0, 0))],
        out_specs=pl.BlockSpec((1, n, width), lambda q, core: (q, 0, 0)))
    core = jnp.reshape(lax.axis_index("c"), (1,)).astype(jnp.int32)
    return pl.pallas_call(
        body, name=name, grid_spec=grid_spec, out_shape=jax.ShapeDtypeStruct((N_CHIPS, n, width), g4.dtype),
        compiler_params=pltpu.CompilerParams(dimension_semantics=("arbitrary",), vmem_limit_bytes=VMEM_LIMIT_BYTES),
    )(core, g4, ra)


class StepIO:
    def __init__(self, pieces):
        self.pieces = pieces
        self.full = {}
        self.grad = {}
        self.from_sibling = {}
        self.recv = {}

    def w(self, name):
        return self.full[name]

    def put(self, name, g):
        self.grad[name] = g

    def _by_chip_core(self, name):
        g = self.grad[name]
        return g.reshape((N_CHIPS, 2, g.shape[0] // N_DEV) + g.shape[1:])

    def gather_items(self, groups):
        names = [n for grp in groups for n in MAT_GROUPS[grp]]
        items = [("g2", self.pieces[n], None if n in SLOT_MAJOR else 0) for n in names]

        def sink(outs):
            for n, o in zip(names, outs):
                self.full[n] = o.reshape((-1,) + o.shape[2:]) if n in SLOT_MAJOR else o

        return items, sink

    def scatter_a_items(self, group):
        names = MAT_GROUPS[group]
        items = [("sa", self._by_chip_core(n), None) for n in names]

        def sink(outs):
            for n, o in zip(names, outs):
                self.from_sibling[n] = o

        return items, sink

    def scatter_b_items(self, group):
        names = MAT_GROUPS[group]
        items = [("sb", chip_partial(self._by_chip_core(n), self.from_sibling[n], "partial_" + n), None)
                 for n in names]

        def sink(outs):
            for n, o in zip(names, outs):
                self.recv[n] = o

        return items, sink

    def hook(self, site):
        parts = []
        if site in GATHER_PLAN:
            parts.append(self.gather_items(GATHER_PLAN[site]))
        if site in SCATTER_A_PLAN:
            parts.append(self.scatter_a_items(SCATTER_A_PLAN[site]))
        if site in SCATTER_B_PLAN:
            parts.append(self.scatter_b_items(SCATTER_B_PLAN[site]))
        if not parts:
            return None
        return combine_hooks(parts)


def combine_hooks(parts):
    items = [it for its, _ in parts for it in its]

    def sink(outs):
        p = 0
        for its, snk in parts:
            snk(outs[p:p + len(its)])
            p += len(its)

    return Comm(items), sink


def step(x, target, wts, ms, vs):
    me = _my_index()

    pieces = {}
    for li in range(2):
        for hi in range(2):
            tag = "%d%d" % (li, hi)
            pieces["w1t_" + tag] = wts["ffn_w1"][li, hi].T.astype(BF16)
            pieces["w3t_" + tag] = wts["ffn_w3"][li, hi].T.astype(BF16)
            pieces["w2_" + tag] = wts["ffn_w2"][li, hi].astype(BF16)
    pieces["w_int"] = wts["ssm_w_in"][0].T.astype(BF16)
    pieces["w_out"] = wts["ssm_w_out"][0].astype(BF16)
    pieces["w_kv"] = wts["w_kv"].astype(BF16)
    pieces["w_q"] = wts["w_q"][0].astype(BF16)
    pieces["w_o"] = wts["w_o"][0].astype(BF16)
    io = StepIO(pieces)

    small_sharded = [(n, s, a) for n, s, a in SMALL if a is not None]
    loc = jnp.concatenate([wts[n].reshape(-1) for n, _, _ in small_sharded])
    loc_rows = -(-loc.shape[0] // (8 * LANES)) * 8
    loc = jnp.pad(loc, (0, loc_rows * LANES - loc.shape[0])).reshape(loc_rows, LANES)
    got_small = []
    comm, sink = combine_hooks([io.gather_items([FIRST_GATHER]), ([("g", loc, None)], got_small.extend)])
    sink(comm_only(comm, "gather_first"))
    gath_small = got_small[0].reshape(N_DEV, -1)
    small = {}
    off = 0
    for n, s, a in small_sharded:
        shard = _shard_shape(s, a)
        cnt = int(np.prod(shard))
        small[n] = _unshard_view(gath_small[:, off:off + cnt].reshape((N_DEV,) + shard), shard, a)
        off += cnt
    for n, s, a in SMALL:
        if a is None:
            small[n] = wts[n]

    loss_part, grad_x, g_small_local = local_step(x[0], target[0], small, io)
    loss = lax.psum(loss_part, ("x", "y", "c"))

    small_flat = jnp.concatenate([g_small_local[n].reshape(-1) for n, _, _ in SMALL])
    small_buf = jnp.pad(small_flat, (0, SMALL_FULL_ROWS * SMALL_W - small_flat.shape[0]))
    small_buf = small_buf.reshape(SMALL_FULL_ROWS, SMALL_W)
    got_small = []
    comm, sink = combine_hooks([io.scatter_b_items(LAST_SCATTER), ([("g", small_buf, None)], got_small.extend)])
    sink(comm_only(comm, "exchange_last"))
    small_all = got_small[0]

    def sum_body(r_ref, o_ref):
        o_ref[...] = _slot_sum(r_ref)

    vmem = pl.BlockSpec(memory_space=pltpu.VMEM)
    small_sum, = pcall(sum_body, name="sum_small", grid=(), in_specs=[vmem], out_specs=[vmem],
                       out_shape=[jax.ShapeDtypeStruct((SMALL_FULL_ROWS, SMALL_W), F32)], args=[small_all])
    small_sum = small_sum.reshape(-1)
    g_small = {}
    off = 0
    for n, s, a in SMALL:
        cnt = int(np.prod(s))
        gfull = small_sum[off:off + cnt].reshape(s)
        off += cnt
        if a is None:
            g_small[n] = gfull
        else:
            width = s[a] // N_DEV
            g_small[n] = lax.dynamic_slice_in_dim(gfull, me * width, width, axis=a)

    out = {}

    def emit(name, res, shape):
        for kind, arr in zip(("grad", "delta", "new_m", "new_v"), res):
            out[kind + "_" + name] = arr.reshape(shape)

    for name, key in (("ffn_w1", "w1t_"), ("ffn_w3", "w3t_")):
        shp = wts[name].shape
        view = lambda t: t.reshape((4,) + shp[2:])
        res = adamw_cols([io.recv[key + tag] for tag in FFN_TAGS], view(wts[name]), view(ms[name]), view(vs[name]),
                         "adamw_" + name)
        emit(name, res, shp)
    shp = wts["ffn_w2"].shape
    view = lambda t: t.reshape((4,) + shp[2:])
    res = adamw_rows([io.recv["w2_" + tag] for tag in FFN_TAGS], view(wts["ffn_w2"]), view(ms["ffn_w2"]),
                     view(vs["ffn_w2"]), "adamw_ffn_w2")
    emit("ffn_w2", res, shp)
    res = adamw_cols([io.recv["w_int"]], wts["ssm_w_in"], ms["ssm_w_in"], vs["ssm_w_in"], "adamw_ssm_w_in")
    emit("ssm_w_in", res, wts["ssm_w_in"].shape)
    for name, key in (("ssm_w_out", "w_out"), ("w_kv", "w_kv"), ("w_q", "w_q"), ("w_o", "w_o")):
        shp = wts[name].shape
        view = lambda t: t.reshape((1,) + shp[-2:])
        res = adamw_rows([io.recv[key]], view(wts[name]), view(ms[name]), view(vs[name]), "adamw_" + name)
        emit(name, res, shp)

    res_s = rowmap(lambda gg, ww, mm_, vv: _adamw(gg, ww, mm_, vv),
                   [_small_local(g_small), _small_local(wts), _small_local(ms), _small_local(vs)], [],
                   [(LANES, F32)] * 3, tm=SMALL_LOCAL_ROWS, name="adamw_small")
    flat_s = [r.reshape(-1) for r in res_s]
    off = 0
    for n, s, a in SMALL:
        shard = s if a is None else _shard_shape(s, a)
        cnt = int(np.prod(shard))
        out["grad_" + n] = g_small[n]
        for kind, arr in zip(("delta", "new_m", "new_v"), flat_s):
            out[kind + "_" + n] = arr[off:off + cnt].reshape(shard)
        off += cnt
    out["loss"] = loss
    out["grad_x"] = grad_x[None]
    return out


def kernel(x, ffn_norm, ffn_w1, ffn_w3, ffn_w2, ssm_norm, ssm_w_in, ssm_conv_w, ssm_conv_b, ssm_dt_bias, ssm_a_log, ssm_d, ssm_gate_norm, ssm_w_out, kv_norm, w_kv, k_norm, attn_norm, w_q, q_norm, sinks, w_o, rel_bias, loss_target, m_ffn_norm, m_ffn_w1, m_ffn_w3, m_ffn_w2, m_ssm_norm, m_ssm_w_in, m_ssm_conv_w, m_ssm_conv_b, m_ssm_dt_bias, m_ssm_a_log, m_ssm_d, m_ssm_gate_norm, m_ssm_w_out, m_kv_norm, m_w_kv, m_k_norm, m_attn_norm, m_w_q, m_q_norm, m_sinks, m_w_o, m_rel_bias, v_ffn_norm, v_ffn_w1, v_ffn_w3, v_ffn_w2, v_ssm_norm, v_ssm_w_in, v_ssm_conv_w, v_ssm_conv_b, v_ssm_dt_bias, v_ssm_a_log, v_ssm_d, v_ssm_gate_norm, v_ssm_w_out, v_kv_norm, v_w_kv, v_k_norm, v_attn_norm, v_w_q, v_q_norm, v_sinks, v_w_o, v_rel_bias):
    args = locals()
    wts = {n: args[n] for n in WEIGHT_NAMES}
    ms = {n: args["m_" + n] for n in WEIGHT_NAMES}
    vs = {n: args["v_" + n] for n in WEIGHT_NAMES}
    out = step(x, loss_target, wts, ms, vs)
    result = [out["loss"], out["grad_x"]]
    for kind in ("grad", "delta", "new_m", "new_v"):
        result += [out[kind + "_" + n] for n in WEIGHT_NAMES]
    return tuple(result)
```
